```python
import math
import jax
import jax.numpy as jnp
from jax import lax
import numpy as np

D_MODEL = 2048
BATCH = 8
SEQ = 8192
DEPTH = 2

MLA_HEADS = 8
Q_LORA = 512
KV_LORA = 512
NOPE_DIM = 128
ROPE_DIM = 64
V_DIM = 128
ROPE_THETA = 10000.0
DIL_HEADS = 8
DIL_HEAD_DIM = 128
DIL_BRANCHES = ((128, 1), (512, 4), (2048, 16))
DIL_WIDTH = DIL_HEADS * DIL_HEAD_DIM
BLK = 128
IN_COLS = Q_LORA + KV_LORA + ROPE_DIM + 3 * DIL_WIDTH
MIX_WIDTH = MLA_HEADS * V_DIM + DIL_WIDTH
S5_GROUP = 16
S5_GROUPS = D_MODEL // S5_GROUP
S5_STATE = 64
FFN_HIDDEN = -(-8 * D_MODEL // (3 * 256)) * 256
PLE_DIM = 256
DEEPNORM_ALPHA = (2.0 * DEPTH) ** 0.25
DEEPNORM_BETA = (8.0 * DEPTH) ** -0.25
NEG = -1e30

kernel_name = 'hybrid_mla_dilated_s5_block'


def _layernorm(x, g, b, eps=1e-5):
    xf = x.astype(jnp.float32)
    mu = xf.mean(-1, keepdims=True)
    var = jnp.square(xf - mu).mean(-1, keepdims=True)
    y = (xf - mu) * lax.rsqrt(var + eps) * g.astype(jnp.float32) + b.astype(jnp.float32)
    return y.astype(x.dtype)


def _rmsnorm(x, g, eps=1e-6):
    xf = x.astype(jnp.float32)
    y = xf * lax.rsqrt(jnp.square(xf).mean(-1, keepdims=True) + eps) * g.astype(jnp.float32)
    return y.astype(x.dtype)


def _rope(t, positions):
    half = t.shape[-1] // 2
    inv_freq = ROPE_THETA ** (-jnp.arange(half, dtype=jnp.float32) / half)
    ang = positions.astype(jnp.float32)[..., None] * inv_freq
    cos, sin = jnp.cos(ang)[:, :, None, :], jnp.sin(ang)[:, :, None, :]
    t1, t2 = t[..., :half], t[..., half:]
    return jnp.concatenate([t1 * cos - t2 * sin, t1 * sin + t2 * cos], axis=-1).astype(t.dtype)


def _alibi_slopes(n):
    return jnp.asarray(2.0 ** (-8.0 * np.arange(1, n + 1) / n), dtype=jnp.float32)


def _mla_causal(q, k, v):
    B, S, H, Dqk = q.shape
    nb = S // BLK
    scale = Dqk ** -0.5
    qb = q.reshape(B, nb, BLK, H, Dqk).transpose(1, 0, 2, 3, 4)
    kpos = jnp.arange(S)

    def block(args):
        qblk, bi = args
        s = jnp.einsum('bqhd,bkhd->bhqk', qblk, k).astype(jnp.float32) * scale
        qpos = bi * BLK + jnp.arange(BLK)
        s = jnp.where(kpos[None, :] <= qpos[:, None], s, NEG)
        pr = jax.nn.softmax(s, axis=-1).astype(v.dtype)
        return jnp.einsum('bhqk,bkhd->bqhd', pr, v)

    out = lax.map(block, (qb, jnp.arange(nb)))
    return out.transpose(1, 0, 2, 3, 4).reshape(B, S, H * v.shape[-1])


def _dilated_branch(q, k, v, window, dilation, slopes):
    B, S, H, Dh = q.shape
    d = dilation
    L = S // d
    steps = window // d
    nb = -(-L // BLK)
    Lp = nb * BLK

    def to_classes(t):
        t = t.reshape(B, L, d, H, Dh).transpose(0, 2, 1, 3, 4).reshape(B * d, L, H, Dh)
        return jnp.pad(t, ((0, 0), (0, Lp - L), (0, 0), (0, 0)))

    def windows(t):
        tb = jnp.pad(t, ((0, 0), (BLK, 0), (0, 0), (0, 0))).reshape(B * d, nb + 1, BLK, H, Dh)
        return jnp.concatenate([tb[:, :-1], tb[:, 1:]], axis=2)

    qb = to_classes(q).reshape(B * d, nb, BLK, H, Dh)
    kw = windows(to_classes(k))
    vw = windows(to_classes(v))
    s = jnp.einsum('zbqhd,zbkhd->zbhqk', qb, kw).astype(jnp.float32) * (Dh ** -0.5)
    qi = jnp.arange(BLK)[:, None]
    ki = jnp.arange(2 * BLK)[None, :]
    dist = qi + BLK - ki
    kabs = jnp.arange(nb)[:, None, None] * BLK + ki[None] - BLK
    valid = (dist >= 0)[None] & (dist <= steps)[None] & (kabs >= 0)
    bias = -slopes[:, None, None] * (d * dist).astype(jnp.float32)[None]
    s = jnp.where(valid[None, :, None], s + bias[None, None], NEG)
    m = s.max(-1, keepdims=True)
    e = jnp.exp(s - m)
    l = e.sum(-1, keepdims=True)
    o = jnp.einsum('zbhqk,zbkhd->zbqhd', (e / l).astype(v.dtype), vw)
    lse = (m + jnp.log(l))[..., 0]
    o = o.reshape(B * d, Lp, H, Dh)[:, :L].reshape(B, d, L, H, Dh).transpose(0, 2, 1, 3, 4).reshape(B, S, H, Dh)
    lse = lse.transpose(0, 1, 3, 2).reshape(B * d, Lp, H)[:, :L].reshape(B, d, L, H).transpose(0, 2, 1, 3).reshape(B, S, H)
    return o, lse


def _hybrid_attention(h, positions, w_in, q_norm, w_q_b, kv_norm, w_kv_b, w_out):
    B, S, _ = h.shape
    splits = list(np.cumsum([Q_LORA, KV_LORA, ROPE_DIM, DIL_WIDTH, DIL_WIDTH]))
    q_lat, kv_lat, k_pe, qd, kd, vd = jnp.split(h @ w_in, splits, axis=-1)
    q = (_rmsnorm(q_lat, q_norm) @ w_q_b).reshape(B, S, MLA_HEADS, NOPE_DIM + ROPE_DIM)
    q = jnp.concatenate([q[..., :NOPE_DIM], _rope(q[..., NOPE_DIM:], positions)], axis=-1)
    kv = (_rmsnorm(kv_lat, kv_norm) @ w_kv_b).reshape(B, S, MLA_HEADS, NOPE_DIM + V_DIM)
    k_pe = jnp.broadcast_to(_rope(k_pe[:, :, None, :], positions), (B, S, MLA_HEADS, ROPE_DIM))
    k = jnp.concatenate([kv[..., :NOPE_DIM], k_pe], axis=-1)
    out_a = _mla_causal(q, k, kv[..., NOPE_DIM:])
    qd = qd.reshape(B, S, DIL_HEADS, DIL_HEAD_DIM)
    kd = kd.reshape(B, S, DIL_HEADS, DIL_HEAD_DIM)
    vd = vd.reshape(B, S, DIL_HEADS, DIL_HEAD_DIM)
    slopes = _alibi_slopes(DIL_HEADS)
    outs, lses = [], []
    for window, dilation in DIL_BRANCHES:
        o_g, lse_g = _dilated_branch(qd, kd, vd, window, dilation, slopes)
        outs.append(o_g)
        lses.append(lse_g)
    wts = jax.nn.softmax(jnp.stack(lses), axis=0)
    out_b = jnp.einsum('gbsh,gbshd->bshd', wts, jnp.stack(outs).astype(jnp.float32))
    out_b = out_b.astype(h.dtype).reshape(B, S, DIL_WIDTH)
    return jnp.concatenate([out_a, out_b], axis=-1) @ w_out


def _s5_glu(h, a_re, a_im, log_dt, b_re, b_im, c_re, c_im, d_skip, w_glu):
    B, S, D = h.shape
    f32 = jnp.float32
    u = h.astype(f32).reshape(B, S, S5_GROUPS, S5_GROUP)
    A = lax.complex(a_re.astype(f32), a_im.astype(f32))
    dt = jnp.exp(log_dt.astype(f32))[:, None]
    A_bar = jnp.exp(A * dt)
    B_bar = ((A_bar - 1.0) / A)[..., None] * lax.complex(b_re.astype(f32), b_im.astype(f32))
    C_mat = lax.complex(c_re.astype(f32), c_im.astype(f32))
    bu = jnp.einsum('bsgc,gpc->bsgp', u.astype(jnp.complex64), B_bar)
    a = jnp.broadcast_to(A_bar, bu.shape)

    def combine(left, right):
        a_l, b_l = left
        a_r, b_r = right
        return a_r * a_l, a_r * b_l + b_r

    _, states = lax.associative_scan(combine, (a, bu), axis=1)
    y = jnp.einsum('bsgp,gcp->bsgc', states, C_mat).real.reshape(B, S, D)
    y = y + d_skip.astype(f32) * h.astype(f32)
    z = jax.nn.gelu(y).astype(h.dtype)
    val, gate = jnp.split(z @ w_glu, 2, axis=-1)
    return val * jax.nn.sigmoid(gate)


def _swiglu(h, w_in, w_out):
    g, u = jnp.split(h @ w_in, 2, axis=-1)
    return (jax.nn.silu(g) * u) @ w_out


def _fwd_setup_inputs(seed: int = 0) -> dict:
    key = jax.random.key(seed)
    ks = iter(jax.random.split(key, 40))
    ne, no = (DEPTH + 1) // 2, DEPTH // 2
    f32 = jnp.float32

    def nrm(shape, scale):
        return jax.random.normal(next(ks), shape, f32) * scale

    x = nrm((BATCH, SEQ, D_MODEL), 1.0)
    p = nrm((DEPTH, BATCH, SEQ, PLE_DIM), 1.0)
    positions = jnp.broadcast_to(jnp.arange(SEQ, dtype=jnp.int32), (BATCH, SEQ))
    attn_w_in = nrm((ne, D_MODEL, IN_COLS), D_MODEL ** -0.5)
    mla_q_norm = 1.0 + nrm((ne, Q_LORA), 0.02)
    mla_w_q_b = nrm((ne, Q_LORA, MLA_HEADS * (NOPE_DIM + ROPE_DIM)), Q_LORA ** -0.5)
    mla_kv_norm = 1.0 + nrm((ne, KV_LORA), 0.02)
    mla_w_kv_b = nrm((ne, KV_LORA, MLA_HEADS * (NOPE_DIM + V_DIM)), KV_LORA ** -0.5)
    attn_w_out = nrm((ne, MIX_WIDTH, D_MODEL), MIX_WIDTH ** -0.5 * DEEPNORM_BETA)
    s5_a_re = -0.5 + nrm((no, S5_GROUPS, S5_STATE), 0.01)
    s5_a_im = math.pi * jnp.arange(S5_STATE, dtype=f32) + nrm((no, S5_GROUPS, S5_STATE), 0.01)
    s5_log_dt = jax.random.uniform(next(ks), (no, S5_GROUPS), f32, math.log(1e-3), math.log(1e-1))
    s5_b_re = nrm((no, S5_GROUPS, S5_STATE, S5_GROUP), (2 * S5_GROUP) ** -0.5)
    s5_b_im = nrm((no, S5_GROUPS, S5_STATE, S5_GROUP), (2 * S5_GROUP) ** -0.5)
    s5_c_re = nrm((no, S5_GROUPS, S5_GROUP, S5_STATE), (2 * S5_STATE) ** -0.5)
    s5_c_im = nrm((no, S5_GROUPS, S5_GROUP, S5_STATE), (2 * S5_STATE) ** -0.5)
    s5_d = nrm((no, D_MODEL), 1.0)
    s5_w_glu = jnp.concatenate([nrm((no, D_MODEL, D_MODEL), D_MODEL ** -0.5 * DEEPNORM_BETA),
                                nrm((no, D_MODEL, D_MODEL), D_MODEL ** -0.5)], axis=-1)
    ln1_g = 1.0 + nrm((DEPTH, D_MODEL), 0.02)
    ln1_b = nrm((DEPTH, D_MODEL), 0.02)
    ffn_w_in = nrm((DEPTH, D_MODEL, 2 * FFN_HIDDEN), D_MODEL ** -0.5)
    ffn_w_out = nrm((DEPTH, FFN_HIDDEN, D_MODEL), FFN_HIDDEN ** -0.5 * DEEPNORM_BETA)
    ple_w = nrm((DEPTH, PLE_DIM, D_MODEL), PLE_DIM ** -0.5)
    ple_gate_w = nrm((DEPTH, D_MODEL, D_MODEL), D_MODEL ** -0.5)
    ln2_g = 1.0 + nrm((DEPTH, D_MODEL), 0.02)
    ln2_b = nrm((DEPTH, D_MODEL), 0.02)
    return {'x': x, 'p': p, 'positions': positions,
            'attn_w_in': attn_w_in, 'mla_q_norm': mla_q_norm, 'mla_w_q_b': mla_w_q_b,
            'mla_kv_norm': mla_kv_norm, 'mla_w_kv_b': mla_w_kv_b, 'attn_w_out': attn_w_out,
            's5_a_re': s5_a_re, 's5_a_im': s5_a_im, 's5_log_dt': s5_log_dt,
            's5_b_re': s5_b_re, 's5_b_im': s5_b_im, 's5_c_re': s5_c_re, 's5_c_im': s5_c_im,
            's5_d': s5_d, 's5_w_glu': s5_w_glu,
            'ln1_g': ln1_g, 'ln1_b': ln1_b, 'ffn_w_in': ffn_w_in, 'ffn_w_out': ffn_w_out,
            'ple_w': ple_w, 'ple_gate_w': ple_gate_w, 'ln2_g': ln2_g, 'ln2_b': ln2_b}


def _fwd_reference(x, p, positions, attn_w_in, mla_q_norm, mla_w_q_b, mla_kv_norm, mla_w_kv_b, attn_w_out,
              s5_a_re, s5_a_im, s5_log_dt, s5_b_re, s5_b_im, s5_c_re, s5_c_im, s5_d, s5_w_glu,
              ln1_g, ln1_b, ffn_w_in, ffn_w_out, ple_w, ple_gate_w, ln2_g, ln2_b):
    h = x
    for i in range(DEPTH):
        j = i // 2
        if i % 2 == 0:
            mix = _hybrid_attention(h, positions, attn_w_in[j], mla_q_norm[j], mla_w_q_b[j],
                                    mla_kv_norm[j], mla_w_kv_b[j], attn_w_out[j])
        else:
            mix = _s5_glu(h, s5_a_re[j], s5_a_im[j], s5_log_dt[j], s5_b_re[j], s5_b_im[j],
                          s5_c_re[j], s5_c_im[j], s5_d[j], s5_w_glu[j])
        h = _layernorm(DEEPNORM_ALPHA * h + mix, ln1_g[i], ln1_b[i])
        ple = (p[i] @ ple_w[i]) * jax.nn.sigmoid(h @ ple_gate_w[i])
        h = _layernorm(DEEPNORM_ALPHA * h + _swiglu(h, ffn_w_in[i], ffn_w_out[i]) + ple, ln2_g[i], ln2_b[i])
    return h


import jax as _jax
import jax.numpy as _jnp

TWIN_FORMAT = 'train_step'
FWD_PARAMS = ['x', 'p', 'positions', 'attn_w_in', 'mla_q_norm', 'mla_w_q_b', 'mla_kv_norm', 'mla_w_kv_b', 'attn_w_out', 's5_a_re', 's5_a_im', 's5_log_dt', 's5_b_re', 's5_b_im', 's5_c_re', 's5_c_im', 's5_d', 's5_w_glu', 'ln1_g', 'ln1_b', 'ffn_w_in', 'ffn_w_out', 'ple_w', 'ple_gate_w', 'ln2_g', 'ln2_b']
TWIN_WEIGHTS = ['attn_w_in', 'mla_q_norm', 'mla_w_q_b', 'mla_kv_norm', 'mla_w_kv_b', 'attn_w_out', 's5_a_re', 's5_a_im', 's5_log_dt', 's5_b_re', 's5_b_im', 's5_c_re', 's5_c_im', 's5_d', 's5_w_glu', 'ln1_g', 'ln1_b', 'ffn_w_in', 'ffn_w_out', 'ple_w', 'ple_gate_w', 'ln2_g', 'ln2_b']
TWIN_DIFF_INPUT = 'x'
TWIN_INPUTS = ['x', 'p', 'positions', 'attn_w_in', 'mla_q_norm', 'mla_w_q_b', 'mla_kv_norm', 'mla_w_kv_b', 'attn_w_out', 's5_a_re', 's5_a_im', 's5_log_dt', 's5_b_re', 's5_b_im', 's5_c_re', 's5_c_im', 's5_d', 's5_w_glu', 'ln1_g', 'ln1_b', 'ffn_w_in', 'ffn_w_out', 'ple_w', 'ple_gate_w', 'ln2_g', 'ln2_b', 'loss_target', 'm_attn_w_in', 'm_mla_q_norm', 'm_mla_w_q_b', 'm_mla_kv_norm', 'm_mla_w_kv_b', 'm_attn_w_out', 'm_s5_a_re', 'm_s5_a_im', 'm_s5_log_dt', 'm_s5_b_re', 'm_s5_b_im', 'm_s5_c_re', 'm_s5_c_im', 'm_s5_d', 'm_s5_w_glu', 'm_ln1_g', 'm_ln1_b', 'm_ffn_w_in', 'm_ffn_w_out', 'm_ple_w', 'm_ple_gate_w', 'm_ln2_g', 'm_ln2_b', 'v_attn_w_in', 'v_mla_q_norm', 'v_mla_w_q_b', 'v_mla_kv_norm', 'v_mla_w_kv_b', 'v_attn_w_out', 'v_s5_a_re', 'v_s5_a_im', 'v_s5_log_dt', 'v_s5_b_re', 'v_s5_b_im', 'v_s5_c_re', 'v_s5_c_im', 'v_s5_d', 'v_s5_w_glu', 'v_ln1_g', 'v_ln1_b', 'v_ffn_w_in', 'v_ffn_w_out', 'v_ple_w', 'v_ple_gate_w', 'v_ln2_g', 'v_ln2_b']
TWIN_OUTPUTS = ['loss', 'grad_x', 'grad_attn_w_in', 'grad_mla_q_norm', 'grad_mla_w_q_b', 'grad_mla_kv_norm', 'grad_mla_w_kv_b', 'grad_attn_w_out', 'grad_s5_a_re', 'grad_s5_a_im', 'grad_s5_log_dt', 'grad_s5_b_re', 'grad_s5_b_im', 'grad_s5_c_re', 'grad_s5_c_im', 'grad_s5_d', 'grad_s5_w_glu', 'grad_ln1_g', 'grad_ln1_b', 'grad_ffn_w_in', 'grad_ffn_w_out', 'grad_ple_w', 'grad_ple_gate_w', 'grad_ln2_g', 'grad_ln2_b', 'delta_attn_w_in', 'delta_mla_q_norm', 'delta_mla_w_q_b', 'delta_mla_kv_norm', 'delta_mla_w_kv_b', 'delta_attn_w_out', 'delta_s5_a_re', 'delta_s5_a_im', 'delta_s5_log_dt', 'delta_s5_b_re', 'delta_s5_b_im', 'delta_s5_c_re', 'delta_s5_c_im', 'delta_s5_d', 'delta_s5_w_glu', 'delta_ln1_g', 'delta_ln1_b', 'delta_ffn_w_in', 'delta_ffn_w_out', 'delta_ple_w', 'delta_ple_gate_w', 'delta_ln2_g', 'delta_ln2_b', 'new_m_attn_w_in', 'new_m_mla_q_norm', 'new_m_mla_w_q_b', 'new_m_mla_kv_norm', 'new_m_mla_w_kv_b', 'new_m_attn_w_out', 'new_m_s5_a_re', 'new_m_s5_a_im', 'new_m_s5_log_dt', 'new_m_s5_b_re', 'new_m_s5_b_im', 'new_m_s5_c_re', 'new_m_s5_c_im', 'new_m_s5_d', 'new_m_s5_w_glu', 'new_m_ln1_g', 'new_m_ln1_b', 'new_m_ffn_w_in', 'new_m_ffn_w_out', 'new_m_ple_w', 'new_m_ple_gate_w', 'new_m_ln2_g', 'new_m_ln2_b', 'new_v_attn_w_in', 'new_v_mla_q_norm', 'new_v_mla_w_q_b', 'new_v_mla_kv_norm', 'new_v_mla_w_kv_b', 'new_v_attn_w_out', 'new_v_s5_a_re', 'new_v_s5_a_im', 'new_v_s5_log_dt', 'new_v_s5_b_re', 'new_v_s5_b_im', 'new_v_s5_c_re', 'new_v_s5_c_im', 'new_v_s5_d', 'new_v_s5_w_glu', 'new_v_ln1_g', 'new_v_ln1_b', 'new_v_ffn_w_in', 'new_v_ffn_w_out', 'new_v_ple_w', 'new_v_ple_gate_w', 'new_v_ln2_g', 'new_v_ln2_b']
TWIN_LEAF_KINDS = {'loss': 'loss', 'grad_x': 'grad_x', 'grad_attn_w_in': 'grad_w', 'grad_mla_q_norm': 'grad_w', 'grad_mla_w_q_b': 'grad_w', 'grad_mla_kv_norm': 'grad_w', 'grad_mla_w_kv_b': 'grad_w', 'grad_attn_w_out': 'grad_w', 'grad_s5_a_re': 'grad_w', 'grad_s5_a_im': 'grad_w', 'grad_s5_log_dt': 'grad_w', 'grad_s5_b_re': 'grad_w', 'grad_s5_b_im': 'grad_w', 'grad_s5_c_re': 'grad_w', 'grad_s5_c_im': 'grad_w', 'grad_s5_d': 'grad_w', 'grad_s5_w_glu': 'grad_w', 'grad_ln1_g': 'grad_w', 'grad_ln1_b': 'grad_w', 'grad_ffn_w_in': 'grad_w', 'grad_ffn_w_out': 'grad_w', 'grad_ple_w': 'grad_w', 'grad_ple_gate_w': 'grad_w', 'grad_ln2_g': 'grad_w', 'grad_ln2_b': 'grad_w', 'delta_attn_w_in': 'delta_w', 'delta_mla_q_norm': 'delta_w', 'delta_mla_w_q_b': 'delta_w', 'delta_mla_kv_norm': 'delta_w', 'delta_mla_w_kv_b': 'delta_w', 'delta_attn_w_out': 'delta_w', 'delta_s5_a_re': 'delta_w', 'delta_s5_a_im': 'delta_w', 'delta_s5_log_dt': 'delta_w', 'delta_s5_b_re': 'delta_w', 'delta_s5_b_im': 'delta_w', 'delta_s5_c_re': 'delta_w', 'delta_s5_c_im': 'delta_w', 'delta_s5_d': 'delta_w', 'delta_s5_w_glu': 'delta_w', 'delta_ln1_g': 'delta_w', 'delta_ln1_b': 'delta_w', 'delta_ffn_w_in': 'delta_w', 'delta_ffn_w_out': 'delta_w', 'delta_ple_w': 'delta_w', 'delta_ple_gate_w': 'delta_w', 'delta_ln2_g': 'delta_w', 'delta_ln2_b': 'delta_w', 'new_m_attn_w_in': 'new_m', 'new_m_mla_q_norm': 'new_m', 'new_m_mla_w_q_b': 'new_m', 'new_m_mla_kv_norm': 'new_m', 'new_m_mla_w_kv_b': 'new_m', 'new_m_attn_w_out': 'new_m', 'new_m_s5_a_re': 'new_m', 'new_m_s5_a_im': 'new_m', 'new_m_s5_log_dt': 'new_m', 'new_m_s5_b_re': 'new_m', 'new_m_s5_b_im': 'new_m', 'new_m_s5_c_re': 'new_m', 'new_m_s5_c_im': 'new_m', 'new_m_s5_d': 'new_m', 'new_m_s5_w_glu': 'new_m', 'new_m_ln1_g': 'new_m', 'new_m_ln1_b': 'new_m', 'new_m_ffn_w_in': 'new_m', 'new_m_ffn_w_out': 'new_m', 'new_m_ple_w': 'new_m', 'new_m_ple_gate_w': 'new_m', 'new_m_ln2_g': 'new_m', 'new_m_ln2_b': 'new_m', 'new_v_attn_w_in': 'new_v', 'new_v_mla_q_norm': 'new_v', 'new_v_mla_w_q_b': 'new_v', 'new_v_mla_kv_norm': 'new_v', 'new_v_mla_w_kv_b': 'new_v', 'new_v_attn_w_out': 'new_v', 'new_v_s5_a_re': 'new_v', 'new_v_s5_a_im': 'new_v', 'new_v_s5_log_dt': 'new_v', 'new_v_s5_b_re': 'new_v', 'new_v_s5_b_im': 'new_v', 'new_v_s5_c_re': 'new_v', 'new_v_s5_c_im': 'new_v', 'new_v_s5_d': 'new_v', 'new_v_s5_w_glu': 'new_v', 'new_v_ln1_g': 'new_v', 'new_v_ln1_b': 'new_v', 'new_v_ffn_w_in': 'new_v', 'new_v_ffn_w_out': 'new_v', 'new_v_ple_w': 'new_v', 'new_v_ple_gate_w': 'new_v', 'new_v_ln2_g': 'new_v', 'new_v_ln2_b': 'new_v'}


def _forward(args):
    return _fwd_reference(*[args[k] for k in FWD_PARAMS])


def _output_shape():
    def fwd():
        inp = _fwd_setup_inputs(0)
        return _fwd_reference(*[inp[k] for k in FWD_PARAMS])
    out = _jax.eval_shape(fwd)
    return out.shape, out.dtype

N_MICROBATCH = 1
ADAM_LR = 0.001
ADAM_B1 = 0.9
ADAM_B2 = 0.999
ADAM_EPS = 1e-08
ADAM_WD = 0.01
ADAM_STEP = 10
PER_EXAMPLE_BATCH_AXIS = {'x': 0, 'p': 1, 'positions': 0, 'loss_target': 0}
SHARED_INPUTS = []
_WEIGHT_DTYPES = {'attn_w_in': _jnp.float32, 'mla_q_norm': _jnp.float32, 'mla_w_q_b': _jnp.float32, 'mla_kv_norm': _jnp.float32, 'mla_w_kv_b': _jnp.float32, 'attn_w_out': _jnp.float32, 's5_a_re': _jnp.float32, 's5_a_im': _jnp.float32, 's5_log_dt': _jnp.float32, 's5_b_re': _jnp.float32, 's5_b_im': _jnp.float32, 's5_c_re': _jnp.float32, 's5_c_im': _jnp.float32, 's5_d': _jnp.float32, 's5_w_glu': _jnp.float32, 'ln1_g': _jnp.float32, 'ln1_b': _jnp.float32, 'ffn_w_in': _jnp.float32, 'ffn_w_out': _jnp.float32, 'ple_w': _jnp.float32, 'ple_gate_w': _jnp.float32, 'ln2_g': _jnp.float32, 'ln2_b': _jnp.float32}
MOMENT_SCALE = {'attn_w_in': 1.286886e-02, 'mla_q_norm': 1.086143e-02, 'mla_w_q_b': 6.009641e-03, 'mla_kv_norm': 1.573477e-02, 'mla_w_kv_b': 7.614069e-03, 'attn_w_out': 2.601582e-02, 's5_a_re': 8.113116e-04, 's5_a_im': 7.982711e-04, 's5_log_dt': 3.748838e-01, 's5_b_re': 5.224155e-04, 's5_b_im': 5.139519e-04, 's5_c_re': 1.018848e-03, 's5_c_im': 1.022796e-03, 's5_d': 2.559863e-02, 's5_w_glu': 3.776928e-02, 'ln1_g': 8.503765e-01, 'ln1_b': 4.461411e-01, 'ffn_w_in': 1.464963e-02, 'ffn_w_out': 4.784977e-02, 'ple_w': 4.352498e-02, 'ple_gate_w': 1.696775e-02, 'ln2_g': 2.264610e+01, 'ln2_b': 1.313960e+00}


def _to_microbatches(a, axis):
    t = _jnp.moveaxis(a, axis, 0)
    t = t.reshape((N_MICROBATCH, t.shape[0] // N_MICROBATCH) + t.shape[1:])
    return _jnp.moveaxis(t, 1, axis + 1)


def setup_inputs(seed: int = 0) -> dict:
    inp = _fwd_setup_inputs(seed)
    key = _jax.random.fold_in(_jax.random.key(seed), 7919)
    shape, _ = _output_shape()
    out = dict(inp)
    out["loss_target"] = _jax.random.normal(_jax.random.fold_in(key, 0), shape, _jnp.float32)
    for i, name in enumerate(TWIN_WEIGHTS):
        w = inp[name].astype(_jnp.float32)
        if MOMENT_SCALE is None:
            s = _jnp.sqrt(_jnp.mean(_jnp.square(w)) + 1e-30)
        else:
            s = MOMENT_SCALE[name]
        km, kv = _jax.random.split(_jax.random.fold_in(key, i + 1))
        out[name] = w
        out["m_" + name] = s * _jax.random.normal(km, w.shape, _jnp.float32)
        out["v_" + name] = (s * s) * _jax.random.uniform(kv, w.shape, _jnp.float32, 0.5, 1.5)
    if N_MICROBATCH > 1:
        for name, axis in PER_EXAMPLE_BATCH_AXIS.items():
            out[name] = _to_microbatches(out[name], axis)
    return {'x': out['x'], 'p': out['p'], 'positions': out['positions'], 'attn_w_in': out['attn_w_in'], 'mla_q_norm': out['mla_q_norm'], 'mla_w_q_b': out['mla_w_q_b'], 'mla_kv_norm': out['mla_kv_norm'], 'mla_w_kv_b': out['mla_w_kv_b'], 'attn_w_out': out['attn_w_out'], 's5_a_re': out['s5_a_re'], 's5_a_im': out['s5_a_im'], 's5_log_dt': out['s5_log_dt'], 's5_b_re': out['s5_b_re'], 's5_b_im': out['s5_b_im'], 's5_c_re': out['s5_c_re'], 's5_c_im': out['s5_c_im'], 's5_d': out['s5_d'], 's5_w_glu': out['s5_w_glu'], 'ln1_g': out['ln1_g'], 'ln1_b': out['ln1_b'], 'ffn_w_in': out['ffn_w_in'], 'ffn_w_out': out['ffn_w_out'], 'ple_w': out['ple_w'], 'ple_gate_w': out['ple_gate_w'], 'ln2_g': out['ln2_g'], 'ln2_b': out['ln2_b'], 'loss_target': out['loss_target'], 'm_attn_w_in': out['m_attn_w_in'], 'm_mla_q_norm': out['m_mla_q_norm'], 'm_mla_w_q_b': out['m_mla_w_q_b'], 'm_mla_kv_norm': out['m_mla_kv_norm'], 'm_mla_w_kv_b': out['m_mla_w_kv_b'], 'm_attn_w_out': out['m_attn_w_out'], 'm_s5_a_re': out['m_s5_a_re'], 'm_s5_a_im': out['m_s5_a_im'], 'm_s5_log_dt': out['m_s5_log_dt'], 'm_s5_b_re': out['m_s5_b_re'], 'm_s5_b_im': out['m_s5_b_im'], 'm_s5_c_re': out['m_s5_c_re'], 'm_s5_c_im': out['m_s5_c_im'], 'm_s5_d': out['m_s5_d'], 'm_s5_w_glu': out['m_s5_w_glu'], 'm_ln1_g': out['m_ln1_g'], 'm_ln1_b': out['m_ln1_b'], 'm_ffn_w_in': out['m_ffn_w_in'], 'm_ffn_w_out': out['m_ffn_w_out'], 'm_ple_w': out['m_ple_w'], 'm_ple_gate_w': out['m_ple_gate_w'], 'm_ln2_g': out['m_ln2_g'], 'm_ln2_b': out['m_ln2_b'], 'v_attn_w_in': out['v_attn_w_in'], 'v_mla_q_norm': out['v_mla_q_norm'], 'v_mla_w_q_b': out['v_mla_w_q_b'], 'v_mla_kv_norm': out['v_mla_kv_norm'], 'v_mla_w_kv_b': out['v_mla_w_kv_b'], 'v_attn_w_out': out['v_attn_w_out'], 'v_s5_a_re': out['v_s5_a_re'], 'v_s5_a_im': out['v_s5_a_im'], 'v_s5_log_dt': out['v_s5_log_dt'], 'v_s5_b_re': out['v_s5_b_re'], 'v_s5_b_im': out['v_s5_b_im'], 'v_s5_c_re': out['v_s5_c_re'], 'v_s5_c_im': out['v_s5_c_im'], 'v_s5_d': out['v_s5_d'], 'v_s5_w_glu': out['v_s5_w_glu'], 'v_ln1_g': out['v_ln1_g'], 'v_ln1_b': out['v_ln1_b'], 'v_ffn_w_in': out['v_ffn_w_in'], 'v_ffn_w_out': out['v_ffn_w_out'], 'v_ple_w': out['v_ple_w'], 'v_ple_gate_w': out['v_ple_gate_w'], 'v_ln2_g': out['v_ln2_g'], 'v_ln2_b': out['v_ln2_b']}


def _loss(weights, diff, rest, loss_target):
    with _jax.named_scope("forward"):
        args = {**rest, TWIN_DIFF_INPUT: diff, **{k: w.astype(_WEIGHT_DTYPES[k]) for k, w in weights.items()}}
        y = _forward(args)
    with _jax.named_scope("loss_head"):
        err = _jnp.square(y.astype(_jnp.float32) - loss_target)
        return 0.5 * _jnp.sum(_jnp.mean(err, axis=-1)) if err.ndim else 0.5 * err


def _adamw(w, g, m, v):
    m = ADAM_B1 * m + (1.0 - ADAM_B1) * g
    v = ADAM_B2 * v + (1.0 - ADAM_B2) * _jnp.square(g)
    m_hat = m / (1.0 - ADAM_B1 ** ADAM_STEP)
    v_hat = v / (1.0 - ADAM_B2 ** ADAM_STEP)
    delta = -ADAM_LR * (m_hat / (_jnp.sqrt(v_hat) + ADAM_EPS) + ADAM_WD * w)
    return delta, m, v


def reference(x, p, positions, attn_w_in, mla_q_norm, mla_w_q_b, mla_kv_norm, mla_w_kv_b, attn_w_out, s5_a_re, s5_a_im, s5_log_dt, s5_b_re, s5_b_im, s5_c_re, s5_c_im, s5_d, s5_w_glu, ln1_g, ln1_b, ffn_w_in, ffn_w_out, ple_w, ple_gate_w, ln2_g, ln2_b, loss_target, m_attn_w_in, m_mla_q_norm, m_mla_w_q_b, m_mla_kv_norm, m_mla_w_kv_b, m_attn_w_out, m_s5_a_re, m_s5_a_im, m_s5_log_dt, m_s5_b_re, m_s5_b_im, m_s5_c_re, m_s5_c_im, m_s5_d, m_s5_w_glu, m_ln1_g, m_ln1_b, m_ffn_w_in, m_ffn_w_out, m_ple_w, m_ple_gate_w, m_ln2_g, m_ln2_b, v_attn_w_in, v_mla_q_norm, v_mla_w_q_b, v_mla_kv_norm, v_mla_w_kv_b, v_attn_w_out, v_s5_a_re, v_s5_a_im, v_s5_log_dt, v_s5_b_re, v_s5_b_im, v_s5_c_re, v_s5_c_im, v_s5_d, v_s5_w_glu, v_ln1_g, v_ln1_b, v_ffn_w_in, v_ffn_w_out, v_ple_w, v_ple_gate_w, v_ln2_g, v_ln2_b):
    given = dict(x=x, p=p, positions=positions, attn_w_in=attn_w_in, mla_q_norm=mla_q_norm, mla_w_q_b=mla_w_q_b, mla_kv_norm=mla_kv_norm, mla_w_kv_b=mla_w_kv_b, attn_w_out=attn_w_out, s5_a_re=s5_a_re, s5_a_im=s5_a_im, s5_log_dt=s5_log_dt, s5_b_re=s5_b_re, s5_b_im=s5_b_im, s5_c_re=s5_c_re, s5_c_im=s5_c_im, s5_d=s5_d, s5_w_glu=s5_w_glu, ln1_g=ln1_g, ln1_b=ln1_b, ffn_w_in=ffn_w_in, ffn_w_out=ffn_w_out, ple_w=ple_w, ple_gate_w=ple_gate_w, ln2_g=ln2_g, ln2_b=ln2_b, loss_target=loss_target, m_attn_w_in=m_attn_w_in, m_mla_q_norm=m_mla_q_norm, m_mla_w_q_b=m_mla_w_q_b, m_mla_kv_norm=m_mla_kv_norm, m_mla_w_kv_b=m_mla_w_kv_b, m_attn_w_out=m_attn_w_out, m_s5_a_re=m_s5_a_re, m_s5_a_im=m_s5_a_im, m_s5_log_dt=m_s5_log_dt, m_s5_b_re=m_s5_b_re, m_s5_b_im=m_s5_b_im, m_s5_c_re=m_s5_c_re, m_s5_c_im=m_s5_c_im, m_s5_d=m_s5_d, m_s5_w_glu=m_s5_w_glu, m_ln1_g=m_ln1_g, m_ln1_b=m_ln1_b, m_ffn_w_in=m_ffn_w_in, m_ffn_w_out=m_ffn_w_out, m_ple_w=m_ple_w, m_ple_gate_w=m_ple_gate_w, m_ln2_g=m_ln2_g, m_ln2_b=m_ln2_b, v_attn_w_in=v_attn_w_in, v_mla_q_norm=v_mla_q_norm, v_mla_w_q_b=v_mla_w_q_b, v_mla_kv_norm=v_mla_kv_norm, v_mla_w_kv_b=v_mla_w_kv_b, v_attn_w_out=v_attn_w_out, v_s5_a_re=v_s5_a_re, v_s5_a_im=v_s5_a_im, v_s5_log_dt=v_s5_log_dt, v_s5_b_re=v_s5_b_re, v_s5_b_im=v_s5_b_im, v_s5_c_re=v_s5_c_re, v_s5_c_im=v_s5_c_im, v_s5_d=v_s5_d, v_s5_w_glu=v_s5_w_glu, v_ln1_g=v_ln1_g, v_ln1_b=v_ln1_b, v_ffn_w_in=v_ffn_w_in, v_ffn_w_out=v_ffn_w_out, v_ple_w=v_ple_w, v_ple_gate_w=v_ple_gate_w, v_ln2_g=v_ln2_g, v_ln2_b=v_ln2_b)
    weights = {n: given[n] for n in TWIN_WEIGHTS}
    shared = {n: given[n] for n in SHARED_INPUTS}
    per_example = {n: given[n] for n in ['x', 'p', 'positions']}
    grad_fn = _jax.value_and_grad(_loss, argnums=(0, 1))

    def one_microbatch(ex, loss_target):
        ex = dict(ex)
        diff = ex.pop(TWIN_DIFF_INPUT)
        return grad_fn(weights, diff, {**shared, **ex}, loss_target)

    if N_MICROBATCH == 1:
        loss, (grad_w, grad_x) = one_microbatch(per_example, given["loss_target"])
    else:
        def body(carry, xs):
            loss_sum, grad_sum = carry
            l_k, (gw_k, gx_k) = one_microbatch(xs[0], xs[1])
            with _jax.named_scope("update"):
                return (loss_sum + l_k, _jax.tree.map(_jnp.add, grad_sum, gw_k)), gx_k

        init = (_jnp.zeros((), _jnp.float32), _jax.tree.map(_jnp.zeros_like, weights))
        (loss, grad_w), grad_x = _jax.lax.scan(body, init, (per_example, given["loss_target"]))
    with _jax.named_scope("update"):
        delta_w, new_m, new_v = {}, {}, {}
        for n in TWIN_WEIGHTS:
            delta_w[n], new_m[n], new_v[n] = _adamw(weights[n], grad_w[n], given["m_" + n], given["v_" + n])
    return (loss, grad_x, *[grad_w[n] for n in TWIN_WEIGHTS], *[delta_w[n] for n in TWIN_WEIGHTS],
            *[new_m[n] for n in TWIN_WEIGHTS], *[new_v[n] for n in TWIN_WEIGHTS])
```

```python
import functools
import math

import numpy as np
import jax
import jax.numpy as jnp
from jax import lax
from jax.experimental import pallas as pl
from jax.experimental.pallas import tpu as pltpu

F32 = jnp.float32
BF16 = jnp.bfloat16

NOPE_DIM, ROPE_DIM, V_DIM = 128, 64, 128
DIL_HEAD_DIM = 128
DIL_BRANCHES = ((128, 1), (512, 4), (2048, 16))
BLK = 128
LANE = 128
ROPE_THETA = 10000.0
S5_GROUP, S5_STATE = 16, 64
S5_CHUNK = 32
NEG = -1e30
ADAM_LR, ADAM_B1, ADAM_B2, ADAM_EPS, ADAM_WD, ADAM_STEP = 0.001, 0.9, 0.999, 1e-08, 0.01, 10
N_DEV = 8
AXES = ("x", "y", "c")
VMEM_LIMIT = 56 * 1024 * 1024
PACK_COLS = 1024
HI = lax.Precision.HIGHEST


def _tile(n, cands):
    for c in cands:
        if n % c == 0:
            return c
    return n


def _params(sem=None):
    return pltpu.CompilerParams(dimension_semantics=sem, vmem_limit_bytes=VMEM_LIMIT)


def _dot(a, b, dims, prec=None):
    return lax.dot_general(a, b, (dims, ((), ())), preferred_element_type=F32, precision=prec)


NN = ((1,), (0,))
NT = ((1,), (1,))
TN = ((0,), (0,))


def _mm(a, b, *, ta=False, tb=False, out_dtype=F32, name):
    (k, m) = a.shape if ta else a.shape[::-1]
    (n, k2) = b.shape if tb else b.shape[::-1]
    assert k == k2, (a.shape, b.shape, ta, tb)
    tm = _tile(m, (1024, 512, 256, 128))
    tn = _tile(n, (1024, 512, 384, 256, 128))
    tk = _tile(k, (512, 384, 256, 128))
    nk = k // tk
    dims = (((0,) if ta else (1,)), ((1,) if tb else (0,)))

    def body(a_ref, b_ref, o_ref, acc_ref):
        kk = pl.program_id(2)

        @pl.when(kk == 0)
        def _():
            acc_ref[...] = jnp.zeros_like(acc_ref)

        acc_ref[...] += _dot(a_ref[...].astype(BF16), b_ref[...].astype(BF16), dims)

        @pl.when(kk == nk - 1)
        def _():
            o_ref[...] = acc_ref[...].astype(o_ref.dtype)

    a_spec = pl.BlockSpec((tk, tm), lambda i, j, kk: (kk, i)) if ta else pl.BlockSpec((tm, tk), lambda i, j, kk: (i, kk))
    b_spec = pl.BlockSpec((tn, tk), lambda i, j, kk: (j, kk)) if tb else pl.BlockSpec((tk, tn), lambda i, j, kk: (kk, j))
    return pl.pallas_call(
        body,
        name=name,
        grid=(m // tm, n // tn, nk),
        in_specs=[a_spec, b_spec],
        out_specs=pl.BlockSpec((tm, tn), lambda i, j, kk: (i, j)),
        out_shape=jax.ShapeDtypeStruct((m, n), out_dtype),
        scratch_shapes=[pltpu.VMEM((tm, tn), F32)],
        compiler_params=_params(("parallel", "parallel", "arbitrary")),
    )(a, b)


def _row_spec(spec, tq):
    _, w, cb = spec
    return pl.BlockSpec((tq, w), lambda i, j: (i, cb + j))


def _full_spec(p):
    return pl.BlockSpec(p.shape, lambda i, j: (0,) * p.ndim)


def _rowwise(fn, rows, pars, outs, *, name, tq, ncol=1):
    t = rows[0][0].shape[0]
    tq = _tile(t, (tq, 128, 64, 32, 16, 8))
    nr, npar = len(rows), len(pars)

    def body(*refs):
        vals = [r[...].astype(F32) for r in refs[: nr + npar]]
        res = fn(*vals)
        for o, r in zip(refs[nr + npar :], res):
            o[...] = r.astype(o.dtype)

    res = pl.pallas_call(
        body,
        name=name,
        grid=(t // tq, ncol),
        in_specs=[_row_spec(s, tq) for s in rows] + [_full_spec(p) for p in pars],
        out_specs=[pl.BlockSpec((tq, w), lambda i, j: (i, j)) for (_, w, _) in outs],
        out_shape=[jax.ShapeDtypeStruct((t, wt), dt) for (wt, _, dt) in outs],
        compiler_params=_params(("parallel", "parallel")),
    )(*[s[0] for s in rows], *pars)
    return res


def _rowwise_bwd(fn, rows, pars, cots, *, need, drow, name, tq, ncol=1):
    t = rows[0][0].shape[0]
    tq = _tile(t, (tq, 128, 64, 32, 16, 8))
    nr, npar = len(rows), len(pars)
    assert ncol == 1 or npar == 0
    flat_cots = [s for c in cots for s in c]
    ncot = len(flat_cots)
    want = [k for k in range(nr) if need[k]]

    def body(*refs):
        vals = [r[...].astype(F32) for r in refs[: nr + npar]]
        cref = refs[nr + npar : nr + npar + ncot]
        oref = refs[nr + npar + ncot :]
        cvals, pos = [], 0
        for c in cots:
            acc = cref[pos][...].astype(F32)
            for q in range(1, len(c)):
                acc = acc + cref[pos + q][...].astype(F32)
            cvals.append(acc)
            pos += len(c)

        def closed(*diff):
            full = list(vals)
            for k, dv in zip(want + list(range(nr, nr + npar)), diff):
                full[k] = dv
            return tuple(fn(*full))

        diff_in = [vals[k] for k in want] + vals[nr:]
        _, vjp = jax.vjp(closed, *diff_in)
        grads = vjp(tuple(cvals))
        for q in range(len(want)):
            oref[q][...] = grads[q].astype(oref[q].dtype)
        if npar:
            first = pl.program_id(0) == 0

            @pl.when(first)
            def _():
                for q in range(npar):
                    oref[len(want) + q][...] = jnp.zeros_like(oref[len(want) + q])

            for q in range(npar):
                oref[len(want) + q][...] += grads[len(want) + q]

    out_specs = [pl.BlockSpec((tq, rows[k][1]), lambda i, j, cb=rows[k][2]: (i, j)) for k in want]
    out_specs += [_full_spec(p) for p in pars]
    out_shape = [jax.ShapeDtypeStruct((t, wt), dt) for (wt, dt) in drow]
    out_shape += [jax.ShapeDtypeStruct(p.shape, F32) for p in pars]
    return pl.pallas_call(
        body,
        name=name,
        grid=(t // tq, ncol),
        in_specs=[_row_spec(s, tq) for s in rows] + [_full_spec(p) for p in pars] + [_row_spec(s, tq) for s in flat_cots],
        out_specs=out_specs,
        out_shape=out_shape,
        compiler_params=_params(("arbitrary", "arbitrary") if npar else ("parallel", "parallel")),
    )(*[s[0] for s in rows], *pars, *[s[0] for s in flat_cots])


def _whole(a, w=None):
    return (a, a.shape[1] if w is None else w, 0)


def _rms_fn(ql, kvl, gq, gk):
    def one(x, g):
        return x * lax.rsqrt(jnp.mean(x * x, -1, keepdims=True) + 1e-6) * g

    return one(ql, gq), one(kvl, gk)


def _layernorm(s, g, b):
    mu = jnp.mean(s, -1, keepdims=True)
    d = s - mu
    var = jnp.mean(d * d, -1, keepdims=True)
    return d * lax.rsqrt(var + 1e-5) * g + b


def _ln_mix_fn(alpha):
    def fn(h, mix, g, b):
        return (_layernorm(alpha * h + mix, g, b),)

    return fn


def _ln_glu_fn(alpha):
    def fn(h, val, gate, g, b):
        return (_layernorm(alpha * h + val * jax.nn.sigmoid(gate), g, b),)

    return fn


def _ln_ffn_fn(alpha):
    def fn(h, f, pw, gate, g, b):
        return (_layernorm(alpha * h + f + pw * jax.nn.sigmoid(gate), g, b),)

    return fn


def _swiglu_fn(g, u):
    return (jax.nn.silu(g) * u,)


def _gelu_fn(ys, h, d):
    return (jax.nn.gelu(ys + d * h),)


def _merge_fn(o1, o2, o3, l1, l2, l3):
    m = jnp.maximum(jnp.maximum(l1, l2), l3)
    e1, e2, e3 = jnp.exp(l1 - m), jnp.exp(l2 - m), jnp.exp(l3 - m)
    return ((e1 * o1 + e2 * o2 + e3 * o3) / (e1 + e2 + e3),)


def _swap_halves(t):
    w = t.shape[1]
    lane = lax.broadcasted_iota(jnp.int32, t.shape, 1) % LANE
    up = jnp.where(lane < ROPE_DIM, pltpu.roll(t, ROPE_DIM // 2, 1), 0.0)
    return jnp.where(lane < ROPE_DIM // 2, pltpu.roll(t, w - ROPE_DIM // 2, 1), up)


def _rope_fwd_fn(nh):
    def fn(qpe, kpe, cos, sin):
        cq, sq = jnp.tile(cos, (1, nh)), jnp.tile(sin, (1, nh))
        return qpe * cq + _swap_halves(qpe) * sq, kpe * cos + _swap_halves(kpe) * sin

    return fn


def _rope_bwd_fn(nh):
    def fn(dq, dk_heads, cos, sin):
        cq, sq = jnp.tile(cos, (1, nh)), jnp.tile(sin, (1, nh))
        dk = dk_heads[:, :LANE]
        for h in range(1, nh):
            dk = dk + dk_heads[:, h * LANE : (h + 1) * LANE]
        return dq * cq + _swap_halves(dq * sq), dk * cos + _swap_halves(dk * sin)

    return fn


def _rope_tables(positions, name):
    t = positions.shape[0]
    tq = _tile(t, (512, 128, 8))
    half = ROPE_DIM // 2

    def body(p_ref, c_ref, s_ref):
        lane = lax.broadcasted_iota(jnp.int32, (tq, LANE), 1)
        idx = (lane % half).astype(F32)
        inv_freq = jnp.exp(idx * (-math.log(ROPE_THETA) / half))
        ang = p_ref[...].astype(F32) * inv_freq
        live = lane < ROPE_DIM
        c_ref[...] = jnp.where(live, jnp.cos(ang), 0.0)
        s_ref[...] = jnp.where(live, jnp.where(lane < half, -jnp.sin(ang), jnp.sin(ang)), 0.0)

    return pl.pallas_call(
        body,
        name=name,
        grid=(t // tq,),
        in_specs=[pl.BlockSpec((tq, 1), lambda i: (i, 0))],
        out_specs=[pl.BlockSpec((tq, LANE), lambda i: (i, 0))] * 2,
        out_shape=[jax.ShapeDtypeStruct((t, LANE), F32)] * 2,
        compiler_params=_params(("parallel",)),
    )(positions)


def _loss_kernel(y, target, name):
    t, d = y.shape
    tq = _tile(t, (256, 128, 8))

    def body(y_ref, t_ref, dy_ref, l_ref):
        @pl.when(pl.program_id(0) == 0)
        def _():
            l_ref[...] = jnp.zeros_like(l_ref)

        e = y_ref[...] - t_ref[...]
        dy_ref[...] = e * (1.0 / d)
        l_ref[...] += jnp.sum(e * e) * (0.5 / d)

    return pl.pallas_call(
        body,
        name=name,
        grid=(t // tq,),
        in_specs=[pl.BlockSpec((tq, d), lambda i: (i, 0))] * 2,
        out_specs=[pl.BlockSpec((tq, d), lambda i: (i, 0)), pl.BlockSpec((8, LANE), lambda i: (0, 0))],
        out_shape=[jax.ShapeDtypeStruct((t, d), F32), jax.ShapeDtypeStruct((8, LANE), F32)],
        compiler_params=_params(("arbitrary",)),
    )(y, target)


def _add2(a, b, name):
    return _rowwise(lambda p, q: (p + q,), [_whole(a), _whole(b)], [], [(a.shape[1], a.shape[1], F32)], name=name, tq=256)[0]


def _mla_tiles(t):
    tq = _tile(t, (512, 256, 128))
    return tq, t // tq


def _mla_fwd(qf, qpe, kvf, kpe, nh, name):
    t = qf.shape[0]
    tq, nq = _mla_tiles(t)
    scale = (NOPE_DIM + ROPE_DIM) ** -0.5

    def body(qn_ref, qp_ref, kn_ref, kp_ref, v_ref, o_ref, lse_ref):
        i = pl.program_id(1)
        qn = qn_ref[...].astype(BF16)
        qp = qp_ref[...]

        def step(j, carry, masked):
            m, l, acc = carry
            ks = pl.ds(pl.multiple_of(j * tq, tq), tq)
            s = (_dot(qn, kn_ref[ks, :], NT) + _dot(qp, kp_ref[ks, :], NT)) * scale
            if masked:
                row = lax.broadcasted_iota(jnp.int32, (tq, tq), 0)
                col = lax.broadcasted_iota(jnp.int32, (tq, tq), 1)
                s = jnp.where(col <= row, s, NEG)
            m_new = jnp.maximum(m, jnp.max(s, -1, keepdims=True))
            p = jnp.exp(s - m_new)
            a = jnp.exp(m - m_new)
            return m_new, a * l + jnp.sum(p, -1, keepdims=True), a * acc + _dot(p.astype(BF16), v_ref[ks, :], NN)

        init = (jnp.full((tq, 1), NEG, F32), jnp.zeros((tq, 1), F32), jnp.zeros((tq, V_DIM), F32))
        carry = lax.fori_loop(0, i, lambda j, c: step(j, c, False), init)
        m, l, acc = step(i, carry, True)
        o_ref[...] = (acc / l).astype(o_ref.dtype)
        lse_ref[...] = jnp.broadcast_to(m + jnp.log(l), (tq, LANE))

    blk = lambda h, i: (i, h)
    return pl.pallas_call(
        body,
        name=name,
        grid=(nh, nq),
        in_specs=[
            pl.BlockSpec((tq, LANE), blk),
            pl.BlockSpec((tq, LANE), blk),
            pl.BlockSpec((t, LANE), lambda h, i: (0, h)),
            pl.BlockSpec((t, LANE), lambda h, i: (0, 0)),
            pl.BlockSpec((t, LANE), lambda h, i: (0, nh + h)),
        ],
        out_specs=[pl.BlockSpec((tq, LANE), blk), pl.BlockSpec((tq, LANE), blk)],
        out_shape=[jax.ShapeDtypeStruct((t, nh * LANE), BF16), jax.ShapeDtypeStruct((t, nh * LANE), F32)],
        compiler_params=_params(("parallel", "parallel")),
    )(qf, qpe, kvf, kpe, kvf)


def _mla_bwd_q(qf, qpe, kvf, kpe, do, o, lse, nh, name):
    t = qf.shape[0]
    tq, nq = _mla_tiles(t)
    scale = (NOPE_DIM + ROPE_DIM) ** -0.5

    def body(qn_ref, qp_ref, kn_ref, kp_ref, v_ref, do_ref, o_ref, lse_ref, dqn_ref, dqp_ref, dl_ref):
        i = pl.program_id(1)
        qn = qn_ref[...].astype(BF16)
        qp = qp_ref[...]
        dof = do_ref[...].astype(F32)
        dob = dof.astype(BF16)
        delta = jnp.sum(dof * o_ref[...].astype(F32), -1, keepdims=True)
        lse1 = lse_ref[:, :1]

        def step(j, carry, masked):
            dqn, dqp = carry
            ks = pl.ds(pl.multiple_of(j * tq, tq), tq)
            kn, kp = kn_ref[ks, :], kp_ref[ks, :]
            s = (_dot(qn, kn, NT) + _dot(qp, kp, NT)) * scale
            p = jnp.exp(s - lse1)
            if masked:
                row = lax.broadcasted_iota(jnp.int32, (tq, tq), 0)
                col = lax.broadcasted_iota(jnp.int32, (tq, tq), 1)
                p = jnp.where(col <= row, p, 0.0)
            dp = _dot(dob, v_ref[ks, :], NT)
            ds = (p * (dp - delta) * scale).astype(BF16)
            return dqn + _dot(ds, kn, NN), dqp + _dot(ds, kp, NN)

        init = (jnp.zeros((tq, LANE), F32), jnp.zeros((tq, LANE), F32))
        carry = lax.fori_loop(0, i, lambda j, c: step(j, c, False), init)
        dqn, dqp = step(i, carry, True)
        dqn_ref[...] = dqn
        dqp_ref[...] = dqp
        dl_ref[...] = jnp.broadcast_to(delta, (tq, LANE))

    blk = lambda h, i: (i, h)
    bs = pl.BlockSpec((tq, LANE), blk)
    return pl.pallas_call(
        body,
        name=name,
        grid=(nh, nq),
        in_specs=[
            bs,
            bs,
            pl.BlockSpec((t, LANE), lambda h, i: (0, h)),
            pl.BlockSpec((t, LANE), lambda h, i: (0, 0)),
            pl.BlockSpec((t, LANE), lambda h, i: (0, nh + h)),
            bs,
            bs,
            bs,
        ],
        out_specs=[bs, bs, bs],
        out_shape=[jax.ShapeDtypeStruct((t, nh * LANE), F32)] * 3,
        compiler_params=_params(("parallel", "parallel")),
    )(qf, qpe, kvf, kpe, kvf, do, o, lse)


def _mla_bwd_kv(qn16, qpe, kvf, kpe, do16, lse_row, delta_row, nh, name):
    t = qn16.shape[0]
    tq, nq = _mla_tiles(t)
    scale = (NOPE_DIM + ROPE_DIM) ** -0.5

    def body(kn_ref, kp_ref, v_ref, qn_ref, qp_ref, do_ref, lse_ref, dl_ref, dkn_ref, dkp_ref, dv_ref):
        j = pl.program_id(1)
        kn, kp, v = kn_ref[...], kp_ref[...], v_ref[...]

        def step(i, carry, masked):
            dkn, dkp, dv = carry
            qs = pl.ds(pl.multiple_of(i * tq, tq), tq)
            qn, qp, dob = qn_ref[qs, :], qp_ref[qs, :], do_ref[qs, :]
            st = (_dot(kn, qn, NT) + _dot(kp, qp, NT)) * scale
            pt = jnp.exp(st - lse_ref[0, :, qs])
            if masked:
                key = lax.broadcasted_iota(jnp.int32, (tq, tq), 0)
                qry = lax.broadcasted_iota(jnp.int32, (tq, tq), 1)
                pt = jnp.where(key <= qry, pt, 0.0)
            dv = dv + _dot(pt.astype(BF16), dob, NN)
            dpt = _dot(v, dob, NT)
            dst = (pt * (dpt - dl_ref[0, :, qs]) * scale).astype(BF16)
            return dkn + _dot(dst, qn, NN), dkp + _dot(dst, qp, NN), dv

        z = jnp.zeros((tq, LANE), F32)
        carry = step(j, (z, z, z), True)
        dkn, dkp, dv = lax.fori_loop(j + 1, nq, lambda i, c: step(i, c, False), carry)
        dkn_ref[...] = dkn
        dkp_ref[...] = dkp
        dv_ref[...] = dv

    blk = pl.BlockSpec((tq, LANE), lambda h, j: (j, h))
    res = lambda f: pl.BlockSpec((t, LANE), f)
    row = pl.BlockSpec((1, 1, t), lambda h, j: (h, 0, 0))
    return pl.pallas_call(
        body,
        name=name,
        grid=(nh, nq),
        in_specs=[
            blk,
            pl.BlockSpec((tq, LANE), lambda h, j: (j, 0)),
            pl.BlockSpec((tq, LANE), lambda h, j: (j, nh + h)),
            res(lambda h, j: (0, h)),
            res(lambda h, j: (0, h)),
            res(lambda h, j: (0, h)),
            row,
            row,
        ],
        out_specs=[blk, blk, blk],
        out_shape=[jax.ShapeDtypeStruct((t, nh * LANE), F32)] * 3,
        compiler_params=_params(("parallel", "parallel")),
    )(kvf, kpe, kvf, qn16, qpe, do16, lse_row, delta_row)


def _alibi_slopes(n):
    return [float(2.0 ** (-8.0 * i / n)) for i in range(1, n + 1)]


def _band_masks(has_prev):
    qi = lax.broadcasted_iota(jnp.int32, (BLK, BLK), 0)
    ki = lax.broadcasted_iota(jnp.int32, (BLK, BLK), 1)
    return ki <= qi, (ki >= qi) & has_prev, (qi - ki).astype(F32), (qi - ki + BLK).astype(F32)


def _band_fwd(zc, nh, dil, name):
    t = zc.shape[0]
    nb = t // BLK
    nbc = nb // dil
    scale = DIL_HEAD_DIM ** -0.5
    slopes = _alibi_slopes(nh)
    dw = nh * LANE

    def body(q_ref, kc_ref, kp_ref, vc_ref, vp_ref, o_ref, l_ref):
        b = pl.program_id(0)
        mask_c, mask_p, dist_c, dist_p = _band_masks((b % nbc) > 0)
        for h in range(nh):
            sl = slice(h * LANE, (h + 1) * LANE)
            q = q_ref[:, sl]
            bias = slopes[h] * dil
            sc = jnp.where(mask_c, _dot(q, kc_ref[:, sl], NT) * scale - bias * dist_c, NEG)
            sp = jnp.where(mask_p, _dot(q, kp_ref[:, sl], NT) * scale - bias * dist_p, NEG)
            m = jnp.maximum(jnp.max(sc, -1, keepdims=True), jnp.max(sp, -1, keepdims=True))
            ec, ep = jnp.exp(sc - m), jnp.exp(sp - m)
            l = jnp.sum(ec, -1, keepdims=True) + jnp.sum(ep, -1, keepdims=True)
            inv = 1.0 / l
            o = _dot((ec * inv).astype(BF16), vc_ref[:, sl], NN) + _dot((ep * inv).astype(BF16), vp_ref[:, sl], NN)
            o_ref[:, sl] = o
            l_ref[:, sl] = jnp.broadcast_to(m + jnp.log(l), (BLK, LANE))

    prev = lambda b: jnp.maximum(b - 1, 0)
    return pl.pallas_call(
        body,
        name=name,
        grid=(nb,),
        in_specs=[
            pl.BlockSpec((BLK, dw), lambda b: (b, 0)),
            pl.BlockSpec((BLK, dw), lambda b: (b, 1)),
            pl.BlockSpec((BLK, dw), lambda b: (prev(b), 1)),
            pl.BlockSpec((BLK, dw), lambda b: (b, 2)),
            pl.BlockSpec((BLK, dw), lambda b: (prev(b), 2)),
        ],
        out_specs=[pl.BlockSpec((BLK, dw), lambda b: (b, 0))] * 2,
        out_shape=[jax.ShapeDtypeStruct((t, dw), F32)] * 2,
        compiler_params=_params(("parallel",)),
    )(zc, zc, zc, zc, zc)


def _band_bwd(zc, o, lse, do, dl, nh, dil, name):
    t = zc.shape[0]
    nb = t // BLK
    nbc = nb // dil
    scale = DIL_HEAD_DIM ** -0.5
    slopes = _alibi_slopes(nh)
    dw = nh * LANE

    def body(q_ref, k_ref, v_ref, kp_ref, vp_ref, qn_ref, o_ref, l_ref, do_ref, dl_ref, on_ref, ln_ref, don_ref, dln_ref,
             dq_ref, dk_ref, dv_ref):
        b = pl.program_id(0)
        has_prev = (b % nbc) > 0
        has_next = jnp.logical_and(((b + 1) % nbc) > 0, b + 1 < nb)
        mask_c, mask_p, dist_c, dist_p = _band_masks(has_prev)
        _, mask_n, _, _ = _band_masks(has_next)
        for h in range(nh):
            sl = slice(h * LANE, (h + 1) * LANE)
            bias = slopes[h] * dil
            q, k, v = q_ref[:, sl], k_ref[:, sl], v_ref[:, sl]
            kp, vp, qn = kp_ref[:, sl], vp_ref[:, sl], qn_ref[:, sl]
            dof, donf = do_ref[:, sl], don_ref[:, sl]
            dob, donb = dof.astype(BF16), donf.astype(BF16)
            lse1, lsen1 = l_ref[:, sl][:, :1], ln_ref[:, sl][:, :1]
            adj = jnp.sum(dl_ref[:, sl] - dof * o_ref[:, sl], -1, keepdims=True)
            adjn = jnp.sum(dln_ref[:, sl] - donf * on_ref[:, sl], -1, keepdims=True)
            pa = jnp.where(mask_c, jnp.exp(_dot(q, k, NT) * scale - bias * dist_c - lse1), 0.0)
            pb = jnp.where(mask_p, jnp.exp(_dot(q, kp, NT) * scale - bias * dist_p - lse1), 0.0)
            dsa = (pa * (_dot(dob, v, NT) + adj)).astype(BF16)
            dsb = (pb * (_dot(dob, vp, NT) + adj)).astype(BF16)
            dq_ref[:, sl] = (_dot(dsa, k, NN) + _dot(dsb, kp, NN)) * scale
            pc = jnp.where(mask_n, jnp.exp(_dot(qn, k, NT) * scale - bias * dist_p - lsen1), 0.0)
            dsc = (pc * (_dot(donb, v, NT) + adjn)).astype(BF16)
            dk_ref[:, sl] = (_dot(dsa, q, TN) + _dot(dsc, qn, TN)) * scale
            dv_ref[:, sl] = _dot(pa.astype(BF16), dob, TN) + _dot(pc.astype(BF16), donb, TN)

    prev = lambda b: jnp.maximum(b - 1, 0)
    nxt = lambda b: jnp.minimum(b + 1, nb - 1)
    spec = lambda f: pl.BlockSpec((BLK, dw), f)
    cur0, cur1, cur2 = spec(lambda b: (b, 0)), spec(lambda b: (b, 1)), spec(lambda b: (b, 2))
    nxt0 = spec(lambda b: (nxt(b), 0))
    return pl.pallas_call(
        body,
        name=name,
        grid=(nb,),
        in_specs=[cur0, cur1, cur2, spec(lambda b: (prev(b), 1)), spec(lambda b: (prev(b), 2)), nxt0,
                  cur0, cur0, cur0, cur0, nxt0, nxt0, nxt0, nxt0],
        out_specs=[cur0] * 3,
        out_shape=[jax.ShapeDtypeStruct((t, dw), F32)] * 3,
        compiler_params=_params(("parallel",)),
    )(zc, zc, zc, zc, zc, zc, o, lse, do, dl, o, lse, do, dl)


def _s5_ops(a_re, a_im, ldt, bt_re, bt_im, c_re, c_im):
    L, g, p = S5_CHUNK, S5_GROUP, S5_STATE
    dt = jnp.exp(ldt)
    lr, li = a_re * dt, a_im * dt
    er = jnp.exp(lr)
    lam_re, lam_im = er * jnp.cos(li), er * jnp.sin(li)
    nr, ni = lam_re - 1.0, lam_im
    den = a_re * a_re + a_im * a_im
    fr, fi = (nr * a_re + ni * a_im) / den, (ni * a_re - nr * a_im) / den
    bb_re, bb_im = fr * bt_re - fi * bt_im, fr * bt_im + fi * bt_re

    def power(tau):
        mag = jnp.exp(tau * lr)
        return mag * jnp.cos(tau * li), mag * jnp.sin(tau * li)

    step = lax.broadcasted_iota(jnp.int32, (L, 1), 0).astype(F32)

    def outer(pr, pi, mr, mi):
        re = pr[:, None, :] * mr[None] - pi[:, None, :] * mi[None]
        im = pr[:, None, :] * mi[None] + pi[:, None, :] * mr[None]
        return re.reshape(L * g, p), im.reshape(L * g, p)

    half = float(L // 2)
    cp_re, cp_im = outer(*power(step - half), c_re, c_im)
    pb_re, pb_im = outer(*power(half - step), bb_re, bb_im)
    toep = _dot(cp_re, pb_re, NT, HI) - _dot(cp_im, pb_im, NT, HI)
    trow = lax.broadcasted_iota(jnp.int32, (L * g, L * g), 0) // g
    scol = lax.broadcasted_iota(jnp.int32, (L * g, L * g), 1) // g
    toep = jnp.where(trow >= scol, toep, 0.0)
    et_re, et_im = outer(*power(float(L - 1) - step), bb_re, bb_im)
    f_re, f_im = outer(*power(step + 1.0), c_re, c_im)
    big_re, big_im = power(jnp.full((1, 1), float(L), F32))
    return toep, et_re, et_im, f_re, f_im, big_re, big_im


def _s5_y(ops, u, sp_re, sp_im):
    toep, _, _, f_re, f_im, _, _ = ops
    return _dot(u, toep, NT, HI) + _dot(sp_re, f_re, NT, HI) - _dot(sp_im, f_im, NT, HI)


def _s5_group_specs(ng):
    vec = pl.BlockSpec((1, 1, S5_STATE), lambda g: (g, 0, 0))
    one = pl.BlockSpec((1, 1, 1), lambda g: (g, 0, 0))
    mat = pl.BlockSpec((1, S5_GROUP, S5_STATE), lambda g: (g, 0, 0))
    return [vec, vec, one, mat, mat, mat, mat]


def _s5_load(refs):
    return [r[0] for r in refs]


def _s5_local(prm, u, name):
    ng, n, w = u.shape
    p = S5_STATE

    def body(*refs):
        ops = _s5_ops(*_s5_load(refs[:7]))
        uu = refs[7][0]
        refs[8][0] = _dot(uu, ops[1], NN, HI)
        refs[9][0] = _dot(uu, ops[2], NN, HI)
        refs[10][0] = ops[5]
        refs[11][0] = ops[6]

    blk = lambda a, b: pl.BlockSpec((1, a, b), lambda g: (g, 0, 0))
    return pl.pallas_call(
        body,
        name=name,
        grid=(ng,),
        in_specs=_s5_group_specs(ng) + [blk(n, w)],
        out_specs=[blk(n, p), blk(n, p), blk(1, p), blk(1, p)],
        out_shape=[jax.ShapeDtypeStruct((ng, n, p), F32)] * 2 + [jax.ShapeDtypeStruct((ng, 1, p), F32)] * 2,
        compiler_params=_params(("parallel",)),
    )(*prm, u)


def _s5_carry(e_re, e_im, lam_re, lam_im, name):
    n, w = e_re.shape
    cw = _tile(w, (1024, 512, 256, 128))

    def body(er_ref, ei_ref, lr_ref, li_ref, sr_ref, si_ref):
        lr, li = lr_ref[...], li_ref[...]

        def step(k, carry):
            sr, si = carry
            row = pl.ds(k, 1)
            sr_ref[row, :] = sr
            si_ref[row, :] = si
            return lr * sr - li * si + er_ref[row, :], li * sr + lr * si + ei_ref[row, :]

        z = jnp.zeros((1, cw), F32)
        lax.fori_loop(0, n, step, (z, z))

    col = pl.BlockSpec((n, cw), lambda j: (0, j))
    one = pl.BlockSpec((1, cw), lambda j: (0, j))
    return pl.pallas_call(
        body,
        name=name,
        grid=(w // cw,),
        in_specs=[col, col, one, one],
        out_specs=[col, col],
        out_shape=[jax.ShapeDtypeStruct((n, w), F32)] * 2,
        compiler_params=_params(("parallel",)),
    )(e_re, e_im, lam_re, lam_im)


def _s5_carry_bwd(dsp_re, dsp_im, sp_re, sp_im, lam_re, lam_im, name):
    n, w = dsp_re.shape
    cw = _tile(w, (1024, 512, 256, 128))

    def body(dr_ref, di_ref, sr_ref, si_ref, lr_ref, li_ref, gr_ref, gi_ref, dlr_ref, dli_ref):
        lr, li = lr_ref[...], li_ref[...]

        def step(q, carry):
            gr_next, gi_next, dr_next, di_next, alr, ali = carry
            k = n - 1 - q
            row = pl.ds(k, 1)
            gr = dr_next + lr * gr_next + li * gi_next
            gi = di_next - li * gr_next + lr * gi_next
            gr_ref[row, :] = gr
            gi_ref[row, :] = gi
            sr, si = sr_ref[row, :], si_ref[row, :]
            return gr, gi, dr_ref[row, :], di_ref[row, :], alr + gr * sr + gi * si, ali + gi * sr - gr * si

        z = jnp.zeros((1, cw), F32)
        out = lax.fori_loop(0, n, step, (z, z, z, z, z, z))
        dlr_ref[...] = out[4]
        dli_ref[...] = out[5]

    col = pl.BlockSpec((n, cw), lambda j: (0, j))
    one = pl.BlockSpec((1, cw), lambda j: (0, j))
    return pl.pallas_call(
        body,
        name=name,
        grid=(w // cw,),
        in_specs=[col, col, col, col, one, one],
        out_specs=[col, col, one, one],
        out_shape=[jax.ShapeDtypeStruct((n, w), F32)] * 2 + [jax.ShapeDtypeStruct((1, w), F32)] * 2,
        compiler_params=_params(("parallel",)),
    )(dsp_re, dsp_im, sp_re, sp_im, lam_re, lam_im)


def _s5_out(prm, u, sp_re, sp_im, name):
    ng, n, w = u.shape
    p = S5_STATE

    def body(*refs):
        ops = _s5_ops(*_s5_load(refs[:7]))
        refs[10][0] = _s5_y(ops, refs[7][0], refs[8][0], refs[9][0])

    blk = lambda a, b: pl.BlockSpec((1, a, b), lambda g: (g, 0, 0))
    return pl.pallas_call(
        body,
        name=name,
        grid=(ng,),
        in_specs=_s5_group_specs(ng) + [blk(n, w), blk(n, p), blk(n, p)],
        out_specs=blk(n, w),
        out_shape=jax.ShapeDtypeStruct((ng, n, w), F32),
        compiler_params=_params(("parallel",)),
    )(*prm, u, sp_re, sp_im)


def _s5_bwd_state(prm, dy, name):
    ng, n, w = dy.shape
    p = S5_STATE

    def body(*refs):
        ops = _s5_ops(*_s5_load(refs[:7]))
        d = refs[7][0]
        refs[8][0] = _dot(d, ops[3], NN, HI)
        refs[9][0] = -_dot(d, ops[4], NN, HI)

    blk = lambda a, b: pl.BlockSpec((1, a, b), lambda g: (g, 0, 0))
    return pl.pallas_call(
        body,
        name=name,
        grid=(ng,),
        in_specs=_s5_group_specs(ng) + [blk(n, w)],
        out_specs=[blk(n, p), blk(n, p)],
        out_shape=[jax.ShapeDtypeStruct((ng, n, p), F32)] * 2,
        compiler_params=_params(("parallel",)),
    )(*prm, dy)


def _s5_bwd_main(prm, u, sp_re, sp_im, dy, g_re, g_im, dlam_re, dlam_im, name):
    ng, n, w = u.shape
    p = S5_STATE

    def body(*refs):
        prm_v = _s5_load(refs[:7])
        uu, sr, si, d, gr, gi, dlr, dli = [r[0] for r in refs[7:15]]

        def phi(*args):
            ops = _s5_ops(*args[:7])
            y = _s5_y(ops, args[7], sr, si)
            e_re, e_im = _dot(args[7], ops[1], NN, HI), _dot(args[7], ops[2], NN, HI)
            return (jnp.sum(d * y) + jnp.sum(gr * e_re) + jnp.sum(gi * e_im)
                    + jnp.sum(dlr * ops[5]) + jnp.sum(dli * ops[6]))

        grads = jax.grad(phi, argnums=tuple(range(8)))(*prm_v, uu)
        for q in range(8):
            refs[15 + q][0] = grads[q]

    blk = lambda a, b: pl.BlockSpec((1, a, b), lambda g: (g, 0, 0))
    prm_specs = _s5_group_specs(ng)
    return pl.pallas_call(
        body,
        name=name,
        grid=(ng,),
        in_specs=prm_specs + [blk(n, w), blk(n, p), blk(n, p), blk(n, w), blk(n, p), blk(n, p), blk(1, p), blk(1, p)],
        out_specs=prm_specs + [blk(n, w)],
        out_shape=[jax.ShapeDtypeStruct(a.shape, F32) for a in prm] + [jax.ShapeDtypeStruct((ng, n, w), F32)],
        compiler_params=_params(("parallel",)),
    )(*prm, u, sp_re, sp_im, dy, g_re, g_im, dlam_re, dlam_im)


def _to_groups(h):
    t, d = h.shape
    ng, n = d // S5_GROUP, t // S5_CHUNK
    return h.reshape(n, S5_CHUNK, ng, S5_GROUP).transpose(2, 0, 1, 3).reshape(ng, n, S5_CHUNK * S5_GROUP)


def _from_groups(y, t, d):
    ng, n = d // S5_GROUP, t // S5_CHUNK
    return y.reshape(ng, n, S5_CHUNK, S5_GROUP).transpose(1, 2, 0, 3).reshape(t, d)


def _states_to_cols(e):
    ng, n, p = e.shape
    return e.transpose(1, 0, 2).reshape(n, ng * p)


def _cols_to_states(s, ng):
    n = s.shape[0]
    return s.reshape(n, ng, S5_STATE).transpose(1, 0, 2)


def _me():
    return lax.axis_index("x"), lax.axis_index("y"), lax.axis_index("c")


def _flip(v, bit):
    return 1 - v if bit else v


def _all_gather(x, name):
    r, c = x.shape

    def body(x_ref, out_ref, send_sems, recv_sems, local_sem):
        mx, my, mc = _me()
        me, sibling = (mx, my, mc), (mx, my, 1 - mc)
        chips = [(1 - mx, my), (mx, 1 - my), (1 - mx, 1 - my)]

        def slot(px, py, pc):
            return out_ref.at[4 * px + 2 * py + pc]

        def copy(k, block, to, src=None):
            return pltpu.make_async_remote_copy(
                src_ref=slot(*block) if src is None else src,
                dst_ref=slot(*block),
                send_sem=send_sems.at[k],
                recv_sem=recv_sems.at[k],
                device_id=to,
                device_id_type=pl.DeviceIdType.MESH,
            )

        mine = pltpu.make_async_copy(x_ref, slot(*me), local_sem)
        mine.start()
        first = [copy(0, me, sibling, src=x_ref)]
        first += [copy(1 + j, me, (*chip, mc), src=x_ref) for j, chip in enumerate(chips)]
        for cp in first:
            cp.start()
        passed = [copy(4 + j, (*chip, mc), sibling) for j, chip in enumerate(chips)]
        for j, chip in enumerate(chips):
            copy(1 + j, (*chip, mc), me).wait_recv()
            passed[j].start()
        copy(0, sibling, me).wait_recv()
        for j, chip in enumerate(chips):
            copy(4 + j, (*chip, 1 - mc), me).wait_recv()
        for cp in first + passed:
            cp.wait_send()
        mine.wait()

    return pl.pallas_call(
        body,
        name=name,
        in_specs=[pl.BlockSpec(memory_space=pl.ANY)],
        out_specs=pl.BlockSpec(memory_space=pl.ANY),
        out_shape=jax.ShapeDtypeStruct((N_DEV, r, c), x.dtype),
        scratch_shapes=[pltpu.SemaphoreType.DMA((7,)), pltpu.SemaphoreType.DMA((7,)), pltpu.SemaphoreType.DMA],
    )(x)


def _all_to_all(g, name):
    _, r, c = g.shape

    def body(g_ref, out_ref, send_sems, recv_sems, local_sem):
        mx, my, mc = _me()
        mine_idx = 4 * mx + 2 * my + mc
        own = pltpu.make_async_copy(g_ref.at[mine_idx], out_ref.at[mine_idx], local_sem)
        own.start()
        copies = []
        for m in range(1, N_DEV):
            px, py, pc = _flip(mx, m & 4), _flip(my, m & 2), _flip(mc, m & 1)
            copies.append(
                pltpu.make_async_remote_copy(
                    src_ref=g_ref.at[4 * px + 2 * py + pc],
                    dst_ref=out_ref.at[mine_idx],
                    send_sem=send_sems.at[m - 1],
                    recv_sem=recv_sems.at[m - 1],
                    device_id=(px, py, pc),
                    device_id_type=pl.DeviceIdType.MESH,
                )
            )
        for cp in copies:
            cp.start()
        for cp in copies:
            cp.wait_recv()
        for cp in copies:
            cp.wait_send()
        own.wait()

    return pl.pallas_call(
        body,
        name=name,
        in_specs=[pl.BlockSpec(memory_space=pl.ANY)],
        out_specs=pl.BlockSpec(memory_space=pl.ANY),
        out_shape=jax.ShapeDtypeStruct(g.shape, g.dtype),
        scratch_shapes=[pltpu.SemaphoreType.DMA((7,)), pltpu.SemaphoreType.DMA((7,)), pltpu.SemaphoreType.DMA],
    )(g)


def _sum_slots(recv, name):
    _, r, c = recv.shape
    tr = _tile(r, (256, 128, 64, 32, 16, 8))

    def body(r_ref, o_ref):
        acc = r_ref[0]
        for k in range(1, N_DEV):
            acc = acc + r_ref[k]
        o_ref[...] = acc

    return pl.pallas_call(
        body,
        name=name,
        grid=(r // tr,),
        in_specs=[pl.BlockSpec((N_DEV, tr, c), lambda i: (0, i, 0))],
        out_specs=pl.BlockSpec((tr, c), lambda i: (i, 0)),
        out_shape=jax.ShapeDtypeStruct((r, c), F32),
        compiler_params=_params(("parallel",)),
    )(recv)


def _adamw(w, g, m, v, name):
    shape = w.shape
    c = shape[-1]
    as2d = lambda a: a.reshape(-1, c)
    w2, g2, m2, v2 = as2d(w), as2d(g), as2d(m), as2d(v)
    r = w2.shape[0]
    tr = _tile(r, (256, 128, 64, 32, 16, 8))
    c1 = 1.0 / (1.0 - ADAM_B1 ** ADAM_STEP)
    c2 = 1.0 / (1.0 - ADAM_B2 ** ADAM_STEP)

    def body(w_ref, g_ref, m_ref, v_ref, d_ref, mo_ref, vo_ref):
        gg = g_ref[...]
        mn = ADAM_B1 * m_ref[...] + (1.0 - ADAM_B1) * gg
        vn = ADAM_B2 * v_ref[...] + (1.0 - ADAM_B2) * (gg * gg)
        d_ref[...] = -ADAM_LR * ((mn * c1) / (jnp.sqrt(vn * c2) + ADAM_EPS) + ADAM_WD * w_ref[...])
        mo_ref[...] = mn
        vo_ref[...] = vn

    spec = pl.BlockSpec((tr, c), lambda i: (i, 0))
    d, mn, vn = pl.pallas_call(
        body,
        name=name,
        grid=(r // tr,),
        in_specs=[spec] * 4,
        out_specs=[spec] * 3,
        out_shape=[jax.ShapeDtypeStruct((r, c), F32)] * 3,
        compiler_params=_params(("parallel",)),
    )(w2, g2, m2, v2)
    return d.reshape(shape), mn.reshape(shape), vn.reshape(shape)


COL_SHARDED = ("attn_w_in", "mla_w_q_b", "mla_w_kv_b", "s5_d", "s5_w_glu", "ffn_w_in", "ple_w")
ROW_SHARDED = ("attn_w_out", "ffn_w_out", "ple_gate_w")
REPLICATED = ("mla_q_norm", "mla_kv_norm", "s5_a_re", "s5_a_im", "s5_log_dt", "s5_b_re", "s5_b_im", "s5_c_re",
              "s5_c_im", "ln1_g", "ln1_b", "ln2_g", "ln2_b")
WEIGHTS = ("attn_w_in", "mla_q_norm", "mla_w_q_b", "mla_kv_norm", "mla_w_kv_b", "attn_w_out", "s5_a_re", "s5_a_im",
           "s5_log_dt", "s5_b_re", "s5_b_im", "s5_c_re", "s5_c_im", "s5_d", "s5_w_glu", "ln1_g", "ln1_b", "ffn_w_in",
           "ffn_w_out", "ple_w", "ple_gate_w", "ln2_g", "ln2_b")


def _unshard(name, gathered):
    if name in COL_SHARDED:
        full = jnp.moveaxis(gathered, 0, -2)
        return full.reshape(full.shape[:-2] + (full.shape[-2] * full.shape[-1],))
    full = jnp.moveaxis(gathered, 0, 1)
    return full.reshape((full.shape[0], full.shape[1] * full.shape[2]) + full.shape[3:])


def _by_owner(name, grad):
    if name in COL_SHARDED:
        g = grad.reshape(grad.shape[:-1] + (N_DEV, grad.shape[-1] // N_DEV))
        return jnp.moveaxis(g, -2, 0).reshape(N_DEV, -1)
    if name in ROW_SHARDED:
        g = grad.reshape((grad.shape[0], N_DEV, grad.shape[1] // N_DEV) + grad.shape[2:])
        return jnp.moveaxis(g, 1, 0).reshape(N_DEV, -1)
    return jnp.broadcast_to(grad.reshape(1, -1), (N_DEV, grad.size))


def _pack(pieces, axis, dtype, row_mult):
    flat = jnp.concatenate(pieces, axis=axis)
    n = flat.shape[axis]
    unit = PACK_COLS * row_mult
    pad = (-n) % unit
    if pad:
        widths = [(0, 0)] * flat.ndim
        widths[axis] = (0, pad)
        flat = jnp.pad(flat, widths)
    return flat.astype(dtype).reshape(flat.shape[:axis] + ((n + pad) // PACK_COLS, PACK_COLS))


def _ffn_block(h, p_i, w_in, w_out, w_ple, w_pg, g, b, alpha, tag):
    t, d = h.shape
    hid = w_out.shape[0]
    cw = _tile(hid, (512, 256, 128))
    ncb = hid // cw
    gu = _mm(h, w_in, name=f"ffn_in_{tag}")
    act = _rowwise(_swiglu_fn, [(gu, cw, 0), (gu, cw, ncb)], [], [(hid, cw, BF16)], name=f"swiglu_{tag}", tq=256, ncol=ncb)[0]
    f = _mm(act, w_out, name=f"ffn_out_{tag}")
    pw = _mm(p_i, w_ple, name=f"ple_{tag}")
    gate = _mm(h, w_pg, name=f"ple_gate_{tag}")
    out = _rowwise(_ln_ffn_fn(alpha), [_whole(h), _whole(f), _whole(pw), _whole(gate)], [g, b], [(d, d, F32)],
                   name=f"ln2_{tag}", tq=256)[0]
    return out, (h, p_i, gu, act, f, pw, gate)


def _ffn_block_bwd(saved, dout, w_in, w_out, w_pg, g, b, alpha, tag):
    h, p_i, gu, act, f, pw, gate = saved
    t, d = h.shape
    hid = w_out.shape[0]
    cw = _tile(hid, (512, 256, 128))
    ncb = hid // cw
    dh_a, df, dpw, dgate, dg, db = _rowwise_bwd(
        _ln_ffn_fn(alpha), [_whole(h), _whole(f), _whole(pw), _whole(gate)], [g, b], [dout],
        need=[True] * 4, drow=[(d, F32), (d, BF16), (d, BF16), (d, BF16)], name=f"ln2_bwd_{tag}", tq=128)
    dw_out = _mm(act, df, ta=True, name=f"ffn_out_dw_{tag}")
    dact = _mm(df, w_out, tb=True, out_dtype=BF16, name=f"ffn_out_dx_{tag}")
    dg_, du_ = _rowwise_bwd(_swiglu_fn, [(gu, cw, 0), (gu, cw, ncb)], [], [[(dact, cw, 0)]], need=[True, True],
                            drow=[(hid, BF16), (hid, BF16)], name=f"swiglu_bwd_{tag}", tq=256, ncol=ncb)
    dgu = jnp.concatenate([dg_, du_], axis=1)
    dw_in = _mm(h, dgu, ta=True, name=f"ffn_in_dw_{tag}")
    dh_b = _mm(dgu, w_in, tb=True, name=f"ffn_in_dx_{tag}")
    dw_ple = _mm(p_i, dpw, ta=True, name=f"ple_dw_{tag}")
    dw_pg = _mm(h, dgate, ta=True, name=f"ple_gate_dw_{tag}")
    dh_c = _mm(dgate, w_pg, tb=True, name=f"ple_gate_dx_{tag}")
    return [dh_a, dh_b, dh_c], dict(ffn_w_in=dw_in, ffn_w_out=dw_out, ple_w=dw_ple, ple_gate_w=dw_pg, ln2_g=dg, ln2_b=db)


def kernel(x, p, positions, attn_w_in, mla_q_norm, mla_w_q_b, mla_kv_norm, mla_w_kv_b, attn_w_out, s5_a_re, s5_a_im, s5_log_dt, s5_b_re, s5_b_im, s5_c_re, s5_c_im, s5_d, s5_w_glu, ln1_g, ln1_b, ffn_w_in, ffn_w_out, ple_w, ple_gate_w, ln2_g, ln2_b, loss_target, m_attn_w_in, m_mla_q_norm, m_mla_w_q_b, m_mla_kv_norm, m_mla_w_kv_b, m_attn_w_out, m_s5_a_re, m_s5_a_im, m_s5_log_dt, m_s5_b_re, m_s5_b_im, m_s5_c_re, m_s5_c_im, m_s5_d, m_s5_w_glu, m_ln1_g, m_ln1_b, m_ffn_w_in, m_ffn_w_out, m_ple_w, m_ple_gate_w, m_ln2_g, m_ln2_b, v_attn_w_in, v_mla_q_norm, v_mla_w_q_b, v_mla_kv_norm, v_mla_w_kv_b, v_attn_w_out, v_s5_a_re, v_s5_a_im, v_s5_log_dt, v_s5_b_re, v_s5_b_im, v_s5_c_re, v_s5_c_im, v_s5_d, v_s5_w_glu, v_ln1_g, v_ln1_b, v_ffn_w_in, v_ffn_w_out, v_ple_w, v_ple_gate_w, v_ln2_g, v_ln2_b):
    local = dict(attn_w_in=attn_w_in, mla_q_norm=mla_q_norm, mla_w_q_b=mla_w_q_b, mla_kv_norm=mla_kv_norm,
                 mla_w_kv_b=mla_w_kv_b, attn_w_out=attn_w_out, s5_a_re=s5_a_re, s5_a_im=s5_a_im, s5_log_dt=s5_log_dt,
                 s5_b_re=s5_b_re, s5_b_im=s5_b_im, s5_c_re=s5_c_re, s5_c_im=s5_c_im, s5_d=s5_d, s5_w_glu=s5_w_glu,
                 ln1_g=ln1_g, ln1_b=ln1_b, ffn_w_in=ffn_w_in, ffn_w_out=ffn_w_out, ple_w=ple_w, ple_gate_w=ple_gate_w,
                 ln2_g=ln2_g, ln2_b=ln2_b)
    mom_m = dict(zip(WEIGHTS, (m_attn_w_in, m_mla_q_norm, m_mla_w_q_b, m_mla_kv_norm, m_mla_w_kv_b, m_attn_w_out, m_s5_a_re, m_s5_a_im, m_s5_log_dt, m_s5_b_re, m_s5_b_im, m_s5_c_re, m_s5_c_im, m_s5_d, m_s5_w_glu, m_ln1_g, m_ln1_b, m_ffn_w_in, m_ffn_w_out, m_ple_w, m_ple_gate_w, m_ln2_g, m_ln2_b)))
    mom_v = dict(zip(WEIGHTS, (v_attn_w_in, v_mla_q_norm, v_mla_w_q_b, v_mla_kv_norm, v_mla_w_kv_b, v_attn_w_out, v_s5_a_re, v_s5_a_im, v_s5_log_dt, v_s5_b_re, v_s5_b_im, v_s5_c_re, v_s5_c_im, v_s5_d, v_s5_w_glu, v_ln1_g, v_ln1_b, v_ffn_w_in, v_ffn_w_out, v_ple_w, v_ple_gate_w, v_ln2_g, v_ln2_b)))

    t, d = x.shape[1], x.shape[2]
    depth = ln1_g.shape[0]
    alpha = (2.0 * depth) ** 0.25
    ql, kvl = mla_q_norm.shape[-1], mla_kv_norm.shape[-1]
    nh = mla_w_q_b.shape[-1] * N_DEV // (NOPE_DIM + ROPE_DIM)
    dw = (attn_w_in.shape[-1] * N_DEV - ql - kvl - ROPE_DIM) // 3
    ndh = dw // DIL_HEAD_DIM
    ng = d // S5_GROUP
    assert ql == kvl and (ql + kvl) % LANE == 0 and nh * V_DIM == dw

    sharded = [n for n in WEIGHTS if n in COL_SHARDED or n in ROW_SHARDED]
    as_words = lambda n: (lax.bitcast_convert_type(local[n], BF16) if n == "s5_d" else local[n].astype(BF16)).reshape(-1)
    packed = _pack([as_words(n) for n in sharded], 0, BF16, 16)
    gathered = _all_gather(packed, "gather_weights").reshape(N_DEV, -1)
    full, off = {}, 0
    for n in sharded:
        size = local[n].size * (2 if n == "s5_d" else 1)
        piece = gathered[:, off : off + size]
        off += size
        if n == "s5_d":
            piece = lax.bitcast_convert_type(piece.reshape((N_DEV,) + local[n].shape + (2,)), F32)
        full[n] = _unshard(n, piece.reshape((N_DEV,) + local[n].shape))

    w_in = full["attn_w_in"][0]
    lat = ql + kvl
    w0 = jnp.concatenate([w_in[:, : lat + ROPE_DIM], jnp.zeros((d, LANE - ROPE_DIM), BF16), w_in[:, lat + ROPE_DIM :]], axis=1)
    wq = full["mla_w_q_b"][0].reshape(ql, nh, NOPE_DIM + ROPE_DIM)
    wq_pe = jnp.pad(wq[:, :, NOPE_DIM:], ((0, 0), (0, 0), (0, LANE - ROPE_DIM)))
    wqp = jnp.concatenate([wq[:, :, :NOPE_DIM].reshape(ql, nh * LANE), wq_pe.reshape(ql, nh * LANE)], axis=1)
    wkv = full["mla_w_kv_b"][0].reshape(kvl, nh, NOPE_DIM + V_DIM)
    wkvp = jnp.concatenate([wkv[:, :, :NOPE_DIM].reshape(kvl, nh * LANE), wkv[:, :, NOPE_DIM:].reshape(kvl, nh * LANE)], axis=1)
    w_out = full["attn_w_out"][0]
    w_glu = full["s5_w_glu"][0]
    d_skip = full["s5_d"]

    h0 = x[0]
    target = loss_target[0]
    pos = positions.reshape(t, 1)
    grads = {}

    z = _mm(h0, w0, name="attn_in")
    gq, gk = mla_q_norm.reshape(1, ql), mla_kv_norm.reshape(1, kvl)
    qn, kvn = _rowwise(_rms_fn, [(z, ql, 0), (z, kvl, 1)], [gq, gk], [(ql, ql, BF16), (kvl, kvl, BF16)], name="rms", tq=256)
    qf = _mm(qn, wqp, name="q_up")
    kvf = _mm(kvn, wkvp, out_dtype=BF16, name="kv_up")
    cos, sin = _rope_tables(pos, "rope_tables")
    pe_cb = lat // LANE
    qpe, kpe = _rowwise(_rope_fwd_fn(nh), [(qf, nh * LANE, 1), (z, LANE, pe_cb), _whole(cos), _whole(sin)], [],
                        [(nh * LANE, nh * LANE, BF16), (LANE, LANE, BF16)], name="rope", tq=256)
    out_a, lse_a = _mla_fwd(qf, qpe, kvf, kpe, nh, "mla_fwd")

    dil_cb = (lat + LANE)
    zd = z[:, dil_cb:].astype(BF16)

    def to_classes(a, dil):
        if dil == 1:
            return a
        return a.reshape(t // dil, dil, a.shape[1]).transpose(1, 0, 2).reshape(t, a.shape[1])

    def from_classes(a, dil):
        if dil == 1:
            return a
        return a.reshape(dil, t // dil, a.shape[1]).transpose(1, 0, 2).reshape(t, a.shape[1])

    band = []
    for window, dil in DIL_BRANCHES:
        assert window // dil == BLK
        zc = to_classes(zd, dil)
        o_c, l_c = _band_fwd(zc, ndh, dil, f"band_fwd_d{dil}")
        band.append((dil, zc, o_c, l_c, from_classes(o_c, dil), from_classes(l_c, dil)))
    merge_rows = [_whole(b[4]) for b in band] + [_whole(b[5]) for b in band]
    out_b = _rowwise(_merge_fn, merge_rows, [], [(dw, dw, BF16)], name="merge", tq=256)[0]
    att = jnp.concatenate([out_a, out_b], axis=1)
    mix0 = _mm(att, w_out, name="attn_out")
    g1, b1 = ln1_g[0:1], ln1_b[0:1]
    h1 = _rowwise(_ln_mix_fn(alpha), [_whole(h0), _whole(mix0)], [g1, b1], [(d, d, F32)], name="ln1_l0", tq=256)[0]
    h2, saved_f0 = _ffn_block(h1, p[0, 0], full["ffn_w_in"][0], full["ffn_w_out"][0], full["ple_w"][0],
                              full["ple_gate_w"][0], ln2_g[0:1], ln2_b[0:1], alpha, "l0")

    prm = [s5_a_re[0].reshape(ng, 1, S5_STATE), s5_a_im[0].reshape(ng, 1, S5_STATE), s5_log_dt[0].reshape(ng, 1, 1),
           s5_b_re[0].transpose(0, 2, 1), s5_b_im[0].transpose(0, 2, 1), s5_c_re[0], s5_c_im[0]]
    u = _to_groups(h2)
    e_re, e_im, lam_re, lam_im = _s5_local(prm, u, "s5_local")
    lam_re_c, lam_im_c = lam_re.reshape(1, ng * S5_STATE), lam_im.reshape(1, ng * S5_STATE)
    sp_re_c, sp_im_c = _s5_carry(_states_to_cols(e_re), _states_to_cols(e_im), lam_re_c, lam_im_c, "s5_carry")
    sp_re, sp_im = _cols_to_states(sp_re_c, ng), _cols_to_states(sp_im_c, ng)
    ys = _from_groups(_s5_out(prm, u, sp_re, sp_im, "s5_out"), t, d)
    z5 = _rowwise(_gelu_fn, [_whole(ys), _whole(h2)], [d_skip], [(d, d, BF16)], name="gelu", tq=256)[0]
    vg = _mm(z5, w_glu, name="glu_in")
    g3, b3 = ln1_g[1:2], ln1_b[1:2]
    h3 = _rowwise(_ln_glu_fn(alpha), [_whole(h2), (vg, d, 0), (vg, d, 1)], [g3, b3], [(d, d, F32)], name="ln1_l1", tq=256)[0]
    h4, saved_f1 = _ffn_block(h3, p[1, 0], full["ffn_w_in"][1], full["ffn_w_out"][1], full["ple_w"][1],
                              full["ple_gate_w"][1], ln2_g[1:2], ln2_b[1:2], alpha, "l1")

    dh4, loss_acc = _loss_kernel(h4, target, "loss")
    loss = lax.psum(loss_acc[0, 0], AXES)

    dh3, gf1 = _ffn_block_bwd(saved_f1, [_whole(dh4)], full["ffn_w_in"][1], full["ffn_w_out"][1],
                              full["ple_gate_w"][1], ln2_g[1:2], ln2_b[1:2], alpha, "l1")
    dh2_a, dval, dgate, dg3, db3 = _rowwise_bwd(
        _ln_glu_fn(alpha), [_whole(h2), (vg, d, 0), (vg, d, 1)], [g3, b3], [[_whole(a) for a in dh3]],
        need=[True] * 3, drow=[(d, F32), (d, BF16), (d, BF16)], name="ln1_bwd_l1", tq=128)
    dvg = jnp.concatenate([dval, dgate], axis=1)
    grads["s5_w_glu"] = _mm(z5, dvg, ta=True, name="glu_dw")[None]
    dz5 = _mm(dvg, w_glu, tb=True, name="glu_dx")
    dys, dh2_b, dd = _rowwise_bwd(_gelu_fn, [_whole(ys), _whole(h2)], [d_skip], [[_whole(dz5)]], need=[True, True],
                                  drow=[(d, F32), (d, F32)], name="gelu_bwd", tq=128)
    grads["s5_d"] = dd
    dy = _to_groups(dys)
    dsp_re, dsp_im = _s5_bwd_state(prm, dy, "s5_bwd_state")
    g_re_c, g_im_c, dlam_re_c, dlam_im_c = _s5_carry_bwd(_states_to_cols(dsp_re), _states_to_cols(dsp_im), sp_re_c, sp_im_c,
                                                         lam_re_c, lam_im_c, "s5_carry_bwd")
    s5g = _s5_bwd_main(prm, u, sp_re, sp_im, dy, _cols_to_states(g_re_c, ng), _cols_to_states(g_im_c, ng),
                       dlam_re_c.reshape(ng, 1, S5_STATE), dlam_im_c.reshape(ng, 1, S5_STATE), "s5_bwd_main")
    grads["s5_a_re"] = s5g[0].reshape(s5_a_re.shape)
    grads["s5_a_im"] = s5g[1].reshape(s5_a_im.shape)
    grads["s5_log_dt"] = s5g[2].reshape(s5_log_dt.shape)
    grads["s5_b_re"] = s5g[3].transpose(0, 2, 1)[None]
    grads["s5_b_im"] = s5g[4].transpose(0, 2, 1)[None]
    grads["s5_c_re"] = s5g[5][None]
    grads["s5_c_im"] = s5g[6][None]
    dh2_c = _from_groups(s5g[7], t, d)

    dh1, gf0 = _ffn_block_bwd(saved_f0, [_whole(dh2_a), _whole(dh2_b), _whole(dh2_c)], full["ffn_w_in"][0],
                              full["ffn_w_out"][0], full["ple_gate_w"][0], ln2_g[0:1], ln2_b[0:1], alpha, "l0")
    for k in ("ffn_w_in", "ffn_w_out", "ple_w", "ple_gate_w", "ln2_g", "ln2_b"):
        grads[k] = jnp.stack([gf0[k], gf1[k]]) if gf0[k].ndim == 2 and gf0[k].shape[0] != 1 else jnp.concatenate([gf0[k], gf1[k]])
    dh0_a, dmix, dg1, db1 = _rowwise_bwd(_ln_mix_fn(alpha), [_whole(h0), _whole(mix0)], [g1, b1], [[_whole(a) for a in dh1]],
                                         need=[True, True], drow=[(d, F32), (d, BF16)], name="ln1_bwd_l0", tq=128)
    grads["ln1_g"] = jnp.concatenate([dg1, dg3])
    grads["ln1_b"] = jnp.concatenate([db1, db3])
    grads["attn_w_out"] = _mm(att, dmix, ta=True, name="attn_out_dw")[None]
    datt = _mm(dmix, w_out, tb=True, name="attn_out_dx")

    dmerge = _rowwise_bwd(_merge_fn, merge_rows, [], [[(datt, dw, 1)]], need=[True] * 6, drow=[(dw, F32)] * 6,
                          name="merge_bwd", tq=128)
    dq_s = dk_s = dv_s = None
    for k, (dil, zc, o_c, l_c, _, _) in enumerate(band):
        do_c, dl_c = to_classes(dmerge[k], dil), to_classes(dmerge[3 + k], dil)
        dq_c, dk_c, dv_c = _band_bwd(zc, o_c, l_c, do_c, dl_c, ndh, dil, f"band_bwd_d{dil}")
        dq_n, dk_n, dv_n = from_classes(dq_c, dil), from_classes(dk_c, dil), from_classes(dv_c, dil)
        dq_s = dq_n if dq_s is None else dq_s + dq_n
        dk_s = dk_n if dk_s is None else dk_s + dk_n
        dv_s = dv_n if dv_s is None else dv_s + dv_n

    dqn, dqp, delta = _mla_bwd_q(qf, qpe, kvf, kpe, datt, out_a, lse_a, nh, "mla_bwd_q")
    to_row = lambda a: a[:, ::LANE].T.reshape(nh, 1, t)
    dkn, dkp, dv = _mla_bwd_kv(qf[:, : nh * LANE].astype(BF16), qpe, kvf, kpe, datt[:, : nh * LANE].astype(BF16),
                               to_row(lse_a), to_row(delta), nh, "mla_bwd_kv")
    dq_pe, dk_pe = _rowwise(_rope_bwd_fn(nh), [_whole(dqp), _whole(dkp), _whole(cos), _whole(sin)], [],
                            [(nh * LANE, nh * LANE, BF16), (LANE, LANE, BF16)], name="rope_bwd", tq=256)
    dqf = jnp.concatenate([dqn.astype(BF16), dq_pe], axis=1)
    dkvf = jnp.concatenate([dkn, dv], axis=1).astype(BF16)
    dwqp = _mm(qn, dqf, ta=True, name="q_up_dw")
    dqn_in = _mm(dqf, wqp, tb=True, name="q_up_dx")
    dwkvp = _mm(kvn, dkvf, ta=True, name="kv_up_dw")
    dkvn_in = _mm(dkvf, wkvp, tb=True, name="kv_up_dx")
    dql, dkvl, dgq, dgk = _rowwise_bwd(_rms_fn, [(z, ql, 0), (z, kvl, 1)], [gq, gk], [[_whole(dqn_in)], [_whole(dkvn_in)]],
                                       need=[True, True], drow=[(ql, BF16), (kvl, BF16)], name="rms_bwd", tq=256)
    grads["mla_q_norm"], grads["mla_kv_norm"] = dgq, dgk
    dz = jnp.concatenate([dql, dkvl, dk_pe, dq_s.astype(BF16), dk_s.astype(BF16), dv_s.astype(BF16)], axis=1)
    dw0 = _mm(h0, dz, ta=True, name="attn_in_dw")
    dh0_b = _mm(dz, w0, tb=True, name="attn_in_dx")
    grad_x = _add2(dh0_a, dh0_b, "grad_x")[None]
    grads["attn_w_in"] = jnp.concatenate([dw0[:, : lat + ROPE_DIM], dw0[:, lat + LANE :]], axis=1)[None]
    dwq_n = dwqp[:, : nh * LANE].reshape(ql, nh, NOPE_DIM)
    dwq_r = dwqp[:, nh * LANE :].reshape(ql, nh, LANE)[:, :, :ROPE_DIM]
    grads["mla_w_q_b"] = jnp.concatenate([dwq_n, dwq_r], axis=2).reshape(1, ql, nh * (NOPE_DIM + ROPE_DIM))
    dwkv_k = dwkvp[:, : nh * LANE].reshape(kvl, nh, NOPE_DIM)
    dwkv_v = dwkvp[:, nh * LANE :].reshape(kvl, nh, V_DIM)
    grads["mla_w_kv_b"] = jnp.concatenate([dwkv_k, dwkv_v], axis=2).reshape(1, kvl, nh * (NOPE_DIM + V_DIM))

    send = _pack([_by_owner(n, grads[n].astype(F32)) for n in WEIGHTS], 1, F32, 8)
    summed = _sum_slots(_all_to_all(send, "exchange_grads"), "sum_grads").reshape(-1)
    g_out, d_out, m_out, v_out, off = [], [], [], [], 0
    for n in WEIGHTS:
        size = local[n].size
        g = summed[off : off + size].reshape(local[n].shape)
        off += size
        dlt, mn, vn = _adamw(local[n], g, mom_m[n], mom_v[n], f"adamw_{n}")
        g_out.append(g)
        d_out.append(dlt)
        m_out.append(mn)
        v_out.append(vn)
    return (loss, grad_x, *g_out, *d_out, *m_out, *v_out)
```

```python
import functools
import math

import numpy as np
import jax
import jax.numpy as jnp
from jax import lax
from jax.experimental import pallas as pl
from jax.experimental.pallas import tpu as pltpu

F32 = jnp.float32
BF16 = jnp.bfloat16

NOPE_DIM, ROPE_DIM, V_DIM = 128, 64, 128
DIL_HEAD_DIM = 128
DIL_BRANCHES = ((128, 1), (512, 4), (2048, 16))
BLK = 128
LANE = 128
ROPE_THETA = 10000.0
S5_GROUP, S5_STATE = 16, 64
S5_CHUNK = 32
NEG = -1e30
ADAM_LR, ADAM_B1, ADAM_B2, ADAM_EPS, ADAM_WD, ADAM_STEP = 0.001, 0.9, 0.999, 1e-08, 0.01, 10
N_DEV = 8
AXES = ("x", "y", "c")
VMEM_LIMIT = 56 * 1024 * 1024
MM_VMEM_BUDGET = 40 * 1024 * 1024
PACK_COLS = 1024
HI = lax.Precision.HIGHEST


def _tile(n, cands):
    for c in cands:
        if n % c == 0:
            return c
    return n


def _params(sem=None):
    return pltpu.CompilerParams(dimension_semantics=sem, vmem_limit_bytes=VMEM_LIMIT)


def _dot(a, b, dims, prec=None):
    return lax.dot_general(a, b, (dims, ((), ())), preferred_element_type=F32, precision=prec)


NN = ((1,), (0,))
NT = ((1,), (1,))
TN = ((0,), (0,))


def _mm(a, b, *, ta=False, tb=False, out_dtype=F32, name):
    (k, m) = a.shape if ta else a.shape[::-1]
    (n, k2) = b.shape if tb else b.shape[::-1]
    assert k == k2, (a.shape, b.shape, ta, tb)
    tm = _tile(m, (1024, 512, 256, 128))
    tn = _tile(n, (1024, 512, 384, 256, 128))
    sa, sb, so = a.dtype.itemsize, b.dtype.itemsize, jnp.dtype(out_dtype).itemsize

    def fits(tk):
        return 2 * (tm * tk * sa + tk * tn * sb) + 2 * tm * tn * so + 4 * tm * tn <= MM_VMEM_BUDGET

    tk = next((c for c in (2048, 1536, 1408, 1152, 1024, 768, 512, 384, 256, 128) if k % c == 0 and fits(c)), k)
    nk = k // tk
    dims = (((0,) if ta else (1,)), ((1,) if tb else (0,)))

    def body(a_ref, b_ref, o_ref, *acc):
        part = _dot(a_ref[...].astype(BF16), b_ref[...].astype(BF16), dims)
        if nk == 1:
            o_ref[...] = part.astype(o_ref.dtype)
            return
        (acc_ref,) = acc
        kk = pl.program_id(2)

        @pl.when(kk == 0)
        def _():
            acc_ref[...] = part

        @pl.when(kk > 0)
        def _():
            acc_ref[...] += part

        @pl.when(kk == nk - 1)
        def _():
            o_ref[...] = acc_ref[...].astype(o_ref.dtype)

    a_spec = pl.BlockSpec((tk, tm), lambda i, j, kk: (kk, i)) if ta else pl.BlockSpec((tm, tk), lambda i, j, kk: (i, kk))
    b_spec = pl.BlockSpec((tn, tk), lambda i, j, kk: (j, kk)) if tb else pl.BlockSpec((tk, tn), lambda i, j, kk: (kk, j))
    return pl.pallas_call(
        body,
        name=name,
        grid=(m // tm, n // tn, nk),
        in_specs=[a_spec, b_spec],
        out_specs=pl.BlockSpec((tm, tn), lambda i, j, kk: (i, j)),
        out_shape=jax.ShapeDtypeStruct((m, n), out_dtype),
        scratch_shapes=[pltpu.VMEM((tm, tn), F32)] if nk > 1 else [],
        compiler_params=_params(("parallel", "parallel", "arbitrary")),
    )(a, b)


def _row_spec(spec, tq):
    _, w, cb = spec
    return pl.BlockSpec((tq, w), lambda i, j: (i, cb + j))


def _full_spec(p):
    return pl.BlockSpec(p.shape, lambda i, j: (0,) * p.ndim)


def _rowwise(fn, rows, pars, outs, *, name, tq, ncol=1):
    t = rows[0][0].shape[0]
    tq = _tile(t, (tq, 128, 64, 32, 16, 8))
    nr, npar = len(rows), len(pars)

    def body(*refs):
        vals = [r[...].astype(F32) for r in refs[: nr + npar]]
        res = fn(*vals)
        for o, r in zip(refs[nr + npar :], res):
            o[...] = r.astype(o.dtype)

    res = pl.pallas_call(
        body,
        name=name,
        grid=(t // tq, ncol),
        in_specs=[_row_spec(s, tq) for s in rows] + [_full_spec(p) for p in pars],
        out_specs=[pl.BlockSpec((tq, w), lambda i, j: (i, j)) for (_, w, _) in outs],
        out_shape=[jax.ShapeDtypeStruct((t, wt), dt) for (wt, _, dt) in outs],
        compiler_params=_params(("parallel", "parallel")),
    )(*[s[0] for s in rows], *pars)
    return res


def _rowwise_bwd(fn, rows, pars, cots, *, need, drow, name, tq, ncol=1):
    t = rows[0][0].shape[0]
    tq = _tile(t, (tq, 128, 64, 32, 16, 8))
    nr, npar = len(rows), len(pars)
    assert ncol == 1 or npar == 0
    flat_cots = [s for c in cots for s in c]
    ncot = len(flat_cots)
    want = [k for k in range(nr) if need[k]]

    def body(*refs):
        vals = [r[...].astype(F32) for r in refs[: nr + npar]]
        cref = refs[nr + npar : nr + npar + ncot]
        oref = refs[nr + npar + ncot :]
        cvals, pos = [], 0
        for c in cots:
            acc = cref[pos][...].astype(F32)
            for q in range(1, len(c)):
                acc = acc + cref[pos + q][...].astype(F32)
            cvals.append(acc)
            pos += len(c)

        def closed(*diff):
            full = list(vals)
            for k, dv in zip(want + list(range(nr, nr + npar)), diff):
                full[k] = dv
            return tuple(fn(*full))

        diff_in = [vals[k] for k in want] + vals[nr:]
        _, vjp = jax.vjp(closed, *diff_in)
        grads = vjp(tuple(cvals))
        for q in range(len(want)):
            oref[q][...] = grads[q].astype(oref[q].dtype)
        if npar:
            first = pl.program_id(0) == 0

            @pl.when(first)
            def _():
                for q in range(npar):
                    oref[len(want) + q][...] = jnp.zeros_like(oref[len(want) + q])

            for q in range(npar):
                oref[len(want) + q][...] += grads[len(want) + q]

    out_specs = [pl.BlockSpec((tq, rows[k][1]), lambda i, j, cb=rows[k][2]: (i, j)) for k in want]
    out_specs += [_full_spec(p) for p in pars]
    out_shape = [jax.ShapeDtypeStruct((t, wt), dt) for (wt, dt) in drow]
    out_shape += [jax.ShapeDtypeStruct(p.shape, F32) for p in pars]
    return pl.pallas_call(
        body,
        name=name,
        grid=(t // tq, ncol),
        in_specs=[_row_spec(s, tq) for s in rows] + [_full_spec(p) for p in pars] + [_row_spec(s, tq) for s in flat_cots],
        out_specs=out_specs,
        out_shape=out_shape,
        compiler_params=_params(("arbitrary", "arbitrary") if npar else ("parallel", "parallel")),
    )(*[s[0] for s in rows], *pars, *[s[0] for s in flat_cots])


def _whole(a, w=None):
    return (a, a.shape[1] if w is None else w, 0)


def _rms_fn(ql, kvl, gq, gk):
    def one(x, g):
        return x * lax.rsqrt(jnp.mean(x * x, -1, keepdims=True) + 1e-6) * g

    return one(ql, gq), one(kvl, gk)


def _layernorm(s, g, b):
    mu = jnp.mean(s, -1, keepdims=True)
    d = s - mu
    var = jnp.mean(d * d, -1, keepdims=True)
    return d * lax.rsqrt(var + 1e-5) * g + b


def _ln_mix_fn(alpha):
    def fn(h, mix, g, b):
        return (_layernorm(alpha * h + mix, g, b),)

    return fn


def _ln_glu_fn(alpha):
    def fn(h, val, gate, g, b):
        return (_layernorm(alpha * h + val * jax.nn.sigmoid(gate), g, b),)

    return fn


def _ln_ffn_fn(alpha):
    def fn(h, f, pw, gate, g, b):
        return (_layernorm(alpha * h + f + pw * jax.nn.sigmoid(gate), g, b),)

    return fn


def _swiglu_fn(g, u):
    return (jax.nn.silu(g) * u,)


def _gelu_fn(ys, h, d):
    return (jax.nn.gelu(ys + d * h),)


def _merge_fn(o1, o2, o3, l1, l2, l3):
    m = jnp.maximum(jnp.maximum(l1, l2), l3)
    e1, e2, e3 = jnp.exp(l1 - m), jnp.exp(l2 - m), jnp.exp(l3 - m)
    return ((e1 * o1 + e2 * o2 + e3 * o3) / (e1 + e2 + e3),)


def _swap_halves(t):
    w = t.shape[1]
    lane = lax.broadcasted_iota(jnp.int32, t.shape, 1) % LANE
    up = jnp.where(lane < ROPE_DIM, pltpu.roll(t, ROPE_DIM // 2, 1), 0.0)
    return jnp.where(lane < ROPE_DIM // 2, pltpu.roll(t, w - ROPE_DIM // 2, 1), up)


def _rope_fwd_fn(nh):
    def fn(qpe, kpe, cos, sin):
        cq, sq = jnp.tile(cos, (1, nh)), jnp.tile(sin, (1, nh))
        return qpe * cq + _swap_halves(qpe) * sq, kpe * cos + _swap_halves(kpe) * sin

    return fn


def _rope_bwd_fn(nh):
    def fn(dq, dk_heads, cos, sin):
        cq, sq = jnp.tile(cos, (1, nh)), jnp.tile(sin, (1, nh))
        dk = dk_heads[:, :LANE]
        for h in range(1, nh):
            dk = dk + dk_heads[:, h * LANE : (h + 1) * LANE]
        return dq * cq + _swap_halves(dq * sq), dk * cos + _swap_halves(dk * sin)

    return fn


def _rope_tables(positions, name):
    t = positions.shape[0]
    tq = _tile(t, (512, 128, 8))
    half = ROPE_DIM // 2

    def body(p_ref, c_ref, s_ref):
        lane = lax.broadcasted_iota(jnp.int32, (tq, LANE), 1)
        idx = (lane % half).astype(F32)
        inv_freq = jnp.exp(idx * (-math.log(ROPE_THETA) / half))
        ang = p_ref[...].astype(F32) * inv_freq
        live = lane < ROPE_DIM
        c_ref[...] = jnp.where(live, jnp.cos(ang), 0.0)
        s_ref[...] = jnp.where(live, jnp.where(lane < half, -jnp.sin(ang), jnp.sin(ang)), 0.0)

    return pl.pallas_call(
        body,
        name=name,
        grid=(t // tq,),
        in_specs=[pl.BlockSpec((tq, 1), lambda i: (i, 0))],
        out_specs=[pl.BlockSpec((tq, LANE), lambda i: (i, 0))] * 2,
        out_shape=[jax.ShapeDtypeStruct((t, LANE), F32)] * 2,
        compiler_params=_params(("parallel",)),
    )(positions)


def _loss_kernel(y, target, name):
    t, d = y.shape
    tq = _tile(t, (256, 128, 8))

    def body(y_ref, t_ref, dy_ref, l_ref):
        @pl.when(pl.program_id(0) == 0)
        def _():
            l_ref[...] = jnp.zeros_like(l_ref)

        e = y_ref[...] - t_ref[...]
        dy_ref[...] = e * (1.0 / d)
        l_ref[...] += jnp.sum(e * e) * (0.5 / d)

    return pl.pallas_call(
        body,
        name=name,
        grid=(t // tq,),
        in_specs=[pl.BlockSpec((tq, d), lambda i: (i, 0))] * 2,
        out_specs=[pl.BlockSpec((tq, d), lambda i: (i, 0)), pl.BlockSpec((8, LANE), lambda i: (0, 0))],
        out_shape=[jax.ShapeDtypeStruct((t, d), F32), jax.ShapeDtypeStruct((8, LANE), F32)],
        compiler_params=_params(("arbitrary",)),
    )(y, target)


def _addn(arrs, name):
    w = arrs[0].shape[1]
    return _rowwise(lambda *v: (functools.reduce(lambda p, q: p + q, v),), [_whole(a) for a in arrs], [], [(w, w, F32)],
                    name=name, tq=256)[0]


def _mla_tiles(t):
    tq = _tile(t, (512, 256, 128))
    return tq, t // tq


def _mla_fwd(qf, qpe, kvf, kpe, nh, name):
    t = qf.shape[0]
    tq, nq = _mla_tiles(t)
    scale = (NOPE_DIM + ROPE_DIM) ** -0.5

    def body(qn_ref, qp_ref, kn_ref, kp_ref, v_ref, o_ref, lse_ref):
        i = pl.program_id(1)
        qn = qn_ref[...].astype(BF16)
        qp = qp_ref[...]

        def step(j, carry, masked):
            m, l, acc = carry
            ks = pl.ds(pl.multiple_of(j * tq, tq), tq)
            s = (_dot(qn, kn_ref[ks, :], NT) + _dot(qp, kp_ref[ks, :], NT)) * scale
            if masked:
                row = lax.broadcasted_iota(jnp.int32, (tq, tq), 0)
                col = lax.broadcasted_iota(jnp.int32, (tq, tq), 1)
                s = jnp.where(col <= row, s, NEG)
            m_new = jnp.maximum(m, jnp.max(s, -1, keepdims=True))
            p = jnp.exp(s - m_new)
            a = jnp.exp(m - m_new)
            return m_new, a * l + jnp.sum(p, -1, keepdims=True), a * acc + _dot(p.astype(BF16), v_ref[ks, :], NN)

        init = (jnp.full((tq, 1), NEG, F32), jnp.zeros((tq, 1), F32), jnp.zeros((tq, V_DIM), F32))
        carry = lax.fori_loop(0, i, lambda j, c: step(j, c, False), init)
        m, l, acc = step(i, carry, True)
        o_ref[...] = (acc / l).astype(o_ref.dtype)
        lse_ref[...] = jnp.broadcast_to(m + jnp.log(l), (tq, LANE))

    blk = lambda h, i: (i, h)
    return pl.pallas_call(
        body,
        name=name,
        grid=(nh, nq),
        in_specs=[
            pl.BlockSpec((tq, LANE), blk),
            pl.BlockSpec((tq, LANE), blk),
            pl.BlockSpec((t, LANE), lambda h, i: (0, h)),
            pl.BlockSpec((t, LANE), lambda h, i: (0, 0)),
            pl.BlockSpec((t, LANE), lambda h, i: (0, nh + h)),
        ],
        out_specs=[pl.BlockSpec((tq, LANE), blk), pl.BlockSpec((tq, LANE), blk)],
        out_shape=[jax.ShapeDtypeStruct((t, nh * LANE), BF16), jax.ShapeDtypeStruct((t, nh * LANE), F32)],
        compiler_params=_params(("parallel", "parallel")),
    )(qf, qpe, kvf, kpe, kvf)


def _mla_bwd_q(qf, qpe, kvf, kpe, do, o, lse, nh, name):
    t = qf.shape[0]
    tq, nq = _mla_tiles(t)
    scale = (NOPE_DIM + ROPE_DIM) ** -0.5

    def body(qn_ref, qp_ref, kn_ref, kp_ref, v_ref, do_ref, o_ref, lse_ref, dqn_ref, dqp_ref, dl_ref):
        i = pl.program_id(1)
        qn = qn_ref[...].astype(BF16)
        qp = qp_ref[...]
        dof = do_ref[...].astype(F32)
        dob = dof.astype(BF16)
        delta = jnp.sum(dof * o_ref[...].astype(F32), -1, keepdims=True)
        lse1 = lse_ref[:, :1]

        def step(j, carry, masked):
            dqn, dqp = carry
            ks = pl.ds(pl.multiple_of(j * tq, tq), tq)
            kn, kp = kn_ref[ks, :], kp_ref[ks, :]
            s = (_dot(qn, kn, NT) + _dot(qp, kp, NT)) * scale
            p = jnp.exp(s - lse1)
            if masked:
                row = lax.broadcasted_iota(jnp.int32, (tq, tq), 0)
                col = lax.broadcasted_iota(jnp.int32, (tq, tq), 1)
                p = jnp.where(col <= row, p, 0.0)
            dp = _dot(dob, v_ref[ks, :], NT)
            ds = (p * (dp - delta) * scale).astype(BF16)
            return dqn + _dot(ds, kn, NN), dqp + _dot(ds, kp, NN)

        init = (jnp.zeros((tq, LANE), F32), jnp.zeros((tq, LANE), F32))
        carry = lax.fori_loop(0, i, lambda j, c: step(j, c, False), init)
        dqn, dqp = step(i, carry, True)
        dqn_ref[...] = dqn
        dqp_ref[...] = dqp
        dl_ref[...] = jnp.broadcast_to(delta, (tq, LANE))

    blk = lambda h, i: (i, h)
    bs = pl.BlockSpec((tq, LANE), blk)
    return pl.pallas_call(
        body,
        name=name,
        grid=(nh, nq),
        in_specs=[
            bs,
            bs,
            pl.BlockSpec((t, LANE), lambda h, i: (0, h)),
            pl.BlockSpec((t, LANE), lambda h, i: (0, 0)),
            pl.BlockSpec((t, LANE), lambda h, i: (0, nh + h)),
            bs,
            bs,
            bs,
        ],
        out_specs=[bs, bs, bs],
        out_shape=[jax.ShapeDtypeStruct((t, nh * LANE), F32)] * 3,
        compiler_params=_params(("parallel", "parallel")),
    )(qf, qpe, kvf, kpe, kvf, do, o, lse)


def _mla_bwd_kv(qn16, qpe, kvf, kpe, do16, lse_row, delta_row, nh, name):
    t = qn16.shape[0]
    tq, nq = _mla_tiles(t)
    scale = (NOPE_DIM + ROPE_DIM) ** -0.5

    def body(kn_ref, kp_ref, v_ref, qn_ref, qp_ref, do_ref, lse_ref, dl_ref, dkn_ref, dkp_ref, dv_ref):
        j = pl.program_id(1)
        kn, kp, v = kn_ref[...], kp_ref[...], v_ref[...]

        def step(i, carry, masked):
            dkn, dkp, dv = carry
            qs = pl.ds(pl.multiple_of(i * tq, tq), tq)
            qn, qp, dob = qn_ref[qs, :], qp_ref[qs, :], do_ref[qs, :]
            st = (_dot(kn, qn, NT) + _dot(kp, qp, NT)) * scale
            pt = jnp.exp(st - lse_ref[0, :, qs])
            if masked:
                key = lax.broadcasted_iota(jnp.int32, (tq, tq), 0)
                qry = lax.broadcasted_iota(jnp.int32, (tq, tq), 1)
                pt = jnp.where(key <= qry, pt, 0.0)
            dv = dv + _dot(pt.astype(BF16), dob, NN)
            dpt = _dot(v, dob, NT)
            dst = (pt * (dpt - dl_ref[0, :, qs]) * scale).astype(BF16)
            return dkn + _dot(dst, qn, NN), dkp + _dot(dst, qp, NN), dv

        z = jnp.zeros((tq, LANE), F32)
        carry = step(j, (z, z, z), True)
        dkn, dkp, dv = lax.fori_loop(j + 1, nq, lambda i, c: step(i, c, False), carry)
        dkn_ref[...] = dkn
        dkp_ref[...] = dkp
        dv_ref[...] = dv

    blk = pl.BlockSpec((tq, LANE), lambda h, j: (j, h))
    res = lambda f: pl.BlockSpec((t, LANE), f)
    row = pl.BlockSpec((1, 1, t), lambda h, j: (h, 0, 0))
    return pl.pallas_call(
        body,
        name=name,
        grid=(nh, nq),
        in_specs=[
            blk,
            pl.BlockSpec((tq, LANE), lambda h, j: (j, 0)),
            pl.BlockSpec((tq, LANE), lambda h, j: (j, nh + h)),
            res(lambda h, j: (0, h)),
            res(lambda h, j: (0, h)),
            res(lambda h, j: (0, h)),
            row,
            row,
        ],
        out_specs=[blk, blk, blk],
        out_shape=[jax.ShapeDtypeStruct((t, nh * LANE), F32)] * 3,
        compiler_params=_params(("parallel", "parallel")),
    )(kvf, kpe, kvf, qn16, qpe, do16, lse_row, delta_row)


def _alibi_slopes(n):
    return [float(2.0 ** (-8.0 * i / n)) for i in range(1, n + 1)]


def _band_masks(has_prev):
    qi = lax.broadcasted_iota(jnp.int32, (BLK, BLK), 0)
    ki = lax.broadcasted_iota(jnp.int32, (BLK, BLK), 1)
    return ki <= qi, (ki >= qi) & has_prev, (qi - ki).astype(F32), (qi - ki + BLK).astype(F32)


def _band_fwd(zc, nh, dil, name):
    t = zc.shape[0]
    nb = t // BLK
    nbc = nb // dil
    scale = DIL_HEAD_DIM ** -0.5
    slopes = _alibi_slopes(nh)
    dw = nh * LANE

    def body(q_ref, kc_ref, kp_ref, vc_ref, vp_ref, o_ref, l_ref):
        b = pl.program_id(0)
        mask_c, mask_p, dist_c, dist_p = _band_masks((b % nbc) > 0)
        for h in range(nh):
            sl = slice(h * LANE, (h + 1) * LANE)
            q = q_ref[:, sl]
            bias = slopes[h] * dil
            sc = jnp.where(mask_c, _dot(q, kc_ref[:, sl], NT) * scale - bias * dist_c, NEG)
            sp = jnp.where(mask_p, _dot(q, kp_ref[:, sl], NT) * scale - bias * dist_p, NEG)
            m = jnp.maximum(jnp.max(sc, -1, keepdims=True), jnp.max(sp, -1, keepdims=True))
            ec, ep = jnp.exp(sc - m), jnp.exp(sp - m)
            l = jnp.sum(ec, -1, keepdims=True) + jnp.sum(ep, -1, keepdims=True)
            inv = 1.0 / l
            o = _dot((ec * inv).astype(BF16), vc_ref[:, sl], NN) + _dot((ep * inv).astype(BF16), vp_ref[:, sl], NN)
            o_ref[:, sl] = o
            l_ref[:, sl] = jnp.broadcast_to(m + jnp.log(l), (BLK, LANE))

    prev = lambda b: jnp.maximum(b - 1, 0)
    return pl.pallas_call(
        body,
        name=name,
        grid=(nb,),
        in_specs=[
            pl.BlockSpec((BLK, dw), lambda b: (b, 0)),
            pl.BlockSpec((BLK, dw), lambda b: (b, 1)),
            pl.BlockSpec((BLK, dw), lambda b: (prev(b), 1)),
            pl.BlockSpec((BLK, dw), lambda b: (b, 2)),
            pl.BlockSpec((BLK, dw), lambda b: (prev(b), 2)),
        ],
        out_specs=[pl.BlockSpec((BLK, dw), lambda b: (b, 0))] * 2,
        out_shape=[jax.ShapeDtypeStruct((t, dw), F32)] * 2,
        compiler_params=_params(("parallel",)),
    )(zc, zc, zc, zc, zc)


def _band_bwd(zc, o, lse, do, dl, nh, dil, name):
    t = zc.shape[0]
    nb = t // BLK
    nbc = nb // dil
    scale = DIL_HEAD_DIM ** -0.5
    slopes = _alibi_slopes(nh)
    dw = nh * LANE

    def body(q_ref, k_ref, v_ref, kp_ref, vp_ref, qn_ref, o_ref, l_ref, do_ref, dl_ref, on_ref, ln_ref, don_ref, dln_ref,
             dq_ref, dk_ref, dv_ref):
        b = pl.program_id(0)
        has_prev = (b % nbc) > 0
        has_next = jnp.logical_and(((b + 1) % nbc) > 0, b + 1 < nb)
        mask_c, mask_p, dist_c, dist_p = _band_masks(has_prev)
        _, mask_n, _, _ = _band_masks(has_next)
        for h in range(nh):
            sl = slice(h * LANE, (h + 1) * LANE)
            bias = slopes[h] * dil
            q, k, v = q_ref[:, sl], k_ref[:, sl], v_ref[:, sl]
            kp, vp, qn = kp_ref[:, sl], vp_ref[:, sl], qn_ref[:, sl]
            dof, donf = do_ref[:, sl], don_ref[:, sl]
            dob, donb = dof.astype(BF16), donf.astype(BF16)
            lse1, lsen1 = l_ref[:, sl][:, :1], ln_ref[:, sl][:, :1]
            adj = jnp.sum(dl_ref[:, sl] - dof * o_ref[:, sl], -1, keepdims=True)
            adjn = jnp.sum(dln_ref[:, sl] - donf * on_ref[:, sl], -1, keepdims=True)
            pa = jnp.where(mask_c, jnp.exp(_dot(q, k, NT) * scale - bias * dist_c - lse1), 0.0)
            pb = jnp.where(mask_p, jnp.exp(_dot(q, kp, NT) * scale - bias * dist_p - lse1), 0.0)
            dsa = (pa * (_dot(dob, v, NT) + adj)).astype(BF16)
            dsb = (pb * (_dot(dob, vp, NT) + adj)).astype(BF16)
            dq_ref[:, sl] = (_dot(dsa, k, NN) + _dot(dsb, kp, NN)) * scale
            pc = jnp.where(mask_n, jnp.exp(_dot(qn, k, NT) * scale - bias * dist_p - lsen1), 0.0)
            dsc = (pc * (_dot(donb, v, NT) + adjn)).astype(BF16)
            dk_ref[:, sl] = (_dot(dsa, q, TN) + _dot(dsc, qn, TN)) * scale
            dv_ref[:, sl] = _dot(pa.astype(BF16), dob, TN) + _dot(pc.astype(BF16), donb, TN)

    prev = lambda b: jnp.maximum(b - 1, 0)
    nxt = lambda b: jnp.minimum(b + 1, nb - 1)
    spec = lambda f: pl.BlockSpec((BLK, dw), f)
    cur0, cur1, cur2 = spec(lambda b: (b, 0)), spec(lambda b: (b, 1)), spec(lambda b: (b, 2))
    nxt0 = spec(lambda b: (nxt(b), 0))
    return pl.pallas_call(
        body,
        name=name,
        grid=(nb,),
        in_specs=[cur0, cur1, cur2, spec(lambda b: (prev(b), 1)), spec(lambda b: (prev(b), 2)), nxt0,
                  cur0, cur0, cur0, cur0, nxt0, nxt0, nxt0, nxt0],
        out_specs=[cur0] * 3,
        out_shape=[jax.ShapeDtypeStruct((t, dw), F32)] * 3,
        compiler_params=_params(("parallel",)),
    )(zc, zc, zc, zc, zc, zc, o, lse, do, dl, o, lse, do, dl)


def _s5_ops(a_re, a_im, ldt, bt_re, bt_im, c_re, c_im):
    L, g, p = S5_CHUNK, S5_GROUP, S5_STATE
    dt = jnp.exp(ldt)
    lr, li = a_re * dt, a_im * dt
    er = jnp.exp(lr)
    lam_re, lam_im = er * jnp.cos(li), er * jnp.sin(li)
    nr, ni = lam_re - 1.0, lam_im
    den = a_re * a_re + a_im * a_im
    fr, fi = (nr * a_re + ni * a_im) / den, (ni * a_re - nr * a_im) / den
    bb_re, bb_im = fr * bt_re - fi * bt_im, fr * bt_im + fi * bt_re

    def power(tau):
        mag = jnp.exp(tau * lr)
        return mag * jnp.cos(tau * li), mag * jnp.sin(tau * li)

    step = lax.broadcasted_iota(jnp.int32, (L, 1), 0).astype(F32)

    def outer(pr, pi, mr, mi):
        re = pr[:, None, :] * mr[None] - pi[:, None, :] * mi[None]
        im = pr[:, None, :] * mi[None] + pi[:, None, :] * mr[None]
        return re.reshape(L * g, p), im.reshape(L * g, p)

    half = float(L // 2)
    cp_re, cp_im = outer(*power(step - half), c_re, c_im)
    pb_re, pb_im = outer(*power(half - step), bb_re, bb_im)
    toep = _dot(cp_re, pb_re, NT, HI) - _dot(cp_im, pb_im, NT, HI)
    trow = lax.broadcasted_iota(jnp.int32, (L * g, L * g), 0) // g
    scol = lax.broadcasted_iota(jnp.int32, (L * g, L * g), 1) // g
    toep = jnp.where(trow >= scol, toep, 0.0)
    et_re, et_im = outer(*power(float(L - 1) - step), bb_re, bb_im)
    f_re, f_im = outer(*power(step + 1.0), c_re, c_im)
    big_re, big_im = power(jnp.full((1, 1), float(L), F32))
    return toep, et_re, et_im, f_re, f_im, big_re, big_im


def _s5_y(ops, u, sp_re, sp_im):
    toep, _, _, f_re, f_im, _, _ = ops
    return _dot(u, toep, NT, HI) + _dot(sp_re, f_re, NT, HI) - _dot(sp_im, f_im, NT, HI)


def _s5_group_specs(ng):
    vec = pl.BlockSpec((1, 1, S5_STATE), lambda g: (g, 0, 0))
    one = pl.BlockSpec((1, 1, 1), lambda g: (g, 0, 0))
    mat = pl.BlockSpec((1, S5_GROUP, S5_STATE), lambda g: (g, 0, 0))
    return [vec, vec, one, mat, mat, mat, mat]


def _s5_load(refs):
    return [r[0] for r in refs]


def _s5_local(prm, u, name):
    ng, n, w = u.shape
    p = S5_STATE

    def body(*refs):
        ops = _s5_ops(*_s5_load(refs[:7]))
        uu = refs[7][0]
        refs[8][0] = _dot(uu, ops[1], NN, HI)
        refs[9][0] = _dot(uu, ops[2], NN, HI)
        refs[10][0] = ops[5]
        refs[11][0] = ops[6]

    blk = lambda a, b: pl.BlockSpec((1, a, b), lambda g: (g, 0, 0))
    return pl.pallas_call(
        body,
        name=name,
        grid=(ng,),
        in_specs=_s5_group_specs(ng) + [blk(n, w)],
        out_specs=[blk(n, p), blk(n, p), blk(1, p), blk(1, p)],
        out_shape=[jax.ShapeDtypeStruct((ng, n, p), F32)] * 2 + [jax.ShapeDtypeStruct((ng, 1, p), F32)] * 2,
        compiler_params=_params(("parallel",)),
    )(*prm, u)


def _s5_carry(e_re, e_im, lam_re, lam_im, name):
    n, w = e_re.shape
    cw = _tile(w, (1024, 512, 256, 128))

    def body(er_ref, ei_ref, lr_ref, li_ref, sr_ref, si_ref):
        lr, li = lr_ref[...], li_ref[...]

        def step(k, carry):
            sr, si = carry
            row = pl.ds(k, 1)
            sr_ref[row, :] = sr
            si_ref[row, :] = si
            return lr * sr - li * si + er_ref[row, :], li * sr + lr * si + ei_ref[row, :]

        z = jnp.zeros((1, cw), F32)
        lax.fori_loop(0, n, step, (z, z))

    col = pl.BlockSpec((n, cw), lambda j: (0, j))
    one = pl.BlockSpec((1, cw), lambda j: (0, j))
    return pl.pallas_call(
        body,
        name=name,
        grid=(w // cw,),
        in_specs=[col, col, one, one],
        out_specs=[col, col],
        out_shape=[jax.ShapeDtypeStruct((n, w), F32)] * 2,
        compiler_params=_params(("parallel",)),
    )(e_re, e_im, lam_re, lam_im)


def _s5_carry_bwd(dsp_re, dsp_im, sp_re, sp_im, lam_re, lam_im, name):
    n, w = dsp_re.shape
    cw = _tile(w, (1024, 512, 256, 128))

    def body(dr_ref, di_ref, sr_ref, si_ref, lr_ref, li_ref, gr_ref, gi_ref, dlr_ref, dli_ref):
        lr, li = lr_ref[...], li_ref[...]

        def step(q, carry):
            gr_next, gi_next, dr_next, di_next, alr, ali = carry
            k = n - 1 - q
            row = pl.ds(k, 1)
            gr = dr_next + lr * gr_next + li * gi_next
            gi = di_next - li * gr_next + lr * gi_next
            gr_ref[row, :] = gr
            gi_ref[row, :] = gi
            sr, si = sr_ref[row, :], si_ref[row, :]
            return gr, gi, dr_ref[row, :], di_ref[row, :], alr + gr * sr + gi * si, ali + gi * sr - gr * si

        z = jnp.zeros((1, cw), F32)
        out = lax.fori_loop(0, n, step, (z, z, z, z, z, z))
        dlr_ref[...] = out[4]
        dli_ref[...] = out[5]

    col = pl.BlockSpec((n, cw), lambda j: (0, j))
    one = pl.BlockSpec((1, cw), lambda j: (0, j))
    return pl.pallas_call(
        body,
        name=name,
        grid=(w // cw,),
        in_specs=[col, col, col, col, one, one],
        out_specs=[col, col, one, one],
        out_shape=[jax.ShapeDtypeStruct((n, w), F32)] * 2 + [jax.ShapeDtypeStruct((1, w), F32)] * 2,
        compiler_params=_params(("parallel",)),
    )(dsp_re, dsp_im, sp_re, sp_im, lam_re, lam_im)


def _s5_out(prm, u, sp_re, sp_im, name):
    ng, n, w = u.shape
    p = S5_STATE

    def body(*refs):
        ops = _s5_ops(*_s5_load(refs[:7]))
        refs[10][0] = _s5_y(ops, refs[7][0], refs[8][0], refs[9][0])

    blk = lambda a, b: pl.BlockSpec((1, a, b), lambda g: (g, 0, 0))
    return pl.pallas_call(
        body,
        name=name,
        grid=(ng,),
        in_specs=_s5_group_specs(ng) + [blk(n, w), blk(n, p), blk(n, p)],
        out_specs=blk(n, w),
        out_shape=jax.ShapeDtypeStruct((ng, n, w), F32),
        compiler_params=_params(("parallel",)),
    )(*prm, u, sp_re, sp_im)


def _s5_bwd_state(prm, dy, name):
    ng, n, w = dy.shape
    p = S5_STATE

    def body(*refs):
        ops = _s5_ops(*_s5_load(refs[:7]))
        d = refs[7][0]
        refs[8][0] = _dot(d, ops[3], NN, HI)
        refs[9][0] = -_dot(d, ops[4], NN, HI)

    blk = lambda a, b: pl.BlockSpec((1, a, b), lambda g: (g, 0, 0))
    return pl.pallas_call(
        body,
        name=name,
        grid=(ng,),
        in_specs=_s5_group_specs(ng) + [blk(n, w)],
        out_specs=[blk(n, p), blk(n, p)],
        out_shape=[jax.ShapeDtypeStruct((ng, n, p), F32)] * 2,
        compiler_params=_params(("parallel",)),
    )(*prm, dy)


def _s5_bwd_main(prm, u, sp_re, sp_im, dy, g_re, g_im, dlam_re, dlam_im, name):
    ng, n, w = u.shape
    p = S5_STATE

    def body(*refs):
        prm_v = _s5_load(refs[:7])
        uu, sr, si, d, gr, gi, dlr, dli = [r[0] for r in refs[7:15]]

        def phi(*args):
            ops = _s5_ops(*args[:7])
            y = _s5_y(ops, args[7], sr, si)
            e_re, e_im = _dot(args[7], ops[1], NN, HI), _dot(args[7], ops[2], NN, HI)
            return (jnp.sum(d * y) + jnp.sum(gr * e_re) + jnp.sum(gi * e_im)
                    + jnp.sum(dlr * ops[5]) + jnp.sum(dli * ops[6]))

        grads = jax.grad(phi, argnums=tuple(range(8)))(*prm_v, uu)
        for q in range(8):
            refs[15 + q][0] = grads[q]

    blk = lambda a, b: pl.BlockSpec((1, a, b), lambda g: (g, 0, 0))
    prm_specs = _s5_group_specs(ng)
    return pl.pallas_call(
        body,
        name=name,
        grid=(ng,),
        in_specs=prm_specs + [blk(n, w), blk(n, p), blk(n, p), blk(n, w), blk(n, p), blk(n, p), blk(1, p), blk(1, p)],
        out_specs=prm_specs + [blk(n, w)],
        out_shape=[jax.ShapeDtypeStruct(a.shape, F32) for a in prm] + [jax.ShapeDtypeStruct((ng, n, w), F32)],
        compiler_params=_params(("parallel",)),
    )(*prm, u, sp_re, sp_im, dy, g_re, g_im, dlam_re, dlam_im)


def _to_groups(h):
    t, d = h.shape
    ng, n = d // S5_GROUP, t // S5_CHUNK
    return h.reshape(n, S5_CHUNK, ng, S5_GROUP).transpose(2, 0, 1, 3).reshape(ng, n, S5_CHUNK * S5_GROUP)


def _from_groups(y, t, d):
    ng, n = d // S5_GROUP, t // S5_CHUNK
    return y.reshape(ng, n, S5_CHUNK, S5_GROUP).transpose(1, 2, 0, 3).reshape(t, d)


def _states_to_cols(e):
    ng, n, p = e.shape
    return e.transpose(1, 0, 2).reshape(n, ng * p)


def _cols_to_states(s, ng):
    n = s.shape[0]
    return s.reshape(n, ng, S5_STATE).transpose(1, 0, 2)


def _me():
    return lax.axis_index("x"), lax.axis_index("y"), lax.axis_index("c")


def _flip(v, bit):
    return 1 - v if bit else v


def _all_gather(x, name):
    r, c = x.shape

    def body(x_ref, out_ref, send_sems, recv_sems, local_sem):
        mx, my, mc = _me()
        me, sibling = (mx, my, mc), (mx, my, 1 - mc)
        chips = [(1 - mx, my), (mx, 1 - my), (1 - mx, 1 - my)]

        def slot(px, py, pc):
            return out_ref.at[4 * px + 2 * py + pc]

        def copy(k, block, to, src=None):
            return pltpu.make_async_remote_copy(
                src_ref=slot(*block) if src is None else src,
                dst_ref=slot(*block),
                send_sem=send_sems.at[k],
                recv_sem=recv_sems.at[k],
                device_id=to,
                device_id_type=pl.DeviceIdType.MESH,
            )

        mine = pltpu.make_async_copy(x_ref, slot(*me), local_sem)
        mine.start()
        first = [copy(0, me, sibling, src=x_ref)]
        first += [copy(1 + j, me, (*chip, mc), src=x_ref) for j, chip in enumerate(chips)]
        for cp in first:
            cp.start()
        passed = [copy(4 + j, (*chip, mc), sibling) for j, chip in enumerate(chips)]
        for j, chip in enumerate(chips):
            copy(1 + j, (*chip, mc), me).wait_recv()
            passed[j].start()
        copy(0, sibling, me).wait_recv()
        for j, chip in enumerate(chips):
            copy(4 + j, (*chip, 1 - mc), me).wait_recv()
        for cp in first + passed:
            cp.wait_send()
        mine.wait()

    return pl.pallas_call(
        body,
        name=name,
        in_specs=[pl.BlockSpec(memory_space=pl.ANY)],
        out_specs=pl.BlockSpec(memory_space=pl.ANY),
        out_shape=jax.ShapeDtypeStruct((N_DEV, r, c), x.dtype),
        scratch_shapes=[pltpu.SemaphoreType.DMA((7,)), pltpu.SemaphoreType.DMA((7,)), pltpu.SemaphoreType.DMA],
    )(x)


def _all_to_all(g, name):
    _, r, c = g.shape

    def body(g_ref, out_ref, send_sems, recv_sems, local_sem):
        mx, my, mc = _me()
        mine_idx = 4 * mx + 2 * my + mc
        own = pltpu.make_async_copy(g_ref.at[mine_idx], out_ref.at[mine_idx], local_sem)
        own.start()
        copies = []
        for m in range(1, N_DEV):
            px, py, pc = _flip(mx, m & 4), _flip(my, m & 2), _flip(mc, m & 1)
            copies.append(
                pltpu.make_async_remote_copy(
                    src_ref=g_ref.at[4 * px + 2 * py + pc],
                    dst_ref=out_ref.at[mine_idx],
                    send_sem=send_sems.at[m - 1],
                    recv_sem=recv_sems.at[m - 1],
                    device_id=(px, py, pc),
                    device_id_type=pl.DeviceIdType.MESH,
                )
            )
        for cp in copies:
            cp.start()
        for cp in copies:
            cp.wait_recv()
        for cp in copies:
            cp.wait_send()
        own.wait()

    return pl.pallas_call(
        body,
        name=name,
        in_specs=[pl.BlockSpec(memory_space=pl.ANY)],
        out_specs=pl.BlockSpec(memory_space=pl.ANY),
        out_shape=jax.ShapeDtypeStruct(g.shape, g.dtype),
        scratch_shapes=[pltpu.SemaphoreType.DMA((7,)), pltpu.SemaphoreType.DMA((7,)), pltpu.SemaphoreType.DMA],
    )(g)


def _sum_slots(recv, name):
    _, r, c = recv.shape
    tr = _tile(r, (256, 128, 64, 32, 16, 8))

    def body(r_ref, o_ref):
        acc = r_ref[0].astype(F32)
        for k in range(1, N_DEV):
            acc = acc + r_ref[k].astype(F32)
        o_ref[...] = acc

    return pl.pallas_call(
        body,
        name=name,
        grid=(r // tr,),
        in_specs=[pl.BlockSpec((N_DEV, tr, c), lambda i: (0, i, 0))],
        out_specs=pl.BlockSpec((tr, c), lambda i: (i, 0)),
        out_shape=jax.ShapeDtypeStruct((r, c), F32),
        compiler_params=_params(("parallel",)),
    )(recv)


def _adamw(w, g, m, v, name):
    shape = w.shape
    c = shape[-1]
    as2d = lambda a: a.reshape(-1, c)
    w2, g2, m2, v2 = as2d(w), as2d(g), as2d(m), as2d(v)
    r = w2.shape[0]
    tr = _tile(r, (256, 128, 64, 32, 16, 8))
    c1 = 1.0 / (1.0 - ADAM_B1 ** ADAM_STEP)
    c2 = 1.0 / (1.0 - ADAM_B2 ** ADAM_STEP)

    def body(w_ref, g_ref, m_ref, v_ref, d_ref, mo_ref, vo_ref):
        gg = g_ref[...]
        mn = ADAM_B1 * m_ref[...] + (1.0 - ADAM_B1) * gg
        vn = ADAM_B2 * v_ref[...] + (1.0 - ADAM_B2) * (gg * gg)
        d_ref[...] = -ADAM_LR * ((mn * c1) / (jnp.sqrt(vn * c2) + ADAM_EPS) + ADAM_WD * w_ref[...])
        mo_ref[...] = mn
        vo_ref[...] = vn

    spec = pl.BlockSpec((tr, c), lambda i: (i, 0))
    d, mn, vn = pl.pallas_call(
        body,
        name=name,
        grid=(r // tr,),
        in_specs=[spec] * 4,
        out_specs=[spec] * 3,
        out_shape=[jax.ShapeDtypeStruct((r, c), F32)] * 3,
        compiler_params=_params(("parallel",)),
    )(w2, g2, m2, v2)
    return d.reshape(shape), mn.reshape(shape), vn.reshape(shape)


COL_SHARDED = ("attn_w_in", "mla_w_q_b", "mla_w_kv_b", "s5_d", "s5_w_glu", "ffn_w_in", "ple_w")
ROW_SHARDED = ("attn_w_out", "ffn_w_out", "ple_gate_w")
REPLICATED = ("mla_q_norm", "mla_kv_norm", "s5_a_re", "s5_a_im", "s5_log_dt", "s5_b_re", "s5_b_im", "s5_c_re",
              "s5_c_im", "ln1_g", "ln1_b", "ln2_g", "ln2_b")
BIG_WEIGHTS = ("attn_w_in", "mla_w_q_b", "mla_w_kv_b", "attn_w_out", "s5_w_glu", "ffn_w_in", "ffn_w_out", "ple_w", "ple_gate_w")
WEIGHTS = ("attn_w_in", "mla_q_norm", "mla_w_q_b", "mla_kv_norm", "mla_w_kv_b", "attn_w_out", "s5_a_re", "s5_a_im",
           "s5_log_dt", "s5_b_re", "s5_b_im", "s5_c_re", "s5_c_im", "s5_d", "s5_w_glu", "ln1_g", "ln1_b", "ffn_w_in",
           "ffn_w_out", "ple_w", "ple_gate_w", "ln2_g", "ln2_b")


def _unshard(name, gathered):
    if name in COL_SHARDED:
        full = jnp.moveaxis(gathered, 0, -2)
        return full.reshape(full.shape[:-2] + (full.shape[-2] * full.shape[-1],))
    full = jnp.moveaxis(gathered, 0, 1)
    return full.reshape((full.shape[0], full.shape[1] * full.shape[2]) + full.shape[3:])


def _by_owner(name, grad):
    if name in COL_SHARDED:
        g = grad.reshape(grad.shape[:-1] + (N_DEV, grad.shape[-1] // N_DEV))
        return jnp.moveaxis(g, -2, 0).reshape(N_DEV, -1)
    if name in ROW_SHARDED:
        g = grad.reshape((grad.shape[0], N_DEV, grad.shape[1] // N_DEV) + grad.shape[2:])
        return jnp.moveaxis(g, 1, 0).reshape(N_DEV, -1)
    return jnp.broadcast_to(grad.reshape(1, -1), (N_DEV, grad.size))


def _pack(pieces, axis, dtype, row_mult):
    n = sum(p.shape[axis] for p in pieces)
    pad = (-n) % (PACK_COLS * row_mult)
    if pad:
        shape = list(pieces[0].shape)
        shape[axis] = pad
        pieces = list(pieces) + [jnp.zeros(shape, dtype)]
    flat = jnp.concatenate(pieces, axis=axis)
    return flat.reshape(flat.shape[:axis] + ((n + pad) // PACK_COLS, PACK_COLS))


def _twice(fn):
    return lambda *a: fn(*a) * 2


def _ffn_block(h, hb, p_i, w_in, w_out, w_ple, w_pg, g, b, alpha, tag):
    t, d = h.shape
    hid = w_out.shape[0]
    cw = _tile(hid, (512, 256, 128))
    ncb = hid // cw
    gu = _mm(hb, w_in, name=f"ffn_in_{tag}")
    act = _rowwise(_swiglu_fn, [(gu, cw, 0), (gu, cw, ncb)], [], [(hid, cw, BF16)], name=f"swiglu_{tag}", tq=256, ncol=ncb)[0]
    f = _mm(act, w_out, name=f"ffn_out_{tag}")
    pw = _mm(p_i, w_ple, name=f"ple_{tag}")
    gate = _mm(hb, w_pg, name=f"ple_gate_{tag}")
    out, outb = _rowwise(_twice(_ln_ffn_fn(alpha)), [_whole(h), _whole(f), _whole(pw), _whole(gate)], [g, b],
                         [(d, d, F32), (d, d, BF16)], name=f"ln2_{tag}", tq=256)
    return out, outb, (h, hb, p_i, gu, act, f, pw, gate)


def _ffn_block_bwd(saved, dout, w_in, w_out, w_pg, g, b, alpha, tag):
    h, hb, p_i, gu, act, f, pw, gate = saved
    t, d = h.shape
    hid = w_out.shape[0]
    cw = _tile(hid, (512, 256, 128))
    ncb = hid // cw
    dh_a, df, dpw, dgate, dg, db = _rowwise_bwd(
        _ln_ffn_fn(alpha), [_whole(h), _whole(f), _whole(pw), _whole(gate)], [g, b], [dout],
        need=[True] * 4, drow=[(d, F32), (d, BF16), (d, BF16), (d, BF16)], name=f"ln2_bwd_{tag}", tq=128)
    dw_out = _mm(act, df, ta=True, name=f"ffn_out_dw_{tag}")
    dact = _mm(df, w_out, tb=True, out_dtype=BF16, name=f"ffn_out_dx_{tag}")
    dg_, du_ = _rowwise_bwd(_swiglu_fn, [(gu, cw, 0), (gu, cw, ncb)], [], [[(dact, cw, 0)]], need=[True, True],
                            drow=[(hid, BF16), (hid, BF16)], name=f"swiglu_bwd_{tag}", tq=256, ncol=ncb)
    dgu = jnp.concatenate([dg_, du_], axis=1)
    dw_in = _mm(hb, dgu, ta=True, name=f"ffn_in_dw_{tag}")
    dh_b = _mm(dgu, w_in, tb=True, name=f"ffn_in_dx_{tag}")
    dw_ple = _mm(p_i, dpw, ta=True, name=f"ple_dw_{tag}")
    dw_pg = _mm(hb, dgate, ta=True, name=f"ple_gate_dw_{tag}")
    dh_c = _mm(dgate, w_pg, tb=True, name=f"ple_gate_dx_{tag}")
    return [dh_a, dh_b, dh_c], dict(ffn_w_in=dw_in, ffn_w_out=dw_out, ple_w=dw_ple, ple_gate_w=dw_pg, ln2_g=dg, ln2_b=db)


def kernel(x, p, positions, attn_w_in, mla_q_norm, mla_w_q_b, mla_kv_norm, mla_w_kv_b, attn_w_out, s5_a_re, s5_a_im, s5_log_dt, s5_b_re, s5_b_im, s5_c_re, s5_c_im, s5_d, s5_w_glu, ln1_g, ln1_b, ffn_w_in, ffn_w_out, ple_w, ple_gate_w, ln2_g, ln2_b, loss_target, m_attn_w_in, m_mla_q_norm, m_mla_w_q_b, m_mla_kv_norm, m_mla_w_kv_b, m_attn_w_out, m_s5_a_re, m_s5_a_im, m_s5_log_dt, m_s5_b_re, m_s5_b_im, m_s5_c_re, m_s5_c_im, m_s5_d, m_s5_w_glu, m_ln1_g, m_ln1_b, m_ffn_w_in, m_ffn_w_out, m_ple_w, m_ple_gate_w, m_ln2_g, m_ln2_b, v_attn_w_in, v_mla_q_norm, v_mla_w_q_b, v_mla_kv_norm, v_mla_w_kv_b, v_attn_w_out, v_s5_a_re, v_s5_a_im, v_s5_log_dt, v_s5_b_re, v_s5_b_im, v_s5_c_re, v_s5_c_im, v_s5_d, v_s5_w_glu, v_ln1_g, v_ln1_b, v_ffn_w_in, v_ffn_w_out, v_ple_w, v_ple_gate_w, v_ln2_g, v_ln2_b):
    local = dict(attn_w_in=attn_w_in, mla_q_norm=mla_q_norm, mla_w_q_b=mla_w_q_b, mla_kv_norm=mla_kv_norm,
                 mla_w_kv_b=mla_w_kv_b, attn_w_out=attn_w_out, s5_a_re=s5_a_re, s5_a_im=s5_a_im, s5_log_dt=s5_log_dt,
                 s5_b_re=s5_b_re, s5_b_im=s5_b_im, s5_c_re=s5_c_re, s5_c_im=s5_c_im, s5_d=s5_d, s5_w_glu=s5_w_glu,
                 ln1_g=ln1_g, ln1_b=ln1_b, ffn_w_in=ffn_w_in, ffn_w_out=ffn_w_out, ple_w=ple_w, ple_gate_w=ple_gate_w,
                 ln2_g=ln2_g, ln2_b=ln2_b)
    mom_m = dict(zip(WEIGHTS, (m_attn_w_in, m_mla_q_norm, m_mla_w_q_b, m_mla_kv_norm, m_mla_w_kv_b, m_attn_w_out, m_s5_a_re, m_s5_a_im, m_s5_log_dt, m_s5_b_re, m_s5_b_im, m_s5_c_re, m_s5_c_im, m_s5_d, m_s5_w_glu, m_ln1_g, m_ln1_b, m_ffn_w_in, m_ffn_w_out, m_ple_w, m_ple_gate_w, m_ln2_g, m_ln2_b)))
    mom_v = dict(zip(WEIGHTS, (v_attn_w_in, v_mla_q_norm, v_mla_w_q_b, v_mla_kv_norm, v_mla_w_kv_b, v_attn_w_out, v_s5_a_re, v_s5_a_im, v_s5_log_dt, v_s5_b_re, v_s5_b_im, v_s5_c_re, v_s5_c_im, v_s5_d, v_s5_w_glu, v_ln1_g, v_ln1_b, v_ffn_w_in, v_ffn_w_out, v_ple_w, v_ple_gate_w, v_ln2_g, v_ln2_b)))

    t, d = x.shape[1], x.shape[2]
    depth = ln1_g.shape[0]
    alpha = (2.0 * depth) ** 0.25
    ql, kvl = mla_q_norm.shape[-1], mla_kv_norm.shape[-1]
    nh = mla_w_q_b.shape[-1] * N_DEV // (NOPE_DIM + ROPE_DIM)
    dw = (attn_w_in.shape[-1] * N_DEV - ql - kvl - ROPE_DIM) // 3
    ndh = dw // DIL_HEAD_DIM
    ng = d // S5_GROUP
    assert ql == kvl and (ql + kvl) % LANE == 0 and nh * V_DIM == dw

    sharded = [n for n in WEIGHTS if n in COL_SHARDED or n in ROW_SHARDED]
    def as_words(n):
        if n == "s5_d":
            return lax.bitcast_convert_type(local[n], BF16).reshape(-1)
        if n == "attn_w_in":
            return local[n][0].T.astype(BF16).reshape(-1)
        return local[n].astype(BF16).reshape(-1)

    packed = _pack([as_words(n) for n in sharded], 0, BF16, 16)
    gathered = _all_gather(packed, "gather_weights").reshape(N_DEV, -1)
    full, off = {}, 0
    for n in sharded:
        size = local[n].size * (2 if n == "s5_d" else 1)
        piece = gathered[:, off : off + size]
        off += size
        if n == "attn_w_in":
            full[n] = piece.reshape(N_DEV * local[n].shape[2], d)
            continue
        if n == "s5_d":
            piece = lax.bitcast_convert_type(piece.reshape((N_DEV,) + local[n].shape + (2,)), F32)
        full[n] = _unshard(n, piece.reshape((N_DEV,) + local[n].shape))

    w_in_t = full["attn_w_in"]
    lat = ql + kvl
    w_lat_t = jnp.concatenate([w_in_t[: lat + ROPE_DIM], jnp.zeros((LANE - ROPE_DIM, d), BF16)], axis=0)
    w_dil_t = w_in_t[lat + ROPE_DIM :]
    wq = full["mla_w_q_b"][0].reshape(ql, nh, NOPE_DIM + ROPE_DIM)
    wq_pe = jnp.pad(wq[:, :, NOPE_DIM:], ((0, 0), (0, 0), (0, LANE - ROPE_DIM)))
    wqp = jnp.concatenate([wq[:, :, :NOPE_DIM].reshape(ql, nh * LANE), wq_pe.reshape(ql, nh * LANE)], axis=1)
    wkv = full["mla_w_kv_b"][0].reshape(kvl, nh, NOPE_DIM + V_DIM)
    wkvp = jnp.concatenate([wkv[:, :, :NOPE_DIM].reshape(kvl, nh * LANE), wkv[:, :, NOPE_DIM:].reshape(kvl, nh * LANE)], axis=1)
    w_out = full["attn_w_out"][0]
    w_glu = full["s5_w_glu"][0]
    d_skip = full["s5_d"]

    h0 = x[0]
    h0b = h0.astype(BF16)
    pb = p.astype(BF16)
    target = loss_target[0]
    pos = positions.reshape(t, 1)
    grads = {}

    z = _mm(h0b, w_lat_t, tb=True, name="attn_in_lat")
    zd = _mm(h0b, w_dil_t, tb=True, out_dtype=BF16, name="attn_in_dil")
    gq, gk = mla_q_norm.reshape(1, ql), mla_kv_norm.reshape(1, kvl)
    qn, kvn = _rowwise(_rms_fn, [(z, ql, 0), (z, kvl, 1)], [gq, gk], [(ql, ql, BF16), (kvl, kvl, BF16)], name="rms", tq=256)
    qf = _mm(qn, wqp, name="q_up")
    kvf = _mm(kvn, wkvp, out_dtype=BF16, name="kv_up")
    cos, sin = _rope_tables(pos, "rope_tables")
    pe_cb = lat // LANE
    qpe, kpe = _rowwise(_rope_fwd_fn(nh), [(qf, nh * LANE, 1), (z, LANE, pe_cb), _whole(cos), _whole(sin)], [],
                        [(nh * LANE, nh * LANE, BF16), (LANE, LANE, BF16)], name="rope", tq=256)
    out_a, lse_a = _mla_fwd(qf, qpe, kvf, kpe, nh, "mla_fwd")

    def to_classes(a, dil):
        if dil == 1:
            return a
        return a.reshape(t // dil, dil, a.shape[1]).transpose(1, 0, 2).reshape(t, a.shape[1])

    def from_classes(a, dil):
        if dil == 1:
            return a
        return a.reshape(dil, t // dil, a.shape[1]).transpose(1, 0, 2).reshape(t, a.shape[1])

    band = []
    for window, dil in DIL_BRANCHES:
        assert window // dil == BLK
        zc = to_classes(zd, dil)
        o_c, l_c = _band_fwd(zc, ndh, dil, f"band_fwd_d{dil}")
        band.append((dil, zc, o_c, l_c, from_classes(o_c, dil), from_classes(l_c, dil)))
    merge_rows = [_whole(b[4]) for b in band] + [_whole(b[5]) for b in band]
    out_b = _rowwise(_merge_fn, merge_rows, [], [(dw, dw, BF16)], name="merge", tq=256)[0]
    att = jnp.concatenate([out_a, out_b], axis=1)
    mix0 = _mm(att, w_out, name="attn_out")
    g1, b1 = ln1_g[0:1], ln1_b[0:1]
    h1, h1b = _rowwise(_twice(_ln_mix_fn(alpha)), [_whole(h0), _whole(mix0)], [g1, b1], [(d, d, F32), (d, d, BF16)],
                       name="ln1_l0", tq=256)
    h2, _, saved_f0 = _ffn_block(h1, h1b, pb[0, 0], full["ffn_w_in"][0], full["ffn_w_out"][0], full["ple_w"][0],
                                 full["ple_gate_w"][0], ln2_g[0:1], ln2_b[0:1], alpha, "l0")

    prm = [s5_a_re[0].reshape(ng, 1, S5_STATE), s5_a_im[0].reshape(ng, 1, S5_STATE), s5_log_dt[0].reshape(ng, 1, 1),
           s5_b_re[0].transpose(0, 2, 1), s5_b_im[0].transpose(0, 2, 1), s5_c_re[0], s5_c_im[0]]
    u = _to_groups(h2)
    e_re, e_im, lam_re, lam_im = _s5_local(prm, u, "s5_local")
    lam_re_c, lam_im_c = lam_re.reshape(1, ng * S5_STATE), lam_im.reshape(1, ng * S5_STATE)
    sp_re_c, sp_im_c = _s5_carry(_states_to_cols(e_re), _states_to_cols(e_im), lam_re_c, lam_im_c, "s5_carry")
    sp_re, sp_im = _cols_to_states(sp_re_c, ng), _cols_to_states(sp_im_c, ng)
    ys = _from_groups(_s5_out(prm, u, sp_re, sp_im, "s5_out"), t, d)
    z5 = _rowwise(_gelu_fn, [_whole(ys), _whole(h2)], [d_skip], [(d, d, BF16)], name="gelu", tq=256)[0]
    vg = _mm(z5, w_glu, name="glu_in")
    g3, b3 = ln1_g[1:2], ln1_b[1:2]
    h3, h3b = _rowwise(_twice(_ln_glu_fn(alpha)), [_whole(h2), (vg, d, 0), (vg, d, 1)], [g3, b3],
                       [(d, d, F32), (d, d, BF16)], name="ln1_l1", tq=256)
    h4, _, saved_f1 = _ffn_block(h3, h3b, pb[1, 0], full["ffn_w_in"][1], full["ffn_w_out"][1], full["ple_w"][1],
                                 full["ple_gate_w"][1], ln2_g[1:2], ln2_b[1:2], alpha, "l1")

    dh4, loss_acc = _loss_kernel(h4, target, "loss")
    loss = lax.psum(loss_acc[0, 0], AXES)

    dh3, gf1 = _ffn_block_bwd(saved_f1, [_whole(dh4)], full["ffn_w_in"][1], full["ffn_w_out"][1],
                              full["ple_gate_w"][1], ln2_g[1:2], ln2_b[1:2], alpha, "l1")
    dh2_a, dval, dgate, dg3, db3 = _rowwise_bwd(
        _ln_glu_fn(alpha), [_whole(h2), (vg, d, 0), (vg, d, 1)], [g3, b3], [[_whole(a) for a in dh3]],
        need=[True] * 3, drow=[(d, F32), (d, BF16), (d, BF16)], name="ln1_bwd_l1", tq=128)
    dvg = jnp.concatenate([dval, dgate], axis=1)
    grads["s5_w_glu"] = _mm(z5, dvg, ta=True, name="glu_dw")[None]
    dz5 = _mm(dvg, w_glu, tb=True, name="glu_dx")
    dys, dh2_b, dd = _rowwise_bwd(_gelu_fn, [_whole(ys), _whole(h2)], [d_skip], [[_whole(dz5)]], need=[True, True],
                                  drow=[(d, F32), (d, F32)], name="gelu_bwd", tq=128)
    grads["s5_d"] = dd
    dy = _to_groups(dys)
    dsp_re, dsp_im = _s5_bwd_state(prm, dy, "s5_bwd_state")
    g_re_c, g_im_c, dlam_re_c, dlam_im_c = _s5_carry_bwd(_states_to_cols(dsp_re), _states_to_cols(dsp_im), sp_re_c, sp_im_c,
                                                         lam_re_c, lam_im_c, "s5_carry_bwd")
    s5g = _s5_bwd_main(prm, u, sp_re, sp_im, dy, _cols_to_states(g_re_c, ng), _cols_to_states(g_im_c, ng),
                       dlam_re_c.reshape(ng, 1, S5_STATE), dlam_im_c.reshape(ng, 1, S5_STATE), "s5_bwd_main")
    grads["s5_a_re"] = s5g[0].reshape(s5_a_re.shape)
    grads["s5_a_im"] = s5g[1].reshape(s5_a_im.shape)
    grads["s5_log_dt"] = s5g[2].reshape(s5_log_dt.shape)
    grads["s5_b_re"] = s5g[3].transpose(0, 2, 1)[None]
    grads["s5_b_im"] = s5g[4].transpose(0, 2, 1)[None]
    grads["s5_c_re"] = s5g[5][None]
    grads["s5_c_im"] = s5g[6][None]
    dh2_c = _from_groups(s5g[7], t, d)

    dh1, gf0 = _ffn_block_bwd(saved_f0, [_whole(dh2_a), _whole(dh2_b), _whole(dh2_c)], full["ffn_w_in"][0],
                              full["ffn_w_out"][0], full["ple_gate_w"][0], ln2_g[0:1], ln2_b[0:1], alpha, "l0")
    for k in ("ffn_w_in", "ffn_w_out", "ple_w", "ple_gate_w", "ln2_g", "ln2_b"):
        grads[k] = jnp.stack([gf0[k], gf1[k]]) if gf0[k].ndim == 2 and gf0[k].shape[0] != 1 else jnp.concatenate([gf0[k], gf1[k]])
    dh0_a, dmix, dg1, db1 = _rowwise_bwd(_ln_mix_fn(alpha), [_whole(h0), _whole(mix0)], [g1, b1], [[_whole(a) for a in dh1]],
                                         need=[True, True], drow=[(d, F32), (d, BF16)], name="ln1_bwd_l0", tq=128)
    grads["ln1_g"] = jnp.concatenate([dg1, dg3])
    grads["ln1_b"] = jnp.concatenate([db1, db3])
    grads["attn_w_out"] = _mm(att, dmix, ta=True, name="attn_out_dw")[None]
    datt = _mm(dmix, w_out, tb=True, name="attn_out_dx")

    dmerge = _rowwise_bwd(_merge_fn, merge_rows, [], [[(datt, dw, 1)]], need=[True] * 6, drow=[(dw, F32)] * 6,
                          name="merge_bwd", tq=128)
    dq_s = dk_s = dv_s = None
    for k, (dil, zc, o_c, l_c, _, _) in enumerate(band):
        do_c, dl_c = to_classes(dmerge[k], dil), to_classes(dmerge[3 + k], dil)
        dq_c, dk_c, dv_c = _band_bwd(zc, o_c, l_c, do_c, dl_c, ndh, dil, f"band_bwd_d{dil}")
        dq_n, dk_n, dv_n = from_classes(dq_c, dil), from_classes(dk_c, dil), from_classes(dv_c, dil)
        dq_s = dq_n if dq_s is None else dq_s + dq_n
        dk_s = dk_n if dk_s is None else dk_s + dk_n
        dv_s = dv_n if dv_s is None else dv_s + dv_n

    dqn, dqp, delta = _mla_bwd_q(qf, qpe, kvf, kpe, datt, out_a, lse_a, nh, "mla_bwd_q")
    to_row = lambda a: a[:, ::LANE].T.reshape(nh, 1, t)
    dkn, dkp, dv = _mla_bwd_kv(qf[:, : nh * LANE].astype(BF16), qpe, kvf, kpe, datt[:, : nh * LANE].astype(BF16),
                               to_row(lse_a), to_row(delta), nh, "mla_bwd_kv")
    dq_pe, dk_pe = _rowwise(_rope_bwd_fn(nh), [_whole(dqp), _whole(dkp), _whole(cos), _whole(sin)], [],
                            [(nh * LANE, nh * LANE, BF16), (LANE, LANE, BF16)], name="rope_bwd", tq=256)
    dqf = jnp.concatenate([dqn.astype(BF16), dq_pe], axis=1)
    dkvf = jnp.concatenate([dkn, dv], axis=1).astype(BF16)
    dwqp = _mm(qn, dqf, ta=True, name="q_up_dw")
    dqn_in = _mm(dqf, wqp, tb=True, name="q_up_dx")
    dwkvp = _mm(kvn, dkvf, ta=True, name="kv_up_dw")
    dkvn_in = _mm(dkvf, wkvp, tb=True, name="kv_up_dx")
    dql, dkvl, dgq, dgk = _rowwise_bwd(_rms_fn, [(z, ql, 0), (z, kvl, 1)], [gq, gk], [[_whole(dqn_in)], [_whole(dkvn_in)]],
                                       need=[True, True], drow=[(ql, BF16), (kvl, BF16)], name="rms_bwd", tq=256)
    grads["mla_q_norm"], grads["mla_kv_norm"] = dgq, dgk
    dz_lat = jnp.concatenate([dql, dkvl, dk_pe], axis=1)
    dz_dil = jnp.concatenate([dq_s, dk_s, dv_s], axis=1).astype(BF16)
    dw_lat_t = _mm(dz_lat, h0b, ta=True, name="attn_in_lat_dw")
    dw_dil_t = _mm(dz_dil, h0b, ta=True, name="attn_in_dil_dw")
    dh0_b = _mm(dz_lat, w_lat_t, name="attn_in_lat_dx")
    dh0_c = _mm(dz_dil, w_dil_t, name="attn_in_dil_dx")
    grad_x = _addn([dh0_a, dh0_b, dh0_c], "grad_x")[None]
    grads["attn_w_in"] = jnp.concatenate([dw_lat_t[: lat + ROPE_DIM], dw_dil_t], axis=0)
    dwq_n = dwqp[:, : nh * LANE].reshape(ql, nh, NOPE_DIM)
    dwq_r = dwqp[:, nh * LANE :].reshape(ql, nh, LANE)[:, :, :ROPE_DIM]
    grads["mla_w_q_b"] = jnp.concatenate([dwq_n, dwq_r], axis=2).reshape(1, ql, nh * (NOPE_DIM + ROPE_DIM))
    dwkv_k = dwkvp[:, : nh * LANE].reshape(kvl, nh, NOPE_DIM)
    dwkv_v = dwkvp[:, nh * LANE :].reshape(kvl, nh, V_DIM)
    grads["mla_w_kv_b"] = jnp.concatenate([dwkv_k, dwkv_v], axis=2).reshape(1, kvl, nh * (NOPE_DIM + V_DIM))

    def owner_rows(n):
        return grads[n].reshape(N_DEV, -1) if n == "attn_w_in" else _by_owner(n, grads[n])

    send_big = _pack([owner_rows(n).astype(BF16) for n in WEIGHTS if n in BIG_WEIGHTS], 1, BF16, 16)
    send_small = _pack([owner_rows(n).astype(F32) for n in WEIGHTS if n not in BIG_WEIGHTS], 1, F32, 8)
    summed = {True: _sum_slots(_all_to_all(send_big, "exchange_grads_big"), "sum_grads_big").reshape(-1),
              False: _sum_slots(_all_to_all(send_small, "exchange_grads_small"), "sum_grads_small").reshape(-1)}
    offs = {True: 0, False: 0}
    g_out, d_out, m_out, v_out = [], [], [], []
    for n in WEIGHTS:
        size, is_big = local[n].size, n in BIG_WEIGHTS
        g = summed[is_big][offs[is_big] : offs[is_big] + size]
        offs[is_big] += size
        if n == "attn_w_in":
            g = g.reshape(local[n].shape[2], d).T[None]
        else:
            g = g.reshape(local[n].shape)
        dlt, mn, vn = _adamw(local[n], g, mom_m[n], mom_v[n], f"adamw_{n}")
        g_out.append(g)
        d_out.append(dlt)
        m_out.append(mn)
        v_out.append(vn)
    return (loss, grad_x, *g_out, *d_out, *m_out, *v_out)
```

```python
import functools
import math

import numpy as np
import jax
import jax.numpy as jnp
from jax import lax
from jax.experimental import pallas as pl
from jax.experimental.pallas import tpu as pltpu

F32 = jnp.float32
BF16 = jnp.bfloat16

NOPE_DIM, ROPE_DIM, V_DIM = 128, 64, 128
DIL_HEAD_DIM = 128
DIL_BRANCHES = ((128, 1), (512, 4), (2048, 16))
BLK = 128
LANE = 128
ROPE_THETA = 10000.0
S5_GROUP, S5_STATE = 16, 64
S5_CHUNK = 32
NEG = -1e30
ADAM_LR, ADAM_B1, ADAM_B2, ADAM_EPS, ADAM_WD, ADAM_STEP = 0.001, 0.9, 0.999, 1e-08, 0.01, 10
N_DEV = 8
AXES = ("x", "y", "c")
VMEM_LIMIT = 56 * 1024 * 1024
MM_VMEM_BUDGET = 40 * 1024 * 1024
PACK_COLS = 1024
HI = lax.Precision.HIGHEST


def _tile(n, cands):
    for c in cands:
        if n % c == 0:
            return c
    return n


def _params(sem=None):
    return pltpu.CompilerParams(dimension_semantics=sem, vmem_limit_bytes=VMEM_LIMIT)


def _dot(a, b, dims, prec=None):
    return lax.dot_general(a, b, (dims, ((), ())), preferred_element_type=F32, precision=prec)


NN = ((1,), (0,))
NT = ((1,), (1,))
TN = ((0,), (0,))


def _mm(a, b, *, ta=False, tb=False, out_dtype=F32, name):
    (k, m) = a.shape if ta else a.shape[::-1]
    (n, k2) = b.shape if tb else b.shape[::-1]
    assert k == k2, (a.shape, b.shape, ta, tb)
    tm = _tile(m, (1024, 512, 256, 128))
    tn = _tile(n, (1024, 512, 384, 256, 128))
    sa, sb, so = a.dtype.itemsize, b.dtype.itemsize, jnp.dtype(out_dtype).itemsize

    def fits(tk):
        return 2 * (tm * tk * sa + tk * tn * sb) + 2 * tm * tn * so + 4 * tm * tn <= MM_VMEM_BUDGET

    tk = next((c for c in (2048, 1536, 1408, 1152, 1024, 768, 512, 384, 256, 128) if k % c == 0 and fits(c)), k)
    nk = k // tk
    dims = (((0,) if ta else (1,)), ((1,) if tb else (0,)))

    def body(a_ref, b_ref, o_ref, *acc):
        part = _dot(a_ref[...].astype(BF16), b_ref[...].astype(BF16), dims)
        if nk == 1:
            o_ref[...] = part.astype(o_ref.dtype)
            return
        (acc_ref,) = acc
        kk = pl.program_id(2)

        @pl.when(kk == 0)
        def _():
            acc_ref[...] = part

        @pl.when(kk > 0)
        def _():
            acc_ref[...] += part

        @pl.when(kk == nk - 1)
        def _():
            o_ref[...] = acc_ref[...].astype(o_ref.dtype)

    a_spec = pl.BlockSpec((tk, tm), lambda i, j, kk: (kk, i)) if ta else pl.BlockSpec((tm, tk), lambda i, j, kk: (i, kk))
    b_spec = pl.BlockSpec((tn, tk), lambda i, j, kk: (j, kk)) if tb else pl.BlockSpec((tk, tn), lambda i, j, kk: (kk, j))
    return pl.pallas_call(
        body,
        name=name,
        grid=(m // tm, n // tn, nk),
        in_specs=[a_spec, b_spec],
        out_specs=pl.BlockSpec((tm, tn), lambda i, j, kk: (i, j)),
        out_shape=jax.ShapeDtypeStruct((m, n), out_dtype),
        scratch_shapes=[pltpu.VMEM((tm, tn), F32)] if nk > 1 else [],
        compiler_params=_params(("parallel", "parallel", "arbitrary")),
    )(a, b)


def _row_spec(spec, tq):
    _, w, cb = spec
    return pl.BlockSpec((tq, w), lambda i, j: (i, cb + j))


def _full_spec(p):
    return pl.BlockSpec(p.shape, lambda i, j: (0,) * p.ndim)


def _rowwise(fn, rows, pars, outs, *, name, tq, ncol=1):
    t = rows[0][0].shape[0]
    tq = _tile(t, (tq, 128, 64, 32, 16, 8))
    nr, npar = len(rows), len(pars)

    def body(*refs):
        vals = [r[...].astype(F32) for r in refs[: nr + npar]]
        res = fn(*vals)
        for o, r in zip(refs[nr + npar :], res):
            o[...] = r.astype(o.dtype)

    res = pl.pallas_call(
        body,
        name=name,
        grid=(t // tq, ncol),
        in_specs=[_row_spec(s, tq) for s in rows] + [_full_spec(p) for p in pars],
        out_specs=[pl.BlockSpec((tq, w), lambda i, j: (i, j)) for (_, w, _) in outs],
        out_shape=[jax.ShapeDtypeStruct((t, wt), dt) for (wt, _, dt) in outs],
        compiler_params=_params(("parallel", "parallel")),
    )(*[s[0] for s in rows], *pars)
    return res


def _rowwise_bwd(fn, rows, pars, cots, *, need, drow, name, tq, ncol=1):
    t = rows[0][0].shape[0]
    tq = _tile(t, (tq, 128, 64, 32, 16, 8))
    nr, npar = len(rows), len(pars)
    assert ncol == 1 or npar == 0
    flat_cots = [s for c in cots for s in c]
    ncot = len(flat_cots)
    want = [k for k in range(nr) if need[k]]

    def body(*refs):
        vals = [r[...].astype(F32) for r in refs[: nr + npar]]
        cref = refs[nr + npar : nr + npar + ncot]
        oref = refs[nr + npar + ncot :]
        cvals, pos = [], 0
        for c in cots:
            acc = cref[pos][...].astype(F32)
            for q in range(1, len(c)):
                acc = acc + cref[pos + q][...].astype(F32)
            cvals.append(acc)
            pos += len(c)

        def closed(*diff):
            full = list(vals)
            for k, dv in zip(want + list(range(nr, nr + npar)), diff):
                full[k] = dv
            return tuple(fn(*full))

        diff_in = [vals[k] for k in want] + vals[nr:]
        _, vjp = jax.vjp(closed, *diff_in)
        grads = vjp(tuple(cvals))
        for q in range(len(want)):
            oref[q][...] = grads[q].astype(oref[q].dtype)
        if npar:
            first = pl.program_id(0) == 0

            @pl.when(first)
            def _():
                for q in range(npar):
                    oref[len(want) + q][...] = jnp.zeros_like(oref[len(want) + q])

            for q in range(npar):
                oref[len(want) + q][...] += grads[len(want) + q]

    out_specs = [pl.BlockSpec((tq, rows[k][1]), lambda i, j, cb=rows[k][2]: (i, j)) for k in want]
    out_specs += [_full_spec(p) for p in pars]
    out_shape = [jax.ShapeDtypeStruct((t, wt), dt) for (wt, dt) in drow]
    out_shape += [jax.ShapeDtypeStruct(p.shape, F32) for p in pars]
    return pl.pallas_call(
        body,
        name=name,
        grid=(t // tq, ncol),
        in_specs=[_row_spec(s, tq) for s in rows] + [_full_spec(p) for p in pars] + [_row_spec(s, tq) for s in flat_cots],
        out_specs=out_specs,
        out_shape=out_shape,
        compiler_params=_params(("arbitrary", "arbitrary") if npar else ("parallel", "parallel")),
    )(*[s[0] for s in rows], *pars, *[s[0] for s in flat_cots])


def _whole(a, w=None):
    return (a, a.shape[1] if w is None else w, 0)


def _rms_fn(ql, kvl, gq, gk):
    def one(x, g):
        return x * lax.rsqrt(jnp.mean(x * x, -1, keepdims=True) + 1e-6) * g

    return one(ql, gq), one(kvl, gk)


def _layernorm(s, g, b):
    mu = jnp.mean(s, -1, keepdims=True)
    d = s - mu
    var = jnp.mean(d * d, -1, keepdims=True)
    return d * lax.rsqrt(var + 1e-5) * g + b


def _ln_mix_fn(alpha):
    def fn(h, mix, g, b):
        return (_layernorm(alpha * h + mix, g, b),)

    return fn


def _ln_glu_fn(alpha):
    def fn(h, val, gate, g, b):
        return (_layernorm(alpha * h + val * jax.nn.sigmoid(gate), g, b),)

    return fn


def _ln_ffn_fn(alpha):
    def fn(h, f, pw, gate, g, b):
        return (_layernorm(alpha * h + f + pw * jax.nn.sigmoid(gate), g, b),)

    return fn


def _swiglu_fn(g, u):
    return (jax.nn.silu(g) * u,)


def _gelu_fn(ys, h, d):
    return (jax.nn.gelu(ys + d * h),)


def _merge_fn(o1, o2, o3, l1, l2, l3):
    m = jnp.maximum(jnp.maximum(l1, l2), l3)
    e1, e2, e3 = jnp.exp(l1 - m), jnp.exp(l2 - m), jnp.exp(l3 - m)
    return ((e1 * o1 + e2 * o2 + e3 * o3) / (e1 + e2 + e3),)


def _swap_halves(t):
    w = t.shape[1]
    lane = lax.broadcasted_iota(jnp.int32, t.shape, 1) % LANE
    up = jnp.where(lane < ROPE_DIM, pltpu.roll(t, ROPE_DIM // 2, 1), 0.0)
    return jnp.where(lane < ROPE_DIM // 2, pltpu.roll(t, w - ROPE_DIM // 2, 1), up)


def _rope_fwd_fn(nh):
    def fn(qpe, kpe, cos, sin):
        cq, sq = jnp.tile(cos, (1, nh)), jnp.tile(sin, (1, nh))
        return qpe * cq + _swap_halves(qpe) * sq, kpe * cos + _swap_halves(kpe) * sin

    return fn


def _rope_bwd_fn(nh):
    def fn(dq, dk_heads, cos, sin):
        cq, sq = jnp.tile(cos, (1, nh)), jnp.tile(sin, (1, nh))
        dk = dk_heads[:, :LANE]
        for h in range(1, nh):
            dk = dk + dk_heads[:, h * LANE : (h + 1) * LANE]
        return dq * cq + _swap_halves(dq * sq), dk * cos + _swap_halves(dk * sin)

    return fn


def _rope_tables(positions, name):
    t = positions.shape[0]
    tq = _tile(t, (512, 128, 8))
    half = ROPE_DIM // 2

    def body(p_ref, c_ref, s_ref):
        lane = lax.broadcasted_iota(jnp.int32, (tq, LANE), 1)
        idx = (lane % half).astype(F32)
        inv_freq = jnp.exp(idx * (-math.log(ROPE_THETA) / half))
        ang = p_ref[...].astype(F32) * inv_freq
        live = lane < ROPE_DIM
        c_ref[...] = jnp.where(live, jnp.cos(ang), 0.0)
        s_ref[...] = jnp.where(live, jnp.where(lane < half, -jnp.sin(ang), jnp.sin(ang)), 0.0)

    return pl.pallas_call(
        body,
        name=name,
        grid=(t // tq,),
        in_specs=[pl.BlockSpec((tq, 1), lambda i: (i, 0))],
        out_specs=[pl.BlockSpec((tq, LANE), lambda i: (i, 0))] * 2,
        out_shape=[jax.ShapeDtypeStruct((t, LANE), F32)] * 2,
        compiler_params=_params(("parallel",)),
    )(positions)


def _loss_kernel(y, target, name):
    t, d = y.shape
    tq = _tile(t, (256, 128, 8))

    def body(y_ref, t_ref, dy_ref, l_ref):
        @pl.when(pl.program_id(0) == 0)
        def _():
            l_ref[...] = jnp.zeros_like(l_ref)

        e = y_ref[...] - t_ref[...]
        dy_ref[...] = e * (1.0 / d)
        l_ref[...] += jnp.sum(e * e) * (0.5 / d)

    return pl.pallas_call(
        body,
        name=name,
        grid=(t // tq,),
        in_specs=[pl.BlockSpec((tq, d), lambda i: (i, 0))] * 2,
        out_specs=[pl.BlockSpec((tq, d), lambda i: (i, 0)), pl.BlockSpec((8, LANE), lambda i: (0, 0))],
        out_shape=[jax.ShapeDtypeStruct((t, d), F32), jax.ShapeDtypeStruct((8, LANE), F32)],
        compiler_params=_params(("arbitrary",)),
    )(y, target)


def _addn(arrs, name):
    w = arrs[0].shape[1]
    return _rowwise(lambda *v: (functools.reduce(lambda p, q: p + q, v),), [_whole(a) for a in arrs], [], [(w, w, F32)],
                    name=name, tq=256)[0]


def _mla_tiles(t):
    tq = _tile(t, (512, 256, 128))
    return tq, t // tq


def _call_with_side(body, side, n_in, grid, *, name, in_specs, out_specs, out_shape):
    if side is None:
        return pl.pallas_call(body, name=name, grid=grid, in_specs=in_specs, out_specs=out_specs, out_shape=out_shape,
                              compiler_params=_params(("parallel", "parallel")))
    n_out = len(out_specs)

    def wrapped(*refs):
        ins, outs = refs[:n_in], refs[n_in + 1 : n_in + 1 + n_out]
        done = _run_side(side, (refs[n_in], refs[n_in + 1 + n_out], *refs[n_in + 2 + n_out :]), grid[0], grid[1])
        body(*ins, *outs)
        done()

    call = pl.pallas_call(wrapped, name=name, grid=grid, in_specs=in_specs + [ANY_SPEC], out_specs=out_specs + [ANY_SPEC],
                          out_shape=out_shape + [_side_out_shape(side)], scratch_shapes=COMM_SEMS,
                          compiler_params=_params(("arbitrary", "arbitrary")))
    return lambda *args: call(*args, side[1])


def _mla_fwd(qf, qpe, kvf, kpe, nh, name, side=None):
    t = qf.shape[0]
    tq, nq = _mla_tiles(t)
    scale = (NOPE_DIM + ROPE_DIM) ** -0.5

    def body(qn_ref, qp_ref, kn_ref, kp_ref, v_ref, o_ref, lse_ref):
        i = pl.program_id(1)
        qn = qn_ref[...].astype(BF16)
        qp = qp_ref[...]

        def step(j, carry, masked):
            m, l, acc = carry
            ks = pl.ds(pl.multiple_of(j * tq, tq), tq)
            s = (_dot(qn, kn_ref[ks, :], NT) + _dot(qp, kp_ref[ks, :], NT)) * scale
            if masked:
                row = lax.broadcasted_iota(jnp.int32, (tq, tq), 0)
                col = lax.broadcasted_iota(jnp.int32, (tq, tq), 1)
                s = jnp.where(col <= row, s, NEG)
            m_new = jnp.maximum(m, jnp.max(s, -1, keepdims=True))
            p = jnp.exp(s - m_new)
            a = jnp.exp(m - m_new)
            return m_new, a * l + jnp.sum(p, -1, keepdims=True), a * acc + _dot(p.astype(BF16), v_ref[ks, :], NN)

        init = (jnp.full((tq, 1), NEG, F32), jnp.zeros((tq, 1), F32), jnp.zeros((tq, V_DIM), F32))
        carry = lax.fori_loop(0, i, lambda j, c: step(j, c, False), init)
        m, l, acc = step(i, carry, True)
        o_ref[...] = (acc / l).astype(o_ref.dtype)
        lse_ref[...] = jnp.broadcast_to(m + jnp.log(l), (tq, LANE))

    blk = lambda h, i: (i, h)
    return _call_with_side(
        body, side, 5, (nh, nq),
        name=name,
        in_specs=[
            pl.BlockSpec((tq, LANE), blk),
            pl.BlockSpec((tq, LANE), blk),
            pl.BlockSpec((t, LANE), lambda h, i: (0, h)),
            pl.BlockSpec((t, LANE), lambda h, i: (0, 0)),
            pl.BlockSpec((t, LANE), lambda h, i: (0, nh + h)),
        ],
        out_specs=[pl.BlockSpec((tq, LANE), blk), pl.BlockSpec((tq, LANE), blk)],
        out_shape=[jax.ShapeDtypeStruct((t, nh * LANE), BF16), jax.ShapeDtypeStruct((t, nh * LANE), F32)],
    )(qf, qpe, kvf, kpe, kvf)


def _mla_bwd_q(qf, qpe, kvf, kpe, do, o, lse, nh, name, side=None):
    t = qf.shape[0]
    tq, nq = _mla_tiles(t)
    scale = (NOPE_DIM + ROPE_DIM) ** -0.5

    def body(qn_ref, qp_ref, kn_ref, kp_ref, v_ref, do_ref, o_ref, lse_ref, dqn_ref, dqp_ref, dl_ref):
        i = pl.program_id(1)
        qn = qn_ref[...].astype(BF16)
        qp = qp_ref[...]
        dof = do_ref[...].astype(F32)
        dob = dof.astype(BF16)
        delta = jnp.sum(dof * o_ref[...].astype(F32), -1, keepdims=True)
        lse1 = lse_ref[:, :1]

        def step(j, carry, masked):
            dqn, dqp = carry
            ks = pl.ds(pl.multiple_of(j * tq, tq), tq)
            kn, kp = kn_ref[ks, :], kp_ref[ks, :]
            s = (_dot(qn, kn, NT) + _dot(qp, kp, NT)) * scale
            p = jnp.exp(s - lse1)
            if masked:
                row = lax.broadcasted_iota(jnp.int32, (tq, tq), 0)
                col = lax.broadcasted_iota(jnp.int32, (tq, tq), 1)
                p = jnp.where(col <= row, p, 0.0)
            dp = _dot(dob, v_ref[ks, :], NT)
            ds = (p * (dp - delta) * scale).astype(BF16)
            return dqn + _dot(ds, kn, NN), dqp + _dot(ds, kp, NN)

        init = (jnp.zeros((tq, LANE), F32), jnp.zeros((tq, LANE), F32))
        carry = lax.fori_loop(0, i, lambda j, c: step(j, c, False), init)
        dqn, dqp = step(i, carry, True)
        dqn_ref[...] = dqn
        dqp_ref[...] = dqp
        dl_ref[...] = jnp.broadcast_to(delta, (tq, LANE))

    blk = lambda h, i: (i, h)
    bs = pl.BlockSpec((tq, LANE), blk)
    return _call_with_side(
        body, side, 8, (nh, nq),
        name=name,
        in_specs=[
            bs,
            bs,
            pl.BlockSpec((t, LANE), lambda h, i: (0, h)),
            pl.BlockSpec((t, LANE), lambda h, i: (0, 0)),
            pl.BlockSpec((t, LANE), lambda h, i: (0, nh + h)),
            bs,
            bs,
            bs,
        ],
        out_specs=[bs, bs, bs],
        out_shape=[jax.ShapeDtypeStruct((t, nh * LANE), F32)] * 3,
    )(qf, qpe, kvf, kpe, kvf, do, o, lse)


def _mla_bwd_kv(qn16, qpe, kvf, kpe, do16, lse_row, delta_row, nh, name, side=None):
    t = qn16.shape[0]
    tq, nq = _mla_tiles(t)
    scale = (NOPE_DIM + ROPE_DIM) ** -0.5

    def body(kn_ref, kp_ref, v_ref, qn_ref, qp_ref, do_ref, lse_ref, dl_ref, dkn_ref, dkp_ref, dv_ref):
        j = pl.program_id(1)
        kn, kp, v = kn_ref[...], kp_ref[...], v_ref[...]

        def step(i, carry, masked):
            dkn, dkp, dv = carry
            qs = pl.ds(pl.multiple_of(i * tq, tq), tq)
            qn, qp, dob = qn_ref[qs, :], qp_ref[qs, :], do_ref[qs, :]
            st = (_dot(kn, qn, NT) + _dot(kp, qp, NT)) * scale
            pt = jnp.exp(st - lse_ref[0, :, qs])
            if masked:
                key = lax.broadcasted_iota(jnp.int32, (tq, tq), 0)
                qry = lax.broadcasted_iota(jnp.int32, (tq, tq), 1)
                pt = jnp.where(key <= qry, pt, 0.0)
            dv = dv + _dot(pt.astype(BF16), dob, NN)
            dpt = _dot(v, dob, NT)
            dst = (pt * (dpt - dl_ref[0, :, qs]) * scale).astype(BF16)
            return dkn + _dot(dst, qn, NN), dkp + _dot(dst, qp, NN), dv

        z = jnp.zeros((tq, LANE), F32)
        carry = step(j, (z, z, z), True)
        dkn, dkp, dv = lax.fori_loop(j + 1, nq, lambda i, c: step(i, c, False), carry)
        dkn_ref[...] = dkn
        dkp_ref[...] = dkp
        dv_ref[...] = dv

    blk = pl.BlockSpec((tq, LANE), lambda h, j: (j, h))
    res = lambda f: pl.BlockSpec((t, LANE), f)
    row = pl.BlockSpec((1, 1, t), lambda h, j: (h, 0, 0))
    return _call_with_side(
        body, side, 8, (nh, nq),
        name=name,
        in_specs=[
            blk,
            pl.BlockSpec((tq, LANE), lambda h, j: (j, 0)),
            pl.BlockSpec((tq, LANE), lambda h, j: (j, nh + h)),
            res(lambda h, j: (0, h)),
            res(lambda h, j: (0, h)),
            res(lambda h, j: (0, h)),
            row,
            row,
        ],
        out_specs=[blk, blk, blk],
        out_shape=[jax.ShapeDtypeStruct((t, nh * LANE), F32)] * 3,
    )(kvf, kpe, kvf, qn16, qpe, do16, lse_row, delta_row)


def _alibi_slopes(n):
    return [float(2.0 ** (-8.0 * i / n)) for i in range(1, n + 1)]


def _band_masks(has_prev):
    qi = lax.broadcasted_iota(jnp.int32, (BLK, BLK), 0)
    ki = lax.broadcasted_iota(jnp.int32, (BLK, BLK), 1)
    return ki <= qi, (ki >= qi) & has_prev, (qi - ki).astype(F32), (qi - ki + BLK).astype(F32)


def _band_fwd(zc, nh, dil, name):
    t = zc.shape[0]
    nb = t // BLK
    nbc = nb // dil
    scale = DIL_HEAD_DIM ** -0.5
    slopes = _alibi_slopes(nh)
    dw = nh * LANE

    def body(q_ref, kc_ref, kp_ref, vc_ref, vp_ref, o_ref, l_ref):
        b = pl.program_id(0)
        mask_c, mask_p, dist_c, dist_p = _band_masks((b % nbc) > 0)
        for h in range(nh):
            sl = slice(h * LANE, (h + 1) * LANE)
            q = q_ref[:, sl]
            bias = slopes[h] * dil
            sc = jnp.where(mask_c, _dot(q, kc_ref[:, sl], NT) * scale - bias * dist_c, NEG)
            sp = jnp.where(mask_p, _dot(q, kp_ref[:, sl], NT) * scale - bias * dist_p, NEG)
            m = jnp.maximum(jnp.max(sc, -1, keepdims=True), jnp.max(sp, -1, keepdims=True))
            ec, ep = jnp.exp(sc - m), jnp.exp(sp - m)
            l = jnp.sum(ec, -1, keepdims=True) + jnp.sum(ep, -1, keepdims=True)
            inv = 1.0 / l
            o = _dot((ec * inv).astype(BF16), vc_ref[:, sl], NN) + _dot((ep * inv).astype(BF16), vp_ref[:, sl], NN)
            o_ref[:, sl] = o
            l_ref[:, sl] = jnp.broadcast_to(m + jnp.log(l), (BLK, LANE))

    prev = lambda b: jnp.maximum(b - 1, 0)
    return pl.pallas_call(
        body,
        name=name,
        grid=(nb,),
        in_specs=[
            pl.BlockSpec((BLK, dw), lambda b: (b, 0)),
            pl.BlockSpec((BLK, dw), lambda b: (b, 1)),
            pl.BlockSpec((BLK, dw), lambda b: (prev(b), 1)),
            pl.BlockSpec((BLK, dw), lambda b: (b, 2)),
            pl.BlockSpec((BLK, dw), lambda b: (prev(b), 2)),
        ],
        out_specs=[pl.BlockSpec((BLK, dw), lambda b: (b, 0))] * 2,
        out_shape=[jax.ShapeDtypeStruct((t, dw), F32)] * 2,
        compiler_params=_params(("parallel",)),
    )(zc, zc, zc, zc, zc)


def _band_bwd(zc, o, lse, do, dl, nh, dil, name):
    t = zc.shape[0]
    nb = t // BLK
    nbc = nb // dil
    scale = DIL_HEAD_DIM ** -0.5
    slopes = _alibi_slopes(nh)
    dw = nh * LANE

    def body(q_ref, k_ref, v_ref, kp_ref, vp_ref, qn_ref, o_ref, l_ref, do_ref, dl_ref, on_ref, ln_ref, don_ref, dln_ref,
             dq_ref, dk_ref, dv_ref):
        b = pl.program_id(0)
        has_prev = (b % nbc) > 0
        has_next = jnp.logical_and(((b + 1) % nbc) > 0, b + 1 < nb)
        mask_c, mask_p, dist_c, dist_p = _band_masks(has_prev)
        _, mask_n, _, _ = _band_masks(has_next)
        for h in range(nh):
            sl = slice(h * LANE, (h + 1) * LANE)
            bias = slopes[h] * dil
            q, k, v = q_ref[:, sl], k_ref[:, sl], v_ref[:, sl]
            kp, vp, qn = kp_ref[:, sl], vp_ref[:, sl], qn_ref[:, sl]
            dof, donf = do_ref[:, sl], don_ref[:, sl]
            dob, donb = dof.astype(BF16), donf.astype(BF16)
            lse1, lsen1 = l_ref[:, sl][:, :1], ln_ref[:, sl][:, :1]
            adj = jnp.sum(dl_ref[:, sl] - dof * o_ref[:, sl], -1, keepdims=True)
            adjn = jnp.sum(dln_ref[:, sl] - donf * on_ref[:, sl], -1, keepdims=True)
            pa = jnp.where(mask_c, jnp.exp(_dot(q, k, NT) * scale - bias * dist_c - lse1), 0.0)
            pb = jnp.where(mask_p, jnp.exp(_dot(q, kp, NT) * scale - bias * dist_p - lse1), 0.0)
            dsa = (pa * (_dot(dob, v, NT) + adj)).astype(BF16)
            dsb = (pb * (_dot(dob, vp, NT) + adj)).astype(BF16)
            dq_ref[:, sl] = (_dot(dsa, k, NN) + _dot(dsb, kp, NN)) * scale
            pc = jnp.where(mask_n, jnp.exp(_dot(qn, k, NT) * scale - bias * dist_p - lsen1), 0.0)
            dsc = (pc * (_dot(donb, v, NT) + adjn)).astype(BF16)
            dk_ref[:, sl] = (_dot(dsa, q, TN) + _dot(dsc, qn, TN)) * scale
            dv_ref[:, sl] = _dot(pa.astype(BF16), dob, TN) + _dot(pc.astype(BF16), donb, TN)

    prev = lambda b: jnp.maximum(b - 1, 0)
    nxt = lambda b: jnp.minimum(b + 1, nb - 1)
    spec = lambda f: pl.BlockSpec((BLK, dw), f)
    cur0, cur1, cur2 = spec(lambda b: (b, 0)), spec(lambda b: (b, 1)), spec(lambda b: (b, 2))
    nxt0 = spec(lambda b: (nxt(b), 0))
    return pl.pallas_call(
        body,
        name=name,
        grid=(nb,),
        in_specs=[cur0, cur1, cur2, spec(lambda b: (prev(b), 1)), spec(lambda b: (prev(b), 2)), nxt0,
                  cur0, cur0, cur0, cur0, nxt0, nxt0, nxt0, nxt0],
        out_specs=[cur0] * 3,
        out_shape=[jax.ShapeDtypeStruct((t, dw), F32)] * 3,
        compiler_params=_params(("parallel",)),
    )(zc, zc, zc, zc, zc, zc, o, lse, do, dl, o, lse, do, dl)


def _s5_ops(a_re, a_im, ldt, bt_re, bt_im, c_re, c_im):
    L, g, p = S5_CHUNK, S5_GROUP, S5_STATE
    dt = jnp.exp(ldt)
    lr, li = a_re * dt, a_im * dt
    er = jnp.exp(lr)
    lam_re, lam_im = er * jnp.cos(li), er * jnp.sin(li)
    nr, ni = lam_re - 1.0, lam_im
    den = a_re * a_re + a_im * a_im
    fr, fi = (nr * a_re + ni * a_im) / den, (ni * a_re - nr * a_im) / den
    bb_re, bb_im = fr * bt_re - fi * bt_im, fr * bt_im + fi * bt_re

    def power(tau):
        mag = jnp.exp(tau * lr)
        return mag * jnp.cos(tau * li), mag * jnp.sin(tau * li)

    step = lax.broadcasted_iota(jnp.int32, (L, 1), 0).astype(F32)

    def outer(pr, pi, mr, mi):
        re = pr[:, None, :] * mr[None] - pi[:, None, :] * mi[None]
        im = pr[:, None, :] * mi[None] + pi[:, None, :] * mr[None]
        return re.reshape(L * g, p), im.reshape(L * g, p)

    half = float(L // 2)
    cp_re, cp_im = outer(*power(step - half), c_re, c_im)
    pb_re, pb_im = outer(*power(half - step), bb_re, bb_im)
    toep = _dot(cp_re, pb_re, NT, HI) - _dot(cp_im, pb_im, NT, HI)
    trow = lax.broadcasted_iota(jnp.int32, (L * g, L * g), 0) // g
    scol = lax.broadcasted_iota(jnp.int32, (L * g, L * g), 1) // g
    toep = jnp.where(trow >= scol, toep, 0.0)
    et_re, et_im = outer(*power(float(L - 1) - step), bb_re, bb_im)
    f_re, f_im = outer(*power(step + 1.0), c_re, c_im)
    big_re, big_im = power(jnp.full((1, 1), float(L), F32))
    return toep, et_re, et_im, f_re, f_im, big_re, big_im


def _s5_y(ops, u, sp_re, sp_im):
    toep, _, _, f_re, f_im, _, _ = ops
    return _dot(u, toep, NT, HI) + _dot(sp_re, f_re, NT, HI) - _dot(sp_im, f_im, NT, HI)


def _s5_group_specs(ng):
    vec = pl.BlockSpec((1, 1, S5_STATE), lambda g: (g, 0, 0))
    one = pl.BlockSpec((1, 1, 1), lambda g: (g, 0, 0))
    mat = pl.BlockSpec((1, S5_GROUP, S5_STATE), lambda g: (g, 0, 0))
    return [vec, vec, one, mat, mat, mat, mat]


def _s5_load(refs):
    return [r[0] for r in refs]


def _s5_local(prm, u, name):
    ng, n, w = u.shape
    p = S5_STATE

    def body(*refs):
        ops = _s5_ops(*_s5_load(refs[:7]))
        uu = refs[7][0]
        refs[8][0] = _dot(uu, ops[1], NN, HI)
        refs[9][0] = _dot(uu, ops[2], NN, HI)
        refs[10][0] = ops[5]
        refs[11][0] = ops[6]

    blk = lambda a, b: pl.BlockSpec((1, a, b), lambda g: (g, 0, 0))
    return pl.pallas_call(
        body,
        name=name,
        grid=(ng,),
        in_specs=_s5_group_specs(ng) + [blk(n, w)],
        out_specs=[blk(n, p), blk(n, p), blk(1, p), blk(1, p)],
        out_shape=[jax.ShapeDtypeStruct((ng, n, p), F32)] * 2 + [jax.ShapeDtypeStruct((ng, 1, p), F32)] * 2,
        compiler_params=_params(("parallel",)),
    )(*prm, u)


def _s5_carry(e_re, e_im, lam_re, lam_im, name):
    n, w = e_re.shape
    cw = _tile(w, (1024, 512, 256, 128))

    def body(er_ref, ei_ref, lr_ref, li_ref, sr_ref, si_ref):
        lr, li = lr_ref[...], li_ref[...]

        def step(k, carry):
            sr, si = carry
            row = pl.ds(k, 1)
            sr_ref[row, :] = sr
            si_ref[row, :] = si
            return lr * sr - li * si + er_ref[row, :], li * sr + lr * si + ei_ref[row, :]

        z = jnp.zeros((1, cw), F32)
        lax.fori_loop(0, n, step, (z, z))

    col = pl.BlockSpec((n, cw), lambda j: (0, j))
    one = pl.BlockSpec((1, cw), lambda j: (0, j))
    return pl.pallas_call(
        body,
        name=name,
        grid=(w // cw,),
        in_specs=[col, col, one, one],
        out_specs=[col, col],
        out_shape=[jax.ShapeDtypeStruct((n, w), F32)] * 2,
        compiler_params=_params(("parallel",)),
    )(e_re, e_im, lam_re, lam_im)


def _s5_carry_bwd(dsp_re, dsp_im, sp_re, sp_im, lam_re, lam_im, name):
    n, w = dsp_re.shape
    cw = _tile(w, (1024, 512, 256, 128))

    def body(dr_ref, di_ref, sr_ref, si_ref, lr_ref, li_ref, gr_ref, gi_ref, dlr_ref, dli_ref):
        lr, li = lr_ref[...], li_ref[...]

        def step(q, carry):
            gr_next, gi_next, dr_next, di_next, alr, ali = carry
            k = n - 1 - q
            row = pl.ds(k, 1)
            gr = dr_next + lr * gr_next + li * gi_next
            gi = di_next - li * gr_next + lr * gi_next
            gr_ref[row, :] = gr
            gi_ref[row, :] = gi
            sr, si = sr_ref[row, :], si_ref[row, :]
            return gr, gi, dr_ref[row, :], di_ref[row, :], alr + gr * sr + gi * si, ali + gi * sr - gr * si

        z = jnp.zeros((1, cw), F32)
        out = lax.fori_loop(0, n, step, (z, z, z, z, z, z))
        dlr_ref[...] = out[4]
        dli_ref[...] = out[5]

    col = pl.BlockSpec((n, cw), lambda j: (0, j))
    one = pl.BlockSpec((1, cw), lambda j: (0, j))
    return pl.pallas_call(
        body,
        name=name,
        grid=(w // cw,),
        in_specs=[col, col, col, col, one, one],
        out_specs=[col, col, one, one],
        out_shape=[jax.ShapeDtypeStruct((n, w), F32)] * 2 + [jax.ShapeDtypeStruct((1, w), F32)] * 2,
        compiler_params=_params(("parallel",)),
    )(dsp_re, dsp_im, sp_re, sp_im, lam_re, lam_im)


def _s5_out(prm, u, sp_re, sp_im, name):
    ng, n, w = u.shape
    p = S5_STATE

    def body(*refs):
        ops = _s5_ops(*_s5_load(refs[:7]))
        refs[10][0] = _s5_y(ops, refs[7][0], refs[8][0], refs[9][0])

    blk = lambda a, b: pl.BlockSpec((1, a, b), lambda g: (g, 0, 0))
    return pl.pallas_call(
        body,
        name=name,
        grid=(ng,),
        in_specs=_s5_group_specs(ng) + [blk(n, w), blk(n, p), blk(n, p)],
        out_specs=blk(n, w),
        out_shape=jax.ShapeDtypeStruct((ng, n, w), F32),
        compiler_params=_params(("parallel",)),
    )(*prm, u, sp_re, sp_im)


def _s5_bwd_state(prm, dy, name):
    ng, n, w = dy.shape
    p = S5_STATE

    def body(*refs):
        ops = _s5_ops(*_s5_load(refs[:7]))
        d = refs[7][0]
        refs[8][0] = _dot(d, ops[3], NN, HI)
        refs[9][0] = -_dot(d, ops[4], NN, HI)

    blk = lambda a, b: pl.BlockSpec((1, a, b), lambda g: (g, 0, 0))
    return pl.pallas_call(
        body,
        name=name,
        grid=(ng,),
        in_specs=_s5_group_specs(ng) + [blk(n, w)],
        out_specs=[blk(n, p), blk(n, p)],
        out_shape=[jax.ShapeDtypeStruct((ng, n, p), F32)] * 2,
        compiler_params=_params(("parallel",)),
    )(*prm, dy)


def _s5_bwd_main(prm, u, sp_re, sp_im, dy, g_re, g_im, dlam_re, dlam_im, name):
    ng, n, w = u.shape
    p = S5_STATE

    def body(*refs):
        prm_v = _s5_load(refs[:7])
        uu, sr, si, d, gr, gi, dlr, dli = [r[0] for r in refs[7:15]]

        def phi(*args):
            ops = _s5_ops(*args[:7])
            y = _s5_y(ops, args[7], sr, si)
            e_re, e_im = _dot(args[7], ops[1], NN, HI), _dot(args[7], ops[2], NN, HI)
            return (jnp.sum(d * y) + jnp.sum(gr * e_re) + jnp.sum(gi * e_im)
                    + jnp.sum(dlr * ops[5]) + jnp.sum(dli * ops[6]))

        grads = jax.grad(phi, argnums=tuple(range(8)))(*prm_v, uu)
        for q in range(8):
            refs[15 + q][0] = grads[q]

    blk = lambda a, b: pl.BlockSpec((1, a, b), lambda g: (g, 0, 0))
    prm_specs = _s5_group_specs(ng)
    return pl.pallas_call(
        body,
        name=name,
        grid=(ng,),
        in_specs=prm_specs + [blk(n, w), blk(n, p), blk(n, p), blk(n, w), blk(n, p), blk(n, p), blk(1, p), blk(1, p)],
        out_specs=prm_specs + [blk(n, w)],
        out_shape=[jax.ShapeDtypeStruct(a.shape, F32) for a in prm] + [jax.ShapeDtypeStruct((ng, n, w), F32)],
        compiler_params=_params(("parallel",)),
    )(*prm, u, sp_re, sp_im, dy, g_re, g_im, dlam_re, dlam_im)


def _to_groups(h):
    t, d = h.shape
    ng, n = d // S5_GROUP, t // S5_CHUNK
    return h.reshape(n, S5_CHUNK, ng, S5_GROUP).transpose(2, 0, 1, 3).reshape(ng, n, S5_CHUNK * S5_GROUP)


def _from_groups(y, t, d):
    ng, n = d // S5_GROUP, t // S5_CHUNK
    return y.reshape(ng, n, S5_CHUNK, S5_GROUP).transpose(1, 2, 0, 3).reshape(t, d)


def _states_to_cols(e):
    ng, n, p = e.shape
    return e.transpose(1, 0, 2).reshape(n, ng * p)


def _cols_to_states(s, ng):
    n = s.shape[0]
    return s.reshape(n, ng, S5_STATE).transpose(1, 0, 2)


def _me():
    return lax.axis_index("x"), lax.axis_index("y"), lax.axis_index("c")


def _flip(v, bit):
    return 1 - v if bit else v


COMM_SEMS = [pltpu.SemaphoreType.DMA((7,)), pltpu.SemaphoreType.DMA((7,)), pltpu.SemaphoreType.DMA]
ANY_SPEC = pl.BlockSpec(memory_space=pl.ANY)


def _gather_phases(x_ref, out_ref, send_sems, recv_sems, local_sem):
    mx, my, mc = _me()
    me, sibling = (mx, my, mc), (mx, my, 1 - mc)
    chips = [(1 - mx, my), (mx, 1 - my), (1 - mx, 1 - my)]

    def slot(px, py, pc):
        return out_ref.at[4 * px + 2 * py + pc]

    def copy(k, block, to, src=None):
        return pltpu.make_async_remote_copy(
            src_ref=slot(*block) if src is None else src,
            dst_ref=slot(*block),
            send_sem=send_sems.at[k],
            recv_sem=recv_sems.at[k],
            device_id=to,
            device_id_type=pl.DeviceIdType.MESH,
        )

    mine = pltpu.make_async_copy(x_ref, slot(*me), local_sem)
    first = [copy(0, me, sibling, src=x_ref)] + [copy(1 + j, me, (*chip, mc), src=x_ref) for j, chip in enumerate(chips)]
    passed = [copy(4 + j, (*chip, mc), sibling) for j, chip in enumerate(chips)]

    def start():
        mine.start()
        for cp in first:
            cp.start()

    def relay():
        for j, chip in enumerate(chips):
            copy(1 + j, (*chip, mc), me).wait_recv()
            passed[j].start()

    def finish():
        copy(0, sibling, me).wait_recv()
        for j, chip in enumerate(chips):
            copy(4 + j, (*chip, 1 - mc), me).wait_recv()
        for cp in first + passed:
            cp.wait_send()
        mine.wait()

    return start, relay, finish


def _exchange_phases(g_ref, out_ref, send_sems, recv_sems, local_sem):
    mx, my, mc = _me()
    mine_idx = 4 * mx + 2 * my + mc
    own = pltpu.make_async_copy(g_ref.at[mine_idx], out_ref.at[mine_idx], local_sem)
    copies = []
    for m in range(1, N_DEV):
        px, py, pc = _flip(mx, m & 4), _flip(my, m & 2), _flip(mc, m & 1)
        copies.append(
            pltpu.make_async_remote_copy(
                src_ref=g_ref.at[4 * px + 2 * py + pc],
                dst_ref=out_ref.at[mine_idx],
                send_sem=send_sems.at[m - 1],
                recv_sem=recv_sems.at[m - 1],
                device_id=(px, py, pc),
                device_id_type=pl.DeviceIdType.MESH,
            )
        )

    def start():
        own.start()
        for cp in copies:
            cp.start()

    def finish():
        for cp in copies:
            cp.wait_recv()
        for cp in copies:
            cp.wait_send()
        own.wait()

    return start, None, finish


def _side_out_shape(side):
    kind, arr = side
    return jax.ShapeDtypeStruct((N_DEV,) + arr.shape if kind == "gather" else arr.shape, arr.dtype)


def _run_side(side, refs, nsteps_major, nsteps_minor):
    kind = side[0]
    start, relay, finish = (_gather_phases if kind == "gather" else _exchange_phases)(*refs)
    a, b = pl.program_id(0), pl.program_id(1)
    pl.when(jnp.logical_and(a == 0, b == 0))(start)
    if relay is not None:
        pl.when(jnp.logical_and(a == nsteps_major - 1, b == 0))(relay)
    return lambda: pl.when(jnp.logical_and(a == nsteps_major - 1, b == nsteps_minor - 1))(finish)


def _all_gather(x, name):
    def body(*refs):
        start, relay, finish = _gather_phases(*refs)
        start()
        relay()
        finish()

    return pl.pallas_call(body, name=name, in_specs=[ANY_SPEC], out_specs=ANY_SPEC,
                          out_shape=_side_out_shape(("gather", x)), scratch_shapes=COMM_SEMS)(x)


def _all_to_all(g, name):
    def body(*refs):
        start, _, finish = _exchange_phases(*refs)
        start()
        finish()

    return pl.pallas_call(body, name=name, in_specs=[ANY_SPEC], out_specs=ANY_SPEC,
                          out_shape=_side_out_shape(("exchange", g)), scratch_shapes=COMM_SEMS)(g)


def _sum_slots(recv, name):
    _, r, c = recv.shape
    tr = _tile(r, (256, 128, 64, 32, 16, 8))

    def body(r_ref, o_ref):
        acc = r_ref[0].astype(F32)
        for k in range(1, N_DEV):
            acc = acc + r_ref[k].astype(F32)
        o_ref[...] = acc

    return pl.pallas_call(
        body,
        name=name,
        grid=(r // tr,),
        in_specs=[pl.BlockSpec((N_DEV, tr, c), lambda i: (0, i, 0))],
        out_specs=pl.BlockSpec((tr, c), lambda i: (i, 0)),
        out_shape=jax.ShapeDtypeStruct((r, c), F32),
        compiler_params=_params(("parallel",)),
    )(recv)


def _adamw(w, g, m, v, name):
    shape = w.shape
    c = shape[-1]
    as2d = lambda a: a.reshape(-1, c)
    w2, g2, m2, v2 = as2d(w), as2d(g), as2d(m), as2d(v)
    r = w2.shape[0]
    tr = _tile(r, (256, 128, 64, 32, 16, 8))
    c1 = 1.0 / (1.0 - ADAM_B1 ** ADAM_STEP)
    c2 = 1.0 / (1.0 - ADAM_B2 ** ADAM_STEP)

    def body(w_ref, g_ref, m_ref, v_ref, d_ref, mo_ref, vo_ref):
        gg = g_ref[...]
        mn = ADAM_B1 * m_ref[...] + (1.0 - ADAM_B1) * gg
        vn = ADAM_B2 * v_ref[...] + (1.0 - ADAM_B2) * (gg * gg)
        d_ref[...] = -ADAM_LR * ((mn * c1) / (jnp.sqrt(vn * c2) + ADAM_EPS) + ADAM_WD * w_ref[...])
        mo_ref[...] = mn
        vo_ref[...] = vn

    spec = pl.BlockSpec((tr, c), lambda i: (i, 0))
    d, mn, vn = pl.pallas_call(
        body,
        name=name,
        grid=(r // tr,),
        in_specs=[spec] * 4,
        out_specs=[spec] * 3,
        out_shape=[jax.ShapeDtypeStruct((r, c), F32)] * 3,
        compiler_params=_params(("parallel",)),
    )(w2, g2, m2, v2)
    return d.reshape(shape), mn.reshape(shape), vn.reshape(shape)


COL_SHARDED = ("attn_w_in", "mla_w_q_b", "mla_w_kv_b", "s5_d", "s5_w_glu", "ffn_w_in", "ple_w")
ROW_SHARDED = ("attn_w_out", "ffn_w_out", "ple_gate_w")
REPLICATED = ("mla_q_norm", "mla_kv_norm", "s5_a_re", "s5_a_im", "s5_log_dt", "s5_b_re", "s5_b_im", "s5_c_re",
              "s5_c_im", "ln1_g", "ln1_b", "ln2_g", "ln2_b")
ATTENTION_WEIGHTS = ("attn_w_in", "mla_w_q_b", "mla_w_kv_b", "attn_w_out")
BIG_WEIGHTS = ("attn_w_in", "mla_w_q_b", "mla_w_kv_b", "attn_w_out", "s5_w_glu", "ffn_w_in", "ffn_w_out", "ple_w", "ple_gate_w")
WEIGHTS = ("attn_w_in", "mla_q_norm", "mla_w_q_b", "mla_kv_norm", "mla_w_kv_b", "attn_w_out", "s5_a_re", "s5_a_im",
           "s5_log_dt", "s5_b_re", "s5_b_im", "s5_c_re", "s5_c_im", "s5_d", "s5_w_glu", "ln1_g", "ln1_b", "ffn_w_in",
           "ffn_w_out", "ple_w", "ple_gate_w", "ln2_g", "ln2_b")


def _unshard(name, gathered):
    if name in COL_SHARDED:
        full = jnp.moveaxis(gathered, 0, -2)
        return full.reshape(full.shape[:-2] + (full.shape[-2] * full.shape[-1],))
    full = jnp.moveaxis(gathered, 0, 1)
    return full.reshape((full.shape[0], full.shape[1] * full.shape[2]) + full.shape[3:])


def _by_owner(name, grad):
    if name in COL_SHARDED:
        g = grad.reshape(grad.shape[:-1] + (N_DEV, grad.shape[-1] // N_DEV))
        return jnp.moveaxis(g, -2, 0).reshape(N_DEV, -1)
    if name in ROW_SHARDED:
        g = grad.reshape((grad.shape[0], N_DEV, grad.shape[1] // N_DEV) + grad.shape[2:])
        return jnp.moveaxis(g, 1, 0).reshape(N_DEV, -1)
    return jnp.broadcast_to(grad.reshape(1, -1), (N_DEV, grad.size))


def _owner_rows(name, g):
    k, n = g.shape
    if name in COL_SHARDED:
        return g.reshape(k, N_DEV, n // N_DEV).transpose(1, 0, 2).reshape(N_DEV, -1)
    assert name in ROW_SHARDED, name
    return g.reshape(N_DEV, -1)


def _pack(pieces, axis, dtype, row_mult):
    n = sum(p.shape[axis] for p in pieces)
    pad = (-n) % (PACK_COLS * row_mult)
    if pad:
        shape = list(pieces[0].shape)
        shape[axis] = pad
        pieces = list(pieces) + [jnp.zeros(shape, dtype)]
    flat = jnp.concatenate(pieces, axis=axis)
    return flat.reshape(flat.shape[:axis] + ((n + pad) // PACK_COLS, PACK_COLS))


def _twice(fn):
    return lambda *a: fn(*a) * 2


def _ffn_block(h, hb, p_i, w_in, w_out, w_ple, w_pg, g, b, alpha, tag):
    t, d = h.shape
    hid = w_out.shape[0]
    cw = _tile(hid, (512, 256, 128))
    ncb = hid // cw
    gu = _mm(hb, w_in, name=f"ffn_in_{tag}")
    act = _rowwise(_swiglu_fn, [(gu, cw, 0), (gu, cw, ncb)], [], [(hid, cw, BF16)], name=f"swiglu_{tag}", tq=256, ncol=ncb)[0]
    f = _mm(act, w_out, name=f"ffn_out_{tag}")
    pw = _mm(p_i, w_ple, name=f"ple_{tag}")
    gate = _mm(hb, w_pg, name=f"ple_gate_{tag}")
    out, outb = _rowwise(_twice(_ln_ffn_fn(alpha)), [_whole(h), _whole(f), _whole(pw), _whole(gate)], [g, b],
                         [(d, d, F32), (d, d, BF16)], name=f"ln2_{tag}", tq=256)
    return out, outb, (h, hb, p_i, gu, act, f, pw, gate)


def _ffn_block_bwd(saved, dout, w_in, w_out, w_pg, g, b, alpha, tag):
    h, hb, p_i, gu, act, f, pw, gate = saved
    t, d = h.shape
    hid = w_out.shape[0]
    cw = _tile(hid, (512, 256, 128))
    ncb = hid // cw
    dh_a, df, dpw, dgate, dg, db = _rowwise_bwd(
        _ln_ffn_fn(alpha), [_whole(h), _whole(f), _whole(pw), _whole(gate)], [g, b], [dout],
        need=[True] * 4, drow=[(d, F32), (d, BF16), (d, BF16), (d, BF16)], name=f"ln2_bwd_{tag}", tq=128)
    dw_out = _mm(act, df, ta=True, name=f"ffn_out_dw_{tag}")
    dact = _mm(df, w_out, tb=True, out_dtype=BF16, name=f"ffn_out_dx_{tag}")
    dg_, du_ = _rowwise_bwd(_swiglu_fn, [(gu, cw, 0), (gu, cw, ncb)], [], [[(dact, cw, 0)]], need=[True, True],
                            drow=[(hid, BF16), (hid, BF16)], name=f"swiglu_bwd_{tag}", tq=256, ncol=ncb)
    dgu = jnp.concatenate([dg_, du_], axis=1)
    dw_in = _mm(hb, dgu, ta=True, name=f"ffn_in_dw_{tag}")
    dh_b = _mm(dgu, w_in, tb=True, name=f"ffn_in_dx_{tag}")
    dw_ple = _mm(p_i, dpw, ta=True, name=f"ple_dw_{tag}")
    dw_pg = _mm(hb, dgate, ta=True, name=f"ple_gate_dw_{tag}")
    dh_c = _mm(dgate, w_pg, tb=True, name=f"ple_gate_dx_{tag}")
    return [dh_a, dh_b, dh_c], dict(ffn_w_in=dw_in, ffn_w_out=dw_out, ple_w=dw_ple, ple_gate_w=dw_pg, ln2_g=dg, ln2_b=db)


def kernel(x, p, positions, attn_w_in, mla_q_norm, mla_w_q_b, mla_kv_norm, mla_w_kv_b, attn_w_out, s5_a_re, s5_a_im, s5_log_dt, s5_b_re, s5_b_im, s5_c_re, s5_c_im, s5_d, s5_w_glu, ln1_g, ln1_b, ffn_w_in, ffn_w_out, ple_w, ple_gate_w, ln2_g, ln2_b, loss_target, m_attn_w_in, m_mla_q_norm, m_mla_w_q_b, m_mla_kv_norm, m_mla_w_kv_b, m_attn_w_out, m_s5_a_re, m_s5_a_im, m_s5_log_dt, m_s5_b_re, m_s5_b_im, m_s5_c_re, m_s5_c_im, m_s5_d, m_s5_w_glu, m_ln1_g, m_ln1_b, m_ffn_w_in, m_ffn_w_out, m_ple_w, m_ple_gate_w, m_ln2_g, m_ln2_b, v_attn_w_in, v_mla_q_norm, v_mla_w_q_b, v_mla_kv_norm, v_mla_w_kv_b, v_attn_w_out, v_s5_a_re, v_s5_a_im, v_s5_log_dt, v_s5_b_re, v_s5_b_im, v_s5_c_re, v_s5_c_im, v_s5_d, v_s5_w_glu, v_ln1_g, v_ln1_b, v_ffn_w_in, v_ffn_w_out, v_ple_w, v_ple_gate_w, v_ln2_g, v_ln2_b):
    local = dict(attn_w_in=attn_w_in, mla_q_norm=mla_q_norm, mla_w_q_b=mla_w_q_b, mla_kv_norm=mla_kv_norm,
                 mla_w_kv_b=mla_w_kv_b, attn_w_out=attn_w_out, s5_a_re=s5_a_re, s5_a_im=s5_a_im, s5_log_dt=s5_log_dt,
                 s5_b_re=s5_b_re, s5_b_im=s5_b_im, s5_c_re=s5_c_re, s5_c_im=s5_c_im, s5_d=s5_d, s5_w_glu=s5_w_glu,
                 ln1_g=ln1_g, ln1_b=ln1_b, ffn_w_in=ffn_w_in, ffn_w_out=ffn_w_out, ple_w=ple_w, ple_gate_w=ple_gate_w,
                 ln2_g=ln2_g, ln2_b=ln2_b)
    mom_m = dict(zip(WEIGHTS, (m_attn_w_in, m_mla_q_norm, m_mla_w_q_b, m_mla_kv_norm, m_mla_w_kv_b, m_attn_w_out, m_s5_a_re, m_s5_a_im, m_s5_log_dt, m_s5_b_re, m_s5_b_im, m_s5_c_re, m_s5_c_im, m_s5_d, m_s5_w_glu, m_ln1_g, m_ln1_b, m_ffn_w_in, m_ffn_w_out, m_ple_w, m_ple_gate_w, m_ln2_g, m_ln2_b)))
    mom_v = dict(zip(WEIGHTS, (v_attn_w_in, v_mla_q_norm, v_mla_w_q_b, v_mla_kv_norm, v_mla_w_kv_b, v_attn_w_out, v_s5_a_re, v_s5_a_im, v_s5_log_dt, v_s5_b_re, v_s5_b_im, v_s5_c_re, v_s5_c_im, v_s5_d, v_s5_w_glu, v_ln1_g, v_ln1_b, v_ffn_w_in, v_ffn_w_out, v_ple_w, v_ple_gate_w, v_ln2_g, v_ln2_b)))

    t, d = x.shape[1], x.shape[2]
    depth = ln1_g.shape[0]
    alpha = (2.0 * depth) ** 0.25
    ql, kvl = mla_q_norm.shape[-1], mla_kv_norm.shape[-1]
    nh = mla_w_q_b.shape[-1] * N_DEV // (NOPE_DIM + ROPE_DIM)
    dw = (attn_w_in.shape[-1] * N_DEV - ql - kvl - ROPE_DIM) // 3
    ndh = dw // DIL_HEAD_DIM
    ng = d // S5_GROUP
    assert ql == kvl and (ql + kvl) % LANE == 0 and nh * V_DIM == dw

    sharded = [n for n in WEIGHTS if n in COL_SHARDED or n in ROW_SHARDED]
    def as_words(n):
        if n == "s5_d":
            return lax.bitcast_convert_type(local[n], BF16).reshape(-1)
        if n == "attn_w_in":
            return local[n][0].T.astype(BF16).reshape(-1)
        return local[n].astype(BF16).reshape(-1)

    full = {}

    def unpack_weights(names, gathered):
        gathered, off = gathered.reshape(N_DEV, -1), 0
        for n in names:
            size = local[n].size * (2 if n == "s5_d" else 1)
            piece = gathered[:, off : off + size]
            off += size
            if n == "attn_w_in":
                full[n] = piece.reshape(N_DEV * local[n].shape[2], d)
                continue
            if n == "s5_d":
                piece = lax.bitcast_convert_type(piece.reshape((N_DEV,) + local[n].shape + (2,)), F32)
            full[n] = _unshard(n, piece.reshape((N_DEV,) + local[n].shape))

    first_w = [n for n in sharded if n in ATTENTION_WEIGHTS]
    later_w = [n for n in sharded if n not in ATTENTION_WEIGHTS]
    unpack_weights(first_w, _all_gather(_pack([as_words(n) for n in first_w], 0, BF16, 16), "gather_attn_weights"))
    later_packed = _pack([as_words(n) for n in later_w], 0, BF16, 16)

    w_in_t = full["attn_w_in"]
    lat = ql + kvl
    w_lat_t = jnp.concatenate([w_in_t[: lat + ROPE_DIM], jnp.zeros((LANE - ROPE_DIM, d), BF16)], axis=0)
    w_dil_t = w_in_t[lat + ROPE_DIM :]
    wq = full["mla_w_q_b"][0].reshape(ql, nh, NOPE_DIM + ROPE_DIM)
    wq_pe = jnp.pad(wq[:, :, NOPE_DIM:], ((0, 0), (0, 0), (0, LANE - ROPE_DIM)))
    wqp = jnp.concatenate([wq[:, :, :NOPE_DIM].reshape(ql, nh * LANE), wq_pe.reshape(ql, nh * LANE)], axis=1)
    wkv = full["mla_w_kv_b"][0].reshape(kvl, nh, NOPE_DIM + V_DIM)
    wkvp = jnp.concatenate([wkv[:, :, :NOPE_DIM].reshape(kvl, nh * LANE), wkv[:, :, NOPE_DIM:].reshape(kvl, nh * LANE)], axis=1)
    w_out = full["attn_w_out"][0]

    h0 = x[0]
    h0b = h0.astype(BF16)
    pb = p.astype(BF16)
    target = loss_target[0]
    pos = positions.reshape(t, 1)
    grads = {}

    z = _mm(h0b, w_lat_t, tb=True, name="attn_in_lat")
    zd = _mm(h0b, w_dil_t, tb=True, out_dtype=BF16, name="attn_in_dil")
    gq, gk = mla_q_norm.reshape(1, ql), mla_kv_norm.reshape(1, kvl)
    qn, kvn = _rowwise(_rms_fn, [(z, ql, 0), (z, kvl, 1)], [gq, gk], [(ql, ql, BF16), (kvl, kvl, BF16)], name="rms", tq=256)
    qf = _mm(qn, wqp, name="q_up")
    kvf = _mm(kvn, wkvp, out_dtype=BF16, name="kv_up")
    cos, sin = _rope_tables(pos, "rope_tables")
    pe_cb = lat // LANE
    qpe, kpe = _rowwise(_rope_fwd_fn(nh), [(qf, nh * LANE, 1), (z, LANE, pe_cb), _whole(cos), _whole(sin)], [],
                        [(nh * LANE, nh * LANE, BF16), (LANE, LANE, BF16)], name="rope", tq=256)
    out_a, lse_a, later_gathered = _mla_fwd(qf, qpe, kvf, kpe, nh, "mla_fwd", side=("gather", later_packed))
    unpack_weights(later_w, later_gathered)
    w_glu = full["s5_w_glu"][0]
    d_skip = full["s5_d"]

    def to_classes(a, dil):
        if dil == 1:
            return a
        return a.reshape(t // dil, dil, a.shape[1]).transpose(1, 0, 2).reshape(t, a.shape[1])

    def from_classes(a, dil):
        if dil == 1:
            return a
        return a.reshape(dil, t // dil, a.shape[1]).transpose(1, 0, 2).reshape(t, a.shape[1])

    band = []
    for window, dil in DIL_BRANCHES:
        assert window // dil == BLK
        zc = to_classes(zd, dil)
        o_c, l_c = _band_fwd(zc, ndh, dil, f"band_fwd_d{dil}")
        band.append((dil, zc, o_c, l_c, from_classes(o_c, dil), from_classes(l_c, dil)))
    merge_rows = [_whole(b[4]) for b in band] + [_whole(b[5]) for b in band]
    out_b = _rowwise(_merge_fn, merge_rows, [], [(dw, dw, BF16)], name="merge", tq=256)[0]
    att = jnp.concatenate([out_a, out_b], axis=1)
    mix0 = _mm(att, w_out, name="attn_out")
    g1, b1 = ln1_g[0:1], ln1_b[0:1]
    h1, h1b = _rowwise(_twice(_ln_mix_fn(alpha)), [_whole(h0), _whole(mix0)], [g1, b1], [(d, d, F32), (d, d, BF16)],
                       name="ln1_l0", tq=256)
    h2, _, saved_f0 = _ffn_block(h1, h1b, pb[0, 0], full["ffn_w_in"][0], full["ffn_w_out"][0], full["ple_w"][0],
                                 full["ple_gate_w"][0], ln2_g[0:1], ln2_b[0:1], alpha, "l0")

    prm = [s5_a_re[0].reshape(ng, 1, S5_STATE), s5_a_im[0].reshape(ng, 1, S5_STATE), s5_log_dt[0].reshape(ng, 1, 1),
           s5_b_re[0].transpose(0, 2, 1), s5_b_im[0].transpose(0, 2, 1), s5_c_re[0], s5_c_im[0]]
    u = _to_groups(h2)
    e_re, e_im, lam_re, lam_im = _s5_local(prm, u, "s5_local")
    lam_re_c, lam_im_c = lam_re.reshape(1, ng * S5_STATE), lam_im.reshape(1, ng * S5_STATE)
    sp_re_c, sp_im_c = _s5_carry(_states_to_cols(e_re), _states_to_cols(e_im), lam_re_c, lam_im_c, "s5_carry")
    sp_re, sp_im = _cols_to_states(sp_re_c, ng), _cols_to_states(sp_im_c, ng)
    ys = _from_groups(_s5_out(prm, u, sp_re, sp_im, "s5_out"), t, d)
    z5 = _rowwise(_gelu_fn, [_whole(ys), _whole(h2)], [d_skip], [(d, d, BF16)], name="gelu", tq=256)[0]
    vg = _mm(z5, w_glu, name="glu_in")
    g3, b3 = ln1_g[1:2], ln1_b[1:2]
    h3, h3b = _rowwise(_twice(_ln_glu_fn(alpha)), [_whole(h2), (vg, d, 0), (vg, d, 1)], [g3, b3],
                       [(d, d, F32), (d, d, BF16)], name="ln1_l1", tq=256)
    h4, _, saved_f1 = _ffn_block(h3, h3b, pb[1, 0], full["ffn_w_in"][1], full["ffn_w_out"][1], full["ple_w"][1],
                                 full["ple_gate_w"][1], ln2_g[1:2], ln2_b[1:2], alpha, "l1")

    dh4, loss_acc = _loss_kernel(h4, target, "loss")
    loss = lax.psum(loss_acc[0, 0], AXES)

    dh3, gf1 = _ffn_block_bwd(saved_f1, [_whole(dh4)], full["ffn_w_in"][1], full["ffn_w_out"][1],
                              full["ple_gate_w"][1], ln2_g[1:2], ln2_b[1:2], alpha, "l1")
    dh2_a, dval, dgate, dg3, db3 = _rowwise_bwd(
        _ln_glu_fn(alpha), [_whole(h2), (vg, d, 0), (vg, d, 1)], [g3, b3], [[_whole(a) for a in dh3]],
        need=[True] * 3, drow=[(d, F32), (d, BF16), (d, BF16)], name="ln1_bwd_l1", tq=128)
    dvg = jnp.concatenate([dval, dgate], axis=1)
    dw_glu = _mm(z5, dvg, ta=True, name="glu_dw")
    dz5 = _mm(dvg, w_glu, tb=True, name="glu_dx")
    dys, dh2_b, dd = _rowwise_bwd(_gelu_fn, [_whole(ys), _whole(h2)], [d_skip], [[_whole(dz5)]], need=[True, True],
                                  drow=[(d, F32), (d, F32)], name="gelu_bwd", tq=128)
    grads["s5_d"] = dd
    dy = _to_groups(dys)
    dsp_re, dsp_im = _s5_bwd_state(prm, dy, "s5_bwd_state")
    g_re_c, g_im_c, dlam_re_c, dlam_im_c = _s5_carry_bwd(_states_to_cols(dsp_re), _states_to_cols(dsp_im), sp_re_c, sp_im_c,
                                                         lam_re_c, lam_im_c, "s5_carry_bwd")
    s5g = _s5_bwd_main(prm, u, sp_re, sp_im, dy, _cols_to_states(g_re_c, ng), _cols_to_states(g_im_c, ng),
                       dlam_re_c.reshape(ng, 1, S5_STATE), dlam_im_c.reshape(ng, 1, S5_STATE), "s5_bwd_main")
    grads["s5_a_re"] = s5g[0].reshape(s5_a_re.shape)
    grads["s5_a_im"] = s5g[1].reshape(s5_a_im.shape)
    grads["s5_log_dt"] = s5g[2].reshape(s5_log_dt.shape)
    grads["s5_b_re"] = s5g[3].transpose(0, 2, 1)[None]
    grads["s5_b_im"] = s5g[4].transpose(0, 2, 1)[None]
    grads["s5_c_re"] = s5g[5][None]
    grads["s5_c_im"] = s5g[6][None]
    dh2_c = _from_groups(s5g[7], t, d)

    dh1, gf0 = _ffn_block_bwd(saved_f0, [_whole(dh2_a), _whole(dh2_b), _whole(dh2_c)], full["ffn_w_in"][0],
                              full["ffn_w_out"][0], full["ple_gate_w"][0], ln2_g[0:1], ln2_b[0:1], alpha, "l0")
    for k in ("ln2_g", "ln2_b"):
        grads[k] = jnp.concatenate([gf0[k], gf1[k]])

    shards = {}

    def send_buffer(items, dtype, row_mult):
        return _pack([rows.astype(dtype) for _, rows in items], 1, dtype, row_mult)

    def take(items, recv, tag):
        flat, off = _sum_slots(recv, f"sum_grads_{tag}").reshape(-1), 0
        for key, rows in items:
            shards[key] = flat[off : off + rows.shape[1]]
            off += rows.shape[1]

    ffn_names = ("ffn_w_in", "ffn_w_out", "ple_w", "ple_gate_w")
    items_l1 = [(("s5_w_glu", 0), _owner_rows("s5_w_glu", dw_glu))] + [((k, 1), _owner_rows(k, gf1[k])) for k in ffn_names]
    items_l0 = [((k, 0), _owner_rows(k, gf0[k])) for k in ffn_names]
    dh0_a, dmix, dg1, db1 = _rowwise_bwd(_ln_mix_fn(alpha), [_whole(h0), _whole(mix0)], [g1, b1], [[_whole(a) for a in dh1]],
                                         need=[True, True], drow=[(d, F32), (d, BF16)], name="ln1_bwd_l0", tq=128)
    grads["ln1_g"] = jnp.concatenate([dg1, dg3])
    grads["ln1_b"] = jnp.concatenate([db1, db3])
    dw_out = _mm(att, dmix, ta=True, name="attn_out_dw")
    datt = _mm(dmix, w_out, tb=True, name="attn_out_dx")

    dmerge = _rowwise_bwd(_merge_fn, merge_rows, [], [[(datt, dw, 1)]], need=[True] * 6, drow=[(dw, F32)] * 6,
                          name="merge_bwd", tq=128)
    dq_s = dk_s = dv_s = None
    for k, (dil, zc, o_c, l_c, _, _) in enumerate(band):
        do_c, dl_c = to_classes(dmerge[k], dil), to_classes(dmerge[3 + k], dil)
        dq_c, dk_c, dv_c = _band_bwd(zc, o_c, l_c, do_c, dl_c, ndh, dil, f"band_bwd_d{dil}")
        dq_n, dk_n, dv_n = from_classes(dq_c, dil), from_classes(dk_c, dil), from_classes(dv_c, dil)
        dq_s = dq_n if dq_s is None else dq_s + dq_n
        dk_s = dk_n if dk_s is None else dk_s + dk_n
        dv_s = dv_n if dv_s is None else dv_s + dv_n

    dqn, dqp, delta, recv_l1 = _mla_bwd_q(qf, qpe, kvf, kpe, datt, out_a, lse_a, nh, "mla_bwd_q",
                                          side=("exchange", send_buffer(items_l1, BF16, 16)))
    take(items_l1, recv_l1, "l1")
    to_row = lambda a: a[:, ::LANE].T.reshape(nh, 1, t)
    dkn, dkp, dv, recv_l0 = _mla_bwd_kv(qf[:, : nh * LANE].astype(BF16), qpe, kvf, kpe, datt[:, : nh * LANE].astype(BF16),
                                        to_row(lse_a), to_row(delta), nh, "mla_bwd_kv",
                                        side=("exchange", send_buffer(items_l0, BF16, 16)))
    take(items_l0, recv_l0, "l0")
    dq_pe, dk_pe = _rowwise(_rope_bwd_fn(nh), [_whole(dqp), _whole(dkp), _whole(cos), _whole(sin)], [],
                            [(nh * LANE, nh * LANE, BF16), (LANE, LANE, BF16)], name="rope_bwd", tq=256)
    dqf = jnp.concatenate([dqn.astype(BF16), dq_pe], axis=1)
    dkvf = jnp.concatenate([dkn, dv], axis=1).astype(BF16)
    dwqp = _mm(qn, dqf, ta=True, name="q_up_dw")
    dqn_in = _mm(dqf, wqp, tb=True, name="q_up_dx")
    dwkvp = _mm(kvn, dkvf, ta=True, name="kv_up_dw")
    dkvn_in = _mm(dkvf, wkvp, tb=True, name="kv_up_dx")
    dql, dkvl, dgq, dgk = _rowwise_bwd(_rms_fn, [(z, ql, 0), (z, kvl, 1)], [gq, gk], [[_whole(dqn_in)], [_whole(dkvn_in)]],
                                       need=[True, True], drow=[(ql, BF16), (kvl, BF16)], name="rms_bwd", tq=256)
    grads["mla_q_norm"], grads["mla_kv_norm"] = dgq, dgk
    dz_lat = jnp.concatenate([dql, dkvl, dk_pe], axis=1)
    dz_dil = jnp.concatenate([dq_s, dk_s, dv_s], axis=1).astype(BF16)
    dw_lat_t = _mm(dz_lat, h0b, ta=True, name="attn_in_lat_dw")
    dw_dil_t = _mm(dz_dil, h0b, ta=True, name="attn_in_dil_dw")
    dh0_b = _mm(dz_lat, w_lat_t, name="attn_in_lat_dx")
    dh0_c = _mm(dz_dil, w_dil_t, name="attn_in_dil_dx")
    grad_x = _addn([dh0_a, dh0_b, dh0_c], "grad_x")[None]
    dw_in_t = jnp.concatenate([dw_lat_t[: lat + ROPE_DIM], dw_dil_t], axis=0)
    dwq_n = dwqp[:, : nh * LANE].reshape(ql, nh, NOPE_DIM)
    dwq_r = dwqp[:, nh * LANE :].reshape(ql, nh, LANE)[:, :, :ROPE_DIM]
    dwq = jnp.concatenate([dwq_n, dwq_r], axis=2).reshape(ql, nh * (NOPE_DIM + ROPE_DIM))
    dwkv_k = dwkvp[:, : nh * LANE].reshape(kvl, nh, NOPE_DIM)
    dwkv_v = dwkvp[:, nh * LANE :].reshape(kvl, nh, V_DIM)
    dwkv = jnp.concatenate([dwkv_k, dwkv_v], axis=2).reshape(kvl, nh * (NOPE_DIM + V_DIM))

    items_att = [(("attn_w_in", 0), dw_in_t.reshape(N_DEV, -1)), (("mla_w_q_b", 0), _owner_rows("mla_w_q_b", dwq)),
                 (("mla_w_kv_b", 0), _owner_rows("mla_w_kv_b", dwkv)), (("attn_w_out", 0), _owner_rows("attn_w_out", dw_out))]
    take(items_att, _all_to_all(send_buffer(items_att, BF16, 16), "exchange_grads_attn"), "attn")
    items_small = [((n, None), _by_owner(n, grads[n])) for n in WEIGHTS if n not in BIG_WEIGHTS]
    take(items_small, _all_to_all(send_buffer(items_small, F32, 8), "exchange_grads_small"), "small")

    g_out, d_out, m_out, v_out = [], [], [], []
    for n in WEIGHTS:
        if n == "attn_w_in":
            g = shards[(n, 0)].reshape(local[n].shape[2], d).T[None]
        elif n in BIG_WEIGHTS:
            g = jnp.concatenate([shards[(n, layer)] for layer in range(local[n].shape[0])]).reshape(local[n].shape)
        else:
            g = shards[(n, None)].reshape(local[n].shape)
        dlt, mn, vn = _adamw(local[n], g, mom_m[n], mom_v[n], f"adamw_{n}")
        g_out.append(g)
        d_out.append(dlt)
        m_out.append(mn)
        v_out.append(vn)
    return (loss, grad_x, *g_out, *d_out, *m_out, *v_out)
```

```python
import functools
import math

import numpy as np
import jax
import jax.numpy as jnp
from jax import lax
from jax.experimental import pallas as pl
from jax.experimental.pallas import tpu as pltpu

F32 = jnp.float32
BF16 = jnp.bfloat16

NOPE_DIM, ROPE_DIM, V_DIM = 128, 64, 128
DIL_HEAD_DIM = 128
DIL_BRANCHES = ((128, 1), (512, 4), (2048, 16))
BLK = 128
LANE = 128
ROPE_THETA = 10000.0
S5_GROUP, S5_STATE = 16, 64
S5_CHUNK = 32
NEG = -1e30
ADAM_LR, ADAM_B1, ADAM_B2, ADAM_EPS, ADAM_WD, ADAM_STEP = 0.001, 0.9, 0.999, 1e-08, 0.01, 10
N_DEV = 8
AXES = ("x", "y", "c")
VMEM_LIMIT = 56 * 1024 * 1024
MM_VMEM_BUDGET = 40 * 1024 * 1024
PACK_COLS = 1024
HI = lax.Precision.HIGH


def _tile(n, cands):
    for c in cands:
        if n % c == 0:
            return c
    return n


def _params(sem=None):
    return pltpu.CompilerParams(dimension_semantics=sem, vmem_limit_bytes=VMEM_LIMIT)


def _dot(a, b, dims, prec=None):
    return lax.dot_general(a, b, (dims, ((), ())), preferred_element_type=F32, precision=prec)


NN = ((1,), (0,))
NT = ((1,), (1,))
TN = ((0,), (0,))


def _mm(a, b, *, ta=False, tb=False, out_dtype=F32, name):
    (k, m) = a.shape if ta else a.shape[::-1]
    (n, k2) = b.shape if tb else b.shape[::-1]
    assert k == k2, (a.shape, b.shape, ta, tb)
    tm = _tile(m, (1024, 512, 256, 128))
    tn = _tile(n, (1024, 512, 384, 256, 128))
    sa, sb, so = a.dtype.itemsize, b.dtype.itemsize, jnp.dtype(out_dtype).itemsize

    def fits(tk):
        return 2 * (tm * tk * sa + tk * tn * sb) + 2 * tm * tn * so + 4 * tm * tn <= MM_VMEM_BUDGET

    tk = next((c for c in (2048, 1536, 1408, 1152, 1024, 768, 512, 384, 256, 128) if k % c == 0 and fits(c)), k)
    nk = k // tk
    dims = (((0,) if ta else (1,)), ((1,) if tb else (0,)))

    def body(a_ref, b_ref, o_ref, *acc):
        part = _dot(a_ref[...].astype(BF16), b_ref[...].astype(BF16), dims)
        if nk == 1:
            o_ref[...] = part.astype(o_ref.dtype)
            return
        (acc_ref,) = acc
        kk = pl.program_id(2)

        @pl.when(kk == 0)
        def _():
            acc_ref[...] = part

        @pl.when(kk > 0)
        def _():
            acc_ref[...] += part

        @pl.when(kk == nk - 1)
        def _():
            o_ref[...] = acc_ref[...].astype(o_ref.dtype)

    a_spec = pl.BlockSpec((tk, tm), lambda i, j, kk: (kk, i)) if ta else pl.BlockSpec((tm, tk), lambda i, j, kk: (i, kk))
    b_spec = pl.BlockSpec((tn, tk), lambda i, j, kk: (j, kk)) if tb else pl.BlockSpec((tk, tn), lambda i, j, kk: (kk, j))
    return pl.pallas_call(
        body,
        name=name,
        grid=(m // tm, n // tn, nk),
        in_specs=[a_spec, b_spec],
        out_specs=pl.BlockSpec((tm, tn), lambda i, j, kk: (i, j)),
        out_shape=jax.ShapeDtypeStruct((m, n), out_dtype),
        scratch_shapes=[pltpu.VMEM((tm, tn), F32)] if nk > 1 else [],
        compiler_params=_params(("parallel", "parallel", "arbitrary")),
    )(a, b)


def _row_spec(spec, tq):
    _, w, cb = spec
    return pl.BlockSpec((tq, w), lambda i, j: (i, cb + j))


def _full_spec(p):
    return pl.BlockSpec(p.shape, lambda i, j: (0,) * p.ndim)


def _rowwise(fn, rows, pars, outs, *, name, tq, ncol=1):
    t = rows[0][0].shape[0]
    tq = _tile(t, (tq, 128, 64, 32, 16, 8))
    nr, npar = len(rows), len(pars)

    def body(*refs):
        vals = [r[...].astype(F32) for r in refs[: nr + npar]]
        res = fn(*vals)
        for o, r in zip(refs[nr + npar :], res):
            o[...] = r.astype(o.dtype)

    res = pl.pallas_call(
        body,
        name=name,
        grid=(t // tq, ncol),
        in_specs=[_row_spec(s, tq) for s in rows] + [_full_spec(p) for p in pars],
        out_specs=[pl.BlockSpec((tq, w), lambda i, j: (i, j)) for (_, w, _) in outs],
        out_shape=[jax.ShapeDtypeStruct((t, wt), dt) for (wt, _, dt) in outs],
        compiler_params=_params(("parallel", "parallel")),
    )(*[s[0] for s in rows], *pars)
    return res


def _rowwise_bwd(fn, rows, pars, cots, *, need, drow, name, tq, ncol=1):
    t = rows[0][0].shape[0]
    tq = _tile(t, (tq, 128, 64, 32, 16, 8))
    nr, npar = len(rows), len(pars)
    assert ncol == 1 or npar == 0
    flat_cots = [s for c in cots for s in c]
    ncot = len(flat_cots)
    want = [k for k in range(nr) if need[k]]

    def body(*refs):
        vals = [r[...].astype(F32) for r in refs[: nr + npar]]
        cref = refs[nr + npar : nr + npar + ncot]
        oref = refs[nr + npar + ncot :]
        cvals, pos = [], 0
        for c in cots:
            acc = cref[pos][...].astype(F32)
            for q in range(1, len(c)):
                acc = acc + cref[pos + q][...].astype(F32)
            cvals.append(acc)
            pos += len(c)

        def closed(*diff):
            full = list(vals)
            for k, dv in zip(want + list(range(nr, nr + npar)), diff):
                full[k] = dv
            return tuple(fn(*full))

        diff_in = [vals[k] for k in want] + vals[nr:]
        _, vjp = jax.vjp(closed, *diff_in)
        grads = vjp(tuple(cvals))
        for q in range(len(want)):
            oref[q][...] = grads[q].astype(oref[q].dtype)
        if npar:
            first = pl.program_id(0) == 0

            @pl.when(first)
            def _():
                for q in range(npar):
                    oref[len(want) + q][...] = jnp.zeros_like(oref[len(want) + q])

            for q in range(npar):
                oref[len(want) + q][...] += grads[len(want) + q]

    out_specs = [pl.BlockSpec((tq, rows[k][1]), lambda i, j, cb=rows[k][2]: (i, j)) for k in want]
    out_specs += [_full_spec(p) for p in pars]
    out_shape = [jax.ShapeDtypeStruct((t, wt), dt) for (wt, dt) in drow]
    out_shape += [jax.ShapeDtypeStruct(p.shape, F32) for p in pars]
    return pl.pallas_call(
        body,
        name=name,
        grid=(t // tq, ncol),
        in_specs=[_row_spec(s, tq) for s in rows] + [_full_spec(p) for p in pars] + [_row_spec(s, tq) for s in flat_cots],
        out_specs=out_specs,
        out_shape=out_shape,
        compiler_params=_params(("arbitrary", "arbitrary") if npar else ("parallel", "parallel")),
    )(*[s[0] for s in rows], *pars, *[s[0] for s in flat_cots])


def _whole(a, w=None):
    return (a, a.shape[1] if w is None else w, 0)


def _rms_fn(ql, kvl, gq, gk):
    def one(x, g):
        return x * lax.rsqrt(jnp.mean(x * x, -1, keepdims=True) + 1e-6) * g

    return one(ql, gq), one(kvl, gk)


def _layernorm(s, g, b):
    mu = jnp.mean(s, -1, keepdims=True)
    d = s - mu
    var = jnp.mean(d * d, -1, keepdims=True)
    return d * lax.rsqrt(var + 1e-5) * g + b


def _ln_mix_fn(alpha):
    def fn(h, mix, g, b):
        return (_layernorm(alpha * h + mix, g, b),)

    return fn


def _ln_glu_fn(alpha):
    def fn(h, val, gate, g, b):
        return (_layernorm(alpha * h + val * jax.nn.sigmoid(gate), g, b),)

    return fn


def _ln_ffn_fn(alpha):
    def fn(h, f, pw, gate, g, b):
        return (_layernorm(alpha * h + f + pw * jax.nn.sigmoid(gate), g, b),)

    return fn


def _swiglu_fn(g, u):
    return (jax.nn.silu(g) * u,)


def _gelu_fn(ys, h, d):
    return (jax.nn.gelu(ys + d * h),)


def _merge_fn(o1, o2, o3, l1, l2, l3):
    m = jnp.maximum(jnp.maximum(l1, l2), l3)
    e1, e2, e3 = jnp.exp(l1 - m), jnp.exp(l2 - m), jnp.exp(l3 - m)
    return ((e1 * o1 + e2 * o2 + e3 * o3) / (e1 + e2 + e3),)


def _swap_halves(t):
    w = t.shape[1]
    lane = lax.broadcasted_iota(jnp.int32, t.shape, 1) % LANE
    up = jnp.where(lane < ROPE_DIM, pltpu.roll(t, ROPE_DIM // 2, 1), 0.0)
    return jnp.where(lane < ROPE_DIM // 2, pltpu.roll(t, w - ROPE_DIM // 2, 1), up)


def _rope_fwd_fn(nh):
    def fn(qpe, kpe, cos, sin):
        cq, sq = jnp.tile(cos, (1, nh)), jnp.tile(sin, (1, nh))
        return qpe * cq + _swap_halves(qpe) * sq, kpe * cos + _swap_halves(kpe) * sin

    return fn


def _rope_bwd_fn(nh):
    def fn(dq, dk_heads, cos, sin):
        cq, sq = jnp.tile(cos, (1, nh)), jnp.tile(sin, (1, nh))
        dk = dk_heads[:, :LANE]
        for h in range(1, nh):
            dk = dk + dk_heads[:, h * LANE : (h + 1) * LANE]
        return dq * cq + _swap_halves(dq * sq), dk * cos + _swap_halves(dk * sin)

    return fn


def _rope_tables(positions, name):
    t = positions.shape[0]
    tq = _tile(t, (512, 128, 8))
    half = ROPE_DIM // 2

    def body(p_ref, c_ref, s_ref):
        lane = lax.broadcasted_iota(jnp.int32, (tq, LANE), 1)
        idx = (lane % half).astype(F32)
        inv_freq = jnp.exp(idx * (-math.log(ROPE_THETA) / half))
        ang = p_ref[...].astype(F32) * inv_freq
        live = lane < ROPE_DIM
        c_ref[...] = jnp.where(live, jnp.cos(ang), 0.0)
        s_ref[...] = jnp.where(live, jnp.where(lane < half, -jnp.sin(ang), jnp.sin(ang)), 0.0)

    return pl.pallas_call(
        body,
        name=name,
        grid=(t // tq,),
        in_specs=[pl.BlockSpec((tq, 1), lambda i: (i, 0))],
        out_specs=[pl.BlockSpec((tq, LANE), lambda i: (i, 0))] * 2,
        out_shape=[jax.ShapeDtypeStruct((t, LANE), F32)] * 2,
        compiler_params=_params(("parallel",)),
    )(positions)


def _loss_kernel(y, target, name):
    t, d = y.shape
    tq = _tile(t, (256, 128, 8))

    def body(y_ref, t_ref, dy_ref, l_ref):
        @pl.when(pl.program_id(0) == 0)
        def _():
            l_ref[...] = jnp.zeros_like(l_ref)

        e = y_ref[...] - t_ref[...]
        dy_ref[...] = e * (1.0 / d)
        l_ref[...] += jnp.sum(e * e) * (0.5 / d)

    return pl.pallas_call(
        body,
        name=name,
        grid=(t // tq,),
        in_specs=[pl.BlockSpec((tq, d), lambda i: (i, 0))] * 2,
        out_specs=[pl.BlockSpec((tq, d), lambda i: (i, 0)), pl.BlockSpec((8, LANE), lambda i: (0, 0))],
        out_shape=[jax.ShapeDtypeStruct((t, d), F32), jax.ShapeDtypeStruct((8, LANE), F32)],
        compiler_params=_params(("arbitrary",)),
    )(y, target)


def _addn(arrs, name):
    w = arrs[0].shape[1]
    return _rowwise(lambda *v: (functools.reduce(lambda p, q: p + q, v),), [_whole(a) for a in arrs], [], [(w, w, F32)],
                    name=name, tq=256)[0]


def _mla_tiles(t):
    tq = _tile(t, (512, 256, 128))
    return tq, t // tq


def _call_with_side(body, side, n_in, grid, *, name, in_specs, out_specs, out_shape):
    if side is None:
        return pl.pallas_call(body, name=name, grid=grid, in_specs=in_specs, out_specs=out_specs, out_shape=out_shape,
                              compiler_params=_params(("parallel", "parallel")))
    n_out = len(out_specs)

    def wrapped(*refs):
        ins, outs = refs[:n_in], refs[n_in + 1 : n_in + 1 + n_out]
        done = _run_side(side, (refs[n_in], refs[n_in + 1 + n_out], *refs[n_in + 2 + n_out :]), grid[0], grid[1])
        body(*ins, *outs)
        done()

    call = pl.pallas_call(wrapped, name=name, grid=grid, in_specs=in_specs + [ANY_SPEC], out_specs=out_specs + [ANY_SPEC],
                          out_shape=out_shape + [_side_out_shape(side)], scratch_shapes=COMM_SEMS,
                          compiler_params=_params(("arbitrary", "arbitrary")))
    return lambda *args: call(*args, side[1])


def _mla_fwd(qf, qpe, kvf, kpe, nh, name, side=None):
    t = qf.shape[0]
    tq, nq = _mla_tiles(t)
    scale = (NOPE_DIM + ROPE_DIM) ** -0.5

    def body(qn_ref, qp_ref, kn_ref, kp_ref, v_ref, o_ref, lse_ref):
        i = pl.program_id(1)
        qn = qn_ref[...].astype(BF16)
        qp = qp_ref[...]

        def step(j, carry, masked):
            m, l, acc = carry
            ks = pl.ds(pl.multiple_of(j * tq, tq), tq)
            s = (_dot(qn, kn_ref[ks, :], NT) + _dot(qp, kp_ref[ks, :], NT)) * scale
            if masked:
                row = lax.broadcasted_iota(jnp.int32, (tq, tq), 0)
                col = lax.broadcasted_iota(jnp.int32, (tq, tq), 1)
                s = jnp.where(col <= row, s, NEG)
            m_new = jnp.maximum(m, jnp.max(s, -1, keepdims=True))
            p = jnp.exp(s - m_new)
            a = jnp.exp(m - m_new)
            return m_new, a * l + jnp.sum(p, -1, keepdims=True), a * acc + _dot(p.astype(BF16), v_ref[ks, :], NN)

        init = (jnp.full((tq, 1), NEG, F32), jnp.zeros((tq, 1), F32), jnp.zeros((tq, V_DIM), F32))
        carry = lax.fori_loop(0, i, lambda j, c: step(j, c, False), init)
        m, l, acc = step(i, carry, True)
        o_ref[...] = (acc / l).astype(o_ref.dtype)
        lse_ref[...] = jnp.broadcast_to(m + jnp.log(l), (tq, LANE))

    blk = lambda h, i: (i, h)
    return _call_with_side(
        body, side, 5, (nh, nq),
        name=name,
        in_specs=[
            pl.BlockSpec((tq, LANE), blk),
            pl.BlockSpec((tq, LANE), blk),
            pl.BlockSpec((t, LANE), lambda h, i: (0, h)),
            pl.BlockSpec((t, LANE), lambda h, i: (0, 0)),
            pl.BlockSpec((t, LANE), lambda h, i: (0, nh + h)),
        ],
        out_specs=[pl.BlockSpec((tq, LANE), blk), pl.BlockSpec((tq, LANE), blk)],
        out_shape=[jax.ShapeDtypeStruct((t, nh * LANE), BF16), jax.ShapeDtypeStruct((t, nh * LANE), F32)],
    )(qf, qpe, kvf, kpe, kvf)


def _mla_bwd_q(qf, qpe, kvf, kpe, do, o, lse, nh, name, side=None):
    t = qf.shape[0]
    tq, nq = _mla_tiles(t)
    scale = (NOPE_DIM + ROPE_DIM) ** -0.5

    def body(qn_ref, qp_ref, kn_ref, kp_ref, v_ref, do_ref, o_ref, lse_ref, dqn_ref, dqp_ref, dl_ref):
        i = pl.program_id(1)
        qn = qn_ref[...].astype(BF16)
        qp = qp_ref[...]
        dof = do_ref[...].astype(F32)
        dob = dof.astype(BF16)
        delta = jnp.sum(dof * o_ref[...].astype(F32), -1, keepdims=True)
        lse1 = lse_ref[:, :1]

        def step(j, carry, masked):
            dqn, dqp = carry
            ks = pl.ds(pl.multiple_of(j * tq, tq), tq)
            kn, kp = kn_ref[ks, :], kp_ref[ks, :]
            s = (_dot(qn, kn, NT) + _dot(qp, kp, NT)) * scale
            p = jnp.exp(s - lse1)
            if masked:
                row = lax.broadcasted_iota(jnp.int32, (tq, tq), 0)
                col = lax.broadcasted_iota(jnp.int32, (tq, tq), 1)
                p = jnp.where(col <= row, p, 0.0)
            dp = _dot(dob, v_ref[ks, :], NT)
            ds = (p * (dp - delta) * scale).astype(BF16)
            return dqn + _dot(ds, kn, NN), dqp + _dot(ds, kp, NN)

        init = (jnp.zeros((tq, LANE), F32), jnp.zeros((tq, LANE), F32))
        carry = lax.fori_loop(0, i, lambda j, c: step(j, c, False), init)
        dqn, dqp = step(i, carry, True)
        dqn_ref[...] = dqn
        dqp_ref[...] = dqp
        dl_ref[...] = jnp.broadcast_to(delta, (tq, LANE))

    blk = lambda h, i: (i, h)
    bs = pl.BlockSpec((tq, LANE), blk)
    return _call_with_side(
        body, side, 8, (nh, nq),
        name=name,
        in_specs=[
            bs,
            bs,
            pl.BlockSpec((t, LANE), lambda h, i: (0, h)),
            pl.BlockSpec((t, LANE), lambda h, i: (0, 0)),
            pl.BlockSpec((t, LANE), lambda h, i: (0, nh + h)),
            bs,
            bs,
            bs,
        ],
        out_specs=[bs, bs, bs],
        out_shape=[jax.ShapeDtypeStruct((t, nh * LANE), F32)] * 3,
    )(qf, qpe, kvf, kpe, kvf, do, o, lse)


def _mla_bwd_kv(qn16, qpe, kvf, kpe, do16, lse_row, delta_row, nh, name, side=None):
    t = qn16.shape[0]
    tq, nq = _mla_tiles(t)
    scale = (NOPE_DIM + ROPE_DIM) ** -0.5

    def body(kn_ref, kp_ref, v_ref, qn_ref, qp_ref, do_ref, lse_ref, dl_ref, dkn_ref, dkp_ref, dv_ref):
        j = pl.program_id(1)
        kn, kp, v = kn_ref[...], kp_ref[...], v_ref[...]

        def step(i, carry, masked):
            dkn, dkp, dv = carry
            qs = pl.ds(pl.multiple_of(i * tq, tq), tq)
            qn, qp, dob = qn_ref[qs, :], qp_ref[qs, :], do_ref[qs, :]
            st = (_dot(kn, qn, NT) + _dot(kp, qp, NT)) * scale
            pt = jnp.exp(st - lse_ref[0, :, qs])
            if masked:
                key = lax.broadcasted_iota(jnp.int32, (tq, tq), 0)
                qry = lax.broadcasted_iota(jnp.int32, (tq, tq), 1)
                pt = jnp.where(key <= qry, pt, 0.0)
            dv = dv + _dot(pt.astype(BF16), dob, NN)
            dpt = _dot(v, dob, NT)
            dst = (pt * (dpt - dl_ref[0, :, qs]) * scale).astype(BF16)
            return dkn + _dot(dst, qn, NN), dkp + _dot(dst, qp, NN), dv

        z = jnp.zeros((tq, LANE), F32)
        carry = step(j, (z, z, z), True)
        dkn, dkp, dv = lax.fori_loop(j + 1, nq, lambda i, c: step(i, c, False), carry)
        dkn_ref[...] = dkn
        dkp_ref[...] = dkp
        dv_ref[...] = dv

    blk = pl.BlockSpec((tq, LANE), lambda h, j: (j, h))
    res = lambda f: pl.BlockSpec((t, LANE), f)
    row = pl.BlockSpec((1, 1, t), lambda h, j: (h, 0, 0))
    return _call_with_side(
        body, side, 8, (nh, nq),
        name=name,
        in_specs=[
            blk,
            pl.BlockSpec((tq, LANE), lambda h, j: (j, 0)),
            pl.BlockSpec((tq, LANE), lambda h, j: (j, nh + h)),
            res(lambda h, j: (0, h)),
            res(lambda h, j: (0, h)),
            res(lambda h, j: (0, h)),
            row,
            row,
        ],
        out_specs=[blk, blk, blk],
        out_shape=[jax.ShapeDtypeStruct((t, nh * LANE), F32)] * 3,
    )(kvf, kpe, kvf, qn16, qpe, do16, lse_row, delta_row)


def _alibi_slopes(n):
    return [float(2.0 ** (-8.0 * i / n)) for i in range(1, n + 1)]


def _band_masks(has_prev):
    qi = lax.broadcasted_iota(jnp.int32, (BLK, BLK), 0)
    ki = lax.broadcasted_iota(jnp.int32, (BLK, BLK), 1)
    return ki <= qi, (ki >= qi) & has_prev, (qi - ki).astype(F32), (qi - ki + BLK).astype(F32)


def _band_fwd(zd, nh, dil, name):
    t = zd.shape[0]
    nbc = t // BLK // dil
    scale = DIL_HEAD_DIM ** -0.5
    slopes = _alibi_slopes(nh)
    dw = nh * LANE

    def body(q_ref, kc_ref, kp_ref, vc_ref, vp_ref, o_ref, l_ref):
        mask_c, mask_p, dist_c, dist_p = _band_masks(pl.program_id(1) > 0)
        for h in range(nh):
            sl = slice(h * LANE, (h + 1) * LANE)
            q = q_ref[:, sl]
            bias = slopes[h] * dil
            sc = jnp.where(mask_c, _dot(q, kc_ref[:, sl], NT) * scale - bias * dist_c, NEG)
            sp = jnp.where(mask_p, _dot(q, kp_ref[:, sl], NT) * scale - bias * dist_p, NEG)
            m = jnp.maximum(jnp.max(sc, -1, keepdims=True), jnp.max(sp, -1, keepdims=True))
            ec, ep = jnp.exp(sc - m), jnp.exp(sp - m)
            l = jnp.sum(ec, -1, keepdims=True) + jnp.sum(ep, -1, keepdims=True)
            inv = 1.0 / l
            o = _dot((ec * inv).astype(BF16), vc_ref[:, sl], NN) + _dot((ep * inv).astype(BF16), vp_ref[:, sl], NN)
            o_ref[:, sl] = o
            l_ref[:, sl] = jnp.broadcast_to(m + jnp.log(l), (BLK, LANE))

    zv = zd.reshape(t // dil, dil * 3 * dw)
    prev = lambda i: jnp.maximum(i - 1, 0)
    spec = lambda f: pl.BlockSpec((BLK, dw), f)
    o, lse = pl.pallas_call(
        body,
        name=name,
        grid=(dil, nbc),
        in_specs=[
            spec(lambda r, i: (i, 3 * r)),
            spec(lambda r, i: (i, 3 * r + 1)),
            spec(lambda r, i: (prev(i), 3 * r + 1)),
            spec(lambda r, i: (i, 3 * r + 2)),
            spec(lambda r, i: (prev(i), 3 * r + 2)),
        ],
        out_specs=[spec(lambda r, i: (i, r))] * 2,
        out_shape=[jax.ShapeDtypeStruct((t // dil, dil * dw), F32)] * 2,
        compiler_params=_params(("parallel", "parallel")),
    )(zv, zv, zv, zv, zv)
    return o.reshape(t, dw), lse.reshape(t, dw)


def _band_bwd(zd, o, lse, do, dl, nh, dil, name):
    t = zd.shape[0]
    nbc = t // BLK // dil
    scale = DIL_HEAD_DIM ** -0.5
    slopes = _alibi_slopes(nh)
    dw = nh * LANE

    def body(q_ref, k_ref, v_ref, kp_ref, vp_ref, qn_ref, o_ref, l_ref, do_ref, dl_ref, on_ref, ln_ref, don_ref, dln_ref,
             dq_ref, dk_ref, dv_ref):
        has_prev = pl.program_id(1) > 0
        has_next = pl.program_id(1) + 1 < nbc
        mask_c, mask_p, dist_c, dist_p = _band_masks(has_prev)
        _, mask_n, _, _ = _band_masks(has_next)
        for h in range(nh):
            sl = slice(h * LANE, (h + 1) * LANE)
            bias = slopes[h] * dil
            q, k, v = q_ref[:, sl], k_ref[:, sl], v_ref[:, sl]
            kp, vp, qn = kp_ref[:, sl], vp_ref[:, sl], qn_ref[:, sl]
            dof, donf = do_ref[:, sl], don_ref[:, sl]
            dob, donb = dof.astype(BF16), donf.astype(BF16)
            lse1, lsen1 = l_ref[:, sl][:, :1], ln_ref[:, sl][:, :1]
            adj = jnp.sum(dl_ref[:, sl] - dof * o_ref[:, sl], -1, keepdims=True)
            adjn = jnp.sum(dln_ref[:, sl] - donf * on_ref[:, sl], -1, keepdims=True)
            pa = jnp.where(mask_c, jnp.exp(_dot(q, k, NT) * scale - bias * dist_c - lse1), 0.0)
            pb = jnp.where(mask_p, jnp.exp(_dot(q, kp, NT) * scale - bias * dist_p - lse1), 0.0)
            dsa = (pa * (_dot(dob, v, NT) + adj)).astype(BF16)
            dsb = (pb * (_dot(dob, vp, NT) + adj)).astype(BF16)
            dq_ref[:, sl] = (_dot(dsa, k, NN) + _dot(dsb, kp, NN)) * scale
            pc = jnp.where(mask_n, jnp.exp(_dot(qn, k, NT) * scale - bias * dist_p - lsen1), 0.0)
            dsc = (pc * (_dot(donb, v, NT) + adjn)).astype(BF16)
            dk_ref[:, sl] = (_dot(dsa, q, TN) + _dot(dsc, qn, TN)) * scale
            dv_ref[:, sl] = _dot(pa.astype(BF16), dob, TN) + _dot(pc.astype(BF16), donb, TN)

    prev = lambda i: jnp.maximum(i - 1, 0)
    nxt = lambda i: jnp.minimum(i + 1, nbc - 1)
    spec = lambda f: pl.BlockSpec((BLK, dw), f)
    zv = zd.reshape(t // dil, dil * 3 * dw)
    view = lambda a: a.reshape(t // dil, dil * dw)
    cur, nx = spec(lambda r, i: (i, r)), spec(lambda r, i: (nxt(i), r))
    grads = pl.pallas_call(
        body,
        name=name,
        grid=(dil, nbc),
        in_specs=[spec(lambda r, i: (i, 3 * r)), spec(lambda r, i: (i, 3 * r + 1)), spec(lambda r, i: (i, 3 * r + 2)),
                  spec(lambda r, i: (prev(i), 3 * r + 1)), spec(lambda r, i: (prev(i), 3 * r + 2)),
                  spec(lambda r, i: (nxt(i), 3 * r)), cur, cur, cur, cur, nx, nx, nx, nx],
        out_specs=[cur] * 3,
        out_shape=[jax.ShapeDtypeStruct((t // dil, dil * dw), F32)] * 3,
        compiler_params=_params(("parallel", "parallel")),
    )(zv, zv, zv, zv, zv, zv, view(o), view(lse), view(do), view(dl), view(o), view(lse), view(do), view(dl))
    return [g.reshape(t, dw) for g in grads]


def _s5_ops(a_re, a_im, ldt, bt_re, bt_im, c_re, c_im):
    L, g, p = S5_CHUNK, S5_GROUP, S5_STATE
    dt = jnp.exp(ldt)
    lr, li = a_re * dt, a_im * dt
    er = jnp.exp(lr)
    lam_re, lam_im = er * jnp.cos(li), er * jnp.sin(li)
    nr, ni = lam_re - 1.0, lam_im
    den = a_re * a_re + a_im * a_im
    fr, fi = (nr * a_re + ni * a_im) / den, (ni * a_re - nr * a_im) / den
    bb_re, bb_im = fr * bt_re - fi * bt_im, fr * bt_im + fi * bt_re

    def power(tau):
        mag = jnp.exp(tau * lr)
        return mag * jnp.cos(tau * li), mag * jnp.sin(tau * li)

    step = lax.broadcasted_iota(jnp.int32, (L, 1), 0).astype(F32)

    def outer(pr, pi, mr, mi):
        re = pr[:, None, :] * mr[None] - pi[:, None, :] * mi[None]
        im = pr[:, None, :] * mi[None] + pi[:, None, :] * mr[None]
        return re.reshape(L * g, p), im.reshape(L * g, p)

    half = float(L // 2)
    cp_re, cp_im = outer(*power(step - half), c_re, c_im)
    pb_re, pb_im = outer(*power(half - step), bb_re, bb_im)
    toep = _dot(cp_re, pb_re, NT, HI) - _dot(cp_im, pb_im, NT, HI)
    trow = lax.broadcasted_iota(jnp.int32, (L * g, L * g), 0) // g
    scol = lax.broadcasted_iota(jnp.int32, (L * g, L * g), 1) // g
    toep = jnp.where(trow >= scol, toep, 0.0)
    et_re, et_im = outer(*power(float(L - 1) - step), bb_re, bb_im)
    f_re, f_im = outer(*power(step + 1.0), c_re, c_im)
    big_re, big_im = power(jnp.full((1, 1), float(L), F32))
    return toep, et_re, et_im, f_re, f_im, big_re, big_im


def _s5_y(ops, u, sp_re, sp_im):
    toep, _, _, f_re, f_im, _, _ = ops
    return _dot(u, toep, NT, HI) + _dot(sp_re, f_re, NT, HI) - _dot(sp_im, f_im, NT, HI)


def _s5_group_specs(ng):
    vec = pl.BlockSpec((1, 1, S5_STATE), lambda g: (g, 0, 0))
    one = pl.BlockSpec((1, 1, 1), lambda g: (g, 0, 0))
    mat = pl.BlockSpec((1, S5_GROUP, S5_STATE), lambda g: (g, 0, 0))
    return [vec, vec, one, mat, mat, mat, mat]


def _s5_load(refs):
    return [r[0] for r in refs]


def _s5_local(prm, u, name):
    ng, n, w = u.shape
    p = S5_STATE

    def body(*refs):
        ops = _s5_ops(*_s5_load(refs[:7]))
        uu = refs[7][0]
        refs[8][0] = _dot(uu, ops[1], NN, HI)
        refs[9][0] = _dot(uu, ops[2], NN, HI)
        refs[10][0] = ops[5]
        refs[11][0] = ops[6]

    blk = lambda a, b: pl.BlockSpec((1, a, b), lambda g: (g, 0, 0))
    return pl.pallas_call(
        body,
        name=name,
        grid=(ng,),
        in_specs=_s5_group_specs(ng) + [blk(n, w)],
        out_specs=[blk(n, p), blk(n, p), blk(1, p), blk(1, p)],
        out_shape=[jax.ShapeDtypeStruct((ng, n, p), F32)] * 2 + [jax.ShapeDtypeStruct((ng, 1, p), F32)] * 2,
        compiler_params=_params(("parallel",)),
    )(*prm, u)


def _s5_carry(e_re, e_im, lam_re, lam_im, name):
    n, w = e_re.shape
    cw = _tile(w, (1024, 512, 256, 128))

    def body(er_ref, ei_ref, lr_ref, li_ref, sr_ref, si_ref):
        lr, li = lr_ref[...], li_ref[...]

        def step(k, carry):
            sr, si = carry
            row = pl.ds(k, 1)
            sr_ref[row, :] = sr
            si_ref[row, :] = si
            return lr * sr - li * si + er_ref[row, :], li * sr + lr * si + ei_ref[row, :]

        z = jnp.zeros((1, cw), F32)
        lax.fori_loop(0, n, step, (z, z))

    col = pl.BlockSpec((n, cw), lambda j: (0, j))
    one = pl.BlockSpec((1, cw), lambda j: (0, j))
    return pl.pallas_call(
        body,
        name=name,
        grid=(w // cw,),
        in_specs=[col, col, one, one],
        out_specs=[col, col],
        out_shape=[jax.ShapeDtypeStruct((n, w), F32)] * 2,
        compiler_params=_params(("parallel",)),
    )(e_re, e_im, lam_re, lam_im)


def _s5_carry_bwd(dsp_re, dsp_im, sp_re, sp_im, lam_re, lam_im, name):
    n, w = dsp_re.shape
    cw = _tile(w, (1024, 512, 256, 128))

    def body(dr_ref, di_ref, sr_ref, si_ref, lr_ref, li_ref, gr_ref, gi_ref, dlr_ref, dli_ref):
        lr, li = lr_ref[...], li_ref[...]

        def step(q, carry):
            gr_next, gi_next, dr_next, di_next, alr, ali = carry
            k = n - 1 - q
            row = pl.ds(k, 1)
            gr = dr_next + lr * gr_next + li * gi_next
            gi = di_next - li * gr_next + lr * gi_next
            gr_ref[row, :] = gr
            gi_ref[row, :] = gi
            sr, si = sr_ref[row, :], si_ref[row, :]
            return gr, gi, dr_ref[row, :], di_ref[row, :], alr + gr * sr + gi * si, ali + gi * sr - gr * si

        z = jnp.zeros((1, cw), F32)
        out = lax.fori_loop(0, n, step, (z, z, z, z, z, z))
        dlr_ref[...] = out[4]
        dli_ref[...] = out[5]

    col = pl.BlockSpec((n, cw), lambda j: (0, j))
    one = pl.BlockSpec((1, cw), lambda j: (0, j))
    return pl.pallas_call(
        body,
        name=name,
        grid=(w // cw,),
        in_specs=[col, col, col, col, one, one],
        out_specs=[col, col, one, one],
        out_shape=[jax.ShapeDtypeStruct((n, w), F32)] * 2 + [jax.ShapeDtypeStruct((1, w), F32)] * 2,
        compiler_params=_params(("parallel",)),
    )(dsp_re, dsp_im, sp_re, sp_im, lam_re, lam_im)


def _s5_out(prm, u, sp_re, sp_im, name):
    ng, n, w = u.shape
    p = S5_STATE

    def body(*refs):
        ops = _s5_ops(*_s5_load(refs[:7]))
        refs[10][0] = _s5_y(ops, refs[7][0], refs[8][0], refs[9][0])

    blk = lambda a, b: pl.BlockSpec((1, a, b), lambda g: (g, 0, 0))
    return pl.pallas_call(
        body,
        name=name,
        grid=(ng,),
        in_specs=_s5_group_specs(ng) + [blk(n, w), blk(n, p), blk(n, p)],
        out_specs=blk(n, w),
        out_shape=jax.ShapeDtypeStruct((ng, n, w), F32),
        compiler_params=_params(("parallel",)),
    )(*prm, u, sp_re, sp_im)


def _s5_bwd_state(prm, dy, name):
    ng, n, w = dy.shape
    p = S5_STATE

    def body(*refs):
        ops = _s5_ops(*_s5_load(refs[:7]))
        d = refs[7][0]
        refs[8][0] = _dot(d, ops[3], NN, HI)
        refs[9][0] = -_dot(d, ops[4], NN, HI)

    blk = lambda a, b: pl.BlockSpec((1, a, b), lambda g: (g, 0, 0))
    return pl.pallas_call(
        body,
        name=name,
        grid=(ng,),
        in_specs=_s5_group_specs(ng) + [blk(n, w)],
        out_specs=[blk(n, p), blk(n, p)],
        out_shape=[jax.ShapeDtypeStruct((ng, n, p), F32)] * 2,
        compiler_params=_params(("parallel",)),
    )(*prm, dy)


def _s5_bwd_main(prm, u, sp_re, sp_im, dy, g_re, g_im, dlam_re, dlam_im, name):
    ng, n, w = u.shape
    p = S5_STATE

    def body(*refs):
        prm_v = _s5_load(refs[:7])
        uu, sr, si, d, gr, gi, dlr, dli = [r[0] for r in refs[7:15]]

        def phi(*args):
            ops = _s5_ops(*args[:7])
            y = _s5_y(ops, args[7], sr, si)
            e_re, e_im = _dot(args[7], ops[1], NN, HI), _dot(args[7], ops[2], NN, HI)
            return (jnp.sum(d * y) + jnp.sum(gr * e_re) + jnp.sum(gi * e_im)
                    + jnp.sum(dlr * ops[5]) + jnp.sum(dli * ops[6]))

        grads = jax.grad(phi, argnums=tuple(range(8)))(*prm_v, uu)
        for q in range(8):
            refs[15 + q][0] = grads[q]

    blk = lambda a, b: pl.BlockSpec((1, a, b), lambda g: (g, 0, 0))
    prm_specs = _s5_group_specs(ng)
    return pl.pallas_call(
        body,
        name=name,
        grid=(ng,),
        in_specs=prm_specs + [blk(n, w), blk(n, p), blk(n, p), blk(n, w), blk(n, p), blk(n, p), blk(1, p), blk(1, p)],
        out_specs=prm_specs + [blk(n, w)],
        out_shape=[jax.ShapeDtypeStruct(a.shape, F32) for a in prm] + [jax.ShapeDtypeStruct((ng, n, w), F32)],
        compiler_params=_params(("parallel",)),
    )(*prm, u, sp_re, sp_im, dy, g_re, g_im, dlam_re, dlam_im)


def _to_groups(h):
    t, d = h.shape
    ng, n = d // S5_GROUP, t // S5_CHUNK
    return h.reshape(n, S5_CHUNK, ng, S5_GROUP).transpose(2, 0, 1, 3).reshape(ng, n, S5_CHUNK * S5_GROUP)


def _from_groups(y, t, d):
    ng, n = d // S5_GROUP, t // S5_CHUNK
    return y.reshape(ng, n, S5_CHUNK, S5_GROUP).transpose(1, 2, 0, 3).reshape(t, d)


def _states_to_cols(e):
    ng, n, p = e.shape
    return e.transpose(1, 0, 2).reshape(n, ng * p)


def _cols_to_states(s, ng):
    n = s.shape[0]
    return s.reshape(n, ng, S5_STATE).transpose(1, 0, 2)


def _me():
    return lax.axis_index("x"), lax.axis_index("y"), lax.axis_index("c")


def _flip(v, bit):
    return 1 - v if bit else v


COMM_SEMS = [pltpu.SemaphoreType.DMA((7,)), pltpu.SemaphoreType.DMA((7,)), pltpu.SemaphoreType.DMA]
ANY_SPEC = pl.BlockSpec(memory_space=pl.ANY)


def _gather_phases(x_ref, out_ref, send_sems, recv_sems, local_sem):
    mx, my, mc = _me()
    me, sibling = (mx, my, mc), (mx, my, 1 - mc)
    chips = [(1 - mx, my), (mx, 1 - my), (1 - mx, 1 - my)]

    def slot(px, py, pc):
        return out_ref.at[4 * px + 2 * py + pc]

    def copy(k, block, to, src=None):
        return pltpu.make_async_remote_copy(
            src_ref=slot(*block) if src is None else src,
            dst_ref=slot(*block),
            send_sem=send_sems.at[k],
            recv_sem=recv_sems.at[k],
            device_id=to,
            device_id_type=pl.DeviceIdType.MESH,
        )

    mine = pltpu.make_async_copy(x_ref, slot(*me), local_sem)
    first = [copy(0, me, sibling, src=x_ref)] + [copy(1 + j, me, (*chip, mc), src=x_ref) for j, chip in enumerate(chips)]
    passed = [copy(4 + j, (*chip, mc), sibling) for j, chip in enumerate(chips)]

    def start():
        mine.start()
        for cp in first:
            cp.start()

    def relay():
        for j, chip in enumerate(chips):
            copy(1 + j, (*chip, mc), me).wait_recv()
            passed[j].start()

    def finish():
        copy(0, sibling, me).wait_recv()
        for j, chip in enumerate(chips):
            copy(4 + j, (*chip, 1 - mc), me).wait_recv()
        for cp in first + passed:
            cp.wait_send()
        mine.wait()

    return start, relay, finish


def _exchange_phases(g_ref, out_ref, send_sems, recv_sems, local_sem):
    mx, my, mc = _me()
    mine_idx = 4 * mx + 2 * my + mc
    own = pltpu.make_async_copy(g_ref.at[mine_idx], out_ref.at[mine_idx], local_sem)
    copies = []
    for m in range(1, N_DEV):
        px, py, pc = _flip(mx, m & 4), _flip(my, m & 2), _flip(mc, m & 1)
        copies.append(
            pltpu.make_async_remote_copy(
                src_ref=g_ref.at[4 * px + 2 * py + pc],
                dst_ref=out_ref.at[mine_idx],
                send_sem=send_sems.at[m - 1],
                recv_sem=recv_sems.at[m - 1],
                device_id=(px, py, pc),
                device_id_type=pl.DeviceIdType.MESH,
            )
        )

    def start():
        own.start()
        for cp in copies:
            cp.start()

    def finish():
        for cp in copies:
            cp.wait_recv()
        for cp in copies:
            cp.wait_send()
        own.wait()

    return start, None, finish


def _side_out_shape(side):
    kind, arr = side
    return jax.ShapeDtypeStruct((N_DEV,) + arr.shape if kind == "gather" else arr.shape, arr.dtype)


def _run_side(side, refs, nsteps_major, nsteps_minor):
    kind = side[0]
    start, relay, finish = (_gather_phases if kind == "gather" else _exchange_phases)(*refs)
    a, b = pl.program_id(0), pl.program_id(1)
    pl.when(jnp.logical_and(a == 0, b == 0))(start)
    if relay is not None:
        pl.when(jnp.logical_and(a == nsteps_major - 1, b == 0))(relay)
    return lambda: pl.when(jnp.logical_and(a == nsteps_major - 1, b == nsteps_minor - 1))(finish)


def _all_gather(x, name):
    def body(*refs):
        start, relay, finish = _gather_phases(*refs)
        start()
        relay()
        finish()

    return pl.pallas_call(body, name=name, in_specs=[ANY_SPEC], out_specs=ANY_SPEC,
                          out_shape=_side_out_shape(("gather", x)), scratch_shapes=COMM_SEMS)(x)


def _all_to_all(g, name):
    def body(*refs):
        start, _, finish = _exchange_phases(*refs)
        start()
        finish()

    return pl.pallas_call(body, name=name, in_specs=[ANY_SPEC], out_specs=ANY_SPEC,
                          out_shape=_side_out_shape(("exchange", g)), scratch_shapes=COMM_SEMS)(g)


def _sum_slots(recv, name):
    _, r, c = recv.shape
    tr = _tile(r, (256, 128, 64, 32, 16, 8))

    def body(r_ref, o_ref):
        acc = r_ref[0].astype(F32)
        for k in range(1, N_DEV):
            acc = acc + r_ref[k].astype(F32)
        o_ref[...] = acc

    return pl.pallas_call(
        body,
        name=name,
        grid=(r // tr,),
        in_specs=[pl.BlockSpec((N_DEV, tr, c), lambda i: (0, i, 0))],
        out_specs=pl.BlockSpec((tr, c), lambda i: (i, 0)),
        out_shape=jax.ShapeDtypeStruct((r, c), F32),
        compiler_params=_params(("parallel",)),
    )(recv)


def _adamw(w, g, m, v, name):
    shape = w.shape
    c = shape[-1]
    as2d = lambda a: a.reshape(-1, c)
    w2, g2, m2, v2 = as2d(w), as2d(g), as2d(m), as2d(v)
    r = w2.shape[0]
    tr = _tile(r, (256, 128, 64, 32, 16, 8))
    c1 = 1.0 / (1.0 - ADAM_B1 ** ADAM_STEP)
    c2 = 1.0 / (1.0 - ADAM_B2 ** ADAM_STEP)

    def body(w_ref, g_ref, m_ref, v_ref, d_ref, mo_ref, vo_ref):
        gg = g_ref[...]
        mn = ADAM_B1 * m_ref[...] + (1.0 - ADAM_B1) * gg
        vn = ADAM_B2 * v_ref[...] + (1.0 - ADAM_B2) * (gg * gg)
        d_ref[...] = -ADAM_LR * ((mn * c1) / (jnp.sqrt(vn * c2) + ADAM_EPS) + ADAM_WD * w_ref[...])
        mo_ref[...] = mn
        vo_ref[...] = vn

    spec = pl.BlockSpec((tr, c), lambda i: (i, 0))
    d, mn, vn = pl.pallas_call(
        body,
        name=name,
        grid=(r // tr,),
        in_specs=[spec] * 4,
        out_specs=[spec] * 3,
        out_shape=[jax.ShapeDtypeStruct((r, c), F32)] * 3,
        compiler_params=_params(("parallel",)),
    )(w2, g2, m2, v2)
    return d.reshape(shape), mn.reshape(shape), vn.reshape(shape)


COL_SHARDED = ("attn_w_in", "mla_w_q_b", "mla_w_kv_b", "s5_d", "s5_w_glu", "ffn_w_in", "ple_w")
ROW_SHARDED = ("attn_w_out", "ffn_w_out", "ple_gate_w")
REPLICATED = ("mla_q_norm", "mla_kv_norm", "s5_a_re", "s5_a_im", "s5_log_dt", "s5_b_re", "s5_b_im", "s5_c_re",
              "s5_c_im", "ln1_g", "ln1_b", "ln2_g", "ln2_b")
ATTENTION_WEIGHTS = ("attn_w_in", "mla_w_q_b", "mla_w_kv_b", "attn_w_out")
BIG_WEIGHTS = ("attn_w_in", "mla_w_q_b", "mla_w_kv_b", "attn_w_out", "s5_w_glu", "ffn_w_in", "ffn_w_out", "ple_w", "ple_gate_w")
WEIGHTS = ("attn_w_in", "mla_q_norm", "mla_w_q_b", "mla_kv_norm", "mla_w_kv_b", "attn_w_out", "s5_a_re", "s5_a_im",
           "s5_log_dt", "s5_b_re", "s5_b_im", "s5_c_re", "s5_c_im", "s5_d", "s5_w_glu", "ln1_g", "ln1_b", "ffn_w_in",
           "ffn_w_out", "ple_w", "ple_gate_w", "ln2_g", "ln2_b")


def _unshard(name, gathered):
    if name in COL_SHARDED:
        full = jnp.moveaxis(gathered, 0, -2)
        return full.reshape(full.shape[:-2] + (full.shape[-2] * full.shape[-1],))
    full = jnp.moveaxis(gathered, 0, 1)
    return full.reshape((full.shape[0], full.shape[1] * full.shape[2]) + full.shape[3:])


def _by_owner(name, grad):
    if name in COL_SHARDED:
        g = grad.reshape(grad.shape[:-1] + (N_DEV, grad.shape[-1] // N_DEV))
        return jnp.moveaxis(g, -2, 0).reshape(N_DEV, -1)
    if name in ROW_SHARDED:
        g = grad.reshape((grad.shape[0], N_DEV, grad.shape[1] // N_DEV) + grad.shape[2:])
        return jnp.moveaxis(g, 1, 0).reshape(N_DEV, -1)
    return jnp.broadcast_to(grad.reshape(1, -1), (N_DEV, grad.size))


def _owner_rows(name, g):
    k, n = g.shape
    if name in COL_SHARDED:
        return g.reshape(k, N_DEV, n // N_DEV).transpose(1, 0, 2).reshape(N_DEV, -1)
    assert name in ROW_SHARDED, name
    return g.reshape(N_DEV, -1)


def _pack(pieces, axis, dtype, row_mult):
    n = sum(p.shape[axis] for p in pieces)
    pad = (-n) % (PACK_COLS * row_mult)
    if pad:
        shape = list(pieces[0].shape)
        shape[axis] = pad
        pieces = list(pieces) + [jnp.zeros(shape, dtype)]
    flat = jnp.concatenate(pieces, axis=axis)
    return flat.reshape(flat.shape[:axis] + ((n + pad) // PACK_COLS, PACK_COLS))


def _twice(fn):
    return lambda *a: fn(*a) * 2


def _ffn_block(h, hb, p_i, w_in, w_out, w_ple, w_pg, g, b, alpha, tag):
    t, d = h.shape
    hid = w_out.shape[0]
    cw = _tile(hid, (1408, 512, 256, 128))
    ncb = hid // cw
    gu = _mm(hb, w_in, name=f"ffn_in_{tag}")
    act = _rowwise(_swiglu_fn, [(gu, cw, 0), (gu, cw, ncb)], [], [(hid, cw, BF16)], name=f"swiglu_{tag}", tq=512, ncol=ncb)[0]
    f = _mm(act, w_out, name=f"ffn_out_{tag}")
    pw = _mm(p_i, w_ple, name=f"ple_{tag}")
    gate = _mm(hb, w_pg, name=f"ple_gate_{tag}")
    out, outb = _rowwise(_twice(_ln_ffn_fn(alpha)), [_whole(h), _whole(f), _whole(pw), _whole(gate)], [g, b],
                         [(d, d, F32), (d, d, BF16)], name=f"ln2_{tag}", tq=256)
    return out, outb, (h, hb, p_i, gu, act, f, pw, gate)


def _ffn_block_bwd(saved, dout, w_in, w_out, w_pg, g, b, alpha, tag):
    h, hb, p_i, gu, act, f, pw, gate = saved
    t, d = h.shape
    hid = w_out.shape[0]
    cw = _tile(hid, (1408, 512, 256, 128))
    ncb = hid // cw
    dh_a, df, dpw, dgate, dg, db = _rowwise_bwd(
        _ln_ffn_fn(alpha), [_whole(h), _whole(f), _whole(pw), _whole(gate)], [g, b], [dout],
        need=[True] * 4, drow=[(d, F32), (d, BF16), (d, BF16), (d, BF16)], name=f"ln2_bwd_{tag}", tq=128)
    dw_out = _mm(act, df, ta=True, name=f"ffn_out_dw_{tag}")
    dact = _mm(df, w_out, tb=True, out_dtype=BF16, name=f"ffn_out_dx_{tag}")
    dg_, du_ = _rowwise_bwd(_swiglu_fn, [(gu, cw, 0), (gu, cw, ncb)], [], [[(dact, cw, 0)]], need=[True, True],
                            drow=[(hid, BF16), (hid, BF16)], name=f"swiglu_bwd_{tag}", tq=256, ncol=ncb)
    dgu = jnp.concatenate([dg_, du_], axis=1)
    dw_in = _mm(hb, dgu, ta=True, name=f"ffn_in_dw_{tag}")
    dh_b = _mm(dgu, w_in, tb=True, name=f"ffn_in_dx_{tag}")
    dw_ple = _mm(p_i, dpw, ta=True, name=f"ple_dw_{tag}")
    dw_pg = _mm(hb, dgate, ta=True, name=f"ple_gate_dw_{tag}")
    dh_c = _mm(dgate, w_pg, tb=True, name=f"ple_gate_dx_{tag}")
    return [dh_a, dh_b, dh_c], dict(ffn_w_in=dw_in, ffn_w_out=dw_out, ple_w=dw_ple, ple_gate_w=dw_pg, ln2_g=dg, ln2_b=db)


def kernel(x, p, positions, attn_w_in, mla_q_norm, mla_w_q_b, mla_kv_norm, mla_w_kv_b, attn_w_out, s5_a_re, s5_a_im, s5_log_dt, s5_b_re, s5_b_im, s5_c_re, s5_c_im, s5_d, s5_w_glu, ln1_g, ln1_b, ffn_w_in, ffn_w_out, ple_w, ple_gate_w, ln2_g, ln2_b, loss_target, m_attn_w_in, m_mla_q_norm, m_mla_w_q_b, m_mla_kv_norm, m_mla_w_kv_b, m_attn_w_out, m_s5_a_re, m_s5_a_im, m_s5_log_dt, m_s5_b_re, m_s5_b_im, m_s5_c_re, m_s5_c_im, m_s5_d, m_s5_w_glu, m_ln1_g, m_ln1_b, m_ffn_w_in, m_ffn_w_out, m_ple_w, m_ple_gate_w, m_ln2_g, m_ln2_b, v_attn_w_in, v_mla_q_norm, v_mla_w_q_b, v_mla_kv_norm, v_mla_w_kv_b, v_attn_w_out, v_s5_a_re, v_s5_a_im, v_s5_log_dt, v_s5_b_re, v_s5_b_im, v_s5_c_re, v_s5_c_im, v_s5_d, v_s5_w_glu, v_ln1_g, v_ln1_b, v_ffn_w_in, v_ffn_w_out, v_ple_w, v_ple_gate_w, v_ln2_g, v_ln2_b):
    local = dict(attn_w_in=attn_w_in, mla_q_norm=mla_q_norm, mla_w_q_b=mla_w_q_b, mla_kv_norm=mla_kv_norm,
                 mla_w_kv_b=mla_w_kv_b, attn_w_out=attn_w_out, s5_a_re=s5_a_re, s5_a_im=s5_a_im, s5_log_dt=s5_log_dt,
                 s5_b_re=s5_b_re, s5_b_im=s5_b_im, s5_c_re=s5_c_re, s5_c_im=s5_c_im, s5_d=s5_d, s5_w_glu=s5_w_glu,
                 ln1_g=ln1_g, ln1_b=ln1_b, ffn_w_in=ffn_w_in, ffn_w_out=ffn_w_out, ple_w=ple_w, ple_gate_w=ple_gate_w,
                 ln2_g=ln2_g, ln2_b=ln2_b)
    mom_m = dict(zip(WEIGHTS, (m_attn_w_in, m_mla_q_norm, m_mla_w_q_b, m_mla_kv_norm, m_mla_w_kv_b, m_attn_w_out, m_s5_a_re, m_s5_a_im, m_s5_log_dt, m_s5_b_re, m_s5_b_im, m_s5_c_re, m_s5_c_im, m_s5_d, m_s5_w_glu, m_ln1_g, m_ln1_b, m_ffn_w_in, m_ffn_w_out, m_ple_w, m_ple_gate_w, m_ln2_g, m_ln2_b)))
    mom_v = dict(zip(WEIGHTS, (v_attn_w_in, v_mla_q_norm, v_mla_w_q_b, v_mla_kv_norm, v_mla_w_kv_b, v_attn_w_out, v_s5_a_re, v_s5_a_im, v_s5_log_dt, v_s5_b_re, v_s5_b_im, v_s5_c_re, v_s5_c_im, v_s5_d, v_s5_w_glu, v_ln1_g, v_ln1_b, v_ffn_w_in, v_ffn_w_out, v_ple_w, v_ple_gate_w, v_ln2_g, v_ln2_b)))

    t, d = x.shape[1], x.shape[2]
    depth = ln1_g.shape[0]
    alpha = (2.0 * depth) ** 0.25
    ql, kvl = mla_q_norm.shape[-1], mla_kv_norm.shape[-1]
    nh = mla_w_q_b.shape[-1] * N_DEV // (NOPE_DIM + ROPE_DIM)
    dw = (attn_w_in.shape[-1] * N_DEV - ql - kvl - ROPE_DIM) // 3
    ndh = dw // DIL_HEAD_DIM
    ng = d // S5_GROUP
    assert ql == kvl and (ql + kvl) % LANE == 0 and nh * V_DIM == dw

    sharded = [n for n in WEIGHTS if n in COL_SHARDED or n in ROW_SHARDED]
    def as_words(n):
        if n == "s5_d":
            return lax.bitcast_convert_type(local[n], BF16).reshape(-1)
        if n == "attn_w_in":
            return local[n][0].T.astype(BF16).reshape(-1)
        return local[n].astype(BF16).reshape(-1)

    full = {}

    def unpack_weights(names, gathered):
        gathered, off = gathered.reshape(N_DEV, -1), 0
        for n in names:
            size = local[n].size * (2 if n == "s5_d" else 1)
            piece = gathered[:, off : off + size]
            off += size
            if n == "attn_w_in":
                full[n] = piece.reshape(N_DEV * local[n].shape[2], d)
                continue
            if n == "s5_d":
                piece = lax.bitcast_convert_type(piece.reshape((N_DEV,) + local[n].shape + (2,)), F32)
            full[n] = _unshard(n, piece.reshape((N_DEV,) + local[n].shape))

    first_w = [n for n in sharded if n in ATTENTION_WEIGHTS]
    later_w = [n for n in sharded if n not in ATTENTION_WEIGHTS]
    unpack_weights(first_w, _all_gather(_pack([as_words(n) for n in first_w], 0, BF16, 16), "gather_attn_weights"))
    later_packed = _pack([as_words(n) for n in later_w], 0, BF16, 16)

    w_in_t = full["attn_w_in"]
    lat = ql + kvl
    w_lat_t = jnp.concatenate([w_in_t[: lat + ROPE_DIM], jnp.zeros((LANE - ROPE_DIM, d), BF16)], axis=0)
    w_dil_t = w_in_t[lat + ROPE_DIM :]
    wq = full["mla_w_q_b"][0].reshape(ql, nh, NOPE_DIM + ROPE_DIM)
    wq_pe = jnp.pad(wq[:, :, NOPE_DIM:], ((0, 0), (0, 0), (0, LANE - ROPE_DIM)))
    wqp = jnp.concatenate([wq[:, :, :NOPE_DIM].reshape(ql, nh * LANE), wq_pe.reshape(ql, nh * LANE)], axis=1)
    wkv = full["mla_w_kv_b"][0].reshape(kvl, nh, NOPE_DIM + V_DIM)
    wkvp = jnp.concatenate([wkv[:, :, :NOPE_DIM].reshape(kvl, nh * LANE), wkv[:, :, NOPE_DIM:].reshape(kvl, nh * LANE)], axis=1)
    w_out = full["attn_w_out"][0]

    h0 = x[0]
    h0b = h0.astype(BF16)
    pb = p.astype(BF16)
    target = loss_target[0]
    pos = positions.reshape(t, 1)
    grads = {}

    z = _mm(h0b, w_lat_t, tb=True, name="attn_in_lat")
    zd = _mm(h0b, w_dil_t, tb=True, out_dtype=BF16, name="attn_in_dil")
    gq, gk = mla_q_norm.reshape(1, ql), mla_kv_norm.reshape(1, kvl)
    qn, kvn = _rowwise(_rms_fn, [(z, ql, 0), (z, kvl, 1)], [gq, gk], [(ql, ql, BF16), (kvl, kvl, BF16)], name="rms", tq=256)
    qf = _mm(qn, wqp, name="q_up")
    kvf = _mm(kvn, wkvp, out_dtype=BF16, name="kv_up")
    cos, sin = _rope_tables(pos, "rope_tables")
    pe_cb = lat // LANE
    qpe, kpe = _rowwise(_rope_fwd_fn(nh), [(qf, nh * LANE, 1), (z, LANE, pe_cb), _whole(cos), _whole(sin)], [],
                        [(nh * LANE, nh * LANE, BF16), (LANE, LANE, BF16)], name="rope", tq=256)
    out_a, lse_a, later_gathered = _mla_fwd(qf, qpe, kvf, kpe, nh, "mla_fwd", side=("gather", later_packed))
    unpack_weights(later_w, later_gathered)
    w_glu = full["s5_w_glu"][0]
    d_skip = full["s5_d"]

    band = []
    for window, dil in DIL_BRANCHES:
        assert window // dil == BLK
        band.append((dil,) + tuple(_band_fwd(zd, ndh, dil, f"band_fwd_d{dil}")))
    merge_rows = [_whole(b[1]) for b in band] + [_whole(b[2]) for b in band]
    out_b = _rowwise(_merge_fn, merge_rows, [], [(dw, dw, BF16)], name="merge", tq=256)[0]
    att = jnp.concatenate([out_a, out_b], axis=1)
    mix0 = _mm(att, w_out, name="attn_out")
    g1, b1 = ln1_g[0:1], ln1_b[0:1]
    h1, h1b = _rowwise(_twice(_ln_mix_fn(alpha)), [_whole(h0), _whole(mix0)], [g1, b1], [(d, d, F32), (d, d, BF16)],
                       name="ln1_l0", tq=256)
    h2, _, saved_f0 = _ffn_block(h1, h1b, pb[0, 0], full["ffn_w_in"][0], full["ffn_w_out"][0], full["ple_w"][0],
                                 full["ple_gate_w"][0], ln2_g[0:1], ln2_b[0:1], alpha, "l0")

    prm = [s5_a_re[0].reshape(ng, 1, S5_STATE), s5_a_im[0].reshape(ng, 1, S5_STATE), s5_log_dt[0].reshape(ng, 1, 1),
           s5_b_re[0].transpose(0, 2, 1), s5_b_im[0].transpose(0, 2, 1), s5_c_re[0], s5_c_im[0]]
    u = _to_groups(h2)
    e_re, e_im, lam_re, lam_im = _s5_local(prm, u, "s5_local")
    lam_re_c, lam_im_c = lam_re.reshape(1, ng * S5_STATE), lam_im.reshape(1, ng * S5_STATE)
    sp_re_c, sp_im_c = _s5_carry(_states_to_cols(e_re), _states_to_cols(e_im), lam_re_c, lam_im_c, "s5_carry")
    sp_re, sp_im = _cols_to_states(sp_re_c, ng), _cols_to_states(sp_im_c, ng)
    ys = _from_groups(_s5_out(prm, u, sp_re, sp_im, "s5_out"), t, d)
    z5 = _rowwise(_gelu_fn, [_whole(ys), _whole(h2)], [d_skip], [(d, d, BF16)], name="gelu", tq=256)[0]
    vg = _mm(z5, w_glu, name="glu_in")
    g3, b3 = ln1_g[1:2], ln1_b[1:2]
    h3, h3b = _rowwise(_twice(_ln_glu_fn(alpha)), [_whole(h2), (vg, d, 0), (vg, d, 1)], [g3, b3],
                       [(d, d, F32), (d, d, BF16)], name="ln1_l1", tq=256)
    h4, _, saved_f1 = _ffn_block(h3, h3b, pb[1, 0], full["ffn_w_in"][1], full["ffn_w_out"][1], full["ple_w"][1],
                                 full["ple_gate_w"][1], ln2_g[1:2], ln2_b[1:2], alpha, "l1")

    dh4, loss_acc = _loss_kernel(h4, target, "loss")
    loss = lax.psum(loss_acc[0, 0], AXES)

    dh3, gf1 = _ffn_block_bwd(saved_f1, [_whole(dh4)], full["ffn_w_in"][1], full["ffn_w_out"][1],
                              full["ple_gate_w"][1], ln2_g[1:2], ln2_b[1:2], alpha, "l1")
    dh2_a, dval, dgate, dg3, db3 = _rowwise_bwd(
        _ln_glu_fn(alpha), [_whole(h2), (vg, d, 0), (vg, d, 1)], [g3, b3], [[_whole(a) for a in dh3]],
        need=[True] * 3, drow=[(d, F32), (d, BF16), (d, BF16)], name="ln1_bwd_l1", tq=128)
    dvg = jnp.concatenate([dval, dgate], axis=1)
    dw_glu = _mm(z5, dvg, ta=True, name="glu_dw")
    dz5 = _mm(dvg, w_glu, tb=True, name="glu_dx")
    dys, dh2_b, dd = _rowwise_bwd(_gelu_fn, [_whole(ys), _whole(h2)], [d_skip], [[_whole(dz5)]], need=[True, True],
                                  drow=[(d, F32), (d, F32)], name="gelu_bwd", tq=128)
    grads["s5_d"] = dd
    dy = _to_groups(dys)
    dsp_re, dsp_im = _s5_bwd_state(prm, dy, "s5_bwd_state")
    g_re_c, g_im_c, dlam_re_c, dlam_im_c = _s5_carry_bwd(_states_to_cols(dsp_re), _states_to_cols(dsp_im), sp_re_c, sp_im_c,
                                                         lam_re_c, lam_im_c, "s5_carry_bwd")
    s5g = _s5_bwd_main(prm, u, sp_re, sp_im, dy, _cols_to_states(g_re_c, ng), _cols_to_states(g_im_c, ng),
                       dlam_re_c.reshape(ng, 1, S5_STATE), dlam_im_c.reshape(ng, 1, S5_STATE), "s5_bwd_main")
    grads["s5_a_re"] = s5g[0].reshape(s5_a_re.shape)
    grads["s5_a_im"] = s5g[1].reshape(s5_a_im.shape)
    grads["s5_log_dt"] = s5g[2].reshape(s5_log_dt.shape)
    grads["s5_b_re"] = s5g[3].transpose(0, 2, 1)[None]
    grads["s5_b_im"] = s5g[4].transpose(0, 2, 1)[None]
    grads["s5_c_re"] = s5g[5][None]
    grads["s5_c_im"] = s5g[6][None]
    dh2_c = _from_groups(s5g[7], t, d)

    dh1, gf0 = _ffn_block_bwd(saved_f0, [_whole(dh2_a), _whole(dh2_b), _whole(dh2_c)], full["ffn_w_in"][0],
                              full["ffn_w_out"][0], full["ple_gate_w"][0], ln2_g[0:1], ln2_b[0:1], alpha, "l0")
    for k in ("ln2_g", "ln2_b"):
        grads[k] = jnp.concatenate([gf0[k], gf1[k]])

    shards = {}

    def send_buffer(items, dtype, row_mult):
        return _pack([rows.astype(dtype) for _, rows in items], 1, dtype, row_mult)

    def take(items, recv, tag):
        flat, off = _sum_slots(recv, f"sum_grads_{tag}").reshape(-1), 0
        for key, rows in items:
            shards[key] = flat[off : off + rows.shape[1]]
            off += rows.shape[1]

    ffn_names = ("ffn_w_in", "ffn_w_out", "ple_w", "ple_gate_w")
    items_l1 = [(("s5_w_glu", 0), _owner_rows("s5_w_glu", dw_glu))] + [((k, 1), _owner_rows(k, gf1[k])) for k in ffn_names]
    items_l0 = [((k, 0), _owner_rows(k, gf0[k])) for k in ffn_names]
    dh0_a, dmix, dg1, db1 = _rowwise_bwd(_ln_mix_fn(alpha), [_whole(h0), _whole(mix0)], [g1, b1], [[_whole(a) for a in dh1]],
                                         need=[True, True], drow=[(d, F32), (d, BF16)], name="ln1_bwd_l0", tq=128)
    grads["ln1_g"] = jnp.concatenate([dg1, dg3])
    grads["ln1_b"] = jnp.concatenate([db1, db3])
    dw_out = _mm(att, dmix, ta=True, name="attn_out_dw")
    datt = _mm(dmix, w_out, tb=True, name="attn_out_dx")

    dmerge = _rowwise_bwd(_merge_fn, merge_rows, [], [[(datt, dw, 1)]], need=[True] * 6, drow=[(dw, F32)] * 6,
                          name="merge_bwd", tq=128)
    dq_s = dk_s = dv_s = None
    for k, (dil, o_g, l_g) in enumerate(band):
        dq_n, dk_n, dv_n = _band_bwd(zd, o_g, l_g, dmerge[k], dmerge[3 + k], ndh, dil, f"band_bwd_d{dil}")
        dq_s = dq_n if dq_s is None else dq_s + dq_n
        dk_s = dk_n if dk_s is None else dk_s + dk_n
        dv_s = dv_n if dv_s is None else dv_s + dv_n

    dqn, dqp, delta, recv_l1 = _mla_bwd_q(qf, qpe, kvf, kpe, datt, out_a, lse_a, nh, "mla_bwd_q",
                                          side=("exchange", send_buffer(items_l1, BF16, 16)))
    take(items_l1, recv_l1, "l1")
    to_row = lambda a: a[:, ::LANE].T.reshape(nh, 1, t)
    dkn, dkp, dv, recv_l0 = _mla_bwd_kv(qf[:, : nh * LANE].astype(BF16), qpe, kvf, kpe, datt[:, : nh * LANE].astype(BF16),
                                        to_row(lse_a), to_row(delta), nh, "mla_bwd_kv",
                                        side=("exchange", send_buffer(items_l0, BF16, 16)))
    take(items_l0, recv_l0, "l0")
    dq_pe, dk_pe = _rowwise(_rope_bwd_fn(nh), [_whole(dqp), _whole(dkp), _whole(cos), _whole(sin)], [],
                            [(nh * LANE, nh * LANE, BF16), (LANE, LANE, BF16)], name="rope_bwd", tq=256)
    dqf = jnp.concatenate([dqn.astype(BF16), dq_pe], axis=1)
    dkvf = jnp.concatenate([dkn, dv], axis=1).astype(BF16)
    dwqp = _mm(qn, dqf, ta=True, name="q_up_dw")
    dqn_in = _mm(dqf, wqp, tb=True, name="q_up_dx")
    dwkvp = _mm(kvn, dkvf, ta=True, name="kv_up_dw")
    dkvn_in = _mm(dkvf, wkvp, tb=True, name="kv_up_dx")
    dql, dkvl, dgq, dgk = _rowwise_bwd(_rms_fn, [(z, ql, 0), (z, kvl, 1)], [gq, gk], [[_whole(dqn_in)], [_whole(dkvn_in)]],
                                       need=[True, True], drow=[(ql, BF16), (kvl, BF16)], name="rms_bwd", tq=256)
    grads["mla_q_norm"], grads["mla_kv_norm"] = dgq, dgk
    dz_lat = jnp.concatenate([dql, dkvl, dk_pe], axis=1)
    dz_dil = jnp.concatenate([dq_s, dk_s, dv_s], axis=1).astype(BF16)
    dw_lat_t = _mm(dz_lat, h0b, ta=True, name="attn_in_lat_dw")
    dw_dil_t = _mm(dz_dil, h0b, ta=True, name="attn_in_dil_dw")
    dh0_b = _mm(dz_lat, w_lat_t, name="attn_in_lat_dx")
    dh0_c = _mm(dz_dil, w_dil_t, name="attn_in_dil_dx")
    grad_x = _addn([dh0_a, dh0_b, dh0_c], "grad_x")[None]
    dw_in_t = jnp.concatenate([dw_lat_t[: lat + ROPE_DIM], dw_dil_t], axis=0)
    dwq_n = dwqp[:, : nh * LANE].reshape(ql, nh, NOPE_DIM)
    dwq_r = dwqp[:, nh * LANE :].reshape(ql, nh, LANE)[:, :, :ROPE_DIM]
    dwq = jnp.concatenate([dwq_n, dwq_r], axis=2).reshape(ql, nh * (NOPE_DIM + ROPE_DIM))
    dwkv_k = dwkvp[:, : nh * LANE].reshape(kvl, nh, NOPE_DIM)
    dwkv_v = dwkvp[:, nh * LANE :].reshape(kvl, nh, V_DIM)
    dwkv = jnp.concatenate([dwkv_k, dwkv_v], axis=2).reshape(kvl, nh * (NOPE_DIM + V_DIM))

    items_att = [(("attn_w_in", 0), dw_in_t.reshape(N_DEV, -1)), (("mla_w_q_b", 0), _owner_rows("mla_w_q_b", dwq)),
                 (("mla_w_kv_b", 0), _owner_rows("mla_w_kv_b", dwkv)), (("attn_w_out", 0), _owner_rows("attn_w_out", dw_out))]
    take(items_att, _all_to_all(send_buffer(items_att, BF16, 16), "exchange_grads_attn"), "attn")
    items_small = [((n, None), _by_owner(n, grads[n])) for n in WEIGHTS if n not in BIG_WEIGHTS]
    take(items_small, _all_to_all(send_buffer(items_small, F32, 8), "exchange_grads_small"), "small")

    g_out, d_out, m_out, v_out = [], [], [], []
    for n in WEIGHTS:
        if n == "attn_w_in":
            g = shards[(n, 0)].reshape(local[n].shape[2], d).T[None]
        elif n in BIG_WEIGHTS:
            g = jnp.concatenate([shards[(n, layer)] for layer in range(local[n].shape[0])]).reshape(local[n].shape)
        else:
            g = shards[(n, None)].reshape(local[n].shape)
        dlt, mn, vn = _adamw(local[n], g, mom_m[n], mom_v[n], f"adamw_{n}")
        g_out.append(g)
        d_out.append(dlt)
        m_out.append(mn)
        v_out.append(vn)
    return (loss, grad_x, *g_out, *d_out, *m_out, *v_out)
```

```python
import functools
import math

import numpy as np
import jax
import jax.numpy as jnp
from jax import lax
from jax.experimental import pallas as pl
from jax.experimental.pallas import tpu as pltpu

F32 = jnp.float32
BF16 = jnp.bfloat16

NOPE_DIM, ROPE_DIM, V_DIM = 128, 64, 128
DIL_HEAD_DIM = 128
DIL_BRANCHES = ((128, 1), (512, 4), (2048, 16))
BLK = 128
LANE = 128
ROPE_THETA = 10000.0
S5_GROUP, S5_STATE = 16, 64
S5_CHUNK = 32
NEG = -1e30
ADAM_LR, ADAM_B1, ADAM_B2, ADAM_EPS, ADAM_WD, ADAM_STEP = 0.001, 0.9, 0.999, 1e-08, 0.01, 10
N_DEV = 8
AXES = ("x", "y", "c")
VMEM_LIMIT = 56 * 1024 * 1024
MM_VMEM_BUDGET = 40 * 1024 * 1024
PACK_COLS = 1024
HI = lax.Precision.HIGH


def _tile(n, cands):
    for c in cands:
        if n % c == 0:
            return c
    return n


def _params(sem=None):
    return pltpu.CompilerParams(dimension_semantics=sem, vmem_limit_bytes=VMEM_LIMIT)


def _dot(a, b, dims, prec=None):
    return lax.dot_general(a, b, (dims, ((), ())), preferred_element_type=F32, precision=prec)


NN = ((1,), (0,))
NT = ((1,), (1,))
TN = ((0,), (0,))


def _mm(a, b, *, ta=False, tb=False, out_dtype=F32, name):
    (k, m) = a.shape if ta else a.shape[::-1]
    (n, k2) = b.shape if tb else b.shape[::-1]
    assert k == k2, (a.shape, b.shape, ta, tb)
    tm = _tile(m, (1024, 512, 256, 128))
    tn = _tile(n, (1024, 512, 384, 256, 128))
    sa, sb, so = a.dtype.itemsize, b.dtype.itemsize, jnp.dtype(out_dtype).itemsize

    def fits(tk):
        return 2 * (tm * tk * sa + tk * tn * sb) + 2 * tm * tn * so + 4 * tm * tn <= MM_VMEM_BUDGET

    tk = next((c for c in (2816, 2048, 1536, 1408, 1152, 1024, 768, 512, 384, 256, 128) if k % c == 0 and fits(c)), k)
    nk = k // tk
    dims = (((0,) if ta else (1,)), ((1,) if tb else (0,)))

    def body(a_ref, b_ref, o_ref, *acc):
        part = _dot(a_ref[...].astype(BF16), b_ref[...].astype(BF16), dims)
        if nk == 1:
            o_ref[...] = part.astype(o_ref.dtype)
            return
        (acc_ref,) = acc
        kk = pl.program_id(2)

        @pl.when(kk == 0)
        def _():
            acc_ref[...] = part

        @pl.when(kk > 0)
        def _():
            acc_ref[...] += part

        @pl.when(kk == nk - 1)
        def _():
            o_ref[...] = acc_ref[...].astype(o_ref.dtype)

    a_spec = pl.BlockSpec((tk, tm), lambda i, j, kk: (kk, i)) if ta else pl.BlockSpec((tm, tk), lambda i, j, kk: (i, kk))
    b_spec = pl.BlockSpec((tn, tk), lambda i, j, kk: (j, kk)) if tb else pl.BlockSpec((tk, tn), lambda i, j, kk: (kk, j))
    return pl.pallas_call(
        body,
        name=name,
        grid=(m // tm, n // tn, nk),
        in_specs=[a_spec, b_spec],
        out_specs=pl.BlockSpec((tm, tn), lambda i, j, kk: (i, j)),
        out_shape=jax.ShapeDtypeStruct((m, n), out_dtype),
        scratch_shapes=[pltpu.VMEM((tm, tn), F32)] if nk > 1 else [],
        compiler_params=_params(("parallel", "parallel", "arbitrary")),
    )(a, b)


def _row_spec(spec, tq):
    _, w, cb = spec
    return pl.BlockSpec((tq, w), lambda i, j: (i, cb + j))


def _full_spec(p):
    return pl.BlockSpec(p.shape, lambda i, j: (0,) * p.ndim)


def _rowwise(fn, rows, pars, outs, *, name, tq, ncol=1):
    t = rows[0][0].shape[0]
    tq = _tile(t, (tq, 128, 64, 32, 16, 8))
    nr, npar = len(rows), len(pars)

    def body(*refs):
        vals = [r[...].astype(F32) for r in refs[: nr + npar]]
        res = fn(*vals)
        for o, r in zip(refs[nr + npar :], res):
            o[...] = r.astype(o.dtype)

    res = pl.pallas_call(
        body,
        name=name,
        grid=(t // tq, ncol),
        in_specs=[_row_spec(s, tq) for s in rows] + [_full_spec(p) for p in pars],
        out_specs=[pl.BlockSpec((tq, w), lambda i, j: (i, j)) for (_, w, _) in outs],
        out_shape=[jax.ShapeDtypeStruct((t, wt), dt) for (wt, _, dt) in outs],
        compiler_params=_params(("parallel", "parallel")),
    )(*[s[0] for s in rows], *pars)
    return res


def _rowwise_bwd(fn, rows, pars, cots, *, need, drow, name, tq, ncol=1):
    t = rows[0][0].shape[0]
    tq = _tile(t, (tq, 128, 64, 32, 16, 8))
    nr, npar = len(rows), len(pars)
    assert ncol == 1 or npar == 0
    flat_cots = [s for c in cots for s in c]
    ncot = len(flat_cots)
    want = [k for k in range(nr) if need[k]]

    def body(*refs):
        vals = [r[...].astype(F32) for r in refs[: nr + npar]]
        cref = refs[nr + npar : nr + npar + ncot]
        oref = refs[nr + npar + ncot :]
        cvals, pos = [], 0
        for c in cots:
            acc = cref[pos][...].astype(F32)
            for q in range(1, len(c)):
                acc = acc + cref[pos + q][...].astype(F32)
            cvals.append(acc)
            pos += len(c)

        def closed(*diff):
            full = list(vals)
            for k, dv in zip(want + list(range(nr, nr + npar)), diff):
                full[k] = dv
            return tuple(fn(*full))

        diff_in = [vals[k] for k in want] + vals[nr:]
        _, vjp = jax.vjp(closed, *diff_in)
        grads = vjp(tuple(cvals))
        for q in range(len(want)):
            oref[q][...] = grads[q].astype(oref[q].dtype)
        if npar:
            first = pl.program_id(0) == 0

            @pl.when(first)
            def _():
                for q in range(npar):
                    oref[len(want) + q][...] = jnp.zeros_like(oref[len(want) + q])

            for q in range(npar):
                oref[len(want) + q][...] += grads[len(want) + q]

    out_specs = [pl.BlockSpec((tq, rows[k][1]), lambda i, j, cb=rows[k][2]: (i, j)) for k in want]
    out_specs += [_full_spec(p) for p in pars]
    out_shape = [jax.ShapeDtypeStruct((t, wt), dt) for (wt, dt) in drow]
    out_shape += [jax.ShapeDtypeStruct(p.shape, F32) for p in pars]
    return pl.pallas_call(
        body,
        name=name,
        grid=(t // tq, ncol),
        in_specs=[_row_spec(s, tq) for s in rows] + [_full_spec(p) for p in pars] + [_row_spec(s, tq) for s in flat_cots],
        out_specs=out_specs,
        out_shape=out_shape,
        compiler_params=_params(("arbitrary", "arbitrary") if npar else ("parallel", "parallel")),
    )(*[s[0] for s in rows], *pars, *[s[0] for s in flat_cots])


def _whole(a, w=None):
    return (a, a.shape[1] if w is None else w, 0)


def _rms_fn(ql, kvl, gq, gk):
    def one(x, g):
        return x * lax.rsqrt(jnp.mean(x * x, -1, keepdims=True) + 1e-6) * g

    return one(ql, gq), one(kvl, gk)


def _layernorm(s, g, b):
    mu = jnp.mean(s, -1, keepdims=True)
    d = s - mu
    var = jnp.mean(d * d, -1, keepdims=True)
    return d * lax.rsqrt(var + 1e-5) * g + b


def _ln_mix_fn(alpha):
    def fn(h, mix, g, b):
        return (_layernorm(alpha * h + mix, g, b),)

    return fn


def _ln_glu_fn(alpha):
    def fn(h, val, gate, g, b):
        return (_layernorm(alpha * h + val * jax.nn.sigmoid(gate), g, b),)

    return fn


def _ln_ffn_fn(alpha):
    def fn(h, f, pw, gate, g, b):
        return (_layernorm(alpha * h + f + pw * jax.nn.sigmoid(gate), g, b),)

    return fn


def _swiglu_fn(g, u):
    return (jax.nn.silu(g) * u,)


def _gelu_fn(ys, h, d):
    return (jax.nn.gelu(ys + d * h),)


def _merge_fn(o1, o2, o3, l1, l2, l3):
    m = jnp.maximum(jnp.maximum(l1, l2), l3)
    e1, e2, e3 = jnp.exp(l1 - m), jnp.exp(l2 - m), jnp.exp(l3 - m)
    return ((e1 * o1 + e2 * o2 + e3 * o3) / (e1 + e2 + e3),)


def _swap_halves(t):
    w = t.shape[1]
    lane = lax.broadcasted_iota(jnp.int32, t.shape, 1) % LANE
    up = jnp.where(lane < ROPE_DIM, pltpu.roll(t, ROPE_DIM // 2, 1), 0.0)
    return jnp.where(lane < ROPE_DIM // 2, pltpu.roll(t, w - ROPE_DIM // 2, 1), up)


def _rope_fwd_fn(nh):
    def fn(qpe, kpe, cos, sin):
        cq, sq = jnp.tile(cos, (1, nh)), jnp.tile(sin, (1, nh))
        return qpe * cq + _swap_halves(qpe) * sq, kpe * cos + _swap_halves(kpe) * sin

    return fn


def _rope_bwd_fn(nh):
    def fn(dq, dk_heads, cos, sin):
        cq, sq = jnp.tile(cos, (1, nh)), jnp.tile(sin, (1, nh))
        dk = dk_heads[:, :LANE]
        for h in range(1, nh):
            dk = dk + dk_heads[:, h * LANE : (h + 1) * LANE]
        return dq * cq + _swap_halves(dq * sq), dk * cos + _swap_halves(dk * sin)

    return fn


def _rope_tables(positions, name):
    t = positions.shape[0]
    tq = _tile(t, (512, 128, 8))
    half = ROPE_DIM // 2

    def body(p_ref, c_ref, s_ref):
        lane = lax.broadcasted_iota(jnp.int32, (tq, LANE), 1)
        idx = (lane % half).astype(F32)
        inv_freq = jnp.exp(idx * (-math.log(ROPE_THETA) / half))
        ang = p_ref[...].astype(F32) * inv_freq
        live = lane < ROPE_DIM
        c_ref[...] = jnp.where(live, jnp.cos(ang), 0.0)
        s_ref[...] = jnp.where(live, jnp.where(lane < half, -jnp.sin(ang), jnp.sin(ang)), 0.0)

    return pl.pallas_call(
        body,
        name=name,
        grid=(t // tq,),
        in_specs=[pl.BlockSpec((tq, 1), lambda i: (i, 0))],
        out_specs=[pl.BlockSpec((tq, LANE), lambda i: (i, 0))] * 2,
        out_shape=[jax.ShapeDtypeStruct((t, LANE), F32)] * 2,
        compiler_params=_params(("parallel",)),
    )(positions)


def _loss_kernel(y, target, name):
    t, d = y.shape
    tq = _tile(t, (256, 128, 8))

    def body(y_ref, t_ref, dy_ref, l_ref):
        @pl.when(pl.program_id(0) == 0)
        def _():
            l_ref[...] = jnp.zeros_like(l_ref)

        e = y_ref[...] - t_ref[...]
        dy_ref[...] = e * (1.0 / d)
        l_ref[...] += jnp.sum(e * e) * (0.5 / d)

    return pl.pallas_call(
        body,
        name=name,
        grid=(t // tq,),
        in_specs=[pl.BlockSpec((tq, d), lambda i: (i, 0))] * 2,
        out_specs=[pl.BlockSpec((tq, d), lambda i: (i, 0)), pl.BlockSpec((8, LANE), lambda i: (0, 0))],
        out_shape=[jax.ShapeDtypeStruct((t, d), F32), jax.ShapeDtypeStruct((8, LANE), F32)],
        compiler_params=_params(("arbitrary",)),
    )(y, target)


def _addn(arrs, name):
    w = arrs[0].shape[1]
    return _rowwise(lambda *v: (functools.reduce(lambda p, q: p + q, v),), [_whole(a) for a in arrs], [], [(w, w, F32)],
                    name=name, tq=256)[0]


def _mla_tiles(t):
    tq = _tile(t, (512, 256, 128))
    return tq, t // tq


def _call_with_side(body, side, n_in, grid, *, name, in_specs, out_specs, out_shape):
    if side is None:
        return pl.pallas_call(body, name=name, grid=grid, in_specs=in_specs, out_specs=out_specs, out_shape=out_shape,
                              compiler_params=_params(("parallel", "parallel")))
    n_out = len(out_specs)

    def wrapped(*refs):
        ins, outs = refs[:n_in], refs[n_in + 1 : n_in + 1 + n_out]
        done = _run_side(side, (refs[n_in], refs[n_in + 1 + n_out], *refs[n_in + 2 + n_out :]), grid[0], grid[1])
        body(*ins, *outs)
        done()

    call = pl.pallas_call(wrapped, name=name, grid=grid, in_specs=in_specs + [ANY_SPEC], out_specs=out_specs + [ANY_SPEC],
                          out_shape=out_shape + [_side_out_shape(side)], scratch_shapes=COMM_SEMS,
                          compiler_params=_params(("arbitrary", "arbitrary")))
    return lambda *args: call(*args, side[1])


def _mla_fwd(qf, qpe, kvf, kpe, nh, name, side=None):
    t = qf.shape[0]
    tq, nq = _mla_tiles(t)
    scale = (NOPE_DIM + ROPE_DIM) ** -0.5

    def body(qn_ref, qp_ref, kn_ref, kp_ref, v_ref, o_ref, lse_ref):
        i = pl.program_id(1)
        qn = qn_ref[...].astype(BF16)
        qp = qp_ref[...]

        def step(j, carry, masked):
            m, l, acc = carry
            ks = pl.ds(pl.multiple_of(j * tq, tq), tq)
            s = (_dot(qn, kn_ref[ks, :], NT) + _dot(qp, kp_ref[ks, :], NT)) * scale
            if masked:
                row = lax.broadcasted_iota(jnp.int32, (tq, tq), 0)
                col = lax.broadcasted_iota(jnp.int32, (tq, tq), 1)
                s = jnp.where(col <= row, s, NEG)
            m_new = jnp.maximum(m, jnp.max(s, -1, keepdims=True))
            p = jnp.exp(s - m_new)
            a = jnp.exp(m - m_new)
            return m_new, a * l + jnp.sum(p, -1, keepdims=True), a * acc + _dot(p.astype(BF16), v_ref[ks, :], NN)

        init = (jnp.full((tq, 1), NEG, F32), jnp.zeros((tq, 1), F32), jnp.zeros((tq, V_DIM), F32))
        carry = lax.fori_loop(0, i, lambda j, c: step(j, c, False), init)
        m, l, acc = step(i, carry, True)
        o_ref[...] = (acc / l).astype(o_ref.dtype)
        lse_ref[...] = jnp.broadcast_to(m + jnp.log(l), (tq, LANE))

    blk = lambda h, i: (i, h)
    return _call_with_side(
        body, side, 5, (nh, nq),
        name=name,
        in_specs=[
            pl.BlockSpec((tq, LANE), blk),
            pl.BlockSpec((tq, LANE), blk),
            pl.BlockSpec((t, LANE), lambda h, i: (0, h)),
            pl.BlockSpec((t, LANE), lambda h, i: (0, 0)),
            pl.BlockSpec((t, LANE), lambda h, i: (0, nh + h)),
        ],
        out_specs=[pl.BlockSpec((tq, LANE), blk), pl.BlockSpec((tq, LANE), blk)],
        out_shape=[jax.ShapeDtypeStruct((t, nh * LANE), BF16), jax.ShapeDtypeStruct((t, nh * LANE), F32)],
    )(qf, qpe, kvf, kpe, kvf)


def _mla_bwd_q(qf, qpe, kvf, kpe, do, o, lse, nh, name, side=None):
    t = qf.shape[0]
    tq, nq = _mla_tiles(t)
    scale = (NOPE_DIM + ROPE_DIM) ** -0.5

    def body(qn_ref, qp_ref, kn_ref, kp_ref, v_ref, do_ref, o_ref, lse_ref, dqn_ref, dqp_ref, dl_ref):
        i = pl.program_id(1)
        qn = qn_ref[...].astype(BF16)
        qp = qp_ref[...]
        dof = do_ref[...].astype(F32)
        dob = dof.astype(BF16)
        delta = jnp.sum(dof * o_ref[...].astype(F32), -1, keepdims=True)
        lse1 = lse_ref[:, :1]

        def step(j, carry, masked):
            dqn, dqp = carry
            ks = pl.ds(pl.multiple_of(j * tq, tq), tq)
            kn, kp = kn_ref[ks, :], kp_ref[ks, :]
            s = (_dot(qn, kn, NT) + _dot(qp, kp, NT)) * scale
            p = jnp.exp(s - lse1)
            if masked:
                row = lax.broadcasted_iota(jnp.int32, (tq, tq), 0)
                col = lax.broadcasted_iota(jnp.int32, (tq, tq), 1)
                p = jnp.where(col <= row, p, 0.0)
            dp = _dot(dob, v_ref[ks, :], NT)
            ds = (p * (dp - delta) * scale).astype(BF16)
            return dqn + _dot(ds, kn, NN), dqp + _dot(ds, kp, NN)

        init = (jnp.zeros((tq, LANE), F32), jnp.zeros((tq, LANE), F32))
        carry = lax.fori_loop(0, i, lambda j, c: step(j, c, False), init)
        dqn, dqp = step(i, carry, True)
        dqn_ref[...] = dqn
        dqp_ref[...] = dqp
        dl_ref[...] = jnp.broadcast_to(delta, (tq, LANE))

    blk = lambda h, i: (i, h)
    bs = pl.BlockSpec((tq, LANE), blk)
    return _call_with_side(
        body, side, 8, (nh, nq),
        name=name,
        in_specs=[
            bs,
            bs,
            pl.BlockSpec((t, LANE), lambda h, i: (0, h)),
            pl.BlockSpec((t, LANE), lambda h, i: (0, 0)),
            pl.BlockSpec((t, LANE), lambda h, i: (0, nh + h)),
            bs,
            bs,
            bs,
        ],
        out_specs=[bs, bs, bs],
        out_shape=[jax.ShapeDtypeStruct((t, nh * LANE), F32)] * 3,
    )(qf, qpe, kvf, kpe, kvf, do, o, lse)


def _mla_bwd_kv(qn16, qpe, kvf, kpe, do16, lse_row, delta_row, nh, name, side=None):
    t = qn16.shape[0]
    tq, nq = _mla_tiles(t)
    scale = (NOPE_DIM + ROPE_DIM) ** -0.5

    def body(kn_ref, kp_ref, v_ref, qn_ref, qp_ref, do_ref, lse_ref, dl_ref, dkn_ref, dkp_ref, dv_ref):
        j = pl.program_id(1)
        kn, kp, v = kn_ref[...], kp_ref[...], v_ref[...]

        def step(i, carry, masked):
            dkn, dkp, dv = carry
            qs = pl.ds(pl.multiple_of(i * tq, tq), tq)
            qn, qp, dob = qn_ref[qs, :], qp_ref[qs, :], do_ref[qs, :]
            st = (_dot(kn, qn, NT) + _dot(kp, qp, NT)) * scale
            pt = jnp.exp(st - lse_ref[0, :, qs])
            if masked:
                key = lax.broadcasted_iota(jnp.int32, (tq, tq), 0)
                qry = lax.broadcasted_iota(jnp.int32, (tq, tq), 1)
                pt = jnp.where(key <= qry, pt, 0.0)
            dv = dv + _dot(pt.astype(BF16), dob, NN)
            dpt = _dot(v, dob, NT)
            dst = (pt * (dpt - dl_ref[0, :, qs]) * scale).astype(BF16)
            return dkn + _dot(dst, qn, NN), dkp + _dot(dst, qp, NN), dv

        z = jnp.zeros((tq, LANE), F32)
        carry = step(j, (z, z, z), True)
        dkn, dkp, dv = lax.fori_loop(j + 1, nq, lambda i, c: step(i, c, False), carry)
        dkn_ref[...] = dkn
        dkp_ref[...] = dkp
        dv_ref[...] = dv

    blk = pl.BlockSpec((tq, LANE), lambda h, j: (j, h))
    res = lambda f: pl.BlockSpec((t, LANE), f)
    row = pl.BlockSpec((1, 1, t), lambda h, j: (h, 0, 0))
    return _call_with_side(
        body, side, 8, (nh, nq),
        name=name,
        in_specs=[
            blk,
            pl.BlockSpec((tq, LANE), lambda h, j: (j, 0)),
            pl.BlockSpec((tq, LANE), lambda h, j: (j, nh + h)),
            res(lambda h, j: (0, h)),
            res(lambda h, j: (0, h)),
            res(lambda h, j: (0, h)),
            row,
            row,
        ],
        out_specs=[blk, blk, blk],
        out_shape=[jax.ShapeDtypeStruct((t, nh * LANE), F32)] * 3,
    )(kvf, kpe, kvf, qn16, qpe, do16, lse_row, delta_row)


def _alibi_slopes(n):
    return [float(2.0 ** (-8.0 * i / n)) for i in range(1, n + 1)]


def _band_masks(has_prev):
    qi = lax.broadcasted_iota(jnp.int32, (BLK, BLK), 0)
    ki = lax.broadcasted_iota(jnp.int32, (BLK, BLK), 1)
    return ki <= qi, (ki >= qi) & has_prev, (qi - ki).astype(F32), (qi - ki + BLK).astype(F32)


def _band_fwd(zd, nh, dil, name):
    t = zd.shape[0]
    nbc = t // BLK // dil
    scale = DIL_HEAD_DIM ** -0.5
    slopes = _alibi_slopes(nh)
    dw = nh * LANE

    def body(q_ref, kc_ref, kp_ref, vc_ref, vp_ref, o_ref, l_ref):
        mask_c, mask_p, dist_c, dist_p = _band_masks(pl.program_id(1) > 0)
        for h in range(nh):
            sl = slice(h * LANE, (h + 1) * LANE)
            q = q_ref[:, sl]
            bias = slopes[h] * dil
            sc = jnp.where(mask_c, _dot(q, kc_ref[:, sl], NT) * scale - bias * dist_c, NEG)
            sp = jnp.where(mask_p, _dot(q, kp_ref[:, sl], NT) * scale - bias * dist_p, NEG)
            m = jnp.maximum(jnp.max(sc, -1, keepdims=True), jnp.max(sp, -1, keepdims=True))
            ec, ep = jnp.exp(sc - m), jnp.exp(sp - m)
            l = jnp.sum(ec, -1, keepdims=True) + jnp.sum(ep, -1, keepdims=True)
            inv = 1.0 / l
            o = _dot((ec * inv).astype(BF16), vc_ref[:, sl], NN) + _dot((ep * inv).astype(BF16), vp_ref[:, sl], NN)
            o_ref[:, sl] = o
            l_ref[:, sl] = jnp.broadcast_to(m + jnp.log(l), (BLK, LANE))

    prev = lambda i: jnp.maximum(i - 1, 0)
    spec = lambda f: pl.BlockSpec((BLK, dw), f)
    return pl.pallas_call(
        body,
        name=name,
        grid=(dil, nbc),
        in_specs=[
            spec(lambda r, i: (r * nbc + i, 0)),
            spec(lambda r, i: (r * nbc + i, 1)),
            spec(lambda r, i: (r * nbc + prev(i), 1)),
            spec(lambda r, i: (r * nbc + i, 2)),
            spec(lambda r, i: (r * nbc + prev(i), 2)),
        ],
        out_specs=[spec(lambda r, i: (r * nbc + i, 0))] * 2,
        out_shape=[jax.ShapeDtypeStruct((t, dw), F32)] * 2,
        compiler_params=_params(("parallel", "parallel")),
    )(zd, zd, zd, zd, zd)


def _band_bwd(zd, o, lse, do, dl, nh, dil, name):
    t = zd.shape[0]
    nbc = t // BLK // dil
    scale = DIL_HEAD_DIM ** -0.5
    slopes = _alibi_slopes(nh)
    dw = nh * LANE

    def body(q_ref, k_ref, v_ref, kp_ref, vp_ref, qn_ref, o_ref, l_ref, do_ref, dl_ref, on_ref, ln_ref, don_ref, dln_ref,
             dq_ref, dk_ref, dv_ref):
        has_prev = pl.program_id(1) > 0
        has_next = pl.program_id(1) + 1 < nbc
        mask_c, mask_p, dist_c, dist_p = _band_masks(has_prev)
        _, mask_n, _, _ = _band_masks(has_next)
        for h in range(nh):
            sl = slice(h * LANE, (h + 1) * LANE)
            bias = slopes[h] * dil
            q, k, v = q_ref[:, sl], k_ref[:, sl], v_ref[:, sl]
            kp, vp, qn = kp_ref[:, sl], vp_ref[:, sl], qn_ref[:, sl]
            dof, donf = do_ref[:, sl], don_ref[:, sl]
            dob, donb = dof.astype(BF16), donf.astype(BF16)
            lse1, lsen1 = l_ref[:, sl][:, :1], ln_ref[:, sl][:, :1]
            adj = jnp.sum(dl_ref[:, sl] - dof * o_ref[:, sl], -1, keepdims=True)
            adjn = jnp.sum(dln_ref[:, sl] - donf * on_ref[:, sl], -1, keepdims=True)
            pa = jnp.where(mask_c, jnp.exp(_dot(q, k, NT) * scale - bias * dist_c - lse1), 0.0)
            pb = jnp.where(mask_p, jnp.exp(_dot(q, kp, NT) * scale - bias * dist_p - lse1), 0.0)
            dsa = (pa * (_dot(dob, v, NT) + adj)).astype(BF16)
            dsb = (pb * (_dot(dob, vp, NT) + adj)).astype(BF16)
            dq_ref[:, sl] = (_dot(dsa, k, NN) + _dot(dsb, kp, NN)) * scale
            pc = jnp.where(mask_n, jnp.exp(_dot(qn, k, NT) * scale - bias * dist_p - lsen1), 0.0)
            dsc = (pc * (_dot(donb, v, NT) + adjn)).astype(BF16)
            dk_ref[:, sl] = (_dot(dsa, q, TN) + _dot(dsc, qn, TN)) * scale
            dv_ref[:, sl] = _dot(pa.astype(BF16), dob, TN) + _dot(pc.astype(BF16), donb, TN)

    prev = lambda i: jnp.maximum(i - 1, 0)
    nxt = lambda i: jnp.minimum(i + 1, nbc - 1)
    spec = lambda f: pl.BlockSpec((BLK, dw), f)
    cur, nx = spec(lambda r, i: (r * nbc + i, 0)), spec(lambda r, i: (r * nbc + nxt(i), 0))
    return pl.pallas_call(
        body,
        name=name,
        grid=(dil, nbc),
        in_specs=[cur, spec(lambda r, i: (r * nbc + i, 1)), spec(lambda r, i: (r * nbc + i, 2)),
                  spec(lambda r, i: (r * nbc + prev(i), 1)), spec(lambda r, i: (r * nbc + prev(i), 2)),
                  nx, cur, cur, cur, cur, nx, nx, nx, nx],
        out_specs=[cur] * 3,
        out_shape=[jax.ShapeDtypeStruct((t, dw), F32)] * 3,
        compiler_params=_params(("parallel", "parallel")),
    )(zd, zd, zd, zd, zd, zd, o, lse, do, dl, o, lse, do, dl)


def _s5_ops(a_re, a_im, ldt, bt_re, bt_im, c_re, c_im):
    L, g, p = S5_CHUNK, S5_GROUP, S5_STATE
    dt = jnp.exp(ldt)
    lr, li = a_re * dt, a_im * dt
    er = jnp.exp(lr)
    lam_re, lam_im = er * jnp.cos(li), er * jnp.sin(li)
    nr, ni = lam_re - 1.0, lam_im
    den = a_re * a_re + a_im * a_im
    fr, fi = (nr * a_re + ni * a_im) / den, (ni * a_re - nr * a_im) / den
    bb_re, bb_im = fr * bt_re - fi * bt_im, fr * bt_im + fi * bt_re

    def power(tau):
        mag = jnp.exp(tau * lr)
        return mag * jnp.cos(tau * li), mag * jnp.sin(tau * li)

    step = lax.broadcasted_iota(jnp.int32, (L, 1), 0).astype(F32)

    def outer(pr, pi, mr, mi):
        re = pr[:, None, :] * mr[None] - pi[:, None, :] * mi[None]
        im = pr[:, None, :] * mi[None] + pi[:, None, :] * mr[None]
        return re.reshape(L * g, p), im.reshape(L * g, p)

    half = float(L // 2)
    cp_re, cp_im = outer(*power(step - half), c_re, c_im)
    pb_re, pb_im = outer(*power(half - step), bb_re, bb_im)
    toep = _dot(cp_re, pb_re, NT, HI) - _dot(cp_im, pb_im, NT, HI)
    trow = lax.broadcasted_iota(jnp.int32, (L * g, L * g), 0) // g
    scol = lax.broadcasted_iota(jnp.int32, (L * g, L * g), 1) // g
    toep = jnp.where(trow >= scol, toep, 0.0)
    et_re, et_im = outer(*power(float(L - 1) - step), bb_re, bb_im)
    f_re, f_im = outer(*power(step + 1.0), c_re, c_im)
    big_re, big_im = power(jnp.full((1, 1), float(L), F32))
    return toep, et_re, et_im, f_re, f_im, big_re, big_im


def _s5_y(ops, u, sp_re, sp_im):
    toep, _, _, f_re, f_im, _, _ = ops
    return _dot(u, toep, NT, HI) + _dot(sp_re, f_re, NT, HI) - _dot(sp_im, f_im, NT, HI)


def _s5_group_specs(ng):
    vec = pl.BlockSpec((1, 1, S5_STATE), lambda g: (g, 0, 0))
    one = pl.BlockSpec((1, 1, 1), lambda g: (g, 0, 0))
    mat = pl.BlockSpec((1, S5_GROUP, S5_STATE), lambda g: (g, 0, 0))
    return [vec, vec, one, mat, mat, mat, mat]


def _s5_load(refs):
    return [r[0] for r in refs]


def _s5_local(prm, u, name):
    ng, n, w = u.shape
    p = S5_STATE

    def body(*refs):
        ops = _s5_ops(*_s5_load(refs[:7]))
        uu = refs[7][0]
        refs[8][0] = _dot(uu, ops[1], NN, HI)
        refs[9][0] = _dot(uu, ops[2], NN, HI)
        refs[10][0] = ops[5]
        refs[11][0] = ops[6]

    blk = lambda a, b: pl.BlockSpec((1, a, b), lambda g: (g, 0, 0))
    return pl.pallas_call(
        body,
        name=name,
        grid=(ng,),
        in_specs=_s5_group_specs(ng) + [blk(n, w)],
        out_specs=[blk(n, p), blk(n, p), blk(1, p), blk(1, p)],
        out_shape=[jax.ShapeDtypeStruct((ng, n, p), F32)] * 2 + [jax.ShapeDtypeStruct((ng, 1, p), F32)] * 2,
        compiler_params=_params(("parallel",)),
    )(*prm, u)


def _s5_carry(e_re, e_im, lam_re, lam_im, name):
    n, w = e_re.shape
    cw = _tile(w, (1024, 512, 256, 128))

    def body(er_ref, ei_ref, lr_ref, li_ref, sr_ref, si_ref):
        lr, li = lr_ref[...], li_ref[...]

        def step(k, carry):
            sr, si = carry
            row = pl.ds(k, 1)
            sr_ref[row, :] = sr
            si_ref[row, :] = si
            return lr * sr - li * si + er_ref[row, :], li * sr + lr * si + ei_ref[row, :]

        z = jnp.zeros((1, cw), F32)
        lax.fori_loop(0, n, step, (z, z))

    col = pl.BlockSpec((n, cw), lambda j: (0, j))
    one = pl.BlockSpec((1, cw), lambda j: (0, j))
    return pl.pallas_call(
        body,
        name=name,
        grid=(w // cw,),
        in_specs=[col, col, one, one],
        out_specs=[col, col],
        out_shape=[jax.ShapeDtypeStruct((n, w), F32)] * 2,
        compiler_params=_params(("parallel",)),
    )(e_re, e_im, lam_re, lam_im)


def _s5_carry_bwd(dsp_re, dsp_im, sp_re, sp_im, lam_re, lam_im, name):
    n, w = dsp_re.shape
    cw = _tile(w, (1024, 512, 256, 128))

    def body(dr_ref, di_ref, sr_ref, si_ref, lr_ref, li_ref, gr_ref, gi_ref, dlr_ref, dli_ref):
        lr, li = lr_ref[...], li_ref[...]

        def step(q, carry):
            gr_next, gi_next, dr_next, di_next, alr, ali = carry
            k = n - 1 - q
            row = pl.ds(k, 1)
            gr = dr_next + lr * gr_next + li * gi_next
            gi = di_next - li * gr_next + lr * gi_next
            gr_ref[row, :] = gr
            gi_ref[row, :] = gi
            sr, si = sr_ref[row, :], si_ref[row, :]
            return gr, gi, dr_ref[row, :], di_ref[row, :], alr + gr * sr + gi * si, ali + gi * sr - gr * si

        z = jnp.zeros((1, cw), F32)
        out = lax.fori_loop(0, n, step, (z, z, z, z, z, z))
        dlr_ref[...] = out[4]
        dli_ref[...] = out[5]

    col = pl.BlockSpec((n, cw), lambda j: (0, j))
    one = pl.BlockSpec((1, cw), lambda j: (0, j))
    return pl.pallas_call(
        body,
        name=name,
        grid=(w // cw,),
        in_specs=[col, col, col, col, one, one],
        out_specs=[col, col, one, one],
        out_shape=[jax.ShapeDtypeStruct((n, w), F32)] * 2 + [jax.ShapeDtypeStruct((1, w), F32)] * 2,
        compiler_params=_params(("parallel",)),
    )(dsp_re, dsp_im, sp_re, sp_im, lam_re, lam_im)


def _s5_out(prm, u, sp_re, sp_im, name):
    ng, n, w = u.shape
    p = S5_STATE

    def body(*refs):
        ops = _s5_ops(*_s5_load(refs[:7]))
        refs[10][0] = _s5_y(ops, refs[7][0], refs[8][0], refs[9][0])

    blk = lambda a, b: pl.BlockSpec((1, a, b), lambda g: (g, 0, 0))
    return pl.pallas_call(
        body,
        name=name,
        grid=(ng,),
        in_specs=_s5_group_specs(ng) + [blk(n, w), blk(n, p), blk(n, p)],
        out_specs=blk(n, w),
        out_shape=jax.ShapeDtypeStruct((ng, n, w), F32),
        compiler_params=_params(("parallel",)),
    )(*prm, u, sp_re, sp_im)


def _s5_bwd_state(prm, dy, name):
    ng, n, w = dy.shape
    p = S5_STATE

    def body(*refs):
        ops = _s5_ops(*_s5_load(refs[:7]))
        d = refs[7][0]
        refs[8][0] = _dot(d, ops[3], NN, HI)
        refs[9][0] = -_dot(d, ops[4], NN, HI)

    blk = lambda a, b: pl.BlockSpec((1, a, b), lambda g: (g, 0, 0))
    return pl.pallas_call(
        body,
        name=name,
        grid=(ng,),
        in_specs=_s5_group_specs(ng) + [blk(n, w)],
        out_specs=[blk(n, p), blk(n, p)],
        out_shape=[jax.ShapeDtypeStruct((ng, n, p), F32)] * 2,
        compiler_params=_params(("parallel",)),
    )(*prm, dy)


def _s5_bwd_main(prm, u, sp_re, sp_im, dy, g_re, g_im, dlam_re, dlam_im, name):
    ng, n, w = u.shape
    p = S5_STATE

    def body(*refs):
        prm_v = _s5_load(refs[:7])
        uu, sr, si, d, gr, gi, dlr, dli = [r[0] for r in refs[7:15]]

        def phi(*args):
            ops = _s5_ops(*args[:7])
            y = _s5_y(ops, args[7], sr, si)
            e_re, e_im = _dot(args[7], ops[1], NN, HI), _dot(args[7], ops[2], NN, HI)
            return (jnp.sum(d * y) + jnp.sum(gr * e_re) + jnp.sum(gi * e_im)
                    + jnp.sum(dlr * ops[5]) + jnp.sum(dli * ops[6]))

        grads = jax.grad(phi, argnums=tuple(range(8)))(*prm_v, uu)
        for q in range(8):
            refs[15 + q][0] = grads[q]

    blk = lambda a, b: pl.BlockSpec((1, a, b), lambda g: (g, 0, 0))
    prm_specs = _s5_group_specs(ng)
    return pl.pallas_call(
        body,
        name=name,
        grid=(ng,),
        in_specs=prm_specs + [blk(n, w), blk(n, p), blk(n, p), blk(n, w), blk(n, p), blk(n, p), blk(1, p), blk(1, p)],
        out_specs=prm_specs + [blk(n, w)],
        out_shape=[jax.ShapeDtypeStruct(a.shape, F32) for a in prm] + [jax.ShapeDtypeStruct((ng, n, w), F32)],
        compiler_params=_params(("parallel",)),
    )(*prm, u, sp_re, sp_im, dy, g_re, g_im, dlam_re, dlam_im)


SLAB_GROUPS = LANE // S5_GROUP
CHUNK_TILES = S5_CHUNK * S5_GROUP // LANE


def _lane_segment(b):
    lane = lax.broadcasted_iota(jnp.int32, (1, LANE), 1)
    return jnp.logical_and(lane >= S5_GROUP * b, lane < S5_GROUP * (b + 1))


def _to_groups(h, name):
    t, d = h.shape
    n = t // S5_CHUNK

    def body(x_ref, o_ref):
        for a in range(CHUNK_TILES):
            rows = [x_ref[pl.ds(SLAB_GROUPS * a + b, n, stride=S5_CHUNK), :] for b in range(SLAB_GROUPS)]
            for g in range(SLAB_GROUPS):
                acc = jnp.zeros((n, LANE), F32)
                for b in range(SLAB_GROUPS):
                    shift = (S5_GROUP * (b - g)) % LANE
                    piece = pltpu.roll(rows[b], shift, 1) if shift else rows[b]
                    acc = jnp.where(_lane_segment(b), piece, acc)
                o_ref[g, :, a * LANE : (a + 1) * LANE] = acc

    return pl.pallas_call(
        body,
        name=name,
        grid=(d // LANE,),
        in_specs=[pl.BlockSpec((t, LANE), lambda q: (0, q))],
        out_specs=pl.BlockSpec((SLAB_GROUPS, n, S5_CHUNK * S5_GROUP), lambda q: (q, 0, 0)),
        out_shape=jax.ShapeDtypeStruct((d // S5_GROUP, n, S5_CHUNK * S5_GROUP), F32),
        compiler_params=_params(("parallel",)),
    )(h)


def _from_groups(y, name):
    ng, n, w = y.shape
    t, d = n * S5_CHUNK, ng * S5_GROUP

    def body(y_ref, x_ref):
        for a in range(CHUNK_TILES):
            tiles = [y_ref[g, :, a * LANE : (a + 1) * LANE] for g in range(SLAB_GROUPS)]
            for b in range(SLAB_GROUPS):
                acc = jnp.zeros((n, LANE), F32)
                for g in range(SLAB_GROUPS):
                    shift = (S5_GROUP * (g - b)) % LANE
                    piece = pltpu.roll(tiles[g], shift, 1) if shift else tiles[g]
                    acc = jnp.where(_lane_segment(g), piece, acc)
                x_ref[pl.ds(SLAB_GROUPS * a + b, n, stride=S5_CHUNK), :] = acc

    return pl.pallas_call(
        body,
        name=name,
        grid=(d // LANE,),
        in_specs=[pl.BlockSpec((SLAB_GROUPS, n, w), lambda q: (q, 0, 0))],
        out_specs=pl.BlockSpec((t, LANE), lambda q: (0, q)),
        out_shape=jax.ShapeDtypeStruct((t, d), F32),
        compiler_params=_params(("parallel",)),
    )(y)


def _states_to_cols(e):
    ng, n, p = e.shape
    return e.transpose(1, 0, 2).reshape(n, ng * p)


def _cols_to_states(s, ng):
    n = s.shape[0]
    return s.reshape(n, ng, S5_STATE).transpose(1, 0, 2)


def _me():
    return lax.axis_index("x"), lax.axis_index("y"), lax.axis_index("c")


def _flip(v, bit):
    return 1 - v if bit else v


COMM_SEMS = [pltpu.SemaphoreType.DMA((7,)), pltpu.SemaphoreType.DMA((7,)), pltpu.SemaphoreType.DMA]
ANY_SPEC = pl.BlockSpec(memory_space=pl.ANY)


def _gather_phases(x_ref, out_ref, send_sems, recv_sems, local_sem):
    mx, my, mc = _me()
    me, sibling = (mx, my, mc), (mx, my, 1 - mc)
    chips = [(1 - mx, my), (mx, 1 - my), (1 - mx, 1 - my)]

    def slot(px, py, pc):
        return out_ref.at[4 * px + 2 * py + pc]

    def copy(k, block, to, src=None):
        return pltpu.make_async_remote_copy(
            src_ref=slot(*block) if src is None else src,
            dst_ref=slot(*block),
            send_sem=send_sems.at[k],
            recv_sem=recv_sems.at[k],
            device_id=to,
            device_id_type=pl.DeviceIdType.MESH,
        )

    mine = pltpu.make_async_copy(x_ref, slot(*me), local_sem)
    first = [copy(0, me, sibling, src=x_ref)] + [copy(1 + j, me, (*chip, mc), src=x_ref) for j, chip in enumerate(chips)]
    passed = [copy(4 + j, (*chip, mc), sibling) for j, chip in enumerate(chips)]

    def start():
        mine.start()
        for cp in first:
            cp.start()

    def relay():
        for j, chip in enumerate(chips):
            copy(1 + j, (*chip, mc), me).wait_recv()
            passed[j].start()

    def finish():
        copy(0, sibling, me).wait_recv()
        for j, chip in enumerate(chips):
            copy(4 + j, (*chip, 1 - mc), me).wait_recv()
        for cp in first + passed:
            cp.wait_send()
        mine.wait()

    return start, relay, finish


def _exchange_phases(g_ref, out_ref, send_sems, recv_sems, local_sem):
    mx, my, mc = _me()
    mine_idx = 4 * mx + 2 * my + mc
    own = pltpu.make_async_copy(g_ref.at[mine_idx], out_ref.at[mine_idx], local_sem)
    copies = []
    for m in range(1, N_DEV):
        px, py, pc = _flip(mx, m & 4), _flip(my, m & 2), _flip(mc, m & 1)
        copies.append(
            pltpu.make_async_remote_copy(
                src_ref=g_ref.at[4 * px + 2 * py + pc],
                dst_ref=out_ref.at[mine_idx],
                send_sem=send_sems.at[m - 1],
                recv_sem=recv_sems.at[m - 1],
                device_id=(px, py, pc),
                device_id_type=pl.DeviceIdType.MESH,
            )
        )

    def start():
        own.start()
        for cp in copies:
            cp.start()

    def finish():
        for cp in copies:
            cp.wait_recv()
        for cp in copies:
            cp.wait_send()
        own.wait()

    return start, None, finish


def _side_out_shape(side):
    kind, arr = side
    return jax.ShapeDtypeStruct((N_DEV,) + arr.shape if kind == "gather" else arr.shape, arr.dtype)


def _run_side(side, refs, nsteps_major, nsteps_minor):
    kind = side[0]
    start, relay, finish = (_gather_phases if kind == "gather" else _exchange_phases)(*refs)
    a, b = pl.program_id(0), pl.program_id(1)
    pl.when(jnp.logical_and(a == 0, b == 0))(start)
    if relay is not None:
        pl.when(jnp.logical_and(a == nsteps_major - 1, b == 0))(relay)
    return lambda: pl.when(jnp.logical_and(a == nsteps_major - 1, b == nsteps_minor - 1))(finish)


def _all_gather(x, name):
    def body(*refs):
        start, relay, finish = _gather_phases(*refs)
        start()
        relay()
        finish()

    return pl.pallas_call(body, name=name, in_specs=[ANY_SPEC], out_specs=ANY_SPEC,
                          out_shape=_side_out_shape(("gather", x)), scratch_shapes=COMM_SEMS)(x)


def _all_to_all(g, name):
    def body(*refs):
        start, _, finish = _exchange_phases(*refs)
        start()
        finish()

    return pl.pallas_call(body, name=name, in_specs=[ANY_SPEC], out_specs=ANY_SPEC,
                          out_shape=_side_out_shape(("exchange", g)), scratch_shapes=COMM_SEMS)(g)


def _sum_slots(recv, name):
    _, r, c = recv.shape
    tr = _tile(r, (256, 128, 64, 32, 16, 8))

    def body(r_ref, o_ref):
        acc = r_ref[0].astype(F32)
        for k in range(1, N_DEV):
            acc = acc + r_ref[k].astype(F32)
        o_ref[...] = acc

    return pl.pallas_call(
        body,
        name=name,
        grid=(r // tr,),
        in_specs=[pl.BlockSpec((N_DEV, tr, c), lambda i: (0, i, 0))],
        out_specs=pl.BlockSpec((tr, c), lambda i: (i, 0)),
        out_shape=jax.ShapeDtypeStruct((r, c), F32),
        compiler_params=_params(("parallel",)),
    )(recv)


def _adamw(w, g, m, v, name):
    shape = w.shape
    c = shape[-1]
    as2d = lambda a: a.reshape(-1, c)
    w2, g2, m2, v2 = as2d(w), as2d(g), as2d(m), as2d(v)
    r = w2.shape[0]
    tr = _tile(r, (256, 128, 64, 32, 16, 8))
    c1 = 1.0 / (1.0 - ADAM_B1 ** ADAM_STEP)
    c2 = 1.0 / (1.0 - ADAM_B2 ** ADAM_STEP)

    def body(w_ref, g_ref, m_ref, v_ref, d_ref, mo_ref, vo_ref):
        gg = g_ref[...]
        mn = ADAM_B1 * m_ref[...] + (1.0 - ADAM_B1) * gg
        vn = ADAM_B2 * v_ref[...] + (1.0 - ADAM_B2) * (gg * gg)
        d_ref[...] = -ADAM_LR * ((mn * c1) / (jnp.sqrt(vn * c2) + ADAM_EPS) + ADAM_WD * w_ref[...])
        mo_ref[...] = mn
        vo_ref[...] = vn

    spec = pl.BlockSpec((tr, c), lambda i: (i, 0))
    d, mn, vn = pl.pallas_call(
        body,
        name=name,
        grid=(r // tr,),
        in_specs=[spec] * 4,
        out_specs=[spec] * 3,
        out_shape=[jax.ShapeDtypeStruct((r, c), F32)] * 3,
        compiler_params=_params(("parallel",)),
    )(w2, g2, m2, v2)
    return d.reshape(shape), mn.reshape(shape), vn.reshape(shape)


COL_SHARDED = ("attn_w_in", "mla_w_q_b", "mla_w_kv_b", "s5_d", "s5_w_glu", "ffn_w_in", "ple_w")
ROW_SHARDED = ("attn_w_out", "ffn_w_out", "ple_gate_w")
REPLICATED = ("mla_q_norm", "mla_kv_norm", "s5_a_re", "s5_a_im", "s5_log_dt", "s5_b_re", "s5_b_im", "s5_c_re",
              "s5_c_im", "ln1_g", "ln1_b", "ln2_g", "ln2_b")
ATTENTION_WEIGHTS = ("attn_w_in", "mla_w_q_b", "mla_w_kv_b", "attn_w_out")
BIG_WEIGHTS = ("attn_w_in", "mla_w_q_b", "mla_w_kv_b", "attn_w_out", "s5_w_glu", "ffn_w_in", "ffn_w_out", "ple_w", "ple_gate_w")
WEIGHTS = ("attn_w_in", "mla_q_norm", "mla_w_q_b", "mla_kv_norm", "mla_w_kv_b", "attn_w_out", "s5_a_re", "s5_a_im",
           "s5_log_dt", "s5_b_re", "s5_b_im", "s5_c_re", "s5_c_im", "s5_d", "s5_w_glu", "ln1_g", "ln1_b", "ffn_w_in",
           "ffn_w_out", "ple_w", "ple_gate_w", "ln2_g", "ln2_b")


def _unshard(name, gathered):
    if name in COL_SHARDED:
        full = jnp.moveaxis(gathered, 0, -2)
        return full.reshape(full.shape[:-2] + (full.shape[-2] * full.shape[-1],))
    full = jnp.moveaxis(gathered, 0, 1)
    return full.reshape((full.shape[0], full.shape[1] * full.shape[2]) + full.shape[3:])


def _by_owner(name, grad):
    if name in COL_SHARDED:
        g = grad.reshape(grad.shape[:-1] + (N_DEV, grad.shape[-1] // N_DEV))
        return jnp.moveaxis(g, -2, 0).reshape(N_DEV, -1)
    if name in ROW_SHARDED:
        g = grad.reshape((grad.shape[0], N_DEV, grad.shape[1] // N_DEV) + grad.shape[2:])
        return jnp.moveaxis(g, 1, 0).reshape(N_DEV, -1)
    return jnp.broadcast_to(grad.reshape(1, -1), (N_DEV, grad.size))


def _owner_rows(name, g):
    k, n = g.shape
    if name in COL_SHARDED:
        return g.reshape(k, N_DEV, n // N_DEV).transpose(1, 0, 2).reshape(N_DEV, -1)
    assert name in ROW_SHARDED, name
    return g.reshape(N_DEV, -1)


def _pack(pieces, axis, dtype, row_mult):
    n = sum(p.shape[axis] for p in pieces)
    pad = (-n) % (PACK_COLS * row_mult)
    if pad:
        shape = list(pieces[0].shape)
        shape[axis] = pad
        pieces = list(pieces) + [jnp.zeros(shape, dtype)]
    flat = jnp.concatenate(pieces, axis=axis)
    return flat.reshape(flat.shape[:axis] + ((n + pad) // PACK_COLS, PACK_COLS))


def _twice(fn):
    return lambda *a: fn(*a) * 2


def _ffn_block(h, hb, p_i, w_in, w_out, w_ple, w_pg, g, b, alpha, tag):
    t, d = h.shape
    hid = w_out.shape[0]
    cw = _tile(hid, (1408, 512, 256, 128))
    ncb = hid // cw
    gu = _mm(hb, w_in, name=f"ffn_in_{tag}")
    act = _rowwise(_swiglu_fn, [(gu, cw, 0), (gu, cw, ncb)], [], [(hid, cw, BF16)], name=f"swiglu_{tag}", tq=512, ncol=ncb)[0]
    f = _mm(act, w_out, name=f"ffn_out_{tag}")
    pw = _mm(p_i, w_ple, name=f"ple_{tag}")
    gate = _mm(hb, w_pg, name=f"ple_gate_{tag}")
    out, outb = _rowwise(_twice(_ln_ffn_fn(alpha)), [_whole(h), _whole(f), _whole(pw), _whole(gate)], [g, b],
                         [(d, d, F32), (d, d, BF16)], name=f"ln2_{tag}", tq=256)
    return out, outb, (h, hb, p_i, gu, act, f, pw, gate)


def _ffn_block_bwd(saved, dout, w_in, w_out, w_pg, g, b, alpha, tag):
    h, hb, p_i, gu, act, f, pw, gate = saved
    t, d = h.shape
    hid = w_out.shape[0]
    cw = _tile(hid, (1408, 512, 256, 128))
    ncb = hid // cw
    dh_a, df, dpw, dgate, dg, db = _rowwise_bwd(
        _ln_ffn_fn(alpha), [_whole(h), _whole(f), _whole(pw), _whole(gate)], [g, b], [dout],
        need=[True] * 4, drow=[(d, F32), (d, BF16), (d, BF16), (d, BF16)], name=f"ln2_bwd_{tag}", tq=128)
    dw_out = _mm(act, df, ta=True, out_dtype=BF16, name=f"ffn_out_dw_{tag}")
    dact = _mm(df, w_out, tb=True, out_dtype=BF16, name=f"ffn_out_dx_{tag}")
    dg_, du_ = _rowwise_bwd(_swiglu_fn, [(gu, cw, 0), (gu, cw, ncb)], [], [[(dact, cw, 0)]], need=[True, True],
                            drow=[(hid, BF16), (hid, BF16)], name=f"swiglu_bwd_{tag}", tq=256, ncol=ncb)
    dgu = jnp.concatenate([dg_, du_], axis=1)
    dw_in = _mm(hb, dgu, ta=True, out_dtype=BF16, name=f"ffn_in_dw_{tag}")
    dh_b = _mm(dgu, w_in, tb=True, name=f"ffn_in_dx_{tag}")
    dw_ple = _mm(p_i, dpw, ta=True, out_dtype=BF16, name=f"ple_dw_{tag}")
    dw_pg = _mm(hb, dgate, ta=True, out_dtype=BF16, name=f"ple_gate_dw_{tag}")
    dh_c = _mm(dgate, w_pg, tb=True, name=f"ple_gate_dx_{tag}")
    return [dh_a, dh_b, dh_c], dict(ffn_w_in=dw_in, ffn_w_out=dw_out, ple_w=dw_ple, ple_gate_w=dw_pg, ln2_g=dg, ln2_b=db)


def kernel(x, p, positions, attn_w_in, mla_q_norm, mla_w_q_b, mla_kv_norm, mla_w_kv_b, attn_w_out, s5_a_re, s5_a_im, s5_log_dt, s5_b_re, s5_b_im, s5_c_re, s5_c_im, s5_d, s5_w_glu, ln1_g, ln1_b, ffn_w_in, ffn_w_out, ple_w, ple_gate_w, ln2_g, ln2_b, loss_target, m_attn_w_in, m_mla_q_norm, m_mla_w_q_b, m_mla_kv_norm, m_mla_w_kv_b, m_attn_w_out, m_s5_a_re, m_s5_a_im, m_s5_log_dt, m_s5_b_re, m_s5_b_im, m_s5_c_re, m_s5_c_im, m_s5_d, m_s5_w_glu, m_ln1_g, m_ln1_b, m_ffn_w_in, m_ffn_w_out, m_ple_w, m_ple_gate_w, m_ln2_g, m_ln2_b, v_attn_w_in, v_mla_q_norm, v_mla_w_q_b, v_mla_kv_norm, v_mla_w_kv_b, v_attn_w_out, v_s5_a_re, v_s5_a_im, v_s5_log_dt, v_s5_b_re, v_s5_b_im, v_s5_c_re, v_s5_c_im, v_s5_d, v_s5_w_glu, v_ln1_g, v_ln1_b, v_ffn_w_in, v_ffn_w_out, v_ple_w, v_ple_gate_w, v_ln2_g, v_ln2_b):
    local = dict(attn_w_in=attn_w_in, mla_q_norm=mla_q_norm, mla_w_q_b=mla_w_q_b, mla_kv_norm=mla_kv_norm,
                 mla_w_kv_b=mla_w_kv_b, attn_w_out=attn_w_out, s5_a_re=s5_a_re, s5_a_im=s5_a_im, s5_log_dt=s5_log_dt,
                 s5_b_re=s5_b_re, s5_b_im=s5_b_im, s5_c_re=s5_c_re, s5_c_im=s5_c_im, s5_d=s5_d, s5_w_glu=s5_w_glu,
                 ln1_g=ln1_g, ln1_b=ln1_b, ffn_w_in=ffn_w_in, ffn_w_out=ffn_w_out, ple_w=ple_w, ple_gate_w=ple_gate_w,
                 ln2_g=ln2_g, ln2_b=ln2_b)
    mom_m = dict(zip(WEIGHTS, (m_attn_w_in, m_mla_q_norm, m_mla_w_q_b, m_mla_kv_norm, m_mla_w_kv_b, m_attn_w_out, m_s5_a_re, m_s5_a_im, m_s5_log_dt, m_s5_b_re, m_s5_b_im, m_s5_c_re, m_s5_c_im, m_s5_d, m_s5_w_glu, m_ln1_g, m_ln1_b, m_ffn_w_in, m_ffn_w_out, m_ple_w, m_ple_gate_w, m_ln2_g, m_ln2_b)))
    mom_v = dict(zip(WEIGHTS, (v_attn_w_in, v_mla_q_norm, v_mla_w_q_b, v_mla_kv_norm, v_mla_w_kv_b, v_attn_w_out, v_s5_a_re, v_s5_a_im, v_s5_log_dt, v_s5_b_re, v_s5_b_im, v_s5_c_re, v_s5_c_im, v_s5_d, v_s5_w_glu, v_ln1_g, v_ln1_b, v_ffn_w_in, v_ffn_w_out, v_ple_w, v_ple_gate_w, v_ln2_g, v_ln2_b)))

    t, d = x.shape[1], x.shape[2]
    depth = ln1_g.shape[0]
    alpha = (2.0 * depth) ** 0.25
    ql, kvl = mla_q_norm.shape[-1], mla_kv_norm.shape[-1]
    nh = mla_w_q_b.shape[-1] * N_DEV // (NOPE_DIM + ROPE_DIM)
    dw = (attn_w_in.shape[-1] * N_DEV - ql - kvl - ROPE_DIM) // 3
    ndh = dw // DIL_HEAD_DIM
    ng = d // S5_GROUP
    assert ql == kvl and (ql + kvl) % LANE == 0 and nh * V_DIM == dw

    sharded = [n for n in WEIGHTS if n in COL_SHARDED or n in ROW_SHARDED]
    def as_words(n):
        if n == "s5_d":
            return lax.bitcast_convert_type(local[n], BF16).reshape(-1)
        if n == "attn_w_in":
            return local[n][0].T.astype(BF16).reshape(-1)
        return local[n].astype(BF16).reshape(-1)

    full = {}

    def unpack_weights(names, gathered):
        gathered, off = gathered.reshape(N_DEV, -1), 0
        for n in names:
            size = local[n].size * (2 if n == "s5_d" else 1)
            piece = gathered[:, off : off + size]
            off += size
            if n == "attn_w_in":
                full[n] = piece.reshape(N_DEV * local[n].shape[2], d)
                continue
            if n == "s5_d":
                piece = lax.bitcast_convert_type(piece.reshape((N_DEV,) + local[n].shape + (2,)), F32)
            full[n] = _unshard(n, piece.reshape((N_DEV,) + local[n].shape))

    first_w = [n for n in sharded if n in ATTENTION_WEIGHTS]
    later_w = [n for n in sharded if n not in ATTENTION_WEIGHTS]
    unpack_weights(first_w, _all_gather(_pack([as_words(n) for n in first_w], 0, BF16, 16), "gather_attn_weights"))
    later_packed = _pack([as_words(n) for n in later_w], 0, BF16, 16)

    w_in_t = full["attn_w_in"]
    lat = ql + kvl
    w_lat_t = jnp.concatenate([w_in_t[: lat + ROPE_DIM], jnp.zeros((LANE - ROPE_DIM, d), BF16)], axis=0)
    w_dil_t = w_in_t[lat + ROPE_DIM :]
    wq = full["mla_w_q_b"][0].reshape(ql, nh, NOPE_DIM + ROPE_DIM)
    wq_pe = jnp.pad(wq[:, :, NOPE_DIM:], ((0, 0), (0, 0), (0, LANE - ROPE_DIM)))
    wqp = jnp.concatenate([wq[:, :, :NOPE_DIM].reshape(ql, nh * LANE), wq_pe.reshape(ql, nh * LANE)], axis=1)
    wkv = full["mla_w_kv_b"][0].reshape(kvl, nh, NOPE_DIM + V_DIM)
    wkvp = jnp.concatenate([wkv[:, :, :NOPE_DIM].reshape(kvl, nh * LANE), wkv[:, :, NOPE_DIM:].reshape(kvl, nh * LANE)], axis=1)
    w_out = full["attn_w_out"][0]

    h0 = x[0]
    h0b = h0.astype(BF16)
    pb = p.astype(BF16)
    target = loss_target[0]
    pos = positions.reshape(t, 1)
    grads = {}

    z = _mm(h0b, w_lat_t, tb=True, name="attn_in_lat")
    zd = _mm(h0b, w_dil_t, tb=True, out_dtype=BF16, name="attn_in_dil")
    gq, gk = mla_q_norm.reshape(1, ql), mla_kv_norm.reshape(1, kvl)
    qn, kvn = _rowwise(_rms_fn, [(z, ql, 0), (z, kvl, 1)], [gq, gk], [(ql, ql, BF16), (kvl, kvl, BF16)], name="rms", tq=256)
    qf = _mm(qn, wqp, name="q_up")
    kvf = _mm(kvn, wkvp, out_dtype=BF16, name="kv_up")
    cos, sin = _rope_tables(pos, "rope_tables")
    pe_cb = lat // LANE
    qpe, kpe = _rowwise(_rope_fwd_fn(nh), [(qf, nh * LANE, 1), (z, LANE, pe_cb), _whole(cos), _whole(sin)], [],
                        [(nh * LANE, nh * LANE, BF16), (LANE, LANE, BF16)], name="rope", tq=256)
    out_a, lse_a, later_gathered = _mla_fwd(qf, qpe, kvf, kpe, nh, "mla_fwd", side=("gather", later_packed))
    unpack_weights(later_w, later_gathered)
    w_glu = full["s5_w_glu"][0]
    d_skip = full["s5_d"]

    def to_classes(a, dil):
        if dil == 1:
            return a
        return a.reshape(t // dil, dil, a.shape[1]).transpose(1, 0, 2).reshape(t, a.shape[1])

    def from_classes(a, dil):
        if dil == 1:
            return a
        return a.reshape(dil, t // dil, a.shape[1]).transpose(1, 0, 2).reshape(t, a.shape[1])

    band = []
    for window, dil in DIL_BRANCHES:
        assert window // dil == BLK
        zc = to_classes(zd, dil)
        o_c, l_c = _band_fwd(zc, ndh, dil, f"band_fwd_d{dil}")
        band.append((dil, zc, o_c, l_c, from_classes(o_c, dil), from_classes(l_c, dil)))
    merge_rows = [_whole(b[4]) for b in band] + [_whole(b[5]) for b in band]
    out_b = _rowwise(_merge_fn, merge_rows, [], [(dw, dw, BF16)], name="merge", tq=256)[0]
    att = jnp.concatenate([out_a, out_b], axis=1)
    mix0 = _mm(att, w_out, name="attn_out")
    g1, b1 = ln1_g[0:1], ln1_b[0:1]
    h1, h1b = _rowwise(_twice(_ln_mix_fn(alpha)), [_whole(h0), _whole(mix0)], [g1, b1], [(d, d, F32), (d, d, BF16)],
                       name="ln1_l0", tq=256)
    h2, _, saved_f0 = _ffn_block(h1, h1b, pb[0, 0], full["ffn_w_in"][0], full["ffn_w_out"][0], full["ple_w"][0],
                                 full["ple_gate_w"][0], ln2_g[0:1], ln2_b[0:1], alpha, "l0")

    prm = [s5_a_re[0].reshape(ng, 1, S5_STATE), s5_a_im[0].reshape(ng, 1, S5_STATE), s5_log_dt[0].reshape(ng, 1, 1),
           s5_b_re[0].transpose(0, 2, 1), s5_b_im[0].transpose(0, 2, 1), s5_c_re[0], s5_c_im[0]]
    u = _to_groups(h2, "s5_regroup_u")
    e_re, e_im, lam_re, lam_im = _s5_local(prm, u, "s5_local")
    lam_re_c, lam_im_c = lam_re.reshape(1, ng * S5_STATE), lam_im.reshape(1, ng * S5_STATE)
    sp_re_c, sp_im_c = _s5_carry(_states_to_cols(e_re), _states_to_cols(e_im), lam_re_c, lam_im_c, "s5_carry")
    sp_re, sp_im = _cols_to_states(sp_re_c, ng), _cols_to_states(sp_im_c, ng)
    ys = _from_groups(_s5_out(prm, u, sp_re, sp_im, "s5_out"), "s5_ungroup_y")
    z5 = _rowwise(_gelu_fn, [_whole(ys), _whole(h2)], [d_skip], [(d, d, BF16)], name="gelu", tq=256)[0]
    vg = _mm(z5, w_glu, name="glu_in")
    g3, b3 = ln1_g[1:2], ln1_b[1:2]
    h3, h3b = _rowwise(_twice(_ln_glu_fn(alpha)), [_whole(h2), (vg, d, 0), (vg, d, 1)], [g3, b3],
                       [(d, d, F32), (d, d, BF16)], name="ln1_l1", tq=256)
    h4, _, saved_f1 = _ffn_block(h3, h3b, pb[1, 0], full["ffn_w_in"][1], full["ffn_w_out"][1], full["ple_w"][1],
                                 full["ple_gate_w"][1], ln2_g[1:2], ln2_b[1:2], alpha, "l1")

    dh4, loss_acc = _loss_kernel(h4, target, "loss")
    loss = lax.psum(loss_acc[0, 0], AXES)

    dh3, gf1 = _ffn_block_bwd(saved_f1, [_whole(dh4)], full["ffn_w_in"][1], full["ffn_w_out"][1],
                              full["ple_gate_w"][1], ln2_g[1:2], ln2_b[1:2], alpha, "l1")
    dh2_a, dval, dgate, dg3, db3 = _rowwise_bwd(
        _ln_glu_fn(alpha), [_whole(h2), (vg, d, 0), (vg, d, 1)], [g3, b3], [[_whole(a) for a in dh3]],
        need=[True] * 3, drow=[(d, F32), (d, BF16), (d, BF16)], name="ln1_bwd_l1", tq=128)
    dvg = jnp.concatenate([dval, dgate], axis=1)
    dw_glu = _mm(z5, dvg, ta=True, out_dtype=BF16, name="glu_dw")
    dz5 = _mm(dvg, w_glu, tb=True, name="glu_dx")
    dys, dh2_b, dd = _rowwise_bwd(_gelu_fn, [_whole(ys), _whole(h2)], [d_skip], [[_whole(dz5)]], need=[True, True],
                                  drow=[(d, F32), (d, F32)], name="gelu_bwd", tq=128)
    grads["s5_d"] = dd
    dy = _to_groups(dys, "s5_regroup_dy")
    dsp_re, dsp_im = _s5_bwd_state(prm, dy, "s5_bwd_state")
    g_re_c, g_im_c, dlam_re_c, dlam_im_c = _s5_carry_bwd(_states_to_cols(dsp_re), _states_to_cols(dsp_im), sp_re_c, sp_im_c,
                                                         lam_re_c, lam_im_c, "s5_carry_bwd")
    s5g = _s5_bwd_main(prm, u, sp_re, sp_im, dy, _cols_to_states(g_re_c, ng), _cols_to_states(g_im_c, ng),
                       dlam_re_c.reshape(ng, 1, S5_STATE), dlam_im_c.reshape(ng, 1, S5_STATE), "s5_bwd_main")
    grads["s5_a_re"] = s5g[0].reshape(s5_a_re.shape)
    grads["s5_a_im"] = s5g[1].reshape(s5_a_im.shape)
    grads["s5_log_dt"] = s5g[2].reshape(s5_log_dt.shape)
    grads["s5_b_re"] = s5g[3].transpose(0, 2, 1)[None]
    grads["s5_b_im"] = s5g[4].transpose(0, 2, 1)[None]
    grads["s5_c_re"] = s5g[5][None]
    grads["s5_c_im"] = s5g[6][None]
    dh2_c = _from_groups(s5g[7], "s5_ungroup_du")

    dh1, gf0 = _ffn_block_bwd(saved_f0, [_whole(dh2_a), _whole(dh2_b), _whole(dh2_c)], full["ffn_w_in"][0],
                              full["ffn_w_out"][0], full["ple_gate_w"][0], ln2_g[0:1], ln2_b[0:1], alpha, "l0")
    for k in ("ln2_g", "ln2_b"):
        grads[k] = jnp.concatenate([gf0[k], gf1[k]])

    shards = {}

    def send_buffer(items, dtype, row_mult):
        return _pack([rows.astype(dtype) for _, rows in items], 1, dtype, row_mult)

    def take(items, recv, tag):
        flat, off = _sum_slots(recv, f"sum_grads_{tag}").reshape(-1), 0
        for key, rows in items:
            shards[key] = flat[off : off + rows.shape[1]]
            off += rows.shape[1]

    ffn_names = ("ffn_w_in", "ffn_w_out", "ple_w", "ple_gate_w")
    items_l1 = [(("s5_w_glu", 0), _owner_rows("s5_w_glu", dw_glu))] + [((k, 1), _owner_rows(k, gf1[k])) for k in ffn_names]
    items_l0 = [((k, 0), _owner_rows(k, gf0[k])) for k in ffn_names]
    dh0_a, dmix, dg1, db1 = _rowwise_bwd(_ln_mix_fn(alpha), [_whole(h0), _whole(mix0)], [g1, b1], [[_whole(a) for a in dh1]],
                                         need=[True, True], drow=[(d, F32), (d, BF16)], name="ln1_bwd_l0", tq=128)
    grads["ln1_g"] = jnp.concatenate([dg1, dg3])
    grads["ln1_b"] = jnp.concatenate([db1, db3])
    dw_out = _mm(att, dmix, ta=True, out_dtype=BF16, name="attn_out_dw")
    datt = _mm(dmix, w_out, tb=True, name="attn_out_dx")

    dmerge = _rowwise_bwd(_merge_fn, merge_rows, [], [[(datt, dw, 1)]], need=[True] * 6, drow=[(dw, F32)] * 6,
                          name="merge_bwd", tq=128)
    dq_s = dk_s = dv_s = None
    for k, (dil, zc, o_c, l_c, _, _) in enumerate(band):
        do_c, dl_c = to_classes(dmerge[k], dil), to_classes(dmerge[3 + k], dil)
        dq_c, dk_c, dv_c = _band_bwd(zc, o_c, l_c, do_c, dl_c, ndh, dil, f"band_bwd_d{dil}")
        dq_n, dk_n, dv_n = from_classes(dq_c, dil), from_classes(dk_c, dil), from_classes(dv_c, dil)
        dq_s = dq_n if dq_s is None else dq_s + dq_n
        dk_s = dk_n if dk_s is None else dk_s + dk_n
        dv_s = dv_n if dv_s is None else dv_s + dv_n

    dqn, dqp, delta, recv_l1 = _mla_bwd_q(qf, qpe, kvf, kpe, datt, out_a, lse_a, nh, "mla_bwd_q",
                                          side=("exchange", send_buffer(items_l1, BF16, 16)))
    take(items_l1, recv_l1, "l1")
    to_row = lambda a: a[:, ::LANE].T.reshape(nh, 1, t)
    dkn, dkp, dv, recv_l0 = _mla_bwd_kv(qf[:, : nh * LANE].astype(BF16), qpe, kvf, kpe, datt[:, : nh * LANE].astype(BF16),
                                        to_row(lse_a), to_row(delta), nh, "mla_bwd_kv",
                                        side=("exchange", send_buffer(items_l0, BF16, 16)))
    take(items_l0, recv_l0, "l0")
    dq_pe, dk_pe = _rowwise(_rope_bwd_fn(nh), [_whole(dqp), _whole(dkp), _whole(cos), _whole(sin)], [],
                            [(nh * LANE, nh * LANE, BF16), (LANE, LANE, BF16)], name="rope_bwd", tq=256)
    dqf = jnp.concatenate([dqn.astype(BF16), dq_pe], axis=1)
    dkvf = jnp.concatenate([dkn, dv], axis=1).astype(BF16)
    dwqp = _mm(qn, dqf, ta=True, name="q_up_dw")
    dqn_in = _mm(dqf, wqp, tb=True, name="q_up_dx")
    dwkvp = _mm(kvn, dkvf, ta=True, name="kv_up_dw")
    dkvn_in = _mm(dkvf, wkvp, tb=True, name="kv_up_dx")
    dql, dkvl, dgq, dgk = _rowwise_bwd(_rms_fn, [(z, ql, 0), (z, kvl, 1)], [gq, gk], [[_whole(dqn_in)], [_whole(dkvn_in)]],
                                       need=[True, True], drow=[(ql, BF16), (kvl, BF16)], name="rms_bwd", tq=256)
    grads["mla_q_norm"], grads["mla_kv_norm"] = dgq, dgk
    dz_lat = jnp.concatenate([dql, dkvl, dk_pe], axis=1)
    dz_dil = jnp.concatenate([dq_s, dk_s, dv_s], axis=1).astype(BF16)
    dw_lat_t = _mm(dz_lat, h0b, ta=True, out_dtype=BF16, name="attn_in_lat_dw")
    dw_dil_t = _mm(dz_dil, h0b, ta=True, out_dtype=BF16, name="attn_in_dil_dw")
    dh0_b = _mm(dz_lat, w_lat_t, name="attn_in_lat_dx")
    dh0_c = _mm(dz_dil, w_dil_t, name="attn_in_dil_dx")
    grad_x = _addn([dh0_a, dh0_b, dh0_c], "grad_x")[None]
    dw_in_t = jnp.concatenate([dw_lat_t[: lat + ROPE_DIM], dw_dil_t], axis=0)
    dwq_n = dwqp[:, : nh * LANE].reshape(ql, nh, NOPE_DIM)
    dwq_r = dwqp[:, nh * LANE :].reshape(ql, nh, LANE)[:, :, :ROPE_DIM]
    dwq = jnp.concatenate([dwq_n, dwq_r], axis=2).reshape(ql, nh * (NOPE_DIM + ROPE_DIM))
    dwkv_k = dwkvp[:, : nh * LANE].reshape(kvl, nh, NOPE_DIM)
    dwkv_v = dwkvp[:, nh * LANE :].reshape(kvl, nh, V_DIM)
    dwkv = jnp.concatenate([dwkv_k, dwkv_v], axis=2).reshape(kvl, nh * (NOPE_DIM + V_DIM))

    items_att = [(("attn_w_in", 0), dw_in_t.reshape(N_DEV, -1)), (("mla_w_q_b", 0), _owner_rows("mla_w_q_b", dwq)),
                 (("mla_w_kv_b", 0), _owner_rows("mla_w_kv_b", dwkv)), (("attn_w_out", 0), _owner_rows("attn_w_out", dw_out))]
    take(items_att, _all_to_all(send_buffer(items_att, BF16, 16), "exchange_grads_attn"), "attn")
    items_small = [((n, None), _by_owner(n, grads[n])) for n in WEIGHTS if n not in BIG_WEIGHTS]
    take(items_small, _all_to_all(send_buffer(items_small, F32, 8), "exchange_grads_small"), "small")

    g_out, d_out, m_out, v_out = [], [], [], []
    for n in WEIGHTS:
        if n == "attn_w_in":
            g = shards[(n, 0)].reshape(local[n].shape[2], d).T[None]
        elif n in BIG_WEIGHTS:
            g = jnp.concatenate([shards[(n, layer)] for layer in range(local[n].shape[0])]).reshape(local[n].shape)
        else:
            g = shards[(n, None)].reshape(local[n].shape)
        dlt, mn, vn = _adamw(local[n], g, mom_m[n], mom_v[n], f"adamw_{n}")
        g_out.append(g)
        d_out.append(dlt)
        m_out.append(mn)
        v_out.append(vn)
    return (loss, grad_x, *g_out, *d_out, *m_out, *v_out)
```

```python
import functools
import math

import numpy as np
import jax
import jax.numpy as jnp
from jax import lax
from jax.experimental import pallas as pl
from jax.experimental.pallas import tpu as pltpu

F32 = jnp.float32
BF16 = jnp.bfloat16

NOPE_DIM, ROPE_DIM, V_DIM = 128, 64, 128
DIL_HEAD_DIM = 128
DIL_BRANCHES = ((128, 1), (512, 4), (2048, 16))
BLK = 128
LANE = 128
ROPE_THETA = 10000.0
S5_GROUP, S5_STATE = 16, 64
S5_CHUNK = 32
NEG = -1e30
ADAM_LR, ADAM_B1, ADAM_B2, ADAM_EPS, ADAM_WD, ADAM_STEP = 0.001, 0.9, 0.999, 1e-08, 0.01, 10
N_DEV = 8
AXES = ("x", "y", "c")
VMEM_LIMIT = 56 * 1024 * 1024
MM_VMEM_BUDGET = 40 * 1024 * 1024
PACK_COLS = 1024
PACK_ROW_TILE = 16
HI = lax.Precision.HIGH


def _tile(n, cands):
    for c in cands:
        if n % c == 0:
            return c
    return n


def _params(sem=None):
    return pltpu.CompilerParams(dimension_semantics=sem, vmem_limit_bytes=VMEM_LIMIT)


def _dot(a, b, dims, prec=None):
    return lax.dot_general(a, b, (dims, ((), ())), preferred_element_type=F32, precision=prec)


NN = ((1,), (0,))
NT = ((1,), (1,))
TN = ((0,), (0,))


def _mm(a, b, *, ta=False, tb=False, out_dtype=F32, name):
    (k, m) = a.shape if ta else a.shape[::-1]
    (n, k2) = b.shape if tb else b.shape[::-1]
    assert k == k2, (a.shape, b.shape, ta, tb)
    tm = _tile(m, (1024, 512, 256, 128))
    tn = _tile(n, (1024, 512, 384, 256, 128))
    sa, sb, so = a.dtype.itemsize, b.dtype.itemsize, jnp.dtype(out_dtype).itemsize

    def fits(tk):
        return 2 * (tm * tk * sa + tk * tn * sb) + 2 * tm * tn * so + 4 * tm * tn <= MM_VMEM_BUDGET

    tk = next((c for c in (2816, 2048, 1536, 1408, 1152, 1024, 768, 512, 384, 256, 128) if k % c == 0 and fits(c)), k)
    nk = k // tk
    dims = (((0,) if ta else (1,)), ((1,) if tb else (0,)))

    def body(a_ref, b_ref, o_ref, *acc):
        part = _dot(a_ref[...].astype(BF16), b_ref[...].astype(BF16), dims)
        if nk == 1:
            o_ref[...] = part.astype(o_ref.dtype)
            return
        (acc_ref,) = acc
        kk = pl.program_id(2)

        @pl.when(kk == 0)
        def _():
            acc_ref[...] = part

        @pl.when(kk > 0)
        def _():
            acc_ref[...] += part

        @pl.when(kk == nk - 1)
        def _():
            o_ref[...] = acc_ref[...].astype(o_ref.dtype)

    a_spec = pl.BlockSpec((tk, tm), lambda i, j, kk: (kk, i)) if ta else pl.BlockSpec((tm, tk), lambda i, j, kk: (i, kk))
    b_spec = pl.BlockSpec((tn, tk), lambda i, j, kk: (j, kk)) if tb else pl.BlockSpec((tk, tn), lambda i, j, kk: (kk, j))
    return pl.pallas_call(
        body,
        name=name,
        grid=(m // tm, n // tn, nk),
        in_specs=[a_spec, b_spec],
        out_specs=pl.BlockSpec((tm, tn), lambda i, j, kk: (i, j)),
        out_shape=jax.ShapeDtypeStruct((m, n), out_dtype),
        scratch_shapes=[pltpu.VMEM((tm, tn), F32)] if nk > 1 else [],
        compiler_params=_params(("parallel", "parallel", "arbitrary")),
    )(a, b)


def _row_spec(spec, tq):
    _, w, cb = spec
    return pl.BlockSpec((tq, w), lambda i, j: (i, cb + j))


def _full_spec(p):
    return pl.BlockSpec(p.shape, lambda i, j: (0,) * p.ndim)


def _rowwise(fn, rows, pars, outs, *, name, tq, ncol=1):
    t = rows[0][0].shape[0]
    tq = _tile(t, (tq, 128, 64, 32, 16, 8))
    nr, npar = len(rows), len(pars)

    def body(*refs):
        vals = [r[...].astype(F32) for r in refs[: nr + npar]]
        res = fn(*vals)
        for o, r in zip(refs[nr + npar :], res):
            o[...] = r.astype(o.dtype)

    res = pl.pallas_call(
        body,
        name=name,
        grid=(t // tq, ncol),
        in_specs=[_row_spec(s, tq) for s in rows] + [_full_spec(p) for p in pars],
        out_specs=[pl.BlockSpec((tq, w), lambda i, j: (i, j)) for (_, w, _) in outs],
        out_shape=[jax.ShapeDtypeStruct((t, wt), dt) for (wt, _, dt) in outs],
        compiler_params=_params(("parallel", "parallel")),
    )(*[s[0] for s in rows], *pars)
    return res


def _rowwise_bwd(fn, rows, pars, cots, *, need, drow, name, tq, ncol=1):
    t = rows[0][0].shape[0]
    tq = _tile(t, (tq, 128, 64, 32, 16, 8))
    nr, npar = len(rows), len(pars)
    assert ncol == 1 or npar == 0
    flat_cots = [s for c in cots for s in c]
    ncot = len(flat_cots)
    want = [k for k in range(nr) if need[k]]

    def body(*refs):
        vals = [r[...].astype(F32) for r in refs[: nr + npar]]
        cref = refs[nr + npar : nr + npar + ncot]
        oref = refs[nr + npar + ncot :]
        cvals, pos = [], 0
        for c in cots:
            acc = cref[pos][...].astype(F32)
            for q in range(1, len(c)):
                acc = acc + cref[pos + q][...].astype(F32)
            cvals.append(acc)
            pos += len(c)

        def closed(*diff):
            full = list(vals)
            for k, dv in zip(want + list(range(nr, nr + npar)), diff):
                full[k] = dv
            return tuple(fn(*full))

        diff_in = [vals[k] for k in want] + vals[nr:]
        _, vjp = jax.vjp(closed, *diff_in)
        grads = vjp(tuple(cvals))
        for q in range(len(want)):
            oref[q][...] = grads[q].astype(oref[q].dtype)
        if npar:
            first = pl.program_id(0) == 0

            @pl.when(first)
            def _():
                for q in range(npar):
                    oref[len(want) + q][...] = jnp.zeros_like(oref[len(want) + q])

            for q in range(npar):
                oref[len(want) + q][...] += grads[len(want) + q]

    out_specs = [pl.BlockSpec((tq, rows[k][1]), lambda i, j, cb=rows[k][2]: (i, j)) for k in want]
    out_specs += [_full_spec(p) for p in pars]
    out_shape = [jax.ShapeDtypeStruct((t, wt), dt) for (wt, dt) in drow]
    out_shape += [jax.ShapeDtypeStruct(p.shape, F32) for p in pars]
    return pl.pallas_call(
        body,
        name=name,
        grid=(t // tq, ncol),
        in_specs=[_row_spec(s, tq) for s in rows] + [_full_spec(p) for p in pars] + [_row_spec(s, tq) for s in flat_cots],
        out_specs=out_specs,
        out_shape=out_shape,
        compiler_params=_params(("arbitrary", "arbitrary") if npar else ("parallel", "parallel")),
    )(*[s[0] for s in rows], *pars, *[s[0] for s in flat_cots])


def _whole(a, w=None):
    return (a, a.shape[1] if w is None else w, 0)


def _rms_fn(ql, kvl, gq, gk):
    def one(x, g):
        return x * lax.rsqrt(jnp.mean(x * x, -1, keepdims=True) + 1e-6) * g

    return one(ql, gq), one(kvl, gk)


def _layernorm(s, g, b):
    mu = jnp.mean(s, -1, keepdims=True)
    d = s - mu
    var = jnp.mean(d * d, -1, keepdims=True)
    return d * lax.rsqrt(var + 1e-5) * g + b


def _ln_mix_fn(alpha):
    def fn(h, mix, g, b):
        return (_layernorm(alpha * h + mix, g, b),)

    return fn


def _ln_glu_fn(alpha):
    def fn(h, val, gate, g, b):
        return (_layernorm(alpha * h + val * jax.nn.sigmoid(gate), g, b),)

    return fn


def _ln_ffn_fn(alpha):
    def fn(h, f, pw, gate, g, b):
        return (_layernorm(alpha * h + f + pw * jax.nn.sigmoid(gate), g, b),)

    return fn


def _swiglu_fn(g, u):
    return (jax.nn.silu(g) * u,)


def _gelu_fn(ys, h, d):
    return (jax.nn.gelu(ys + d * h),)


def _merge_fn(o1, o2, o3, l1, l2, l3):
    m = jnp.maximum(jnp.maximum(l1, l2), l3)
    e1, e2, e3 = jnp.exp(l1 - m), jnp.exp(l2 - m), jnp.exp(l3 - m)
    return ((e1 * o1 + e2 * o2 + e3 * o3) / (e1 + e2 + e3),)


def _swap_halves(t):
    w = t.shape[1]
    lane = lax.broadcasted_iota(jnp.int32, t.shape, 1) % LANE
    up = jnp.where(lane < ROPE_DIM, pltpu.roll(t, ROPE_DIM // 2, 1), 0.0)
    return jnp.where(lane < ROPE_DIM // 2, pltpu.roll(t, w - ROPE_DIM // 2, 1), up)


def _rope_fwd_fn(nh):
    def fn(qpe, kpe, cos, sin):
        cq, sq = jnp.tile(cos, (1, nh)), jnp.tile(sin, (1, nh))
        return qpe * cq + _swap_halves(qpe) * sq, kpe * cos + _swap_halves(kpe) * sin

    return fn


def _rope_bwd_fn(nh):
    def fn(dq, dk_heads, cos, sin):
        cq, sq = jnp.tile(cos, (1, nh)), jnp.tile(sin, (1, nh))
        dk = dk_heads[:, :LANE]
        for h in range(1, nh):
            dk = dk + dk_heads[:, h * LANE : (h + 1) * LANE]
        return dq * cq + _swap_halves(dq * sq), dk * cos + _swap_halves(dk * sin)

    return fn


def _rope_tables(positions, name):
    t = positions.shape[0]
    tq = _tile(t, (512, 128, 8))
    half = ROPE_DIM // 2

    def body(p_ref, c_ref, s_ref):
        lane = lax.broadcasted_iota(jnp.int32, (tq, LANE), 1)
        idx = (lane % half).astype(F32)
        inv_freq = jnp.exp(idx * (-math.log(ROPE_THETA) / half))
        ang = p_ref[...].astype(F32) * inv_freq
        live = lane < ROPE_DIM
        c_ref[...] = jnp.where(live, jnp.cos(ang), 0.0)
        s_ref[...] = jnp.where(live, jnp.where(lane < half, -jnp.sin(ang), jnp.sin(ang)), 0.0)

    return pl.pallas_call(
        body,
        name=name,
        grid=(t // tq,),
        in_specs=[pl.BlockSpec((tq, 1), lambda i: (i, 0))],
        out_specs=[pl.BlockSpec((tq, LANE), lambda i: (i, 0))] * 2,
        out_shape=[jax.ShapeDtypeStruct((t, LANE), F32)] * 2,
        compiler_params=_params(("parallel",)),
    )(positions)


def _loss_kernel(y, target, name):
    t, d = y.shape
    tq = _tile(t, (256, 128, 8))

    def body(y_ref, t_ref, dy_ref, l_ref):
        @pl.when(pl.program_id(0) == 0)
        def _():
            l_ref[...] = jnp.zeros_like(l_ref)

        e = y_ref[...] - t_ref[...]
        dy_ref[...] = e * (1.0 / d)
        l_ref[...] += jnp.sum(e * e) * (0.5 / d)

    return pl.pallas_call(
        body,
        name=name,
        grid=(t // tq,),
        in_specs=[pl.BlockSpec((tq, d), lambda i: (i, 0))] * 2,
        out_specs=[pl.BlockSpec((tq, d), lambda i: (i, 0)), pl.BlockSpec((8, LANE), lambda i: (0, 0))],
        out_shape=[jax.ShapeDtypeStruct((t, d), F32), jax.ShapeDtypeStruct((8, LANE), F32)],
        compiler_params=_params(("arbitrary",)),
    )(y, target)


def _addn(arrs, name):
    w = arrs[0].shape[1]
    return _rowwise(lambda *v: (functools.reduce(lambda p, q: p + q, v),), [_whole(a) for a in arrs], [], [(w, w, F32)],
                    name=name, tq=256)[0]


def _mla_tiles(t):
    tq = _tile(t, (512, 256, 128))
    return tq, t // tq


def _call_with_side(body, side, n_in, grid, *, name, in_specs, out_specs, out_shape):
    if side is None:
        return pl.pallas_call(body, name=name, grid=grid, in_specs=in_specs, out_specs=out_specs, out_shape=out_shape,
                              compiler_params=_params(("parallel", "parallel")))
    n_out = len(out_specs)

    def wrapped(*refs):
        ins, outs = refs[:n_in], refs[n_in + 1 : n_in + 1 + n_out]
        done = _run_side(side, (refs[n_in], refs[n_in + 1 + n_out], *refs[n_in + 2 + n_out :]), grid[0], grid[1])
        body(*ins, *outs)
        done()

    call = pl.pallas_call(wrapped, name=name, grid=grid, in_specs=in_specs + [ANY_SPEC], out_specs=out_specs + [ANY_SPEC],
                          out_shape=out_shape + [_side_out_shape(side)], scratch_shapes=COMM_SEMS,
                          compiler_params=_params(("arbitrary", "arbitrary")))
    return lambda *args: call(*args, side[1])


def _mla_fwd(qf, qpe, kvf, kpe, nh, name, side=None):
    t = qf.shape[0]
    tq, nq = _mla_tiles(t)
    scale = (NOPE_DIM + ROPE_DIM) ** -0.5

    def body(qn_ref, qp_ref, kn_ref, kp_ref, v_ref, o_ref, lse_ref):
        i = pl.program_id(1)
        qn = qn_ref[...].astype(BF16)
        qp = qp_ref[...]

        def step(j, carry, masked):
            m, l, acc = carry
            ks = pl.ds(pl.multiple_of(j * tq, tq), tq)
            s = (_dot(qn, kn_ref[ks, :], NT) + _dot(qp, kp_ref[ks, :], NT)) * scale
            if masked:
                row = lax.broadcasted_iota(jnp.int32, (tq, tq), 0)
                col = lax.broadcasted_iota(jnp.int32, (tq, tq), 1)
                s = jnp.where(col <= row, s, NEG)
            m_new = jnp.maximum(m, jnp.max(s, -1, keepdims=True))
            p = jnp.exp(s - m_new)
            a = jnp.exp(m - m_new)
            return m_new, a * l + jnp.sum(p, -1, keepdims=True), a * acc + _dot(p.astype(BF16), v_ref[ks, :], NN)

        init = (jnp.full((tq, 1), NEG, F32), jnp.zeros((tq, 1), F32), jnp.zeros((tq, V_DIM), F32))
        carry = lax.fori_loop(0, i, lambda j, c: step(j, c, False), init)
        m, l, acc = step(i, carry, True)
        o_ref[...] = (acc / l).astype(o_ref.dtype)
        lse_ref[...] = jnp.broadcast_to(m + jnp.log(l), (tq, LANE))

    blk = lambda h, i: (i, h)
    return _call_with_side(
        body, side, 5, (nh, nq),
        name=name,
        in_specs=[
            pl.BlockSpec((tq, LANE), blk),
            pl.BlockSpec((tq, LANE), blk),
            pl.BlockSpec((t, LANE), lambda h, i: (0, h)),
            pl.BlockSpec((t, LANE), lambda h, i: (0, 0)),
            pl.BlockSpec((t, LANE), lambda h, i: (0, nh + h)),
        ],
        out_specs=[pl.BlockSpec((tq, LANE), blk), pl.BlockSpec((tq, LANE), blk)],
        out_shape=[jax.ShapeDtypeStruct((t, nh * LANE), BF16), jax.ShapeDtypeStruct((t, nh * LANE), F32)],
    )(qf, qpe, kvf, kpe, kvf)


def _mla_bwd_q(qf, qpe, kvf, kpe, do, o, lse, nh, name, side=None):
    t = qf.shape[0]
    tq, nq = _mla_tiles(t)
    scale = (NOPE_DIM + ROPE_DIM) ** -0.5

    def body(qn_ref, qp_ref, kn_ref, kp_ref, v_ref, do_ref, o_ref, lse_ref, dqn_ref, dqp_ref, dl_ref):
        i = pl.program_id(1)
        qn = qn_ref[...].astype(BF16)
        qp = qp_ref[...]
        dof = do_ref[...].astype(F32)
        dob = dof.astype(BF16)
        delta = jnp.sum(dof * o_ref[...].astype(F32), -1, keepdims=True)
        lse1 = lse_ref[:, :1]

        def step(j, carry, masked):
            dqn, dqp = carry
            ks = pl.ds(pl.multiple_of(j * tq, tq), tq)
            kn, kp = kn_ref[ks, :], kp_ref[ks, :]
            s = (_dot(qn, kn, NT) + _dot(qp, kp, NT)) * scale
            p = jnp.exp(s - lse1)
            if masked:
                row = lax.broadcasted_iota(jnp.int32, (tq, tq), 0)
                col = lax.broadcasted_iota(jnp.int32, (tq, tq), 1)
                p = jnp.where(col <= row, p, 0.0)
            dp = _dot(dob, v_ref[ks, :], NT)
            ds = (p * (dp - delta) * scale).astype(BF16)
            return dqn + _dot(ds, kn, NN), dqp + _dot(ds, kp, NN)

        init = (jnp.zeros((tq, LANE), F32), jnp.zeros((tq, LANE), F32))
        carry = lax.fori_loop(0, i, lambda j, c: step(j, c, False), init)
        dqn, dqp = step(i, carry, True)
        dqn_ref[...] = dqn
        dqp_ref[...] = dqp
        dl_ref[...] = jnp.broadcast_to(delta, (tq, LANE))

    blk = lambda h, i: (i, h)
    bs = pl.BlockSpec((tq, LANE), blk)
    return _call_with_side(
        body, side, 8, (nh, nq),
        name=name,
        in_specs=[
            bs,
            bs,
            pl.BlockSpec((t, LANE), lambda h, i: (0, h)),
            pl.BlockSpec((t, LANE), lambda h, i: (0, 0)),
            pl.BlockSpec((t, LANE), lambda h, i: (0, nh + h)),
            bs,
            bs,
            bs,
        ],
        out_specs=[bs, bs, bs],
        out_shape=[jax.ShapeDtypeStruct((t, nh * LANE), F32)] * 3,
    )(qf, qpe, kvf, kpe, kvf, do, o, lse)


def _mla_bwd_kv(qn16, qpe, kvf, kpe, do16, lse_row, delta_row, nh, name, side=None):
    t = qn16.shape[0]
    tq, nq = _mla_tiles(t)
    scale = (NOPE_DIM + ROPE_DIM) ** -0.5

    def body(kn_ref, kp_ref, v_ref, qn_ref, qp_ref, do_ref, lse_ref, dl_ref, dkn_ref, dkp_ref, dv_ref):
        j = pl.program_id(1)
        kn, kp, v = kn_ref[...], kp_ref[...], v_ref[...]

        def step(i, carry, masked):
            dkn, dkp, dv = carry
            qs = pl.ds(pl.multiple_of(i * tq, tq), tq)
            qn, qp, dob = qn_ref[qs, :], qp_ref[qs, :], do_ref[qs, :]
            st = (_dot(kn, qn, NT) + _dot(kp, qp, NT)) * scale
            pt = jnp.exp(st - lse_ref[0, :, qs])
            if masked:
                key = lax.broadcasted_iota(jnp.int32, (tq, tq), 0)
                qry = lax.broadcasted_iota(jnp.int32, (tq, tq), 1)
                pt = jnp.where(key <= qry, pt, 0.0)
            dv = dv + _dot(pt.astype(BF16), dob, NN)
            dpt = _dot(v, dob, NT)
            dst = (pt * (dpt - dl_ref[0, :, qs]) * scale).astype(BF16)
            return dkn + _dot(dst, qn, NN), dkp + _dot(dst, qp, NN), dv

        z = jnp.zeros((tq, LANE), F32)
        carry = step(j, (z, z, z), True)
        dkn, dkp, dv = lax.fori_loop(j + 1, nq, lambda i, c: step(i, c, False), carry)
        dkn_ref[...] = dkn
        dkp_ref[...] = dkp
        dv_ref[...] = dv

    blk = pl.BlockSpec((tq, LANE), lambda h, j: (j, h))
    res = lambda f: pl.BlockSpec((t, LANE), f)
    row = pl.BlockSpec((1, 1, t), lambda h, j: (h, 0, 0))
    return _call_with_side(
        body, side, 8, (nh, nq),
        name=name,
        in_specs=[
            blk,
            pl.BlockSpec((tq, LANE), lambda h, j: (j, 0)),
            pl.BlockSpec((tq, LANE), lambda h, j: (j, nh + h)),
            res(lambda h, j: (0, h)),
            res(lambda h, j: (0, h)),
            res(lambda h, j: (0, h)),
            row,
            row,
        ],
        out_specs=[blk, blk, blk],
        out_shape=[jax.ShapeDtypeStruct((t, nh * LANE), F32)] * 3,
    )(kvf, kpe, kvf, qn16, qpe, do16, lse_row, delta_row)


def _alibi_slopes(n):
    return [float(2.0 ** (-8.0 * i / n)) for i in range(1, n + 1)]


def _window_mask(has_prev):
    qi = lax.broadcasted_iota(jnp.int32, (BLK, 2 * BLK), 0)
    ki = lax.broadcasted_iota(jnp.int32, (BLK, 2 * BLK), 1)
    dist = qi + BLK - ki
    valid = (dist >= 0) & (dist <= BLK) & ((ki >= BLK) | has_prev)
    return valid, dist.astype(F32)


def _band_fwd(zd, nh, dil, name):
    t = zd.shape[0]
    nbc = t // BLK // dil
    scale = DIL_HEAD_DIM ** -0.5
    slopes = _alibi_slopes(nh)
    dw = nh * LANE

    def body(q_ref, kc_ref, kp_ref, vc_ref, vp_ref, o_ref, l_ref):
        mask, dist = _window_mask(pl.program_id(1) > 0)
        for h in range(nh):
            sl = slice(h * LANE, (h + 1) * LANE)
            keys = jnp.concatenate([kp_ref[:, sl], kc_ref[:, sl]], axis=0)
            vals = jnp.concatenate([vp_ref[:, sl], vc_ref[:, sl]], axis=0)
            s = jnp.where(mask, _dot(q_ref[:, sl], keys, NT) * scale - (slopes[h] * dil) * dist, NEG)
            m = jnp.max(s, -1, keepdims=True)
            e = jnp.exp(s - m)
            l = jnp.sum(e, -1, keepdims=True)
            o_ref[:, sl] = _dot((e * (1.0 / l)).astype(BF16), vals, NN)
            l_ref[:, sl] = jnp.broadcast_to(m + jnp.log(l), (BLK, LANE))

    prev = lambda i: jnp.maximum(i - 1, 0)
    spec = lambda f: pl.BlockSpec((BLK, dw), f)
    return pl.pallas_call(
        body,
        name=name,
        grid=(dil, nbc),
        in_specs=[
            spec(lambda r, i: (r * nbc + i, 0)),
            spec(lambda r, i: (r * nbc + i, 1)),
            spec(lambda r, i: (r * nbc + prev(i), 1)),
            spec(lambda r, i: (r * nbc + i, 2)),
            spec(lambda r, i: (r * nbc + prev(i), 2)),
        ],
        out_specs=[spec(lambda r, i: (r * nbc + i, 0))] * 2,
        out_shape=[jax.ShapeDtypeStruct((t, dw), F32)] * 2,
        compiler_params=_params(("parallel", "parallel")),
    )(zd, zd, zd, zd, zd)


def _band_bwd(zd, o, lse, do, dl, nh, dil, name):
    t = zd.shape[0]
    nbc = t // BLK // dil
    scale = DIL_HEAD_DIM ** -0.5
    slopes = _alibi_slopes(nh)
    dw = nh * LANE

    def body(q_ref, k_ref, v_ref, kp_ref, vp_ref, qn_ref, o_ref, l_ref, do_ref, dl_ref, on_ref, ln_ref, don_ref, dln_ref,
             dq_ref, dk_ref, dv_ref):
        mask, dist = _window_mask(pl.program_id(1) > 0)
        mask_n, dist_n = _window_mask(pl.program_id(1) + 1 < nbc)
        mask_n, dist_n = mask_n[:, :BLK], dist_n[:, :BLK]
        for h in range(nh):
            sl = slice(h * LANE, (h + 1) * LANE)
            bias = slopes[h] * dil
            q, k, v, qn = q_ref[:, sl], k_ref[:, sl], v_ref[:, sl], qn_ref[:, sl]
            keys = jnp.concatenate([kp_ref[:, sl], k], axis=0)
            vals = jnp.concatenate([vp_ref[:, sl], v], axis=0)
            dof, donf = do_ref[:, sl], don_ref[:, sl]
            dob, donb = dof.astype(BF16), donf.astype(BF16)
            lse1, lsen1 = l_ref[:, sl][:, :1], ln_ref[:, sl][:, :1]
            adj = jnp.sum(dl_ref[:, sl] - dof * o_ref[:, sl], -1, keepdims=True)
            adjn = jnp.sum(dln_ref[:, sl] - donf * on_ref[:, sl], -1, keepdims=True)
            p = jnp.where(mask, jnp.exp(_dot(q, keys, NT) * scale - bias * dist - lse1), 0.0)
            ds = (p * (_dot(dob, vals, NT) + adj)).astype(BF16)
            dq_ref[:, sl] = _dot(ds, keys, NN) * scale
            pn = jnp.where(mask_n, jnp.exp(_dot(qn, k, NT) * scale - bias * dist_n - lsen1), 0.0)
            dsn = (pn * (_dot(donb, v, NT) + adjn)).astype(BF16)
            both_q = jnp.concatenate([q, qn], axis=0)
            dk_ref[:, sl] = _dot(jnp.concatenate([ds[:, BLK:], dsn], axis=0), both_q, TN) * scale
            dv_ref[:, sl] = _dot(jnp.concatenate([p[:, BLK:].astype(BF16), pn.astype(BF16)], axis=0),
                                 jnp.concatenate([dob, donb], axis=0), TN)

    prev = lambda i: jnp.maximum(i - 1, 0)
    nxt = lambda i: jnp.minimum(i + 1, nbc - 1)
    spec = lambda f: pl.BlockSpec((BLK, dw), f)
    cur, nx = spec(lambda r, i: (r * nbc + i, 0)), spec(lambda r, i: (r * nbc + nxt(i), 0))
    return pl.pallas_call(
        body,
        name=name,
        grid=(dil, nbc),
        in_specs=[cur, spec(lambda r, i: (r * nbc + i, 1)), spec(lambda r, i: (r * nbc + i, 2)),
                  spec(lambda r, i: (r * nbc + prev(i), 1)), spec(lambda r, i: (r * nbc + prev(i), 2)),
                  nx, cur, cur, cur, cur, nx, nx, nx, nx],
        out_specs=[cur] * 3,
        out_shape=[jax.ShapeDtypeStruct((t, dw), F32)] * 3,
        compiler_params=_params(("parallel", "parallel")),
    )(zd, zd, zd, zd, zd, zd, o, lse, do, dl, o, lse, do, dl)


def _s5_ops(a_re, a_im, ldt, bt_re, bt_im, c_re, c_im):
    L, g, p = S5_CHUNK, S5_GROUP, S5_STATE
    dt = jnp.exp(ldt)
    lr, li = a_re * dt, a_im * dt
    er = jnp.exp(lr)
    lam_re, lam_im = er * jnp.cos(li), er * jnp.sin(li)
    nr, ni = lam_re - 1.0, lam_im
    den = a_re * a_re + a_im * a_im
    fr, fi = (nr * a_re + ni * a_im) / den, (ni * a_re - nr * a_im) / den
    bb_re, bb_im = fr * bt_re - fi * bt_im, fr * bt_im + fi * bt_re

    def power(tau):
        mag = jnp.exp(tau * lr)
        return mag * jnp.cos(tau * li), mag * jnp.sin(tau * li)

    step = lax.broadcasted_iota(jnp.int32, (L, 1), 0).astype(F32)

    def outer(pr, pi, mr, mi):
        re = pr[:, None, :] * mr[None] - pi[:, None, :] * mi[None]
        im = pr[:, None, :] * mi[None] + pi[:, None, :] * mr[None]
        return re.reshape(L * g, p), im.reshape(L * g, p)

    half = float(L // 2)
    cp_re, cp_im = outer(*power(step - half), c_re, c_im)
    pb_re, pb_im = outer(*power(half - step), bb_re, bb_im)
    toep = _dot(cp_re, pb_re, NT, HI) - _dot(cp_im, pb_im, NT, HI)
    trow = lax.broadcasted_iota(jnp.int32, (L * g, L * g), 0) // g
    scol = lax.broadcasted_iota(jnp.int32, (L * g, L * g), 1) // g
    toep = jnp.where(trow >= scol, toep, 0.0)
    et_re, et_im = outer(*power(float(L - 1) - step), bb_re, bb_im)
    f_re, f_im = outer(*power(step + 1.0), c_re, c_im)
    big_re, big_im = power(jnp.full((1, 1), float(L), F32))
    return toep, et_re, et_im, f_re, f_im, big_re, big_im


def _s5_y(ops, u, sp_re, sp_im):
    toep, _, _, f_re, f_im, _, _ = ops
    return _dot(u, toep, NT, HI) + _dot(sp_re, f_re, NT, HI) - _dot(sp_im, f_im, NT, HI)


def _s5_group_specs(ng):
    vec = pl.BlockSpec((1, 1, S5_STATE), lambda g: (g, 0, 0))
    one = pl.BlockSpec((1, 1, 1), lambda g: (g, 0, 0))
    mat = pl.BlockSpec((1, S5_GROUP, S5_STATE), lambda g: (g, 0, 0))
    return [vec, vec, one, mat, mat, mat, mat]


def _s5_load(refs):
    return [r[0] for r in refs]


def _s5_local(prm, u, name):
    ng, n, w = u.shape
    p = S5_STATE

    def body(*refs):
        ops = _s5_ops(*_s5_load(refs[:7]))
        uu = refs[7][0]
        refs[8][0] = _dot(uu, ops[1], NN, HI)
        refs[9][0] = _dot(uu, ops[2], NN, HI)
        refs[10][0] = ops[5]
        refs[11][0] = ops[6]

    blk = lambda a, b: pl.BlockSpec((1, a, b), lambda g: (g, 0, 0))
    return pl.pallas_call(
        body,
        name=name,
        grid=(ng,),
        in_specs=_s5_group_specs(ng) + [blk(n, w)],
        out_specs=[blk(n, p), blk(n, p), blk(1, p), blk(1, p)],
        out_shape=[jax.ShapeDtypeStruct((ng, n, p), F32)] * 2 + [jax.ShapeDtypeStruct((ng, 1, p), F32)] * 2,
        compiler_params=_params(("parallel",)),
    )(*prm, u)


def _s5_carry(e_re, e_im, lam_re, lam_im, name):
    n, w = e_re.shape
    cw = _tile(w, (1024, 512, 256, 128))

    def body(er_ref, ei_ref, lr_ref, li_ref, sr_ref, si_ref):
        lr, li = lr_ref[...], li_ref[...]

        def step(k, carry):
            sr, si = carry
            row = pl.ds(k, 1)
            sr_ref[row, :] = sr
            si_ref[row, :] = si
            return lr * sr - li * si + er_ref[row, :], li * sr + lr * si + ei_ref[row, :]

        z = jnp.zeros((1, cw), F32)
        lax.fori_loop(0, n, step, (z, z))

    col = pl.BlockSpec((n, cw), lambda j: (0, j))
    one = pl.BlockSpec((1, cw), lambda j: (0, j))
    return pl.pallas_call(
        body,
        name=name,
        grid=(w // cw,),
        in_specs=[col, col, one, one],
        out_specs=[col, col],
        out_shape=[jax.ShapeDtypeStruct((n, w), F32)] * 2,
        compiler_params=_params(("parallel",)),
    )(e_re, e_im, lam_re, lam_im)


def _s5_carry_bwd(dsp_re, dsp_im, sp_re, sp_im, lam_re, lam_im, name):
    n, w = dsp_re.shape
    cw = _tile(w, (1024, 512, 256, 128))

    def body(dr_ref, di_ref, sr_ref, si_ref, lr_ref, li_ref, gr_ref, gi_ref, dlr_ref, dli_ref):
        lr, li = lr_ref[...], li_ref[...]

        def step(q, carry):
            gr_next, gi_next, dr_next, di_next, alr, ali = carry
            k = n - 1 - q
            row = pl.ds(k, 1)
            gr = dr_next + lr * gr_next + li * gi_next
            gi = di_next - li * gr_next + lr * gi_next
            gr_ref[row, :] = gr
            gi_ref[row, :] = gi
            sr, si = sr_ref[row, :], si_ref[row, :]
            return gr, gi, dr_ref[row, :], di_ref[row, :], alr + gr * sr + gi * si, ali + gi * sr - gr * si

        z = jnp.zeros((1, cw), F32)
        out = lax.fori_loop(0, n, step, (z, z, z, z, z, z))
        dlr_ref[...] = out[4]
        dli_ref[...] = out[5]

    col = pl.BlockSpec((n, cw), lambda j: (0, j))
    one = pl.BlockSpec((1, cw), lambda j: (0, j))
    return pl.pallas_call(
        body,
        name=name,
        grid=(w // cw,),
        in_specs=[col, col, col, col, one, one],
        out_specs=[col, col, one, one],
        out_shape=[jax.ShapeDtypeStruct((n, w), F32)] * 2 + [jax.ShapeDtypeStruct((1, w), F32)] * 2,
        compiler_params=_params(("parallel",)),
    )(dsp_re, dsp_im, sp_re, sp_im, lam_re, lam_im)


def _s5_out(prm, u, sp_re, sp_im, name):
    ng, n, w = u.shape
    p = S5_STATE

    def body(*refs):
        ops = _s5_ops(*_s5_load(refs[:7]))
        refs[10][0] = _s5_y(ops, refs[7][0], refs[8][0], refs[9][0])

    blk = lambda a, b: pl.BlockSpec((1, a, b), lambda g: (g, 0, 0))
    return pl.pallas_call(
        body,
        name=name,
        grid=(ng,),
        in_specs=_s5_group_specs(ng) + [blk(n, w), blk(n, p), blk(n, p)],
        out_specs=blk(n, w),
        out_shape=jax.ShapeDtypeStruct((ng, n, w), F32),
        compiler_params=_params(("parallel",)),
    )(*prm, u, sp_re, sp_im)


def _s5_bwd_state(prm, dy, name):
    ng, n, w = dy.shape
    p = S5_STATE

    def body(*refs):
        ops = _s5_ops(*_s5_load(refs[:7]))
        d = refs[7][0]
        refs[8][0] = _dot(d, ops[3], NN, HI)
        refs[9][0] = -_dot(d, ops[4], NN, HI)

    blk = lambda a, b: pl.BlockSpec((1, a, b), lambda g: (g, 0, 0))
    return pl.pallas_call(
        body,
        name=name,
        grid=(ng,),
        in_specs=_s5_group_specs(ng) + [blk(n, w)],
        out_specs=[blk(n, p), blk(n, p)],
        out_shape=[jax.ShapeDtypeStruct((ng, n, p), F32)] * 2,
        compiler_params=_params(("parallel",)),
    )(*prm, dy)


def _s5_bwd_main(prm, u, sp_re, sp_im, dy, g_re, g_im, dlam_re, dlam_im, name):
    ng, n, w = u.shape
    p = S5_STATE

    def body(*refs):
        prm_v = _s5_load(refs[:7])
        uu, sr, si, d, gr, gi, dlr, dli = [r[0] for r in refs[7:15]]

        def phi(*args):
            ops = _s5_ops(*args[:7])
            y = _s5_y(ops, args[7], sr, si)
            e_re, e_im = _dot(args[7], ops[1], NN, HI), _dot(args[7], ops[2], NN, HI)
            return (jnp.sum(d * y) + jnp.sum(gr * e_re) + jnp.sum(gi * e_im)
                    + jnp.sum(dlr * ops[5]) + jnp.sum(dli * ops[6]))

        grads = jax.grad(phi, argnums=tuple(range(8)))(*prm_v, uu)
        for q in range(8):
            refs[15 + q][0] = grads[q]

    blk = lambda a, b: pl.BlockSpec((1, a, b), lambda g: (g, 0, 0))
    prm_specs = _s5_group_specs(ng)
    return pl.pallas_call(
        body,
        name=name,
        grid=(ng,),
        in_specs=prm_specs + [blk(n, w), blk(n, p), blk(n, p), blk(n, w), blk(n, p), blk(n, p), blk(1, p), blk(1, p)],
        out_specs=prm_specs + [blk(n, w)],
        out_shape=[jax.ShapeDtypeStruct(a.shape, F32) for a in prm] + [jax.ShapeDtypeStruct((ng, n, w), F32)],
        compiler_params=_params(("parallel",)),
    )(*prm, u, sp_re, sp_im, dy, g_re, g_im, dlam_re, dlam_im)


SLAB_GROUPS = LANE // S5_GROUP
CHUNK_TILES = S5_CHUNK * S5_GROUP // LANE


def _lane_segment(b):
    lane = lax.broadcasted_iota(jnp.int32, (1, LANE), 1)
    return jnp.logical_and(lane >= S5_GROUP * b, lane < S5_GROUP * (b + 1))


def _to_groups(h, name):
    t, d = h.shape
    n = t // S5_CHUNK

    def body(x_ref, o_ref):
        for a in range(CHUNK_TILES):
            rows = [x_ref[pl.ds(SLAB_GROUPS * a + b, n, stride=S5_CHUNK), :] for b in range(SLAB_GROUPS)]
            for g in range(SLAB_GROUPS):
                acc = jnp.zeros((n, LANE), F32)
                for b in range(SLAB_GROUPS):
                    shift = (S5_GROUP * (b - g)) % LANE
                    piece = pltpu.roll(rows[b], shift, 1) if shift else rows[b]
                    acc = jnp.where(_lane_segment(b), piece, acc)
                o_ref[g, :, a * LANE : (a + 1) * LANE] = acc

    return pl.pallas_call(
        body,
        name=name,
        grid=(d // LANE,),
        in_specs=[pl.BlockSpec((t, LANE), lambda q: (0, q))],
        out_specs=pl.BlockSpec((SLAB_GROUPS, n, S5_CHUNK * S5_GROUP), lambda q: (q, 0, 0)),
        out_shape=jax.ShapeDtypeStruct((d // S5_GROUP, n, S5_CHUNK * S5_GROUP), F32),
        compiler_params=_params(("parallel",)),
    )(h)


def _from_groups(y, name):
    ng, n, w = y.shape
    t, d = n * S5_CHUNK, ng * S5_GROUP

    def body(y_ref, x_ref):
        for a in range(CHUNK_TILES):
            tiles = [y_ref[g, :, a * LANE : (a + 1) * LANE] for g in range(SLAB_GROUPS)]
            for b in range(SLAB_GROUPS):
                acc = jnp.zeros((n, LANE), F32)
                for g in range(SLAB_GROUPS):
                    shift = (S5_GROUP * (g - b)) % LANE
                    piece = pltpu.roll(tiles[g], shift, 1) if shift else tiles[g]
                    acc = jnp.where(_lane_segment(g), piece, acc)
                x_ref[pl.ds(SLAB_GROUPS * a + b, n, stride=S5_CHUNK), :] = acc

    return pl.pallas_call(
        body,
        name=name,
        grid=(d // LANE,),
        in_specs=[pl.BlockSpec((SLAB_GROUPS, n, w), lambda q: (q, 0, 0))],
        out_specs=pl.BlockSpec((t, LANE), lambda q: (0, q)),
        out_shape=jax.ShapeDtypeStruct((t, d), F32),
        compiler_params=_params(("parallel",)),
    )(y)


def _states_to_cols(e):
    ng, n, p = e.shape
    return e.transpose(1, 0, 2).reshape(n, ng * p)


def _cols_to_states(s, ng):
    n = s.shape[0]
    return s.reshape(n, ng, S5_STATE).transpose(1, 0, 2)


def _me():
    return lax.axis_index("x"), lax.axis_index("y"), lax.axis_index("c")


def _flip(v, bit):
    return 1 - v if bit else v


COMM_SEMS = [pltpu.SemaphoreType.DMA((7,)), pltpu.SemaphoreType.DMA((7,)), pltpu.SemaphoreType.DMA]
ANY_SPEC = pl.BlockSpec(memory_space=pl.ANY)


def _gather_phases(x_ref, out_ref, send_sems, recv_sems, local_sem):
    mx, my, mc = _me()
    me, sibling = (mx, my, mc), (mx, my, 1 - mc)
    chips = [(1 - mx, my), (mx, 1 - my), (1 - mx, 1 - my)]

    def slot(px, py, pc):
        return out_ref.at[4 * px + 2 * py + pc]

    def copy(k, block, to, src=None):
        return pltpu.make_async_remote_copy(
            src_ref=slot(*block) if src is None else src,
            dst_ref=slot(*block),
            send_sem=send_sems.at[k],
            recv_sem=recv_sems.at[k],
            device_id=to,
            device_id_type=pl.DeviceIdType.MESH,
        )

    mine = pltpu.make_async_copy(x_ref, slot(*me), local_sem)
    first = [copy(0, me, sibling, src=x_ref)] + [copy(1 + j, me, (*chip, mc), src=x_ref) for j, chip in enumerate(chips)]
    passed = [copy(4 + j, (*chip, mc), sibling) for j, chip in enumerate(chips)]

    def start():
        mine.start()
        for cp in first:
            cp.start()

    def relay():
        for j, chip in enumerate(chips):
            copy(1 + j, (*chip, mc), me).wait_recv()
            passed[j].start()

    def finish():
        copy(0, sibling, me).wait_recv()
        for j, chip in enumerate(chips):
            copy(4 + j, (*chip, 1 - mc), me).wait_recv()
        for cp in first + passed:
            cp.wait_send()
        mine.wait()

    return start, relay, finish


def _exchange_phases(g_ref, out_ref, send_sems, recv_sems, local_sem):
    mx, my, mc = _me()
    mine_idx = 4 * mx + 2 * my + mc
    own = pltpu.make_async_copy(g_ref.at[mine_idx], out_ref.at[mine_idx], local_sem)
    copies = []
    for m in range(1, N_DEV):
        px, py, pc = _flip(mx, m & 4), _flip(my, m & 2), _flip(mc, m & 1)
        copies.append(
            pltpu.make_async_remote_copy(
                src_ref=g_ref.at[4 * px + 2 * py + pc],
                dst_ref=out_ref.at[mine_idx],
                send_sem=send_sems.at[m - 1],
                recv_sem=recv_sems.at[m - 1],
                device_id=(px, py, pc),
                device_id_type=pl.DeviceIdType.MESH,
            )
        )

    def start():
        own.start()
        for cp in copies:
            cp.start()

    def finish():
        for cp in copies:
            cp.wait_recv()
        for cp in copies:
            cp.wait_send()
        own.wait()

    return start, None, finish


def _side_out_shape(side):
    kind, arr = side
    return jax.ShapeDtypeStruct((N_DEV,) + arr.shape if kind == "gather" else arr.shape, arr.dtype)


def _run_side(side, refs, nsteps_major, nsteps_minor):
    kind = side[0]
    start, relay, finish = (_gather_phases if kind == "gather" else _exchange_phases)(*refs)
    a, b = pl.program_id(0), pl.program_id(1)
    pl.when(jnp.logical_and(a == 0, b == 0))(start)
    if relay is not None:
        pl.when(jnp.logical_and(a == nsteps_major - 1, b == 0))(relay)
    return lambda: pl.when(jnp.logical_and(a == nsteps_major - 1, b == nsteps_minor - 1))(finish)


def _all_gather(x, name):
    def body(*refs):
        start, relay, finish = _gather_phases(*refs)
        start()
        relay()
        finish()

    return pl.pallas_call(body, name=name, in_specs=[ANY_SPEC], out_specs=ANY_SPEC,
                          out_shape=_side_out_shape(("gather", x)), scratch_shapes=COMM_SEMS)(x)


def _all_to_all(g, name):
    def body(*refs):
        start, _, finish = _exchange_phases(*refs)
        start()
        finish()

    return pl.pallas_call(body, name=name, in_specs=[ANY_SPEC], out_specs=ANY_SPEC,
                          out_shape=_side_out_shape(("exchange", g)), scratch_shapes=COMM_SEMS)(g)


def _sum_slots(recv, name):
    _, r, c = recv.shape
    tr = _tile(r, (256, 128, 64, 32, 16, 8))

    def body(r_ref, o_ref):
        acc = r_ref[0].astype(F32)
        for k in range(1, N_DEV):
            acc = acc + r_ref[k].astype(F32)
        o_ref[...] = acc

    return pl.pallas_call(
        body,
        name=name,
        grid=(r // tr,),
        in_specs=[pl.BlockSpec((N_DEV, tr, c), lambda i: (0, i, 0))],
        out_specs=pl.BlockSpec((tr, c), lambda i: (i, 0)),
        out_shape=jax.ShapeDtypeStruct((r, c), F32),
        compiler_params=_params(("parallel",)),
    )(recv)


def _adamw(w, g, m, v, name):
    shape = w.shape
    c = shape[-1]
    as2d = lambda a: a.reshape(-1, c)
    w2, g2, m2, v2 = as2d(w), as2d(g), as2d(m), as2d(v)
    r = w2.shape[0]
    tr = _tile(r, (256, 128, 64, 32, 16, 8))
    c1 = 1.0 / (1.0 - ADAM_B1 ** ADAM_STEP)
    c2 = 1.0 / (1.0 - ADAM_B2 ** ADAM_STEP)

    def body(w_ref, g_ref, m_ref, v_ref, d_ref, mo_ref, vo_ref):
        gg = g_ref[...]
        mn = ADAM_B1 * m_ref[...] + (1.0 - ADAM_B1) * gg
        vn = ADAM_B2 * v_ref[...] + (1.0 - ADAM_B2) * (gg * gg)
        d_ref[...] = -ADAM_LR * ((mn * c1) / (jnp.sqrt(vn * c2) + ADAM_EPS) + ADAM_WD * w_ref[...])
        mo_ref[...] = mn
        vo_ref[...] = vn

    spec = pl.BlockSpec((tr, c), lambda i: (i, 0))
    d, mn, vn = pl.pallas_call(
        body,
        name=name,
        grid=(r // tr,),
        in_specs=[spec] * 4,
        out_specs=[spec] * 3,
        out_shape=[jax.ShapeDtypeStruct((r, c), F32)] * 3,
        compiler_params=_params(("parallel",)),
    )(w2, g2, m2, v2)
    return d.reshape(shape), mn.reshape(shape), vn.reshape(shape)


COL_SHARDED = ("attn_w_in", "mla_w_q_b", "mla_w_kv_b", "s5_d", "s5_w_glu", "ffn_w_in", "ple_w")
ROW_SHARDED = ("attn_w_out", "ffn_w_out", "ple_gate_w")
REPLICATED = ("mla_q_norm", "mla_kv_norm", "s5_a_re", "s5_a_im", "s5_log_dt", "s5_b_re", "s5_b_im", "s5_c_re",
              "s5_c_im", "ln1_g", "ln1_b", "ln2_g", "ln2_b")
ATTENTION_WEIGHTS = ("attn_w_in", "mla_w_q_b", "mla_w_kv_b", "attn_w_out")
BIG_WEIGHTS = ("attn_w_in", "mla_w_q_b", "mla_w_kv_b", "attn_w_out", "s5_w_glu", "ffn_w_in", "ffn_w_out", "ple_w", "ple_gate_w")
WEIGHTS = ("attn_w_in", "mla_q_norm", "mla_w_q_b", "mla_kv_norm", "mla_w_kv_b", "attn_w_out", "s5_a_re", "s5_a_im",
           "s5_log_dt", "s5_b_re", "s5_b_im", "s5_c_re", "s5_c_im", "s5_d", "s5_w_glu", "ln1_g", "ln1_b", "ffn_w_in",
           "ffn_w_out", "ple_w", "ple_gate_w", "ln2_g", "ln2_b")


def _unshard(name, gathered):
    if name in COL_SHARDED:
        full = jnp.moveaxis(gathered, 0, -2)
        return full.reshape(full.shape[:-2] + (full.shape[-2] * full.shape[-1],))
    full = jnp.moveaxis(gathered, 0, 1)
    return full.reshape((full.shape[0], full.shape[1] * full.shape[2]) + full.shape[3:])


def _by_owner(name, grad):
    if name in COL_SHARDED:
        g = grad.reshape(grad.shape[:-1] + (N_DEV, grad.shape[-1] // N_DEV))
        return jnp.moveaxis(g, -2, 0).reshape(N_DEV, -1)
    if name in ROW_SHARDED:
        g = grad.reshape((grad.shape[0], N_DEV, grad.shape[1] // N_DEV) + grad.shape[2:])
        return jnp.moveaxis(g, 1, 0).reshape(N_DEV, -1)
    return jnp.broadcast_to(grad.reshape(1, -1), (N_DEV, grad.size))


def _owner_rows(name, g):
    k, n = g.shape
    if name in COL_SHARDED:
        return g.reshape(k, N_DEV, n // N_DEV).transpose(1, 0, 2).reshape(N_DEV, -1)
    assert name in ROW_SHARDED, name
    return g.reshape(N_DEV, -1)


def _pack(pieces, axis):
    blocks, where, row = [], [], 0
    for p in pieces:
        n = p.shape[axis]
        pad = (-n) % (PACK_COLS * PACK_ROW_TILE)
        if pad:
            shape = list(p.shape)
            shape[axis] = pad
            p = jnp.concatenate([p, jnp.zeros(shape, p.dtype)], axis=axis)
        rows = (n + pad) // PACK_COLS
        blocks.append(p.reshape(p.shape[:axis] + (rows, PACK_COLS)))
        where.append((row, rows, n))
        row += rows
    return jnp.concatenate(blocks, axis=axis), where


def _unpack(buf, axis, where):
    row, rows, n = where
    part = lax.slice_in_dim(buf, row, row + rows, axis=axis)
    return lax.slice_in_dim(part.reshape(part.shape[:axis] + (rows * PACK_COLS,)), 0, n, axis=axis)


def _twice(fn):
    return lambda *a: fn(*a) * 2


def _ffn_block(h, hb, p_i, w_in, w_out, w_ple, w_pg, g, b, alpha, tag):
    t, d = h.shape
    hid = w_out.shape[0]
    cw = _tile(hid, (1408, 512, 256, 128))
    ncb = hid // cw
    gu = _mm(hb, w_in, name=f"ffn_in_{tag}")
    act = _rowwise(_swiglu_fn, [(gu, cw, 0), (gu, cw, ncb)], [], [(hid, cw, BF16)], name=f"swiglu_{tag}", tq=512, ncol=ncb)[0]
    f = _mm(act, w_out, name=f"ffn_out_{tag}")
    pw = _mm(p_i, w_ple, name=f"ple_{tag}")
    gate = _mm(hb, w_pg, name=f"ple_gate_{tag}")
    out, outb = _rowwise(_twice(_ln_ffn_fn(alpha)), [_whole(h), _whole(f), _whole(pw), _whole(gate)], [g, b],
                         [(d, d, F32), (d, d, BF16)], name=f"ln2_{tag}", tq=256)
    return out, outb, (h, hb, p_i, gu, act, f, pw, gate)


def _ffn_block_bwd(saved, dout, w_in, w_out, w_pg, g, b, alpha, tag):
    h, hb, p_i, gu, act, f, pw, gate = saved
    t, d = h.shape
    hid = w_out.shape[0]
    cw = _tile(hid, (1408, 512, 256, 128))
    ncb = hid // cw
    dh_a, df, dpw, dgate, dg, db = _rowwise_bwd(
        _ln_ffn_fn(alpha), [_whole(h), _whole(f), _whole(pw), _whole(gate)], [g, b], [dout],
        need=[True] * 4, drow=[(d, F32), (d, BF16), (d, BF16), (d, BF16)], name=f"ln2_bwd_{tag}", tq=128)
    dw_out = _mm(act, df, ta=True, out_dtype=BF16, name=f"ffn_out_dw_{tag}")
    dact = _mm(df, w_out, tb=True, out_dtype=BF16, name=f"ffn_out_dx_{tag}")
    dg_, du_ = _rowwise_bwd(_swiglu_fn, [(gu, cw, 0), (gu, cw, ncb)], [], [[(dact, cw, 0)]], need=[True, True],
                            drow=[(hid, BF16), (hid, BF16)], name=f"swiglu_bwd_{tag}", tq=256, ncol=ncb)
    dgu = jnp.concatenate([dg_, du_], axis=1)
    dw_in = _mm(hb, dgu, ta=True, out_dtype=BF16, name=f"ffn_in_dw_{tag}")
    dh_b = _mm(dgu, w_in, tb=True, name=f"ffn_in_dx_{tag}")
    dw_ple = _mm(p_i, dpw, ta=True, out_dtype=BF16, name=f"ple_dw_{tag}")
    dw_pg = _mm(hb, dgate, ta=True, out_dtype=BF16, name=f"ple_gate_dw_{tag}")
    dh_c = _mm(dgate, w_pg, tb=True, name=f"ple_gate_dx_{tag}")
    return [dh_a, dh_b, dh_c], dict(ffn_w_in=dw_in, ffn_w_out=dw_out, ple_w=dw_ple, ple_gate_w=dw_pg, ln2_g=dg, ln2_b=db)


def kernel(x, p, positions, attn_w_in, mla_q_norm, mla_w_q_b, mla_kv_norm, mla_w_kv_b, attn_w_out, s5_a_re, s5_a_im, s5_log_dt, s5_b_re, s5_b_im, s5_c_re, s5_c_im, s5_d, s5_w_glu, ln1_g, ln1_b, ffn_w_in, ffn_w_out, ple_w, ple_gate_w, ln2_g, ln2_b, loss_target, m_attn_w_in, m_mla_q_norm, m_mla_w_q_b, m_mla_kv_norm, m_mla_w_kv_b, m_attn_w_out, m_s5_a_re, m_s5_a_im, m_s5_log_dt, m_s5_b_re, m_s5_b_im, m_s5_c_re, m_s5_c_im, m_s5_d, m_s5_w_glu, m_ln1_g, m_ln1_b, m_ffn_w_in, m_ffn_w_out, m_ple_w, m_ple_gate_w, m_ln2_g, m_ln2_b, v_attn_w_in, v_mla_q_norm, v_mla_w_q_b, v_mla_kv_norm, v_mla_w_kv_b, v_attn_w_out, v_s5_a_re, v_s5_a_im, v_s5_log_dt, v_s5_b_re, v_s5_b_im, v_s5_c_re, v_s5_c_im, v_s5_d, v_s5_w_glu, v_ln1_g, v_ln1_b, v_ffn_w_in, v_ffn_w_out, v_ple_w, v_ple_gate_w, v_ln2_g, v_ln2_b):
    local = dict(attn_w_in=attn_w_in, mla_q_norm=mla_q_norm, mla_w_q_b=mla_w_q_b, mla_kv_norm=mla_kv_norm,
                 mla_w_kv_b=mla_w_kv_b, attn_w_out=attn_w_out, s5_a_re=s5_a_re, s5_a_im=s5_a_im, s5_log_dt=s5_log_dt,
                 s5_b_re=s5_b_re, s5_b_im=s5_b_im, s5_c_re=s5_c_re, s5_c_im=s5_c_im, s5_d=s5_d, s5_w_glu=s5_w_glu,
                 ln1_g=ln1_g, ln1_b=ln1_b, ffn_w_in=ffn_w_in, ffn_w_out=ffn_w_out, ple_w=ple_w, ple_gate_w=ple_gate_w,
                 ln2_g=ln2_g, ln2_b=ln2_b)
    mom_m = dict(zip(WEIGHTS, (m_attn_w_in, m_mla_q_norm, m_mla_w_q_b, m_mla_kv_norm, m_mla_w_kv_b, m_attn_w_out, m_s5_a_re, m_s5_a_im, m_s5_log_dt, m_s5_b_re, m_s5_b_im, m_s5_c_re, m_s5_c_im, m_s5_d, m_s5_w_glu, m_ln1_g, m_ln1_b, m_ffn_w_in, m_ffn_w_out, m_ple_w, m_ple_gate_w, m_ln2_g, m_ln2_b)))
    mom_v = dict(zip(WEIGHTS, (v_attn_w_in, v_mla_q_norm, v_mla_w_q_b, v_mla_kv_norm, v_mla_w_kv_b, v_attn_w_out, v_s5_a_re, v_s5_a_im, v_s5_log_dt, v_s5_b_re, v_s5_b_im, v_s5_c_re, v_s5_c_im, v_s5_d, v_s5_w_glu, v_ln1_g, v_ln1_b, v_ffn_w_in, v_ffn_w_out, v_ple_w, v_ple_gate_w, v_ln2_g, v_ln2_b)))

    t, d = x.shape[1], x.shape[2]
    depth = ln1_g.shape[0]
    alpha = (2.0 * depth) ** 0.25
    ql, kvl = mla_q_norm.shape[-1], mla_kv_norm.shape[-1]
    nh = mla_w_q_b.shape[-1] * N_DEV // (NOPE_DIM + ROPE_DIM)
    dw = (attn_w_in.shape[-1] * N_DEV - ql - kvl - ROPE_DIM) // 3
    ndh = dw // DIL_HEAD_DIM
    ng = d // S5_GROUP
    assert ql == kvl and (ql + kvl) % LANE == 0 and nh * V_DIM == dw

    sharded = [n for n in WEIGHTS if n in COL_SHARDED or n in ROW_SHARDED]
    def as_words(n):
        if n == "s5_d":
            return lax.bitcast_convert_type(local[n], BF16).reshape(-1)
        if n == "attn_w_in":
            return local[n][0].T.astype(BF16).reshape(-1)
        return local[n].astype(BF16).reshape(-1)

    full = {}

    def unpack_weights(names, gathered, where):
        for n, place in zip(names, where):
            piece = _unpack(gathered, 1, place)
            if n == "attn_w_in":
                full[n] = piece.reshape(N_DEV * local[n].shape[2], d)
                continue
            if n == "s5_d":
                piece = lax.bitcast_convert_type(piece.reshape((N_DEV,) + local[n].shape + (2,)), F32)
            full[n] = _unshard(n, piece.reshape((N_DEV,) + local[n].shape))

    first_w = [n for n in sharded if n in ATTENTION_WEIGHTS]
    later_w = [n for n in sharded if n not in ATTENTION_WEIGHTS]
    first_packed, first_where = _pack([as_words(n) for n in first_w], 0)
    unpack_weights(first_w, _all_gather(first_packed, "gather_attn_weights"), first_where)
    later_packed, later_where = _pack([as_words(n) for n in later_w], 0)

    w_in_t = full["attn_w_in"]
    lat = ql + kvl
    w_lat_t = jnp.concatenate([w_in_t[: lat + ROPE_DIM], jnp.zeros((LANE - ROPE_DIM, d), BF16)], axis=0)
    w_dil_t = w_in_t[lat + ROPE_DIM :]
    wq = full["mla_w_q_b"][0].reshape(ql, nh, NOPE_DIM + ROPE_DIM)
    wq_pe = jnp.pad(wq[:, :, NOPE_DIM:], ((0, 0), (0, 0), (0, LANE - ROPE_DIM)))
    wqp = jnp.concatenate([wq[:, :, :NOPE_DIM].reshape(ql, nh * LANE), wq_pe.reshape(ql, nh * LANE)], axis=1)
    wkv = full["mla_w_kv_b"][0].reshape(kvl, nh, NOPE_DIM + V_DIM)
    wkvp = jnp.concatenate([wkv[:, :, :NOPE_DIM].reshape(kvl, nh * LANE), wkv[:, :, NOPE_DIM:].reshape(kvl, nh * LANE)], axis=1)
    w_out = full["attn_w_out"][0]

    h0 = x[0]
    h0b = h0.astype(BF16)
    pb = p.astype(BF16)
    target = loss_target[0]
    pos = positions.reshape(t, 1)
    grads = {}

    z = _mm(h0b, w_lat_t, tb=True, name="attn_in_lat")
    zd = _mm(h0b, w_dil_t, tb=True, out_dtype=BF16, name="attn_in_dil")
    gq, gk = mla_q_norm.reshape(1, ql), mla_kv_norm.reshape(1, kvl)
    qn, kvn = _rowwise(_rms_fn, [(z, ql, 0), (z, kvl, 1)], [gq, gk], [(ql, ql, BF16), (kvl, kvl, BF16)], name="rms", tq=256)
    qf = _mm(qn, wqp, name="q_up")
    kvf = _mm(kvn, wkvp, out_dtype=BF16, name="kv_up")
    cos, sin = _rope_tables(pos, "rope_tables")
    pe_cb = lat // LANE
    qpe, kpe = _rowwise(_rope_fwd_fn(nh), [(qf, nh * LANE, 1), (z, LANE, pe_cb), _whole(cos), _whole(sin)], [],
                        [(nh * LANE, nh * LANE, BF16), (LANE, LANE, BF16)], name="rope", tq=256)
    out_a, lse_a, later_gathered = _mla_fwd(qf, qpe, kvf, kpe, nh, "mla_fwd", side=("gather", later_packed))
    unpack_weights(later_w, later_gathered, later_where)
    w_glu = full["s5_w_glu"][0]
    d_skip = full["s5_d"]

    def to_classes(a, dil):
        if dil == 1:
            return a
        return a.reshape(t // dil, dil, a.shape[1]).transpose(1, 0, 2).reshape(t, a.shape[1])

    def from_classes(a, dil):
        if dil == 1:
            return a
        return a.reshape(dil, t // dil, a.shape[1]).transpose(1, 0, 2).reshape(t, a.shape[1])

    band = []
    for window, dil in DIL_BRANCHES:
        assert window // dil == BLK
        zc = to_classes(zd, dil)
        o_c, l_c = _band_fwd(zc, ndh, dil, f"band_fwd_d{dil}")
        band.append((dil, zc, o_c, l_c, from_classes(o_c, dil), from_classes(l_c, dil)))
    merge_rows = [_whole(b[4]) for b in band] + [_whole(b[5]) for b in band]
    out_b = _rowwise(_merge_fn, merge_rows, [], [(dw, dw, BF16)], name="merge", tq=256)[0]
    att = jnp.concatenate([out_a, out_b], axis=1)
    mix0 = _mm(att, w_out, name="attn_out")
    g1, b1 = ln1_g[0:1], ln1_b[0:1]
    h1, h1b = _rowwise(_twice(_ln_mix_fn(alpha)), [_whole(h0), _whole(mix0)], [g1, b1], [(d, d, F32), (d, d, BF16)],
                       name="ln1_l0", tq=256)
    h2, _, saved_f0 = _ffn_block(h1, h1b, pb[0, 0], full["ffn_w_in"][0], full["ffn_w_out"][0], full["ple_w"][0],
                                 full["ple_gate_w"][0], ln2_g[0:1], ln2_b[0:1], alpha, "l0")

    prm = [s5_a_re[0].reshape(ng, 1, S5_STATE), s5_a_im[0].reshape(ng, 1, S5_STATE), s5_log_dt[0].reshape(ng, 1, 1),
           s5_b_re[0].transpose(0, 2, 1), s5_b_im[0].transpose(0, 2, 1), s5_c_re[0], s5_c_im[0]]
    u = _to_groups(h2, "s5_regroup_u")
    e_re, e_im, lam_re, lam_im = _s5_local(prm, u, "s5_local")
    lam_re_c, lam_im_c = lam_re.reshape(1, ng * S5_STATE), lam_im.reshape(1, ng * S5_STATE)
    sp_re_c, sp_im_c = _s5_carry(_states_to_cols(e_re), _states_to_cols(e_im), lam_re_c, lam_im_c, "s5_carry")
    sp_re, sp_im = _cols_to_states(sp_re_c, ng), _cols_to_states(sp_im_c, ng)
    ys = _from_groups(_s5_out(prm, u, sp_re, sp_im, "s5_out"), "s5_ungroup_y")
    z5 = _rowwise(_gelu_fn, [_whole(ys), _whole(h2)], [d_skip], [(d, d, BF16)], name="gelu", tq=256)[0]
    vg = _mm(z5, w_glu, name="glu_in")
    g3, b3 = ln1_g[1:2], ln1_b[1:2]
    h3, h3b = _rowwise(_twice(_ln_glu_fn(alpha)), [_whole(h2), (vg, d, 0), (vg, d, 1)], [g3, b3],
                       [(d, d, F32), (d, d, BF16)], name="ln1_l1", tq=256)
    h4, _, saved_f1 = _ffn_block(h3, h3b, pb[1, 0], full["ffn_w_in"][1], full["ffn_w_out"][1], full["ple_w"][1],
                                 full["ple_gate_w"][1], ln2_g[1:2], ln2_b[1:2], alpha, "l1")

    dh4, loss_acc = _loss_kernel(h4, target, "loss")
    loss = lax.psum(loss_acc[0, 0], AXES)

    dh3, gf1 = _ffn_block_bwd(saved_f1, [_whole(dh4)], full["ffn_w_in"][1], full["ffn_w_out"][1],
                              full["ple_gate_w"][1], ln2_g[1:2], ln2_b[1:2], alpha, "l1")
    dh2_a, dval, dgate, dg3, db3 = _rowwise_bwd(
        _ln_glu_fn(alpha), [_whole(h2), (vg, d, 0), (vg, d, 1)], [g3, b3], [[_whole(a) for a in dh3]],
        need=[True] * 3, drow=[(d, F32), (d, BF16), (d, BF16)], name="ln1_bwd_l1", tq=128)
    dvg = jnp.concatenate([dval, dgate], axis=1)
    dw_glu = _mm(z5, dvg, ta=True, out_dtype=BF16, name="glu_dw")
    dz5 = _mm(dvg, w_glu, tb=True, name="glu_dx")
    dys, dh2_b, dd = _rowwise_bwd(_gelu_fn, [_whole(ys), _whole(h2)], [d_skip], [[_whole(dz5)]], need=[True, True],
                                  drow=[(d, F32), (d, F32)], name="gelu_bwd", tq=128)
    grads["s5_d"] = dd
    dy = _to_groups(dys, "s5_regroup_dy")
    dsp_re, dsp_im = _s5_bwd_state(prm, dy, "s5_bwd_state")
    g_re_c, g_im_c, dlam_re_c, dlam_im_c = _s5_carry_bwd(_states_to_cols(dsp_re), _states_to_cols(dsp_im), sp_re_c, sp_im_c,
                                                         lam_re_c, lam_im_c, "s5_carry_bwd")
    s5g = _s5_bwd_main(prm, u, sp_re, sp_im, dy, _cols_to_states(g_re_c, ng), _cols_to_states(g_im_c, ng),
                       dlam_re_c.reshape(ng, 1, S5_STATE), dlam_im_c.reshape(ng, 1, S5_STATE), "s5_bwd_main")
    grads["s5_a_re"] = s5g[0].reshape(s5_a_re.shape)
    grads["s5_a_im"] = s5g[1].reshape(s5_a_im.shape)
    grads["s5_log_dt"] = s5g[2].reshape(s5_log_dt.shape)
    grads["s5_b_re"] = s5g[3].transpose(0, 2, 1)[None]
    grads["s5_b_im"] = s5g[4].transpose(0, 2, 1)[None]
    grads["s5_c_re"] = s5g[5][None]
    grads["s5_c_im"] = s5g[6][None]
    dh2_c = _from_groups(s5g[7], "s5_ungroup_du")

    dh1, gf0 = _ffn_block_bwd(saved_f0, [_whole(dh2_a), _whole(dh2_b), _whole(dh2_c)], full["ffn_w_in"][0],
                              full["ffn_w_out"][0], full["ple_gate_w"][0], ln2_g[0:1], ln2_b[0:1], alpha, "l0")
    for k in ("ln2_g", "ln2_b"):
        grads[k] = jnp.concatenate([gf0[k], gf1[k]])

    shards = {}

    def send_buffer(items, dtype):
        return _pack([rows.astype(dtype) for _, rows in items], 1)

    def take(items, where, recv, tag):
        summed = _sum_slots(recv, f"sum_grads_{tag}")
        for (key, _), place in zip(items, where):
            shards[key] = _unpack(summed, 0, place)

    ffn_names = ("ffn_w_in", "ffn_w_out", "ple_w", "ple_gate_w")
    items_l1 = [(("s5_w_glu", 0), _owner_rows("s5_w_glu", dw_glu))] + [((k, 1), _owner_rows(k, gf1[k])) for k in ffn_names]
    items_l0 = [((k, 0), _owner_rows(k, gf0[k])) for k in ffn_names]
    dh0_a, dmix, dg1, db1 = _rowwise_bwd(_ln_mix_fn(alpha), [_whole(h0), _whole(mix0)], [g1, b1], [[_whole(a) for a in dh1]],
                                         need=[True, True], drow=[(d, F32), (d, BF16)], name="ln1_bwd_l0", tq=128)
    grads["ln1_g"] = jnp.concatenate([dg1, dg3])
    grads["ln1_b"] = jnp.concatenate([db1, db3])
    dw_out = _mm(att, dmix, ta=True, out_dtype=BF16, name="attn_out_dw")
    datt = _mm(dmix, w_out, tb=True, name="attn_out_dx")

    dmerge = _rowwise_bwd(_merge_fn, merge_rows, [], [[(datt, dw, 1)]], need=[True] * 6, drow=[(dw, F32)] * 6,
                          name="merge_bwd", tq=128)
    dq_s = dk_s = dv_s = None
    for k, (dil, zc, o_c, l_c, _, _) in enumerate(band):
        do_c, dl_c = to_classes(dmerge[k], dil), to_classes(dmerge[3 + k], dil)
        dq_c, dk_c, dv_c = _band_bwd(zc, o_c, l_c, do_c, dl_c, ndh, dil, f"band_bwd_d{dil}")
        dq_n, dk_n, dv_n = from_classes(dq_c, dil), from_classes(dk_c, dil), from_classes(dv_c, dil)
        dq_s = dq_n if dq_s is None else dq_s + dq_n
        dk_s = dk_n if dk_s is None else dk_s + dk_n
        dv_s = dv_n if dv_s is None else dv_s + dv_n

    send_l1, where_l1 = send_buffer(items_l1, BF16)
    dqn, dqp, delta, recv_l1 = _mla_bwd_q(qf, qpe, kvf, kpe, datt, out_a, lse_a, nh, "mla_bwd_q", side=("exchange", send_l1))
    take(items_l1, where_l1, recv_l1, "l1")
    to_row = lambda a: a[:, ::LANE].T.reshape(nh, 1, t)
    send_l0, where_l0 = send_buffer(items_l0, BF16)
    dkn, dkp, dv, recv_l0 = _mla_bwd_kv(qf[:, : nh * LANE].astype(BF16), qpe, kvf, kpe, datt[:, : nh * LANE].astype(BF16),
                                        to_row(lse_a), to_row(delta), nh, "mla_bwd_kv", side=("exchange", send_l0))
    take(items_l0, where_l0, recv_l0, "l0")
    dq_pe, dk_pe = _rowwise(_rope_bwd_fn(nh), [_whole(dqp), _whole(dkp), _whole(cos), _whole(sin)], [],
                            [(nh * LANE, nh * LANE, BF16), (LANE, LANE, BF16)], name="rope_bwd", tq=256)
    dqf = jnp.concatenate([dqn.astype(BF16), dq_pe], axis=1)
    dkvf = jnp.concatenate([dkn, dv], axis=1).astype(BF16)
    dwqp = _mm(qn, dqf, ta=True, name="q_up_dw")
    dqn_in = _mm(dqf, wqp, tb=True, name="q_up_dx")
    dwkvp = _mm(kvn, dkvf, ta=True, name="kv_up_dw")
    dkvn_in = _mm(dkvf, wkvp, tb=True, name="kv_up_dx")
    dql, dkvl, dgq, dgk = _rowwise_bwd(_rms_fn, [(z, ql, 0), (z, kvl, 1)], [gq, gk], [[_whole(dqn_in)], [_whole(dkvn_in)]],
                                       need=[True, True], drow=[(ql, BF16), (kvl, BF16)], name="rms_bwd", tq=256)
    grads["mla_q_norm"], grads["mla_kv_norm"] = dgq, dgk
    dz_lat = jnp.concatenate([dql, dkvl, dk_pe], axis=1)
    dz_dil = jnp.concatenate([dq_s, dk_s, dv_s], axis=1).astype(BF16)
    dw_lat_t = _mm(dz_lat, h0b, ta=True, out_dtype=BF16, name="attn_in_lat_dw")
    dw_dil_t = _mm(dz_dil, h0b, ta=True, out_dtype=BF16, name="attn_in_dil_dw")
    dh0_b = _mm(dz_lat, w_lat_t, name="attn_in_lat_dx")
    dh0_c = _mm(dz_dil, w_dil_t, name="attn_in_dil_dx")
    grad_x = _addn([dh0_a, dh0_b, dh0_c], "grad_x")[None]
    dw_in_t = jnp.concatenate([dw_lat_t[: lat + ROPE_DIM], dw_dil_t], axis=0)
    dwq_n = dwqp[:, : nh * LANE].reshape(ql, nh, NOPE_DIM)
    dwq_r = dwqp[:, nh * LANE :].reshape(ql, nh, LANE)[:, :, :ROPE_DIM]
    dwq = jnp.concatenate([dwq_n, dwq_r], axis=2).reshape(ql, nh * (NOPE_DIM + ROPE_DIM))
    dwkv_k = dwkvp[:, : nh * LANE].reshape(kvl, nh, NOPE_DIM)
    dwkv_v = dwkvp[:, nh * LANE :].reshape(kvl, nh, V_DIM)
    dwkv = jnp.concatenate([dwkv_k, dwkv_v], axis=2).reshape(kvl, nh * (NOPE_DIM + V_DIM))

    items_att = [(("attn_w_in", 0), dw_in_t.reshape(N_DEV, -1)), (("mla_w_q_b", 0), _owner_rows("mla_w_q_b", dwq)),
                 (("mla_w_kv_b", 0), _owner_rows("mla_w_kv_b", dwkv)), (("attn_w_out", 0), _owner_rows("attn_w_out", dw_out))]
    send_att, where_att = send_buffer(items_att, BF16)
    take(items_att, where_att, _all_to_all(send_att, "exchange_grads_attn"), "attn")
    items_small = [((n, None), _by_owner(n, grads[n])) for n in WEIGHTS if n not in BIG_WEIGHTS]
    send_small, where_small = send_buffer(items_small, F32)
    take(items_small, where_small, _all_to_all(send_small, "exchange_grads_small"), "small")

    g_out, d_out, m_out, v_out = [], [], [], []
    for n in WEIGHTS:
        if n == "attn_w_in":
            g = shards[(n, 0)].reshape(local[n].shape[2], d).T[None]
        elif n in BIG_WEIGHTS:
            g = jnp.concatenate([shards[(n, layer)] for layer in range(local[n].shape[0])]).reshape(local[n].shape)
        else:
            g = shards[(n, None)].reshape(local[n].shape)
        dlt, mn, vn = _adamw(local[n], g, mom_m[n], mom_v[n], f"adamw_{n}")
        g_out.append(g)
        d_out.append(dlt)
        m_out.append(mn)
        v_out.append(vn)
    return (loss, grad_x, *g_out, *d_out, *m_out, *v_out)
```

```python
import functools
import math

import numpy as np
import jax
import jax.numpy as jnp
from jax import lax
from jax.experimental import pallas as pl
from jax.experimental.pallas import tpu as pltpu

F32 = jnp.float32
BF16 = jnp.bfloat16

NOPE_DIM, ROPE_DIM, V_DIM = 128, 64, 128
DIL_HEAD_DIM = 128
DIL_BRANCHES = ((128, 1), (512, 4), (2048, 16))
BLK = 128
LANE = 128
ROPE_THETA = 10000.0
S5_GROUP, S5_STATE = 16, 64
S5_CHUNK = 32
NEG = -1e30
ADAM_LR, ADAM_B1, ADAM_B2, ADAM_EPS, ADAM_WD, ADAM_STEP = 0.001, 0.9, 0.999, 1e-08, 0.01, 10
N_DEV = 8
AXES = ("x", "y", "c")
VMEM_LIMIT = 56 * 1024 * 1024
MM_VMEM_BUDGET = 40 * 1024 * 1024
PACK_COLS = 1024
PACK_ROW_TILE = 16
HI = lax.Precision.HIGH


def _tile(n, cands):
    for c in cands:
        if n % c == 0:
            return c
    return n


def _params(sem=None):
    return pltpu.CompilerParams(dimension_semantics=sem, vmem_limit_bytes=VMEM_LIMIT)


def _dot(a, b, dims, prec=None):
    return lax.dot_general(a, b, (dims, ((), ())), preferred_element_type=F32, precision=prec)


NN = ((1,), (0,))
NT = ((1,), (1,))
TN = ((0,), (0,))


def _mm(a, b, *, ta=False, tb=False, out_dtype=F32, name):
    (k, m) = a.shape if ta else a.shape[::-1]
    (n, k2) = b.shape if tb else b.shape[::-1]
    assert k == k2, (a.shape, b.shape, ta, tb)
    tm = _tile(m, (1024, 512, 256, 128))
    tn = _tile(n, (1024, 512, 384, 256, 128))
    sa, sb, so = a.dtype.itemsize, b.dtype.itemsize, jnp.dtype(out_dtype).itemsize

    def fits(tk):
        return 2 * (tm * tk * sa + tk * tn * sb) + 2 * tm * tn * so + 4 * tm * tn <= MM_VMEM_BUDGET

    tk = next((c for c in (2816, 2048, 1536, 1408, 1152, 1024, 768, 512, 384, 256, 128) if k % c == 0 and fits(c)), k)
    nk = k // tk
    dims = (((0,) if ta else (1,)), ((1,) if tb else (0,)))

    def body(a_ref, b_ref, o_ref, *acc):
        part = _dot(a_ref[...].astype(BF16), b_ref[...].astype(BF16), dims)
        if nk == 1:
            o_ref[...] = part.astype(o_ref.dtype)
            return
        (acc_ref,) = acc
        kk = pl.program_id(2)

        @pl.when(kk == 0)
        def _():
            acc_ref[...] = part

        @pl.when(kk > 0)
        def _():
            acc_ref[...] += part

        @pl.when(kk == nk - 1)
        def _():
            o_ref[...] = acc_ref[...].astype(o_ref.dtype)

    a_spec = pl.BlockSpec((tk, tm), lambda i, j, kk: (kk, i)) if ta else pl.BlockSpec((tm, tk), lambda i, j, kk: (i, kk))
    b_spec = pl.BlockSpec((tn, tk), lambda i, j, kk: (j, kk)) if tb else pl.BlockSpec((tk, tn), lambda i, j, kk: (kk, j))
    return pl.pallas_call(
        body,
        name=name,
        grid=(m // tm, n // tn, nk),
        in_specs=[a_spec, b_spec],
        out_specs=pl.BlockSpec((tm, tn), lambda i, j, kk: (i, j)),
        out_shape=jax.ShapeDtypeStruct((m, n), out_dtype),
        scratch_shapes=[pltpu.VMEM((tm, tn), F32)] if nk > 1 else [],
        compiler_params=_params(("parallel", "parallel", "arbitrary")),
    )(a, b)


def _row_spec(spec, tq):
    _, w, cb = spec
    return pl.BlockSpec((tq, w), lambda i, j: (i, cb + j))


def _full_spec(p):
    return pl.BlockSpec(p.shape, lambda i, j: (0,) * p.ndim)


def _rowwise(fn, rows, pars, outs, *, name, tq, ncol=1):
    t = rows[0][0].shape[0]
    tq = _tile(t, (tq, 128, 64, 32, 16, 8))
    nr, npar = len(rows), len(pars)

    def body(*refs):
        vals = [r[...].astype(F32) for r in refs[: nr + npar]]
        res = fn(*vals)
        for o, r in zip(refs[nr + npar :], res):
            o[...] = r.astype(o.dtype)

    res = pl.pallas_call(
        body,
        name=name,
        grid=(t // tq, ncol),
        in_specs=[_row_spec(s, tq) for s in rows] + [_full_spec(p) for p in pars],
        out_specs=[pl.BlockSpec((tq, w), lambda i, j: (i, j)) for (_, w, _) in outs],
        out_shape=[jax.ShapeDtypeStruct((t, wt), dt) for (wt, _, dt) in outs],
        compiler_params=_params(("parallel", "parallel")),
    )(*[s[0] for s in rows], *pars)
    return res


def _rowwise_bwd(fn, rows, pars, cots, *, need, drow, name, tq, ncol=1):
    t = rows[0][0].shape[0]
    tq = _tile(t, (tq, 128, 64, 32, 16, 8))
    nr, npar = len(rows), len(pars)
    assert ncol == 1 or npar == 0
    flat_cots = [s for c in cots for s in c]
    ncot = len(flat_cots)
    want = [k for k in range(nr) if need[k]]

    def body(*refs):
        vals = [r[...].astype(F32) for r in refs[: nr + npar]]
        cref = refs[nr + npar : nr + npar + ncot]
        oref = refs[nr + npar + ncot :]
        cvals, pos = [], 0
        for c in cots:
            acc = cref[pos][...].astype(F32)
            for q in range(1, len(c)):
                acc = acc + cref[pos + q][...].astype(F32)
            cvals.append(acc)
            pos += len(c)

        def closed(*diff):
            full = list(vals)
            for k, dv in zip(want + list(range(nr, nr + npar)), diff):
                full[k] = dv
            return tuple(fn(*full))

        diff_in = [vals[k] for k in want] + vals[nr:]
        _, vjp = jax.vjp(closed, *diff_in)
        grads = vjp(tuple(cvals))
        for q in range(len(want)):
            oref[q][...] = grads[q].astype(oref[q].dtype)
        if npar:
            first = pl.program_id(0) == 0

            @pl.when(first)
            def _():
                for q in range(npar):
                    oref[len(want) + q][...] = jnp.zeros_like(oref[len(want) + q])

            for q in range(npar):
                oref[len(want) + q][...] += grads[len(want) + q]

    out_specs = [pl.BlockSpec((tq, rows[k][1]), lambda i, j, cb=rows[k][2]: (i, j)) for k in want]
    out_specs += [_full_spec(p) for p in pars]
    out_shape = [jax.ShapeDtypeStruct((t, wt), dt) for (wt, dt) in drow]
    out_shape += [jax.ShapeDtypeStruct(p.shape, F32) for p in pars]
    return pl.pallas_call(
        body,
        name=name,
        grid=(t // tq, ncol),
        in_specs=[_row_spec(s, tq) for s in rows] + [_full_spec(p) for p in pars] + [_row_spec(s, tq) for s in flat_cots],
        out_specs=out_specs,
        out_shape=out_shape,
        compiler_params=_params(("arbitrary", "arbitrary") if npar else ("parallel", "parallel")),
    )(*[s[0] for s in rows], *pars, *[s[0] for s in flat_cots])


def _whole(a, w=None):
    return (a, a.shape[1] if w is None else w, 0)


def _rms_fn(ql, kvl, gq, gk):
    def one(x, g):
        return x * lax.rsqrt(jnp.mean(x * x, -1, keepdims=True) + 1e-6) * g

    return one(ql, gq), one(kvl, gk)


def _layernorm(s, g, b):
    mu = jnp.mean(s, -1, keepdims=True)
    d = s - mu
    var = jnp.mean(d * d, -1, keepdims=True)
    return d * lax.rsqrt(var + 1e-5) * g + b


def _ln_mix_fn(alpha):
    def fn(h, mix, g, b):
        return (_layernorm(alpha * h + mix, g, b),)

    return fn


def _ln_glu_fn(alpha):
    def fn(h, val, gate, g, b):
        return (_layernorm(alpha * h + val * jax.nn.sigmoid(gate), g, b),)

    return fn


def _ln_ffn_fn(alpha):
    def fn(h, f, pw, gate, g, b):
        return (_layernorm(alpha * h + f + pw * jax.nn.sigmoid(gate), g, b),)

    return fn


def _swiglu_fn(g, u):
    return (jax.nn.silu(g) * u,)


def _gelu_fn(ys, h, d):
    return (jax.nn.gelu(ys + d * h),)


def _merge_fn(o1, o2, o3, l1, l2, l3):
    m = jnp.maximum(jnp.maximum(l1, l2), l3)
    e1, e2, e3 = jnp.exp(l1 - m), jnp.exp(l2 - m), jnp.exp(l3 - m)
    return ((e1 * o1 + e2 * o2 + e3 * o3) / (e1 + e2 + e3),)


def _swap_halves(t):
    w = t.shape[1]
    lane = lax.broadcasted_iota(jnp.int32, t.shape, 1) % LANE
    up = jnp.where(lane < ROPE_DIM, pltpu.roll(t, ROPE_DIM // 2, 1), 0.0)
    return jnp.where(lane < ROPE_DIM // 2, pltpu.roll(t, w - ROPE_DIM // 2, 1), up)


def _rope_fwd_fn(nh):
    def fn(qpe, kpe, cos, sin):
        cq, sq = jnp.tile(cos, (1, nh)), jnp.tile(sin, (1, nh))
        return qpe * cq + _swap_halves(qpe) * sq, kpe * cos + _swap_halves(kpe) * sin

    return fn


def _rope_bwd_fn(nh):
    def fn(dq, dk_heads, cos, sin):
        cq, sq = jnp.tile(cos, (1, nh)), jnp.tile(sin, (1, nh))
        dk = dk_heads[:, :LANE]
        for h in range(1, nh):
            dk = dk + dk_heads[:, h * LANE : (h + 1) * LANE]
        return dq * cq + _swap_halves(dq * sq), dk * cos + _swap_halves(dk * sin)

    return fn


def _rope_tables(positions, name):
    t = positions.shape[0]
    tq = _tile(t, (512, 128, 8))
    half = ROPE_DIM // 2

    def body(p_ref, c_ref, s_ref):
        lane = lax.broadcasted_iota(jnp.int32, (tq, LANE), 1)
        idx = (lane % half).astype(F32)
        inv_freq = jnp.exp(idx * (-math.log(ROPE_THETA) / half))
        ang = p_ref[...].astype(F32) * inv_freq
        live = lane < ROPE_DIM
        c_ref[...] = jnp.where(live, jnp.cos(ang), 0.0)
        s_ref[...] = jnp.where(live, jnp.where(lane < half, -jnp.sin(ang), jnp.sin(ang)), 0.0)

    return pl.pallas_call(
        body,
        name=name,
        grid=(t // tq,),
        in_specs=[pl.BlockSpec((tq, 1), lambda i: (i, 0))],
        out_specs=[pl.BlockSpec((tq, LANE), lambda i: (i, 0))] * 2,
        out_shape=[jax.ShapeDtypeStruct((t, LANE), F32)] * 2,
        compiler_params=_params(("parallel",)),
    )(positions)


def _loss_kernel(y, target, name):
    t, d = y.shape
    tq = _tile(t, (256, 128, 8))

    def body(y_ref, t_ref, dy_ref, l_ref):
        @pl.when(pl.program_id(0) == 0)
        def _():
            l_ref[...] = jnp.zeros_like(l_ref)

        e = y_ref[...] - t_ref[...]
        dy_ref[...] = e * (1.0 / d)
        l_ref[...] += jnp.sum(e * e) * (0.5 / d)

    return pl.pallas_call(
        body,
        name=name,
        grid=(t // tq,),
        in_specs=[pl.BlockSpec((tq, d), lambda i: (i, 0))] * 2,
        out_specs=[pl.BlockSpec((tq, d), lambda i: (i, 0)), pl.BlockSpec((8, LANE), lambda i: (0, 0))],
        out_shape=[jax.ShapeDtypeStruct((t, d), F32), jax.ShapeDtypeStruct((8, LANE), F32)],
        compiler_params=_params(("arbitrary",)),
    )(y, target)


def _addn(arrs, name):
    w = arrs[0].shape[1]
    return _rowwise(lambda *v: (functools.reduce(lambda p, q: p + q, v),), [_whole(a) for a in arrs], [], [(w, w, F32)],
                    name=name, tq=256)[0]


def _mla_tiles(t):
    tq = _tile(t, (512, 256, 128))
    return tq, t // tq


def _call_with_side(body, side, n_in, grid, *, name, in_specs, out_specs, out_shape):
    if side is None:
        return pl.pallas_call(body, name=name, grid=grid, in_specs=in_specs, out_specs=out_specs, out_shape=out_shape,
                              compiler_params=_params(("parallel", "parallel")))
    n_out, n = len(out_specs), len(side[1])

    def wrapped(*refs):
        ins, side_ins = refs[:n_in], refs[n_in : n_in + n]
        outs, side_outs = refs[n_in + n : n_in + n + n_out], refs[n_in + n + n_out : n_in + 2 * n + n_out]
        done = _run_side(side, side_ins, side_outs, refs[n_in + 2 * n + n_out :], grid[0], grid[1])
        body(*ins, *outs)
        done()

    call = pl.pallas_call(wrapped, name=name, grid=grid, in_specs=in_specs + [ANY_SPEC] * n,
                          out_specs=out_specs + [ANY_SPEC] * n, out_shape=out_shape + _side_out_shapes(side),
                          scratch_shapes=_comm_sems(n), compiler_params=_params(("arbitrary", "arbitrary")))

    def run(*args):
        res = call(*args, *side[1])
        return (*res[:n_out], list(res[n_out:]))

    return run


def _mla_fwd(qf, qpe, kvf, kpe, nh, name, side=None):
    t = qf.shape[0]
    tq, nq = _mla_tiles(t)
    scale = (NOPE_DIM + ROPE_DIM) ** -0.5

    def body(qn_ref, qp_ref, kn_ref, kp_ref, v_ref, o_ref, lse_ref):
        i = pl.program_id(1)
        qn = qn_ref[...].astype(BF16)
        qp = qp_ref[...]

        def step(j, carry, masked):
            m, l, acc = carry
            ks = pl.ds(pl.multiple_of(j * tq, tq), tq)
            s = (_dot(qn, kn_ref[ks, :], NT) + _dot(qp, kp_ref[ks, :], NT)) * scale
            if masked:
                row = lax.broadcasted_iota(jnp.int32, (tq, tq), 0)
                col = lax.broadcasted_iota(jnp.int32, (tq, tq), 1)
                s = jnp.where(col <= row, s, NEG)
            m_new = jnp.maximum(m, jnp.max(s, -1, keepdims=True))
            p = jnp.exp(s - m_new)
            a = jnp.exp(m - m_new)
            return m_new, a * l + jnp.sum(p, -1, keepdims=True), a * acc + _dot(p.astype(BF16), v_ref[ks, :], NN)

        init = (jnp.full((tq, 1), NEG, F32), jnp.zeros((tq, 1), F32), jnp.zeros((tq, V_DIM), F32))
        carry = lax.fori_loop(0, i, lambda j, c: step(j, c, False), init)
        m, l, acc = step(i, carry, True)
        o_ref[...] = (acc / l).astype(o_ref.dtype)
        lse_ref[...] = jnp.broadcast_to(m + jnp.log(l), (tq, LANE))

    blk = lambda h, i: (i, h)
    return _call_with_side(
        body, side, 5, (nh, nq),
        name=name,
        in_specs=[
            pl.BlockSpec((tq, LANE), blk),
            pl.BlockSpec((tq, LANE), blk),
            pl.BlockSpec((t, LANE), lambda h, i: (0, h)),
            pl.BlockSpec((t, LANE), lambda h, i: (0, 0)),
            pl.BlockSpec((t, LANE), lambda h, i: (0, nh + h)),
        ],
        out_specs=[pl.BlockSpec((tq, LANE), blk), pl.BlockSpec((tq, LANE), blk)],
        out_shape=[jax.ShapeDtypeStruct((t, nh * LANE), BF16), jax.ShapeDtypeStruct((t, nh * LANE), F32)],
    )(qf, qpe, kvf, kpe, kvf)


def _mla_bwd_q(qf, qpe, kvf, kpe, do, o, lse, nh, name, side=None):
    t = qf.shape[0]
    tq, nq = _mla_tiles(t)
    scale = (NOPE_DIM + ROPE_DIM) ** -0.5

    def body(qn_ref, qp_ref, kn_ref, kp_ref, v_ref, do_ref, o_ref, lse_ref, dqn_ref, dqp_ref, dl_ref):
        i = pl.program_id(1)
        qn = qn_ref[...].astype(BF16)
        qp = qp_ref[...]
        dof = do_ref[...].astype(F32)
        dob = dof.astype(BF16)
        delta = jnp.sum(dof * o_ref[...].astype(F32), -1, keepdims=True)
        lse1 = lse_ref[:, :1]

        def step(j, carry, masked):
            dqn, dqp = carry
            ks = pl.ds(pl.multiple_of(j * tq, tq), tq)
            kn, kp = kn_ref[ks, :], kp_ref[ks, :]
            s = (_dot(qn, kn, NT) + _dot(qp, kp, NT)) * scale
            p = jnp.exp(s - lse1)
            if masked:
                row = lax.broadcasted_iota(jnp.int32, (tq, tq), 0)
                col = lax.broadcasted_iota(jnp.int32, (tq, tq), 1)
                p = jnp.where(col <= row, p, 0.0)
            dp = _dot(dob, v_ref[ks, :], NT)
            ds = (p * (dp - delta) * scale).astype(BF16)
            return dqn + _dot(ds, kn, NN), dqp + _dot(ds, kp, NN)

        init = (jnp.zeros((tq, LANE), F32), jnp.zeros((tq, LANE), F32))
        carry = lax.fori_loop(0, i, lambda j, c: step(j, c, False), init)
        dqn, dqp = step(i, carry, True)
        dqn_ref[...] = dqn
        dqp_ref[...] = dqp
        dl_ref[...] = jnp.broadcast_to(delta, (tq, LANE))

    blk = lambda h, i: (i, h)
    bs = pl.BlockSpec((tq, LANE), blk)
    return _call_with_side(
        body, side, 8, (nh, nq),
        name=name,
        in_specs=[
            bs,
            bs,
            pl.BlockSpec((t, LANE), lambda h, i: (0, h)),
            pl.BlockSpec((t, LANE), lambda h, i: (0, 0)),
            pl.BlockSpec((t, LANE), lambda h, i: (0, nh + h)),
            bs,
            bs,
            bs,
        ],
        out_specs=[bs, bs, bs],
        out_shape=[jax.ShapeDtypeStruct((t, nh * LANE), F32)] * 3,
    )(qf, qpe, kvf, kpe, kvf, do, o, lse)


def _mla_bwd_kv(qn16, qpe, kvf, kpe, do16, lse_row, delta_row, nh, name, side=None):
    t = qn16.shape[0]
    tq, nq = _mla_tiles(t)
    scale = (NOPE_DIM + ROPE_DIM) ** -0.5

    def body(kn_ref, kp_ref, v_ref, qn_ref, qp_ref, do_ref, lse_ref, dl_ref, dkn_ref, dkp_ref, dv_ref):
        j = pl.program_id(1)
        kn, kp, v = kn_ref[...], kp_ref[...], v_ref[...]

        def step(i, carry, masked):
            dkn, dkp, dv = carry
            qs = pl.ds(pl.multiple_of(i * tq, tq), tq)
            qn, qp, dob = qn_ref[qs, :], qp_ref[qs, :], do_ref[qs, :]
            st = (_dot(kn, qn, NT) + _dot(kp, qp, NT)) * scale
            pt = jnp.exp(st - lse_ref[0, :, qs])
            if masked:
                key = lax.broadcasted_iota(jnp.int32, (tq, tq), 0)
                qry = lax.broadcasted_iota(jnp.int32, (tq, tq), 1)
                pt = jnp.where(key <= qry, pt, 0.0)
            dv = dv + _dot(pt.astype(BF16), dob, NN)
            dpt = _dot(v, dob, NT)
            dst = (pt * (dpt - dl_ref[0, :, qs]) * scale).astype(BF16)
            return dkn + _dot(dst, qn, NN), dkp + _dot(dst, qp, NN), dv

        z = jnp.zeros((tq, LANE), F32)
        carry = step(j, (z, z, z), True)
        dkn, dkp, dv = lax.fori_loop(j + 1, nq, lambda i, c: step(i, c, False), carry)
        dkn_ref[...] = dkn
        dkp_ref[...] = dkp
        dv_ref[...] = dv

    blk = pl.BlockSpec((tq, LANE), lambda h, j: (j, h))
    res = lambda f: pl.BlockSpec((t, LANE), f)
    row = pl.BlockSpec((1, 1, t), lambda h, j: (h, 0, 0))
    return _call_with_side(
        body, side, 8, (nh, nq),
        name=name,
        in_specs=[
            blk,
            pl.BlockSpec((tq, LANE), lambda h, j: (j, 0)),
            pl.BlockSpec((tq, LANE), lambda h, j: (j, nh + h)),
            res(lambda h, j: (0, h)),
            res(lambda h, j: (0, h)),
            res(lambda h, j: (0, h)),
            row,
            row,
        ],
        out_specs=[blk, blk, blk],
        out_shape=[jax.ShapeDtypeStruct((t, nh * LANE), F32)] * 3,
    )(kvf, kpe, kvf, qn16, qpe, do16, lse_row, delta_row)


def _alibi_slopes(n):
    return [float(2.0 ** (-8.0 * i / n)) for i in range(1, n + 1)]


def _window_mask(has_prev):
    qi = lax.broadcasted_iota(jnp.int32, (BLK, 2 * BLK), 0)
    ki = lax.broadcasted_iota(jnp.int32, (BLK, 2 * BLK), 1)
    dist = qi + BLK - ki
    valid = (dist >= 0) & (dist <= BLK) & ((ki >= BLK) | has_prev)
    return valid, dist.astype(F32)


def _band_fwd(zd, nh, dil, name):
    t = zd.shape[0]
    nbc = t // BLK // dil
    scale = DIL_HEAD_DIM ** -0.5
    slopes = _alibi_slopes(nh)
    dw = nh * LANE

    def body(q_ref, kc_ref, kp_ref, vc_ref, vp_ref, o_ref, l_ref):
        mask, dist = _window_mask(pl.program_id(1) > 0)
        for h in range(nh):
            sl = slice(h * LANE, (h + 1) * LANE)
            keys = jnp.concatenate([kp_ref[:, sl], kc_ref[:, sl]], axis=0)
            vals = jnp.concatenate([vp_ref[:, sl], vc_ref[:, sl]], axis=0)
            s = jnp.where(mask, _dot(q_ref[:, sl], keys, NT) * scale - (slopes[h] * dil) * dist, NEG)
            m = jnp.max(s, -1, keepdims=True)
            e = jnp.exp(s - m)
            l = jnp.sum(e, -1, keepdims=True)
            o_ref[:, sl] = _dot((e * (1.0 / l)).astype(BF16), vals, NN)
            l_ref[:, sl] = jnp.broadcast_to(m + jnp.log(l), (BLK, LANE))

    prev = lambda i: jnp.maximum(i - 1, 0)
    spec = lambda f: pl.BlockSpec((BLK, dw), f)
    return pl.pallas_call(
        body,
        name=name,
        grid=(dil, nbc),
        in_specs=[
            spec(lambda r, i: (r * nbc + i, 0)),
            spec(lambda r, i: (r * nbc + i, 1)),
            spec(lambda r, i: (r * nbc + prev(i), 1)),
            spec(lambda r, i: (r * nbc + i, 2)),
            spec(lambda r, i: (r * nbc + prev(i), 2)),
        ],
        out_specs=[spec(lambda r, i: (r * nbc + i, 0))] * 2,
        out_shape=[jax.ShapeDtypeStruct((t, dw), F32)] * 2,
        compiler_params=_params(("parallel", "parallel")),
    )(zd, zd, zd, zd, zd)


def _band_bwd(zd, o, lse, do, dl, nh, dil, name):
    t = zd.shape[0]
    nbc = t // BLK // dil
    scale = DIL_HEAD_DIM ** -0.5
    slopes = _alibi_slopes(nh)
    dw = nh * LANE

    def body(q_ref, k_ref, v_ref, kp_ref, vp_ref, qn_ref, o_ref, l_ref, do_ref, dl_ref, on_ref, ln_ref, don_ref, dln_ref,
             dq_ref, dk_ref, dv_ref):
        mask, dist = _window_mask(pl.program_id(1) > 0)
        mask_n, dist_n = _window_mask(pl.program_id(1) + 1 < nbc)
        mask_n, dist_n = mask_n[:, :BLK], dist_n[:, :BLK]
        for h in range(nh):
            sl = slice(h * LANE, (h + 1) * LANE)
            bias = slopes[h] * dil
            q, k, v, qn = q_ref[:, sl], k_ref[:, sl], v_ref[:, sl], qn_ref[:, sl]
            keys = jnp.concatenate([kp_ref[:, sl], k], axis=0)
            vals = jnp.concatenate([vp_ref[:, sl], v], axis=0)
            dof, donf = do_ref[:, sl], don_ref[:, sl]
            dob, donb = dof.astype(BF16), donf.astype(BF16)
            lse1, lsen1 = l_ref[:, sl][:, :1], ln_ref[:, sl][:, :1]
            adj = jnp.sum(dl_ref[:, sl] - dof * o_ref[:, sl], -1, keepdims=True)
            adjn = jnp.sum(dln_ref[:, sl] - donf * on_ref[:, sl], -1, keepdims=True)
            p = jnp.where(mask, jnp.exp(_dot(q, keys, NT) * scale - bias * dist - lse1), 0.0)
            ds = (p * (_dot(dob, vals, NT) + adj)).astype(BF16)
            dq_ref[:, sl] = _dot(ds, keys, NN) * scale
            pn = jnp.where(mask_n, jnp.exp(_dot(qn, k, NT) * scale - bias * dist_n - lsen1), 0.0)
            dsn = (pn * (_dot(donb, v, NT) + adjn)).astype(BF16)
            both_q = jnp.concatenate([q, qn], axis=0)
            dk_ref[:, sl] = _dot(jnp.concatenate([ds[:, BLK:], dsn], axis=0), both_q, TN) * scale
            dv_ref[:, sl] = _dot(jnp.concatenate([p[:, BLK:].astype(BF16), pn.astype(BF16)], axis=0),
                                 jnp.concatenate([dob, donb], axis=0), TN)

    prev = lambda i: jnp.maximum(i - 1, 0)
    nxt = lambda i: jnp.minimum(i + 1, nbc - 1)
    spec = lambda f: pl.BlockSpec((BLK, dw), f)
    cur, nx = spec(lambda r, i: (r * nbc + i, 0)), spec(lambda r, i: (r * nbc + nxt(i), 0))
    return pl.pallas_call(
        body,
        name=name,
        grid=(dil, nbc),
        in_specs=[cur, spec(lambda r, i: (r * nbc + i, 1)), spec(lambda r, i: (r * nbc + i, 2)),
                  spec(lambda r, i: (r * nbc + prev(i), 1)), spec(lambda r, i: (r * nbc + prev(i), 2)),
                  nx, cur, cur, cur, cur, nx, nx, nx, nx],
        out_specs=[cur] * 3,
        out_shape=[jax.ShapeDtypeStruct((t, dw), F32)] * 3,
        compiler_params=_params(("parallel", "parallel")),
    )(zd, zd, zd, zd, zd, zd, o, lse, do, dl, o, lse, do, dl)


def _s5_ops(a_re, a_im, ldt, bt_re, bt_im, c_re, c_im):
    L, g, p = S5_CHUNK, S5_GROUP, S5_STATE
    dt = jnp.exp(ldt)
    lr, li = a_re * dt, a_im * dt
    er = jnp.exp(lr)
    lam_re, lam_im = er * jnp.cos(li), er * jnp.sin(li)
    nr, ni = lam_re - 1.0, lam_im
    den = a_re * a_re + a_im * a_im
    fr, fi = (nr * a_re + ni * a_im) / den, (ni * a_re - nr * a_im) / den
    bb_re, bb_im = fr * bt_re - fi * bt_im, fr * bt_im + fi * bt_re

    def power(tau):
        mag = jnp.exp(tau * lr)
        return mag * jnp.cos(tau * li), mag * jnp.sin(tau * li)

    step = lax.broadcasted_iota(jnp.int32, (L, 1), 0).astype(F32)

    def outer(pr, pi, mr, mi):
        re = pr[:, None, :] * mr[None] - pi[:, None, :] * mi[None]
        im = pr[:, None, :] * mi[None] + pi[:, None, :] * mr[None]
        return re.reshape(L * g, p), im.reshape(L * g, p)

    half = float(L // 2)
    cp_re, cp_im = outer(*power(step - half), c_re, c_im)
    pb_re, pb_im = outer(*power(half - step), bb_re, bb_im)
    toep = _dot(cp_re, pb_re, NT, HI) - _dot(cp_im, pb_im, NT, HI)
    trow = lax.broadcasted_iota(jnp.int32, (L * g, L * g), 0) // g
    scol = lax.broadcasted_iota(jnp.int32, (L * g, L * g), 1) // g
    toep = jnp.where(trow >= scol, toep, 0.0)
    et_re, et_im = outer(*power(float(L - 1) - step), bb_re, bb_im)
    f_re, f_im = outer(*power(step + 1.0), c_re, c_im)
    big_re, big_im = power(jnp.full((1, 1), float(L), F32))
    return toep, et_re, et_im, f_re, f_im, big_re, big_im


def _s5_y(ops, u, sp_re, sp_im):
    toep, _, _, f_re, f_im, _, _ = ops
    return _dot(u, toep, NT, HI) + _dot(sp_re, f_re, NT, HI) - _dot(sp_im, f_im, NT, HI)


def _s5_group_specs(ng):
    vec = pl.BlockSpec((1, 1, S5_STATE), lambda g: (g, 0, 0))
    one = pl.BlockSpec((1, 1, 1), lambda g: (g, 0, 0))
    mat = pl.BlockSpec((1, S5_GROUP, S5_STATE), lambda g: (g, 0, 0))
    return [vec, vec, one, mat, mat, mat, mat]


def _s5_load(refs):
    return [r[0] for r in refs]


def _s5_local(prm, u, name):
    ng, n, w = u.shape
    p = S5_STATE

    def body(*refs):
        ops = _s5_ops(*_s5_load(refs[:7]))
        uu = refs[7][0]
        refs[8][0] = _dot(uu, ops[1], NN, HI)
        refs[9][0] = _dot(uu, ops[2], NN, HI)
        refs[10][0] = ops[5]
        refs[11][0] = ops[6]

    blk = lambda a, b: pl.BlockSpec((1, a, b), lambda g: (g, 0, 0))
    return pl.pallas_call(
        body,
        name=name,
        grid=(ng,),
        in_specs=_s5_group_specs(ng) + [blk(n, w)],
        out_specs=[blk(n, p), blk(n, p), blk(1, p), blk(1, p)],
        out_shape=[jax.ShapeDtypeStruct((ng, n, p), F32)] * 2 + [jax.ShapeDtypeStruct((ng, 1, p), F32)] * 2,
        compiler_params=_params(("parallel",)),
    )(*prm, u)


def _s5_carry(e_re, e_im, lam_re, lam_im, name):
    n, w = e_re.shape
    cw = _tile(w, (1024, 512, 256, 128))

    def body(er_ref, ei_ref, lr_ref, li_ref, sr_ref, si_ref):
        lr, li = lr_ref[...], li_ref[...]

        def step(k, carry):
            sr, si = carry
            row = pl.ds(k, 1)
            sr_ref[row, :] = sr
            si_ref[row, :] = si
            return lr * sr - li * si + er_ref[row, :], li * sr + lr * si + ei_ref[row, :]

        z = jnp.zeros((1, cw), F32)
        lax.fori_loop(0, n, step, (z, z))

    col = pl.BlockSpec((n, cw), lambda j: (0, j))
    one = pl.BlockSpec((1, cw), lambda j: (0, j))
    return pl.pallas_call(
        body,
        name=name,
        grid=(w // cw,),
        in_specs=[col, col, one, one],
        out_specs=[col, col],
        out_shape=[jax.ShapeDtypeStruct((n, w), F32)] * 2,
        compiler_params=_params(("parallel",)),
    )(e_re, e_im, lam_re, lam_im)


def _s5_carry_bwd(dsp_re, dsp_im, sp_re, sp_im, lam_re, lam_im, name):
    n, w = dsp_re.shape
    cw = _tile(w, (1024, 512, 256, 128))

    def body(dr_ref, di_ref, sr_ref, si_ref, lr_ref, li_ref, gr_ref, gi_ref, dlr_ref, dli_ref):
        lr, li = lr_ref[...], li_ref[...]

        def step(q, carry):
            gr_next, gi_next, dr_next, di_next, alr, ali = carry
            k = n - 1 - q
            row = pl.ds(k, 1)
            gr = dr_next + lr * gr_next + li * gi_next
            gi = di_next - li * gr_next + lr * gi_next
            gr_ref[row, :] = gr
            gi_ref[row, :] = gi
            sr, si = sr_ref[row, :], si_ref[row, :]
            return gr, gi, dr_ref[row, :], di_ref[row, :], alr + gr * sr + gi * si, ali + gi * sr - gr * si

        z = jnp.zeros((1, cw), F32)
        out = lax.fori_loop(0, n, step, (z, z, z, z, z, z))
        dlr_ref[...] = out[4]
        dli_ref[...] = out[5]

    col = pl.BlockSpec((n, cw), lambda j: (0, j))
    one = pl.BlockSpec((1, cw), lambda j: (0, j))
    return pl.pallas_call(
        body,
        name=name,
        grid=(w // cw,),
        in_specs=[col, col, col, col, one, one],
        out_specs=[col, col, one, one],
        out_shape=[jax.ShapeDtypeStruct((n, w), F32)] * 2 + [jax.ShapeDtypeStruct((1, w), F32)] * 2,
        compiler_params=_params(("parallel",)),
    )(dsp_re, dsp_im, sp_re, sp_im, lam_re, lam_im)


def _s5_out(prm, u, sp_re, sp_im, name):
    ng, n, w = u.shape
    p = S5_STATE

    def body(*refs):
        ops = _s5_ops(*_s5_load(refs[:7]))
        refs[10][0] = _s5_y(ops, refs[7][0], refs[8][0], refs[9][0])

    blk = lambda a, b: pl.BlockSpec((1, a, b), lambda g: (g, 0, 0))
    return pl.pallas_call(
        body,
        name=name,
        grid=(ng,),
        in_specs=_s5_group_specs(ng) + [blk(n, w), blk(n, p), blk(n, p)],
        out_specs=blk(n, w),
        out_shape=jax.ShapeDtypeStruct((ng, n, w), F32),
        compiler_params=_params(("parallel",)),
    )(*prm, u, sp_re, sp_im)


def _s5_bwd_state(prm, dy, name):
    ng, n, w = dy.shape
    p = S5_STATE

    def body(*refs):
        ops = _s5_ops(*_s5_load(refs[:7]))
        d = refs[7][0]
        refs[8][0] = _dot(d, ops[3], NN, HI)
        refs[9][0] = -_dot(d, ops[4], NN, HI)

    blk = lambda a, b: pl.BlockSpec((1, a, b), lambda g: (g, 0, 0))
    return pl.pallas_call(
        body,
        name=name,
        grid=(ng,),
        in_specs=_s5_group_specs(ng) + [blk(n, w)],
        out_specs=[blk(n, p), blk(n, p)],
        out_shape=[jax.ShapeDtypeStruct((ng, n, p), F32)] * 2,
        compiler_params=_params(("parallel",)),
    )(*prm, dy)


def _s5_bwd_main(prm, u, sp_re, sp_im, dy, g_re, g_im, dlam_re, dlam_im, name):
    ng, n, w = u.shape
    p = S5_STATE

    def body(*refs):
        prm_v = _s5_load(refs[:7])
        uu, sr, si, d, gr, gi, dlr, dli = [r[0] for r in refs[7:15]]

        def phi(*args):
            ops = _s5_ops(*args[:7])
            y = _s5_y(ops, args[7], sr, si)
            e_re, e_im = _dot(args[7], ops[1], NN, HI), _dot(args[7], ops[2], NN, HI)
            return (jnp.sum(d * y) + jnp.sum(gr * e_re) + jnp.sum(gi * e_im)
                    + jnp.sum(dlr * ops[5]) + jnp.sum(dli * ops[6]))

        grads = jax.grad(phi, argnums=tuple(range(8)))(*prm_v, uu)
        for q in range(8):
            refs[15 + q][0] = grads[q]

    blk = lambda a, b: pl.BlockSpec((1, a, b), lambda g: (g, 0, 0))
    prm_specs = _s5_group_specs(ng)
    return pl.pallas_call(
        body,
        name=name,
        grid=(ng,),
        in_specs=prm_specs + [blk(n, w), blk(n, p), blk(n, p), blk(n, w), blk(n, p), blk(n, p), blk(1, p), blk(1, p)],
        out_specs=prm_specs + [blk(n, w)],
        out_shape=[jax.ShapeDtypeStruct(a.shape, F32) for a in prm] + [jax.ShapeDtypeStruct((ng, n, w), F32)],
        compiler_params=_params(("parallel",)),
    )(*prm, u, sp_re, sp_im, dy, g_re, g_im, dlam_re, dlam_im)


SLAB_GROUPS = LANE // S5_GROUP
CHUNK_TILES = S5_CHUNK * S5_GROUP // LANE


def _lane_segment(b):
    lane = lax.broadcasted_iota(jnp.int32, (1, LANE), 1)
    return jnp.logical_and(lane >= S5_GROUP * b, lane < S5_GROUP * (b + 1))


def _to_groups(h, name):
    t, d = h.shape
    n = t // S5_CHUNK

    def body(x_ref, o_ref):
        for a in range(CHUNK_TILES):
            rows = [x_ref[pl.ds(SLAB_GROUPS * a + b, n, stride=S5_CHUNK), :] for b in range(SLAB_GROUPS)]
            for g in range(SLAB_GROUPS):
                acc = jnp.zeros((n, LANE), F32)
                for b in range(SLAB_GROUPS):
                    shift = (S5_GROUP * (b - g)) % LANE
                    piece = pltpu.roll(rows[b], shift, 1) if shift else rows[b]
                    acc = jnp.where(_lane_segment(b), piece, acc)
                o_ref[g, :, a * LANE : (a + 1) * LANE] = acc

    return pl.pallas_call(
        body,
        name=name,
        grid=(d // LANE,),
        in_specs=[pl.BlockSpec((t, LANE), lambda q: (0, q))],
        out_specs=pl.BlockSpec((SLAB_GROUPS, n, S5_CHUNK * S5_GROUP), lambda q: (q, 0, 0)),
        out_shape=jax.ShapeDtypeStruct((d // S5_GROUP, n, S5_CHUNK * S5_GROUP), F32),
        compiler_params=_params(("parallel",)),
    )(h)


def _from_groups(y, name):
    ng, n, w = y.shape
    t, d = n * S5_CHUNK, ng * S5_GROUP

    def body(y_ref, x_ref):
        for a in range(CHUNK_TILES):
            tiles = [y_ref[g, :, a * LANE : (a + 1) * LANE] for g in range(SLAB_GROUPS)]
            for b in range(SLAB_GROUPS):
                acc = jnp.zeros((n, LANE), F32)
                for g in range(SLAB_GROUPS):
                    shift = (S5_GROUP * (g - b)) % LANE
                    piece = pltpu.roll(tiles[g], shift, 1) if shift else tiles[g]
                    acc = jnp.where(_lane_segment(g), piece, acc)
                x_ref[pl.ds(SLAB_GROUPS * a + b, n, stride=S5_CHUNK), :] = acc

    return pl.pallas_call(
        body,
        name=name,
        grid=(d // LANE,),
        in_specs=[pl.BlockSpec((SLAB_GROUPS, n, w), lambda q: (q, 0, 0))],
        out_specs=pl.BlockSpec((t, LANE), lambda q: (0, q)),
        out_shape=jax.ShapeDtypeStruct((t, d), F32),
        compiler_params=_params(("parallel",)),
    )(y)


def _states_to_cols(e):
    ng, n, p = e.shape
    return e.transpose(1, 0, 2).reshape(n, ng * p)


def _cols_to_states(s, ng):
    n = s.shape[0]
    return s.reshape(n, ng, S5_STATE).transpose(1, 0, 2)


def _me():
    return lax.axis_index("x"), lax.axis_index("y"), lax.axis_index("c")


def _flip(v, bit):
    return 1 - v if bit else v


COPIES = N_DEV - 1
ANY_SPEC = pl.BlockSpec(memory_space=pl.ANY)


def _comm_sems(n):
    return [pltpu.SemaphoreType.DMA((COPIES * n,)), pltpu.SemaphoreType.DMA((COPIES * n,)), pltpu.SemaphoreType.DMA((n,))]


def _gather_phases(x_refs, out_refs, send_sems, recv_sems, local_sems):
    mx, my, mc = _me()
    me, sibling = (mx, my, mc), (mx, my, 1 - mc)
    chips = [(1 - mx, my), (mx, 1 - my), (1 - mx, 1 - my)]
    arrays = range(len(x_refs))

    def slot(a, px, py, pc):
        return out_refs[a].at[4 * px + 2 * py + pc]

    def copy(a, k, block, to, from_input=False):
        return pltpu.make_async_remote_copy(
            src_ref=x_refs[a] if from_input else slot(a, *block),
            dst_ref=slot(a, *block),
            send_sem=send_sems.at[COPIES * a + k],
            recv_sem=recv_sems.at[COPIES * a + k],
            device_id=to,
            device_id_type=pl.DeviceIdType.MESH,
        )

    mine = [pltpu.make_async_copy(x_refs[a], slot(a, *me), local_sems.at[a]) for a in arrays]
    first = [copy(a, 0, me, sibling, True) for a in arrays]
    first += [copy(a, 1 + j, me, (*chip, mc), True) for a in arrays for j, chip in enumerate(chips)]
    passed = {(a, j): copy(a, 4 + j, (*chip, mc), sibling) for a in arrays for j, chip in enumerate(chips)}

    def start():
        for cp in mine + first:
            cp.start()

    def relay():
        for j, chip in enumerate(chips):
            for a in arrays:
                copy(a, 1 + j, (*chip, mc), me).wait_recv()
                passed[(a, j)].start()

    def finish():
        for a in arrays:
            copy(a, 0, sibling, me).wait_recv()
            for j, chip in enumerate(chips):
                copy(a, 4 + j, (*chip, 1 - mc), me).wait_recv()
        for cp in first + list(passed.values()):
            cp.wait_send()
        for cp in mine:
            cp.wait()

    return start, relay, finish


def _exchange_phases(g_refs, out_refs, send_sems, recv_sems, local_sems):
    mx, my, mc = _me()
    mine_idx = 4 * mx + 2 * my + mc
    arrays = range(len(g_refs))
    own = [pltpu.make_async_copy(g_refs[a].at[mine_idx], out_refs[a].at[mine_idx], local_sems.at[a]) for a in arrays]
    copies = []
    for m in range(1, N_DEV):
        px, py, pc = _flip(mx, m & 4), _flip(my, m & 2), _flip(mc, m & 1)
        for a in arrays:
            copies.append(
                pltpu.make_async_remote_copy(
                    src_ref=g_refs[a].at[4 * px + 2 * py + pc],
                    dst_ref=out_refs[a].at[mine_idx],
                    send_sem=send_sems.at[COPIES * a + m - 1],
                    recv_sem=recv_sems.at[COPIES * a + m - 1],
                    device_id=(px, py, pc),
                    device_id_type=pl.DeviceIdType.MESH,
                )
            )

    def start():
        for cp in own + copies:
            cp.start()

    def finish():
        for cp in copies:
            cp.wait_recv()
        for cp in copies:
            cp.wait_send()
        for cp in own:
            cp.wait()

    return start, None, finish


def _side_out_shapes(side):
    kind, arrs = side
    return [jax.ShapeDtypeStruct((N_DEV,) + a.shape if kind == "gather" else a.shape, a.dtype) for a in arrs]


def _run_side(side, x_refs, out_refs, sems, nsteps_major, nsteps_minor):
    start, relay, finish = (_gather_phases if side[0] == "gather" else _exchange_phases)(x_refs, out_refs, *sems)
    a, b = pl.program_id(0), pl.program_id(1)
    pl.when(jnp.logical_and(a == 0, b == 0))(start)
    if relay is not None:
        pl.when(jnp.logical_and(a == nsteps_major - 1, b == 0))(relay)
    return lambda: pl.when(jnp.logical_and(a == nsteps_major - 1, b == nsteps_minor - 1))(finish)


def _exchange_call(kind, arrs, name):
    n = len(arrs)

    def body(*refs):
        start, relay, finish = (_gather_phases if kind == "gather" else _exchange_phases)(refs[:n], refs[n : 2 * n], *refs[2 * n :])
        start()
        if relay is not None:
            relay()
        finish()

    return pl.pallas_call(body, name=name, in_specs=[ANY_SPEC] * n, out_specs=[ANY_SPEC] * n,
                          out_shape=_side_out_shapes((kind, arrs)), scratch_shapes=_comm_sems(n))(*arrs)


def _sum_slots(recv, name):
    _, r, c = recv.shape
    tr = _tile(r, (256, 128, 64, 32, 16, 8))

    def body(r_ref, o_ref):
        acc = r_ref[0].astype(F32)
        for k in range(1, N_DEV):
            acc = acc + r_ref[k].astype(F32)
        o_ref[...] = acc

    return pl.pallas_call(
        body,
        name=name,
        grid=(r // tr,),
        in_specs=[pl.BlockSpec((N_DEV, tr, c), lambda i: (0, i, 0))],
        out_specs=pl.BlockSpec((tr, c), lambda i: (i, 0)),
        out_shape=jax.ShapeDtypeStruct((r, c), F32),
        compiler_params=_params(("parallel",)),
    )(recv)


def _adamw(w, g, m, v, name):
    shape = w.shape
    c = shape[-1]
    as2d = lambda a: a.reshape(-1, c)
    w2, g2, m2, v2 = as2d(w), as2d(g), as2d(m), as2d(v)
    r = w2.shape[0]
    tr = _tile(r, (256, 128, 64, 32, 16, 8))
    c1 = 1.0 / (1.0 - ADAM_B1 ** ADAM_STEP)
    c2 = 1.0 / (1.0 - ADAM_B2 ** ADAM_STEP)

    def body(w_ref, g_ref, m_ref, v_ref, d_ref, mo_ref, vo_ref):
        gg = g_ref[...]
        mn = ADAM_B1 * m_ref[...] + (1.0 - ADAM_B1) * gg
        vn = ADAM_B2 * v_ref[...] + (1.0 - ADAM_B2) * (gg * gg)
        d_ref[...] = -ADAM_LR * ((mn * c1) / (jnp.sqrt(vn * c2) + ADAM_EPS) + ADAM_WD * w_ref[...])
        mo_ref[...] = mn
        vo_ref[...] = vn

    spec = pl.BlockSpec((tr, c), lambda i: (i, 0))
    d, mn, vn = pl.pallas_call(
        body,
        name=name,
        grid=(r // tr,),
        in_specs=[spec] * 4,
        out_specs=[spec] * 3,
        out_shape=[jax.ShapeDtypeStruct((r, c), F32)] * 3,
        compiler_params=_params(("parallel",)),
    )(w2, g2, m2, v2)
    return d.reshape(shape), mn.reshape(shape), vn.reshape(shape)


COL_SHARDED = ("attn_w_in", "mla_w_q_b", "mla_w_kv_b", "s5_d", "s5_w_glu", "ffn_w_in", "ple_w")
ROW_SHARDED = ("attn_w_out", "ffn_w_out", "ple_gate_w")
REPLICATED = ("mla_q_norm", "mla_kv_norm", "s5_a_re", "s5_a_im", "s5_log_dt", "s5_b_re", "s5_b_im", "s5_c_re",
              "s5_c_im", "ln1_g", "ln1_b", "ln2_g", "ln2_b")
ATTENTION_WEIGHTS = ("attn_w_in", "mla_w_q_b", "mla_w_kv_b", "attn_w_out")
BIG_WEIGHTS = ("attn_w_in", "mla_w_q_b", "mla_w_kv_b", "attn_w_out", "s5_w_glu", "ffn_w_in", "ffn_w_out", "ple_w", "ple_gate_w")
WEIGHTS = ("attn_w_in", "mla_q_norm", "mla_w_q_b", "mla_kv_norm", "mla_w_kv_b", "attn_w_out", "s5_a_re", "s5_a_im",
           "s5_log_dt", "s5_b_re", "s5_b_im", "s5_c_re", "s5_c_im", "s5_d", "s5_w_glu", "ln1_g", "ln1_b", "ffn_w_in",
           "ffn_w_out", "ple_w", "ple_gate_w", "ln2_g", "ln2_b")


def _unshard(name, gathered):
    if name in COL_SHARDED:
        full = jnp.moveaxis(gathered, 0, -2)
        return full.reshape(full.shape[:-2] + (full.shape[-2] * full.shape[-1],))
    full = jnp.moveaxis(gathered, 0, 1)
    return full.reshape((full.shape[0], full.shape[1] * full.shape[2]) + full.shape[3:])


def _by_owner(name, grad):
    if name in COL_SHARDED:
        g = grad.reshape(grad.shape[:-1] + (N_DEV, grad.shape[-1] // N_DEV))
        return jnp.moveaxis(g, -2, 0).reshape(N_DEV, -1)
    if name in ROW_SHARDED:
        g = grad.reshape((grad.shape[0], N_DEV, grad.shape[1] // N_DEV) + grad.shape[2:])
        return jnp.moveaxis(g, 1, 0).reshape(N_DEV, -1)
    return jnp.broadcast_to(grad.reshape(1, -1), (N_DEV, grad.size))


def _owner_rows(name, g):
    k, n = g.shape
    if name in COL_SHARDED:
        return g.reshape(k, N_DEV, n // N_DEV).transpose(1, 0, 2)
    assert name in ROW_SHARDED, name
    return g.reshape(N_DEV, k // N_DEV, n)


def _pack(pieces, axis):
    blocks, where, row = [], [], 0
    for p in pieces:
        n = p.shape[axis]
        pad = (-n) % (PACK_COLS * PACK_ROW_TILE)
        if pad:
            shape = list(p.shape)
            shape[axis] = pad
            p = jnp.concatenate([p, jnp.zeros(shape, p.dtype)], axis=axis)
        rows = (n + pad) // PACK_COLS
        blocks.append(p.reshape(p.shape[:axis] + (rows, PACK_COLS)))
        where.append((row, rows, n))
        row += rows
    return jnp.concatenate(blocks, axis=axis), where


def _unpack(buf, axis, where):
    row, rows, n = where
    part = lax.slice_in_dim(buf, row, row + rows, axis=axis)
    return lax.slice_in_dim(part.reshape(part.shape[:axis] + (rows * PACK_COLS,)), 0, n, axis=axis)


def _twice(fn):
    return lambda *a: fn(*a) * 2


def _ffn_block(h, hb, p_i, w_in, w_out, w_ple, w_pg, g, b, alpha, tag):
    t, d = h.shape
    hid = w_out.shape[0]
    cw = _tile(hid, (1408, 512, 256, 128))
    ncb = hid // cw
    gu = _mm(hb, w_in, out_dtype=BF16, name=f"ffn_in_{tag}")
    act = _rowwise(_swiglu_fn, [(gu, cw, 0), (gu, cw, ncb)], [], [(hid, cw, BF16)], name=f"swiglu_{tag}", tq=512, ncol=ncb)[0]
    f = _mm(act, w_out, name=f"ffn_out_{tag}")
    pw = _mm(p_i, w_ple, name=f"ple_{tag}")
    gate = _mm(hb, w_pg, name=f"ple_gate_{tag}")
    out, outb = _rowwise(_twice(_ln_ffn_fn(alpha)), [_whole(h), _whole(f), _whole(pw), _whole(gate)], [g, b],
                         [(d, d, F32), (d, d, BF16)], name=f"ln2_{tag}", tq=256)
    return out, outb, (h, hb, p_i, gu, act, f, pw, gate)


def _ffn_block_bwd(saved, dout, w_in, w_out, w_pg, g, b, alpha, tag):
    h, hb, p_i, gu, act, f, pw, gate = saved
    t, d = h.shape
    hid = w_out.shape[0]
    cw = _tile(hid, (1408, 512, 256, 128))
    ncb = hid // cw
    dh_a, df, dpw, dgate, dg, db = _rowwise_bwd(
        _ln_ffn_fn(alpha), [_whole(h), _whole(f), _whole(pw), _whole(gate)], [g, b], [dout],
        need=[True] * 4, drow=[(d, F32), (d, BF16), (d, BF16), (d, BF16)], name=f"ln2_bwd_{tag}", tq=128)
    dw_out = _mm(act, df, ta=True, out_dtype=BF16, name=f"ffn_out_dw_{tag}")
    dact = _mm(df, w_out, tb=True, out_dtype=BF16, name=f"ffn_out_dx_{tag}")
    dg_, du_ = _rowwise_bwd(_swiglu_fn, [(gu, cw, 0), (gu, cw, ncb)], [], [[(dact, cw, 0)]], need=[True, True],
                            drow=[(hid, BF16), (hid, BF16)], name=f"swiglu_bwd_{tag}", tq=256, ncol=ncb)
    dgu = jnp.concatenate([dg_, du_], axis=1)
    dw_in = _mm(hb, dgu, ta=True, out_dtype=BF16, name=f"ffn_in_dw_{tag}")
    dh_b = _mm(dgu, w_in, tb=True, name=f"ffn_in_dx_{tag}")
    dw_ple = _mm(p_i, dpw, ta=True, out_dtype=BF16, name=f"ple_dw_{tag}")
    dw_pg = _mm(hb, dgate, ta=True, out_dtype=BF16, name=f"ple_gate_dw_{tag}")
    dh_c = _mm(dgate, w_pg, tb=True, name=f"ple_gate_dx_{tag}")
    return [dh_a, dh_b, dh_c], dict(ffn_w_in=dw_in, ffn_w_out=dw_out, ple_w=dw_ple, ple_gate_w=dw_pg, ln2_g=dg, ln2_b=db)


def kernel(x, p, positions, attn_w_in, mla_q_norm, mla_w_q_b, mla_kv_norm, mla_w_kv_b, attn_w_out, s5_a_re, s5_a_im, s5_log_dt, s5_b_re, s5_b_im, s5_c_re, s5_c_im, s5_d, s5_w_glu, ln1_g, ln1_b, ffn_w_in, ffn_w_out, ple_w, ple_gate_w, ln2_g, ln2_b, loss_target, m_attn_w_in, m_mla_q_norm, m_mla_w_q_b, m_mla_kv_norm, m_mla_w_kv_b, m_attn_w_out, m_s5_a_re, m_s5_a_im, m_s5_log_dt, m_s5_b_re, m_s5_b_im, m_s5_c_re, m_s5_c_im, m_s5_d, m_s5_w_glu, m_ln1_g, m_ln1_b, m_ffn_w_in, m_ffn_w_out, m_ple_w, m_ple_gate_w, m_ln2_g, m_ln2_b, v_attn_w_in, v_mla_q_norm, v_mla_w_q_b, v_mla_kv_norm, v_mla_w_kv_b, v_attn_w_out, v_s5_a_re, v_s5_a_im, v_s5_log_dt, v_s5_b_re, v_s5_b_im, v_s5_c_re, v_s5_c_im, v_s5_d, v_s5_w_glu, v_ln1_g, v_ln1_b, v_ffn_w_in, v_ffn_w_out, v_ple_w, v_ple_gate_w, v_ln2_g, v_ln2_b):
    local = dict(attn_w_in=attn_w_in, mla_q_norm=mla_q_norm, mla_w_q_b=mla_w_q_b, mla_kv_norm=mla_kv_norm,
                 mla_w_kv_b=mla_w_kv_b, attn_w_out=attn_w_out, s5_a_re=s5_a_re, s5_a_im=s5_a_im, s5_log_dt=s5_log_dt,
                 s5_b_re=s5_b_re, s5_b_im=s5_b_im, s5_c_re=s5_c_re, s5_c_im=s5_c_im, s5_d=s5_d, s5_w_glu=s5_w_glu,
                 ln1_g=ln1_g, ln1_b=ln1_b, ffn_w_in=ffn_w_in, ffn_w_out=ffn_w_out, ple_w=ple_w, ple_gate_w=ple_gate_w,
                 ln2_g=ln2_g, ln2_b=ln2_b)
    mom_m = dict(zip(WEIGHTS, (m_attn_w_in, m_mla_q_norm, m_mla_w_q_b, m_mla_kv_norm, m_mla_w_kv_b, m_attn_w_out, m_s5_a_re, m_s5_a_im, m_s5_log_dt, m_s5_b_re, m_s5_b_im, m_s5_c_re, m_s5_c_im, m_s5_d, m_s5_w_glu, m_ln1_g, m_ln1_b, m_ffn_w_in, m_ffn_w_out, m_ple_w, m_ple_gate_w, m_ln2_g, m_ln2_b)))
    mom_v = dict(zip(WEIGHTS, (v_attn_w_in, v_mla_q_norm, v_mla_w_q_b, v_mla_kv_norm, v_mla_w_kv_b, v_attn_w_out, v_s5_a_re, v_s5_a_im, v_s5_log_dt, v_s5_b_re, v_s5_b_im, v_s5_c_re, v_s5_c_im, v_s5_d, v_s5_w_glu, v_ln1_g, v_ln1_b, v_ffn_w_in, v_ffn_w_out, v_ple_w, v_ple_gate_w, v_ln2_g, v_ln2_b)))

    t, d = x.shape[1], x.shape[2]
    depth = ln1_g.shape[0]
    alpha = (2.0 * depth) ** 0.25
    ql, kvl = mla_q_norm.shape[-1], mla_kv_norm.shape[-1]
    nh = mla_w_q_b.shape[-1] * N_DEV // (NOPE_DIM + ROPE_DIM)
    dw = (attn_w_in.shape[-1] * N_DEV - ql - kvl - ROPE_DIM) // 3
    ndh = dw // DIL_HEAD_DIM
    ng = d // S5_GROUP
    assert ql == kvl and (ql + kvl) % LANE == 0 and nh * V_DIM == dw

    sharded = [n for n in WEIGHTS if n in COL_SHARDED or n in ROW_SHARDED]
    def as_sent(n):
        if n == "s5_d":
            return local[n]
        if n == "attn_w_in":
            return local[n][0].T.astype(BF16)
        return local[n].astype(BF16).reshape(-1, local[n].shape[-1])

    full = {}

    def take_weights(names, gathered):
        for n, got in zip(names, gathered):
            if n == "attn_w_in":
                full[n] = got.reshape(N_DEV * local[n].shape[2], d)
            else:
                full[n] = _unshard(n, got.reshape((N_DEV,) + local[n].shape))

    first_w = [n for n in sharded if n in ATTENTION_WEIGHTS]
    later_w = [n for n in sharded if n not in ATTENTION_WEIGHTS]
    take_weights(first_w, _exchange_call("gather", [as_sent(n) for n in first_w], "gather_attn_weights"))

    w_in_t = full["attn_w_in"]
    lat = ql + kvl
    w_lat_t = jnp.concatenate([w_in_t[: lat + ROPE_DIM], jnp.zeros((LANE - ROPE_DIM, d), BF16)], axis=0)
    w_dil_t = w_in_t[lat + ROPE_DIM :]
    wq = full["mla_w_q_b"][0].reshape(ql, nh, NOPE_DIM + ROPE_DIM)
    wq_pe = jnp.pad(wq[:, :, NOPE_DIM:], ((0, 0), (0, 0), (0, LANE - ROPE_DIM)))
    wqp = jnp.concatenate([wq[:, :, :NOPE_DIM].reshape(ql, nh * LANE), wq_pe.reshape(ql, nh * LANE)], axis=1)
    wkv = full["mla_w_kv_b"][0].reshape(kvl, nh, NOPE_DIM + V_DIM)
    wkvp = jnp.concatenate([wkv[:, :, :NOPE_DIM].reshape(kvl, nh * LANE), wkv[:, :, NOPE_DIM:].reshape(kvl, nh * LANE)], axis=1)
    w_out = full["attn_w_out"][0]

    h0 = x[0]
    h0b = h0.astype(BF16)
    pb = p.astype(BF16)
    target = loss_target[0]
    pos = positions.reshape(t, 1)
    grads = {}

    z = _mm(h0b, w_lat_t, tb=True, name="attn_in_lat")
    zd = _mm(h0b, w_dil_t, tb=True, out_dtype=BF16, name="attn_in_dil")
    gq, gk = mla_q_norm.reshape(1, ql), mla_kv_norm.reshape(1, kvl)
    qn, kvn = _rowwise(_rms_fn, [(z, ql, 0), (z, kvl, 1)], [gq, gk], [(ql, ql, BF16), (kvl, kvl, BF16)], name="rms", tq=256)
    qf = _mm(qn, wqp, name="q_up")
    kvf = _mm(kvn, wkvp, out_dtype=BF16, name="kv_up")
    cos, sin = _rope_tables(pos, "rope_tables")
    pe_cb = lat // LANE
    qpe, kpe = _rowwise(_rope_fwd_fn(nh), [(qf, nh * LANE, 1), (z, LANE, pe_cb), _whole(cos), _whole(sin)], [],
                        [(nh * LANE, nh * LANE, BF16), (LANE, LANE, BF16)], name="rope", tq=256)
    out_a, lse_a, later_gathered = _mla_fwd(qf, qpe, kvf, kpe, nh, "mla_fwd", side=("gather", [as_sent(n) for n in later_w]))
    take_weights(later_w, later_gathered)
    w_glu = full["s5_w_glu"][0]
    d_skip = full["s5_d"]

    def to_classes(a, dil):
        if dil == 1:
            return a
        return a.reshape(t // dil, dil, a.shape[1]).transpose(1, 0, 2).reshape(t, a.shape[1])

    def from_classes(a, dil):
        if dil == 1:
            return a
        return a.reshape(dil, t // dil, a.shape[1]).transpose(1, 0, 2).reshape(t, a.shape[1])

    band = []
    for window, dil in DIL_BRANCHES:
        assert window // dil == BLK
        zc = to_classes(zd, dil)
        o_c, l_c = _band_fwd(zc, ndh, dil, f"band_fwd_d{dil}")
        band.append((dil, zc, o_c, l_c, from_classes(o_c, dil), from_classes(l_c, dil)))
    merge_rows = [_whole(b[4]) for b in band] + [_whole(b[5]) for b in band]
    out_b = _rowwise(_merge_fn, merge_rows, [], [(dw, dw, BF16)], name="merge", tq=256)[0]
    att = jnp.concatenate([out_a, out_b], axis=1)
    mix0 = _mm(att, w_out, name="attn_out")
    g1, b1 = ln1_g[0:1], ln1_b[0:1]
    h1, h1b = _rowwise(_twice(_ln_mix_fn(alpha)), [_whole(h0), _whole(mix0)], [g1, b1], [(d, d, F32), (d, d, BF16)],
                       name="ln1_l0", tq=256)
    h2, _, saved_f0 = _ffn_block(h1, h1b, pb[0, 0], full["ffn_w_in"][0], full["ffn_w_out"][0], full["ple_w"][0],
                                 full["ple_gate_w"][0], ln2_g[0:1], ln2_b[0:1], alpha, "l0")

    prm = [s5_a_re[0].reshape(ng, 1, S5_STATE), s5_a_im[0].reshape(ng, 1, S5_STATE), s5_log_dt[0].reshape(ng, 1, 1),
           s5_b_re[0].transpose(0, 2, 1), s5_b_im[0].transpose(0, 2, 1), s5_c_re[0], s5_c_im[0]]
    u = _to_groups(h2, "s5_regroup_u")
    e_re, e_im, lam_re, lam_im = _s5_local(prm, u, "s5_local")
    lam_re_c, lam_im_c = lam_re.reshape(1, ng * S5_STATE), lam_im.reshape(1, ng * S5_STATE)
    sp_re_c, sp_im_c = _s5_carry(_states_to_cols(e_re), _states_to_cols(e_im), lam_re_c, lam_im_c, "s5_carry")
    sp_re, sp_im = _cols_to_states(sp_re_c, ng), _cols_to_states(sp_im_c, ng)
    ys = _from_groups(_s5_out(prm, u, sp_re, sp_im, "s5_out"), "s5_ungroup_y")
    z5 = _rowwise(_gelu_fn, [_whole(ys), _whole(h2)], [d_skip], [(d, d, BF16)], name="gelu", tq=256)[0]
    vg = _mm(z5, w_glu, name="glu_in")
    g3, b3 = ln1_g[1:2], ln1_b[1:2]
    h3, h3b = _rowwise(_twice(_ln_glu_fn(alpha)), [_whole(h2), (vg, d, 0), (vg, d, 1)], [g3, b3],
                       [(d, d, F32), (d, d, BF16)], name="ln1_l1", tq=256)
    h4, _, saved_f1 = _ffn_block(h3, h3b, pb[1, 0], full["ffn_w_in"][1], full["ffn_w_out"][1], full["ple_w"][1],
                                 full["ple_gate_w"][1], ln2_g[1:2], ln2_b[1:2], alpha, "l1")

    dh4, loss_acc = _loss_kernel(h4, target, "loss")
    loss = lax.psum(loss_acc[0, 0], AXES)

    dh3, gf1 = _ffn_block_bwd(saved_f1, [_whole(dh4)], full["ffn_w_in"][1], full["ffn_w_out"][1],
                              full["ple_gate_w"][1], ln2_g[1:2], ln2_b[1:2], alpha, "l1")
    dh2_a, dval, dgate, dg3, db3 = _rowwise_bwd(
        _ln_glu_fn(alpha), [_whole(h2), (vg, d, 0), (vg, d, 1)], [g3, b3], [[_whole(a) for a in dh3]],
        need=[True] * 3, drow=[(d, F32), (d, BF16), (d, BF16)], name="ln1_bwd_l1", tq=128)
    dvg = jnp.concatenate([dval, dgate], axis=1)
    dw_glu = _mm(z5, dvg, ta=True, out_dtype=BF16, name="glu_dw")
    dz5 = _mm(dvg, w_glu, tb=True, name="glu_dx")
    dys, dh2_b, dd = _rowwise_bwd(_gelu_fn, [_whole(ys), _whole(h2)], [d_skip], [[_whole(dz5)]], need=[True, True],
                                  drow=[(d, F32), (d, F32)], name="gelu_bwd", tq=128)
    grads["s5_d"] = dd
    dy = _to_groups(dys, "s5_regroup_dy")
    dsp_re, dsp_im = _s5_bwd_state(prm, dy, "s5_bwd_state")
    g_re_c, g_im_c, dlam_re_c, dlam_im_c = _s5_carry_bwd(_states_to_cols(dsp_re), _states_to_cols(dsp_im), sp_re_c, sp_im_c,
                                                         lam_re_c, lam_im_c, "s5_carry_bwd")
    s5g = _s5_bwd_main(prm, u, sp_re, sp_im, dy, _cols_to_states(g_re_c, ng), _cols_to_states(g_im_c, ng),
                       dlam_re_c.reshape(ng, 1, S5_STATE), dlam_im_c.reshape(ng, 1, S5_STATE), "s5_bwd_main")
    grads["s5_a_re"] = s5g[0].reshape(s5_a_re.shape)
    grads["s5_a_im"] = s5g[1].reshape(s5_a_im.shape)
    grads["s5_log_dt"] = s5g[2].reshape(s5_log_dt.shape)
    grads["s5_b_re"] = s5g[3].transpose(0, 2, 1)[None]
    grads["s5_b_im"] = s5g[4].transpose(0, 2, 1)[None]
    grads["s5_c_re"] = s5g[5][None]
    grads["s5_c_im"] = s5g[6][None]
    dh2_c = _from_groups(s5g[7], "s5_ungroup_du")

    dh1, gf0 = _ffn_block_bwd(saved_f0, [_whole(dh2_a), _whole(dh2_b), _whole(dh2_c)], full["ffn_w_in"][0],
                              full["ffn_w_out"][0], full["ple_gate_w"][0], ln2_g[0:1], ln2_b[0:1], alpha, "l0")
    for k in ("ln2_g", "ln2_b"):
        grads[k] = jnp.concatenate([gf0[k], gf1[k]])

    shards = {}

    def take(items, received):
        for (key, _), recv in zip(items, received):
            shards[key] = _sum_slots(recv, f"sum_grads_{key[0]}_{key[1]}")

    ffn_names = ("ffn_w_in", "ffn_w_out", "ple_w", "ple_gate_w")
    items_l1 = [(("s5_w_glu", 0), _owner_rows("s5_w_glu", dw_glu))] + [((k, 1), _owner_rows(k, gf1[k])) for k in ffn_names]
    items_l0 = [((k, 0), _owner_rows(k, gf0[k])) for k in ffn_names]
    dh0_a, dmix, dg1, db1 = _rowwise_bwd(_ln_mix_fn(alpha), [_whole(h0), _whole(mix0)], [g1, b1], [[_whole(a) for a in dh1]],
                                         need=[True, True], drow=[(d, F32), (d, BF16)], name="ln1_bwd_l0", tq=128)
    grads["ln1_g"] = jnp.concatenate([dg1, dg3])
    grads["ln1_b"] = jnp.concatenate([db1, db3])
    dw_out = _mm(att, dmix, ta=True, out_dtype=BF16, name="attn_out_dw")
    datt = _mm(dmix, w_out, tb=True, name="attn_out_dx")

    dmerge = _rowwise_bwd(_merge_fn, merge_rows, [], [[(datt, dw, 1)]], need=[True] * 6, drow=[(dw, F32)] * 6,
                          name="merge_bwd", tq=128)
    dq_s = dk_s = dv_s = None
    for k, (dil, zc, o_c, l_c, _, _) in enumerate(band):
        do_c, dl_c = to_classes(dmerge[k], dil), to_classes(dmerge[3 + k], dil)
        dq_c, dk_c, dv_c = _band_bwd(zc, o_c, l_c, do_c, dl_c, ndh, dil, f"band_bwd_d{dil}")
        dq_n, dk_n, dv_n = from_classes(dq_c, dil), from_classes(dk_c, dil), from_classes(dv_c, dil)
        dq_s = dq_n if dq_s is None else dq_s + dq_n
        dk_s = dk_n if dk_s is None else dk_s + dk_n
        dv_s = dv_n if dv_s is None else dv_s + dv_n

    dqn, dqp, delta, recv_l1 = _mla_bwd_q(qf, qpe, kvf, kpe, datt, out_a, lse_a, nh, "mla_bwd_q",
                                          side=("exchange", [rows for _, rows in items_l1]))
    take(items_l1, recv_l1)
    to_row = lambda a: a[:, ::LANE].T.reshape(nh, 1, t)
    dkn, dkp, dv, recv_l0 = _mla_bwd_kv(qf[:, : nh * LANE].astype(BF16), qpe, kvf, kpe, datt[:, : nh * LANE].astype(BF16),
                                        to_row(lse_a), to_row(delta), nh, "mla_bwd_kv",
                                        side=("exchange", [rows for _, rows in items_l0]))
    take(items_l0, recv_l0)
    dq_pe, dk_pe = _rowwise(_rope_bwd_fn(nh), [_whole(dqp), _whole(dkp), _whole(cos), _whole(sin)], [],
                            [(nh * LANE, nh * LANE, BF16), (LANE, LANE, BF16)], name="rope_bwd", tq=256)
    dqf = jnp.concatenate([dqn.astype(BF16), dq_pe], axis=1)
    dkvf = jnp.concatenate([dkn, dv], axis=1).astype(BF16)
    dwqp = _mm(qn, dqf, ta=True, name="q_up_dw")
    dqn_in = _mm(dqf, wqp, tb=True, name="q_up_dx")
    dwkvp = _mm(kvn, dkvf, ta=True, name="kv_up_dw")
    dkvn_in = _mm(dkvf, wkvp, tb=True, name="kv_up_dx")
    dql, dkvl, dgq, dgk = _rowwise_bwd(_rms_fn, [(z, ql, 0), (z, kvl, 1)], [gq, gk], [[_whole(dqn_in)], [_whole(dkvn_in)]],
                                       need=[True, True], drow=[(ql, BF16), (kvl, BF16)], name="rms_bwd", tq=256)
    grads["mla_q_norm"], grads["mla_kv_norm"] = dgq, dgk
    dz_lat = jnp.concatenate([dql, dkvl, dk_pe], axis=1)
    dz_dil = jnp.concatenate([dq_s, dk_s, dv_s], axis=1).astype(BF16)
    dw_lat_t = _mm(dz_lat, h0b, ta=True, out_dtype=BF16, name="attn_in_lat_dw")
    dw_dil_t = _mm(dz_dil, h0b, ta=True, out_dtype=BF16, name="attn_in_dil_dw")
    dh0_b = _mm(dz_lat, w_lat_t, name="attn_in_lat_dx")
    dh0_c = _mm(dz_dil, w_dil_t, name="attn_in_dil_dx")
    grad_x = _addn([dh0_a, dh0_b, dh0_c], "grad_x")[None]
    dw_in_t = jnp.concatenate([dw_lat_t[: lat + ROPE_DIM], dw_dil_t], axis=0)
    dwq_n = dwqp[:, : nh * LANE].reshape(ql, nh, NOPE_DIM)
    dwq_r = dwqp[:, nh * LANE :].reshape(ql, nh, LANE)[:, :, :ROPE_DIM]
    dwq = jnp.concatenate([dwq_n, dwq_r], axis=2).reshape(ql, nh * (NOPE_DIM + ROPE_DIM))
    dwkv_k = dwkvp[:, : nh * LANE].reshape(kvl, nh, NOPE_DIM)
    dwkv_v = dwkvp[:, nh * LANE :].reshape(kvl, nh, V_DIM)
    dwkv = jnp.concatenate([dwkv_k, dwkv_v], axis=2).reshape(kvl, nh * (NOPE_DIM + V_DIM))

    items_att = [(("attn_w_in", 0), dw_in_t.reshape(N_DEV, -1, PACK_COLS)),(("mla_w_q_b", 0), _owner_rows("mla_w_q_b", dwq.astype(BF16))),
                 (("mla_w_kv_b", 0), _owner_rows("mla_w_kv_b", dwkv.astype(BF16))), (("attn_w_out", 0), _owner_rows("attn_w_out", dw_out))]
    small = [n for n in WEIGHTS if n not in BIG_WEIGHTS]
    send_small, where_small = _pack([_by_owner(n, grads[n]).astype(F32) for n in small], 1)
    received = _exchange_call("exchange", [rows for _, rows in items_att] + [send_small], "exchange_grads_last")
    take(items_att, received[:-1])
    sum_small = _sum_slots(received[-1], "sum_grads_small")

    g_out, d_out, m_out, v_out = [], [], [], []
    for n in WEIGHTS:
        if n == "attn_w_in":
            g = shards[(n, 0)].reshape(local[n].shape[2], d).T[None]
        elif n in BIG_WEIGHTS:
            g = jnp.stack([shards[(n, layer)] for layer in range(local[n].shape[0])]).reshape(local[n].shape)
        else:
            g = _unpack(sum_small, 0, where_small[small.index(n)]).reshape(local[n].shape)
        dlt, mn, vn = _adamw(local[n], g, mom_m[n], mom_v[n], f"adamw_{n}")
        g_out.append(g)
        d_out.append(dlt)
        m_out.append(mn)
        v_out.append(vn)
    return (loss, grad_x, *g_out, *d_out, *m_out, *v_out)
```

```python
import functools
import math

import numpy as np
import jax
import jax.numpy as jnp
from jax import lax
from jax.experimental import pallas as pl
from jax.experimental.pallas import tpu as pltpu

F32 = jnp.float32
BF16 = jnp.bfloat16

NOPE_DIM, ROPE_DIM, V_DIM = 128, 64, 128
DIL_HEAD_DIM = 128
DIL_BRANCHES = ((128, 1), (512, 4), (2048, 16))
BLK = 128
LANE = 128
ROPE_THETA = 10000.0
S5_GROUP, S5_STATE = 16, 64
S5_CHUNK = 32
NEG = -1e30
ADAM_LR, ADAM_B1, ADAM_B2, ADAM_EPS, ADAM_WD, ADAM_STEP = 0.001, 0.9, 0.999, 1e-08, 0.01, 10
N_DEV = 8
AXES = ("x", "y", "c")
VMEM_LIMIT = 56 * 1024 * 1024
MM_VMEM_BUDGET = 40 * 1024 * 1024
PACK_COLS = 1024
PACK_ROW_TILE = 16
HI = lax.Precision.HIGH


def _tile(n, cands):
    for c in cands:
        if n % c == 0:
            return c
    return n


def _params(sem=None):
    return pltpu.CompilerParams(dimension_semantics=sem, vmem_limit_bytes=VMEM_LIMIT)


def _dot(a, b, dims, prec=None):
    return lax.dot_general(a, b, (dims, ((), ())), preferred_element_type=F32, precision=prec)


NN = ((1,), (0,))
NT = ((1,), (1,))
TN = ((0,), (0,))


def _mm(a, b, *, ta=False, tb=False, out_dtype=F32, name):
    (k, m) = a.shape if ta else a.shape[::-1]
    (n, k2) = b.shape if tb else b.shape[::-1]
    assert k == k2, (a.shape, b.shape, ta, tb)
    tm = _tile(m, (1024, 512, 256, 128))
    tn = _tile(n, (1024, 512, 384, 256, 128))
    sa, sb, so = a.dtype.itemsize, b.dtype.itemsize, jnp.dtype(out_dtype).itemsize

    def fits(tk):
        return 2 * (tm * tk * sa + tk * tn * sb) + 2 * tm * tn * so + 4 * tm * tn <= MM_VMEM_BUDGET

    tk = next((c for c in (2816, 2048, 1536, 1408, 1152, 1024, 768, 512, 384, 256, 128) if k % c == 0 and fits(c)), k)
    nk = k // tk
    dims = (((0,) if ta else (1,)), ((1,) if tb else (0,)))

    def body(a_ref, b_ref, o_ref, *acc):
        part = _dot(a_ref[...].astype(BF16), b_ref[...].astype(BF16), dims)
        if nk == 1:
            o_ref[...] = part.astype(o_ref.dtype)
            return
        (acc_ref,) = acc
        kk = pl.program_id(2)

        @pl.when(kk == 0)
        def _():
            acc_ref[...] = part

        @pl.when(kk > 0)
        def _():
            acc_ref[...] += part

        @pl.when(kk == nk - 1)
        def _():
            o_ref[...] = acc_ref[...].astype(o_ref.dtype)

    a_spec = pl.BlockSpec((tk, tm), lambda i, j, kk: (kk, i)) if ta else pl.BlockSpec((tm, tk), lambda i, j, kk: (i, kk))
    b_spec = pl.BlockSpec((tn, tk), lambda i, j, kk: (j, kk)) if tb else pl.BlockSpec((tk, tn), lambda i, j, kk: (kk, j))
    return pl.pallas_call(
        body,
        name=name,
        grid=(m // tm, n // tn, nk),
        in_specs=[a_spec, b_spec],
        out_specs=pl.BlockSpec((tm, tn), lambda i, j, kk: (i, j)),
        out_shape=jax.ShapeDtypeStruct((m, n), out_dtype),
        scratch_shapes=[pltpu.VMEM((tm, tn), F32)] if nk > 1 else [],
        compiler_params=_params(("parallel", "parallel", "arbitrary")),
    )(a, b)


def _row_spec(spec, tq):
    _, w, cb = spec
    return pl.BlockSpec((tq, w), lambda i, j: (i, cb + j))


def _full_spec(p):
    return pl.BlockSpec(p.shape, lambda i, j: (0,) * p.ndim)


def _rowwise(fn, rows, pars, outs, *, name, tq, ncol=1):
    t = rows[0][0].shape[0]
    tq = _tile(t, (tq, 128, 64, 32, 16, 8))
    nr, npar = len(rows), len(pars)

    def body(*refs):
        vals = [r[...].astype(F32) for r in refs[: nr + npar]]
        res = fn(*vals)
        for o, r in zip(refs[nr + npar :], res):
            o[...] = r.astype(o.dtype)

    res = pl.pallas_call(
        body,
        name=name,
        grid=(t // tq, ncol),
        in_specs=[_row_spec(s, tq) for s in rows] + [_full_spec(p) for p in pars],
        out_specs=[pl.BlockSpec((tq, w), lambda i, j: (i, j)) for (_, w, _) in outs],
        out_shape=[jax.ShapeDtypeStruct((t, wt), dt) for (wt, _, dt) in outs],
        compiler_params=_params(("parallel", "parallel")),
    )(*[s[0] for s in rows], *pars)
    return res


def _rowwise_bwd(fn, rows, pars, cots, *, need, drow, name, tq, ncol=1):
    t = rows[0][0].shape[0]
    tq = _tile(t, (tq, 128, 64, 32, 16, 8))
    nr, npar = len(rows), len(pars)
    assert ncol == 1 or npar == 0
    flat_cots = [s for c in cots for s in c]
    ncot = len(flat_cots)
    want = [k for k in range(nr) if need[k]]

    def body(*refs):
        vals = [r[...].astype(F32) for r in refs[: nr + npar]]
        cref = refs[nr + npar : nr + npar + ncot]
        oref = refs[nr + npar + ncot :]
        cvals, pos = [], 0
        for c in cots:
            acc = cref[pos][...].astype(F32)
            for q in range(1, len(c)):
                acc = acc + cref[pos + q][...].astype(F32)
            cvals.append(acc)
            pos += len(c)

        def closed(*diff):
            full = list(vals)
            for k, dv in zip(want + list(range(nr, nr + npar)), diff):
                full[k] = dv
            return tuple(fn(*full))

        diff_in = [vals[k] for k in want] + vals[nr:]
        _, vjp = jax.vjp(closed, *diff_in)
        grads = vjp(tuple(cvals))
        for q in range(len(want)):
            oref[q][...] = grads[q].astype(oref[q].dtype)
        if npar:
            first = pl.program_id(0) == 0

            @pl.when(first)
            def _():
                for q in range(npar):
                    oref[len(want) + q][...] = jnp.zeros_like(oref[len(want) + q])

            for q in range(npar):
                oref[len(want) + q][...] += grads[len(want) + q]

    out_specs = [pl.BlockSpec((tq, rows[k][1]), lambda i, j, cb=rows[k][2]: (i, j)) for k in want]
    out_specs += [_full_spec(p) for p in pars]
    out_shape = [jax.ShapeDtypeStruct((t, wt), dt) for (wt, dt) in drow]
    out_shape += [jax.ShapeDtypeStruct(p.shape, F32) for p in pars]
    return pl.pallas_call(
        body,
        name=name,
        grid=(t // tq, ncol),
        in_specs=[_row_spec(s, tq) for s in rows] + [_full_spec(p) for p in pars] + [_row_spec(s, tq) for s in flat_cots],
        out_specs=out_specs,
        out_shape=out_shape,
        compiler_params=_params(("arbitrary", "arbitrary") if npar else ("parallel", "parallel")),
    )(*[s[0] for s in rows], *pars, *[s[0] for s in flat_cots])


def _whole(a, w=None):
    return (a, a.shape[1] if w is None else w, 0)


def _rms_fn(ql, kvl, gq, gk):
    def one(x, g):
        return x * lax.rsqrt(jnp.mean(x * x, -1, keepdims=True) + 1e-6) * g

    return one(ql, gq), one(kvl, gk)


def _layernorm(s, g, b):
    mu = jnp.mean(s, -1, keepdims=True)
    d = s - mu
    var = jnp.mean(d * d, -1, keepdims=True)
    return d * lax.rsqrt(var + 1e-5) * g + b


def _ln_mix_fn(alpha):
    def fn(h, mix, g, b):
        return (_layernorm(alpha * h + mix, g, b),)

    return fn


def _ln_glu_fn(alpha):
    def fn(h, val, gate, g, b):
        return (_layernorm(alpha * h + val * jax.nn.sigmoid(gate), g, b),)

    return fn


def _ln_ffn_fn(alpha):
    def fn(h, f, pw, gate, g, b):
        return (_layernorm(alpha * h + f + pw * jax.nn.sigmoid(gate), g, b),)

    return fn


def _swiglu_fn(g, u):
    return (jax.nn.silu(g) * u,)


def _gelu_fn(ys, h, d):
    return (jax.nn.gelu(ys + d * h),)


def _merge_fn(o1, o2, o3, l1, l2, l3):
    m = jnp.maximum(jnp.maximum(l1, l2), l3)
    e1, e2, e3 = jnp.exp(l1 - m), jnp.exp(l2 - m), jnp.exp(l3 - m)
    return ((e1 * o1 + e2 * o2 + e3 * o3) / (e1 + e2 + e3),)


def _swap_halves(t):
    w = t.shape[1]
    lane = lax.broadcasted_iota(jnp.int32, t.shape, 1) % LANE
    up = jnp.where(lane < ROPE_DIM, pltpu.roll(t, ROPE_DIM // 2, 1), 0.0)
    return jnp.where(lane < ROPE_DIM // 2, pltpu.roll(t, w - ROPE_DIM // 2, 1), up)


def _rope_fwd_fn(nh):
    def fn(qpe, kpe, cos, sin):
        cq, sq = jnp.tile(cos, (1, nh)), jnp.tile(sin, (1, nh))
        return qpe * cq + _swap_halves(qpe) * sq, kpe * cos + _swap_halves(kpe) * sin

    return fn


def _rope_bwd_fn(nh):
    def fn(dq, dk_heads, cos, sin):
        cq, sq = jnp.tile(cos, (1, nh)), jnp.tile(sin, (1, nh))
        dk = dk_heads[:, :LANE]
        for h in range(1, nh):
            dk = dk + dk_heads[:, h * LANE : (h + 1) * LANE]
        return dq * cq + _swap_halves(dq * sq), dk * cos + _swap_halves(dk * sin)

    return fn


def _rope_tables(positions, name):
    t = positions.shape[0]
    tq = _tile(t, (512, 128, 8))
    half = ROPE_DIM // 2

    def body(p_ref, c_ref, s_ref):
        lane = lax.broadcasted_iota(jnp.int32, (tq, LANE), 1)
        idx = (lane % half).astype(F32)
        inv_freq = jnp.exp(idx * (-math.log(ROPE_THETA) / half))
        ang = p_ref[...].astype(F32) * inv_freq
        live = lane < ROPE_DIM
        c_ref[...] = jnp.where(live, jnp.cos(ang), 0.0)
        s_ref[...] = jnp.where(live, jnp.where(lane < half, -jnp.sin(ang), jnp.sin(ang)), 0.0)

    return pl.pallas_call(
        body,
        name=name,
        grid=(t // tq,),
        in_specs=[pl.BlockSpec((tq, 1), lambda i: (i, 0))],
        out_specs=[pl.BlockSpec((tq, LANE), lambda i: (i, 0))] * 2,
        out_shape=[jax.ShapeDtypeStruct((t, LANE), F32)] * 2,
        compiler_params=_params(("parallel",)),
    )(positions)


def _loss_kernel(y, target, name):
    t, d = y.shape
    tq = _tile(t, (256, 128, 8))

    def body(y_ref, t_ref, dy_ref, l_ref):
        @pl.when(pl.program_id(0) == 0)
        def _():
            l_ref[...] = jnp.zeros_like(l_ref)

        e = y_ref[...] - t_ref[...]
        dy_ref[...] = e * (1.0 / d)
        l_ref[...] += jnp.sum(e * e) * (0.5 / d)

    return pl.pallas_call(
        body,
        name=name,
        grid=(t // tq,),
        in_specs=[pl.BlockSpec((tq, d), lambda i: (i, 0))] * 2,
        out_specs=[pl.BlockSpec((tq, d), lambda i: (i, 0)), pl.BlockSpec((8, LANE), lambda i: (0, 0))],
        out_shape=[jax.ShapeDtypeStruct((t, d), F32), jax.ShapeDtypeStruct((8, LANE), F32)],
        compiler_params=_params(("arbitrary",)),
    )(y, target)


def _addn(arrs, name):
    w = arrs[0].shape[1]
    return _rowwise(lambda *v: (functools.reduce(lambda p, q: p + q, v),), [_whole(a) for a in arrs], [], [(w, w, F32)],
                    name=name, tq=256)[0]


def _mla_tiles(t):
    tq = _tile(t, (512, 256, 128))
    return tq, t // tq


def _call_with_side(body, side, n_in, grid, *, name, in_specs, out_specs, out_shape):
    if side is None:
        return pl.pallas_call(body, name=name, grid=grid, in_specs=in_specs, out_specs=out_specs, out_shape=out_shape,
                              compiler_params=_params(("parallel", "parallel")))
    n_out, n = len(out_specs), len(side[1])

    def wrapped(*refs):
        ins, side_ins = refs[:n_in], refs[n_in : n_in + n]
        outs, side_outs = refs[n_in + n : n_in + n + n_out], refs[n_in + n + n_out : n_in + 2 * n + n_out]
        done = _run_side(side, side_ins, side_outs, refs[n_in + 2 * n + n_out :], grid[0], grid[1])
        body(*ins, *outs)
        done()

    call = pl.pallas_call(wrapped, name=name, grid=grid, in_specs=in_specs + [ANY_SPEC] * n,
                          out_specs=out_specs + [ANY_SPEC] * n, out_shape=out_shape + _side_out_shapes(side),
                          scratch_shapes=_comm_sems(n), compiler_params=_params(("arbitrary", "arbitrary")))

    def run(*args):
        res = call(*args, *side[1])
        return (*res[:n_out], list(res[n_out:]))

    return run


def _mla_fwd(qf, qpe, kvf, kpe, nh, name, side=None):
    t = qf.shape[0]
    tq, nq = _mla_tiles(t)
    scale = (NOPE_DIM + ROPE_DIM) ** -0.5

    def body(qn_ref, qp_ref, kn_ref, kp_ref, v_ref, o_ref, lse_ref):
        i = pl.program_id(1)
        qn = qn_ref[...].astype(BF16)
        qp = qp_ref[...]

        def step(j, carry, masked):
            m, l, acc = carry
            ks = pl.ds(pl.multiple_of(j * tq, tq), tq)
            s = (_dot(qn, kn_ref[ks, :], NT) + _dot(qp, kp_ref[ks, :], NT)) * scale
            if masked:
                row = lax.broadcasted_iota(jnp.int32, (tq, tq), 0)
                col = lax.broadcasted_iota(jnp.int32, (tq, tq), 1)
                s = jnp.where(col <= row, s, NEG)
            m_new = jnp.maximum(m, jnp.max(s, -1, keepdims=True))
            p = jnp.exp(s - m_new)
            a = jnp.exp(m - m_new)
            return m_new, a * l + jnp.sum(p, -1, keepdims=True), a * acc + _dot(p.astype(BF16), v_ref[ks, :], NN)

        init = (jnp.full((tq, 1), NEG, F32), jnp.zeros((tq, 1), F32), jnp.zeros((tq, V_DIM), F32))
        carry = lax.fori_loop(0, i, lambda j, c: step(j, c, False), init)
        m, l, acc = step(i, carry, True)
        o_ref[...] = (acc / l).astype(o_ref.dtype)
        lse_ref[...] = jnp.broadcast_to(m + jnp.log(l), (tq, LANE))

    blk = lambda h, i: (i, h)
    return _call_with_side(
        body, side, 5, (nh, nq),
        name=name,
        in_specs=[
            pl.BlockSpec((tq, LANE), blk),
            pl.BlockSpec((tq, LANE), blk),
            pl.BlockSpec((t, LANE), lambda h, i: (0, h)),
            pl.BlockSpec((t, LANE), lambda h, i: (0, 0)),
            pl.BlockSpec((t, LANE), lambda h, i: (0, nh + h)),
        ],
        out_specs=[pl.BlockSpec((tq, LANE), blk), pl.BlockSpec((tq, LANE), blk)],
        out_shape=[jax.ShapeDtypeStruct((t, nh * LANE), BF16), jax.ShapeDtypeStruct((t, nh * LANE), F32)],
    )(qf, qpe, kvf, kpe, kvf)


def _mla_bwd_q(qf, qpe, kvf, kpe, do, o, lse, nh, name, side=None):
    t = qf.shape[0]
    tq, nq = _mla_tiles(t)
    scale = (NOPE_DIM + ROPE_DIM) ** -0.5

    def body(qn_ref, qp_ref, kn_ref, kp_ref, v_ref, do_ref, o_ref, lse_ref, dqn_ref, dqp_ref, dl_ref):
        i = pl.program_id(1)
        qn = qn_ref[...].astype(BF16)
        qp = qp_ref[...]
        dof = do_ref[...].astype(F32)
        dob = dof.astype(BF16)
        delta = jnp.sum(dof * o_ref[...].astype(F32), -1, keepdims=True)
        lse1 = lse_ref[:, :1]

        def step(j, carry, masked):
            dqn, dqp = carry
            ks = pl.ds(pl.multiple_of(j * tq, tq), tq)
            kn, kp = kn_ref[ks, :], kp_ref[ks, :]
            s = (_dot(qn, kn, NT) + _dot(qp, kp, NT)) * scale
            p = jnp.exp(s - lse1)
            if masked:
                row = lax.broadcasted_iota(jnp.int32, (tq, tq), 0)
                col = lax.broadcasted_iota(jnp.int32, (tq, tq), 1)
                p = jnp.where(col <= row, p, 0.0)
            dp = _dot(dob, v_ref[ks, :], NT)
            ds = (p * (dp - delta) * scale).astype(BF16)
            return dqn + _dot(ds, kn, NN), dqp + _dot(ds, kp, NN)

        init = (jnp.zeros((tq, LANE), F32), jnp.zeros((tq, LANE), F32))
        carry = lax.fori_loop(0, i, lambda j, c: step(j, c, False), init)
        dqn, dqp = step(i, carry, True)
        dqn_ref[...] = dqn
        dqp_ref[...] = dqp
        dl_ref[...] = jnp.broadcast_to(delta, (tq, LANE))

    blk = lambda h, i: (i, h)
    bs = pl.BlockSpec((tq, LANE), blk)
    return _call_with_side(
        body, side, 8, (nh, nq),
        name=name,
        in_specs=[
            bs,
            bs,
            pl.BlockSpec((t, LANE), lambda h, i: (0, h)),
            pl.BlockSpec((t, LANE), lambda h, i: (0, 0)),
            pl.BlockSpec((t, LANE), lambda h, i: (0, nh + h)),
            bs,
            bs,
            bs,
        ],
        out_specs=[bs, bs, bs],
        out_shape=[jax.ShapeDtypeStruct((t, nh * LANE), F32)] * 3,
    )(qf, qpe, kvf, kpe, kvf, do, o, lse)


def _mla_bwd_kv(qn16, qpe, kvf, kpe, do16, lse_row, delta_row, nh, name, side=None):
    t = qn16.shape[0]
    tq, nq = _mla_tiles(t)
    scale = (NOPE_DIM + ROPE_DIM) ** -0.5

    def body(kn_ref, kp_ref, v_ref, qn_ref, qp_ref, do_ref, lse_ref, dl_ref, dkn_ref, dkp_ref, dv_ref):
        j = pl.program_id(1)
        kn, kp, v = kn_ref[...], kp_ref[...], v_ref[...]

        def step(i, carry, masked):
            dkn, dkp, dv = carry
            qs = pl.ds(pl.multiple_of(i * tq, tq), tq)
            qn, qp, dob = qn_ref[qs, :], qp_ref[qs, :], do_ref[qs, :]
            st = (_dot(kn, qn, NT) + _dot(kp, qp, NT)) * scale
            pt = jnp.exp(st - lse_ref[0, :, qs])
            if masked:
                key = lax.broadcasted_iota(jnp.int32, (tq, tq), 0)
                qry = lax.broadcasted_iota(jnp.int32, (tq, tq), 1)
                pt = jnp.where(key <= qry, pt, 0.0)
            dv = dv + _dot(pt.astype(BF16), dob, NN)
            dpt = _dot(v, dob, NT)
            dst = (pt * (dpt - dl_ref[0, :, qs]) * scale).astype(BF16)
            return dkn + _dot(dst, qn, NN), dkp + _dot(dst, qp, NN), dv

        z = jnp.zeros((tq, LANE), F32)
        carry = step(j, (z, z, z), True)
        dkn, dkp, dv = lax.fori_loop(j + 1, nq, lambda i, c: step(i, c, False), carry)
        dkn_ref[...] = dkn
        dkp_ref[...] = dkp
        dv_ref[...] = dv

    blk = pl.BlockSpec((tq, LANE), lambda h, j: (j, h))
    res = lambda f: pl.BlockSpec((t, LANE), f)
    row = pl.BlockSpec((1, 1, t), lambda h, j: (h, 0, 0))
    return _call_with_side(
        body, side, 8, (nh, nq),
        name=name,
        in_specs=[
            blk,
            pl.BlockSpec((tq, LANE), lambda h, j: (j, 0)),
            pl.BlockSpec((tq, LANE), lambda h, j: (j, nh + h)),
            res(lambda h, j: (0, h)),
            res(lambda h, j: (0, h)),
            res(lambda h, j: (0, h)),
            row,
            row,
        ],
        out_specs=[blk, blk, blk],
        out_shape=[jax.ShapeDtypeStruct((t, nh * LANE), F32)] * 3,
    )(kvf, kpe, kvf, qn16, qpe, do16, lse_row, delta_row)


def _alibi_slopes(n):
    return [float(2.0 ** (-8.0 * i / n)) for i in range(1, n + 1)]


def _window_mask(has_prev):
    qi = lax.broadcasted_iota(jnp.int32, (BLK, 2 * BLK), 0)
    ki = lax.broadcasted_iota(jnp.int32, (BLK, 2 * BLK), 1)
    dist = qi + BLK - ki
    valid = (dist >= 0) & (dist <= BLK) & ((ki >= BLK) | has_prev)
    return valid, dist.astype(F32)


def _band_fwd(zd, nh, dil, name):
    t = zd.shape[0]
    nbc = t // BLK // dil
    scale = DIL_HEAD_DIM ** -0.5
    slopes = _alibi_slopes(nh)
    dw = nh * LANE

    def body(q_ref, kc_ref, kp_ref, vc_ref, vp_ref, o_ref, l_ref):
        mask, dist = _window_mask(pl.program_id(1) > 0)
        for h in range(nh):
            sl = slice(h * LANE, (h + 1) * LANE)
            keys = jnp.concatenate([kp_ref[:, sl], kc_ref[:, sl]], axis=0)
            vals = jnp.concatenate([vp_ref[:, sl], vc_ref[:, sl]], axis=0)
            s = jnp.where(mask, _dot(q_ref[:, sl], keys, NT) * scale - (slopes[h] * dil) * dist, NEG)
            m = jnp.max(s, -1, keepdims=True)
            e = jnp.exp(s - m)
            l = jnp.sum(e, -1, keepdims=True)
            o_ref[:, sl] = _dot((e * (1.0 / l)).astype(BF16), vals, NN)
            l_ref[:, sl] = jnp.broadcast_to(m + jnp.log(l), (BLK, LANE))

    prev = lambda i: jnp.maximum(i - 1, 0)
    spec = lambda f: pl.BlockSpec((BLK, dw), f)
    return pl.pallas_call(
        body,
        name=name,
        grid=(dil, nbc),
        in_specs=[
            spec(lambda r, i: (r * nbc + i, 0)),
            spec(lambda r, i: (r * nbc + i, 1)),
            spec(lambda r, i: (r * nbc + prev(i), 1)),
            spec(lambda r, i: (r * nbc + i, 2)),
            spec(lambda r, i: (r * nbc + prev(i), 2)),
        ],
        out_specs=[spec(lambda r, i: (r * nbc + i, 0))] * 2,
        out_shape=[jax.ShapeDtypeStruct((t, dw), F32)] * 2,
        compiler_params=_params(("parallel", "parallel")),
    )(zd, zd, zd, zd, zd)


def _band_bwd(zd, o, lse, do, dl, nh, dil, name):
    t = zd.shape[0]
    nbc = t // BLK // dil
    scale = DIL_HEAD_DIM ** -0.5
    slopes = _alibi_slopes(nh)
    dw = nh * LANE

    def body(q_ref, k_ref, v_ref, kp_ref, vp_ref, qn_ref, o_ref, l_ref, do_ref, dl_ref, on_ref, ln_ref, don_ref, dln_ref,
             dq_ref, dk_ref, dv_ref):
        mask, dist = _window_mask(pl.program_id(1) > 0)
        mask_n, dist_n = _window_mask(pl.program_id(1) + 1 < nbc)
        mask_n, dist_n = mask_n[:, :BLK], dist_n[:, :BLK]
        for h in range(nh):
            sl = slice(h * LANE, (h + 1) * LANE)
            bias = slopes[h] * dil
            q, k, v, qn = q_ref[:, sl], k_ref[:, sl], v_ref[:, sl], qn_ref[:, sl]
            keys = jnp.concatenate([kp_ref[:, sl], k], axis=0)
            vals = jnp.concatenate([vp_ref[:, sl], v], axis=0)
            dof, donf = do_ref[:, sl], don_ref[:, sl]
            dob, donb = dof.astype(BF16), donf.astype(BF16)
            lse1, lsen1 = l_ref[:, sl][:, :1], ln_ref[:, sl][:, :1]
            adj = jnp.sum(dl_ref[:, sl] - dof * o_ref[:, sl], -1, keepdims=True)
            adjn = jnp.sum(dln_ref[:, sl] - donf * on_ref[:, sl], -1, keepdims=True)
            p = jnp.where(mask, jnp.exp(_dot(q, keys, NT) * scale - bias * dist - lse1), 0.0)
            ds = (p * (_dot(dob, vals, NT) + adj)).astype(BF16)
            dq_ref[:, sl] = _dot(ds, keys, NN) * scale
            pn = jnp.where(mask_n, jnp.exp(_dot(qn, k, NT) * scale - bias * dist_n - lsen1), 0.0)
            dsn = (pn * (_dot(donb, v, NT) + adjn)).astype(BF16)
            both_q = jnp.concatenate([q, qn], axis=0)
            dk_ref[:, sl] = _dot(jnp.concatenate([ds[:, BLK:], dsn], axis=0), both_q, TN) * scale
            dv_ref[:, sl] = _dot(jnp.concatenate([p[:, BLK:].astype(BF16), pn.astype(BF16)], axis=0),
                                 jnp.concatenate([dob, donb], axis=0), TN)

    prev = lambda i: jnp.maximum(i - 1, 0)
    nxt = lambda i: jnp.minimum(i + 1, nbc - 1)
    spec = lambda f: pl.BlockSpec((BLK, dw), f)
    cur, nx = spec(lambda r, i: (r * nbc + i, 0)), spec(lambda r, i: (r * nbc + nxt(i), 0))
    return pl.pallas_call(
        body,
        name=name,
        grid=(dil, nbc),
        in_specs=[cur, spec(lambda r, i: (r * nbc + i, 1)), spec(lambda r, i: (r * nbc + i, 2)),
                  spec(lambda r, i: (r * nbc + prev(i), 1)), spec(lambda r, i: (r * nbc + prev(i), 2)),
                  nx, cur, cur, cur, cur, nx, nx, nx, nx],
        out_specs=[cur] * 3,
        out_shape=[jax.ShapeDtypeStruct((t, dw), F32)] * 3,
        compiler_params=_params(("parallel", "parallel")),
    )(zd, zd, zd, zd, zd, zd, o, lse, do, dl, o, lse, do, dl)


def _s5_ops(a_re, a_im, ldt, bt_re, bt_im, c_re, c_im):
    L, g, p = S5_CHUNK, S5_GROUP, S5_STATE
    dt = jnp.exp(ldt)
    lr, li = a_re * dt, a_im * dt
    er = jnp.exp(lr)
    lam_re, lam_im = er * jnp.cos(li), er * jnp.sin(li)
    nr, ni = lam_re - 1.0, lam_im
    den = a_re * a_re + a_im * a_im
    fr, fi = (nr * a_re + ni * a_im) / den, (ni * a_re - nr * a_im) / den
    bb_re, bb_im = fr * bt_re - fi * bt_im, fr * bt_im + fi * bt_re

    def power(tau):
        mag = jnp.exp(tau * lr)
        return mag * jnp.cos(tau * li), mag * jnp.sin(tau * li)

    step = lax.broadcasted_iota(jnp.int32, (L, 1), 0).astype(F32)

    def outer(pr, pi, mr, mi):
        re = pr[:, None, :] * mr[None] - pi[:, None, :] * mi[None]
        im = pr[:, None, :] * mi[None] + pi[:, None, :] * mr[None]
        return re.reshape(L * g, p), im.reshape(L * g, p)

    half = float(L // 2)
    cp_re, cp_im = outer(*power(step - half), c_re, c_im)
    pb_re, pb_im = outer(*power(half - step), bb_re, bb_im)
    toep = _dot(cp_re, pb_re, NT, HI) - _dot(cp_im, pb_im, NT, HI)
    trow = lax.broadcasted_iota(jnp.int32, (L * g, L * g), 0) // g
    scol = lax.broadcasted_iota(jnp.int32, (L * g, L * g), 1) // g
    toep = jnp.where(trow >= scol, toep, 0.0)
    et_re, et_im = outer(*power(float(L - 1) - step), bb_re, bb_im)
    f_re, f_im = outer(*power(step + 1.0), c_re, c_im)
    big_re, big_im = power(jnp.full((1, 1), float(L), F32))
    return toep, et_re, et_im, f_re, f_im, big_re, big_im


def _s5_y(ops, u, sp_re, sp_im):
    toep, _, _, f_re, f_im, _, _ = ops
    return _dot(u, toep, NT, HI) + _dot(sp_re, f_re, NT, HI) - _dot(sp_im, f_im, NT, HI)


def _s5_group_specs(ng):
    vec = pl.BlockSpec((1, 1, S5_STATE), lambda g: (g, 0, 0))
    one = pl.BlockSpec((1, 1, 1), lambda g: (g, 0, 0))
    mat = pl.BlockSpec((1, S5_GROUP, S5_STATE), lambda g: (g, 0, 0))
    return [vec, vec, one, mat, mat, mat, mat]


def _s5_load(refs):
    return [r[0] for r in refs]


def _s5_local(prm, u, name):
    ng, n, w = u.shape
    p = S5_STATE

    def body(*refs):
        ops = _s5_ops(*_s5_load(refs[:7]))
        uu = refs[7][0]
        refs[8][0] = _dot(uu, ops[1], NN, HI)
        refs[9][0] = _dot(uu, ops[2], NN, HI)
        refs[10][0] = ops[5]
        refs[11][0] = ops[6]

    blk = lambda a, b: pl.BlockSpec((1, a, b), lambda g: (g, 0, 0))
    return pl.pallas_call(
        body,
        name=name,
        grid=(ng,),
        in_specs=_s5_group_specs(ng) + [blk(n, w)],
        out_specs=[blk(n, p), blk(n, p), blk(1, p), blk(1, p)],
        out_shape=[jax.ShapeDtypeStruct((ng, n, p), F32)] * 2 + [jax.ShapeDtypeStruct((ng, 1, p), F32)] * 2,
        compiler_params=_params(("parallel",)),
    )(*prm, u)


def _s5_carry(e_re, e_im, lam_re, lam_im, name):
    n, w = e_re.shape
    cw = _tile(w, (1024, 512, 256, 128))

    def body(er_ref, ei_ref, lr_ref, li_ref, sr_ref, si_ref):
        lr, li = lr_ref[...], li_ref[...]

        def step(k, carry):
            sr, si = carry
            row = pl.ds(k, 1)
            sr_ref[row, :] = sr
            si_ref[row, :] = si
            return lr * sr - li * si + er_ref[row, :], li * sr + lr * si + ei_ref[row, :]

        z = jnp.zeros((1, cw), F32)
        lax.fori_loop(0, n, step, (z, z))

    col = pl.BlockSpec((n, cw), lambda j: (0, j))
    one = pl.BlockSpec((1, cw), lambda j: (0, j))
    return pl.pallas_call(
        body,
        name=name,
        grid=(w // cw,),
        in_specs=[col, col, one, one],
        out_specs=[col, col],
        out_shape=[jax.ShapeDtypeStruct((n, w), F32)] * 2,
        compiler_params=_params(("parallel",)),
    )(e_re, e_im, lam_re, lam_im)


def _s5_carry_bwd(dsp_re, dsp_im, sp_re, sp_im, lam_re, lam_im, name):
    n, w = dsp_re.shape
    cw = _tile(w, (1024, 512, 256, 128))

    def body(dr_ref, di_ref, sr_ref, si_ref, lr_ref, li_ref, gr_ref, gi_ref, dlr_ref, dli_ref):
        lr, li = lr_ref[...], li_ref[...]

        def step(q, carry):
            gr_next, gi_next, dr_next, di_next, alr, ali = carry
            k = n - 1 - q
            row = pl.ds(k, 1)
            gr = dr_next + lr * gr_next + li * gi_next
            gi = di_next - li * gr_next + lr * gi_next
            gr_ref[row, :] = gr
            gi_ref[row, :] = gi
            sr, si = sr_ref[row, :], si_ref[row, :]
            return gr, gi, dr_ref[row, :], di_ref[row, :], alr + gr * sr + gi * si, ali + gi * sr - gr * si

        z = jnp.zeros((1, cw), F32)
        out = lax.fori_loop(0, n, step, (z, z, z, z, z, z))
        dlr_ref[...] = out[4]
        dli_ref[...] = out[5]

    col = pl.BlockSpec((n, cw), lambda j: (0, j))
    one = pl.BlockSpec((1, cw), lambda j: (0, j))
    return pl.pallas_call(
        body,
        name=name,
        grid=(w // cw,),
        in_specs=[col, col, col, col, one, one],
        out_specs=[col, col, one, one],
        out_shape=[jax.ShapeDtypeStruct((n, w), F32)] * 2 + [jax.ShapeDtypeStruct((1, w), F32)] * 2,
        compiler_params=_params(("parallel",)),
    )(dsp_re, dsp_im, sp_re, sp_im, lam_re, lam_im)


def _s5_out(prm, u, sp_re, sp_im, name):
    ng, n, w = u.shape
    p = S5_STATE

    def body(*refs):
        ops = _s5_ops(*_s5_load(refs[:7]))
        refs[10][0] = _s5_y(ops, refs[7][0], refs[8][0], refs[9][0])

    blk = lambda a, b: pl.BlockSpec((1, a, b), lambda g: (g, 0, 0))
    return pl.pallas_call(
        body,
        name=name,
        grid=(ng,),
        in_specs=_s5_group_specs(ng) + [blk(n, w), blk(n, p), blk(n, p)],
        out_specs=blk(n, w),
        out_shape=jax.ShapeDtypeStruct((ng, n, w), F32),
        compiler_params=_params(("parallel",)),
    )(*prm, u, sp_re, sp_im)


def _s5_bwd_state(prm, dy, name):
    ng, n, w = dy.shape
    p = S5_STATE

    def body(*refs):
        ops = _s5_ops(*_s5_load(refs[:7]))
        d = refs[7][0]
        refs[8][0] = _dot(d, ops[3], NN, HI)
        refs[9][0] = -_dot(d, ops[4], NN, HI)

    blk = lambda a, b: pl.BlockSpec((1, a, b), lambda g: (g, 0, 0))
    return pl.pallas_call(
        body,
        name=name,
        grid=(ng,),
        in_specs=_s5_group_specs(ng) + [blk(n, w)],
        out_specs=[blk(n, p), blk(n, p)],
        out_shape=[jax.ShapeDtypeStruct((ng, n, p), F32)] * 2,
        compiler_params=_params(("parallel",)),
    )(*prm, dy)


def _s5_bwd_main(prm, u, sp_re, sp_im, dy, g_re, g_im, dlam_re, dlam_im, name):
    ng, n, w = u.shape
    p = S5_STATE

    def body(*refs):
        prm_v = _s5_load(refs[:7])
        uu, sr, si, d, gr, gi, dlr, dli = [r[0] for r in refs[7:15]]

        def phi(*args):
            ops = _s5_ops(*args[:7])
            y = _s5_y(ops, args[7], sr, si)
            e_re, e_im = _dot(args[7], ops[1], NN, HI), _dot(args[7], ops[2], NN, HI)
            return (jnp.sum(d * y) + jnp.sum(gr * e_re) + jnp.sum(gi * e_im)
                    + jnp.sum(dlr * ops[5]) + jnp.sum(dli * ops[6]))

        grads = jax.grad(phi, argnums=tuple(range(8)))(*prm_v, uu)
        for q in range(8):
            refs[15 + q][0] = grads[q]

    blk = lambda a, b: pl.BlockSpec((1, a, b), lambda g: (g, 0, 0))
    prm_specs = _s5_group_specs(ng)
    return pl.pallas_call(
        body,
        name=name,
        grid=(ng,),
        in_specs=prm_specs + [blk(n, w), blk(n, p), blk(n, p), blk(n, w), blk(n, p), blk(n, p), blk(1, p), blk(1, p)],
        out_specs=prm_specs + [blk(n, w)],
        out_shape=[jax.ShapeDtypeStruct(a.shape, F32) for a in prm] + [jax.ShapeDtypeStruct((ng, n, w), F32)],
        compiler_params=_params(("parallel",)),
    )(*prm, u, sp_re, sp_im, dy, g_re, g_im, dlam_re, dlam_im)


SLAB_GROUPS = LANE // S5_GROUP
CHUNK_TILES = S5_CHUNK * S5_GROUP // LANE


def _segment_transpose(tiles):
    seg = lax.broadcasted_iota(jnp.int32, (1, LANE), 1) // S5_GROUP
    tiles = list(tiles)
    stride = 1
    while stride < SLAB_GROUPS:
        upper = (seg & stride) != 0
        shift = S5_GROUP * stride
        new = list(tiles)
        for i in range(SLAB_GROUPS):
            if i & stride:
                continue
            j = i + stride
            new[i] = jnp.where(upper, pltpu.roll(tiles[j], shift, 1), tiles[i])
            new[j] = jnp.where(upper, tiles[j], pltpu.roll(tiles[i], LANE - shift, 1))
        tiles = new
        stride *= 2
    return tiles


def _to_groups(h, name):
    t, d = h.shape
    n = t // S5_CHUNK

    def body(x_ref, o_ref):
        for a in range(CHUNK_TILES):
            rows = [x_ref[pl.ds(SLAB_GROUPS * a + b, n, stride=S5_CHUNK), :] for b in range(SLAB_GROUPS)]
            for g, tile in enumerate(_segment_transpose(rows)):
                o_ref[g, :, a * LANE : (a + 1) * LANE] = tile

    return pl.pallas_call(
        body,
        name=name,
        grid=(d // LANE,),
        in_specs=[pl.BlockSpec((t, LANE), lambda q: (0, q))],
        out_specs=pl.BlockSpec((SLAB_GROUPS, n, S5_CHUNK * S5_GROUP), lambda q: (q, 0, 0)),
        out_shape=jax.ShapeDtypeStruct((d // S5_GROUP, n, S5_CHUNK * S5_GROUP), F32),
        compiler_params=_params(("parallel",)),
    )(h)


def _from_groups(y, name):
    ng, n, w = y.shape
    t, d = n * S5_CHUNK, ng * S5_GROUP

    def body(y_ref, x_ref):
        for a in range(CHUNK_TILES):
            tiles = [y_ref[g, :, a * LANE : (a + 1) * LANE] for g in range(SLAB_GROUPS)]
            for b, row in enumerate(_segment_transpose(tiles)):
                x_ref[pl.ds(SLAB_GROUPS * a + b, n, stride=S5_CHUNK), :] = row

    return pl.pallas_call(
        body,
        name=name,
        grid=(d // LANE,),
        in_specs=[pl.BlockSpec((SLAB_GROUPS, n, w), lambda q: (q, 0, 0))],
        out_specs=pl.BlockSpec((t, LANE), lambda q: (0, q)),
        out_shape=jax.ShapeDtypeStruct((t, d), F32),
        compiler_params=_params(("parallel",)),
    )(y)


def _states_to_cols(e):
    ng, n, p = e.shape
    return e.transpose(1, 0, 2).reshape(n, ng * p)


def _cols_to_states(s, ng):
    n = s.shape[0]
    return s.reshape(n, ng, S5_STATE).transpose(1, 0, 2)


def _me():
    return lax.axis_index("x"), lax.axis_index("y"), lax.axis_index("c")


def _flip(v, bit):
    return 1 - v if bit else v


COPIES = N_DEV - 1
ANY_SPEC = pl.BlockSpec(memory_space=pl.ANY)


def _comm_sems(n):
    return [pltpu.SemaphoreType.DMA((COPIES * n,)), pltpu.SemaphoreType.DMA((COPIES * n,)), pltpu.SemaphoreType.DMA((n,))]


def _gather_phases(x_refs, out_refs, send_sems, recv_sems, local_sems):
    mx, my, mc = _me()
    me, sibling = (mx, my, mc), (mx, my, 1 - mc)
    chips = [(1 - mx, my), (mx, 1 - my), (1 - mx, 1 - my)]
    arrays = range(len(x_refs))

    def slot(a, px, py, pc):
        return out_refs[a].at[4 * px + 2 * py + pc]

    def copy(a, k, block, to, from_input=False):
        return pltpu.make_async_remote_copy(
            src_ref=x_refs[a] if from_input else slot(a, *block),
            dst_ref=slot(a, *block),
            send_sem=send_sems.at[COPIES * a + k],
            recv_sem=recv_sems.at[COPIES * a + k],
            device_id=to,
            device_id_type=pl.DeviceIdType.MESH,
        )

    mine = [pltpu.make_async_copy(x_refs[a], slot(a, *me), local_sems.at[a]) for a in arrays]
    first = [copy(a, 0, me, sibling, True) for a in arrays]
    first += [copy(a, 1 + j, me, (*chip, mc), True) for a in arrays for j, chip in enumerate(chips)]
    passed = {(a, j): copy(a, 4 + j, (*chip, mc), sibling) for a in arrays for j, chip in enumerate(chips)}

    def start():
        for cp in mine + first:
            cp.start()

    def relay():
        for j, chip in enumerate(chips):
            for a in arrays:
                copy(a, 1 + j, (*chip, mc), me).wait_recv()
                passed[(a, j)].start()

    def finish():
        for a in arrays:
            copy(a, 0, sibling, me).wait_recv()
            for j, chip in enumerate(chips):
                copy(a, 4 + j, (*chip, 1 - mc), me).wait_recv()
        for cp in first + list(passed.values()):
            cp.wait_send()
        for cp in mine:
            cp.wait()

    return start, relay, finish


def _exchange_phases(g_refs, out_refs, send_sems, recv_sems, local_sems):
    mx, my, mc = _me()
    mine_idx = 4 * mx + 2 * my + mc
    arrays = range(len(g_refs))
    own = [pltpu.make_async_copy(g_refs[a].at[mine_idx], out_refs[a].at[mine_idx], local_sems.at[a]) for a in arrays]
    copies = []
    for m in range(1, N_DEV):
        px, py, pc = _flip(mx, m & 4), _flip(my, m & 2), _flip(mc, m & 1)
        for a in arrays:
            copies.append(
                pltpu.make_async_remote_copy(
                    src_ref=g_refs[a].at[4 * px + 2 * py + pc],
                    dst_ref=out_refs[a].at[mine_idx],
                    send_sem=send_sems.at[COPIES * a + m - 1],
                    recv_sem=recv_sems.at[COPIES * a + m - 1],
                    device_id=(px, py, pc),
                    device_id_type=pl.DeviceIdType.MESH,
                )
            )

    def start():
        for cp in own + copies:
            cp.start()

    def finish():
        for cp in copies:
            cp.wait_recv()
        for cp in copies:
            cp.wait_send()
        for cp in own:
            cp.wait()

    return start, None, finish


def _side_out_shapes(side):
    kind, arrs = side
    return [jax.ShapeDtypeStruct((N_DEV,) + a.shape if kind == "gather" else a.shape, a.dtype) for a in arrs]


def _run_side(side, x_refs, out_refs, sems, nsteps_major, nsteps_minor):
    start, relay, finish = (_gather_phases if side[0] == "gather" else _exchange_phases)(x_refs, out_refs, *sems)
    a, b = pl.program_id(0), pl.program_id(1)
    pl.when(jnp.logical_and(a == 0, b == 0))(start)
    if relay is not None:
        pl.when(jnp.logical_and(a == nsteps_major - 1, b == 0))(relay)
    return lambda: pl.when(jnp.logical_and(a == nsteps_major - 1, b == nsteps_minor - 1))(finish)


def _exchange_call(kind, arrs, name):
    n = len(arrs)

    def body(*refs):
        start, relay, finish = (_gather_phases if kind == "gather" else _exchange_phases)(refs[:n], refs[n : 2 * n], *refs[2 * n :])
        start()
        if relay is not None:
            relay()
        finish()

    return pl.pallas_call(body, name=name, in_specs=[ANY_SPEC] * n, out_specs=[ANY_SPEC] * n,
                          out_shape=_side_out_shapes((kind, arrs)), scratch_shapes=_comm_sems(n))(*arrs)


def _sum_slots(recv, name):
    _, r, c = recv.shape
    tr = _tile(r, (256, 128, 64, 32, 16, 8))

    def body(r_ref, o_ref):
        acc = r_ref[0].astype(F32)
        for k in range(1, N_DEV):
            acc = acc + r_ref[k].astype(F32)
        o_ref[...] = acc

    return pl.pallas_call(
        body,
        name=name,
        grid=(r // tr,),
        in_specs=[pl.BlockSpec((N_DEV, tr, c), lambda i: (0, i, 0))],
        out_specs=pl.BlockSpec((tr, c), lambda i: (i, 0)),
        out_shape=jax.ShapeDtypeStruct((r, c), F32),
        compiler_params=_params(("parallel",)),
    )(recv)


def _adamw(w, g, m, v, name):
    shape = w.shape
    c = shape[-1]
    as2d = lambda a: a.reshape(-1, c)
    w2, g2, m2, v2 = as2d(w), as2d(g), as2d(m), as2d(v)
    r = w2.shape[0]
    tr = _tile(r, (256, 128, 64, 32, 16, 8))
    c1 = 1.0 / (1.0 - ADAM_B1 ** ADAM_STEP)
    c2 = 1.0 / (1.0 - ADAM_B2 ** ADAM_STEP)

    def body(w_ref, g_ref, m_ref, v_ref, d_ref, mo_ref, vo_ref):
        gg = g_ref[...]
        mn = ADAM_B1 * m_ref[...] + (1.0 - ADAM_B1) * gg
        vn = ADAM_B2 * v_ref[...] + (1.0 - ADAM_B2) * (gg * gg)
        d_ref[...] = -ADAM_LR * ((mn * c1) / (jnp.sqrt(vn * c2) + ADAM_EPS) + ADAM_WD * w_ref[...])
        mo_ref[...] = mn
        vo_ref[...] = vn

    spec = pl.BlockSpec((tr, c), lambda i: (i, 0))
    d, mn, vn = pl.pallas_call(
        body,
        name=name,
        grid=(r // tr,),
        in_specs=[spec] * 4,
        out_specs=[spec] * 3,
        out_shape=[jax.ShapeDtypeStruct((r, c), F32)] * 3,
        compiler_params=_params(("parallel",)),
    )(w2, g2, m2, v2)
    return d.reshape(shape), mn.reshape(shape), vn.reshape(shape)


COL_SHARDED = ("attn_w_in", "mla_w_q_b", "mla_w_kv_b", "s5_d", "s5_w_glu", "ffn_w_in", "ple_w")
ROW_SHARDED = ("attn_w_out", "ffn_w_out", "ple_gate_w")
REPLICATED = ("mla_q_norm", "mla_kv_norm", "s5_a_re", "s5_a_im", "s5_log_dt", "s5_b_re", "s5_b_im", "s5_c_re",
              "s5_c_im", "ln1_g", "ln1_b", "ln2_g", "ln2_b")
ATTENTION_WEIGHTS = ("attn_w_in", "mla_w_q_b", "mla_w_kv_b", "attn_w_out")
BIG_WEIGHTS = ("attn_w_in", "mla_w_q_b", "mla_w_kv_b", "attn_w_out", "s5_w_glu", "ffn_w_in", "ffn_w_out", "ple_w", "ple_gate_w")
WEIGHTS = ("attn_w_in", "mla_q_norm", "mla_w_q_b", "mla_kv_norm", "mla_w_kv_b", "attn_w_out", "s5_a_re", "s5_a_im",
           "s5_log_dt", "s5_b_re", "s5_b_im", "s5_c_re", "s5_c_im", "s5_d", "s5_w_glu", "ln1_g", "ln1_b", "ffn_w_in",
           "ffn_w_out", "ple_w", "ple_gate_w", "ln2_g", "ln2_b")


def _unshard(name, gathered):
    if name in COL_SHARDED:
        full = jnp.moveaxis(gathered, 0, -2)
        return full.reshape(full.shape[:-2] + (full.shape[-2] * full.shape[-1],))
    full = jnp.moveaxis(gathered, 0, 1)
    return full.reshape((full.shape[0], full.shape[1] * full.shape[2]) + full.shape[3:])


def _by_owner(name, grad):
    if name in COL_SHARDED:
        g = grad.reshape(grad.shape[:-1] + (N_DEV, grad.shape[-1] // N_DEV))
        return jnp.moveaxis(g, -2, 0).reshape(N_DEV, -1)
    if name in ROW_SHARDED:
        g = grad.reshape((grad.shape[0], N_DEV, grad.shape[1] // N_DEV) + grad.shape[2:])
        return jnp.moveaxis(g, 1, 0).reshape(N_DEV, -1)
    return jnp.broadcast_to(grad.reshape(1, -1), (N_DEV, grad.size))


def _owner_rows(name, g):
    k, n = g.shape
    if name in COL_SHARDED:
        return g.reshape(k, N_DEV, n // N_DEV).transpose(1, 0, 2)
    assert name in ROW_SHARDED, name
    return g.reshape(N_DEV, k // N_DEV, n)


def _pack(pieces, axis):
    blocks, where, row = [], [], 0
    for p in pieces:
        n = p.shape[axis]
        pad = (-n) % (PACK_COLS * PACK_ROW_TILE)
        if pad:
            shape = list(p.shape)
            shape[axis] = pad
            p = jnp.concatenate([p, jnp.zeros(shape, p.dtype)], axis=axis)
        rows = (n + pad) // PACK_COLS
        blocks.append(p.reshape(p.shape[:axis] + (rows, PACK_COLS)))
        where.append((row, rows, n))
        row += rows
    return jnp.concatenate(blocks, axis=axis), where


def _unpack(buf, axis, where):
    row, rows, n = where
    part = lax.slice_in_dim(buf, row, row + rows, axis=axis)
    return lax.slice_in_dim(part.reshape(part.shape[:axis] + (rows * PACK_COLS,)), 0, n, axis=axis)


def _twice(fn):
    return lambda *a: fn(*a) * 2


def _ffn_block(h, hb, p_i, w_in, w_out, w_ple, w_pg, g, b, alpha, tag):
    t, d = h.shape
    hid = w_out.shape[0]
    cw = _tile(hid, (1408, 512, 256, 128))
    ncb = hid // cw
    gu = _mm(hb, w_in, out_dtype=BF16, name=f"ffn_in_{tag}")
    act = _rowwise(_swiglu_fn, [(gu, cw, 0), (gu, cw, ncb)], [], [(hid, cw, BF16)], name=f"swiglu_{tag}", tq=512, ncol=ncb)[0]
    f = _mm(act, w_out, name=f"ffn_out_{tag}")
    pw = _mm(p_i, w_ple, name=f"ple_{tag}")
    gate = _mm(hb, w_pg, name=f"ple_gate_{tag}")
    out, outb = _rowwise(_twice(_ln_ffn_fn(alpha)), [_whole(h), _whole(f), _whole(pw), _whole(gate)], [g, b],
                         [(d, d, F32), (d, d, BF16)], name=f"ln2_{tag}", tq=256)
    return out, outb, (h, hb, p_i, gu, act, f, pw, gate)


def _ffn_block_bwd(saved, dout, w_in, w_out, w_pg, g, b, alpha, tag):
    h, hb, p_i, gu, act, f, pw, gate = saved
    t, d = h.shape
    hid = w_out.shape[0]
    cw = _tile(hid, (1408, 512, 256, 128))
    ncb = hid // cw
    dh_a, df, dpw, dgate, dg, db = _rowwise_bwd(
        _ln_ffn_fn(alpha), [_whole(h), _whole(f), _whole(pw), _whole(gate)], [g, b], [dout],
        need=[True] * 4, drow=[(d, F32), (d, BF16), (d, BF16), (d, BF16)], name=f"ln2_bwd_{tag}", tq=128)
    dw_out = _mm(act, df, ta=True, out_dtype=BF16, name=f"ffn_out_dw_{tag}")
    dact = _mm(df, w_out, tb=True, out_dtype=BF16, name=f"ffn_out_dx_{tag}")
    dg_, du_ = _rowwise_bwd(_swiglu_fn, [(gu, cw, 0), (gu, cw, ncb)], [], [[(dact, cw, 0)]], need=[True, True],
                            drow=[(hid, BF16), (hid, BF16)], name=f"swiglu_bwd_{tag}", tq=256, ncol=ncb)
    dgu = jnp.concatenate([dg_, du_], axis=1)
    dw_in = _mm(hb, dgu, ta=True, out_dtype=BF16, name=f"ffn_in_dw_{tag}")
    dh_b = _mm(dgu, w_in, tb=True, name=f"ffn_in_dx_{tag}")
    dw_ple = _mm(p_i, dpw, ta=True, out_dtype=BF16, name=f"ple_dw_{tag}")
    dw_pg = _mm(hb, dgate, ta=True, out_dtype=BF16, name=f"ple_gate_dw_{tag}")
    dh_c = _mm(dgate, w_pg, tb=True, name=f"ple_gate_dx_{tag}")
    return [dh_a, dh_b, dh_c], dict(ffn_w_in=dw_in, ffn_w_out=dw_out, ple_w=dw_ple, ple_gate_w=dw_pg, ln2_g=dg, ln2_b=db)


def kernel(x, p, positions, attn_w_in, mla_q_norm, mla_w_q_b, mla_kv_norm, mla_w_kv_b, attn_w_out, s5_a_re, s5_a_im, s5_log_dt, s5_b_re, s5_b_im, s5_c_re, s5_c_im, s5_d, s5_w_glu, ln1_g, ln1_b, ffn_w_in, ffn_w_out, ple_w, ple_gate_w, ln2_g, ln2_b, loss_target, m_attn_w_in, m_mla_q_norm, m_mla_w_q_b, m_mla_kv_norm, m_mla_w_kv_b, m_attn_w_out, m_s5_a_re, m_s5_a_im, m_s5_log_dt, m_s5_b_re, m_s5_b_im, m_s5_c_re, m_s5_c_im, m_s5_d, m_s5_w_glu, m_ln1_g, m_ln1_b, m_ffn_w_in, m_ffn_w_out, m_ple_w, m_ple_gate_w, m_ln2_g, m_ln2_b, v_attn_w_in, v_mla_q_norm, v_mla_w_q_b, v_mla_kv_norm, v_mla_w_kv_b, v_attn_w_out, v_s5_a_re, v_s5_a_im, v_s5_log_dt, v_s5_b_re, v_s5_b_im, v_s5_c_re, v_s5_c_im, v_s5_d, v_s5_w_glu, v_ln1_g, v_ln1_b, v_ffn_w_in, v_ffn_w_out, v_ple_w, v_ple_gate_w, v_ln2_g, v_ln2_b):
    local = dict(attn_w_in=attn_w_in, mla_q_norm=mla_q_norm, mla_w_q_b=mla_w_q_b, mla_kv_norm=mla_kv_norm,
                 mla_w_kv_b=mla_w_kv_b, attn_w_out=attn_w_out, s5_a_re=s5_a_re, s5_a_im=s5_a_im, s5_log_dt=s5_log_dt,
                 s5_b_re=s5_b_re, s5_b_im=s5_b_im, s5_c_re=s5_c_re, s5_c_im=s5_c_im, s5_d=s5_d, s5_w_glu=s5_w_glu,
                 ln1_g=ln1_g, ln1_b=ln1_b, ffn_w_in=ffn_w_in, ffn_w_out=ffn_w_out, ple_w=ple_w, ple_gate_w=ple_gate_w,
                 ln2_g=ln2_g, ln2_b=ln2_b)
    mom_m = dict(zip(WEIGHTS, (m_attn_w_in, m_mla_q_norm, m_mla_w_q_b, m_mla_kv_norm, m_mla_w_kv_b, m_attn_w_out, m_s5_a_re, m_s5_a_im, m_s5_log_dt, m_s5_b_re, m_s5_b_im, m_s5_c_re, m_s5_c_im, m_s5_d, m_s5_w_glu, m_ln1_g, m_ln1_b, m_ffn_w_in, m_ffn_w_out, m_ple_w, m_ple_gate_w, m_ln2_g, m_ln2_b)))
    mom_v = dict(zip(WEIGHTS, (v_attn_w_in, v_mla_q_norm, v_mla_w_q_b, v_mla_kv_norm, v_mla_w_kv_b, v_attn_w_out, v_s5_a_re, v_s5_a_im, v_s5_log_dt, v_s5_b_re, v_s5_b_im, v_s5_c_re, v_s5_c_im, v_s5_d, v_s5_w_glu, v_ln1_g, v_ln1_b, v_ffn_w_in, v_ffn_w_out, v_ple_w, v_ple_gate_w, v_ln2_g, v_ln2_b)))

    t, d = x.shape[1], x.shape[2]
    depth = ln1_g.shape[0]
    alpha = (2.0 * depth) ** 0.25
    ql, kvl = mla_q_norm.shape[-1], mla_kv_norm.shape[-1]
    nh = mla_w_q_b.shape[-1] * N_DEV // (NOPE_DIM + ROPE_DIM)
    dw = (attn_w_in.shape[-1] * N_DEV - ql - kvl - ROPE_DIM) // 3
    ndh = dw // DIL_HEAD_DIM
    ng = d // S5_GROUP
    assert ql == kvl and (ql + kvl) % LANE == 0 and nh * V_DIM == dw

    sharded = [n for n in WEIGHTS if n in COL_SHARDED or n in ROW_SHARDED]
    def as_sent(n):
        if n == "s5_d":
            return local[n]
        if n == "attn_w_in":
            return local[n][0].T.astype(BF16)
        return local[n].astype(BF16).reshape(-1, local[n].shape[-1])

    full = {}

    def take_weights(names, gathered):
        for n, got in zip(names, gathered):
            if n == "attn_w_in":
                full[n] = got.reshape(N_DEV * local[n].shape[2], d)
            else:
                full[n] = _unshard(n, got.reshape((N_DEV,) + local[n].shape))

    first_w = [n for n in sharded if n in ATTENTION_WEIGHTS]
    later_w = [n for n in sharded if n not in ATTENTION_WEIGHTS]
    take_weights(first_w, _exchange_call("gather", [as_sent(n) for n in first_w], "gather_attn_weights"))

    w_in_t = full["attn_w_in"]
    lat = ql + kvl
    w_lat_t = jnp.concatenate([w_in_t[: lat + ROPE_DIM], jnp.zeros((LANE - ROPE_DIM, d), BF16)], axis=0)
    w_dil_t = w_in_t[lat + ROPE_DIM :]
    wq = full["mla_w_q_b"][0].reshape(ql, nh, NOPE_DIM + ROPE_DIM)
    wq_pe = jnp.pad(wq[:, :, NOPE_DIM:], ((0, 0), (0, 0), (0, LANE - ROPE_DIM)))
    wqp = jnp.concatenate([wq[:, :, :NOPE_DIM].reshape(ql, nh * LANE), wq_pe.reshape(ql, nh * LANE)], axis=1)
    wkv = full["mla_w_kv_b"][0].reshape(kvl, nh, NOPE_DIM + V_DIM)
    wkvp = jnp.concatenate([wkv[:, :, :NOPE_DIM].reshape(kvl, nh * LANE), wkv[:, :, NOPE_DIM:].reshape(kvl, nh * LANE)], axis=1)
    w_out = full["attn_w_out"][0]

    h0 = x[0]
    h0b = h0.astype(BF16)
    pb = p.astype(BF16)
    target = loss_target[0]
    pos = positions.reshape(t, 1)
    grads = {}

    z = _mm(h0b, w_lat_t, tb=True, name="attn_in_lat")
    zd = _mm(h0b, w_dil_t, tb=True, out_dtype=BF16, name="attn_in_dil")
    gq, gk = mla_q_norm.reshape(1, ql), mla_kv_norm.reshape(1, kvl)
    qn, kvn = _rowwise(_rms_fn, [(z, ql, 0), (z, kvl, 1)], [gq, gk], [(ql, ql, BF16), (kvl, kvl, BF16)], name="rms", tq=256)
    qf = _mm(qn, wqp, name="q_up")
    kvf = _mm(kvn, wkvp, out_dtype=BF16, name="kv_up")
    cos, sin = _rope_tables(pos, "rope_tables")
    pe_cb = lat // LANE
    qpe, kpe = _rowwise(_rope_fwd_fn(nh), [(qf, nh * LANE, 1), (z, LANE, pe_cb), _whole(cos), _whole(sin)], [],
                        [(nh * LANE, nh * LANE, BF16), (LANE, LANE, BF16)], name="rope", tq=256)
    out_a, lse_a, later_gathered = _mla_fwd(qf, qpe, kvf, kpe, nh, "mla_fwd", side=("gather", [as_sent(n) for n in later_w]))
    take_weights(later_w, later_gathered)
    w_glu = full["s5_w_glu"][0]
    d_skip = full["s5_d"]

    def to_classes(a, dil):
        if dil == 1:
            return a
        return a.reshape(t // dil, dil, a.shape[1]).transpose(1, 0, 2).reshape(t, a.shape[1])

    def from_classes(a, dil):
        if dil == 1:
            return a
        return a.reshape(dil, t // dil, a.shape[1]).transpose(1, 0, 2).reshape(t, a.shape[1])

    band = []
    for window, dil in DIL_BRANCHES:
        assert window // dil == BLK
        zc = to_classes(zd, dil)
        o_c, l_c = _band_fwd(zc, ndh, dil, f"band_fwd_d{dil}")
        band.append((dil, zc, o_c, l_c, from_classes(o_c, dil), from_classes(l_c, dil)))
    merge_rows = [_whole(b[4]) for b in band] + [_whole(b[5]) for b in band]
    out_b = _rowwise(_merge_fn, merge_rows, [], [(dw, dw, BF16)], name="merge", tq=256)[0]
    att = jnp.concatenate([out_a, out_b], axis=1)
    mix0 = _mm(att, w_out, name="attn_out")
    g1, b1 = ln1_g[0:1], ln1_b[0:1]
    h1, h1b = _rowwise(_twice(_ln_mix_fn(alpha)), [_whole(h0), _whole(mix0)], [g1, b1], [(d, d, F32), (d, d, BF16)],
                       name="ln1_l0", tq=256)
    h2, _, saved_f0 = _ffn_block(h1, h1b, pb[0, 0], full["ffn_w_in"][0], full["ffn_w_out"][0], full["ple_w"][0],
                                 full["ple_gate_w"][0], ln2_g[0:1], ln2_b[0:1], alpha, "l0")

    prm = [s5_a_re[0].reshape(ng, 1, S5_STATE), s5_a_im[0].reshape(ng, 1, S5_STATE), s5_log_dt[0].reshape(ng, 1, 1),
           s5_b_re[0].transpose(0, 2, 1), s5_b_im[0].transpose(0, 2, 1), s5_c_re[0], s5_c_im[0]]
    u = _to_groups(h2, "s5_regroup_u")
    e_re, e_im, lam_re, lam_im = _s5_local(prm, u, "s5_local")
    lam_re_c, lam_im_c = lam_re.reshape(1, ng * S5_STATE), lam_im.reshape(1, ng * S5_STATE)
    sp_re_c, sp_im_c = _s5_carry(_states_to_cols(e_re), _states_to_cols(e_im), lam_re_c, lam_im_c, "s5_carry")
    sp_re, sp_im = _cols_to_states(sp_re_c, ng), _cols_to_states(sp_im_c, ng)
    ys = _from_groups(_s5_out(prm, u, sp_re, sp_im, "s5_out"), "s5_ungroup_y")
    z5 = _rowwise(_gelu_fn, [_whole(ys), _whole(h2)], [d_skip], [(d, d, BF16)], name="gelu", tq=256)[0]
    vg = _mm(z5, w_glu, name="glu_in")
    g3, b3 = ln1_g[1:2], ln1_b[1:2]
    h3, h3b = _rowwise(_twice(_ln_glu_fn(alpha)), [_whole(h2), (vg, d, 0), (vg, d, 1)], [g3, b3],
                       [(d, d, F32), (d, d, BF16)], name="ln1_l1", tq=256)
    h4, _, saved_f1 = _ffn_block(h3, h3b, pb[1, 0], full["ffn_w_in"][1], full["ffn_w_out"][1], full["ple_w"][1],
                                 full["ple_gate_w"][1], ln2_g[1:2], ln2_b[1:2], alpha, "l1")

    dh4, loss_acc = _loss_kernel(h4, target, "loss")
    loss = lax.psum(loss_acc[0, 0], AXES)

    dh3, gf1 = _ffn_block_bwd(saved_f1, [_whole(dh4)], full["ffn_w_in"][1], full["ffn_w_out"][1],
                              full["ple_gate_w"][1], ln2_g[1:2], ln2_b[1:2], alpha, "l1")
    dh2_a, dval, dgate, dg3, db3 = _rowwise_bwd(
        _ln_glu_fn(alpha), [_whole(h2), (vg, d, 0), (vg, d, 1)], [g3, b3], [[_whole(a) for a in dh3]],
        need=[True] * 3, drow=[(d, F32), (d, BF16), (d, BF16)], name="ln1_bwd_l1", tq=128)
    dvg = jnp.concatenate([dval, dgate], axis=1)
    dw_glu = _mm(z5, dvg, ta=True, out_dtype=BF16, name="glu_dw")
    dz5 = _mm(dvg, w_glu, tb=True, name="glu_dx")
    dys, dh2_b, dd = _rowwise_bwd(_gelu_fn, [_whole(ys), _whole(h2)], [d_skip], [[_whole(dz5)]], need=[True, True],
                                  drow=[(d, F32), (d, F32)], name="gelu_bwd", tq=128)
    grads["s5_d"] = dd
    dy = _to_groups(dys, "s5_regroup_dy")
    dsp_re, dsp_im = _s5_bwd_state(prm, dy, "s5_bwd_state")
    g_re_c, g_im_c, dlam_re_c, dlam_im_c = _s5_carry_bwd(_states_to_cols(dsp_re), _states_to_cols(dsp_im), sp_re_c, sp_im_c,
                                                         lam_re_c, lam_im_c, "s5_carry_bwd")
    s5g = _s5_bwd_main(prm, u, sp_re, sp_im, dy, _cols_to_states(g_re_c, ng), _cols_to_states(g_im_c, ng),
                       dlam_re_c.reshape(ng, 1, S5_STATE), dlam_im_c.reshape(ng, 1, S5_STATE), "s5_bwd_main")
    grads["s5_a_re"] = s5g[0].reshape(s5_a_re.shape)
    grads["s5_a_im"] = s5g[1].reshape(s5_a_im.shape)
    grads["s5_log_dt"] = s5g[2].reshape(s5_log_dt.shape)
    grads["s5_b_re"] = s5g[3].transpose(0, 2, 1)[None]
    grads["s5_b_im"] = s5g[4].transpose(0, 2, 1)[None]
    grads["s5_c_re"] = s5g[5][None]
    grads["s5_c_im"] = s5g[6][None]
    dh2_c = _from_groups(s5g[7], "s5_ungroup_du")

    dh1, gf0 = _ffn_block_bwd(saved_f0, [_whole(dh2_a), _whole(dh2_b), _whole(dh2_c)], full["ffn_w_in"][0],
                              full["ffn_w_out"][0], full["ple_gate_w"][0], ln2_g[0:1], ln2_b[0:1], alpha, "l0")
    for k in ("ln2_g", "ln2_b"):
        grads[k] = jnp.concatenate([gf0[k], gf1[k]])

    shards = {}

    def take(items, received):
        for (key, _), recv in zip(items, received):
            shards[key] = _sum_slots(recv, f"sum_grads_{key[0]}_{key[1]}")

    ffn_names = ("ffn_w_in", "ffn_w_out", "ple_w", "ple_gate_w")
    items_l1 = [(("s5_w_glu", 0), _owner_rows("s5_w_glu", dw_glu))] + [((k, 1), _owner_rows(k, gf1[k])) for k in ffn_names]
    items_l0 = [((k, 0), _owner_rows(k, gf0[k])) for k in ffn_names]
    dh0_a, dmix, dg1, db1 = _rowwise_bwd(_ln_mix_fn(alpha), [_whole(h0), _whole(mix0)], [g1, b1], [[_whole(a) for a in dh1]],
                                         need=[True, True], drow=[(d, F32), (d, BF16)], name="ln1_bwd_l0", tq=128)
    grads["ln1_g"] = jnp.concatenate([dg1, dg3])
    grads["ln1_b"] = jnp.concatenate([db1, db3])
    dw_out = _mm(att, dmix, ta=True, out_dtype=BF16, name="attn_out_dw")
    datt = _mm(dmix, w_out, tb=True, name="attn_out_dx")
    items_l0.append((("attn_w_out", 0), _owner_rows("attn_w_out", dw_out)))
    small_early = [n for n in WEIGHTS if n not in BIG_WEIGHTS and n not in ("mla_q_norm", "mla_kv_norm")]
    send_small, where_small = _pack([_by_owner(n, grads[n]).astype(F32) for n in small_early], 1)

    dmerge = _rowwise_bwd(_merge_fn, merge_rows, [], [[(datt, dw, 1)]], need=[True] * 6, drow=[(dw, F32)] * 6,
                          name="merge_bwd", tq=128)
    dq_s = dk_s = dv_s = None
    for k, (dil, zc, o_c, l_c, _, _) in enumerate(band):
        do_c, dl_c = to_classes(dmerge[k], dil), to_classes(dmerge[3 + k], dil)
        dq_c, dk_c, dv_c = _band_bwd(zc, o_c, l_c, do_c, dl_c, ndh, dil, f"band_bwd_d{dil}")
        dq_n, dk_n, dv_n = from_classes(dq_c, dil), from_classes(dk_c, dil), from_classes(dv_c, dil)
        dq_s = dq_n if dq_s is None else dq_s + dq_n
        dk_s = dk_n if dk_s is None else dk_s + dk_n
        dv_s = dv_n if dv_s is None else dv_s + dv_n

    dqn, dqp, delta, recv_l1 = _mla_bwd_q(qf, qpe, kvf, kpe, datt, out_a, lse_a, nh, "mla_bwd_q",
                                          side=("exchange", [rows for _, rows in items_l1]))
    take(items_l1, recv_l1)
    to_row = lambda a: a[:, ::LANE].T.reshape(nh, 1, t)
    dkn, dkp, dv, recv_l0 = _mla_bwd_kv(qf[:, : nh * LANE].astype(BF16), qpe, kvf, kpe, datt[:, : nh * LANE].astype(BF16),
                                        to_row(lse_a), to_row(delta), nh, "mla_bwd_kv",
                                        side=("exchange", [rows for _, rows in items_l0] + [send_small]))
    take(items_l0, recv_l0[:-1])
    sum_small = _sum_slots(recv_l0[-1], "sum_grads_small")
    dq_pe, dk_pe = _rowwise(_rope_bwd_fn(nh), [_whole(dqp), _whole(dkp), _whole(cos), _whole(sin)], [],
                            [(nh * LANE, nh * LANE, BF16), (LANE, LANE, BF16)], name="rope_bwd", tq=256)
    dqf = jnp.concatenate([dqn.astype(BF16), dq_pe], axis=1)
    dkvf = jnp.concatenate([dkn, dv], axis=1).astype(BF16)
    dwqp = _mm(qn, dqf, ta=True, name="q_up_dw")
    dqn_in = _mm(dqf, wqp, tb=True, name="q_up_dx")
    dwkvp = _mm(kvn, dkvf, ta=True, name="kv_up_dw")
    dkvn_in = _mm(dkvf, wkvp, tb=True, name="kv_up_dx")
    dql, dkvl, dgq, dgk = _rowwise_bwd(_rms_fn, [(z, ql, 0), (z, kvl, 1)], [gq, gk], [[_whole(dqn_in)], [_whole(dkvn_in)]],
                                       need=[True, True], drow=[(ql, BF16), (kvl, BF16)], name="rms_bwd", tq=256)
    grads["mla_q_norm"], grads["mla_kv_norm"] = dgq, dgk
    dz_lat = jnp.concatenate([dql, dkvl, dk_pe], axis=1)
    dz_dil = jnp.concatenate([dq_s, dk_s, dv_s], axis=1).astype(BF16)
    dw_lat_t = _mm(dz_lat, h0b, ta=True, out_dtype=BF16, name="attn_in_lat_dw")
    dw_dil_t = _mm(dz_dil, h0b, ta=True, out_dtype=BF16, name="attn_in_dil_dw")
    dh0_b = _mm(dz_lat, w_lat_t, name="attn_in_lat_dx")
    dh0_c = _mm(dz_dil, w_dil_t, name="attn_in_dil_dx")
    grad_x = _addn([dh0_a, dh0_b, dh0_c], "grad_x")[None]
    dw_in_t = jnp.concatenate([dw_lat_t[: lat + ROPE_DIM], dw_dil_t], axis=0)
    dwq_n = dwqp[:, : nh * LANE].reshape(ql, nh, NOPE_DIM)
    dwq_r = dwqp[:, nh * LANE :].reshape(ql, nh, LANE)[:, :, :ROPE_DIM]
    dwq = jnp.concatenate([dwq_n, dwq_r], axis=2).reshape(ql, nh * (NOPE_DIM + ROPE_DIM))
    dwkv_k = dwkvp[:, : nh * LANE].reshape(kvl, nh, NOPE_DIM)
    dwkv_v = dwkvp[:, nh * LANE :].reshape(kvl, nh, V_DIM)
    dwkv = jnp.concatenate([dwkv_k, dwkv_v], axis=2).reshape(kvl, nh * (NOPE_DIM + V_DIM))

    items_att = [(("attn_w_in", 0), dw_in_t.reshape(N_DEV, -1, PACK_COLS)),
                 (("mla_w_q_b", 0), _owner_rows("mla_w_q_b", dwq.astype(BF16))),
                 (("mla_w_kv_b", 0), _owner_rows("mla_w_kv_b", dwkv.astype(BF16)))]
    small_late = ["mla_q_norm", "mla_kv_norm"]
    send_late, where_late = _pack([_by_owner(n, grads[n]).astype(F32) for n in small_late], 1)
    received = _exchange_call("exchange", [rows for _, rows in items_att] + [send_late], "exchange_grads_last")
    take(items_att, received[:-1])
    sum_late = _sum_slots(received[-1], "sum_grads_norms")

    g_out, d_out, m_out, v_out = [], [], [], []
    for n in WEIGHTS:
        if n == "attn_w_in":
            g = shards[(n, 0)].reshape(local[n].shape[2], d).T[None]
        elif n in BIG_WEIGHTS:
            g = jnp.stack([shards[(n, layer)] for layer in range(local[n].shape[0])]).reshape(local[n].shape)
        elif n in small_late:
            g = _unpack(sum_late, 0, where_late[small_late.index(n)]).reshape(local[n].shape)
        else:
            g = _unpack(sum_small, 0, where_small[small_early.index(n)]).reshape(local[n].shape)
        dlt, mn, vn = _adamw(local[n], g, mom_m[n], mom_v[n], f"adamw_{n}")
        g_out.append(g)
        d_out.append(dlt)
        m_out.append(mn)
        v_out.append(vn)
    return (loss, grad_x, *g_out, *d_out, *m_out, *v_out)
```

```python
import functools
import math

import numpy as np
import jax
import jax.numpy as jnp
from jax import lax
from jax.experimental import pallas as pl
from jax.experimental.pallas import tpu as pltpu

F32 = jnp.float32
BF16 = jnp.bfloat16

NOPE_DIM, ROPE_DIM, V_DIM = 128, 64, 128
DIL_HEAD_DIM = 128
DIL_BRANCHES = ((128, 1), (512, 4), (2048, 16))
BLK = 128
LANE = 128
ROPE_THETA = 10000.0
S5_GROUP, S5_STATE = 16, 64
S5_CHUNK = 32
NEG = -1e30
ADAM_LR, ADAM_B1, ADAM_B2, ADAM_EPS, ADAM_WD, ADAM_STEP = 0.001, 0.9, 0.999, 1e-08, 0.01, 10
N_DEV = 8
AXES = ("x", "y", "c")
VMEM_LIMIT = 56 * 1024 * 1024
MM_VMEM_BUDGET = 40 * 1024 * 1024
PACK_COLS = 1024
PACK_ROW_TILE = 16
HI = lax.Precision.HIGH


def _tile(n, cands):
    for c in cands:
        if n % c == 0:
            return c
    return n


def _params(sem=None):
    return pltpu.CompilerParams(dimension_semantics=sem, vmem_limit_bytes=VMEM_LIMIT)


def _dot(a, b, dims, prec=None):
    return lax.dot_general(a, b, (dims, ((), ())), preferred_element_type=F32, precision=prec)


NN = ((1,), (0,))
NT = ((1,), (1,))
TN = ((0,), (0,))


def _mm(a, b, *, ta=False, tb=False, out_dtype=F32, name):
    (k, m) = a.shape if ta else a.shape[::-1]
    (n, k2) = b.shape if tb else b.shape[::-1]
    assert k == k2, (a.shape, b.shape, ta, tb)
    tm = _tile(m, (1024, 512, 256, 128))
    tn = _tile(n, (1024, 512, 384, 256, 128))
    sa, sb, so = a.dtype.itemsize, b.dtype.itemsize, jnp.dtype(out_dtype).itemsize

    def fits(tk):
        return 2 * (tm * tk * sa + tk * tn * sb) + 2 * tm * tn * so + 4 * tm * tn <= MM_VMEM_BUDGET

    tk = next((c for c in (2816, 2048, 1536, 1408, 1152, 1024, 768, 512, 384, 256, 128) if k % c == 0 and fits(c)), k)
    nk = k // tk
    dims = (((0,) if ta else (1,)), ((1,) if tb else (0,)))

    def body(a_ref, b_ref, o_ref, *acc):
        part = _dot(a_ref[...].astype(BF16), b_ref[...].astype(BF16), dims)
        if nk == 1:
            o_ref[...] = part.astype(o_ref.dtype)
            return
        (acc_ref,) = acc
        kk = pl.program_id(2)

        @pl.when(kk == 0)
        def _():
            acc_ref[...] = part

        @pl.when(kk > 0)
        def _():
            acc_ref[...] += part

        @pl.when(kk == nk - 1)
        def _():
            o_ref[...] = acc_ref[...].astype(o_ref.dtype)

    a_spec = pl.BlockSpec((tk, tm), lambda i, j, kk: (kk, i)) if ta else pl.BlockSpec((tm, tk), lambda i, j, kk: (i, kk))
    b_spec = pl.BlockSpec((tn, tk), lambda i, j, kk: (j, kk)) if tb else pl.BlockSpec((tk, tn), lambda i, j, kk: (kk, j))
    return pl.pallas_call(
        body,
        name=name,
        grid=(m // tm, n // tn, nk),
        in_specs=[a_spec, b_spec],
        out_specs=pl.BlockSpec((tm, tn), lambda i, j, kk: (i, j)),
        out_shape=jax.ShapeDtypeStruct((m, n), out_dtype),
        scratch_shapes=[pltpu.VMEM((tm, tn), F32)] if nk > 1 else [],
        compiler_params=_params(("parallel", "parallel", "arbitrary")),
    )(a, b)


def _row_spec(spec, tq):
    _, w, cb = spec
    return pl.BlockSpec((tq, w), lambda i, j: (i, cb + j))


def _full_spec(p):
    return pl.BlockSpec(p.shape, lambda i, j: (0,) * p.ndim)


def _rowwise(fn, rows, pars, outs, *, name, tq, ncol=1):
    t = rows[0][0].shape[0]
    tq = _tile(t, (tq, 128, 64, 32, 16, 8))
    nr, npar = len(rows), len(pars)

    def body(*refs):
        vals = [r[...].astype(F32) for r in refs[: nr + npar]]
        res = fn(*vals)
        for o, r in zip(refs[nr + npar :], res):
            o[...] = r.astype(o.dtype)

    res = pl.pallas_call(
        body,
        name=name,
        grid=(t // tq, ncol),
        in_specs=[_row_spec(s, tq) for s in rows] + [_full_spec(p) for p in pars],
        out_specs=[pl.BlockSpec((tq, w), lambda i, j: (i, j)) for (_, w, _) in outs],
        out_shape=[jax.ShapeDtypeStruct((t, wt), dt) for (wt, _, dt) in outs],
        compiler_params=_params(("parallel", "parallel")),
    )(*[s[0] for s in rows], *pars)
    return res


def _rowwise_bwd(fn, rows, pars, cots, *, need, drow, name, tq, ncol=1):
    t = rows[0][0].shape[0]
    tq = _tile(t, (tq, 128, 64, 32, 16, 8))
    nr, npar = len(rows), len(pars)
    assert ncol == 1 or npar == 0
    flat_cots = [s for c in cots for s in c]
    ncot = len(flat_cots)
    want = [k for k in range(nr) if need[k]]

    def body(*refs):
        vals = [r[...].astype(F32) for r in refs[: nr + npar]]
        cref = refs[nr + npar : nr + npar + ncot]
        oref = refs[nr + npar + ncot :]
        cvals, pos = [], 0
        for c in cots:
            acc = cref[pos][...].astype(F32)
            for q in range(1, len(c)):
                acc = acc + cref[pos + q][...].astype(F32)
            cvals.append(acc)
            pos += len(c)

        def closed(*diff):
            full = list(vals)
            for k, dv in zip(want + list(range(nr, nr + npar)), diff):
                full[k] = dv
            return tuple(fn(*full))

        diff_in = [vals[k] for k in want] + vals[nr:]
        _, vjp = jax.vjp(closed, *diff_in)
        grads = vjp(tuple(cvals))
        for q in range(len(want)):
            oref[q][...] = grads[q].astype(oref[q].dtype)
        if npar:
            first = pl.program_id(0) == 0

            @pl.when(first)
            def _():
                for q in range(npar):
                    oref[len(want) + q][...] = jnp.zeros_like(oref[len(want) + q])

            for q in range(npar):
                oref[len(want) + q][...] += grads[len(want) + q]

    out_specs = [pl.BlockSpec((tq, rows[k][1]), lambda i, j, cb=rows[k][2]: (i, j)) for k in want]
    out_specs += [_full_spec(p) for p in pars]
    out_shape = [jax.ShapeDtypeStruct((t, wt), dt) for (wt, dt) in drow]
    out_shape += [jax.ShapeDtypeStruct(p.shape, F32) for p in pars]
    return pl.pallas_call(
        body,
        name=name,
        grid=(t // tq, ncol),
        in_specs=[_row_spec(s, tq) for s in rows] + [_full_spec(p) for p in pars] + [_row_spec(s, tq) for s in flat_cots],
        out_specs=out_specs,
        out_shape=out_shape,
        compiler_params=_params(("arbitrary", "arbitrary") if npar else ("parallel", "parallel")),
    )(*[s[0] for s in rows], *pars, *[s[0] for s in flat_cots])


def _whole(a, w=None):
    return (a, a.shape[1] if w is None else w, 0)


def _rms_fn(ql, kvl, gq, gk):
    def one(x, g):
        return x * lax.rsqrt(jnp.mean(x * x, -1, keepdims=True) + 1e-6) * g

    return one(ql, gq), one(kvl, gk)


def _layernorm(s, g, b):
    mu = jnp.mean(s, -1, keepdims=True)
    d = s - mu
    var = jnp.mean(d * d, -1, keepdims=True)
    return d * lax.rsqrt(var + 1e-5) * g + b


def _ln_mix_fn(alpha):
    def fn(h, mix, g, b):
        return (_layernorm(alpha * h + mix, g, b),)

    return fn


def _ln_glu_fn(alpha):
    def fn(h, val, gate, g, b):
        return (_layernorm(alpha * h + val * jax.nn.sigmoid(gate), g, b),)

    return fn


def _ln_ffn_fn(alpha):
    def fn(h, f, pw, gate, g, b):
        return (_layernorm(alpha * h + f + pw * jax.nn.sigmoid(gate), g, b),)

    return fn


def _swiglu_fn(g, u):
    return (jax.nn.silu(g) * u,)


def _gelu_fn(ys, h, d):
    return (jax.nn.gelu(ys + d * h),)


def _merge_fn(o1, o2, o3, l1, l2, l3):
    m = jnp.maximum(jnp.maximum(l1, l2), l3)
    e1, e2, e3 = jnp.exp(l1 - m), jnp.exp(l2 - m), jnp.exp(l3 - m)
    return ((e1 * o1 + e2 * o2 + e3 * o3) / (e1 + e2 + e3),)


def _swap_halves(t):
    w = t.shape[1]
    lane = lax.broadcasted_iota(jnp.int32, t.shape, 1) % LANE
    up = jnp.where(lane < ROPE_DIM, pltpu.roll(t, ROPE_DIM // 2, 1), 0.0)
    return jnp.where(lane < ROPE_DIM // 2, pltpu.roll(t, w - ROPE_DIM // 2, 1), up)


def _rope_fwd_fn(nh):
    def fn(qpe, kpe, cos, sin):
        cq, sq = jnp.tile(cos, (1, nh)), jnp.tile(sin, (1, nh))
        return qpe * cq + _swap_halves(qpe) * sq, kpe * cos + _swap_halves(kpe) * sin

    return fn


def _rope_bwd_fn(nh):
    def fn(dq, dk_heads, cos, sin):
        cq, sq = jnp.tile(cos, (1, nh)), jnp.tile(sin, (1, nh))
        dk = dk_heads[:, :LANE]
        for h in range(1, nh):
            dk = dk + dk_heads[:, h * LANE : (h + 1) * LANE]
        return dq * cq + _swap_halves(dq * sq), dk * cos + _swap_halves(dk * sin)

    return fn


def _rope_tables(positions, name):
    t = positions.shape[0]
    tq = _tile(t, (512, 128, 8))
    half = ROPE_DIM // 2

    def body(p_ref, c_ref, s_ref):
        lane = lax.broadcasted_iota(jnp.int32, (tq, LANE), 1)
        idx = (lane % half).astype(F32)
        inv_freq = jnp.exp(idx * (-math.log(ROPE_THETA) / half))
        ang = p_ref[...].astype(F32) * inv_freq
        live = lane < ROPE_DIM
        c_ref[...] = jnp.where(live, jnp.cos(ang), 0.0)
        s_ref[...] = jnp.where(live, jnp.where(lane < half, -jnp.sin(ang), jnp.sin(ang)), 0.0)

    return pl.pallas_call(
        body,
        name=name,
        grid=(t // tq,),
        in_specs=[pl.BlockSpec((tq, 1), lambda i: (i, 0))],
        out_specs=[pl.BlockSpec((tq, LANE), lambda i: (i, 0))] * 2,
        out_shape=[jax.ShapeDtypeStruct((t, LANE), F32)] * 2,
        compiler_params=_params(("parallel",)),
    )(positions)


def _loss_kernel(y, target, name):
    t, d = y.shape
    tq = _tile(t, (256, 128, 8))

    def body(y_ref, t_ref, dy_ref, l_ref):
        @pl.when(pl.program_id(0) == 0)
        def _():
            l_ref[...] = jnp.zeros_like(l_ref)

        e = y_ref[...] - t_ref[...]
        dy_ref[...] = e * (1.0 / d)
        l_ref[...] += jnp.sum(e * e) * (0.5 / d)

    return pl.pallas_call(
        body,
        name=name,
        grid=(t // tq,),
        in_specs=[pl.BlockSpec((tq, d), lambda i: (i, 0))] * 2,
        out_specs=[pl.BlockSpec((tq, d), lambda i: (i, 0)), pl.BlockSpec((8, LANE), lambda i: (0, 0))],
        out_shape=[jax.ShapeDtypeStruct((t, d), F32), jax.ShapeDtypeStruct((8, LANE), F32)],
        compiler_params=_params(("arbitrary",)),
    )(y, target)


def _addn(arrs, name):
    w = arrs[0].shape[1]
    return _rowwise(lambda *v: (functools.reduce(lambda p, q: p + q, v),), [_whole(a) for a in arrs], [], [(w, w, F32)],
                    name=name, tq=256)[0]


def _mla_tiles(t):
    tq = _tile(t, (512, 256, 128))
    return tq, t // tq


def _call_with_side(body, side, n_in, grid, *, name, in_specs, out_specs, out_shape):
    if side is None:
        return pl.pallas_call(body, name=name, grid=grid, in_specs=in_specs, out_specs=out_specs, out_shape=out_shape,
                              compiler_params=_params(("parallel", "parallel")))
    n_out, n = len(out_specs), len(side[1])

    def wrapped(*refs):
        ins, side_ins = refs[:n_in], refs[n_in : n_in + n]
        outs, side_outs = refs[n_in + n : n_in + n + n_out], refs[n_in + n + n_out : n_in + 2 * n + n_out]
        done = _run_side(side, side_ins, side_outs, refs[n_in + 2 * n + n_out :], grid[0], grid[1])
        body(*ins, *outs)
        done()

    call = pl.pallas_call(wrapped, name=name, grid=grid, in_specs=in_specs + [ANY_SPEC] * n,
                          out_specs=out_specs + [ANY_SPEC] * n, out_shape=out_shape + _side_out_shapes(side),
                          scratch_shapes=_comm_sems(n), compiler_params=_params(("arbitrary", "arbitrary")))

    def run(*args):
        res = call(*args, *side[1])
        return (*res[:n_out], list(res[n_out:]))

    return run


def _mla_heads(nope, rope, nh):
    t = nope.shape[0]
    rope = rope.reshape(t, -1, LANE)
    rope = jnp.broadcast_to(rope, (t, nh, LANE))
    return jnp.stack([nope.astype(BF16).reshape(t, nh, LANE), rope], axis=2).reshape(t, 2 * nh * LANE)


def _mla_fwd(qcat, kcat, kvf, nh, name, side=None):
    t = qcat.shape[0]
    tq, nq = _mla_tiles(t)
    scale = (NOPE_DIM + ROPE_DIM) ** -0.5

    def body(q_ref, k_ref, v_ref, o_ref, lse_ref):
        i = pl.program_id(1)
        q = q_ref[...]

        def step(j, carry, masked):
            m, l, acc = carry
            ks = pl.ds(pl.multiple_of(j * tq, tq), tq)
            s = _dot(q, k_ref[ks, :], NT) * scale
            if masked:
                row = lax.broadcasted_iota(jnp.int32, (tq, tq), 0)
                col = lax.broadcasted_iota(jnp.int32, (tq, tq), 1)
                s = jnp.where(col <= row, s, NEG)
            m_new = jnp.maximum(m, jnp.max(s, -1, keepdims=True))
            p = jnp.exp(s - m_new)
            a = jnp.exp(m - m_new)
            return m_new, a * l + jnp.sum(p, -1, keepdims=True), a * acc + _dot(p.astype(BF16), v_ref[ks, :], NN)

        init = (jnp.full((tq, 1), NEG, F32), jnp.zeros((tq, 1), F32), jnp.zeros((tq, V_DIM), F32))
        carry = lax.fori_loop(0, i, lambda j, c: step(j, c, False), init)
        m, l, acc = step(i, carry, True)
        o_ref[...] = (acc / l).astype(o_ref.dtype)
        lse_ref[...] = jnp.broadcast_to(m + jnp.log(l), (tq, LANE))

    blk = lambda h, i: (i, h)
    return _call_with_side(
        body, side, 3, (nh, nq),
        name=name,
        in_specs=[
            pl.BlockSpec((tq, 2 * LANE), blk),
            pl.BlockSpec((t, 2 * LANE), lambda h, i: (0, h)),
            pl.BlockSpec((t, LANE), lambda h, i: (0, nh + h)),
        ],
        out_specs=[pl.BlockSpec((tq, LANE), blk), pl.BlockSpec((tq, LANE), blk)],
        out_shape=[jax.ShapeDtypeStruct((t, nh * LANE), BF16), jax.ShapeDtypeStruct((t, nh * LANE), F32)],
    )(qcat, kcat, kvf)


def _mla_bwd_q(qcat, kcat, kvf, do, o, lse, nh, name, side=None):
    t = qcat.shape[0]
    tq, nq = _mla_tiles(t)
    scale = (NOPE_DIM + ROPE_DIM) ** -0.5

    def body(q_ref, k_ref, v_ref, do_ref, o_ref, lse_ref, dq_ref, dl_ref):
        i = pl.program_id(1)
        q = q_ref[...]
        dof = do_ref[...].astype(F32)
        dob = dof.astype(BF16)
        delta = jnp.sum(dof * o_ref[...].astype(F32), -1, keepdims=True)
        lse1 = lse_ref[:, :1]

        def step(j, dq, masked):
            ks = pl.ds(pl.multiple_of(j * tq, tq), tq)
            k = k_ref[ks, :]
            p = jnp.exp(_dot(q, k, NT) * scale - lse1)
            if masked:
                row = lax.broadcasted_iota(jnp.int32, (tq, tq), 0)
                col = lax.broadcasted_iota(jnp.int32, (tq, tq), 1)
                p = jnp.where(col <= row, p, 0.0)
            dp = _dot(dob, v_ref[ks, :], NT)
            ds = (p * (dp - delta) * scale).astype(BF16)
            return dq + _dot(ds, k, NN)

        dq = lax.fori_loop(0, i, lambda j, c: step(j, c, False), jnp.zeros((tq, 2 * LANE), F32))
        dq_ref[...] = step(i, dq, True)
        dl_ref[...] = jnp.broadcast_to(delta, (tq, LANE))

    blk = lambda h, i: (i, h)
    bs, wide = pl.BlockSpec((tq, LANE), blk), pl.BlockSpec((tq, 2 * LANE), blk)
    return _call_with_side(
        body, side, 6, (nh, nq),
        name=name,
        in_specs=[wide, pl.BlockSpec((t, 2 * LANE), lambda h, i: (0, h)), pl.BlockSpec((t, LANE), lambda h, i: (0, nh + h)),
                  bs, bs, bs],
        out_specs=[wide, bs],
        out_shape=[jax.ShapeDtypeStruct((t, 2 * nh * LANE), F32), jax.ShapeDtypeStruct((t, nh * LANE), F32)],
    )(qcat, kcat, kvf, do, o, lse)


def _mla_bwd_kv(qcat, kcat, kvf, do16, lse_row, delta_row, nh, name, side=None):
    t = qcat.shape[0]
    tq, nq = _mla_tiles(t)
    scale = (NOPE_DIM + ROPE_DIM) ** -0.5

    def body(k_ref, v_ref, q_ref, do_ref, lse_ref, dl_ref, dk_ref, dv_ref):
        j = pl.program_id(1)
        k, v = k_ref[...], v_ref[...]

        def step(i, carry, masked):
            dk, dv = carry
            qs = pl.ds(pl.multiple_of(i * tq, tq), tq)
            q, dob = q_ref[qs, :], do_ref[qs, :]
            pt = jnp.exp(_dot(k, q, NT) * scale - lse_ref[0, :, qs])
            if masked:
                key = lax.broadcasted_iota(jnp.int32, (tq, tq), 0)
                qry = lax.broadcasted_iota(jnp.int32, (tq, tq), 1)
                pt = jnp.where(key <= qry, pt, 0.0)
            dv = dv + _dot(pt.astype(BF16), dob, NN)
            dpt = _dot(v, dob, NT)
            dst = (pt * (dpt - dl_ref[0, :, qs]) * scale).astype(BF16)
            return dk + _dot(dst, q, NN), dv

        carry = step(j, (jnp.zeros((tq, 2 * LANE), F32), jnp.zeros((tq, LANE), F32)), True)
        dk, dv = lax.fori_loop(j + 1, nq, lambda i, c: step(i, c, False), carry)
        dk_ref[...] = dk
        dv_ref[...] = dv

    blk = pl.BlockSpec((tq, LANE), lambda h, j: (j, h))
    wide = pl.BlockSpec((tq, 2 * LANE), lambda h, j: (j, h))
    row = pl.BlockSpec((1, 1, t), lambda h, j: (h, 0, 0))
    return _call_with_side(
        body, side, 6, (nh, nq),
        name=name,
        in_specs=[wide, pl.BlockSpec((tq, LANE), lambda h, j: (j, nh + h)), pl.BlockSpec((t, 2 * LANE), lambda h, j: (0, h)),
                  pl.BlockSpec((t, LANE), lambda h, j: (0, h)), row, row],
        out_specs=[wide, blk],
        out_shape=[jax.ShapeDtypeStruct((t, 2 * nh * LANE), F32), jax.ShapeDtypeStruct((t, nh * LANE), F32)],
    )(kcat, kvf, qcat, do16, lse_row, delta_row)


def _alibi_slopes(n):
    return [float(2.0 ** (-8.0 * i / n)) for i in range(1, n + 1)]


def _window_mask(has_prev):
    qi = lax.broadcasted_iota(jnp.int32, (BLK, 2 * BLK), 0)
    ki = lax.broadcasted_iota(jnp.int32, (BLK, 2 * BLK), 1)
    dist = qi + BLK - ki
    valid = (dist >= 0) & (dist <= BLK) & ((ki >= BLK) | has_prev)
    return valid, dist.astype(F32)


def _band_fwd(zd, nh, dil, name):
    t = zd.shape[0]
    nbc = t // BLK // dil
    scale = DIL_HEAD_DIM ** -0.5
    slopes = _alibi_slopes(nh)
    dw = nh * LANE

    def body(q_ref, kc_ref, kp_ref, vc_ref, vp_ref, o_ref, l_ref):
        mask, dist = _window_mask(pl.program_id(1) > 0)
        for h in range(nh):
            sl = slice(h * LANE, (h + 1) * LANE)
            keys = jnp.concatenate([kp_ref[:, sl], kc_ref[:, sl]], axis=0)
            vals = jnp.concatenate([vp_ref[:, sl], vc_ref[:, sl]], axis=0)
            s = jnp.where(mask, _dot(q_ref[:, sl], keys, NT) * scale - (slopes[h] * dil) * dist, NEG)
            m = jnp.max(s, -1, keepdims=True)
            e = jnp.exp(s - m)
            l = jnp.sum(e, -1, keepdims=True)
            o_ref[:, sl] = _dot((e * (1.0 / l)).astype(BF16), vals, NN)
            l_ref[:, sl] = jnp.broadcast_to(m + jnp.log(l), (BLK, LANE))

    prev = lambda i: jnp.maximum(i - 1, 0)
    spec = lambda f: pl.BlockSpec((BLK, dw), f)
    return pl.pallas_call(
        body,
        name=name,
        grid=(dil, nbc),
        in_specs=[
            spec(lambda r, i: (r * nbc + i, 0)),
            spec(lambda r, i: (r * nbc + i, 1)),
            spec(lambda r, i: (r * nbc + prev(i), 1)),
            spec(lambda r, i: (r * nbc + i, 2)),
            spec(lambda r, i: (r * nbc + prev(i), 2)),
        ],
        out_specs=[spec(lambda r, i: (r * nbc + i, 0))] * 2,
        out_shape=[jax.ShapeDtypeStruct((t, dw), F32)] * 2,
        compiler_params=_params(("parallel", "parallel")),
    )(zd, zd, zd, zd, zd)


def _band_bwd(zd, o, lse, do, dl, nh, dil, name):
    t = zd.shape[0]
    nbc = t // BLK // dil
    scale = DIL_HEAD_DIM ** -0.5
    slopes = _alibi_slopes(nh)
    dw = nh * LANE

    def body(q_ref, k_ref, v_ref, kp_ref, vp_ref, qn_ref, o_ref, l_ref, do_ref, dl_ref, on_ref, ln_ref, don_ref, dln_ref,
             dq_ref, dk_ref, dv_ref):
        mask, dist = _window_mask(pl.program_id(1) > 0)
        mask_n, dist_n = _window_mask(pl.program_id(1) + 1 < nbc)
        mask_n, dist_n = mask_n[:, :BLK], dist_n[:, :BLK]
        for h in range(nh):
            sl = slice(h * LANE, (h + 1) * LANE)
            bias = slopes[h] * dil
            q, k, v, qn = q_ref[:, sl], k_ref[:, sl], v_ref[:, sl], qn_ref[:, sl]
            keys = jnp.concatenate([kp_ref[:, sl], k], axis=0)
            vals = jnp.concatenate([vp_ref[:, sl], v], axis=0)
            dof, donf = do_ref[:, sl], don_ref[:, sl]
            dob, donb = dof.astype(BF16), donf.astype(BF16)
            lse1, lsen1 = l_ref[:, sl][:, :1], ln_ref[:, sl][:, :1]
            adj = jnp.sum(dl_ref[:, sl] - dof * o_ref[:, sl], -1, keepdims=True)
            adjn = jnp.sum(dln_ref[:, sl] - donf * on_ref[:, sl], -1, keepdims=True)
            p = jnp.where(mask, jnp.exp(_dot(q, keys, NT) * scale - bias * dist - lse1), 0.0)
            ds = (p * (_dot(dob, vals, NT) + adj)).astype(BF16)
            dq_ref[:, sl] = _dot(ds, keys, NN) * scale
            pn = jnp.where(mask_n, jnp.exp(_dot(qn, k, NT) * scale - bias * dist_n - lsen1), 0.0)
            dsn = (pn * (_dot(donb, v, NT) + adjn)).astype(BF16)
            both_q = jnp.concatenate([q, qn], axis=0)
            dk_ref[:, sl] = _dot(jnp.concatenate([ds[:, BLK:], dsn], axis=0), both_q, TN) * scale
            dv_ref[:, sl] = _dot(jnp.concatenate([p[:, BLK:].astype(BF16), pn.astype(BF16)], axis=0),
                                 jnp.concatenate([dob, donb], axis=0), TN)

    prev = lambda i: jnp.maximum(i - 1, 0)
    nxt = lambda i: jnp.minimum(i + 1, nbc - 1)
    spec = lambda f: pl.BlockSpec((BLK, dw), f)
    cur, nx = spec(lambda r, i: (r * nbc + i, 0)), spec(lambda r, i: (r * nbc + nxt(i), 0))
    return pl.pallas_call(
        body,
        name=name,
        grid=(dil, nbc),
        in_specs=[cur, spec(lambda r, i: (r * nbc + i, 1)), spec(lambda r, i: (r * nbc + i, 2)),
                  spec(lambda r, i: (r * nbc + prev(i), 1)), spec(lambda r, i: (r * nbc + prev(i), 2)),
                  nx, cur, cur, cur, cur, nx, nx, nx, nx],
        out_specs=[cur] * 3,
        out_shape=[jax.ShapeDtypeStruct((t, dw), F32)] * 3,
        compiler_params=_params(("parallel", "parallel")),
    )(zd, zd, zd, zd, zd, zd, o, lse, do, dl, o, lse, do, dl)


def _s5_ops(a_re, a_im, ldt, bt_re, bt_im, c_re, c_im):
    L, g, p = S5_CHUNK, S5_GROUP, S5_STATE
    dt = jnp.exp(ldt)
    lr, li = a_re * dt, a_im * dt
    er = jnp.exp(lr)
    lam_re, lam_im = er * jnp.cos(li), er * jnp.sin(li)
    nr, ni = lam_re - 1.0, lam_im
    den = a_re * a_re + a_im * a_im
    fr, fi = (nr * a_re + ni * a_im) / den, (ni * a_re - nr * a_im) / den
    bb_re, bb_im = fr * bt_re - fi * bt_im, fr * bt_im + fi * bt_re

    def power(tau):
        mag = jnp.exp(tau * lr)
        return mag * jnp.cos(tau * li), mag * jnp.sin(tau * li)

    step = lax.broadcasted_iota(jnp.int32, (L, 1), 0).astype(F32)

    def outer(pr, pi, mr, mi):
        re = pr[:, None, :] * mr[None] - pi[:, None, :] * mi[None]
        im = pr[:, None, :] * mi[None] + pi[:, None, :] * mr[None]
        return re.reshape(L * g, p), im.reshape(L * g, p)

    half = float(L // 2)
    cp_re, cp_im = outer(*power(step - half), c_re, c_im)
    pb_re, pb_im = outer(*power(half - step), bb_re, bb_im)
    toep = _dot(cp_re, pb_re, NT, HI) - _dot(cp_im, pb_im, NT, HI)
    trow = lax.broadcasted_iota(jnp.int32, (L * g, L * g), 0) // g
    scol = lax.broadcasted_iota(jnp.int32, (L * g, L * g), 1) // g
    toep = jnp.where(trow >= scol, toep, 0.0)
    et_re, et_im = outer(*power(float(L - 1) - step), bb_re, bb_im)
    f_re, f_im = outer(*power(step + 1.0), c_re, c_im)
    big_re, big_im = power(jnp.full((1, 1), float(L), F32))
    return toep, et_re, et_im, f_re, f_im, big_re, big_im


def _s5_y(ops, u, sp_re, sp_im):
    toep, _, _, f_re, f_im, _, _ = ops
    return _dot(u, toep, NT, HI) + _dot(sp_re, f_re, NT, HI) - _dot(sp_im, f_im, NT, HI)


def _s5_group_specs(ng):
    vec = pl.BlockSpec((1, 1, S5_STATE), lambda g: (g, 0, 0))
    one = pl.BlockSpec((1, 1, 1), lambda g: (g, 0, 0))
    mat = pl.BlockSpec((1, S5_GROUP, S5_STATE), lambda g: (g, 0, 0))
    return [vec, vec, one, mat, mat, mat, mat]


def _s5_load(refs):
    return [r[0] for r in refs]


def _s5_local(prm, u, name):
    ng, n, w = u.shape
    p = S5_STATE

    def body(*refs):
        ops = _s5_ops(*_s5_load(refs[:7]))
        uu = refs[7][0]
        refs[8][0] = _dot(uu, ops[1], NN, HI)
        refs[9][0] = _dot(uu, ops[2], NN, HI)
        refs[10][0] = ops[5]
        refs[11][0] = ops[6]

    blk = lambda a, b: pl.BlockSpec((1, a, b), lambda g: (g, 0, 0))
    return pl.pallas_call(
        body,
        name=name,
        grid=(ng,),
        in_specs=_s5_group_specs(ng) + [blk(n, w)],
        out_specs=[blk(n, p), blk(n, p), blk(1, p), blk(1, p)],
        out_shape=[jax.ShapeDtypeStruct((ng, n, p), F32)] * 2 + [jax.ShapeDtypeStruct((ng, 1, p), F32)] * 2,
        compiler_params=_params(("parallel",)),
    )(*prm, u)


def _s5_carry(e_re, e_im, lam_re, lam_im, name):
    n, w = e_re.shape
    cw = _tile(w, (1024, 512, 256, 128))

    def body(er_ref, ei_ref, lr_ref, li_ref, sr_ref, si_ref):
        lr, li = lr_ref[...], li_ref[...]

        def step(k, carry):
            sr, si = carry
            row = pl.ds(k, 1)
            sr_ref[row, :] = sr
            si_ref[row, :] = si
            return lr * sr - li * si + er_ref[row, :], li * sr + lr * si + ei_ref[row, :]

        z = jnp.zeros((1, cw), F32)
        lax.fori_loop(0, n, step, (z, z))

    col = pl.BlockSpec((n, cw), lambda j: (0, j))
    one = pl.BlockSpec((1, cw), lambda j: (0, j))
    return pl.pallas_call(
        body,
        name=name,
        grid=(w // cw,),
        in_specs=[col, col, one, one],
        out_specs=[col, col],
        out_shape=[jax.ShapeDtypeStruct((n, w), F32)] * 2,
        compiler_params=_params(("parallel",)),
    )(e_re, e_im, lam_re, lam_im)


def _s5_carry_bwd(dsp_re, dsp_im, sp_re, sp_im, lam_re, lam_im, name):
    n, w = dsp_re.shape
    cw = _tile(w, (1024, 512, 256, 128))

    def body(dr_ref, di_ref, sr_ref, si_ref, lr_ref, li_ref, gr_ref, gi_ref, dlr_ref, dli_ref):
        lr, li = lr_ref[...], li_ref[...]

        def step(q, carry):
            gr_next, gi_next, dr_next, di_next, alr, ali = carry
            k = n - 1 - q
            row = pl.ds(k, 1)
            gr = dr_next + lr * gr_next + li * gi_next
            gi = di_next - li * gr_next + lr * gi_next
            gr_ref[row, :] = gr
            gi_ref[row, :] = gi
            sr, si = sr_ref[row, :], si_ref[row, :]
            return gr, gi, dr_ref[row, :], di_ref[row, :], alr + gr * sr + gi * si, ali + gi * sr - gr * si

        z = jnp.zeros((1, cw), F32)
        out = lax.fori_loop(0, n, step, (z, z, z, z, z, z))
        dlr_ref[...] = out[4]
        dli_ref[...] = out[5]

    col = pl.BlockSpec((n, cw), lambda j: (0, j))
    one = pl.BlockSpec((1, cw), lambda j: (0, j))
    return pl.pallas_call(
        body,
        name=name,
        grid=(w // cw,),
        in_specs=[col, col, col, col, one, one],
        out_specs=[col, col, one, one],
        out_shape=[jax.ShapeDtypeStruct((n, w), F32)] * 2 + [jax.ShapeDtypeStruct((1, w), F32)] * 2,
        compiler_params=_params(("parallel",)),
    )(dsp_re, dsp_im, sp_re, sp_im, lam_re, lam_im)


def _s5_out(prm, u, sp_re, sp_im, name):
    ng, n, w = u.shape
    p = S5_STATE

    def body(*refs):
        ops = _s5_ops(*_s5_load(refs[:7]))
        refs[10][0] = _s5_y(ops, refs[7][0], refs[8][0], refs[9][0])

    blk = lambda a, b: pl.BlockSpec((1, a, b), lambda g: (g, 0, 0))
    return pl.pallas_call(
        body,
        name=name,
        grid=(ng,),
        in_specs=_s5_group_specs(ng) + [blk(n, w), blk(n, p), blk(n, p)],
        out_specs=blk(n, w),
        out_shape=jax.ShapeDtypeStruct((ng, n, w), F32),
        compiler_params=_params(("parallel",)),
    )(*prm, u, sp_re, sp_im)


def _s5_bwd_state(prm, dy, name):
    ng, n, w = dy.shape
    p = S5_STATE

    def body(*refs):
        ops = _s5_ops(*_s5_load(refs[:7]))
        d = refs[7][0]
        refs[8][0] = _dot(d, ops[3], NN, HI)
        refs[9][0] = -_dot(d, ops[4], NN, HI)

    blk = lambda a, b: pl.BlockSpec((1, a, b), lambda g: (g, 0, 0))
    return pl.pallas_call(
        body,
        name=name,
        grid=(ng,),
        in_specs=_s5_group_specs(ng) + [blk(n, w)],
        out_specs=[blk(n, p), blk(n, p)],
        out_shape=[jax.ShapeDtypeStruct((ng, n, p), F32)] * 2,
        compiler_params=_params(("parallel",)),
    )(*prm, dy)


def _s5_bwd_main(prm, u, sp_re, sp_im, dy, g_re, g_im, dlam_re, dlam_im, name):
    ng, n, w = u.shape
    p = S5_STATE

    def body(*refs):
        prm_v = _s5_load(refs[:7])
        uu, sr, si, d, gr, gi, dlr, dli = [r[0] for r in refs[7:15]]
        ops, ops_vjp = jax.vjp(_s5_ops, *prm_v)
        cots = (_dot(d, uu, TN, HI), _dot(uu, gr, TN, HI), _dot(uu, gi, TN, HI), _dot(d, sr, TN, HI), -_dot(d, si, TN, HI),
                dlr, dli)
        grads = ops_vjp(cots)
        for q in range(7):
            refs[15 + q][0] = grads[q]
        refs[22][0] = _dot(d, ops[0], NN, HI) + _dot(gr, ops[1], NT, HI) + _dot(gi, ops[2], NT, HI)

    blk = lambda a, b: pl.BlockSpec((1, a, b), lambda g: (g, 0, 0))
    prm_specs = _s5_group_specs(ng)
    return pl.pallas_call(
        body,
        name=name,
        grid=(ng,),
        in_specs=prm_specs + [blk(n, w), blk(n, p), blk(n, p), blk(n, w), blk(n, p), blk(n, p), blk(1, p), blk(1, p)],
        out_specs=prm_specs + [blk(n, w)],
        out_shape=[jax.ShapeDtypeStruct(a.shape, F32) for a in prm] + [jax.ShapeDtypeStruct((ng, n, w), F32)],
        compiler_params=_params(("parallel",)),
    )(*prm, u, sp_re, sp_im, dy, g_re, g_im, dlam_re, dlam_im)


SLAB_GROUPS = LANE // S5_GROUP
CHUNK_TILES = S5_CHUNK * S5_GROUP // LANE


def _segment_transpose(tiles):
    seg = lax.broadcasted_iota(jnp.int32, (1, LANE), 1) // S5_GROUP
    tiles = list(tiles)
    stride = 1
    while stride < SLAB_GROUPS:
        upper = (seg & stride) != 0
        shift = S5_GROUP * stride
        new = list(tiles)
        for i in range(SLAB_GROUPS):
            if i & stride:
                continue
            j = i + stride
            new[i] = jnp.where(upper, pltpu.roll(tiles[j], shift, 1), tiles[i])
            new[j] = jnp.where(upper, tiles[j], pltpu.roll(tiles[i], LANE - shift, 1))
        tiles = new
        stride *= 2
    return tiles


def _to_groups(h, name):
    t, d = h.shape
    n = t // S5_CHUNK

    def body(x_ref, o_ref):
        for a in range(CHUNK_TILES):
            rows = [x_ref[pl.ds(SLAB_GROUPS * a + b, n, stride=S5_CHUNK), :] for b in range(SLAB_GROUPS)]
            for g, tile in enumerate(_segment_transpose(rows)):
                o_ref[g, :, a * LANE : (a + 1) * LANE] = tile

    return pl.pallas_call(
        body,
        name=name,
        grid=(d // LANE,),
        in_specs=[pl.BlockSpec((t, LANE), lambda q: (0, q))],
        out_specs=pl.BlockSpec((SLAB_GROUPS, n, S5_CHUNK * S5_GROUP), lambda q: (q, 0, 0)),
        out_shape=jax.ShapeDtypeStruct((d // S5_GROUP, n, S5_CHUNK * S5_GROUP), F32),
        compiler_params=_params(("parallel",)),
    )(h)


def _from_groups(y, name):
    ng, n, w = y.shape
    t, d = n * S5_CHUNK, ng * S5_GROUP

    def body(y_ref, x_ref):
        for a in range(CHUNK_TILES):
            tiles = [y_ref[g, :, a * LANE : (a + 1) * LANE] for g in range(SLAB_GROUPS)]
            for b, row in enumerate(_segment_transpose(tiles)):
                x_ref[pl.ds(SLAB_GROUPS * a + b, n, stride=S5_CHUNK), :] = row

    return pl.pallas_call(
        body,
        name=name,
        grid=(d // LANE,),
        in_specs=[pl.BlockSpec((SLAB_GROUPS, n, w), lambda q: (q, 0, 0))],
        out_specs=pl.BlockSpec((t, LANE), lambda q: (0, q)),
        out_shape=jax.ShapeDtypeStruct((t, d), F32),
        compiler_params=_params(("parallel",)),
    )(y)


def _states_to_cols(e):
    ng, n, p = e.shape
    return e.transpose(1, 0, 2).reshape(n, ng * p)


def _cols_to_states(s, ng):
    n = s.shape[0]
    return s.reshape(n, ng, S5_STATE).transpose(1, 0, 2)


def _me():
    return lax.axis_index("x"), lax.axis_index("y"), lax.axis_index("c")


def _flip(v, bit):
    return 1 - v if bit else v


COPIES = N_DEV - 1
ANY_SPEC = pl.BlockSpec(memory_space=pl.ANY)


def _comm_sems(n):
    return [pltpu.SemaphoreType.DMA((COPIES * n,)), pltpu.SemaphoreType.DMA((COPIES * n,)), pltpu.SemaphoreType.DMA((n,))]


def _gather_phases(x_refs, out_refs, send_sems, recv_sems, local_sems):
    mx, my, mc = _me()
    me, sibling = (mx, my, mc), (mx, my, 1 - mc)
    chips = [(1 - mx, my), (mx, 1 - my), (1 - mx, 1 - my)]
    arrays = range(len(x_refs))

    def slot(a, px, py, pc):
        return out_refs[a].at[4 * px + 2 * py + pc]

    def copy(a, k, block, to, from_input=False):
        return pltpu.make_async_remote_copy(
            src_ref=x_refs[a] if from_input else slot(a, *block),
            dst_ref=slot(a, *block),
            send_sem=send_sems.at[COPIES * a + k],
            recv_sem=recv_sems.at[COPIES * a + k],
            device_id=to,
            device_id_type=pl.DeviceIdType.MESH,
        )

    mine = [pltpu.make_async_copy(x_refs[a], slot(a, *me), local_sems.at[a]) for a in arrays]
    first = [copy(a, 0, me, sibling, True) for a in arrays]
    first += [copy(a, 1 + j, me, (*chip, mc), True) for a in arrays for j, chip in enumerate(chips)]
    passed = {(a, j): copy(a, 4 + j, (*chip, mc), sibling) for a in arrays for j, chip in enumerate(chips)}

    def start():
        for cp in mine + first:
            cp.start()

    def relay():
        for j, chip in enumerate(chips):
            for a in arrays:
                copy(a, 1 + j, (*chip, mc), me).wait_recv()
                passed[(a, j)].start()

    def finish():
        for a in arrays:
            copy(a, 0, sibling, me).wait_recv()
            for j, chip in enumerate(chips):
                copy(a, 4 + j, (*chip, 1 - mc), me).wait_recv()
        for cp in first + list(passed.values()):
            cp.wait_send()
        for cp in mine:
            cp.wait()

    return start, relay, finish


def _exchange_phases(g_refs, out_refs, send_sems, recv_sems, local_sems):
    mx, my, mc = _me()
    mine_idx = 4 * mx + 2 * my + mc
    arrays = range(len(g_refs))
    own = [pltpu.make_async_copy(g_refs[a].at[mine_idx], out_refs[a].at[mine_idx], local_sems.at[a]) for a in arrays]
    copies = []
    for m in range(1, N_DEV):
        px, py, pc = _flip(mx, m & 4), _flip(my, m & 2), _flip(mc, m & 1)
        for a in arrays:
            copies.append(
                pltpu.make_async_remote_copy(
                    src_ref=g_refs[a].at[4 * px + 2 * py + pc],
                    dst_ref=out_refs[a].at[mine_idx],
                    send_sem=send_sems.at[COPIES * a + m - 1],
                    recv_sem=recv_sems.at[COPIES * a + m - 1],
                    device_id=(px, py, pc),
                    device_id_type=pl.DeviceIdType.MESH,
                )
            )

    def start():
        for cp in own + copies:
            cp.start()

    def finish():
        for cp in copies:
            cp.wait_recv()
        for cp in copies:
            cp.wait_send()
        for cp in own:
            cp.wait()

    return start, None, finish


def _side_out_shapes(side):
    kind, arrs = side
    return [jax.ShapeDtypeStruct((N_DEV,) + a.shape if kind == "gather" else a.shape, a.dtype) for a in arrs]


def _run_side(side, x_refs, out_refs, sems, nsteps_major, nsteps_minor):
    start, relay, finish = (_gather_phases if side[0] == "gather" else _exchange_phases)(x_refs, out_refs, *sems)
    a, b = pl.program_id(0), pl.program_id(1)
    pl.when(jnp.logical_and(a == 0, b == 0))(start)
    if relay is not None:
        pl.when(jnp.logical_and(a == nsteps_major - 1, b == 0))(relay)
    return lambda: pl.when(jnp.logical_and(a == nsteps_major - 1, b == nsteps_minor - 1))(finish)


def _exchange_call(kind, arrs, name):
    n = len(arrs)

    def body(*refs):
        start, relay, finish = (_gather_phases if kind == "gather" else _exchange_phases)(refs[:n], refs[n : 2 * n], *refs[2 * n :])
        start()
        if relay is not None:
            relay()
        finish()

    return pl.pallas_call(body, name=name, in_specs=[ANY_SPEC] * n, out_specs=[ANY_SPEC] * n,
                          out_shape=_side_out_shapes((kind, arrs)), scratch_shapes=_comm_sems(n))(*arrs)


def _sum_slots(recv, name):
    _, r, c = recv.shape
    tr = _tile(r, (256, 128, 64, 32, 16, 8))

    def body(r_ref, o_ref):
        acc = r_ref[0].astype(F32)
        for k in range(1, N_DEV):
            acc = acc + r_ref[k].astype(F32)
        o_ref[...] = acc

    return pl.pallas_call(
        body,
        name=name,
        grid=(r // tr,),
        in_specs=[pl.BlockSpec((N_DEV, tr, c), lambda i: (0, i, 0))],
        out_specs=pl.BlockSpec((tr, c), lambda i: (i, 0)),
        out_shape=jax.ShapeDtypeStruct((r, c), F32),
        compiler_params=_params(("parallel",)),
    )(recv)


def _adamw(w, g, m, v, name):
    shape = w.shape
    c = shape[-1]
    as2d = lambda a: a.reshape(-1, c)
    w2, g2, m2, v2 = as2d(w), as2d(g), as2d(m), as2d(v)
    r = w2.shape[0]
    tr = _tile(r, (256, 128, 64, 32, 16, 8))
    c1 = 1.0 / (1.0 - ADAM_B1 ** ADAM_STEP)
    c2 = 1.0 / (1.0 - ADAM_B2 ** ADAM_STEP)

    def body(w_ref, g_ref, m_ref, v_ref, d_ref, mo_ref, vo_ref):
        gg = g_ref[...]
        mn = ADAM_B1 * m_ref[...] + (1.0 - ADAM_B1) * gg
        vn = ADAM_B2 * v_ref[...] + (1.0 - ADAM_B2) * (gg * gg)
        d_ref[...] = -ADAM_LR * ((mn * c1) / (jnp.sqrt(vn * c2) + ADAM_EPS) + ADAM_WD * w_ref[...])
        mo_ref[...] = mn
        vo_ref[...] = vn

    spec = pl.BlockSpec((tr, c), lambda i: (i, 0))
    d, mn, vn = pl.pallas_call(
        body,
        name=name,
        grid=(r // tr,),
        in_specs=[spec] * 4,
        out_specs=[spec] * 3,
        out_shape=[jax.ShapeDtypeStruct((r, c), F32)] * 3,
        compiler_params=_params(("parallel",)),
    )(w2, g2, m2, v2)
    return d.reshape(shape), mn.reshape(shape), vn.reshape(shape)


COL_SHARDED = ("attn_w_in", "mla_w_q_b", "mla_w_kv_b", "s5_d", "s5_w_glu", "ffn_w_in", "ple_w")
ROW_SHARDED = ("attn_w_out", "ffn_w_out", "ple_gate_w")
REPLICATED = ("mla_q_norm", "mla_kv_norm", "s5_a_re", "s5_a_im", "s5_log_dt", "s5_b_re", "s5_b_im", "s5_c_re",
              "s5_c_im", "ln1_g", "ln1_b", "ln2_g", "ln2_b")
ATTENTION_WEIGHTS = ("attn_w_in", "mla_w_q_b", "mla_w_kv_b", "attn_w_out")
BIG_WEIGHTS = ("attn_w_in", "mla_w_q_b", "mla_w_kv_b", "attn_w_out", "s5_w_glu", "ffn_w_in", "ffn_w_out", "ple_w", "ple_gate_w")
WEIGHTS = ("attn_w_in", "mla_q_norm", "mla_w_q_b", "mla_kv_norm", "mla_w_kv_b", "attn_w_out", "s5_a_re", "s5_a_im",
           "s5_log_dt", "s5_b_re", "s5_b_im", "s5_c_re", "s5_c_im", "s5_d", "s5_w_glu", "ln1_g", "ln1_b", "ffn_w_in",
           "ffn_w_out", "ple_w", "ple_gate_w", "ln2_g", "ln2_b")


def _unshard(name, gathered):
    if name in COL_SHARDED:
        full = jnp.moveaxis(gathered, 0, -2)
        return full.reshape(full.shape[:-2] + (full.shape[-2] * full.shape[-1],))
    full = jnp.moveaxis(gathered, 0, 1)
    return full.reshape((full.shape[0], full.shape[1] * full.shape[2]) + full.shape[3:])


def _by_owner(name, grad):
    if name in COL_SHARDED:
        g = grad.reshape(grad.shape[:-1] + (N_DEV, grad.shape[-1] // N_DEV))
        return jnp.moveaxis(g, -2, 0).reshape(N_DEV, -1)
    if name in ROW_SHARDED:
        g = grad.reshape((grad.shape[0], N_DEV, grad.shape[1] // N_DEV) + grad.shape[2:])
        return jnp.moveaxis(g, 1, 0).reshape(N_DEV, -1)
    return jnp.broadcast_to(grad.reshape(1, -1), (N_DEV, grad.size))


def _owner_rows(name, g):
    k, n = g.shape
    if name in COL_SHARDED:
        return g.reshape(k, N_DEV, n // N_DEV).transpose(1, 0, 2)
    assert name in ROW_SHARDED, name
    return g.reshape(N_DEV, k // N_DEV, n)


def _pack(pieces, axis):
    blocks, where, row = [], [], 0
    for p in pieces:
        n = p.shape[axis]
        pad = (-n) % (PACK_COLS * PACK_ROW_TILE)
        if pad:
            shape = list(p.shape)
            shape[axis] = pad
            p = jnp.concatenate([p, jnp.zeros(shape, p.dtype)], axis=axis)
        rows = (n + pad) // PACK_COLS
        blocks.append(p.reshape(p.shape[:axis] + (rows, PACK_COLS)))
        where.append((row, rows, n))
        row += rows
    return jnp.concatenate(blocks, axis=axis), where


def _unpack(buf, axis, where):
    row, rows, n = where
    part = lax.slice_in_dim(buf, row, row + rows, axis=axis)
    return lax.slice_in_dim(part.reshape(part.shape[:axis] + (rows * PACK_COLS,)), 0, n, axis=axis)


def _twice(fn):
    return lambda *a: fn(*a) * 2


def _ffn_block(h, hb, p_i, w_in, w_out, w_ple, w_pg, g, b, alpha, tag):
    t, d = h.shape
    hid = w_out.shape[0]
    cw = _tile(hid, (1408, 512, 256, 128))
    ncb = hid // cw
    gu = _mm(hb, w_in, out_dtype=BF16, name=f"ffn_in_{tag}")
    act = _rowwise(_swiglu_fn, [(gu, cw, 0), (gu, cw, ncb)], [], [(hid, cw, BF16)], name=f"swiglu_{tag}", tq=512, ncol=ncb)[0]
    f = _mm(act, w_out, name=f"ffn_out_{tag}")
    pw = _mm(p_i, w_ple, name=f"ple_{tag}")
    gate = _mm(hb, w_pg, name=f"ple_gate_{tag}")
    out, outb = _rowwise(_twice(_ln_ffn_fn(alpha)), [_whole(h), _whole(f), _whole(pw), _whole(gate)], [g, b],
                         [(d, d, F32), (d, d, BF16)], name=f"ln2_{tag}", tq=256)
    return out, outb, (h, hb, p_i, gu, act, f, pw, gate)


def _ffn_block_bwd(saved, dout, w_in, w_out, w_pg, g, b, alpha, tag):
    h, hb, p_i, gu, act, f, pw, gate = saved
    t, d = h.shape
    hid = w_out.shape[0]
    cw = _tile(hid, (1408, 512, 256, 128))
    ncb = hid // cw
    dh_a, df, dpw, dgate, dg, db = _rowwise_bwd(
        _ln_ffn_fn(alpha), [_whole(h), _whole(f), _whole(pw), _whole(gate)], [g, b], [dout],
        need=[True] * 4, drow=[(d, F32), (d, BF16), (d, BF16), (d, BF16)], name=f"ln2_bwd_{tag}", tq=128)
    dw_out = _mm(act, df, ta=True, out_dtype=BF16, name=f"ffn_out_dw_{tag}")
    dact = _mm(df, w_out, tb=True, out_dtype=BF16, name=f"ffn_out_dx_{tag}")
    dg_, du_ = _rowwise_bwd(_swiglu_fn, [(gu, cw, 0), (gu, cw, ncb)], [], [[(dact, cw, 0)]], need=[True, True],
                            drow=[(hid, BF16), (hid, BF16)], name=f"swiglu_bwd_{tag}", tq=256, ncol=ncb)
    dgu = jnp.concatenate([dg_, du_], axis=1)
    dw_in = _mm(hb, dgu, ta=True, out_dtype=BF16, name=f"ffn_in_dw_{tag}")
    dh_b = _mm(dgu, w_in, tb=True, name=f"ffn_in_dx_{tag}")
    dw_ple = _mm(p_i, dpw, ta=True, out_dtype=BF16, name=f"ple_dw_{tag}")
    dw_pg = _mm(hb, dgate, ta=True, out_dtype=BF16, name=f"ple_gate_dw_{tag}")
    dh_c = _mm(dgate, w_pg, tb=True, name=f"ple_gate_dx_{tag}")
    return [dh_a, dh_b, dh_c], dict(ffn_w_in=dw_in, ffn_w_out=dw_out, ple_w=dw_ple, ple_gate_w=dw_pg, ln2_g=dg, ln2_b=db)


def kernel(x, p, positions, attn_w_in, mla_q_norm, mla_w_q_b, mla_kv_norm, mla_w_kv_b, attn_w_out, s5_a_re, s5_a_im, s5_log_dt, s5_b_re, s5_b_im, s5_c_re, s5_c_im, s5_d, s5_w_glu, ln1_g, ln1_b, ffn_w_in, ffn_w_out, ple_w, ple_gate_w, ln2_g, ln2_b, loss_target, m_attn_w_in, m_mla_q_norm, m_mla_w_q_b, m_mla_kv_norm, m_mla_w_kv_b, m_attn_w_out, m_s5_a_re, m_s5_a_im, m_s5_log_dt, m_s5_b_re, m_s5_b_im, m_s5_c_re, m_s5_c_im, m_s5_d, m_s5_w_glu, m_ln1_g, m_ln1_b, m_ffn_w_in, m_ffn_w_out, m_ple_w, m_ple_gate_w, m_ln2_g, m_ln2_b, v_attn_w_in, v_mla_q_norm, v_mla_w_q_b, v_mla_kv_norm, v_mla_w_kv_b, v_attn_w_out, v_s5_a_re, v_s5_a_im, v_s5_log_dt, v_s5_b_re, v_s5_b_im, v_s5_c_re, v_s5_c_im, v_s5_d, v_s5_w_glu, v_ln1_g, v_ln1_b, v_ffn_w_in, v_ffn_w_out, v_ple_w, v_ple_gate_w, v_ln2_g, v_ln2_b):
    local = dict(attn_w_in=attn_w_in, mla_q_norm=mla_q_norm, mla_w_q_b=mla_w_q_b, mla_kv_norm=mla_kv_norm,
                 mla_w_kv_b=mla_w_kv_b, attn_w_out=attn_w_out, s5_a_re=s5_a_re, s5_a_im=s5_a_im, s5_log_dt=s5_log_dt,
                 s5_b_re=s5_b_re, s5_b_im=s5_b_im, s5_c_re=s5_c_re, s5_c_im=s5_c_im, s5_d=s5_d, s5_w_glu=s5_w_glu,
                 ln1_g=ln1_g, ln1_b=ln1_b, ffn_w_in=ffn_w_in, ffn_w_out=ffn_w_out, ple_w=ple_w, ple_gate_w=ple_gate_w,
                 ln2_g=ln2_g, ln2_b=ln2_b)
    mom_m = dict(zip(WEIGHTS, (m_attn_w_in, m_mla_q_norm, m_mla_w_q_b, m_mla_kv_norm, m_mla_w_kv_b, m_attn_w_out, m_s5_a_re, m_s5_a_im, m_s5_log_dt, m_s5_b_re, m_s5_b_im, m_s5_c_re, m_s5_c_im, m_s5_d, m_s5_w_glu, m_ln1_g, m_ln1_b, m_ffn_w_in, m_ffn_w_out, m_ple_w, m_ple_gate_w, m_ln2_g, m_ln2_b)))
    mom_v = dict(zip(WEIGHTS, (v_attn_w_in, v_mla_q_norm, v_mla_w_q_b, v_mla_kv_norm, v_mla_w_kv_b, v_attn_w_out, v_s5_a_re, v_s5_a_im, v_s5_log_dt, v_s5_b_re, v_s5_b_im, v_s5_c_re, v_s5_c_im, v_s5_d, v_s5_w_glu, v_ln1_g, v_ln1_b, v_ffn_w_in, v_ffn_w_out, v_ple_w, v_ple_gate_w, v_ln2_g, v_ln2_b)))

    t, d = x.shape[1], x.shape[2]
    depth = ln1_g.shape[0]
    alpha = (2.0 * depth) ** 0.25
    ql, kvl = mla_q_norm.shape[-1], mla_kv_norm.shape[-1]
    nh = mla_w_q_b.shape[-1] * N_DEV // (NOPE_DIM + ROPE_DIM)
    dw = (attn_w_in.shape[-1] * N_DEV - ql - kvl - ROPE_DIM) // 3
    ndh = dw // DIL_HEAD_DIM
    ng = d // S5_GROUP
    assert ql == kvl and (ql + kvl) % LANE == 0 and nh * V_DIM == dw

    sharded = [n for n in WEIGHTS if n in COL_SHARDED or n in ROW_SHARDED]
    def as_sent(n):
        if n == "s5_d":
            return local[n]
        if n == "attn_w_in":
            return local[n][0].T.astype(BF16)
        return local[n].astype(BF16).reshape(-1, local[n].shape[-1])

    full = {}

    def take_weights(names, gathered):
        for n, got in zip(names, gathered):
            if n == "attn_w_in":
                full[n] = got.reshape(N_DEV * local[n].shape[2], d)
            else:
                full[n] = _unshard(n, got.reshape((N_DEV,) + local[n].shape))

    first_w = [n for n in sharded if n in ATTENTION_WEIGHTS]
    later_w = [n for n in sharded if n not in ATTENTION_WEIGHTS]
    take_weights(first_w, _exchange_call("gather", [as_sent(n) for n in first_w], "gather_attn_weights"))

    w_in_t = full["attn_w_in"]
    lat = ql + kvl
    w_lat_t = jnp.concatenate([w_in_t[: lat + ROPE_DIM], jnp.zeros((LANE - ROPE_DIM, d), BF16)], axis=0)
    w_dil_t = w_in_t[lat + ROPE_DIM :]
    wq = full["mla_w_q_b"][0].reshape(ql, nh, NOPE_DIM + ROPE_DIM)
    wq_pe = jnp.pad(wq[:, :, NOPE_DIM:], ((0, 0), (0, 0), (0, LANE - ROPE_DIM)))
    wqp = jnp.concatenate([wq[:, :, :NOPE_DIM].reshape(ql, nh * LANE), wq_pe.reshape(ql, nh * LANE)], axis=1)
    wkv = full["mla_w_kv_b"][0].reshape(kvl, nh, NOPE_DIM + V_DIM)
    wkvp = jnp.concatenate([wkv[:, :, :NOPE_DIM].reshape(kvl, nh * LANE), wkv[:, :, NOPE_DIM:].reshape(kvl, nh * LANE)], axis=1)
    w_out = full["attn_w_out"][0]

    h0 = x[0]
    h0b = h0.astype(BF16)
    pb = p.astype(BF16)
    target = loss_target[0]
    pos = positions.reshape(t, 1)
    grads = {}

    z = _mm(h0b, w_lat_t, tb=True, name="attn_in_lat")
    zd = _mm(h0b, w_dil_t, tb=True, out_dtype=BF16, name="attn_in_dil")
    gq, gk = mla_q_norm.reshape(1, ql), mla_kv_norm.reshape(1, kvl)
    qn, kvn = _rowwise(_rms_fn, [(z, ql, 0), (z, kvl, 1)], [gq, gk], [(ql, ql, BF16), (kvl, kvl, BF16)], name="rms", tq=256)
    qf = _mm(qn, wqp, name="q_up")
    kvf = _mm(kvn, wkvp, out_dtype=BF16, name="kv_up")
    cos, sin = _rope_tables(pos, "rope_tables")
    pe_cb = lat // LANE
    qpe, kpe = _rowwise(_rope_fwd_fn(nh), [(qf, nh * LANE, 1), (z, LANE, pe_cb), _whole(cos), _whole(sin)], [],
                        [(nh * LANE, nh * LANE, BF16), (LANE, LANE, BF16)], name="rope", tq=256)
    qcat = _mla_heads(qf[:, : nh * LANE], qpe, nh)
    kcat = _mla_heads(kvf[:, : nh * LANE], kpe, nh)
    out_a, lse_a, later_gathered = _mla_fwd(qcat, kcat, kvf, nh, "mla_fwd", side=("gather", [as_sent(n) for n in later_w]))
    take_weights(later_w, later_gathered)
    w_glu = full["s5_w_glu"][0]
    d_skip = full["s5_d"]

    def to_classes(a, dil):
        if dil == 1:
            return a
        return a.reshape(t // dil, dil, a.shape[1]).transpose(1, 0, 2).reshape(t, a.shape[1])

    def from_classes(a, dil):
        if dil == 1:
            return a
        return a.reshape(dil, t // dil, a.shape[1]).transpose(1, 0, 2).reshape(t, a.shape[1])

    band = []
    for window, dil in DIL_BRANCHES:
        assert window // dil == BLK
        zc = to_classes(zd, dil)
        o_c, l_c = _band_fwd(zc, ndh, dil, f"band_fwd_d{dil}")
        band.append((dil, zc, o_c, l_c, from_classes(o_c, dil), from_classes(l_c, dil)))
    merge_rows = [_whole(b[4]) for b in band] + [_whole(b[5]) for b in band]
    out_b = _rowwise(_merge_fn, merge_rows, [], [(dw, dw, BF16)], name="merge", tq=256)[0]
    att = jnp.concatenate([out_a, out_b], axis=1)
    mix0 = _mm(att, w_out, name="attn_out")
    g1, b1 = ln1_g[0:1], ln1_b[0:1]
    h1, h1b = _rowwise(_twice(_ln_mix_fn(alpha)), [_whole(h0), _whole(mix0)], [g1, b1], [(d, d, F32), (d, d, BF16)],
                       name="ln1_l0", tq=256)
    h2, _, saved_f0 = _ffn_block(h1, h1b, pb[0, 0], full["ffn_w_in"][0], full["ffn_w_out"][0], full["ple_w"][0],
                                 full["ple_gate_w"][0], ln2_g[0:1], ln2_b[0:1], alpha, "l0")

    prm = [s5_a_re[0].reshape(ng, 1, S5_STATE), s5_a_im[0].reshape(ng, 1, S5_STATE), s5_log_dt[0].reshape(ng, 1, 1),
           s5_b_re[0].transpose(0, 2, 1), s5_b_im[0].transpose(0, 2, 1), s5_c_re[0], s5_c_im[0]]
    u = _to_groups(h2, "s5_regroup_u")
    e_re, e_im, lam_re, lam_im = _s5_local(prm, u, "s5_local")
    lam_re_c, lam_im_c = lam_re.reshape(1, ng * S5_STATE), lam_im.reshape(1, ng * S5_STATE)
    sp_re_c, sp_im_c = _s5_carry(_states_to_cols(e_re), _states_to_cols(e_im), lam_re_c, lam_im_c, "s5_carry")
    sp_re, sp_im = _cols_to_states(sp_re_c, ng), _cols_to_states(sp_im_c, ng)
    ys = _from_groups(_s5_out(prm, u, sp_re, sp_im, "s5_out"), "s5_ungroup_y")
    z5 = _rowwise(_gelu_fn, [_whole(ys), _whole(h2)], [d_skip], [(d, d, BF16)], name="gelu", tq=256)[0]
    vg = _mm(z5, w_glu, name="glu_in")
    g3, b3 = ln1_g[1:2], ln1_b[1:2]
    h3, h3b = _rowwise(_twice(_ln_glu_fn(alpha)), [_whole(h2), (vg, d, 0), (vg, d, 1)], [g3, b3],
                       [(d, d, F32), (d, d, BF16)], name="ln1_l1", tq=256)
    h4, _, saved_f1 = _ffn_block(h3, h3b, pb[1, 0], full["ffn_w_in"][1], full["ffn_w_out"][1], full["ple_w"][1],
                                 full["ple_gate_w"][1], ln2_g[1:2], ln2_b[1:2], alpha, "l1")

    dh4, loss_acc = _loss_kernel(h4, target, "loss")
    loss = lax.psum(loss_acc[0, 0], AXES)

    dh3, gf1 = _ffn_block_bwd(saved_f1, [_whole(dh4)], full["ffn_w_in"][1], full["ffn_w_out"][1],
                              full["ple_gate_w"][1], ln2_g[1:2], ln2_b[1:2], alpha, "l1")
    dh2_a, dval, dgate, dg3, db3 = _rowwise_bwd(
        _ln_glu_fn(alpha), [_whole(h2), (vg, d, 0), (vg, d, 1)], [g3, b3], [[_whole(a) for a in dh3]],
        need=[True] * 3, drow=[(d, F32), (d, BF16), (d, BF16)], name="ln1_bwd_l1", tq=128)
    dvg = jnp.concatenate([dval, dgate], axis=1)
    dw_glu = _mm(z5, dvg, ta=True, out_dtype=BF16, name="glu_dw")
    dz5 = _mm(dvg, w_glu, tb=True, name="glu_dx")
    dys, dh2_b, dd = _rowwise_bwd(_gelu_fn, [_whole(ys), _whole(h2)], [d_skip], [[_whole(dz5)]], need=[True, True],
                                  drow=[(d, F32), (d, F32)], name="gelu_bwd", tq=128)
    grads["s5_d"] = dd
    dy = _to_groups(dys, "s5_regroup_dy")
    dsp_re, dsp_im = _s5_bwd_state(prm, dy, "s5_bwd_state")
    g_re_c, g_im_c, dlam_re_c, dlam_im_c = _s5_carry_bwd(_states_to_cols(dsp_re), _states_to_cols(dsp_im), sp_re_c, sp_im_c,
                                                         lam_re_c, lam_im_c, "s5_carry_bwd")
    s5g = _s5_bwd_main(prm, u, sp_re, sp_im, dy, _cols_to_states(g_re_c, ng), _cols_to_states(g_im_c, ng),
                       dlam_re_c.reshape(ng, 1, S5_STATE), dlam_im_c.reshape(ng, 1, S5_STATE), "s5_bwd_main")
    grads["s5_a_re"] = s5g[0].reshape(s5_a_re.shape)
    grads["s5_a_im"] = s5g[1].reshape(s5_a_im.shape)
    grads["s5_log_dt"] = s5g[2].reshape(s5_log_dt.shape)
    grads["s5_b_re"] = s5g[3].transpose(0, 2, 1)[None]
    grads["s5_b_im"] = s5g[4].transpose(0, 2, 1)[None]
    grads["s5_c_re"] = s5g[5][None]
    grads["s5_c_im"] = s5g[6][None]
    dh2_c = _from_groups(s5g[7], "s5_ungroup_du")

    dh1, gf0 = _ffn_block_bwd(saved_f0, [_whole(dh2_a), _whole(dh2_b), _whole(dh2_c)], full["ffn_w_in"][0],
                              full["ffn_w_out"][0], full["ple_gate_w"][0], ln2_g[0:1], ln2_b[0:1], alpha, "l0")
    for k in ("ln2_g", "ln2_b"):
        grads[k] = jnp.concatenate([gf0[k], gf1[k]])

    shards = {}

    def take(items, received):
        for (key, _), recv in zip(items, received):
            shards[key] = _sum_slots(recv, f"sum_grads_{key[0]}_{key[1]}")

    ffn_names = ("ffn_w_in", "ffn_w_out", "ple_w", "ple_gate_w")
    items_l1 = [(("s5_w_glu", 0), _owner_rows("s5_w_glu", dw_glu))] + [((k, 1), _owner_rows(k, gf1[k])) for k in ffn_names]
    items_l0 = [((k, 0), _owner_rows(k, gf0[k])) for k in ffn_names]
    dh0_a, dmix, dg1, db1 = _rowwise_bwd(_ln_mix_fn(alpha), [_whole(h0), _whole(mix0)], [g1, b1], [[_whole(a) for a in dh1]],
                                         need=[True, True], drow=[(d, F32), (d, BF16)], name="ln1_bwd_l0", tq=128)
    grads["ln1_g"] = jnp.concatenate([dg1, dg3])
    grads["ln1_b"] = jnp.concatenate([db1, db3])
    dw_out = _mm(att, dmix, ta=True, out_dtype=BF16, name="attn_out_dw")
    datt = _mm(dmix, w_out, tb=True, name="attn_out_dx")
    items_l0.append((("attn_w_out", 0), _owner_rows("attn_w_out", dw_out)))
    small_early = [n for n in WEIGHTS if n not in BIG_WEIGHTS and n not in ("mla_q_norm", "mla_kv_norm")]
    send_small, where_small = _pack([_by_owner(n, grads[n]).astype(F32) for n in small_early], 1)

    dmerge = _rowwise_bwd(_merge_fn, merge_rows, [], [[(datt, dw, 1)]], need=[True] * 6, drow=[(dw, F32)] * 6,
                          name="merge_bwd", tq=128)
    dq_s = dk_s = dv_s = None
    for k, (dil, zc, o_c, l_c, _, _) in enumerate(band):
        do_c, dl_c = to_classes(dmerge[k], dil), to_classes(dmerge[3 + k], dil)
        dq_c, dk_c, dv_c = _band_bwd(zc, o_c, l_c, do_c, dl_c, ndh, dil, f"band_bwd_d{dil}")
        dq_n, dk_n, dv_n = from_classes(dq_c, dil), from_classes(dk_c, dil), from_classes(dv_c, dil)
        dq_s = dq_n if dq_s is None else dq_s + dq_n
        dk_s = dk_n if dk_s is None else dk_s + dk_n
        dv_s = dv_n if dv_s is None else dv_s + dv_n

    dqcat, delta, recv_l1 = _mla_bwd_q(qcat, kcat, kvf, datt, out_a, lse_a, nh, "mla_bwd_q",
                                       side=("exchange", [rows for _, rows in items_l1]))
    take(items_l1, recv_l1)
    to_row = lambda a: a[:, ::LANE].T.reshape(nh, 1, t)
    dkcat, dv, recv_l0 = _mla_bwd_kv(qcat, kcat, kvf, datt[:, : nh * LANE].astype(BF16), to_row(lse_a), to_row(delta), nh,
                                     "mla_bwd_kv", side=("exchange", [rows for _, rows in items_l0] + [send_small]))
    take(items_l0, recv_l0[:-1])
    sum_small = _sum_slots(recv_l0[-1], "sum_grads_small")
    halves = lambda a, k: a.reshape(t, nh, 2, LANE)[:, :, k].reshape(t, nh * LANE)
    dqn, dqp, dkn, dkp = halves(dqcat, 0), halves(dqcat, 1), halves(dkcat, 0), halves(dkcat, 1)
    dq_pe, dk_pe = _rowwise(_rope_bwd_fn(nh), [_whole(dqp), _whole(dkp), _whole(cos), _whole(sin)], [],
                            [(nh * LANE, nh * LANE, BF16), (LANE, LANE, BF16)], name="rope_bwd", tq=256)
    dqf = jnp.concatenate([dqn.astype(BF16), dq_pe], axis=1)
    dkvf = jnp.concatenate([dkn, dv], axis=1).astype(BF16)
    dwqp = _mm(qn, dqf, ta=True, name="q_up_dw")
    dqn_in = _mm(dqf, wqp, tb=True, name="q_up_dx")
    dwkvp = _mm(kvn, dkvf, ta=True, name="kv_up_dw")
    dkvn_in = _mm(dkvf, wkvp, tb=True, name="kv_up_dx")
    dql, dkvl, dgq, dgk = _rowwise_bwd(_rms_fn, [(z, ql, 0), (z, kvl, 1)], [gq, gk], [[_whole(dqn_in)], [_whole(dkvn_in)]],
                                       need=[True, True], drow=[(ql, BF16), (kvl, BF16)], name="rms_bwd", tq=256)
    grads["mla_q_norm"], grads["mla_kv_norm"] = dgq, dgk
    dz_lat = jnp.concatenate([dql, dkvl, dk_pe], axis=1)
    dz_dil = jnp.concatenate([dq_s, dk_s, dv_s], axis=1).astype(BF16)
    dw_lat_t = _mm(dz_lat, h0b, ta=True, out_dtype=BF16, name="attn_in_lat_dw")
    dw_dil_t = _mm(dz_dil, h0b, ta=True, out_dtype=BF16, name="attn_in_dil_dw")
    dh0_b = _mm(dz_lat, w_lat_t, name="attn_in_lat_dx")
    dh0_c = _mm(dz_dil, w_dil_t, name="attn_in_dil_dx")
    grad_x = _addn([dh0_a, dh0_b, dh0_c], "grad_x")[None]
    dw_in_t = jnp.concatenate([dw_lat_t[: lat + ROPE_DIM], dw_dil_t], axis=0)
    dwq_n = dwqp[:, : nh * LANE].reshape(ql, nh, NOPE_DIM)
    dwq_r = dwqp[:, nh * LANE :].reshape(ql, nh, LANE)[:, :, :ROPE_DIM]
    dwq = jnp.concatenate([dwq_n, dwq_r], axis=2).reshape(ql, nh * (NOPE_DIM + ROPE_DIM))
    dwkv_k = dwkvp[:, : nh * LANE].reshape(kvl, nh, NOPE_DIM)
    dwkv_v = dwkvp[:, nh * LANE :].reshape(kvl, nh, V_DIM)
    dwkv = jnp.concatenate([dwkv_k, dwkv_v], axis=2).reshape(kvl, nh * (NOPE_DIM + V_DIM))

    items_att = [(("attn_w_in", 0), dw_in_t.reshape(N_DEV, -1, PACK_COLS)),
                 (("mla_w_q_b", 0), _owner_rows("mla_w_q_b", dwq.astype(BF16))),
                 (("mla_w_kv_b", 0), _owner_rows("mla_w_kv_b", dwkv.astype(BF16)))]
    small_late = ["mla_q_norm", "mla_kv_norm"]
    send_late, where_late = _pack([_by_owner(n, grads[n]).astype(F32) for n in small_late], 1)
    received = _exchange_call("exchange", [rows for _, rows in items_att] + [send_late], "exchange_grads_last")
    take(items_att, received[:-1])
    sum_late = _sum_slots(received[-1], "sum_grads_norms")

    g_out, d_out, m_out, v_out = [], [], [], []
    for n in WEIGHTS:
        if n == "attn_w_in":
            g = shards[(n, 0)].reshape(local[n].shape[2], d).T[None]
        elif n in BIG_WEIGHTS:
            g = jnp.stack([shards[(n, layer)] for layer in range(local[n].shape[0])]).reshape(local[n].shape)
        elif n in small_late:
            g = _unpack(sum_late, 0, where_late[small_late.index(n)]).reshape(local[n].shape)
        else:
            g = _unpack(sum_small, 0, where_small[small_early.index(n)]).reshape(local[n].shape)
        dlt, mn, vn = _adamw(local[n], g, mom_m[n], mom_v[n], f"adamw_{n}")
        g_out.append(g)
        d_out.append(dlt)
        m_out.append(mn)
        v_out.append(vn)
    return (loss, grad_x, *g_out, *d_out, *m_out, *v_out)
```

```python
import functools
import math

import numpy as np
import jax
import jax.numpy as jnp
from jax import lax
from jax.experimental import pallas as pl
from jax.experimental.pallas import tpu as pltpu

F32 = jnp.float32
BF16 = jnp.bfloat16

NOPE_DIM, ROPE_DIM, V_DIM = 128, 64, 128
DIL_HEAD_DIM = 128
DIL_BRANCHES = ((128, 1), (512, 4), (2048, 16))
BLK = 128
LANE = 128
ROPE_THETA = 10000.0
S5_GROUP, S5_STATE = 16, 64
S5_CHUNK = 32
NEG = -1e30
ADAM_LR, ADAM_B1, ADAM_B2, ADAM_EPS, ADAM_WD, ADAM_STEP = 0.001, 0.9, 0.999, 1e-08, 0.01, 10
N_DEV = 8
AXES = ("x", "y", "c")
VMEM_LIMIT = 56 * 1024 * 1024
MM_VMEM_BUDGET = 40 * 1024 * 1024
PACK_COLS = 1024
PACK_ROW_TILE = 16
HI = lax.Precision.HIGH


def _tile(n, cands):
    for c in cands:
        if n % c == 0:
            return c
    return n


def _params(sem=None):
    return pltpu.CompilerParams(dimension_semantics=sem, vmem_limit_bytes=VMEM_LIMIT)


def _dot(a, b, dims, prec=None):
    return lax.dot_general(a, b, (dims, ((), ())), preferred_element_type=F32, precision=prec)


NN = ((1,), (0,))
NT = ((1,), (1,))
TN = ((0,), (0,))


def _mm(a, b, *, ta=False, tb=False, out_dtype=F32, name):
    (k, m) = a.shape if ta else a.shape[::-1]
    (n, k2) = b.shape if tb else b.shape[::-1]
    assert k == k2, (a.shape, b.shape, ta, tb)
    tm = _tile(m, (1024, 512, 256, 128))
    tn = _tile(n, (1024, 512, 384, 256, 128))
    sa, sb, so = a.dtype.itemsize, b.dtype.itemsize, jnp.dtype(out_dtype).itemsize

    def fits(tk):
        return 2 * (tm * tk * sa + tk * tn * sb) + 2 * tm * tn * so + 4 * tm * tn <= MM_VMEM_BUDGET

    tk = next((c for c in (2816, 2048, 1536, 1408, 1152, 1024, 768, 512, 384, 256, 128) if k % c == 0 and fits(c)), k)
    nk = k // tk
    dims = (((0,) if ta else (1,)), ((1,) if tb else (0,)))

    def body(a_ref, b_ref, o_ref, *acc):
        part = _dot(a_ref[...].astype(BF16), b_ref[...].astype(BF16), dims)
        if nk == 1:
            o_ref[...] = part.astype(o_ref.dtype)
            return
        (acc_ref,) = acc
        kk = pl.program_id(2)

        @pl.when(kk == 0)
        def _():
            acc_ref[...] = part

        @pl.when(kk > 0)
        def _():
            acc_ref[...] += part

        @pl.when(kk == nk - 1)
        def _():
            o_ref[...] = acc_ref[...].astype(o_ref.dtype)

    a_spec = pl.BlockSpec((tk, tm), lambda i, j, kk: (kk, i)) if ta else pl.BlockSpec((tm, tk), lambda i, j, kk: (i, kk))
    b_spec = pl.BlockSpec((tn, tk), lambda i, j, kk: (j, kk)) if tb else pl.BlockSpec((tk, tn), lambda i, j, kk: (kk, j))
    return pl.pallas_call(
        body,
        name=name,
        grid=(m // tm, n // tn, nk),
        in_specs=[a_spec, b_spec],
        out_specs=pl.BlockSpec((tm, tn), lambda i, j, kk: (i, j)),
        out_shape=jax.ShapeDtypeStruct((m, n), out_dtype),
        scratch_shapes=[pltpu.VMEM((tm, tn), F32)] if nk > 1 else [],
        compiler_params=_params(("parallel", "parallel", "arbitrary")),
    )(a, b)


def _row_spec(spec, tq):
    _, w, cb = spec
    return pl.BlockSpec((tq, w), lambda i, j: (i, cb + j))


def _full_spec(p):
    return pl.BlockSpec(p.shape, lambda i, j: (0,) * p.ndim)


def _rowwise(fn, rows, pars, outs, *, name, tq, ncol=1):
    t = rows[0][0].shape[0]
    tq = _tile(t, (tq, 128, 64, 32, 16, 8))
    nr, npar = len(rows), len(pars)

    def body(*refs):
        vals = [r[...].astype(F32) for r in refs[: nr + npar]]
        res = fn(*vals)
        for o, r in zip(refs[nr + npar :], res):
            o[...] = r.astype(o.dtype)

    res = pl.pallas_call(
        body,
        name=name,
        grid=(t // tq, ncol),
        in_specs=[_row_spec(s, tq) for s in rows] + [_full_spec(p) for p in pars],
        out_specs=[pl.BlockSpec((tq, w), lambda i, j: (i, j)) for (_, w, _) in outs],
        out_shape=[jax.ShapeDtypeStruct((t, wt), dt) for (wt, _, dt) in outs],
        compiler_params=_params(("parallel", "parallel")),
    )(*[s[0] for s in rows], *pars)
    return res


def _rowwise_bwd(fn, rows, pars, cots, *, need, drow, name, tq, ncol=1):
    t = rows[0][0].shape[0]
    tq = _tile(t, (tq, 128, 64, 32, 16, 8))
    nr, npar = len(rows), len(pars)
    assert ncol == 1 or npar == 0
    flat_cots = [s for c in cots for s in c]
    ncot = len(flat_cots)
    want = [k for k in range(nr) if need[k]]

    def body(*refs):
        vals = [r[...].astype(F32) for r in refs[: nr + npar]]
        cref = refs[nr + npar : nr + npar + ncot]
        oref = refs[nr + npar + ncot :]
        cvals, pos = [], 0
        for c in cots:
            acc = cref[pos][...].astype(F32)
            for q in range(1, len(c)):
                acc = acc + cref[pos + q][...].astype(F32)
            cvals.append(acc)
            pos += len(c)

        def closed(*diff):
            full = list(vals)
            for k, dv in zip(want + list(range(nr, nr + npar)), diff):
                full[k] = dv
            return tuple(fn(*full))

        diff_in = [vals[k] for k in want] + vals[nr:]
        _, vjp = jax.vjp(closed, *diff_in)
        grads = vjp(tuple(cvals))
        for q in range(len(want)):
            oref[q][...] = grads[q].astype(oref[q].dtype)
        if npar:
            first = pl.program_id(0) == 0

            @pl.when(first)
            def _():
                for q in range(npar):
                    oref[len(want) + q][...] = jnp.zeros_like(oref[len(want) + q])

            for q in range(npar):
                oref[len(want) + q][...] += grads[len(want) + q]

    out_specs = [pl.BlockSpec((tq, rows[k][1]), lambda i, j, cb=rows[k][2]: (i, j)) for k in want]
    out_specs += [_full_spec(p) for p in pars]
    out_shape = [jax.ShapeDtypeStruct((t, wt), dt) for (wt, dt) in drow]
    out_shape += [jax.ShapeDtypeStruct(p.shape, F32) for p in pars]
    return pl.pallas_call(
        body,
        name=name,
        grid=(t // tq, ncol),
        in_specs=[_row_spec(s, tq) for s in rows] + [_full_spec(p) for p in pars] + [_row_spec(s, tq) for s in flat_cots],
        out_specs=out_specs,
        out_shape=out_shape,
        compiler_params=_params(("arbitrary", "arbitrary") if npar else ("parallel", "parallel")),
    )(*[s[0] for s in rows], *pars, *[s[0] for s in flat_cots])


def _whole(a, w=None):
    return (a, a.shape[1] if w is None else w, 0)


def _rms_fn(ql, kvl, gq, gk):
    def one(x, g):
        return x * lax.rsqrt(jnp.mean(x * x, -1, keepdims=True) + 1e-6) * g

    return one(ql, gq), one(kvl, gk)


def _layernorm(s, g, b):
    mu = jnp.mean(s, -1, keepdims=True)
    d = s - mu
    var = jnp.mean(d * d, -1, keepdims=True)
    return d * lax.rsqrt(var + 1e-5) * g + b


def _ln_mix_fn(alpha):
    def fn(h, mix, g, b):
        return (_layernorm(alpha * h + mix, g, b),)

    return fn


def _ln_glu_fn(alpha):
    def fn(h, val, gate, g, b):
        return (_layernorm(alpha * h + val * jax.nn.sigmoid(gate), g, b),)

    return fn


def _ln_ffn_fn(alpha):
    def fn(h, f, pw, gate, g, b):
        return (_layernorm(alpha * h + f + pw * jax.nn.sigmoid(gate), g, b),)

    return fn


def _swiglu_fn(g, u):
    return (jax.nn.silu(g) * u,)


def _gelu_fn(ys, h, d):
    return (jax.nn.gelu(ys + d * h),)


def _merge_fn(o1, o2, o3, l1, l2, l3):
    m = jnp.maximum(jnp.maximum(l1, l2), l3)
    e1, e2, e3 = jnp.exp(l1 - m), jnp.exp(l2 - m), jnp.exp(l3 - m)
    return ((e1 * o1 + e2 * o2 + e3 * o3) / (e1 + e2 + e3),)


def _swap_halves(t):
    w = t.shape[1]
    lane = lax.broadcasted_iota(jnp.int32, t.shape, 1) % LANE
    up = jnp.where(lane < ROPE_DIM, pltpu.roll(t, ROPE_DIM // 2, 1), 0.0)
    return jnp.where(lane < ROPE_DIM // 2, pltpu.roll(t, w - ROPE_DIM // 2, 1), up)


def _rope_fwd_fn(nh):
    def fn(qpe, kpe, cos, sin):
        cq, sq = jnp.tile(cos, (1, nh)), jnp.tile(sin, (1, nh))
        return qpe * cq + _swap_halves(qpe) * sq, kpe * cos + _swap_halves(kpe) * sin

    return fn


def _rope_bwd_fn(nh):
    def fn(dq, dk_heads, cos, sin):
        cq, sq = jnp.tile(cos, (1, nh)), jnp.tile(sin, (1, nh))
        dk = dk_heads[:, :LANE]
        for h in range(1, nh):
            dk = dk + dk_heads[:, h * LANE : (h + 1) * LANE]
        return dq * cq + _swap_halves(dq * sq), dk * cos + _swap_halves(dk * sin)

    return fn


def _rope_tables(positions, name):
    t = positions.shape[0]
    tq = _tile(t, (512, 128, 8))
    half = ROPE_DIM // 2

    def body(p_ref, c_ref, s_ref):
        lane = lax.broadcasted_iota(jnp.int32, (tq, LANE), 1)
        idx = (lane % half).astype(F32)
        inv_freq = jnp.exp(idx * (-math.log(ROPE_THETA) / half))
        ang = p_ref[...].astype(F32) * inv_freq
        live = lane < ROPE_DIM
        c_ref[...] = jnp.where(live, jnp.cos(ang), 0.0)
        s_ref[...] = jnp.where(live, jnp.where(lane < half, -jnp.sin(ang), jnp.sin(ang)), 0.0)

    return pl.pallas_call(
        body,
        name=name,
        grid=(t // tq,),
        in_specs=[pl.BlockSpec((tq, 1), lambda i: (i, 0))],
        out_specs=[pl.BlockSpec((tq, LANE), lambda i: (i, 0))] * 2,
        out_shape=[jax.ShapeDtypeStruct((t, LANE), F32)] * 2,
        compiler_params=_params(("parallel",)),
    )(positions)


def _loss_kernel(y, target, name):
    t, d = y.shape
    tq = _tile(t, (256, 128, 8))

    def body(y_ref, t_ref, dy_ref, l_ref):
        @pl.when(pl.program_id(0) == 0)
        def _():
            l_ref[...] = jnp.zeros_like(l_ref)

        e = y_ref[...] - t_ref[...]
        dy_ref[...] = e * (1.0 / d)
        l_ref[...] += jnp.sum(e * e) * (0.5 / d)

    return pl.pallas_call(
        body,
        name=name,
        grid=(t // tq,),
        in_specs=[pl.BlockSpec((tq, d), lambda i: (i, 0))] * 2,
        out_specs=[pl.BlockSpec((tq, d), lambda i: (i, 0)), pl.BlockSpec((8, LANE), lambda i: (0, 0))],
        out_shape=[jax.ShapeDtypeStruct((t, d), F32), jax.ShapeDtypeStruct((8, LANE), F32)],
        compiler_params=_params(("arbitrary",)),
    )(y, target)


def _addn(arrs, name):
    w = arrs[0].shape[1]
    return _rowwise(lambda *v: (functools.reduce(lambda p, q: p + q, v),), [_whole(a) for a in arrs], [], [(w, w, F32)],
                    name=name, tq=256)[0]


def _mla_tiles(t):
    tq = _tile(t, (512, 256, 128))
    return tq, t // tq


def _call_with_side(body, side, n_in, grid, *, name, in_specs, out_specs, out_shape):
    if side is None:
        return pl.pallas_call(body, name=name, grid=grid, in_specs=in_specs, out_specs=out_specs, out_shape=out_shape,
                              compiler_params=_params(("parallel", "parallel")))
    n_out, n = len(out_specs), len(side[1])

    def wrapped(*refs):
        ins, side_ins = refs[:n_in], refs[n_in : n_in + n]
        outs, side_outs = refs[n_in + n : n_in + n + n_out], refs[n_in + n + n_out : n_in + 2 * n + n_out]
        done = _run_side(side, side_ins, side_outs, refs[n_in + 2 * n + n_out :], grid[0], grid[1])
        body(*ins, *outs)
        done()

    call = pl.pallas_call(wrapped, name=name, grid=grid, in_specs=in_specs + [ANY_SPEC] * n,
                          out_specs=out_specs + [ANY_SPEC] * n, out_shape=out_shape + _side_out_shapes(side),
                          scratch_shapes=_comm_sems(n), compiler_params=_params(("arbitrary", "arbitrary")))

    def run(*args):
        res = call(*args, *side[1])
        return (*res[:n_out], list(res[n_out:]))

    return run


def _mla_fwd(qf, qpe, kvf, kpe, nh, name, side=None):
    t = qf.shape[0]
    tq, nq = _mla_tiles(t)
    scale = (NOPE_DIM + ROPE_DIM) ** -0.5

    def body(qn_ref, qp_ref, kn_ref, kp_ref, v_ref, o_ref, lse_ref):
        i = pl.program_id(1)
        qn = qn_ref[...].astype(BF16)
        qp = qp_ref[...]

        def step(j, carry, masked):
            m, l, acc = carry
            ks = pl.ds(pl.multiple_of(j * tq, tq), tq)
            s = (_dot(qn, kn_ref[ks, :], NT) + _dot(qp, kp_ref[ks, :], NT)) * scale
            if masked:
                row = lax.broadcasted_iota(jnp.int32, (tq, tq), 0)
                col = lax.broadcasted_iota(jnp.int32, (tq, tq), 1)
                s = jnp.where(col <= row, s, NEG)
            m_new = jnp.maximum(m, jnp.max(s, -1, keepdims=True))
            p = jnp.exp(s - m_new)
            a = jnp.exp(m - m_new)
            return m_new, a * l + jnp.sum(p, -1, keepdims=True), a * acc + _dot(p.astype(BF16), v_ref[ks, :], NN)

        init = (jnp.full((tq, 1), NEG, F32), jnp.zeros((tq, 1), F32), jnp.zeros((tq, V_DIM), F32))
        carry = lax.fori_loop(0, i, lambda j, c: step(j, c, False), init)
        m, l, acc = step(i, carry, True)
        o_ref[...] = (acc / l).astype(o_ref.dtype)
        lse_ref[...] = jnp.broadcast_to(m + jnp.log(l), (tq, LANE))

    blk = lambda h, i: (i, h)
    return _call_with_side(
        body, side, 5, (nh, nq),
        name=name,
        in_specs=[
            pl.BlockSpec((tq, LANE), blk),
            pl.BlockSpec((tq, LANE), blk),
            pl.BlockSpec((t, LANE), lambda h, i: (0, h)),
            pl.BlockSpec((t, LANE), lambda h, i: (0, 0)),
            pl.BlockSpec((t, LANE), lambda h, i: (0, nh + h)),
        ],
        out_specs=[pl.BlockSpec((tq, LANE), blk), pl.BlockSpec((tq, LANE), blk)],
        out_shape=[jax.ShapeDtypeStruct((t, nh * LANE), BF16), jax.ShapeDtypeStruct((t, nh * LANE), F32)],
    )(qf, qpe, kvf, kpe, kvf)


def _mla_bwd_q(qf, qpe, kvf, kpe, do, o, lse, nh, name, side=None):
    t = qf.shape[0]
    tq, nq = _mla_tiles(t)
    scale = (NOPE_DIM + ROPE_DIM) ** -0.5

    def body(qn_ref, qp_ref, kn_ref, kp_ref, v_ref, do_ref, o_ref, lse_ref, dqn_ref, dqp_ref, dl_ref):
        i = pl.program_id(1)
        qn = qn_ref[...].astype(BF16)
        qp = qp_ref[...]
        dof = do_ref[...].astype(F32)
        dob = dof.astype(BF16)
        delta = jnp.sum(dof * o_ref[...].astype(F32), -1, keepdims=True)
        lse1 = lse_ref[:, :1]

        def step(j, carry, masked):
            dqn, dqp = carry
            ks = pl.ds(pl.multiple_of(j * tq, tq), tq)
            kn, kp = kn_ref[ks, :], kp_ref[ks, :]
            s = (_dot(qn, kn, NT) + _dot(qp, kp, NT)) * scale
            p = jnp.exp(s - lse1)
            if masked:
                row = lax.broadcasted_iota(jnp.int32, (tq, tq), 0)
                col = lax.broadcasted_iota(jnp.int32, (tq, tq), 1)
                p = jnp.where(col <= row, p, 0.0)
            dp = _dot(dob, v_ref[ks, :], NT)
            ds = (p * (dp - delta) * scale).astype(BF16)
            return dqn + _dot(ds, kn, NN), dqp + _dot(ds, kp, NN)

        init = (jnp.zeros((tq, LANE), F32), jnp.zeros((tq, LANE), F32))
        carry = lax.fori_loop(0, i, lambda j, c: step(j, c, False), init)
        dqn, dqp = step(i, carry, True)
        dqn_ref[...] = dqn
        dqp_ref[...] = dqp
        dl_ref[...] = jnp.broadcast_to(delta, (tq, LANE))

    blk = lambda h, i: (i, h)
    bs = pl.BlockSpec((tq, LANE), blk)
    return _call_with_side(
        body, side, 8, (nh, nq),
        name=name,
        in_specs=[
            bs,
            bs,
            pl.BlockSpec((t, LANE), lambda h, i: (0, h)),
            pl.BlockSpec((t, LANE), lambda h, i: (0, 0)),
            pl.BlockSpec((t, LANE), lambda h, i: (0, nh + h)),
            bs,
            bs,
            bs,
        ],
        out_specs=[bs, bs, bs],
        out_shape=[jax.ShapeDtypeStruct((t, nh * LANE), F32)] * 3,
    )(qf, qpe, kvf, kpe, kvf, do, o, lse)


def _mla_bwd_kv(qn16, qpe, kvf, kpe, do16, lse_row, delta_row, nh, name, side=None):
    t = qn16.shape[0]
    tq, nq = _mla_tiles(t)
    scale = (NOPE_DIM + ROPE_DIM) ** -0.5

    def body(kn_ref, kp_ref, v_ref, qn_ref, qp_ref, do_ref, lse_ref, dl_ref, dkn_ref, dkp_ref, dv_ref):
        j = pl.program_id(1)
        kn, kp, v = kn_ref[...], kp_ref[...], v_ref[...]

        def step(i, carry, masked):
            dkn, dkp, dv = carry
            qs = pl.ds(pl.multiple_of(i * tq, tq), tq)
            qn, qp, dob = qn_ref[qs, :], qp_ref[qs, :], do_ref[qs, :]
            st = (_dot(kn, qn, NT) + _dot(kp, qp, NT)) * scale
            pt = jnp.exp(st - lse_ref[0, :, qs])
            if masked:
                key = lax.broadcasted_iota(jnp.int32, (tq, tq), 0)
                qry = lax.broadcasted_iota(jnp.int32, (tq, tq), 1)
                pt = jnp.where(key <= qry, pt, 0.0)
            dv = dv + _dot(pt.astype(BF16), dob, NN)
            dpt = _dot(v, dob, NT)
            dst = (pt * (dpt - dl_ref[0, :, qs]) * scale).astype(BF16)
            return dkn + _dot(dst, qn, NN), dkp + _dot(dst, qp, NN), dv

        z = jnp.zeros((tq, LANE), F32)
        carry = step(j, (z, z, z), True)
        dkn, dkp, dv = lax.fori_loop(j + 1, nq, lambda i, c: step(i, c, False), carry)
        dkn_ref[...] = dkn
        dkp_ref[...] = dkp
        dv_ref[...] = dv

    blk = pl.BlockSpec((tq, LANE), lambda h, j: (j, h))
    res = lambda f: pl.BlockSpec((t, LANE), f)
    row = pl.BlockSpec((1, 1, t), lambda h, j: (h, 0, 0))
    return _call_with_side(
        body, side, 8, (nh, nq),
        name=name,
        in_specs=[
            blk,
            pl.BlockSpec((tq, LANE), lambda h, j: (j, 0)),
            pl.BlockSpec((tq, LANE), lambda h, j: (j, nh + h)),
            res(lambda h, j: (0, h)),
            res(lambda h, j: (0, h)),
            res(lambda h, j: (0, h)),
            row,
            row,
        ],
        out_specs=[blk, blk, blk],
        out_shape=[jax.ShapeDtypeStruct((t, nh * LANE), F32)] * 3,
    )(kvf, kpe, kvf, qn16, qpe, do16, lse_row, delta_row)


def _alibi_slopes(n):
    return [float(2.0 ** (-8.0 * i / n)) for i in range(1, n + 1)]


def _window_mask(has_prev):
    qi = lax.broadcasted_iota(jnp.int32, (BLK, 2 * BLK), 0)
    ki = lax.broadcasted_iota(jnp.int32, (BLK, 2 * BLK), 1)
    dist = qi + BLK - ki
    valid = (dist >= 0) & (dist <= BLK) & ((ki >= BLK) | has_prev)
    return valid, dist.astype(F32)


def _band_fwd(zd, nh, dil, name):
    t = zd.shape[0]
    nbc = t // BLK // dil
    scale = DIL_HEAD_DIM ** -0.5
    slopes = _alibi_slopes(nh)
    dw = nh * LANE

    def body(q_ref, kc_ref, kp_ref, vc_ref, vp_ref, o_ref, l_ref):
        mask, dist = _window_mask(pl.program_id(1) > 0)
        for h in range(nh):
            sl = slice(h * LANE, (h + 1) * LANE)
            keys = jnp.concatenate([kp_ref[:, sl], kc_ref[:, sl]], axis=0)
            vals = jnp.concatenate([vp_ref[:, sl], vc_ref[:, sl]], axis=0)
            s = jnp.where(mask, _dot(q_ref[:, sl], keys, NT) * scale - (slopes[h] * dil) * dist, NEG)
            m = jnp.max(s, -1, keepdims=True)
            e = jnp.exp(s - m)
            l = jnp.sum(e, -1, keepdims=True)
            o_ref[:, sl] = _dot((e * (1.0 / l)).astype(BF16), vals, NN)
            l_ref[:, sl] = jnp.broadcast_to(m + jnp.log(l), (BLK, LANE))

    prev = lambda i: jnp.maximum(i - 1, 0)
    spec = lambda f: pl.BlockSpec((BLK, dw), f)
    return pl.pallas_call(
        body,
        name=name,
        grid=(dil, nbc),
        in_specs=[
            spec(lambda r, i: (r * nbc + i, 0)),
            spec(lambda r, i: (r * nbc + i, 1)),
            spec(lambda r, i: (r * nbc + prev(i), 1)),
            spec(lambda r, i: (r * nbc + i, 2)),
            spec(lambda r, i: (r * nbc + prev(i), 2)),
        ],
        out_specs=[spec(lambda r, i: (r * nbc + i, 0))] * 2,
        out_shape=[jax.ShapeDtypeStruct((t, dw), F32)] * 2,
        compiler_params=_params(("parallel", "parallel")),
    )(zd, zd, zd, zd, zd)


def _band_bwd(zd, o, lse, do, dl, nh, dil, name):
    t = zd.shape[0]
    nbc = t // BLK // dil
    scale = DIL_HEAD_DIM ** -0.5
    slopes = _alibi_slopes(nh)
    dw = nh * LANE

    def body(q_ref, k_ref, v_ref, kp_ref, vp_ref, qn_ref, o_ref, l_ref, do_ref, dl_ref, on_ref, ln_ref, don_ref, dln_ref,
             dq_ref, dk_ref, dv_ref):
        mask, dist = _window_mask(pl.program_id(1) > 0)
        mask_n, dist_n = _window_mask(pl.program_id(1) + 1 < nbc)
        mask_n, dist_n = mask_n[:, :BLK], dist_n[:, :BLK]
        for h in range(nh):
            sl = slice(h * LANE, (h + 1) * LANE)
            bias = slopes[h] * dil
            q, k, v, qn = q_ref[:, sl], k_ref[:, sl], v_ref[:, sl], qn_ref[:, sl]
            keys = jnp.concatenate([kp_ref[:, sl], k], axis=0)
            vals = jnp.concatenate([vp_ref[:, sl], v], axis=0)
            dof, donf = do_ref[:, sl], don_ref[:, sl]
            dob, donb = dof.astype(BF16), donf.astype(BF16)
            lse1, lsen1 = l_ref[:, sl][:, :1], ln_ref[:, sl][:, :1]
            adj = jnp.sum(dl_ref[:, sl] - dof * o_ref[:, sl], -1, keepdims=True)
            adjn = jnp.sum(dln_ref[:, sl] - donf * on_ref[:, sl], -1, keepdims=True)
            p = jnp.where(mask, jnp.exp(_dot(q, keys, NT) * scale - bias * dist - lse1), 0.0)
            ds = (p * (_dot(dob, vals, NT) + adj)).astype(BF16)
            dq_ref[:, sl] = _dot(ds, keys, NN) * scale
            pn = jnp.where(mask_n, jnp.exp(_dot(qn, k, NT) * scale - bias * dist_n - lsen1), 0.0)
            dsn = (pn * (_dot(donb, v, NT) + adjn)).astype(BF16)
            both_q = jnp.concatenate([q, qn], axis=0)
            dk_ref[:, sl] = _dot(jnp.concatenate([ds[:, BLK:], dsn], axis=0), both_q, TN) * scale
            dv_ref[:, sl] = _dot(jnp.concatenate([p[:, BLK:].astype(BF16), pn.astype(BF16)], axis=0),
                                 jnp.concatenate([dob, donb], axis=0), TN)

    prev = lambda i: jnp.maximum(i - 1, 0)
    nxt = lambda i: jnp.minimum(i + 1, nbc - 1)
    spec = lambda f: pl.BlockSpec((BLK, dw), f)
    cur, nx = spec(lambda r, i: (r * nbc + i, 0)), spec(lambda r, i: (r * nbc + nxt(i), 0))
    return pl.pallas_call(
        body,
        name=name,
        grid=(dil, nbc),
        in_specs=[cur, spec(lambda r, i: (r * nbc + i, 1)), spec(lambda r, i: (r * nbc + i, 2)),
                  spec(lambda r, i: (r * nbc + prev(i), 1)), spec(lambda r, i: (r * nbc + prev(i), 2)),
                  nx, cur, cur, cur, cur, nx, nx, nx, nx],
        out_specs=[cur] * 3,
        out_shape=[jax.ShapeDtypeStruct((t, dw), F32)] * 3,
        compiler_params=_params(("parallel", "parallel")),
    )(zd, zd, zd, zd, zd, zd, o, lse, do, dl, o, lse, do, dl)


def _s5_ops(a_re, a_im, ldt, bt_re, bt_im, c_re, c_im):
    L, g, p = S5_CHUNK, S5_GROUP, S5_STATE
    dt = jnp.exp(ldt)
    lr, li = a_re * dt, a_im * dt
    er = jnp.exp(lr)
    lam_re, lam_im = er * jnp.cos(li), er * jnp.sin(li)
    nr, ni = lam_re - 1.0, lam_im
    den = a_re * a_re + a_im * a_im
    fr, fi = (nr * a_re + ni * a_im) / den, (ni * a_re - nr * a_im) / den
    bb_re, bb_im = fr * bt_re - fi * bt_im, fr * bt_im + fi * bt_re

    def power(tau):
        mag = jnp.exp(tau * lr)
        return mag * jnp.cos(tau * li), mag * jnp.sin(tau * li)

    step = lax.broadcasted_iota(jnp.int32, (L, 1), 0).astype(F32)

    def outer(pr, pi, mr, mi):
        re = pr[:, None, :] * mr[None] - pi[:, None, :] * mi[None]
        im = pr[:, None, :] * mi[None] + pi[:, None, :] * mr[None]
        return re.reshape(L * g, p), im.reshape(L * g, p)

    half = float(L // 2)
    cp_re, cp_im = outer(*power(step - half), c_re, c_im)
    pb_re, pb_im = outer(*power(half - step), bb_re, bb_im)
    toep = _dot(cp_re, pb_re, NT, HI) - _dot(cp_im, pb_im, NT, HI)
    trow = lax.broadcasted_iota(jnp.int32, (L * g, L * g), 0) // g
    scol = lax.broadcasted_iota(jnp.int32, (L * g, L * g), 1) // g
    toep = jnp.where(trow >= scol, toep, 0.0)
    et_re, et_im = outer(*power(float(L - 1) - step), bb_re, bb_im)
    f_re, f_im = outer(*power(step + 1.0), c_re, c_im)
    big_re, big_im = power(jnp.full((1, 1), float(L), F32))
    return toep, et_re, et_im, f_re, f_im, big_re, big_im


def _s5_y(ops, u, sp_re, sp_im):
    toep, _, _, f_re, f_im, _, _ = ops
    return _dot(u, toep, NT) + _dot(sp_re, f_re, NT, HI) - _dot(sp_im, f_im, NT, HI)


def _s5_group_specs(ng):
    vec = pl.BlockSpec((1, 1, S5_STATE), lambda g: (g, 0, 0))
    one = pl.BlockSpec((1, 1, 1), lambda g: (g, 0, 0))
    mat = pl.BlockSpec((1, S5_GROUP, S5_STATE), lambda g: (g, 0, 0))
    return [vec, vec, one, mat, mat, mat, mat]


def _s5_load(refs):
    return [r[0] for r in refs]


def _s5_local(prm, u, name):
    ng, n, w = u.shape
    p = S5_STATE

    def body(*refs):
        ops = _s5_ops(*_s5_load(refs[:7]))
        uu = refs[7][0]
        refs[8][0] = _dot(uu, ops[1], NN, HI)
        refs[9][0] = _dot(uu, ops[2], NN, HI)
        refs[10][0] = ops[5]
        refs[11][0] = ops[6]

    blk = lambda a, b: pl.BlockSpec((1, a, b), lambda g: (g, 0, 0))
    return pl.pallas_call(
        body,
        name=name,
        grid=(ng,),
        in_specs=_s5_group_specs(ng) + [blk(n, w)],
        out_specs=[blk(n, p), blk(n, p), blk(1, p), blk(1, p)],
        out_shape=[jax.ShapeDtypeStruct((ng, n, p), F32)] * 2 + [jax.ShapeDtypeStruct((ng, 1, p), F32)] * 2,
        compiler_params=_params(("parallel",)),
    )(*prm, u)


def _s5_carry(e_re, e_im, lam_re, lam_im, name):
    n, w = e_re.shape
    cw = _tile(w, (1024, 512, 256, 128))

    def body(er_ref, ei_ref, lr_ref, li_ref, sr_ref, si_ref):
        lr, li = lr_ref[...], li_ref[...]

        def step(k, carry):
            sr, si = carry
            row = pl.ds(k, 1)
            sr_ref[row, :] = sr
            si_ref[row, :] = si
            return lr * sr - li * si + er_ref[row, :], li * sr + lr * si + ei_ref[row, :]

        z = jnp.zeros((1, cw), F32)
        lax.fori_loop(0, n, step, (z, z))

    col = pl.BlockSpec((n, cw), lambda j: (0, j))
    one = pl.BlockSpec((1, cw), lambda j: (0, j))
    return pl.pallas_call(
        body,
        name=name,
        grid=(w // cw,),
        in_specs=[col, col, one, one],
        out_specs=[col, col],
        out_shape=[jax.ShapeDtypeStruct((n, w), F32)] * 2,
        compiler_params=_params(("parallel",)),
    )(e_re, e_im, lam_re, lam_im)


def _s5_carry_bwd(dsp_re, dsp_im, sp_re, sp_im, lam_re, lam_im, name):
    n, w = dsp_re.shape
    cw = _tile(w, (1024, 512, 256, 128))

    def body(dr_ref, di_ref, sr_ref, si_ref, lr_ref, li_ref, gr_ref, gi_ref, dlr_ref, dli_ref):
        lr, li = lr_ref[...], li_ref[...]

        def step(q, carry):
            gr_next, gi_next, dr_next, di_next, alr, ali = carry
            k = n - 1 - q
            row = pl.ds(k, 1)
            gr = dr_next + lr * gr_next + li * gi_next
            gi = di_next - li * gr_next + lr * gi_next
            gr_ref[row, :] = gr
            gi_ref[row, :] = gi
            sr, si = sr_ref[row, :], si_ref[row, :]
            return gr, gi, dr_ref[row, :], di_ref[row, :], alr + gr * sr + gi * si, ali + gi * sr - gr * si

        z = jnp.zeros((1, cw), F32)
        out = lax.fori_loop(0, n, step, (z, z, z, z, z, z))
        dlr_ref[...] = out[4]
        dli_ref[...] = out[5]

    col = pl.BlockSpec((n, cw), lambda j: (0, j))
    one = pl.BlockSpec((1, cw), lambda j: (0, j))
    return pl.pallas_call(
        body,
        name=name,
        grid=(w // cw,),
        in_specs=[col, col, col, col, one, one],
        out_specs=[col, col, one, one],
        out_shape=[jax.ShapeDtypeStruct((n, w), F32)] * 2 + [jax.ShapeDtypeStruct((1, w), F32)] * 2,
        compiler_params=_params(("parallel",)),
    )(dsp_re, dsp_im, sp_re, sp_im, lam_re, lam_im)


def _s5_out(prm, u, sp_re, sp_im, name):
    ng, n, w = u.shape
    p = S5_STATE

    def body(*refs):
        ops = _s5_ops(*_s5_load(refs[:7]))
        refs[10][0] = _s5_y(ops, refs[7][0], refs[8][0], refs[9][0])

    blk = lambda a, b: pl.BlockSpec((1, a, b), lambda g: (g, 0, 0))
    return pl.pallas_call(
        body,
        name=name,
        grid=(ng,),
        in_specs=_s5_group_specs(ng) + [blk(n, w), blk(n, p), blk(n, p)],
        out_specs=blk(n, w),
        out_shape=jax.ShapeDtypeStruct((ng, n, w), F32),
        compiler_params=_params(("parallel",)),
    )(*prm, u, sp_re, sp_im)


def _s5_bwd_state(prm, dy, name):
    ng, n, w = dy.shape
    p = S5_STATE

    def body(*refs):
        ops = _s5_ops(*_s5_load(refs[:7]))
        d = refs[7][0]
        refs[8][0] = _dot(d, ops[3], NN, HI)
        refs[9][0] = -_dot(d, ops[4], NN, HI)

    blk = lambda a, b: pl.BlockSpec((1, a, b), lambda g: (g, 0, 0))
    return pl.pallas_call(
        body,
        name=name,
        grid=(ng,),
        in_specs=_s5_group_specs(ng) + [blk(n, w)],
        out_specs=[blk(n, p), blk(n, p)],
        out_shape=[jax.ShapeDtypeStruct((ng, n, p), F32)] * 2,
        compiler_params=_params(("parallel",)),
    )(*prm, dy)


def _s5_bwd_main(prm, u, sp_re, sp_im, dy, g_re, g_im, dlam_re, dlam_im, name):
    ng, n, w = u.shape
    p = S5_STATE

    def body(*refs):
        prm_v = _s5_load(refs[:7])
        uu, sr, si, d, gr, gi, dlr, dli = [r[0] for r in refs[7:15]]
        ops, ops_vjp = jax.vjp(_s5_ops, *prm_v)
        cots = (_dot(d, uu, TN), _dot(uu, gr, TN, HI), _dot(uu, gi, TN, HI), _dot(d, sr, TN, HI), -_dot(d, si, TN, HI),
                dlr, dli)
        grads = ops_vjp(cots)
        for q in range(7):
            refs[15 + q][0] = grads[q]
        refs[22][0] = _dot(d, ops[0], NN) + _dot(gr, ops[1], NT, HI) + _dot(gi, ops[2], NT, HI)

    blk = lambda a, b: pl.BlockSpec((1, a, b), lambda g: (g, 0, 0))
    prm_specs = _s5_group_specs(ng)
    return pl.pallas_call(
        body,
        name=name,
        grid=(ng,),
        in_specs=prm_specs + [blk(n, w), blk(n, p), blk(n, p), blk(n, w), blk(n, p), blk(n, p), blk(1, p), blk(1, p)],
        out_specs=prm_specs + [blk(n, w)],
        out_shape=[jax.ShapeDtypeStruct(a.shape, F32) for a in prm] + [jax.ShapeDtypeStruct((ng, n, w), F32)],
        compiler_params=_params(("parallel",)),
    )(*prm, u, sp_re, sp_im, dy, g_re, g_im, dlam_re, dlam_im)


SLAB_GROUPS = LANE // S5_GROUP
CHUNK_TILES = S5_CHUNK * S5_GROUP // LANE


def _segment_transpose(tiles):
    seg = lax.broadcasted_iota(jnp.int32, (1, LANE), 1) // S5_GROUP
    tiles = list(tiles)
    stride = 1
    while stride < SLAB_GROUPS:
        upper = (seg & stride) != 0
        shift = S5_GROUP * stride
        new = list(tiles)
        for i in range(SLAB_GROUPS):
            if i & stride:
                continue
            j = i + stride
            new[i] = jnp.where(upper, pltpu.roll(tiles[j], shift, 1), tiles[i])
            new[j] = jnp.where(upper, tiles[j], pltpu.roll(tiles[i], LANE - shift, 1))
        tiles = new
        stride *= 2
    return tiles


def _to_groups(h, name):
    t, d = h.shape
    n = t // S5_CHUNK

    def body(x_ref, o_ref):
        for a in range(CHUNK_TILES):
            rows = [x_ref[pl.ds(SLAB_GROUPS * a + b, n, stride=S5_CHUNK), :] for b in range(SLAB_GROUPS)]
            for g, tile in enumerate(_segment_transpose(rows)):
                o_ref[g, :, a * LANE : (a + 1) * LANE] = tile

    return pl.pallas_call(
        body,
        name=name,
        grid=(d // LANE,),
        in_specs=[pl.BlockSpec((t, LANE), lambda q: (0, q))],
        out_specs=pl.BlockSpec((SLAB_GROUPS, n, S5_CHUNK * S5_GROUP), lambda q: (q, 0, 0)),
        out_shape=jax.ShapeDtypeStruct((d // S5_GROUP, n, S5_CHUNK * S5_GROUP), F32),
        compiler_params=_params(("parallel",)),
    )(h)


def _from_groups(y, name):
    ng, n, w = y.shape
    t, d = n * S5_CHUNK, ng * S5_GROUP

    def body(y_ref, x_ref):
        for a in range(CHUNK_TILES):
            tiles = [y_ref[g, :, a * LANE : (a + 1) * LANE] for g in range(SLAB_GROUPS)]
            for b, row in enumerate(_segment_transpose(tiles)):
                x_ref[pl.ds(SLAB_GROUPS * a + b, n, stride=S5_CHUNK), :] = row

    return pl.pallas_call(
        body,
        name=name,
        grid=(d // LANE,),
        in_specs=[pl.BlockSpec((SLAB_GROUPS, n, w), lambda q: (q, 0, 0))],
        out_specs=pl.BlockSpec((t, LANE), lambda q: (0, q)),
        out_shape=jax.ShapeDtypeStruct((t, d), F32),
        compiler_params=_params(("parallel",)),
    )(y)


def _states_to_cols(e):
    ng, n, p = e.shape
    return e.transpose(1, 0, 2).reshape(n, ng * p)


def _cols_to_states(s, ng):
    n = s.shape[0]
    return s.reshape(n, ng, S5_STATE).transpose(1, 0, 2)


def _me():
    return lax.axis_index("x"), lax.axis_index("y"), lax.axis_index("c")


def _flip(v, bit):
    return 1 - v if bit else v


COPIES = N_DEV - 1
ANY_SPEC = pl.BlockSpec(memory_space=pl.ANY)


def _comm_sems(n):
    return [pltpu.SemaphoreType.DMA((COPIES * n,)), pltpu.SemaphoreType.DMA((COPIES * n,)), pltpu.SemaphoreType.DMA((n,))]


def _gather_phases(x_refs, out_refs, send_sems, recv_sems, local_sems):
    mx, my, mc = _me()
    me, sibling = (mx, my, mc), (mx, my, 1 - mc)
    chips = [(1 - mx, my), (mx, 1 - my), (1 - mx, 1 - my)]
    arrays = range(len(x_refs))

    def slot(a, px, py, pc):
        return out_refs[a].at[4 * px + 2 * py + pc]

    def copy(a, k, block, to, from_input=False):
        return pltpu.make_async_remote_copy(
            src_ref=x_refs[a] if from_input else slot(a, *block),
            dst_ref=slot(a, *block),
            send_sem=send_sems.at[COPIES * a + k],
            recv_sem=recv_sems.at[COPIES * a + k],
            device_id=to,
            device_id_type=pl.DeviceIdType.MESH,
        )

    mine = [pltpu.make_async_copy(x_refs[a], slot(a, *me), local_sems.at[a]) for a in arrays]
    first = [copy(a, 0, me, sibling, True) for a in arrays]
    first += [copy(a, 1 + j, me, (*chip, mc), True) for a in arrays for j, chip in enumerate(chips)]
    passed = {(a, j): copy(a, 4 + j, (*chip, mc), sibling) for a in arrays for j, chip in enumerate(chips)}

    def start():
        for cp in mine + first:
            cp.start()

    def relay():
        for j, chip in enumerate(chips):
            for a in arrays:
                copy(a, 1 + j, (*chip, mc), me).wait_recv()
                passed[(a, j)].start()

    def finish():
        for a in arrays:
            copy(a, 0, sibling, me).wait_recv()
            for j, chip in enumerate(chips):
                copy(a, 4 + j, (*chip, 1 - mc), me).wait_recv()
        for cp in first + list(passed.values()):
            cp.wait_send()
        for cp in mine:
            cp.wait()

    return start, relay, finish


def _exchange_phases(g_refs, out_refs, send_sems, recv_sems, local_sems):
    mx, my, mc = _me()
    mine_idx = 4 * mx + 2 * my + mc
    arrays = range(len(g_refs))
    own = [pltpu.make_async_copy(g_refs[a].at[mine_idx], out_refs[a].at[mine_idx], local_sems.at[a]) for a in arrays]
    copies = []
    for m in range(1, N_DEV):
        px, py, pc = _flip(mx, m & 4), _flip(my, m & 2), _flip(mc, m & 1)
        for a in arrays:
            copies.append(
                pltpu.make_async_remote_copy(
                    src_ref=g_refs[a].at[4 * px + 2 * py + pc],
                    dst_ref=out_refs[a].at[mine_idx],
                    send_sem=send_sems.at[COPIES * a + m - 1],
                    recv_sem=recv_sems.at[COPIES * a + m - 1],
                    device_id=(px, py, pc),
                    device_id_type=pl.DeviceIdType.MESH,
                )
            )

    def start():
        for cp in own + copies:
            cp.start()

    def finish():
        for cp in copies:
            cp.wait_recv()
        for cp in copies:
            cp.wait_send()
        for cp in own:
            cp.wait()

    return start, None, finish


def _side_out_shapes(side):
    kind, arrs = side
    return [jax.ShapeDtypeStruct((N_DEV,) + a.shape if kind == "gather" else a.shape, a.dtype) for a in arrs]


def _run_side(side, x_refs, out_refs, sems, nsteps_major, nsteps_minor):
    start, relay, finish = (_gather_phases if side[0] == "gather" else _exchange_phases)(x_refs, out_refs, *sems)
    a, b = pl.program_id(0), pl.program_id(1)
    pl.when(jnp.logical_and(a == 0, b == 0))(start)
    if relay is not None:
        pl.when(jnp.logical_and(a == nsteps_major - 1, b == 0))(relay)
    return lambda: pl.when(jnp.logical_and(a == nsteps_major - 1, b == nsteps_minor - 1))(finish)


def _exchange_call(kind, arrs, name):
    n = len(arrs)

    def body(*refs):
        start, relay, finish = (_gather_phases if kind == "gather" else _exchange_phases)(refs[:n], refs[n : 2 * n], *refs[2 * n :])
        start()
        if relay is not None:
            relay()
        finish()

    return pl.pallas_call(body, name=name, in_specs=[ANY_SPEC] * n, out_specs=[ANY_SPEC] * n,
                          out_shape=_side_out_shapes((kind, arrs)), scratch_shapes=_comm_sems(n))(*arrs)


def _sum_slots(recv, name):
    _, r, c = recv.shape
    tr = _tile(r, (256, 128, 64, 32, 16, 8))

    def body(r_ref, o_ref):
        acc = r_ref[0].astype(F32)
        for k in range(1, N_DEV):
            acc = acc + r_ref[k].astype(F32)
        o_ref[...] = acc

    return pl.pallas_call(
        body,
        name=name,
        grid=(r // tr,),
        in_specs=[pl.BlockSpec((N_DEV, tr, c), lambda i: (0, i, 0))],
        out_specs=pl.BlockSpec((tr, c), lambda i: (i, 0)),
        out_shape=jax.ShapeDtypeStruct((r, c), F32),
        compiler_params=_params(("parallel",)),
    )(recv)


def _adamw(w, g, m, v, name):
    shape = w.shape
    c = shape[-1]
    as2d = lambda a: a.reshape(-1, c)
    w2, g2, m2, v2 = as2d(w), as2d(g), as2d(m), as2d(v)
    r = w2.shape[0]
    tr = _tile(r, (256, 128, 64, 32, 16, 8))
    c1 = 1.0 / (1.0 - ADAM_B1 ** ADAM_STEP)
    c2 = 1.0 / (1.0 - ADAM_B2 ** ADAM_STEP)

    def body(w_ref, g_ref, m_ref, v_ref, d_ref, mo_ref, vo_ref):
        gg = g_ref[...]
        mn = ADAM_B1 * m_ref[...] + (1.0 - ADAM_B1) * gg
        vn = ADAM_B2 * v_ref[...] + (1.0 - ADAM_B2) * (gg * gg)
        d_ref[...] = -ADAM_LR * ((mn * c1) / (jnp.sqrt(vn * c2) + ADAM_EPS) + ADAM_WD * w_ref[...])
        mo_ref[...] = mn
        vo_ref[...] = vn

    spec = pl.BlockSpec((tr, c), lambda i: (i, 0))
    d, mn, vn = pl.pallas_call(
        body,
        name=name,
        grid=(r // tr,),
        in_specs=[spec] * 4,
        out_specs=[spec] * 3,
        out_shape=[jax.ShapeDtypeStruct((r, c), F32)] * 3,
        compiler_params=_params(("parallel",)),
    )(w2, g2, m2, v2)
    return d.reshape(shape), mn.reshape(shape), vn.reshape(shape)


COL_SHARDED = ("attn_w_in", "mla_w_q_b", "mla_w_kv_b", "s5_d", "s5_w_glu", "ffn_w_in", "ple_w")
ROW_SHARDED = ("attn_w_out", "ffn_w_out", "ple_gate_w")
REPLICATED = ("mla_q_norm", "mla_kv_norm", "s5_a_re", "s5_a_im", "s5_log_dt", "s5_b_re", "s5_b_im", "s5_c_re",
              "s5_c_im", "ln1_g", "ln1_b", "ln2_g", "ln2_b")
ATTENTION_WEIGHTS = ("attn_w_in", "mla_w_q_b", "mla_w_kv_b")
BIG_WEIGHTS = ("attn_w_in", "mla_w_q_b", "mla_w_kv_b", "attn_w_out", "s5_w_glu", "ffn_w_in", "ffn_w_out", "ple_w", "ple_gate_w")
WEIGHTS = ("attn_w_in", "mla_q_norm", "mla_w_q_b", "mla_kv_norm", "mla_w_kv_b", "attn_w_out", "s5_a_re", "s5_a_im",
           "s5_log_dt", "s5_b_re", "s5_b_im", "s5_c_re", "s5_c_im", "s5_d", "s5_w_glu", "ln1_g", "ln1_b", "ffn_w_in",
           "ffn_w_out", "ple_w", "ple_gate_w", "ln2_g", "ln2_b")


def _unshard(name, gathered):
    if name in COL_SHARDED:
        full = jnp.moveaxis(gathered, 0, -2)
        return full.reshape(full.shape[:-2] + (full.shape[-2] * full.shape[-1],))
    full = jnp.moveaxis(gathered, 0, 1)
    return full.reshape((full.shape[0], full.shape[1] * full.shape[2]) + full.shape[3:])


def _by_owner(name, grad):
    if name in COL_SHARDED:
        g = grad.reshape(grad.shape[:-1] + (N_DEV, grad.shape[-1] // N_DEV))
        return jnp.moveaxis(g, -2, 0).reshape(N_DEV, -1)
    if name in ROW_SHARDED:
        g = grad.reshape((grad.shape[0], N_DEV, grad.shape[1] // N_DEV) + grad.shape[2:])
        return jnp.moveaxis(g, 1, 0).reshape(N_DEV, -1)
    return jnp.broadcast_to(grad.reshape(1, -1), (N_DEV, grad.size))


def _owner_rows(name, g):
    k, n = g.shape
    if name in COL_SHARDED:
        return g.reshape(k, N_DEV, n // N_DEV).transpose(1, 0, 2)
    assert name in ROW_SHARDED, name
    return g.reshape(N_DEV, k // N_DEV, n)


def _pack(pieces, axis):
    blocks, where, row = [], [], 0
    for p in pieces:
        n = p.shape[axis]
        pad = (-n) % (PACK_COLS * PACK_ROW_TILE)
        if pad:
            shape = list(p.shape)
            shape[axis] = pad
            p = jnp.concatenate([p, jnp.zeros(shape, p.dtype)], axis=axis)
        rows = (n + pad) // PACK_COLS
        blocks.append(p.reshape(p.shape[:axis] + (rows, PACK_COLS)))
        where.append((row, rows, n))
        row += rows
    return jnp.concatenate(blocks, axis=axis), where


def _unpack(buf, axis, where):
    row, rows, n = where
    part = lax.slice_in_dim(buf, row, row + rows, axis=axis)
    return lax.slice_in_dim(part.reshape(part.shape[:axis] + (rows * PACK_COLS,)), 0, n, axis=axis)


def _twice(fn):
    return lambda *a: fn(*a) * 2


def _ffn_block(h, hb, p_i, w_in, w_out, w_ple, w_pg, g, b, alpha, tag):
    t, d = h.shape
    hid = w_out.shape[0]
    cw = _tile(hid, (1408, 512, 256, 128))
    ncb = hid // cw
    gu = _mm(hb, w_in, out_dtype=BF16, name=f"ffn_in_{tag}")
    act = _rowwise(_swiglu_fn, [(gu, cw, 0), (gu, cw, ncb)], [], [(hid, cw, BF16)], name=f"swiglu_{tag}", tq=512, ncol=ncb)[0]
    f = _mm(act, w_out, name=f"ffn_out_{tag}")
    pw = _mm(p_i, w_ple, name=f"ple_{tag}")
    gate = _mm(hb, w_pg, name=f"ple_gate_{tag}")
    out, outb = _rowwise(_twice(_ln_ffn_fn(alpha)), [_whole(h), _whole(f), _whole(pw), _whole(gate)], [g, b],
                         [(d, d, F32), (d, d, BF16)], name=f"ln2_{tag}", tq=256)
    return out, outb, (h, hb, p_i, gu, act, f, pw, gate)


def _ffn_block_bwd(saved, dout, w_in, w_out, w_pg, g, b, alpha, tag):
    h, hb, p_i, gu, act, f, pw, gate = saved
    t, d = h.shape
    hid = w_out.shape[0]
    cw = _tile(hid, (1408, 512, 256, 128))
    ncb = hid // cw
    dh_a, df, dpw, dgate, dg, db = _rowwise_bwd(
        _ln_ffn_fn(alpha), [_whole(h), _whole(f), _whole(pw), _whole(gate)], [g, b], [dout],
        need=[True] * 4, drow=[(d, F32), (d, BF16), (d, BF16), (d, BF16)], name=f"ln2_bwd_{tag}", tq=128)
    dw_out = _mm(act, df, ta=True, out_dtype=BF16, name=f"ffn_out_dw_{tag}")
    dact = _mm(df, w_out, tb=True, out_dtype=BF16, name=f"ffn_out_dx_{tag}")
    dg_, du_ = _rowwise_bwd(_swiglu_fn, [(gu, cw, 0), (gu, cw, ncb)], [], [[(dact, cw, 0)]], need=[True, True],
                            drow=[(hid, BF16), (hid, BF16)], name=f"swiglu_bwd_{tag}", tq=256, ncol=ncb)
    dgu = jnp.concatenate([dg_, du_], axis=1)
    dw_in = _mm(hb, dgu, ta=True, out_dtype=BF16, name=f"ffn_in_dw_{tag}")
    dh_b = _mm(dgu, w_in, tb=True, name=f"ffn_in_dx_{tag}")
    dw_ple = _mm(p_i, dpw, ta=True, out_dtype=BF16, name=f"ple_dw_{tag}")
    dw_pg = _mm(hb, dgate, ta=True, out_dtype=BF16, name=f"ple_gate_dw_{tag}")
    dh_c = _mm(dgate, w_pg, tb=True, name=f"ple_gate_dx_{tag}")
    return [dh_a, dh_b, dh_c], dict(ffn_w_in=dw_in, ffn_w_out=dw_out, ple_w=dw_ple, ple_gate_w=dw_pg, ln2_g=dg, ln2_b=db)


def kernel(x, p, positions, attn_w_in, mla_q_norm, mla_w_q_b, mla_kv_norm, mla_w_kv_b, attn_w_out, s5_a_re, s5_a_im, s5_log_dt, s5_b_re, s5_b_im, s5_c_re, s5_c_im, s5_d, s5_w_glu, ln1_g, ln1_b, ffn_w_in, ffn_w_out, ple_w, ple_gate_w, ln2_g, ln2_b, loss_target, m_attn_w_in, m_mla_q_norm, m_mla_w_q_b, m_mla_kv_norm, m_mla_w_kv_b, m_attn_w_out, m_s5_a_re, m_s5_a_im, m_s5_log_dt, m_s5_b_re, m_s5_b_im, m_s5_c_re, m_s5_c_im, m_s5_d, m_s5_w_glu, m_ln1_g, m_ln1_b, m_ffn_w_in, m_ffn_w_out, m_ple_w, m_ple_gate_w, m_ln2_g, m_ln2_b, v_attn_w_in, v_mla_q_norm, v_mla_w_q_b, v_mla_kv_norm, v_mla_w_kv_b, v_attn_w_out, v_s5_a_re, v_s5_a_im, v_s5_log_dt, v_s5_b_re, v_s5_b_im, v_s5_c_re, v_s5_c_im, v_s5_d, v_s5_w_glu, v_ln1_g, v_ln1_b, v_ffn_w_in, v_ffn_w_out, v_ple_w, v_ple_gate_w, v_ln2_g, v_ln2_b):
    local = dict(attn_w_in=attn_w_in, mla_q_norm=mla_q_norm, mla_w_q_b=mla_w_q_b, mla_kv_norm=mla_kv_norm,
                 mla_w_kv_b=mla_w_kv_b, attn_w_out=attn_w_out, s5_a_re=s5_a_re, s5_a_im=s5_a_im, s5_log_dt=s5_log_dt,
                 s5_b_re=s5_b_re, s5_b_im=s5_b_im, s5_c_re=s5_c_re, s5_c_im=s5_c_im, s5_d=s5_d, s5_w_glu=s5_w_glu,
                 ln1_g=ln1_g, ln1_b=ln1_b, ffn_w_in=ffn_w_in, ffn_w_out=ffn_w_out, ple_w=ple_w, ple_gate_w=ple_gate_w,
                 ln2_g=ln2_g, ln2_b=ln2_b)
    mom_m = dict(zip(WEIGHTS, (m_attn_w_in, m_mla_q_norm, m_mla_w_q_b, m_mla_kv_norm, m_mla_w_kv_b, m_attn_w_out, m_s5_a_re, m_s5_a_im, m_s5_log_dt, m_s5_b_re, m_s5_b_im, m_s5_c_re, m_s5_c_im, m_s5_d, m_s5_w_glu, m_ln1_g, m_ln1_b, m_ffn_w_in, m_ffn_w_out, m_ple_w, m_ple_gate_w, m_ln2_g, m_ln2_b)))
    mom_v = dict(zip(WEIGHTS, (v_attn_w_in, v_mla_q_norm, v_mla_w_q_b, v_mla_kv_norm, v_mla_w_kv_b, v_attn_w_out, v_s5_a_re, v_s5_a_im, v_s5_log_dt, v_s5_b_re, v_s5_b_im, v_s5_c_re, v_s5_c_im, v_s5_d, v_s5_w_glu, v_ln1_g, v_ln1_b, v_ffn_w_in, v_ffn_w_out, v_ple_w, v_ple_gate_w, v_ln2_g, v_ln2_b)))

    t, d = x.shape[1], x.shape[2]
    depth = ln1_g.shape[0]
    alpha = (2.0 * depth) ** 0.25
    ql, kvl = mla_q_norm.shape[-1], mla_kv_norm.shape[-1]
    nh = mla_w_q_b.shape[-1] * N_DEV // (NOPE_DIM + ROPE_DIM)
    dw = (attn_w_in.shape[-1] * N_DEV - ql - kvl - ROPE_DIM) // 3
    ndh = dw // DIL_HEAD_DIM
    ng = d // S5_GROUP
    assert ql == kvl and (ql + kvl) % LANE == 0 and nh * V_DIM == dw

    sharded = [n for n in WEIGHTS if n in COL_SHARDED or n in ROW_SHARDED]
    def as_sent(n):
        if n == "s5_d":
            return local[n]
        if n == "attn_w_in":
            return local[n][0].T.astype(BF16)
        return local[n].astype(BF16).reshape(-1, local[n].shape[-1])

    full = {}

    def take_weights(names, gathered):
        for n, got in zip(names, gathered):
            if n == "attn_w_in":
                full[n] = got.reshape(N_DEV * local[n].shape[2], d)
            else:
                full[n] = _unshard(n, got.reshape((N_DEV,) + local[n].shape))

    first_w = [n for n in sharded if n in ATTENTION_WEIGHTS]
    later_w = [n for n in sharded if n not in ATTENTION_WEIGHTS]
    take_weights(first_w, _exchange_call("gather", [as_sent(n) for n in first_w], "gather_attn_weights"))

    w_in_t = full["attn_w_in"]
    lat = ql + kvl
    w_lat_t = jnp.concatenate([w_in_t[: lat + ROPE_DIM], jnp.zeros((LANE - ROPE_DIM, d), BF16)], axis=0)
    w_dil_t = w_in_t[lat + ROPE_DIM :]
    wq = full["mla_w_q_b"][0].reshape(ql, nh, NOPE_DIM + ROPE_DIM)
    wq_pe = jnp.pad(wq[:, :, NOPE_DIM:], ((0, 0), (0, 0), (0, LANE - ROPE_DIM)))
    wqp = jnp.concatenate([wq[:, :, :NOPE_DIM].reshape(ql, nh * LANE), wq_pe.reshape(ql, nh * LANE)], axis=1)
    wkv = full["mla_w_kv_b"][0].reshape(kvl, nh, NOPE_DIM + V_DIM)
    wkvp = jnp.concatenate([wkv[:, :, :NOPE_DIM].reshape(kvl, nh * LANE), wkv[:, :, NOPE_DIM:].reshape(kvl, nh * LANE)], axis=1)

    h0 = x[0]
    h0b = h0.astype(BF16)
    pb = p.astype(BF16)
    target = loss_target[0]
    pos = positions.reshape(t, 1)
    grads = {}

    z = _mm(h0b, w_lat_t, tb=True, name="attn_in_lat")
    zd = _mm(h0b, w_dil_t, tb=True, out_dtype=BF16, name="attn_in_dil")
    gq, gk = mla_q_norm.reshape(1, ql), mla_kv_norm.reshape(1, kvl)
    qn, kvn = _rowwise(_rms_fn, [(z, ql, 0), (z, kvl, 1)], [gq, gk], [(ql, ql, BF16), (kvl, kvl, BF16)], name="rms", tq=256)
    qf = _mm(qn, wqp, name="q_up")
    kvf = _mm(kvn, wkvp, out_dtype=BF16, name="kv_up")
    cos, sin = _rope_tables(pos, "rope_tables")
    pe_cb = lat // LANE
    qpe, kpe = _rowwise(_rope_fwd_fn(nh), [(qf, nh * LANE, 1), (z, LANE, pe_cb), _whole(cos), _whole(sin)], [],
                        [(nh * LANE, nh * LANE, BF16), (LANE, LANE, BF16)], name="rope", tq=256)
    out_a, lse_a, later_gathered = _mla_fwd(qf, qpe, kvf, kpe, nh, "mla_fwd", side=("gather", [as_sent(n) for n in later_w]))
    take_weights(later_w, later_gathered)
    w_out = full["attn_w_out"][0]
    w_glu = full["s5_w_glu"][0]
    d_skip = full["s5_d"]

    def to_classes(a, dil):
        if dil == 1:
            return a
        return a.reshape(t // dil, dil, a.shape[1]).transpose(1, 0, 2).reshape(t, a.shape[1])

    def from_classes(a, dil):
        if dil == 1:
            return a
        return a.reshape(dil, t // dil, a.shape[1]).transpose(1, 0, 2).reshape(t, a.shape[1])

    band = []
    for window, dil in DIL_BRANCHES:
        assert window // dil == BLK
        zc = to_classes(zd, dil)
        o_c, l_c = _band_fwd(zc, ndh, dil, f"band_fwd_d{dil}")
        band.append((dil, zc, o_c, l_c, from_classes(o_c, dil), from_classes(l_c, dil)))
    merge_rows = [_whole(b[4]) for b in band] + [_whole(b[5]) for b in band]
    out_b = _rowwise(_merge_fn, merge_rows, [], [(dw, dw, BF16)], name="merge", tq=256)[0]
    att = jnp.concatenate([out_a, out_b], axis=1)
    mix0 = _mm(att, w_out, name="attn_out")
    g1, b1 = ln1_g[0:1], ln1_b[0:1]
    h1, h1b = _rowwise(_twice(_ln_mix_fn(alpha)), [_whole(h0), _whole(mix0)], [g1, b1], [(d, d, F32), (d, d, BF16)],
                       name="ln1_l0", tq=256)
    h2, _, saved_f0 = _ffn_block(h1, h1b, pb[0, 0], full["ffn_w_in"][0], full["ffn_w_out"][0], full["ple_w"][0],
                                 full["ple_gate_w"][0], ln2_g[0:1], ln2_b[0:1], alpha, "l0")

    prm = [s5_a_re[0].reshape(ng, 1, S5_STATE), s5_a_im[0].reshape(ng, 1, S5_STATE), s5_log_dt[0].reshape(ng, 1, 1),
           s5_b_re[0].transpose(0, 2, 1), s5_b_im[0].transpose(0, 2, 1), s5_c_re[0], s5_c_im[0]]
    u = _to_groups(h2, "s5_regroup_u")
    e_re, e_im, lam_re, lam_im = _s5_local(prm, u, "s5_local")
    lam_re_c, lam_im_c = lam_re.reshape(1, ng * S5_STATE), lam_im.reshape(1, ng * S5_STATE)
    sp_re_c, sp_im_c = _s5_carry(_states_to_cols(e_re), _states_to_cols(e_im), lam_re_c, lam_im_c, "s5_carry")
    sp_re, sp_im = _cols_to_states(sp_re_c, ng), _cols_to_states(sp_im_c, ng)
    ys = _from_groups(_s5_out(prm, u, sp_re, sp_im, "s5_out"), "s5_ungroup_y")
    z5 = _rowwise(_gelu_fn, [_whole(ys), _whole(h2)], [d_skip], [(d, d, BF16)], name="gelu", tq=256)[0]
    vg = _mm(z5, w_glu, name="glu_in")
    g3, b3 = ln1_g[1:2], ln1_b[1:2]
    h3, h3b = _rowwise(_twice(_ln_glu_fn(alpha)), [_whole(h2), (vg, d, 0), (vg, d, 1)], [g3, b3],
                       [(d, d, F32), (d, d, BF16)], name="ln1_l1", tq=256)
    h4, _, saved_f1 = _ffn_block(h3, h3b, pb[1, 0], full["ffn_w_in"][1], full["ffn_w_out"][1], full["ple_w"][1],
                                 full["ple_gate_w"][1], ln2_g[1:2], ln2_b[1:2], alpha, "l1")

    dh4, loss_acc = _loss_kernel(h4, target, "loss")
    loss = lax.psum(loss_acc[0, 0], AXES)

    dh3, gf1 = _ffn_block_bwd(saved_f1, [_whole(dh4)], full["ffn_w_in"][1], full["ffn_w_out"][1],
                              full["ple_gate_w"][1], ln2_g[1:2], ln2_b[1:2], alpha, "l1")
    dh2_a, dval, dgate, dg3, db3 = _rowwise_bwd(
        _ln_glu_fn(alpha), [_whole(h2), (vg, d, 0), (vg, d, 1)], [g3, b3], [[_whole(a) for a in dh3]],
        need=[True] * 3, drow=[(d, F32), (d, BF16), (d, BF16)], name="ln1_bwd_l1", tq=128)
    dvg = jnp.concatenate([dval, dgate], axis=1)
    dw_glu = _mm(z5, dvg, ta=True, out_dtype=BF16, name="glu_dw")
    dz5 = _mm(dvg, w_glu, tb=True, name="glu_dx")
    dys, dh2_b, dd = _rowwise_bwd(_gelu_fn, [_whole(ys), _whole(h2)], [d_skip], [[_whole(dz5)]], need=[True, True],
                                  drow=[(d, F32), (d, F32)], name="gelu_bwd", tq=128)
    grads["s5_d"] = dd
    dy = _to_groups(dys, "s5_regroup_dy")
    dsp_re, dsp_im = _s5_bwd_state(prm, dy, "s5_bwd_state")
    g_re_c, g_im_c, dlam_re_c, dlam_im_c = _s5_carry_bwd(_states_to_cols(dsp_re), _states_to_cols(dsp_im), sp_re_c, sp_im_c,
                                                         lam_re_c, lam_im_c, "s5_carry_bwd")
    s5g = _s5_bwd_main(prm, u, sp_re, sp_im, dy, _cols_to_states(g_re_c, ng), _cols_to_states(g_im_c, ng),
                       dlam_re_c.reshape(ng, 1, S5_STATE), dlam_im_c.reshape(ng, 1, S5_STATE), "s5_bwd_main")
    grads["s5_a_re"] = s5g[0].reshape(s5_a_re.shape)
    grads["s5_a_im"] = s5g[1].reshape(s5_a_im.shape)
    grads["s5_log_dt"] = s5g[2].reshape(s5_log_dt.shape)
    grads["s5_b_re"] = s5g[3].transpose(0, 2, 1)[None]
    grads["s5_b_im"] = s5g[4].transpose(0, 2, 1)[None]
    grads["s5_c_re"] = s5g[5][None]
    grads["s5_c_im"] = s5g[6][None]
    dh2_c = _from_groups(s5g[7], "s5_ungroup_du")

    dh1, gf0 = _ffn_block_bwd(saved_f0, [_whole(dh2_a), _whole(dh2_b), _whole(dh2_c)], full["ffn_w_in"][0],
                              full["ffn_w_out"][0], full["ple_gate_w"][0], ln2_g[0:1], ln2_b[0:1], alpha, "l0")
    for k in ("ln2_g", "ln2_b"):
        grads[k] = jnp.concatenate([gf0[k], gf1[k]])

    shards = {}

    def take(items, received):
        for (key, _), recv in zip(items, received):
            shards[key] = _sum_slots(recv, f"sum_grads_{key[0]}_{key[1]}")

    ffn_names = ("ffn_w_in", "ffn_w_out", "ple_w", "ple_gate_w")
    items_l1 = [(("s5_w_glu", 0), _owner_rows("s5_w_glu", dw_glu))] + [((k, 1), _owner_rows(k, gf1[k])) for k in ffn_names]
    items_l0 = [((k, 0), _owner_rows(k, gf0[k])) for k in ffn_names]
    dh0_a, dmix, dg1, db1 = _rowwise_bwd(_ln_mix_fn(alpha), [_whole(h0), _whole(mix0)], [g1, b1], [[_whole(a) for a in dh1]],
                                         need=[True, True], drow=[(d, F32), (d, BF16)], name="ln1_bwd_l0", tq=128)
    grads["ln1_g"] = jnp.concatenate([dg1, dg3])
    grads["ln1_b"] = jnp.concatenate([db1, db3])
    dw_out = _mm(att, dmix, ta=True, out_dtype=BF16, name="attn_out_dw")
    datt = _mm(dmix, w_out, tb=True, name="attn_out_dx")
    items_l0.append((("attn_w_out", 0), _owner_rows("attn_w_out", dw_out)))
    small_early = [n for n in WEIGHTS if n not in BIG_WEIGHTS and n not in ("mla_q_norm", "mla_kv_norm")]
    send_small, where_small = _pack([_by_owner(n, grads[n]).astype(F32) for n in small_early], 1)

    dmerge = _rowwise_bwd(_merge_fn, merge_rows, [], [[(datt, dw, 1)]], need=[True] * 6, drow=[(dw, F32)] * 6,
                          name="merge_bwd", tq=128)
    dq_s = dk_s = dv_s = None
    for k, (dil, zc, o_c, l_c, _, _) in enumerate(band):
        do_c, dl_c = to_classes(dmerge[k], dil), to_classes(dmerge[3 + k], dil)
        dq_c, dk_c, dv_c = _band_bwd(zc, o_c, l_c, do_c, dl_c, ndh, dil, f"band_bwd_d{dil}")
        dq_n, dk_n, dv_n = from_classes(dq_c, dil), from_classes(dk_c, dil), from_classes(dv_c, dil)
        dq_s = dq_n if dq_s is None else dq_s + dq_n
        dk_s = dk_n if dk_s is None else dk_s + dk_n
        dv_s = dv_n if dv_s is None else dv_s + dv_n

    dqn, dqp, delta, recv_l1 = _mla_bwd_q(qf, qpe, kvf, kpe, datt, out_a, lse_a, nh, "mla_bwd_q",
                                          side=("exchange", [rows for _, rows in items_l1]))
    take(items_l1, recv_l1)
    to_row = lambda a: a[:, ::LANE].T.reshape(nh, 1, t)
    dkn, dkp, dv, recv_l0 = _mla_bwd_kv(qf[:, : nh * LANE].astype(BF16), qpe, kvf, kpe, datt[:, : nh * LANE].astype(BF16),
                                        to_row(lse_a), to_row(delta), nh, "mla_bwd_kv",
                                        side=("exchange", [rows for _, rows in items_l0] + [send_small]))
    take(items_l0, recv_l0[:-1])
    sum_small = _sum_slots(recv_l0[-1], "sum_grads_small")
    dq_pe, dk_pe = _rowwise(_rope_bwd_fn(nh), [_whole(dqp), _whole(dkp), _whole(cos), _whole(sin)], [],
                            [(nh * LANE, nh * LANE, BF16), (LANE, LANE, BF16)], name="rope_bwd", tq=256)
    dqf = jnp.concatenate([dqn.astype(BF16), dq_pe], axis=1)
    dkvf = jnp.concatenate([dkn, dv], axis=1).astype(BF16)
    dwqp = _mm(qn, dqf, ta=True, name="q_up_dw")
    dqn_in = _mm(dqf, wqp, tb=True, name="q_up_dx")
    dwkvp = _mm(kvn, dkvf, ta=True, name="kv_up_dw")
    dkvn_in = _mm(dkvf, wkvp, tb=True, name="kv_up_dx")
    dql, dkvl, dgq, dgk = _rowwise_bwd(_rms_fn, [(z, ql, 0), (z, kvl, 1)], [gq, gk], [[_whole(dqn_in)], [_whole(dkvn_in)]],
                                       need=[True, True], drow=[(ql, BF16), (kvl, BF16)], name="rms_bwd", tq=256)
    grads["mla_q_norm"], grads["mla_kv_norm"] = dgq, dgk
    dz_lat = jnp.concatenate([dql, dkvl, dk_pe], axis=1)
    dz_dil = jnp.concatenate([dq_s, dk_s, dv_s], axis=1).astype(BF16)
    dw_lat_t = _mm(dz_lat, h0b, ta=True, out_dtype=BF16, name="attn_in_lat_dw")
    dw_dil_t = _mm(dz_dil, h0b, ta=True, out_dtype=BF16, name="attn_in_dil_dw")
    dh0_b = _mm(dz_lat, w_lat_t, name="attn_in_lat_dx")
    dh0_c = _mm(dz_dil, w_dil_t, name="attn_in_dil_dx")
    grad_x = _addn([dh0_a, dh0_b, dh0_c], "grad_x")[None]
    dw_in_t = jnp.concatenate([dw_lat_t[: lat + ROPE_DIM], dw_dil_t], axis=0)
    dwq_n = dwqp[:, : nh * LANE].reshape(ql, nh, NOPE_DIM)
    dwq_r = dwqp[:, nh * LANE :].reshape(ql, nh, LANE)[:, :, :ROPE_DIM]
    dwq = jnp.concatenate([dwq_n, dwq_r], axis=2).reshape(ql, nh * (NOPE_DIM + ROPE_DIM))
    dwkv_k = dwkvp[:, : nh * LANE].reshape(kvl, nh, NOPE_DIM)
    dwkv_v = dwkvp[:, nh * LANE :].reshape(kvl, nh, V_DIM)
    dwkv = jnp.concatenate([dwkv_k, dwkv_v], axis=2).reshape(kvl, nh * (NOPE_DIM + V_DIM))

    items_att = [(("attn_w_in", 0), dw_in_t.reshape(N_DEV, -1, PACK_COLS)),
                 (("mla_w_q_b", 0), _owner_rows("mla_w_q_b", dwq.astype(BF16))),
                 (("mla_w_kv_b", 0), _owner_rows("mla_w_kv_b", dwkv.astype(BF16)))]
    small_late = ["mla_q_norm", "mla_kv_norm"]
    send_late, where_late = _pack([_by_owner(n, grads[n]).astype(F32) for n in small_late], 1)
    received = _exchange_call("exchange", [rows for _, rows in items_att] + [send_late], "exchange_grads_last")
    take(items_att, received[:-1])
    sum_late = _sum_slots(received[-1], "sum_grads_norms")

    g_out, d_out, m_out, v_out = [], [], [], []
    for n in WEIGHTS:
        if n == "attn_w_in":
            g = shards[(n, 0)].reshape(local[n].shape[2], d).T[None]
        elif n in BIG_WEIGHTS:
            g = jnp.stack([shards[(n, layer)] for layer in range(local[n].shape[0])]).reshape(local[n].shape)
        elif n in small_late:
            g = _unpack(sum_late, 0, where_late[small_late.index(n)]).reshape(local[n].shape)
        else:
            g = _unpack(sum_small, 0, where_small[small_early.index(n)]).reshape(local[n].shape)
        dlt, mn, vn = _adamw(local[n], g, mom_m[n], mom_v[n], f"adamw_{n}")
        g_out.append(g)
        d_out.append(dlt)
        m_out.append(mn)
        v_out.append(vn)
    return (loss, grad_x, *g_out, *d_out, *m_out, *v_out)
```

```python
import functools
import math

import numpy as np
import jax
import jax.numpy as jnp
from jax import lax
from jax.experimental import pallas as pl
from jax.experimental.pallas import tpu as pltpu

F32 = jnp.float32
BF16 = jnp.bfloat16

NOPE_DIM, ROPE_DIM, V_DIM = 128, 64, 128
DIL_HEAD_DIM = 128
DIL_BRANCHES = ((128, 1), (512, 4), (2048, 16))
BLK = 128
LANE = 128
ROPE_THETA = 10000.0
S5_GROUP, S5_STATE = 16, 64
S5_CHUNK = 32
NEG = -1e30
ADAM_LR, ADAM_B1, ADAM_B2, ADAM_EPS, ADAM_WD, ADAM_STEP = 0.001, 0.9, 0.999, 1e-08, 0.01, 10
N_DEV = 8
AXES = ("x", "y", "c")
VMEM_LIMIT = 56 * 1024 * 1024
MM_VMEM_BUDGET = 40 * 1024 * 1024
PACK_COLS = 1024
PACK_ROW_TILE = 16
HI = lax.Precision.HIGH


def _tile(n, cands):
    for c in cands:
        if n % c == 0:
            return c
    return n


def _params(sem=None):
    return pltpu.CompilerParams(dimension_semantics=sem, vmem_limit_bytes=VMEM_LIMIT)


def _dot(a, b, dims, prec=None):
    return lax.dot_general(a, b, (dims, ((), ())), preferred_element_type=F32, precision=prec)


NN = ((1,), (0,))
NT = ((1,), (1,))
TN = ((0,), (0,))


def _mm(a, b, *, ta=False, tb=False, out_dtype=F32, name):
    (k, m) = a.shape if ta else a.shape[::-1]
    (n, k2) = b.shape if tb else b.shape[::-1]
    assert k == k2, (a.shape, b.shape, ta, tb)
    tm = _tile(m, (1024, 512, 256, 128))
    tn = _tile(n, (1024, 512, 384, 256, 128))
    sa, sb, so = a.dtype.itemsize, b.dtype.itemsize, jnp.dtype(out_dtype).itemsize

    def fits(tk):
        return 2 * (tm * tk * sa + tk * tn * sb) + 2 * tm * tn * so + 4 * tm * tn <= MM_VMEM_BUDGET

    tk = next((c for c in (2816, 2048, 1536, 1408, 1152, 1024, 768, 512, 384, 256, 128) if k % c == 0 and fits(c)), k)
    nk = k // tk
    dims = (((0,) if ta else (1,)), ((1,) if tb else (0,)))

    def body(a_ref, b_ref, o_ref, *acc):
        part = _dot(a_ref[...].astype(BF16), b_ref[...].astype(BF16), dims)
        if nk == 1:
            o_ref[...] = part.astype(o_ref.dtype)
            return
        (acc_ref,) = acc
        kk = pl.program_id(2)

        @pl.when(kk == 0)
        def _():
            acc_ref[...] = part

        @pl.when(kk > 0)
        def _():
            acc_ref[...] += part

        @pl.when(kk == nk - 1)
        def _():
            o_ref[...] = acc_ref[...].astype(o_ref.dtype)

    a_spec = pl.BlockSpec((tk, tm), lambda i, j, kk: (kk, i)) if ta else pl.BlockSpec((tm, tk), lambda i, j, kk: (i, kk))
    b_spec = pl.BlockSpec((tn, tk), lambda i, j, kk: (j, kk)) if tb else pl.BlockSpec((tk, tn), lambda i, j, kk: (kk, j))
    return pl.pallas_call(
        body,
        name=name,
        grid=(m // tm, n // tn, nk),
        in_specs=[a_spec, b_spec],
        out_specs=pl.BlockSpec((tm, tn), lambda i, j, kk: (i, j)),
        out_shape=jax.ShapeDtypeStruct((m, n), out_dtype),
        scratch_shapes=[pltpu.VMEM((tm, tn), F32)] if nk > 1 else [],
        compiler_params=_params(("parallel", "parallel", "arbitrary")),
    )(a, b)


def _row_spec(spec, tq):
    _, w, cb = spec
    return pl.BlockSpec((tq, w), lambda i, j: (i, cb + j))


def _full_spec(p):
    return pl.BlockSpec(p.shape, lambda i, j: (0,) * p.ndim)


def _rowwise(fn, rows, pars, outs, *, name, tq, ncol=1):
    t = rows[0][0].shape[0]
    tq = _tile(t, (tq, 128, 64, 32, 16, 8))
    nr, npar = len(rows), len(pars)

    def body(*refs):
        vals = [r[...].astype(F32) for r in refs[: nr + npar]]
        res = fn(*vals)
        for o, r in zip(refs[nr + npar :], res):
            o[...] = r.astype(o.dtype)

    res = pl.pallas_call(
        body,
        name=name,
        grid=(t // tq, ncol),
        in_specs=[_row_spec(s, tq) for s in rows] + [_full_spec(p) for p in pars],
        out_specs=[pl.BlockSpec((tq, w), lambda i, j: (i, j)) for (_, w, _) in outs],
        out_shape=[jax.ShapeDtypeStruct((t, wt), dt) for (wt, _, dt) in outs],
        compiler_params=_params(("parallel", "parallel")),
    )(*[s[0] for s in rows], *pars)
    return res


def _rowwise_bwd(fn, rows, pars, cots, *, need, drow, name, tq, ncol=1):
    t = rows[0][0].shape[0]
    tq = _tile(t, (tq, 128, 64, 32, 16, 8))
    nr, npar = len(rows), len(pars)
    assert ncol == 1 or npar == 0
    flat_cots = [s for c in cots for s in c]
    ncot = len(flat_cots)
    want = [k for k in range(nr) if need[k]]

    def body(*refs):
        vals = [r[...].astype(F32) for r in refs[: nr + npar]]
        cref = refs[nr + npar : nr + npar + ncot]
        oref = refs[nr + npar + ncot :]
        cvals, pos = [], 0
        for c in cots:
            acc = cref[pos][...].astype(F32)
            for q in range(1, len(c)):
                acc = acc + cref[pos + q][...].astype(F32)
            cvals.append(acc)
            pos += len(c)

        def closed(*diff):
            full = list(vals)
            for k, dv in zip(want + list(range(nr, nr + npar)), diff):
                full[k] = dv
            return tuple(fn(*full))

        diff_in = [vals[k] for k in want] + vals[nr:]
        _, vjp = jax.vjp(closed, *diff_in)
        grads = vjp(tuple(cvals))
        for q in range(len(want)):
            oref[q][...] = grads[q].astype(oref[q].dtype)
        if npar:
            first = pl.program_id(0) == 0

            @pl.when(first)
            def _():
                for q in range(npar):
                    oref[len(want) + q][...] = jnp.zeros_like(oref[len(want) + q])

            for q in range(npar):
                oref[len(want) + q][...] += grads[len(want) + q]

    out_specs = [pl.BlockSpec((tq, rows[k][1]), lambda i, j, cb=rows[k][2]: (i, j)) for k in want]
    out_specs += [_full_spec(p) for p in pars]
    out_shape = [jax.ShapeDtypeStruct((t, wt), dt) for (wt, dt) in drow]
    out_shape += [jax.ShapeDtypeStruct(p.shape, F32) for p in pars]
    return pl.pallas_call(
        body,
        name=name,
        grid=(t // tq, ncol),
        in_specs=[_row_spec(s, tq) for s in rows] + [_full_spec(p) for p in pars] + [_row_spec(s, tq) for s in flat_cots],
        out_specs=out_specs,
        out_shape=out_shape,
        compiler_params=_params(("arbitrary", "arbitrary") if npar else ("parallel", "parallel")),
    )(*[s[0] for s in rows], *pars, *[s[0] for s in flat_cots])


def _whole(a, w=None):
    return (a, a.shape[1] if w is None else w, 0)


def _rms_fn(ql, kvl, gq, gk):
    def one(x, g):
        return x * lax.rsqrt(jnp.mean(x * x, -1, keepdims=True) + 1e-6) * g

    return one(ql, gq), one(kvl, gk)


def _layernorm(s, g, b):
    mu = jnp.mean(s, -1, keepdims=True)
    d = s - mu
    var = jnp.mean(d * d, -1, keepdims=True)
    return d * lax.rsqrt(var + 1e-5) * g + b


def _ln_mix_fn(alpha):
    def fn(h, mix, g, b):
        return (_layernorm(alpha * h + mix, g, b),)

    return fn


def _ln_glu_fn(alpha):
    def fn(h, val, gate, g, b):
        return (_layernorm(alpha * h + val * jax.nn.sigmoid(gate), g, b),)

    return fn


def _ln_ffn_fn(alpha):
    def fn(h, f, pw, gate, g, b):
        return (_layernorm(alpha * h + f + pw * jax.nn.sigmoid(gate), g, b),)

    return fn


def _swiglu_fn(g, u):
    return (jax.nn.silu(g) * u,)


def _gelu_fn(ys, h, d):
    return (jax.nn.gelu(ys + d * h),)


def _merge_fn(o1, o2, o3, l1, l2, l3):
    m = jnp.maximum(jnp.maximum(l1, l2), l3)
    e1, e2, e3 = jnp.exp(l1 - m), jnp.exp(l2 - m), jnp.exp(l3 - m)
    return ((e1 * o1 + e2 * o2 + e3 * o3) / (e1 + e2 + e3),)


def _swap_halves(t):
    w = t.shape[1]
    lane = lax.broadcasted_iota(jnp.int32, t.shape, 1) % LANE
    up = jnp.where(lane < ROPE_DIM, pltpu.roll(t, ROPE_DIM // 2, 1), 0.0)
    return jnp.where(lane < ROPE_DIM // 2, pltpu.roll(t, w - ROPE_DIM // 2, 1), up)


def _rope_fwd_fn(nh):
    def fn(qpe, kpe, cos, sin):
        cq, sq = jnp.tile(cos, (1, nh)), jnp.tile(sin, (1, nh))
        return qpe * cq + _swap_halves(qpe) * sq, kpe * cos + _swap_halves(kpe) * sin

    return fn


def _rope_bwd_fn(nh):
    def fn(dq, dk_heads, cos, sin):
        cq, sq = jnp.tile(cos, (1, nh)), jnp.tile(sin, (1, nh))
        dk = dk_heads[:, :LANE]
        for h in range(1, nh):
            dk = dk + dk_heads[:, h * LANE : (h + 1) * LANE]
        return dq * cq + _swap_halves(dq * sq), dk * cos + _swap_halves(dk * sin)

    return fn


def _rope_tables(positions, name):
    t = positions.shape[0]
    tq = _tile(t, (512, 128, 8))
    half = ROPE_DIM // 2

    def body(p_ref, c_ref, s_ref):
        lane = lax.broadcasted_iota(jnp.int32, (tq, LANE), 1)
        idx = (lane % half).astype(F32)
        inv_freq = jnp.exp(idx * (-math.log(ROPE_THETA) / half))
        ang = p_ref[...].astype(F32) * inv_freq
        live = lane < ROPE_DIM
        c_ref[...] = jnp.where(live, jnp.cos(ang), 0.0)
        s_ref[...] = jnp.where(live, jnp.where(lane < half, -jnp.sin(ang), jnp.sin(ang)), 0.0)

    return pl.pallas_call(
        body,
        name=name,
        grid=(t // tq,),
        in_specs=[pl.BlockSpec((tq, 1), lambda i: (i, 0))],
        out_specs=[pl.BlockSpec((tq, LANE), lambda i: (i, 0))] * 2,
        out_shape=[jax.ShapeDtypeStruct((t, LANE), F32)] * 2,
        compiler_params=_params(("parallel",)),
    )(positions)


def _loss_kernel(y, target, name):
    t, d = y.shape
    tq = _tile(t, (256, 128, 8))

    def body(y_ref, t_ref, dy_ref, l_ref):
        @pl.when(pl.program_id(0) == 0)
        def _():
            l_ref[...] = jnp.zeros_like(l_ref)

        e = y_ref[...] - t_ref[...]
        dy_ref[...] = e * (1.0 / d)
        l_ref[...] += jnp.sum(e * e) * (0.5 / d)

    return pl.pallas_call(
        body,
        name=name,
        grid=(t // tq,),
        in_specs=[pl.BlockSpec((tq, d), lambda i: (i, 0))] * 2,
        out_specs=[pl.BlockSpec((tq, d), lambda i: (i, 0)), pl.BlockSpec((8, LANE), lambda i: (0, 0))],
        out_shape=[jax.ShapeDtypeStruct((t, d), F32), jax.ShapeDtypeStruct((8, LANE), F32)],
        compiler_params=_params(("arbitrary",)),
    )(y, target)


def _addn(arrs, name):
    w = arrs[0].shape[1]
    return _rowwise(lambda *v: (functools.reduce(lambda p, q: p + q, v),), [_whole(a) for a in arrs], [], [(w, w, F32)],
                    name=name, tq=256)[0]


def _mla_tiles(t):
    tq = _tile(t, (512, 256, 128))
    return tq, t // tq


def _call_with_side(body, side, n_in, grid, *, name, in_specs, out_specs, out_shape):
    if side is None:
        return pl.pallas_call(body, name=name, grid=grid, in_specs=in_specs, out_specs=out_specs, out_shape=out_shape,
                              compiler_params=_params(("parallel", "parallel")))
    n_out, n = len(out_specs), len(side[1])

    def wrapped(*refs):
        ins, side_ins = refs[:n_in], refs[n_in : n_in + n]
        outs, side_outs = refs[n_in + n : n_in + n + n_out], refs[n_in + n + n_out : n_in + 2 * n + n_out]
        done = _run_side(side, side_ins, side_outs, refs[n_in + 2 * n + n_out :], grid[0], grid[1])
        body(*ins, *outs)
        done()

    call = pl.pallas_call(wrapped, name=name, grid=grid, in_specs=in_specs + [ANY_SPEC] * n,
                          out_specs=out_specs + [ANY_SPEC] * n, out_shape=out_shape + _side_out_shapes(side),
                          scratch_shapes=_comm_sems(n), compiler_params=_params(("arbitrary", "arbitrary")))

    def run(*args):
        res = call(*args, *side[1])
        return (*res[:n_out], list(res[n_out:]))

    return run


def _mla_fwd(qf, qpe, kvf, kpe, nh, name, side=None):
    t = qf.shape[0]
    tq, nq = _mla_tiles(t)
    scale = (NOPE_DIM + ROPE_DIM) ** -0.5

    def body(qn_ref, qp_ref, kn_ref, kp_ref, v_ref, o_ref, lse_ref):
        i = pl.program_id(1)
        qn = qn_ref[...].astype(BF16)
        qp = qp_ref[...]

        def step(j, carry, masked):
            m, l, acc = carry
            ks = pl.ds(pl.multiple_of(j * tq, tq), tq)
            s = (_dot(qn, kn_ref[ks, :], NT) + _dot(qp, kp_ref[ks, :], NT)) * scale
            if masked:
                row = lax.broadcasted_iota(jnp.int32, (tq, tq), 0)
                col = lax.broadcasted_iota(jnp.int32, (tq, tq), 1)
                s = jnp.where(col <= row, s, NEG)
            m_new = jnp.maximum(m, jnp.max(s, -1, keepdims=True))
            p = jnp.exp(s - m_new)
            a = jnp.exp(m - m_new)
            return m_new, a * l + jnp.sum(p, -1, keepdims=True), a * acc + _dot(p.astype(BF16), v_ref[ks, :], NN)

        init = (jnp.full((tq, 1), NEG, F32), jnp.zeros((tq, 1), F32), jnp.zeros((tq, V_DIM), F32))
        carry = lax.fori_loop(0, i, lambda j, c: step(j, c, False), init)
        m, l, acc = step(i, carry, True)
        o_ref[...] = (acc / l).astype(o_ref.dtype)
        lse_ref[...] = jnp.broadcast_to(m + jnp.log(l), (tq, LANE))

    blk = lambda h, i: (i, h)
    return _call_with_side(
        body, side, 5, (nh, nq),
        name=name,
        in_specs=[
            pl.BlockSpec((tq, LANE), blk),
            pl.BlockSpec((tq, LANE), blk),
            pl.BlockSpec((t, LANE), lambda h, i: (0, h)),
            pl.BlockSpec((t, LANE), lambda h, i: (0, 0)),
            pl.BlockSpec((t, LANE), lambda h, i: (0, nh + h)),
        ],
        out_specs=[pl.BlockSpec((tq, LANE), blk), pl.BlockSpec((tq, LANE), blk)],
        out_shape=[jax.ShapeDtypeStruct((t, nh * LANE), BF16), jax.ShapeDtypeStruct((t, nh * LANE), F32)],
    )(qf, qpe, kvf, kpe, kvf)


def _mla_bwd_q(qf, qpe, kvf, kpe, do, o, lse, nh, name, side=None):
    t = qf.shape[0]
    tq, nq = _mla_tiles(t)
    scale = (NOPE_DIM + ROPE_DIM) ** -0.5

    def body(qn_ref, qp_ref, kn_ref, kp_ref, v_ref, do_ref, o_ref, lse_ref, dqn_ref, dqp_ref, dl_ref):
        i = pl.program_id(1)
        qn = qn_ref[...].astype(BF16)
        qp = qp_ref[...]
        dof = do_ref[...].astype(F32)
        dob = dof.astype(BF16)
        delta = jnp.sum(dof * o_ref[...].astype(F32), -1, keepdims=True)
        lse1 = lse_ref[:, :1]

        def step(j, carry, masked):
            dqn, dqp = carry
            ks = pl.ds(pl.multiple_of(j * tq, tq), tq)
            kn, kp = kn_ref[ks, :], kp_ref[ks, :]
            s = (_dot(qn, kn, NT) + _dot(qp, kp, NT)) * scale
            p = jnp.exp(s - lse1)
            if masked:
                row = lax.broadcasted_iota(jnp.int32, (tq, tq), 0)
                col = lax.broadcasted_iota(jnp.int32, (tq, tq), 1)
                p = jnp.where(col <= row, p, 0.0)
            dp = _dot(dob, v_ref[ks, :], NT)
            ds = (p * (dp - delta) * scale).astype(BF16)
            return dqn + _dot(ds, kn, NN), dqp + _dot(ds, kp, NN)

        init = (jnp.zeros((tq, LANE), F32), jnp.zeros((tq, LANE), F32))
        carry = lax.fori_loop(0, i, lambda j, c: step(j, c, False), init)
        dqn, dqp = step(i, carry, True)
        dqn_ref[...] = dqn
        dqp_ref[...] = dqp
        dl_ref[...] = jnp.broadcast_to(delta, (tq, LANE))

    blk = lambda h, i: (i, h)
    bs = pl.BlockSpec((tq, LANE), blk)
    return _call_with_side(
        body, side, 8, (nh, nq),
        name=name,
        in_specs=[
            bs,
            bs,
            pl.BlockSpec((t, LANE), lambda h, i: (0, h)),
            pl.BlockSpec((t, LANE), lambda h, i: (0, 0)),
            pl.BlockSpec((t, LANE), lambda h, i: (0, nh + h)),
            bs,
            bs,
            bs,
        ],
        out_specs=[bs, bs, bs],
        out_shape=[jax.ShapeDtypeStruct((t, nh * LANE), F32)] * 3,
    )(qf, qpe, kvf, kpe, kvf, do, o, lse)


def _mla_bwd_kv(qn16, qpe, kvf, kpe, do16, lse_row, delta_row, nh, name, side=None):
    t = qn16.shape[0]
    tq, nq = _mla_tiles(t)
    scale = (NOPE_DIM + ROPE_DIM) ** -0.5

    def body(kn_ref, kp_ref, v_ref, qn_ref, qp_ref, do_ref, lse_ref, dl_ref, dkn_ref, dkp_ref, dv_ref):
        j = pl.program_id(1)
        kn, kp, v = kn_ref[...], kp_ref[...], v_ref[...]

        def step(i, carry, masked):
            dkn, dkp, dv = carry
            qs = pl.ds(pl.multiple_of(i * tq, tq), tq)
            qn, qp, dob = qn_ref[qs, :], qp_ref[qs, :], do_ref[qs, :]
            st = (_dot(kn, qn, NT) + _dot(kp, qp, NT)) * scale
            pt = jnp.exp(st - lse_ref[0, :, qs])
            if masked:
                key = lax.broadcasted_iota(jnp.int32, (tq, tq), 0)
                qry = lax.broadcasted_iota(jnp.int32, (tq, tq), 1)
                pt = jnp.where(key <= qry, pt, 0.0)
            dv = dv + _dot(pt.astype(BF16), dob, NN)
            dpt = _dot(v, dob, NT)
            dst = (pt * (dpt - dl_ref[0, :, qs]) * scale).astype(BF16)
            return dkn + _dot(dst, qn, NN), dkp + _dot(dst, qp, NN), dv

        z = jnp.zeros((tq, LANE), F32)
        carry = step(j, (z, z, z), True)
        dkn, dkp, dv = lax.fori_loop(j + 1, nq, lambda i, c: step(i, c, False), carry)
        dkn_ref[...] = dkn
        dkp_ref[...] = dkp
        dv_ref[...] = dv

    blk = pl.BlockSpec((tq, LANE), lambda h, j: (j, h))
    res = lambda f: pl.BlockSpec((t, LANE), f)
    row = pl.BlockSpec((1, 1, t), lambda h, j: (h, 0, 0))
    return _call_with_side(
        body, side, 8, (nh, nq),
        name=name,
        in_specs=[
            blk,
            pl.BlockSpec((tq, LANE), lambda h, j: (j, 0)),
            pl.BlockSpec((tq, LANE), lambda h, j: (j, nh + h)),
            res(lambda h, j: (0, h)),
            res(lambda h, j: (0, h)),
            res(lambda h, j: (0, h)),
            row,
            row,
        ],
        out_specs=[blk, blk, blk],
        out_shape=[jax.ShapeDtypeStruct((t, nh * LANE), F32)] * 3,
    )(kvf, kpe, kvf, qn16, qpe, do16, lse_row, delta_row)


def _alibi_slopes(n):
    return [float(2.0 ** (-8.0 * i / n)) for i in range(1, n + 1)]


def _window_mask(has_prev):
    qi = lax.broadcasted_iota(jnp.int32, (BLK, 2 * BLK), 0)
    ki = lax.broadcasted_iota(jnp.int32, (BLK, 2 * BLK), 1)
    dist = qi + BLK - ki
    valid = (dist >= 0) & (dist <= BLK) & ((ki >= BLK) | has_prev)
    return valid, dist.astype(F32)


def _band_fwd(zd, nh, dil, name):
    t = zd.shape[0]
    nbc = t // BLK // dil
    scale = DIL_HEAD_DIM ** -0.5
    slopes = _alibi_slopes(nh)
    dw = nh * LANE

    def body(q_ref, kc_ref, kp_ref, vc_ref, vp_ref, o_ref, l_ref):
        mask, dist = _window_mask(pl.program_id(1) > 0)
        for h in range(nh):
            sl = slice(h * LANE, (h + 1) * LANE)
            keys = jnp.concatenate([kp_ref[:, sl], kc_ref[:, sl]], axis=0)
            vals = jnp.concatenate([vp_ref[:, sl], vc_ref[:, sl]], axis=0)
            s = jnp.where(mask, _dot(q_ref[:, sl], keys, NT) * scale - (slopes[h] * dil) * dist, NEG)
            m = jnp.max(s, -1, keepdims=True)
            e = jnp.exp(s - m)
            l = jnp.sum(e, -1, keepdims=True)
            o_ref[:, sl] = _dot((e * (1.0 / l)).astype(BF16), vals, NN)
            l_ref[:, sl] = jnp.broadcast_to(m + jnp.log(l), (BLK, LANE))

    prev = lambda i: jnp.maximum(i - 1, 0)
    spec = lambda f: pl.BlockSpec((BLK, dw), f)
    return pl.pallas_call(
        body,
        name=name,
        grid=(dil, nbc),
        in_specs=[
            spec(lambda r, i: (r * nbc + i, 0)),
            spec(lambda r, i: (r * nbc + i, 1)),
            spec(lambda r, i: (r * nbc + prev(i), 1)),
            spec(lambda r, i: (r * nbc + i, 2)),
            spec(lambda r, i: (r * nbc + prev(i), 2)),
        ],
        out_specs=[spec(lambda r, i: (r * nbc + i, 0))] * 2,
        out_shape=[jax.ShapeDtypeStruct((t, dw), F32)] * 2,
        compiler_params=_params(("parallel", "parallel")),
    )(zd, zd, zd, zd, zd)


def _band_bwd(zd, o, lse, do, dl, nh, dil, name):
    t = zd.shape[0]
    nbc = t // BLK // dil
    scale = DIL_HEAD_DIM ** -0.5
    slopes = _alibi_slopes(nh)
    dw = nh * LANE

    def body(q_ref, k_ref, v_ref, kp_ref, vp_ref, qn_ref, o_ref, l_ref, do_ref, dl_ref, on_ref, ln_ref, don_ref, dln_ref,
             dq_ref, dk_ref, dv_ref):
        mask, dist = _window_mask(pl.program_id(1) > 0)
        mask_n, dist_n = _window_mask(pl.program_id(1) + 1 < nbc)
        mask_n, dist_n = mask_n[:, :BLK], dist_n[:, :BLK]
        for h in range(nh):
            sl = slice(h * LANE, (h + 1) * LANE)
            bias = slopes[h] * dil
            q, k, v, qn = q_ref[:, sl], k_ref[:, sl], v_ref[:, sl], qn_ref[:, sl]
            keys = jnp.concatenate([kp_ref[:, sl], k], axis=0)
            vals = jnp.concatenate([vp_ref[:, sl], v], axis=0)
            dof, donf = do_ref[:, sl], don_ref[:, sl]
            dob, donb = dof.astype(BF16), donf.astype(BF16)
            lse1, lsen1 = l_ref[:, sl][:, :1], ln_ref[:, sl][:, :1]
            adj = jnp.sum(dl_ref[:, sl] - dof * o_ref[:, sl], -1, keepdims=True)
            adjn = jnp.sum(dln_ref[:, sl] - donf * on_ref[:, sl], -1, keepdims=True)
            p = jnp.where(mask, jnp.exp(_dot(q, keys, NT) * scale - bias * dist - lse1), 0.0)
            ds = (p * (_dot(dob, vals, NT) + adj)).astype(BF16)
            dq_ref[:, sl] = _dot(ds, keys, NN) * scale
            pn = jnp.where(mask_n, jnp.exp(_dot(qn, k, NT) * scale - bias * dist_n - lsen1), 0.0)
            dsn = (pn * (_dot(donb, v, NT) + adjn)).astype(BF16)
            both_q = jnp.concatenate([q, qn], axis=0)
            dk_ref[:, sl] = _dot(jnp.concatenate([ds[:, BLK:], dsn], axis=0), both_q, TN) * scale
            dv_ref[:, sl] = _dot(jnp.concatenate([p[:, BLK:].astype(BF16), pn.astype(BF16)], axis=0),
                                 jnp.concatenate([dob, donb], axis=0), TN)

    prev = lambda i: jnp.maximum(i - 1, 0)
    nxt = lambda i: jnp.minimum(i + 1, nbc - 1)
    spec = lambda f: pl.BlockSpec((BLK, dw), f)
    cur, nx = spec(lambda r, i: (r * nbc + i, 0)), spec(lambda r, i: (r * nbc + nxt(i), 0))
    return pl.pallas_call(
        body,
        name=name,
        grid=(dil, nbc),
        in_specs=[cur, spec(lambda r, i: (r * nbc + i, 1)), spec(lambda r, i: (r * nbc + i, 2)),
                  spec(lambda r, i: (r * nbc + prev(i), 1)), spec(lambda r, i: (r * nbc + prev(i), 2)),
                  nx, cur, cur, cur, cur, nx, nx, nx, nx],
        out_specs=[cur] * 3,
        out_shape=[jax.ShapeDtypeStruct((t, dw), F32)] * 3,
        compiler_params=_params(("parallel", "parallel")),
    )(zd, zd, zd, zd, zd, zd, o, lse, do, dl, o, lse, do, dl)


def _s5_ops(a_re, a_im, ldt, bt_re, bt_im, c_re, c_im):
    L, g, p = S5_CHUNK, S5_GROUP, S5_STATE
    dt = jnp.exp(ldt)
    lr, li = a_re * dt, a_im * dt
    er = jnp.exp(lr)
    lam_re, lam_im = er * jnp.cos(li), er * jnp.sin(li)
    nr, ni = lam_re - 1.0, lam_im
    den = a_re * a_re + a_im * a_im
    fr, fi = (nr * a_re + ni * a_im) / den, (ni * a_re - nr * a_im) / den
    bb_re, bb_im = fr * bt_re - fi * bt_im, fr * bt_im + fi * bt_re

    def power(tau):
        mag = jnp.exp(tau * lr)
        return mag * jnp.cos(tau * li), mag * jnp.sin(tau * li)

    step = lax.broadcasted_iota(jnp.int32, (L, 1), 0).astype(F32)

    def outer(pr, pi, mr, mi):
        re = pr[:, None, :] * mr[None] - pi[:, None, :] * mi[None]
        im = pr[:, None, :] * mi[None] + pi[:, None, :] * mr[None]
        return re.reshape(L * g, p), im.reshape(L * g, p)

    half = float(L // 2)
    cp_re, cp_im = outer(*power(step - half), c_re, c_im)
    pb_re, pb_im = outer(*power(half - step), bb_re, bb_im)
    toep = _dot(cp_re, pb_re, NT, HI) - _dot(cp_im, pb_im, NT, HI)
    trow = lax.broadcasted_iota(jnp.int32, (L * g, L * g), 0) // g
    scol = lax.broadcasted_iota(jnp.int32, (L * g, L * g), 1) // g
    toep = jnp.where(trow >= scol, toep, 0.0)
    et_re, et_im = outer(*power(float(L - 1) - step), bb_re, bb_im)
    f_re, f_im = outer(*power(step + 1.0), c_re, c_im)
    big_re, big_im = power(jnp.full((1, 1), float(L), F32))
    return toep, et_re, et_im, f_re, f_im, big_re, big_im


def _s5_y(ops, u, sp_re, sp_im):
    toep, _, _, f_re, f_im, _, _ = ops
    return _dot(u, toep, NT) + _dot(sp_re, f_re, NT, HI) - _dot(sp_im, f_im, NT, HI)


def _s5_group_specs(ng):
    vec = pl.BlockSpec((1, 1, S5_STATE), lambda g: (g, 0, 0))
    one = pl.BlockSpec((1, 1, 1), lambda g: (g, 0, 0))
    mat = pl.BlockSpec((1, S5_GROUP, S5_STATE), lambda g: (g, 0, 0))
    return [vec, vec, one, mat, mat, mat, mat]


def _s5_load(refs):
    return [r[0] for r in refs]


def _s5_local(prm, u, name):
    ng, n, w = u.shape
    p = S5_STATE

    def body(*refs):
        ops = _s5_ops(*_s5_load(refs[:7]))
        uu = refs[7][0]
        refs[8][0] = _dot(uu, ops[1], NN, HI)
        refs[9][0] = _dot(uu, ops[2], NN, HI)
        refs[10][0] = ops[5]
        refs[11][0] = ops[6]

    blk = lambda a, b: pl.BlockSpec((1, a, b), lambda g: (g, 0, 0))
    return pl.pallas_call(
        body,
        name=name,
        grid=(ng,),
        in_specs=_s5_group_specs(ng) + [blk(n, w)],
        out_specs=[blk(n, p), blk(n, p), blk(1, p), blk(1, p)],
        out_shape=[jax.ShapeDtypeStruct((ng, n, p), F32)] * 2 + [jax.ShapeDtypeStruct((ng, 1, p), F32)] * 2,
        compiler_params=_params(("parallel",)),
    )(*prm, u)


def _s5_carry(e_re, e_im, lam_re, lam_im, name):
    n, w = e_re.shape
    cw = _tile(w, (1024, 512, 256, 128))

    def body(er_ref, ei_ref, lr_ref, li_ref, sr_ref, si_ref):
        lr, li = lr_ref[...], li_ref[...]

        def step(k, carry):
            sr, si = carry
            row = pl.ds(k, 1)
            sr_ref[row, :] = sr
            si_ref[row, :] = si
            return lr * sr - li * si + er_ref[row, :], li * sr + lr * si + ei_ref[row, :]

        z = jnp.zeros((1, cw), F32)
        lax.fori_loop(0, n, step, (z, z))

    col = pl.BlockSpec((n, cw), lambda j: (0, j))
    one = pl.BlockSpec((1, cw), lambda j: (0, j))
    return pl.pallas_call(
        body,
        name=name,
        grid=(w // cw,),
        in_specs=[col, col, one, one],
        out_specs=[col, col],
        out_shape=[jax.ShapeDtypeStruct((n, w), F32)] * 2,
        compiler_params=_params(("parallel",)),
    )(e_re, e_im, lam_re, lam_im)


def _s5_carry_bwd(dsp_re, dsp_im, sp_re, sp_im, lam_re, lam_im, name):
    n, w = dsp_re.shape
    cw = _tile(w, (1024, 512, 256, 128))

    def body(dr_ref, di_ref, sr_ref, si_ref, lr_ref, li_ref, gr_ref, gi_ref, dlr_ref, dli_ref):
        lr, li = lr_ref[...], li_ref[...]

        def step(q, carry):
            gr_next, gi_next, dr_next, di_next, alr, ali = carry
            k = n - 1 - q
            row = pl.ds(k, 1)
            gr = dr_next + lr * gr_next + li * gi_next
            gi = di_next - li * gr_next + lr * gi_next
            gr_ref[row, :] = gr
            gi_ref[row, :] = gi
            sr, si = sr_ref[row, :], si_ref[row, :]
            return gr, gi, dr_ref[row, :], di_ref[row, :], alr + gr * sr + gi * si, ali + gi * sr - gr * si

        z = jnp.zeros((1, cw), F32)
        out = lax.fori_loop(0, n, step, (z, z, z, z, z, z))
        dlr_ref[...] = out[4]
        dli_ref[...] = out[5]

    col = pl.BlockSpec((n, cw), lambda j: (0, j))
    one = pl.BlockSpec((1, cw), lambda j: (0, j))
    return pl.pallas_call(
        body,
        name=name,
        grid=(w // cw,),
        in_specs=[col, col, col, col, one, one],
        out_specs=[col, col, one, one],
        out_shape=[jax.ShapeDtypeStruct((n, w), F32)] * 2 + [jax.ShapeDtypeStruct((1, w), F32)] * 2,
        compiler_params=_params(("parallel",)),
    )(dsp_re, dsp_im, sp_re, sp_im, lam_re, lam_im)


def _s5_out(prm, u, sp_re, sp_im, name):
    ng, n, w = u.shape
    p = S5_STATE

    def body(*refs):
        ops = _s5_ops(*_s5_load(refs[:7]))
        refs[10][0] = _s5_y(ops, refs[7][0], refs[8][0], refs[9][0])

    blk = lambda a, b: pl.BlockSpec((1, a, b), lambda g: (g, 0, 0))
    return pl.pallas_call(
        body,
        name=name,
        grid=(ng,),
        in_specs=_s5_group_specs(ng) + [blk(n, w), blk(n, p), blk(n, p)],
        out_specs=blk(n, w),
        out_shape=jax.ShapeDtypeStruct((ng, n, w), F32),
        compiler_params=_params(("parallel",)),
    )(*prm, u, sp_re, sp_im)


def _s5_bwd_state(prm, dy, name):
    ng, n, w = dy.shape
    p = S5_STATE

    def body(*refs):
        ops = _s5_ops(*_s5_load(refs[:7]))
        d = refs[7][0]
        refs[8][0] = _dot(d, ops[3], NN, HI)
        refs[9][0] = -_dot(d, ops[4], NN, HI)

    blk = lambda a, b: pl.BlockSpec((1, a, b), lambda g: (g, 0, 0))
    return pl.pallas_call(
        body,
        name=name,
        grid=(ng,),
        in_specs=_s5_group_specs(ng) + [blk(n, w)],
        out_specs=[blk(n, p), blk(n, p)],
        out_shape=[jax.ShapeDtypeStruct((ng, n, p), F32)] * 2,
        compiler_params=_params(("parallel",)),
    )(*prm, dy)


def _s5_bwd_main(prm, u, sp_re, sp_im, dy, g_re, g_im, dlam_re, dlam_im, name):
    ng, n, w = u.shape
    p = S5_STATE

    def body(*refs):
        prm_v = _s5_load(refs[:7])
        uu, sr, si, d, gr, gi, dlr, dli = [r[0] for r in refs[7:15]]
        ops, ops_vjp = jax.vjp(_s5_ops, *prm_v)
        cots = (_dot(d, uu, TN), _dot(uu, gr, TN, HI), _dot(uu, gi, TN, HI), _dot(d, sr, TN, HI), -_dot(d, si, TN, HI),
                dlr, dli)
        grads = ops_vjp(cots)
        for q in range(7):
            refs[15 + q][0] = grads[q]
        refs[22][0] = _dot(d, ops[0], NN) + _dot(gr, ops[1], NT, HI) + _dot(gi, ops[2], NT, HI)

    blk = lambda a, b: pl.BlockSpec((1, a, b), lambda g: (g, 0, 0))
    prm_specs = _s5_group_specs(ng)
    return pl.pallas_call(
        body,
        name=name,
        grid=(ng,),
        in_specs=prm_specs + [blk(n, w), blk(n, p), blk(n, p), blk(n, w), blk(n, p), blk(n, p), blk(1, p), blk(1, p)],
        out_specs=prm_specs + [blk(n, w)],
        out_shape=[jax.ShapeDtypeStruct(a.shape, F32) for a in prm] + [jax.ShapeDtypeStruct((ng, n, w), F32)],
        compiler_params=_params(("parallel",)),
    )(*prm, u, sp_re, sp_im, dy, g_re, g_im, dlam_re, dlam_im)


SLAB_GROUPS = LANE // S5_GROUP
CHUNK_TILES = S5_CHUNK * S5_GROUP // LANE


def _segment_transpose(tiles):
    seg = lax.broadcasted_iota(jnp.int32, (1, LANE), 1) // S5_GROUP
    tiles = list(tiles)
    stride = 1
    while stride < SLAB_GROUPS:
        upper = (seg & stride) != 0
        shift = S5_GROUP * stride
        new = list(tiles)
        for i in range(SLAB_GROUPS):
            if i & stride:
                continue
            j = i + stride
            new[i] = jnp.where(upper, pltpu.roll(tiles[j], shift, 1), tiles[i])
            new[j] = jnp.where(upper, tiles[j], pltpu.roll(tiles[i], LANE - shift, 1))
        tiles = new
        stride *= 2
    return tiles


def _to_groups(h, name):
    t, d = h.shape
    n = t // S5_CHUNK

    def body(x_ref, o_ref):
        for a in range(CHUNK_TILES):
            rows = [x_ref[pl.ds(SLAB_GROUPS * a + b, n, stride=S5_CHUNK), :] for b in range(SLAB_GROUPS)]
            for g, tile in enumerate(_segment_transpose(rows)):
                o_ref[g, :, a * LANE : (a + 1) * LANE] = tile

    return pl.pallas_call(
        body,
        name=name,
        grid=(d // LANE,),
        in_specs=[pl.BlockSpec((t, LANE), lambda q: (0, q))],
        out_specs=pl.BlockSpec((SLAB_GROUPS, n, S5_CHUNK * S5_GROUP), lambda q: (q, 0, 0)),
        out_shape=jax.ShapeDtypeStruct((d // S5_GROUP, n, S5_CHUNK * S5_GROUP), F32),
        compiler_params=_params(("parallel",)),
    )(h)


def _from_groups(y, name):
    ng, n, w = y.shape
    t, d = n * S5_CHUNK, ng * S5_GROUP

    def body(y_ref, x_ref):
        for a in range(CHUNK_TILES):
            tiles = [y_ref[g, :, a * LANE : (a + 1) * LANE] for g in range(SLAB_GROUPS)]
            for b, row in enumerate(_segment_transpose(tiles)):
                x_ref[pl.ds(SLAB_GROUPS * a + b, n, stride=S5_CHUNK), :] = row

    return pl.pallas_call(
        body,
        name=name,
        grid=(d // LANE,),
        in_specs=[pl.BlockSpec((SLAB_GROUPS, n, w), lambda q: (q, 0, 0))],
        out_specs=pl.BlockSpec((t, LANE), lambda q: (0, q)),
        out_shape=jax.ShapeDtypeStruct((t, d), F32),
        compiler_params=_params(("parallel",)),
    )(y)


def _states_to_cols(e):
    ng, n, p = e.shape
    return e.transpose(1, 0, 2).reshape(n, ng * p)


def _cols_to_states(s, ng):
    n = s.shape[0]
    return s.reshape(n, ng, S5_STATE).transpose(1, 0, 2)


def _me():
    return lax.axis_index("x"), lax.axis_index("y"), lax.axis_index("c")


def _flip(v, bit):
    return 1 - v if bit else v


COPIES = N_DEV - 1
ANY_SPEC = pl.BlockSpec(memory_space=pl.ANY)


def _comm_sems(n):
    return [pltpu.SemaphoreType.DMA((COPIES * n,)), pltpu.SemaphoreType.DMA((COPIES * n,)), pltpu.SemaphoreType.DMA((n,))]


def _gather_phases(x_refs, out_refs, send_sems, recv_sems, local_sems):
    mx, my, mc = _me()
    me, sibling = (mx, my, mc), (mx, my, 1 - mc)
    chips = [(1 - mx, my), (mx, 1 - my), (1 - mx, 1 - my)]
    arrays = range(len(x_refs))

    def slot(a, px, py, pc):
        return out_refs[a].at[4 * px + 2 * py + pc]

    def copy(a, k, block, to, from_input=False):
        return pltpu.make_async_remote_copy(
            src_ref=x_refs[a] if from_input else slot(a, *block),
            dst_ref=slot(a, *block),
            send_sem=send_sems.at[COPIES * a + k],
            recv_sem=recv_sems.at[COPIES * a + k],
            device_id=to,
            device_id_type=pl.DeviceIdType.MESH,
        )

    mine = [pltpu.make_async_copy(x_refs[a], slot(a, *me), local_sems.at[a]) for a in arrays]
    first = [copy(a, 0, me, sibling, True) for a in arrays]
    first += [copy(a, 1 + j, me, (*chip, mc), True) for a in arrays for j, chip in enumerate(chips)]
    passed = {(a, j): copy(a, 4 + j, (*chip, mc), sibling) for a in arrays for j, chip in enumerate(chips)}

    def start():
        for cp in mine + first:
            cp.start()

    def relay():
        for j, chip in enumerate(chips):
            for a in arrays:
                copy(a, 1 + j, (*chip, mc), me).wait_recv()
                passed[(a, j)].start()

    def finish():
        for a in arrays:
            copy(a, 0, sibling, me).wait_recv()
            for j, chip in enumerate(chips):
                copy(a, 4 + j, (*chip, 1 - mc), me).wait_recv()
        for cp in first + list(passed.values()):
            cp.wait_send()
        for cp in mine:
            cp.wait()

    return start, relay, finish


def _exchange_phases(g_refs, out_refs, send_sems, recv_sems, local_sems):
    mx, my, mc = _me()
    mine_idx = 4 * mx + 2 * my + mc
    arrays = range(len(g_refs))
    own = [pltpu.make_async_copy(g_refs[a].at[mine_idx], out_refs[a].at[mine_idx], local_sems.at[a]) for a in arrays]
    copies = []
    for m in range(1, N_DEV):
        px, py, pc = _flip(mx, m & 4), _flip(my, m & 2), _flip(mc, m & 1)
        for a in arrays:
            copies.append(
                pltpu.make_async_remote_copy(
                    src_ref=g_refs[a].at[4 * px + 2 * py + pc],
                    dst_ref=out_refs[a].at[mine_idx],
                    send_sem=send_sems.at[COPIES * a + m - 1],
                    recv_sem=recv_sems.at[COPIES * a + m - 1],
                    device_id=(px, py, pc),
                    device_id_type=pl.DeviceIdType.MESH,
                )
            )

    def start():
        for cp in own + copies:
            cp.start()

    def finish():
        for cp in copies:
            cp.wait_recv()
        for cp in copies:
            cp.wait_send()
        for cp in own:
            cp.wait()

    return start, None, finish


def _side_out_shapes(side):
    kind, arrs = side
    return [jax.ShapeDtypeStruct((N_DEV,) + a.shape if kind == "gather" else a.shape, a.dtype) for a in arrs]


def _run_side(side, x_refs, out_refs, sems, nsteps_major, nsteps_minor):
    start, relay, finish = (_gather_phases if side[0] == "gather" else _exchange_phases)(x_refs, out_refs, *sems)
    a, b = pl.program_id(0), pl.program_id(1)
    pl.when(jnp.logical_and(a == 0, b == 0))(start)
    if relay is not None:
        pl.when(jnp.logical_and(a == nsteps_major - 1, b == 0))(relay)
    return lambda: pl.when(jnp.logical_and(a == nsteps_major - 1, b == nsteps_minor - 1))(finish)


def _exchange_call(kind, arrs, name):
    n = len(arrs)

    def body(*refs):
        start, relay, finish = (_gather_phases if kind == "gather" else _exchange_phases)(refs[:n], refs[n : 2 * n], *refs[2 * n :])
        start()
        if relay is not None:
            relay()
        finish()

    return pl.pallas_call(body, name=name, in_specs=[ANY_SPEC] * n, out_specs=[ANY_SPEC] * n,
                          out_shape=_side_out_shapes((kind, arrs)), scratch_shapes=_comm_sems(n))(*arrs)


def _sum_slots(recv, name):
    _, r, c = recv.shape
    tr = _tile(r, (256, 128, 64, 32, 16, 8))

    def body(r_ref, o_ref):
        acc = r_ref[0].astype(F32)
        for k in range(1, N_DEV):
            acc = acc + r_ref[k].astype(F32)
        o_ref[...] = acc

    return pl.pallas_call(
        body,
        name=name,
        grid=(r // tr,),
        in_specs=[pl.BlockSpec((N_DEV, tr, c), lambda i: (0, i, 0))],
        out_specs=pl.BlockSpec((tr, c), lambda i: (i, 0)),
        out_shape=jax.ShapeDtypeStruct((r, c), F32),
        compiler_params=_params(("parallel",)),
    )(recv)


def _adamw(w, g, m, v, name):
    shape = w.shape
    c = shape[-1]
    as2d = lambda a: a.reshape(-1, c)
    w2, g2, m2, v2 = as2d(w), as2d(g), as2d(m), as2d(v)
    r = w2.shape[0]
    tr = _tile(r, (256, 128, 64, 32, 16, 8))
    c1 = 1.0 / (1.0 - ADAM_B1 ** ADAM_STEP)
    c2 = 1.0 / (1.0 - ADAM_B2 ** ADAM_STEP)

    def body(w_ref, g_ref, m_ref, v_ref, d_ref, mo_ref, vo_ref):
        gg = g_ref[...]
        mn = ADAM_B1 * m_ref[...] + (1.0 - ADAM_B1) * gg
        vn = ADAM_B2 * v_ref[...] + (1.0 - ADAM_B2) * (gg * gg)
        d_ref[...] = -ADAM_LR * ((mn * c1) / (jnp.sqrt(vn * c2) + ADAM_EPS) + ADAM_WD * w_ref[...])
        mo_ref[...] = mn
        vo_ref[...] = vn

    spec = pl.BlockSpec((tr, c), lambda i: (i, 0))
    d, mn, vn = pl.pallas_call(
        body,
        name=name,
        grid=(r // tr,),
        in_specs=[spec] * 4,
        out_specs=[spec] * 3,
        out_shape=[jax.ShapeDtypeStruct((r, c), F32)] * 3,
        compiler_params=_params(("parallel",)),
    )(w2, g2, m2, v2)
    return d.reshape(shape), mn.reshape(shape), vn.reshape(shape)


COL_SHARDED = ("attn_w_in", "mla_w_q_b", "mla_w_kv_b", "s5_d", "s5_w_glu", "ffn_w_in", "ple_w")
ROW_SHARDED = ("attn_w_out", "ffn_w_out", "ple_gate_w")
REPLICATED = ("mla_q_norm", "mla_kv_norm", "s5_a_re", "s5_a_im", "s5_log_dt", "s5_b_re", "s5_b_im", "s5_c_re",
              "s5_c_im", "ln1_g", "ln1_b", "ln2_g", "ln2_b")
ATTENTION_WEIGHTS = ("attn_w_in", "mla_w_q_b", "mla_w_kv_b")
BIG_WEIGHTS = ("attn_w_in", "mla_w_q_b", "mla_w_kv_b", "attn_w_out", "s5_w_glu", "ffn_w_in", "ffn_w_out", "ple_w", "ple_gate_w")
WEIGHTS = ("attn_w_in", "mla_q_norm", "mla_w_q_b", "mla_kv_norm", "mla_w_kv_b", "attn_w_out", "s5_a_re", "s5_a_im",
           "s5_log_dt", "s5_b_re", "s5_b_im", "s5_c_re", "s5_c_im", "s5_d", "s5_w_glu", "ln1_g", "ln1_b", "ffn_w_in",
           "ffn_w_out", "ple_w", "ple_gate_w", "ln2_g", "ln2_b")


def _unshard(name, gathered):
    if name in COL_SHARDED:
        full = jnp.moveaxis(gathered, 0, -2)
        return full.reshape(full.shape[:-2] + (full.shape[-2] * full.shape[-1],))
    full = jnp.moveaxis(gathered, 0, 1)
    return full.reshape((full.shape[0], full.shape[1] * full.shape[2]) + full.shape[3:])


def _by_owner(name, grad):
    if name in COL_SHARDED:
        g = grad.reshape(grad.shape[:-1] + (N_DEV, grad.shape[-1] // N_DEV))
        return jnp.moveaxis(g, -2, 0).reshape(N_DEV, -1)
    if name in ROW_SHARDED:
        g = grad.reshape((grad.shape[0], N_DEV, grad.shape[1] // N_DEV) + grad.shape[2:])
        return jnp.moveaxis(g, 1, 0).reshape(N_DEV, -1)
    return jnp.broadcast_to(grad.reshape(1, -1), (N_DEV, grad.size))


def _owner_rows(name, g):
    k, n = g.shape
    if name in COL_SHARDED:
        return g.reshape(k, N_DEV, n // N_DEV).transpose(1, 0, 2)
    assert name in ROW_SHARDED, name
    return g.reshape(N_DEV, k // N_DEV, n)


def _pack(pieces, axis):
    blocks, where, row = [], [], 0
    for p in pieces:
        n = p.shape[axis]
        pad = (-n) % (PACK_COLS * PACK_ROW_TILE)
        if pad:
            shape = list(p.shape)
            shape[axis] = pad
            p = jnp.concatenate([p, jnp.zeros(shape, p.dtype)], axis=axis)
        rows = (n + pad) // PACK_COLS
        blocks.append(p.reshape(p.shape[:axis] + (rows, PACK_COLS)))
        where.append((row, rows, n))
        row += rows
    return jnp.concatenate(blocks, axis=axis), where


def _unpack(buf, axis, where):
    row, rows, n = where
    part = lax.slice_in_dim(buf, row, row + rows, axis=axis)
    return lax.slice_in_dim(part.reshape(part.shape[:axis] + (rows * PACK_COLS,)), 0, n, axis=axis)


def _twice(fn):
    return lambda *a: fn(*a) * 2


def _ffn_block(h, hb, p_i, w_in, w_out, w_ple, w_pg, g, b, alpha, tag):
    t, d = h.shape
    hid = w_out.shape[0]
    cw = _tile(hid, (1408, 512, 256, 128))
    ncb = hid // cw
    gu = _mm(hb, w_in, out_dtype=BF16, name=f"ffn_in_{tag}")
    act = _rowwise(_swiglu_fn, [(gu, cw, 0), (gu, cw, ncb)], [], [(hid, cw, BF16)], name=f"swiglu_{tag}", tq=1024, ncol=ncb)[0]
    f = _mm(act, w_out, name=f"ffn_out_{tag}")
    pw = _mm(p_i, w_ple, name=f"ple_{tag}")
    gate = _mm(hb, w_pg, name=f"ple_gate_{tag}")
    out, outb = _rowwise(_twice(_ln_ffn_fn(alpha)), [_whole(h), _whole(f), _whole(pw), _whole(gate)], [g, b],
                         [(d, d, F32), (d, d, BF16)], name=f"ln2_{tag}", tq=256)
    return out, outb, (h, hb, p_i, gu, act, f, pw, gate)


def _ffn_block_bwd(saved, dout, w_in, w_out, w_pg, g, b, alpha, tag):
    h, hb, p_i, gu, act, f, pw, gate = saved
    t, d = h.shape
    hid = w_out.shape[0]
    cw = _tile(hid, (1408, 512, 256, 128))
    ncb = hid // cw
    dh_a, df, dpw, dgate, dg, db = _rowwise_bwd(
        _ln_ffn_fn(alpha), [_whole(h), _whole(f), _whole(pw), _whole(gate)], [g, b], [dout],
        need=[True] * 4, drow=[(d, F32), (d, BF16), (d, BF16), (d, BF16)], name=f"ln2_bwd_{tag}", tq=128)
    dw_out = _mm(act, df, ta=True, out_dtype=BF16, name=f"ffn_out_dw_{tag}")
    dact = _mm(df, w_out, tb=True, out_dtype=BF16, name=f"ffn_out_dx_{tag}")
    dg_, du_ = _rowwise_bwd(_swiglu_fn, [(gu, cw, 0), (gu, cw, ncb)], [], [[(dact, cw, 0)]], need=[True, True],
                            drow=[(hid, BF16), (hid, BF16)], name=f"swiglu_bwd_{tag}", tq=512, ncol=ncb)
    dgu = jnp.concatenate([dg_, du_], axis=1)
    dw_in = _mm(hb, dgu, ta=True, out_dtype=BF16, name=f"ffn_in_dw_{tag}")
    dh_b = _mm(dgu, w_in, tb=True, name=f"ffn_in_dx_{tag}")
    dw_ple = _mm(p_i, dpw, ta=True, out_dtype=BF16, name=f"ple_dw_{tag}")
    dw_pg = _mm(hb, dgate, ta=True, out_dtype=BF16, name=f"ple_gate_dw_{tag}")
    dh_c = _mm(dgate, w_pg, tb=True, name=f"ple_gate_dx_{tag}")
    return [dh_a, dh_b, dh_c], dict(ffn_w_in=dw_in, ffn_w_out=dw_out, ple_w=dw_ple, ple_gate_w=dw_pg, ln2_g=dg, ln2_b=db)


def kernel(x, p, positions, attn_w_in, mla_q_norm, mla_w_q_b, mla_kv_norm, mla_w_kv_b, attn_w_out, s5_a_re, s5_a_im, s5_log_dt, s5_b_re, s5_b_im, s5_c_re, s5_c_im, s5_d, s5_w_glu, ln1_g, ln1_b, ffn_w_in, ffn_w_out, ple_w, ple_gate_w, ln2_g, ln2_b, loss_target, m_attn_w_in, m_mla_q_norm, m_mla_w_q_b, m_mla_kv_norm, m_mla_w_kv_b, m_attn_w_out, m_s5_a_re, m_s5_a_im, m_s5_log_dt, m_s5_b_re, m_s5_b_im, m_s5_c_re, m_s5_c_im, m_s5_d, m_s5_w_glu, m_ln1_g, m_ln1_b, m_ffn_w_in, m_ffn_w_out, m_ple_w, m_ple_gate_w, m_ln2_g, m_ln2_b, v_attn_w_in, v_mla_q_norm, v_mla_w_q_b, v_mla_kv_norm, v_mla_w_kv_b, v_attn_w_out, v_s5_a_re, v_s5_a_im, v_s5_log_dt, v_s5_b_re, v_s5_b_im, v_s5_c_re, v_s5_c_im, v_s5_d, v_s5_w_glu, v_ln1_g, v_ln1_b, v_ffn_w_in, v_ffn_w_out, v_ple_w, v_ple_gate_w, v_ln2_g, v_ln2_b):
    local = dict(attn_w_in=attn_w_in, mla_q_norm=mla_q_norm, mla_w_q_b=mla_w_q_b, mla_kv_norm=mla_kv_norm,
                 mla_w_kv_b=mla_w_kv_b, attn_w_out=attn_w_out, s5_a_re=s5_a_re, s5_a_im=s5_a_im, s5_log_dt=s5_log_dt,
                 s5_b_re=s5_b_re, s5_b_im=s5_b_im, s5_c_re=s5_c_re, s5_c_im=s5_c_im, s5_d=s5_d, s5_w_glu=s5_w_glu,
                 ln1_g=ln1_g, ln1_b=ln1_b, ffn_w_in=ffn_w_in, ffn_w_out=ffn_w_out, ple_w=ple_w, ple_gate_w=ple_gate_w,
                 ln2_g=ln2_g, ln2_b=ln2_b)
    mom_m = dict(zip(WEIGHTS, (m_attn_w_in, m_mla_q_norm, m_mla_w_q_b, m_mla_kv_norm, m_mla_w_kv_b, m_attn_w_out, m_s5_a_re, m_s5_a_im, m_s5_log_dt, m_s5_b_re, m_s5_b_im, m_s5_c_re, m_s5_c_im, m_s5_d, m_s5_w_glu, m_ln1_g, m_ln1_b, m_ffn_w_in, m_ffn_w_out, m_ple_w, m_ple_gate_w, m_ln2_g, m_ln2_b)))
    mom_v = dict(zip(WEIGHTS, (v_attn_w_in, v_mla_q_norm, v_mla_w_q_b, v_mla_kv_norm, v_mla_w_kv_b, v_attn_w_out, v_s5_a_re, v_s5_a_im, v_s5_log_dt, v_s5_b_re, v_s5_b_im, v_s5_c_re, v_s5_c_im, v_s5_d, v_s5_w_glu, v_ln1_g, v_ln1_b, v_ffn_w_in, v_ffn_w_out, v_ple_w, v_ple_gate_w, v_ln2_g, v_ln2_b)))

    t, d = x.shape[1], x.shape[2]
    depth = ln1_g.shape[0]
    alpha = (2.0 * depth) ** 0.25
    ql, kvl = mla_q_norm.shape[-1], mla_kv_norm.shape[-1]
    nh = mla_w_q_b.shape[-1] * N_DEV // (NOPE_DIM + ROPE_DIM)
    dw = (attn_w_in.shape[-1] * N_DEV - ql - kvl - ROPE_DIM) // 3
    ndh = dw // DIL_HEAD_DIM
    ng = d // S5_GROUP
    assert ql == kvl and (ql + kvl) % LANE == 0 and nh * V_DIM == dw

    sharded = [n for n in WEIGHTS if n in COL_SHARDED or n in ROW_SHARDED]
    def as_sent(n):
        if n == "s5_d":
            return local[n]
        if n == "attn_w_in":
            return local[n][0].T.astype(BF16)
        return local[n].astype(BF16).reshape(-1, local[n].shape[-1])

    full = {}

    def take_weights(names, gathered):
        for n, got in zip(names, gathered):
            if n == "attn_w_in":
                full[n] = got.reshape(N_DEV * local[n].shape[2], d)
            else:
                full[n] = _unshard(n, got.reshape((N_DEV,) + local[n].shape))

    first_w = [n for n in sharded if n in ATTENTION_WEIGHTS]
    later_w = [n for n in sharded if n not in ATTENTION_WEIGHTS]
    take_weights(first_w, _exchange_call("gather", [as_sent(n) for n in first_w], "gather_attn_weights"))

    w_in_t = full["attn_w_in"]
    lat = ql + kvl
    w_lat_t = jnp.concatenate([w_in_t[: lat + ROPE_DIM], jnp.zeros((LANE - ROPE_DIM, d), BF16)], axis=0)
    w_dil_t = w_in_t[lat + ROPE_DIM :]
    wq = full["mla_w_q_b"][0].reshape(ql, nh, NOPE_DIM + ROPE_DIM)
    wq_pe = jnp.pad(wq[:, :, NOPE_DIM:], ((0, 0), (0, 0), (0, LANE - ROPE_DIM)))
    wqp = jnp.concatenate([wq[:, :, :NOPE_DIM].reshape(ql, nh * LANE), wq_pe.reshape(ql, nh * LANE)], axis=1)
    wkv = full["mla_w_kv_b"][0].reshape(kvl, nh, NOPE_DIM + V_DIM)
    wkvp = jnp.concatenate([wkv[:, :, :NOPE_DIM].reshape(kvl, nh * LANE), wkv[:, :, NOPE_DIM:].reshape(kvl, nh * LANE)], axis=1)

    h0 = x[0]
    h0b = h0.astype(BF16)
    pb = p.astype(BF16)
    target = loss_target[0]
    pos = positions.reshape(t, 1)
    grads = {}

    z = _mm(h0b, w_lat_t, tb=True, name="attn_in_lat")
    zd = _mm(h0b, w_dil_t, tb=True, out_dtype=BF16, name="attn_in_dil")
    gq, gk = mla_q_norm.reshape(1, ql), mla_kv_norm.reshape(1, kvl)
    qn, kvn = _rowwise(_rms_fn, [(z, ql, 0), (z, kvl, 1)], [gq, gk], [(ql, ql, BF16), (kvl, kvl, BF16)], name="rms", tq=256)
    qf = _mm(qn, wqp, name="q_up")
    kvf = _mm(kvn, wkvp, out_dtype=BF16, name="kv_up")
    cos, sin = _rope_tables(pos, "rope_tables")
    pe_cb = lat // LANE
    qpe, kpe = _rowwise(_rope_fwd_fn(nh), [(qf, nh * LANE, 1), (z, LANE, pe_cb), _whole(cos), _whole(sin)], [],
                        [(nh * LANE, nh * LANE, BF16), (LANE, LANE, BF16)], name="rope", tq=256)
    out_a, lse_a, later_gathered = _mla_fwd(qf, qpe, kvf, kpe, nh, "mla_fwd", side=("gather", [as_sent(n) for n in later_w]))
    take_weights(later_w, later_gathered)
    w_out = full["attn_w_out"][0]
    w_glu = full["s5_w_glu"][0]
    d_skip = full["s5_d"]

    def to_classes(a, dil):
        if dil == 1:
            return a
        return a.reshape(t // dil, dil, a.shape[1]).transpose(1, 0, 2).reshape(t, a.shape[1])

    def from_classes(a, dil):
        if dil == 1:
            return a
        return a.reshape(dil, t // dil, a.shape[1]).transpose(1, 0, 2).reshape(t, a.shape[1])

    band = []
    for window, dil in DIL_BRANCHES:
        assert window // dil == BLK
        zc = to_classes(zd, dil)
        o_c, l_c = _band_fwd(zc, ndh, dil, f"band_fwd_d{dil}")
        band.append((dil, zc, o_c, l_c, from_classes(o_c, dil), from_classes(l_c, dil)))
    merge_rows = [_whole(b[4]) for b in band] + [_whole(b[5]) for b in band]
    out_b = _rowwise(_merge_fn, merge_rows, [], [(dw, dw, BF16)], name="merge", tq=256)[0]
    att = jnp.concatenate([out_a, out_b], axis=1)
    mix0 = _mm(att, w_out, name="attn_out")
    g1, b1 = ln1_g[0:1], ln1_b[0:1]
    h1, h1b = _rowwise(_twice(_ln_mix_fn(alpha)), [_whole(h0), _whole(mix0)], [g1, b1], [(d, d, F32), (d, d, BF16)],
                       name="ln1_l0", tq=256)
    h2, _, saved_f0 = _ffn_block(h1, h1b, pb[0, 0], full["ffn_w_in"][0], full["ffn_w_out"][0], full["ple_w"][0],
                                 full["ple_gate_w"][0], ln2_g[0:1], ln2_b[0:1], alpha, "l0")

    prm = [s5_a_re[0].reshape(ng, 1, S5_STATE), s5_a_im[0].reshape(ng, 1, S5_STATE), s5_log_dt[0].reshape(ng, 1, 1),
           s5_b_re[0].transpose(0, 2, 1), s5_b_im[0].transpose(0, 2, 1), s5_c_re[0], s5_c_im[0]]
    u = _to_groups(h2, "s5_regroup_u")
    e_re, e_im, lam_re, lam_im = _s5_local(prm, u, "s5_local")
    lam_re_c, lam_im_c = lam_re.reshape(1, ng * S5_STATE), lam_im.reshape(1, ng * S5_STATE)
    sp_re_c, sp_im_c = _s5_carry(_states_to_cols(e_re), _states_to_cols(e_im), lam_re_c, lam_im_c, "s5_carry")
    sp_re, sp_im = _cols_to_states(sp_re_c, ng), _cols_to_states(sp_im_c, ng)
    ys = _from_groups(_s5_out(prm, u, sp_re, sp_im, "s5_out"), "s5_ungroup_y")
    z5 = _rowwise(_gelu_fn, [_whole(ys), _whole(h2)], [d_skip], [(d, d, BF16)], name="gelu", tq=256)[0]
    vg = _mm(z5, w_glu, name="glu_in")
    g3, b3 = ln1_g[1:2], ln1_b[1:2]
    h3, h3b = _rowwise(_twice(_ln_glu_fn(alpha)), [_whole(h2), (vg, d, 0), (vg, d, 1)], [g3, b3],
                       [(d, d, F32), (d, d, BF16)], name="ln1_l1", tq=256)
    h4, _, saved_f1 = _ffn_block(h3, h3b, pb[1, 0], full["ffn_w_in"][1], full["ffn_w_out"][1], full["ple_w"][1],
                                 full["ple_gate_w"][1], ln2_g[1:2], ln2_b[1:2], alpha, "l1")

    dh4, loss_acc = _loss_kernel(h4, target, "loss")
    loss = lax.psum(loss_acc[0, 0], AXES)

    dh3, gf1 = _ffn_block_bwd(saved_f1, [_whole(dh4)], full["ffn_w_in"][1], full["ffn_w_out"][1],
                              full["ple_gate_w"][1], ln2_g[1:2], ln2_b[1:2], alpha, "l1")
    dh2_a, dval, dgate, dg3, db3 = _rowwise_bwd(
        _ln_glu_fn(alpha), [_whole(h2), (vg, d, 0), (vg, d, 1)], [g3, b3], [[_whole(a) for a in dh3]],
        need=[True] * 3, drow=[(d, F32), (d, BF16), (d, BF16)], name="ln1_bwd_l1", tq=128)
    dvg = jnp.concatenate([dval, dgate], axis=1)
    dw_glu = _mm(z5, dvg, ta=True, out_dtype=BF16, name="glu_dw")
    dz5 = _mm(dvg, w_glu, tb=True, name="glu_dx")
    dys, dh2_b, dd = _rowwise_bwd(_gelu_fn, [_whole(ys), _whole(h2)], [d_skip], [[_whole(dz5)]], need=[True, True],
                                  drow=[(d, F32), (d, F32)], name="gelu_bwd", tq=128)
    grads["s5_d"] = dd
    dy = _to_groups(dys, "s5_regroup_dy")
    dsp_re, dsp_im = _s5_bwd_state(prm, dy, "s5_bwd_state")
    g_re_c, g_im_c, dlam_re_c, dlam_im_c = _s5_carry_bwd(_states_to_cols(dsp_re), _states_to_cols(dsp_im), sp_re_c, sp_im_c,
                                                         lam_re_c, lam_im_c, "s5_carry_bwd")
    s5g = _s5_bwd_main(prm, u, sp_re, sp_im, dy, _cols_to_states(g_re_c, ng), _cols_to_states(g_im_c, ng),
                       dlam_re_c.reshape(ng, 1, S5_STATE), dlam_im_c.reshape(ng, 1, S5_STATE), "s5_bwd_main")
    grads["s5_a_re"] = s5g[0].reshape(s5_a_re.shape)
    grads["s5_a_im"] = s5g[1].reshape(s5_a_im.shape)
    grads["s5_log_dt"] = s5g[2].reshape(s5_log_dt.shape)
    grads["s5_b_re"] = s5g[3].transpose(0, 2, 1)[None]
    grads["s5_b_im"] = s5g[4].transpose(0, 2, 1)[None]
    grads["s5_c_re"] = s5g[5][None]
    grads["s5_c_im"] = s5g[6][None]
    dh2_c = _from_groups(s5g[7], "s5_ungroup_du")

    dh1, gf0 = _ffn_block_bwd(saved_f0, [_whole(dh2_a), _whole(dh2_b), _whole(dh2_c)], full["ffn_w_in"][0],
                              full["ffn_w_out"][0], full["ple_gate_w"][0], ln2_g[0:1], ln2_b[0:1], alpha, "l0")
    for k in ("ln2_g", "ln2_b"):
        grads[k] = jnp.concatenate([gf0[k], gf1[k]])

    shards = {}

    def take(items, received):
        for (key, _), recv in zip(items, received):
            shards[key] = _sum_slots(recv, f"sum_grads_{key[0]}_{key[1]}")

    ffn_names = ("ffn_w_in", "ffn_w_out", "ple_w", "ple_gate_w")
    items_l1 = [(("s5_w_glu", 0), _owner_rows("s5_w_glu", dw_glu))] + [((k, 1), _owner_rows(k, gf1[k])) for k in ffn_names]
    items_l0 = [((k, 0), _owner_rows(k, gf0[k])) for k in ffn_names]
    dh0_a, dmix, dg1, db1 = _rowwise_bwd(_ln_mix_fn(alpha), [_whole(h0), _whole(mix0)], [g1, b1], [[_whole(a) for a in dh1]],
                                         need=[True, True], drow=[(d, F32), (d, BF16)], name="ln1_bwd_l0", tq=128)
    grads["ln1_g"] = jnp.concatenate([dg1, dg3])
    grads["ln1_b"] = jnp.concatenate([db1, db3])
    dw_out = _mm(att, dmix, ta=True, out_dtype=BF16, name="attn_out_dw")
    datt = _mm(dmix, w_out, tb=True, name="attn_out_dx")
    items_l0.append((("attn_w_out", 0), _owner_rows("attn_w_out", dw_out)))
    small_early = [n for n in WEIGHTS if n not in BIG_WEIGHTS and n not in ("mla_q_norm", "mla_kv_norm")]
    send_small, where_small = _pack([_by_owner(n, grads[n]).astype(F32) for n in small_early], 1)

    dmerge = _rowwise_bwd(_merge_fn, merge_rows, [], [[(datt, dw, 1)]], need=[True] * 6, drow=[(dw, F32)] * 6,
                          name="merge_bwd", tq=128)
    dq_s = dk_s = dv_s = None
    for k, (dil, zc, o_c, l_c, _, _) in enumerate(band):
        do_c, dl_c = to_classes(dmerge[k], dil), to_classes(dmerge[3 + k], dil)
        dq_c, dk_c, dv_c = _band_bwd(zc, o_c, l_c, do_c, dl_c, ndh, dil, f"band_bwd_d{dil}")
        dq_n, dk_n, dv_n = from_classes(dq_c, dil), from_classes(dk_c, dil), from_classes(dv_c, dil)
        dq_s = dq_n if dq_s is None else dq_s + dq_n
        dk_s = dk_n if dk_s is None else dk_s + dk_n
        dv_s = dv_n if dv_s is None else dv_s + dv_n

    dqn, dqp, delta, recv_l1 = _mla_bwd_q(qf, qpe, kvf, kpe, datt, out_a, lse_a, nh, "mla_bwd_q",
                                          side=("exchange", [rows for _, rows in items_l1]))
    take(items_l1, recv_l1)
    to_row = lambda a: a[:, ::LANE].T.reshape(nh, 1, t)
    dkn, dkp, dv, recv_l0 = _mla_bwd_kv(qf[:, : nh * LANE].astype(BF16), qpe, kvf, kpe, datt[:, : nh * LANE].astype(BF16),
                                        to_row(lse_a), to_row(delta), nh, "mla_bwd_kv",
                                        side=("exchange", [rows for _, rows in items_l0] + [send_small]))
    take(items_l0, recv_l0[:-1])
    sum_small = _sum_slots(recv_l0[-1], "sum_grads_small")
    dq_pe, dk_pe = _rowwise(_rope_bwd_fn(nh), [_whole(dqp), _whole(dkp), _whole(cos), _whole(sin)], [],
                            [(nh * LANE, nh * LANE, BF16), (LANE, LANE, BF16)], name="rope_bwd", tq=256)
    dqf = jnp.concatenate([dqn.astype(BF16), dq_pe], axis=1)
    dkvf = jnp.concatenate([dkn, dv], axis=1).astype(BF16)
    dwqp = _mm(qn, dqf, ta=True, name="q_up_dw")
    dqn_in = _mm(dqf, wqp, tb=True, name="q_up_dx")
    dwkvp = _mm(kvn, dkvf, ta=True, name="kv_up_dw")
    dkvn_in = _mm(dkvf, wkvp, tb=True, name="kv_up_dx")
    dql, dkvl, dgq, dgk = _rowwise_bwd(_rms_fn, [(z, ql, 0), (z, kvl, 1)], [gq, gk], [[_whole(dqn_in)], [_whole(dkvn_in)]],
                                       need=[True, True], drow=[(ql, BF16), (kvl, BF16)], name="rms_bwd", tq=256)
    grads["mla_q_norm"], grads["mla_kv_norm"] = dgq, dgk
    dz_lat = jnp.concatenate([dql, dkvl, dk_pe], axis=1)
    dz_dil = jnp.concatenate([dq_s, dk_s, dv_s], axis=1).astype(BF16)
    dw_lat_t = _mm(dz_lat, h0b, ta=True, out_dtype=BF16, name="attn_in_lat_dw")
    dw_dil_t = _mm(dz_dil, h0b, ta=True, out_dtype=BF16, name="attn_in_dil_dw")
    dh0_b = _mm(dz_lat, w_lat_t, name="attn_in_lat_dx")
    dh0_c = _mm(dz_dil, w_dil_t, name="attn_in_dil_dx")
    grad_x = _addn([dh0_a, dh0_b, dh0_c], "grad_x")[None]
    dw_in_t = jnp.concatenate([dw_lat_t[: lat + ROPE_DIM], dw_dil_t], axis=0)
    dwq_n = dwqp[:, : nh * LANE].reshape(ql, nh, NOPE_DIM)
    dwq_r = dwqp[:, nh * LANE :].reshape(ql, nh, LANE)[:, :, :ROPE_DIM]
    dwq = jnp.concatenate([dwq_n, dwq_r], axis=2).reshape(ql, nh * (NOPE_DIM + ROPE_DIM))
    dwkv_k = dwkvp[:, : nh * LANE].reshape(kvl, nh, NOPE_DIM)
    dwkv_v = dwkvp[:, nh * LANE :].reshape(kvl, nh, V_DIM)
    dwkv = jnp.concatenate([dwkv_k, dwkv_v], axis=2).reshape(kvl, nh * (NOPE_DIM + V_DIM))

    items_att = [(("attn_w_in", 0), dw_in_t.reshape(N_DEV, -1, PACK_COLS)),
                 (("mla_w_q_b", 0), _owner_rows("mla_w_q_b", dwq.astype(BF16))),
                 (("mla_w_kv_b", 0), _owner_rows("mla_w_kv_b", dwkv.astype(BF16)))]
    small_late = ["mla_q_norm", "mla_kv_norm"]
    send_late, where_late = _pack([_by_owner(n, grads[n]).astype(F32) for n in small_late], 1)
    received = _exchange_call("exchange", [rows for _, rows in items_att] + [send_late], "exchange_grads_last")
    take(items_att, received[:-1])
    sum_late = _sum_slots(received[-1], "sum_grads_norms")

    g_out, d_out, m_out, v_out = [], [], [], []
    for n in WEIGHTS:
        if n == "attn_w_in":
            g = shards[(n, 0)].reshape(local[n].shape[2], d).T[None]
        elif n in BIG_WEIGHTS:
            g = jnp.stack([shards[(n, layer)] for layer in range(local[n].shape[0])]).reshape(local[n].shape)
        elif n in small_late:
            g = _unpack(sum_late, 0, where_late[small_late.index(n)]).reshape(local[n].shape)
        else:
            g = _unpack(sum_small, 0, where_small[small_early.index(n)]).reshape(local[n].shape)
        dlt, mn, vn = _adamw(local[n], g, mom_m[n], mom_v[n], f"adamw_{n}")
        g_out.append(g)
        d_out.append(dlt)
        m_out.append(mn)
        v_out.append(vn)
    return (loss, grad_x, *g_out, *d_out, *m_out, *v_out)
```

```python
import functools
import math

import numpy as np
import jax
import jax.numpy as jnp
from jax import lax
from jax.experimental import pallas as pl
from jax.experimental.pallas import tpu as pltpu

F32 = jnp.float32
BF16 = jnp.bfloat16

NOPE_DIM, ROPE_DIM, V_DIM = 128, 64, 128
DIL_HEAD_DIM = 128
DIL_BRANCHES = ((128, 1), (512, 4), (2048, 16))
BLK = 128
LANE = 128
ROPE_THETA = 10000.0
S5_GROUP, S5_STATE = 16, 64
S5_CHUNK = 32
NEG = -1e30
ADAM_LR, ADAM_B1, ADAM_B2, ADAM_EPS, ADAM_WD, ADAM_STEP = 0.001, 0.9, 0.999, 1e-08, 0.01, 10
N_DEV = 8
AXES = ("x", "y", "c")
VMEM_LIMIT = 56 * 1024 * 1024
MM_VMEM_BUDGET = 40 * 1024 * 1024
PACK_COLS = 1024
PACK_ROW_TILE = 16
HI = lax.Precision.HIGH


def _tile(n, cands):
    for c in cands:
        if n % c == 0:
            return c
    return n


def _params(sem=None):
    return pltpu.CompilerParams(dimension_semantics=sem, vmem_limit_bytes=VMEM_LIMIT)


def _dot(a, b, dims, prec=None):
    return lax.dot_general(a, b, (dims, ((), ())), preferred_element_type=F32, precision=prec)


NN = ((1,), (0,))
NT = ((1,), (1,))
TN = ((0,), (0,))


def _mm(a, b, *, ta=False, tb=False, out_dtype=F32, name):
    (k, m) = a.shape if ta else a.shape[::-1]
    (n, k2) = b.shape if tb else b.shape[::-1]
    assert k == k2, (a.shape, b.shape, ta, tb)
    tm = _tile(m, (1024, 512, 256, 128))
    tn = _tile(n, (1024, 512, 384, 256, 128))
    sa, sb, so = a.dtype.itemsize, b.dtype.itemsize, jnp.dtype(out_dtype).itemsize

    def fits(tk):
        return 2 * (tm * tk * sa + tk * tn * sb) + 2 * tm * tn * so + 4 * tm * tn <= MM_VMEM_BUDGET

    tk = next((c for c in (2816, 2048, 1536, 1408, 1152, 1024, 768, 512, 384, 256, 128) if k % c == 0 and fits(c)), k)
    nk = k // tk
    dims = (((0,) if ta else (1,)), ((1,) if tb else (0,)))

    def body(a_ref, b_ref, o_ref, *acc):
        part = _dot(a_ref[...].astype(BF16), b_ref[...].astype(BF16), dims)
        if nk == 1:
            o_ref[...] = part.astype(o_ref.dtype)
            return
        (acc_ref,) = acc
        kk = pl.program_id(2)

        @pl.when(kk == 0)
        def _():
            acc_ref[...] = part

        @pl.when(kk > 0)
        def _():
            acc_ref[...] += part

        @pl.when(kk == nk - 1)
        def _():
            o_ref[...] = acc_ref[...].astype(o_ref.dtype)

    a_spec = pl.BlockSpec((tk, tm), lambda i, j, kk: (kk, i)) if ta else pl.BlockSpec((tm, tk), lambda i, j, kk: (i, kk))
    b_spec = pl.BlockSpec((tn, tk), lambda i, j, kk: (j, kk)) if tb else pl.BlockSpec((tk, tn), lambda i, j, kk: (kk, j))
    return pl.pallas_call(
        body,
        name=name,
        grid=(m // tm, n // tn, nk),
        in_specs=[a_spec, b_spec],
        out_specs=pl.BlockSpec((tm, tn), lambda i, j, kk: (i, j)),
        out_shape=jax.ShapeDtypeStruct((m, n), out_dtype),
        scratch_shapes=[pltpu.VMEM((tm, tn), F32)] if nk > 1 else [],
        compiler_params=_params(("parallel", "parallel", "arbitrary")),
    )(a, b)


def _row_spec(spec, tq):
    _, w, cb = spec
    return pl.BlockSpec((tq, w), lambda i, j: (i, cb + j))


def _full_spec(p):
    return pl.BlockSpec(p.shape, lambda i, j: (0,) * p.ndim)


def _rowwise(fn, rows, pars, outs, *, name, tq, ncol=1):
    t = rows[0][0].shape[0]
    tq = _tile(t, (tq, 128, 64, 32, 16, 8))
    nr, npar = len(rows), len(pars)

    def body(*refs):
        vals = [r[...].astype(F32) for r in refs[: nr + npar]]
        res = fn(*vals)
        for o, r in zip(refs[nr + npar :], res):
            o[...] = r.astype(o.dtype)

    res = pl.pallas_call(
        body,
        name=name,
        grid=(t // tq, ncol),
        in_specs=[_row_spec(s, tq) for s in rows] + [_full_spec(p) for p in pars],
        out_specs=[pl.BlockSpec((tq, w), lambda i, j: (i, j)) for (_, w, _) in outs],
        out_shape=[jax.ShapeDtypeStruct((t, wt), dt) for (wt, _, dt) in outs],
        compiler_params=_params(("parallel", "parallel")),
    )(*[s[0] for s in rows], *pars)
    return res


def _rowwise_bwd(fn, rows, pars, cots, *, need, drow, name, tq, ncol=1):
    t = rows[0][0].shape[0]
    tq = _tile(t, (tq, 128, 64, 32, 16, 8))
    nr, npar = len(rows), len(pars)
    assert ncol == 1 or npar == 0
    flat_cots = [s for c in cots for s in c]
    ncot = len(flat_cots)
    want = [k for k in range(nr) if need[k]]

    def body(*refs):
        vals = [r[...].astype(F32) for r in refs[: nr + npar]]
        cref = refs[nr + npar : nr + npar + ncot]
        oref = refs[nr + npar + ncot :]
        cvals, pos = [], 0
        for c in cots:
            acc = cref[pos][...].astype(F32)
            for q in range(1, len(c)):
                acc = acc + cref[pos + q][...].astype(F32)
            cvals.append(acc)
            pos += len(c)

        def closed(*diff):
            full = list(vals)
            for k, dv in zip(want + list(range(nr, nr + npar)), diff):
                full[k] = dv
            return tuple(fn(*full))

        diff_in = [vals[k] for k in want] + vals[nr:]
        _, vjp = jax.vjp(closed, *diff_in)
        grads = vjp(tuple(cvals))
        for q in range(len(want)):
            oref[q][...] = grads[q].astype(oref[q].dtype)
        if npar:
            first = pl.program_id(0) == 0

            @pl.when(first)
            def _():
                for q in range(npar):
                    oref[len(want) + q][...] = jnp.zeros_like(oref[len(want) + q])

            for q in range(npar):
                oref[len(want) + q][...] += grads[len(want) + q]

    out_specs = [pl.BlockSpec((tq, rows[k][1]), lambda i, j, cb=rows[k][2]: (i, j)) for k in want]
    out_specs += [_full_spec(p) for p in pars]
    out_shape = [jax.ShapeDtypeStruct((t, wt), dt) for (wt, dt) in drow]
    out_shape += [jax.ShapeDtypeStruct(p.shape, F32) for p in pars]
    return pl.pallas_call(
        body,
        name=name,
        grid=(t // tq, ncol),
        in_specs=[_row_spec(s, tq) for s in rows] + [_full_spec(p) for p in pars] + [_row_spec(s, tq) for s in flat_cots],
        out_specs=out_specs,
        out_shape=out_shape,
        compiler_params=_params(("arbitrary", "arbitrary") if npar else ("parallel", "parallel")),
    )(*[s[0] for s in rows], *pars, *[s[0] for s in flat_cots])


def _whole(a, w=None):
    return (a, a.shape[1] if w is None else w, 0)


def _rms_fn(ql, kvl, gq, gk):
    def one(x, g):
        return x * lax.rsqrt(jnp.mean(x * x, -1, keepdims=True) + 1e-6) * g

    return one(ql, gq), one(kvl, gk)


def _layernorm(s, g, b):
    mu = jnp.mean(s, -1, keepdims=True)
    d = s - mu
    var = jnp.mean(d * d, -1, keepdims=True)
    return d * lax.rsqrt(var + 1e-5) * g + b


def _ln_mix_fn(alpha):
    def fn(h, mix, g, b):
        return (_layernorm(alpha * h + mix, g, b),)

    return fn


def _ln_glu_fn(alpha):
    def fn(h, val, gate, g, b):
        return (_layernorm(alpha * h + val * jax.nn.sigmoid(gate), g, b),)

    return fn


def _ln_ffn_fn(alpha):
    def fn(h, f, pw, gate, g, b):
        return (_layernorm(alpha * h + f + pw * jax.nn.sigmoid(gate), g, b),)

    return fn


def _swiglu_fn(g, u):
    return (jax.nn.silu(g) * u,)


def _gelu_fn(ys, h, d):
    return (jax.nn.gelu(ys + d * h),)


def _merge_fn(o1, o2, o3, l1, l2, l3):
    m = jnp.maximum(jnp.maximum(l1, l2), l3)
    e1, e2, e3 = jnp.exp(l1 - m), jnp.exp(l2 - m), jnp.exp(l3 - m)
    return ((e1 * o1 + e2 * o2 + e3 * o3) / (e1 + e2 + e3),)


def _swap_halves(t):
    w = t.shape[1]
    lane = lax.broadcasted_iota(jnp.int32, t.shape, 1) % LANE
    up = jnp.where(lane < ROPE_DIM, pltpu.roll(t, ROPE_DIM // 2, 1), 0.0)
    return jnp.where(lane < ROPE_DIM // 2, pltpu.roll(t, w - ROPE_DIM // 2, 1), up)


def _rope_fwd_fn(nh):
    def fn(qpe, kpe, cos, sin):
        cq, sq = jnp.tile(cos, (1, nh)), jnp.tile(sin, (1, nh))
        return qpe * cq + _swap_halves(qpe) * sq, kpe * cos + _swap_halves(kpe) * sin

    return fn


def _rope_bwd_fn(nh):
    def fn(dq, dk_heads, cos, sin):
        cq, sq = jnp.tile(cos, (1, nh)), jnp.tile(sin, (1, nh))
        dk = dk_heads[:, :LANE]
        for h in range(1, nh):
            dk = dk + dk_heads[:, h * LANE : (h + 1) * LANE]
        return dq * cq + _swap_halves(dq * sq), dk * cos + _swap_halves(dk * sin)

    return fn


def _rope_tables(positions, name):
    t = positions.shape[0]
    tq = _tile(t, (512, 128, 8))
    half = ROPE_DIM // 2

    def body(p_ref, c_ref, s_ref):
        lane = lax.broadcasted_iota(jnp.int32, (tq, LANE), 1)
        idx = (lane % half).astype(F32)
        inv_freq = jnp.exp(idx * (-math.log(ROPE_THETA) / half))
        ang = p_ref[...].astype(F32) * inv_freq
        live = lane < ROPE_DIM
        c_ref[...] = jnp.where(live, jnp.cos(ang), 0.0)
        s_ref[...] = jnp.where(live, jnp.where(lane < half, -jnp.sin(ang), jnp.sin(ang)), 0.0)

    return pl.pallas_call(
        body,
        name=name,
        grid=(t // tq,),
        in_specs=[pl.BlockSpec((tq, 1), lambda i: (i, 0))],
        out_specs=[pl.BlockSpec((tq, LANE), lambda i: (i, 0))] * 2,
        out_shape=[jax.ShapeDtypeStruct((t, LANE), F32)] * 2,
        compiler_params=_params(("parallel",)),
    )(positions)


def _loss_kernel(y, target, name):
    t, d = y.shape
    tq = _tile(t, (256, 128, 8))

    def body(y_ref, t_ref, dy_ref, l_ref):
        @pl.when(pl.program_id(0) == 0)
        def _():
            l_ref[...] = jnp.zeros_like(l_ref)

        e = y_ref[...] - t_ref[...]
        dy_ref[...] = e * (1.0 / d)
        l_ref[...] += jnp.sum(e * e) * (0.5 / d)

    return pl.pallas_call(
        body,
        name=name,
        grid=(t // tq,),
        in_specs=[pl.BlockSpec((tq, d), lambda i: (i, 0))] * 2,
        out_specs=[pl.BlockSpec((tq, d), lambda i: (i, 0)), pl.BlockSpec((8, LANE), lambda i: (0, 0))],
        out_shape=[jax.ShapeDtypeStruct((t, d), F32), jax.ShapeDtypeStruct((8, LANE), F32)],
        compiler_params=_params(("arbitrary",)),
    )(y, target)


def _addn(arrs, name):
    w = arrs[0].shape[1]
    return _rowwise(lambda *v: (functools.reduce(lambda p, q: p + q, v),), [_whole(a) for a in arrs], [], [(w, w, F32)],
                    name=name, tq=256)[0]


def _mla_tiles(t):
    tq = _tile(t, (512, 256, 128))
    return tq, t // tq


def _call_with_side(body, side, n_in, grid, *, name, in_specs, out_specs, out_shape):
    if side is None:
        return pl.pallas_call(body, name=name, grid=grid, in_specs=in_specs, out_specs=out_specs, out_shape=out_shape,
                              compiler_params=_params(("parallel", "parallel")))
    n_out, n = len(out_specs), len(side[1])

    def wrapped(*refs):
        ins, side_ins = refs[:n_in], refs[n_in : n_in + n]
        outs, side_outs = refs[n_in + n : n_in + n + n_out], refs[n_in + n + n_out : n_in + 2 * n + n_out]
        done = _run_side(side, side_ins, side_outs, refs[n_in + 2 * n + n_out :], grid[0], grid[1])
        body(*ins, *outs)
        done()

    call = pl.pallas_call(wrapped, name=name, grid=grid, in_specs=in_specs + [ANY_SPEC] * n,
                          out_specs=out_specs + [ANY_SPEC] * n, out_shape=out_shape + _side_out_shapes(side),
                          scratch_shapes=_comm_sems(n), compiler_params=_params(("arbitrary", "arbitrary")))

    def run(*args):
        res = call(*args, *side[1])
        return (*res[:n_out], list(res[n_out:]))

    return run


def _mla_fwd(qf, qpe, kvf, kpe, nh, name, side=None):
    t = qf.shape[0]
    tq, nq = _mla_tiles(t)
    scale = (NOPE_DIM + ROPE_DIM) ** -0.5

    def body(qn_ref, qp_ref, kn_ref, kp_ref, v_ref, o_ref, lse_ref):
        i = pl.program_id(1)
        qn = qn_ref[...].astype(BF16)
        qp = qp_ref[...]

        def step(j, carry, masked):
            m, l, acc = carry
            ks = pl.ds(pl.multiple_of(j * tq, tq), tq)
            s = (_dot(qn, kn_ref[ks, :], NT) + _dot(qp, kp_ref[ks, :], NT)) * scale
            if masked:
                row = lax.broadcasted_iota(jnp.int32, (tq, tq), 0)
                col = lax.broadcasted_iota(jnp.int32, (tq, tq), 1)
                s = jnp.where(col <= row, s, NEG)
            m_new = jnp.maximum(m, jnp.max(s, -1, keepdims=True))
            p = jnp.exp(s - m_new)
            a = jnp.exp(m - m_new)
            return m_new, a * l + jnp.sum(p, -1, keepdims=True), a * acc + _dot(p.astype(BF16), v_ref[ks, :], NN)

        init = (jnp.full((tq, 1), NEG, F32), jnp.zeros((tq, 1), F32), jnp.zeros((tq, V_DIM), F32))
        carry = lax.fori_loop(0, i, lambda j, c: step(j, c, False), init)
        m, l, acc = step(i, carry, True)
        o_ref[...] = (acc / l).astype(o_ref.dtype)
        lse_ref[...] = jnp.broadcast_to(m + jnp.log(l), (tq, LANE))

    blk = lambda h, i: (i, h)
    return _call_with_side(
        body, side, 5, (nh, nq),
        name=name,
        in_specs=[
            pl.BlockSpec((tq, LANE), blk),
            pl.BlockSpec((tq, LANE), blk),
            pl.BlockSpec((t, LANE), lambda h, i: (0, h)),
            pl.BlockSpec((t, LANE), lambda h, i: (0, 0)),
            pl.BlockSpec((t, LANE), lambda h, i: (0, nh + h)),
        ],
        out_specs=[pl.BlockSpec((tq, LANE), blk), pl.BlockSpec((tq, LANE), blk)],
        out_shape=[jax.ShapeDtypeStruct((t, nh * LANE), BF16), jax.ShapeDtypeStruct((t, nh * LANE), F32)],
    )(qf, qpe, kvf, kpe, kvf)


def _mla_bwd_q(qf, qpe, kvf, kpe, do, o, lse, nh, name, side=None):
    t = qf.shape[0]
    tq, nq = _mla_tiles(t)
    scale = (NOPE_DIM + ROPE_DIM) ** -0.5

    def body(qn_ref, qp_ref, kn_ref, kp_ref, v_ref, do_ref, o_ref, lse_ref, dqn_ref, dqp_ref, dl_ref):
        i = pl.program_id(1)
        qn = qn_ref[...].astype(BF16)
        qp = qp_ref[...]
        dof = do_ref[...].astype(F32)
        dob = dof.astype(BF16)
        delta = jnp.sum(dof * o_ref[...].astype(F32), -1, keepdims=True)
        lse1 = lse_ref[:, :1]

        def step(j, carry, masked):
            dqn, dqp = carry
            ks = pl.ds(pl.multiple_of(j * tq, tq), tq)
            kn, kp = kn_ref[ks, :], kp_ref[ks, :]
            s = (_dot(qn, kn, NT) + _dot(qp, kp, NT)) * scale
            p = jnp.exp(s - lse1)
            if masked:
                row = lax.broadcasted_iota(jnp.int32, (tq, tq), 0)
                col = lax.broadcasted_iota(jnp.int32, (tq, tq), 1)
                p = jnp.where(col <= row, p, 0.0)
            dp = _dot(dob, v_ref[ks, :], NT)
            ds = (p * (dp - delta) * scale).astype(BF16)
            return dqn + _dot(ds, kn, NN), dqp + _dot(ds, kp, NN)

        init = (jnp.zeros((tq, LANE), F32), jnp.zeros((tq, LANE), F32))
        carry = lax.fori_loop(0, i, lambda j, c: step(j, c, False), init)
        dqn, dqp = step(i, carry, True)
        dqn_ref[...] = dqn
        dqp_ref[...] = dqp
        dl_ref[...] = jnp.broadcast_to(delta, (tq, LANE))

    blk = lambda h, i: (i, h)
    bs = pl.BlockSpec((tq, LANE), blk)
    return _call_with_side(
        body, side, 8, (nh, nq),
        name=name,
        in_specs=[
            bs,
            bs,
            pl.BlockSpec((t, LANE), lambda h, i: (0, h)),
            pl.BlockSpec((t, LANE), lambda h, i: (0, 0)),
            pl.BlockSpec((t, LANE), lambda h, i: (0, nh + h)),
            bs,
            bs,
            bs,
        ],
        out_specs=[bs, bs, bs],
        out_shape=[jax.ShapeDtypeStruct((t, nh * LANE), F32)] * 3,
    )(qf, qpe, kvf, kpe, kvf, do, o, lse)


def _mla_bwd_kv(qn16, qpe, kvf, kpe, do16, lse_row, delta_row, nh, name, side=None):
    t = qn16.shape[0]
    tq, nq = _mla_tiles(t)
    scale = (NOPE_DIM + ROPE_DIM) ** -0.5

    def body(kn_ref, kp_ref, v_ref, qn_ref, qp_ref, do_ref, lse_ref, dl_ref, dkn_ref, dkp_ref, dv_ref):
        j = pl.program_id(1)
        kn, kp, v = kn_ref[...], kp_ref[...], v_ref[...]

        def step(i, carry, masked):
            dkn, dkp, dv = carry
            qs = pl.ds(pl.multiple_of(i * tq, tq), tq)
            qn, qp, dob = qn_ref[qs, :], qp_ref[qs, :], do_ref[qs, :]
            st = (_dot(kn, qn, NT) + _dot(kp, qp, NT)) * scale
            pt = jnp.exp(st - lse_ref[0, :, qs])
            if masked:
                key = lax.broadcasted_iota(jnp.int32, (tq, tq), 0)
                qry = lax.broadcasted_iota(jnp.int32, (tq, tq), 1)
                pt = jnp.where(key <= qry, pt, 0.0)
            dv = dv + _dot(pt.astype(BF16), dob, NN)
            dpt = _dot(v, dob, NT)
            dst = (pt * (dpt - dl_ref[0, :, qs]) * scale).astype(BF16)
            return dkn + _dot(dst, qn, NN), dkp + _dot(dst, qp, NN), dv

        z = jnp.zeros((tq, LANE), F32)
        carry = step(j, (z, z, z), True)
        dkn, dkp, dv = lax.fori_loop(j + 1, nq, lambda i, c: step(i, c, False), carry)
        dkn_ref[...] = dkn
        dkp_ref[...] = dkp
        dv_ref[...] = dv

    blk = pl.BlockSpec((tq, LANE), lambda h, j: (j, h))
    res = lambda f: pl.BlockSpec((t, LANE), f)
    row = pl.BlockSpec((1, 1, t), lambda h, j: (h, 0, 0))
    return _call_with_side(
        body, side, 8, (nh, nq),
        name=name,
        in_specs=[
            blk,
            pl.BlockSpec((tq, LANE), lambda h, j: (j, 0)),
            pl.BlockSpec((tq, LANE), lambda h, j: (j, nh + h)),
            res(lambda h, j: (0, h)),
            res(lambda h, j: (0, h)),
            res(lambda h, j: (0, h)),
            row,
            row,
        ],
        out_specs=[blk, blk, blk],
        out_shape=[jax.ShapeDtypeStruct((t, nh * LANE), F32)] * 3,
    )(kvf, kpe, kvf, qn16, qpe, do16, lse_row, delta_row)


def _alibi_slopes(n):
    return [float(2.0 ** (-8.0 * i / n)) for i in range(1, n + 1)]


def _window_mask(has_prev):
    qi = lax.broadcasted_iota(jnp.int32, (BLK, 2 * BLK), 0)
    ki = lax.broadcasted_iota(jnp.int32, (BLK, 2 * BLK), 1)
    dist = qi + BLK - ki
    valid = (dist >= 0) & (dist <= BLK) & ((ki >= BLK) | has_prev)
    return valid, dist.astype(F32)


def _band_fwd(zd, nh, dil, name):
    t = zd.shape[0]
    nbc = t // BLK // dil
    scale = DIL_HEAD_DIM ** -0.5
    slopes = _alibi_slopes(nh)
    dw = nh * LANE

    def body(q_ref, kc_ref, kp_ref, vc_ref, vp_ref, o_ref, l_ref):
        mask, dist = _window_mask(pl.program_id(1) > 0)
        for h in range(nh):
            sl = slice(h * LANE, (h + 1) * LANE)
            keys = jnp.concatenate([kp_ref[:, sl], kc_ref[:, sl]], axis=0)
            vals = jnp.concatenate([vp_ref[:, sl], vc_ref[:, sl]], axis=0)
            s = jnp.where(mask, _dot(q_ref[:, sl], keys, NT) * scale - (slopes[h] * dil) * dist, NEG)
            m = jnp.max(s, -1, keepdims=True)
            e = jnp.exp(s - m)
            l = jnp.sum(e, -1, keepdims=True)
            o_ref[:, sl] = _dot((e * (1.0 / l)).astype(BF16), vals, NN)
            l_ref[:, sl] = jnp.broadcast_to(m + jnp.log(l), (BLK, LANE))

    prev = lambda i: jnp.maximum(i - 1, 0)
    spec = lambda f: pl.BlockSpec((BLK, dw), f)
    return pl.pallas_call(
        body,
        name=name,
        grid=(dil, nbc),
        in_specs=[
            spec(lambda r, i: (r * nbc + i, 0)),
            spec(lambda r, i: (r * nbc + i, 1)),
            spec(lambda r, i: (r * nbc + prev(i), 1)),
            spec(lambda r, i: (r * nbc + i, 2)),
            spec(lambda r, i: (r * nbc + prev(i), 2)),
        ],
        out_specs=[spec(lambda r, i: (r * nbc + i, 0))] * 2,
        out_shape=[jax.ShapeDtypeStruct((t, dw), F32)] * 2,
        compiler_params=_params(("parallel", "parallel")),
    )(zd, zd, zd, zd, zd)


def _band_bwd(zd, o, lse, do, dl, nh, dil, name):
    t = zd.shape[0]
    nbc = t // BLK // dil
    scale = DIL_HEAD_DIM ** -0.5
    slopes = _alibi_slopes(nh)
    dw = nh * LANE

    def body(q_ref, k_ref, v_ref, kp_ref, vp_ref, qn_ref, o_ref, l_ref, do_ref, dl_ref, on_ref, ln_ref, don_ref, dln_ref,
             dq_ref, dk_ref, dv_ref):
        mask, dist = _window_mask(pl.program_id(1) > 0)
        mask_n, dist_n = _window_mask(pl.program_id(1) + 1 < nbc)
        mask_n, dist_n = mask_n[:, :BLK], dist_n[:, :BLK]
        for h in range(nh):
            sl = slice(h * LANE, (h + 1) * LANE)
            bias = slopes[h] * dil
            q, k, v, qn = q_ref[:, sl], k_ref[:, sl], v_ref[:, sl], qn_ref[:, sl]
            keys = jnp.concatenate([kp_ref[:, sl], k], axis=0)
            vals = jnp.concatenate([vp_ref[:, sl], v], axis=0)
            dof, donf = do_ref[:, sl], don_ref[:, sl]
            dob, donb = dof.astype(BF16), donf.astype(BF16)
            lse1, lsen1 = l_ref[:, sl][:, :1], ln_ref[:, sl][:, :1]
            adj = jnp.sum(dl_ref[:, sl] - dof * o_ref[:, sl], -1, keepdims=True)
            adjn = jnp.sum(dln_ref[:, sl] - donf * on_ref[:, sl], -1, keepdims=True)
            p = jnp.where(mask, jnp.exp(_dot(q, keys, NT) * scale - bias * dist - lse1), 0.0)
            ds = (p * (_dot(dob, vals, NT) + adj)).astype(BF16)
            dq_ref[:, sl] = _dot(ds, keys, NN) * scale
            pn = jnp.where(mask_n, jnp.exp(_dot(qn, k, NT) * scale - bias * dist_n - lsen1), 0.0)
            dsn = (pn * (_dot(donb, v, NT) + adjn)).astype(BF16)
            both_q = jnp.concatenate([q, qn], axis=0)
            dk_ref[:, sl] = _dot(jnp.concatenate([ds[:, BLK:], dsn], axis=0), both_q, TN) * scale
            dv_ref[:, sl] = _dot(jnp.concatenate([p[:, BLK:].astype(BF16), pn.astype(BF16)], axis=0),
                                 jnp.concatenate([dob, donb], axis=0), TN)

    prev = lambda i: jnp.maximum(i - 1, 0)
    nxt = lambda i: jnp.minimum(i + 1, nbc - 1)
    spec = lambda f: pl.BlockSpec((BLK, dw), f)
    cur, nx = spec(lambda r, i: (r * nbc + i, 0)), spec(lambda r, i: (r * nbc + nxt(i), 0))
    return pl.pallas_call(
        body,
        name=name,
        grid=(dil, nbc),
        in_specs=[cur, spec(lambda r, i: (r * nbc + i, 1)), spec(lambda r, i: (r * nbc + i, 2)),
                  spec(lambda r, i: (r * nbc + prev(i), 1)), spec(lambda r, i: (r * nbc + prev(i), 2)),
                  nx, cur, cur, cur, cur, nx, nx, nx, nx],
        out_specs=[cur] * 3,
        out_shape=[jax.ShapeDtypeStruct((t, dw), F32)] * 3,
        compiler_params=_params(("parallel", "parallel")),
    )(zd, zd, zd, zd, zd, zd, o, lse, do, dl, o, lse, do, dl)


def _s5_ops(a_re, a_im, ldt, bt_re, bt_im, c_re, c_im):
    L, g, p = S5_CHUNK, S5_GROUP, S5_STATE
    dt = jnp.exp(ldt)
    lr, li = a_re * dt, a_im * dt
    er = jnp.exp(lr)
    lam_re, lam_im = er * jnp.cos(li), er * jnp.sin(li)
    nr, ni = lam_re - 1.0, lam_im
    den = a_re * a_re + a_im * a_im
    fr, fi = (nr * a_re + ni * a_im) / den, (ni * a_re - nr * a_im) / den
    bb_re, bb_im = fr * bt_re - fi * bt_im, fr * bt_im + fi * bt_re

    def power(tau):
        mag = jnp.exp(tau * lr)
        return mag * jnp.cos(tau * li), mag * jnp.sin(tau * li)

    step = lax.broadcasted_iota(jnp.int32, (L, 1), 0).astype(F32)

    def outer(pr, pi, mr, mi):
        re = pr[:, None, :] * mr[None] - pi[:, None, :] * mi[None]
        im = pr[:, None, :] * mi[None] + pi[:, None, :] * mr[None]
        return re.reshape(L * g, p), im.reshape(L * g, p)

    half = float(L // 2)
    cp_re, cp_im = outer(*power(step - half), c_re, c_im)
    pb_re, pb_im = outer(*power(half - step), bb_re, bb_im)
    toep = _dot(cp_re, pb_re, NT, HI) - _dot(cp_im, pb_im, NT, HI)
    trow = lax.broadcasted_iota(jnp.int32, (L * g, L * g), 0) // g
    scol = lax.broadcasted_iota(jnp.int32, (L * g, L * g), 1) // g
    toep = jnp.where(trow >= scol, toep, 0.0)
    et_re, et_im = outer(*power(float(L - 1) - step), bb_re, bb_im)
    f_re, f_im = outer(*power(step + 1.0), c_re, c_im)
    big_re, big_im = power(jnp.full((1, 1), float(L), F32))
    return toep, et_re, et_im, f_re, f_im, big_re, big_im


def _s5_y(ops, u, sp_re, sp_im):
    toep, _, _, f_re, f_im, _, _ = ops
    return _dot(u, toep, NT) + _dot(sp_re, f_re, NT, HI) - _dot(sp_im, f_im, NT, HI)


def _s5_group_specs(ng):
    vec = pl.BlockSpec((1, 1, S5_STATE), lambda g: (g, 0, 0))
    one = pl.BlockSpec((1, 1, 1), lambda g: (g, 0, 0))
    mat = pl.BlockSpec((1, S5_GROUP, S5_STATE), lambda g: (g, 0, 0))
    return [vec, vec, one, mat, mat, mat, mat]


def _s5_load(refs):
    return [r[0] for r in refs]


def _s5_local(prm, u, name):
    ng, n, w = u.shape
    p = S5_STATE

    def body(*refs):
        ops = _s5_ops(*_s5_load(refs[:7]))
        uu = refs[7][0]
        refs[8][0] = _dot(uu, ops[1], NN, HI)
        refs[9][0] = _dot(uu, ops[2], NN, HI)
        refs[10][0] = ops[5]
        refs[11][0] = ops[6]

    blk = lambda a, b: pl.BlockSpec((1, a, b), lambda g: (g, 0, 0))
    return pl.pallas_call(
        body,
        name=name,
        grid=(ng,),
        in_specs=_s5_group_specs(ng) + [blk(n, w)],
        out_specs=[blk(n, p), blk(n, p), blk(1, p), blk(1, p)],
        out_shape=[jax.ShapeDtypeStruct((ng, n, p), F32)] * 2 + [jax.ShapeDtypeStruct((ng, 1, p), F32)] * 2,
        compiler_params=_params(("parallel",)),
    )(*prm, u)


def _s5_carry(e_re, e_im, lam_re, lam_im, name):
    n, w = e_re.shape
    cw = _tile(w, (1024, 512, 256, 128))

    def body(er_ref, ei_ref, lr_ref, li_ref, sr_ref, si_ref):
        lr, li = lr_ref[...], li_ref[...]

        def step(k, carry):
            sr, si = carry
            row = pl.ds(k, 1)
            sr_ref[row, :] = sr
            si_ref[row, :] = si
            return lr * sr - li * si + er_ref[row, :], li * sr + lr * si + ei_ref[row, :]

        z = jnp.zeros((1, cw), F32)
        lax.fori_loop(0, n, step, (z, z))

    col = pl.BlockSpec((n, cw), lambda j: (0, j))
    one = pl.BlockSpec((1, cw), lambda j: (0, j))
    return pl.pallas_call(
        body,
        name=name,
        grid=(w // cw,),
        in_specs=[col, col, one, one],
        out_specs=[col, col],
        out_shape=[jax.ShapeDtypeStruct((n, w), F32)] * 2,
        compiler_params=_params(("parallel",)),
    )(e_re, e_im, lam_re, lam_im)


def _s5_carry_bwd(dsp_re, dsp_im, sp_re, sp_im, lam_re, lam_im, name):
    n, w = dsp_re.shape
    cw = _tile(w, (1024, 512, 256, 128))

    def body(dr_ref, di_ref, sr_ref, si_ref, lr_ref, li_ref, gr_ref, gi_ref, dlr_ref, dli_ref):
        lr, li = lr_ref[...], li_ref[...]

        def step(q, carry):
            gr_next, gi_next, dr_next, di_next, alr, ali = carry
            k = n - 1 - q
            row = pl.ds(k, 1)
            gr = dr_next + lr * gr_next + li * gi_next
            gi = di_next - li * gr_next + lr * gi_next
            gr_ref[row, :] = gr
            gi_ref[row, :] = gi
            sr, si = sr_ref[row, :], si_ref[row, :]
            return gr, gi, dr_ref[row, :], di_ref[row, :], alr + gr * sr + gi * si, ali + gi * sr - gr * si

        z = jnp.zeros((1, cw), F32)
        out = lax.fori_loop(0, n, step, (z, z, z, z, z, z))
        dlr_ref[...] = out[4]
        dli_ref[...] = out[5]

    col = pl.BlockSpec((n, cw), lambda j: (0, j))
    one = pl.BlockSpec((1, cw), lambda j: (0, j))
    return pl.pallas_call(
        body,
        name=name,
        grid=(w // cw,),
        in_specs=[col, col, col, col, one, one],
        out_specs=[col, col, one, one],
        out_shape=[jax.ShapeDtypeStruct((n, w), F32)] * 2 + [jax.ShapeDtypeStruct((1, w), F32)] * 2,
        compiler_params=_params(("parallel",)),
    )(dsp_re, dsp_im, sp_re, sp_im, lam_re, lam_im)


def _s5_out(prm, u, sp_re, sp_im, name):
    ng, n, w = u.shape
    p = S5_STATE

    def body(*refs):
        ops = _s5_ops(*_s5_load(refs[:7]))
        refs[10][0] = _s5_y(ops, refs[7][0], refs[8][0], refs[9][0])

    blk = lambda a, b: pl.BlockSpec((1, a, b), lambda g: (g, 0, 0))
    return pl.pallas_call(
        body,
        name=name,
        grid=(ng,),
        in_specs=_s5_group_specs(ng) + [blk(n, w), blk(n, p), blk(n, p)],
        out_specs=blk(n, w),
        out_shape=jax.ShapeDtypeStruct((ng, n, w), F32),
        compiler_params=_params(("parallel",)),
    )(*prm, u, sp_re, sp_im)


def _s5_bwd_state(prm, dy, name):
    ng, n, w = dy.shape
    p = S5_STATE

    def body(*refs):
        ops = _s5_ops(*_s5_load(refs[:7]))
        d = refs[7][0]
        refs[8][0] = _dot(d, ops[3], NN, HI)
        refs[9][0] = -_dot(d, ops[4], NN, HI)

    blk = lambda a, b: pl.BlockSpec((1, a, b), lambda g: (g, 0, 0))
    return pl.pallas_call(
        body,
        name=name,
        grid=(ng,),
        in_specs=_s5_group_specs(ng) + [blk(n, w)],
        out_specs=[blk(n, p), blk(n, p)],
        out_shape=[jax.ShapeDtypeStruct((ng, n, p), F32)] * 2,
        compiler_params=_params(("parallel",)),
    )(*prm, dy)


def _s5_bwd_main(prm, u, sp_re, sp_im, dy, g_re, g_im, dlam_re, dlam_im, name):
    ng, n, w = u.shape
    p = S5_STATE

    def body(*refs):
        prm_v = _s5_load(refs[:7])
        uu, sr, si, d, gr, gi, dlr, dli = [r[0] for r in refs[7:15]]
        ops, ops_vjp = jax.vjp(_s5_ops, *prm_v)
        cots = (_dot(d, uu, TN), _dot(uu, gr, TN, HI), _dot(uu, gi, TN, HI), _dot(d, sr, TN, HI), -_dot(d, si, TN, HI),
                dlr, dli)
        grads = ops_vjp(cots)
        for q in range(7):
            refs[15 + q][0] = grads[q]
        refs[22][0] = _dot(d, ops[0], NN) + _dot(gr, ops[1], NT, HI) + _dot(gi, ops[2], NT, HI)

    blk = lambda a, b: pl.BlockSpec((1, a, b), lambda g: (g, 0, 0))
    prm_specs = _s5_group_specs(ng)
    return pl.pallas_call(
        body,
        name=name,
        grid=(ng,),
        in_specs=prm_specs + [blk(n, w), blk(n, p), blk(n, p), blk(n, w), blk(n, p), blk(n, p), blk(1, p), blk(1, p)],
        out_specs=prm_specs + [blk(n, w)],
        out_shape=[jax.ShapeDtypeStruct(a.shape, F32) for a in prm] + [jax.ShapeDtypeStruct((ng, n, w), F32)],
        compiler_params=_params(("parallel",)),
    )(*prm, u, sp_re, sp_im, dy, g_re, g_im, dlam_re, dlam_im)


SLAB_GROUPS = LANE // S5_GROUP
CHUNK_TILES = S5_CHUNK * S5_GROUP // LANE


def _segment_transpose(tiles):
    seg = lax.broadcasted_iota(jnp.int32, (1, LANE), 1) // S5_GROUP
    tiles = list(tiles)
    stride = 1
    while stride < SLAB_GROUPS:
        upper = (seg & stride) != 0
        shift = S5_GROUP * stride
        new = list(tiles)
        for i in range(SLAB_GROUPS):
            if i & stride:
                continue
            j = i + stride
            new[i] = jnp.where(upper, pltpu.roll(tiles[j], shift, 1), tiles[i])
            new[j] = jnp.where(upper, tiles[j], pltpu.roll(tiles[i], LANE - shift, 1))
        tiles = new
        stride *= 2
    return tiles


def _to_groups(h, name):
    t, d = h.shape
    n = t // S5_CHUNK

    def body(x_ref, o_ref):
        for a in range(CHUNK_TILES):
            rows = [x_ref[pl.ds(SLAB_GROUPS * a + b, n, stride=S5_CHUNK), :] for b in range(SLAB_GROUPS)]
            for g, tile in enumerate(_segment_transpose(rows)):
                o_ref[g, :, a * LANE : (a + 1) * LANE] = tile

    return pl.pallas_call(
        body,
        name=name,
        grid=(d // LANE,),
        in_specs=[pl.BlockSpec((t, LANE), lambda q: (0, q))],
        out_specs=pl.BlockSpec((SLAB_GROUPS, n, S5_CHUNK * S5_GROUP), lambda q: (q, 0, 0)),
        out_shape=jax.ShapeDtypeStruct((d // S5_GROUP, n, S5_CHUNK * S5_GROUP), F32),
        compiler_params=_params(("parallel",)),
    )(h)


def _from_groups(y, name):
    ng, n, w = y.shape
    t, d = n * S5_CHUNK, ng * S5_GROUP

    def body(y_ref, x_ref):
        for a in range(CHUNK_TILES):
            tiles = [y_ref[g, :, a * LANE : (a + 1) * LANE] for g in range(SLAB_GROUPS)]
            for b, row in enumerate(_segment_transpose(tiles)):
                x_ref[pl.ds(SLAB_GROUPS * a + b, n, stride=S5_CHUNK), :] = row

    return pl.pallas_call(
        body,
        name=name,
        grid=(d // LANE,),
        in_specs=[pl.BlockSpec((SLAB_GROUPS, n, w), lambda q: (q, 0, 0))],
        out_specs=pl.BlockSpec((t, LANE), lambda q: (0, q)),
        out_shape=jax.ShapeDtypeStruct((t, d), F32),
        compiler_params=_params(("parallel",)),
    )(y)


def _states_to_cols(e):
    ng, n, p = e.shape
    return e.transpose(1, 0, 2).reshape(n, ng * p)


def _cols_to_states(s, ng):
    n = s.shape[0]
    return s.reshape(n, ng, S5_STATE).transpose(1, 0, 2)


def _me():
    return lax.axis_index("x"), lax.axis_index("y"), lax.axis_index("c")


def _flip(v, bit):
    return 1 - v if bit else v


COPIES = N_DEV - 1
ANY_SPEC = pl.BlockSpec(memory_space=pl.ANY)


def _comm_sems(n):
    return [pltpu.SemaphoreType.DMA((COPIES * n,)), pltpu.SemaphoreType.DMA((COPIES * n,)), pltpu.SemaphoreType.DMA((n,))]


def _gather_phases(x_refs, out_refs, send_sems, recv_sems, local_sems):
    mx, my, mc = _me()
    me, sibling = (mx, my, mc), (mx, my, 1 - mc)
    chips = [(1 - mx, my), (mx, 1 - my), (1 - mx, 1 - my)]
    arrays = range(len(x_refs))

    def slot(a, px, py, pc):
        return out_refs[a].at[4 * px + 2 * py + pc]

    def copy(a, k, block, to, from_input=False):
        return pltpu.make_async_remote_copy(
            src_ref=x_refs[a] if from_input else slot(a, *block),
            dst_ref=slot(a, *block),
            send_sem=send_sems.at[COPIES * a + k],
            recv_sem=recv_sems.at[COPIES * a + k],
            device_id=to,
            device_id_type=pl.DeviceIdType.MESH,
        )

    mine = [pltpu.make_async_copy(x_refs[a], slot(a, *me), local_sems.at[a]) for a in arrays]
    first = [copy(a, 0, me, sibling, True) for a in arrays]
    first += [copy(a, 1 + j, me, (*chip, mc), True) for a in arrays for j, chip in enumerate(chips)]
    passed = {(a, j): copy(a, 4 + j, (*chip, mc), sibling) for a in arrays for j, chip in enumerate(chips)}

    def start():
        for cp in mine + first:
            cp.start()

    def relay():
        for j, chip in enumerate(chips):
            for a in arrays:
                copy(a, 1 + j, (*chip, mc), me).wait_recv()
                passed[(a, j)].start()

    def finish():
        for a in arrays:
            copy(a, 0, sibling, me).wait_recv()
            for j, chip in enumerate(chips):
                copy(a, 4 + j, (*chip, 1 - mc), me).wait_recv()
        for cp in first + list(passed.values()):
            cp.wait_send()
        for cp in mine:
            cp.wait()

    return start, relay, finish


def _exchange_phases(g_refs, out_refs, send_sems, recv_sems, local_sems):
    mx, my, mc = _me()
    mine_idx = 4 * mx + 2 * my + mc
    arrays = range(len(g_refs))
    own = [pltpu.make_async_copy(g_refs[a].at[mine_idx], out_refs[a].at[mine_idx], local_sems.at[a]) for a in arrays]
    copies = []
    for m in range(1, N_DEV):
        px, py, pc = _flip(mx, m & 4), _flip(my, m & 2), _flip(mc, m & 1)
        for a in arrays:
            copies.append(
                pltpu.make_async_remote_copy(
                    src_ref=g_refs[a].at[4 * px + 2 * py + pc],
                    dst_ref=out_refs[a].at[mine_idx],
                    send_sem=send_sems.at[COPIES * a + m - 1],
                    recv_sem=recv_sems.at[COPIES * a + m - 1],
                    device_id=(px, py, pc),
                    device_id_type=pl.DeviceIdType.MESH,
                )
            )

    def start():
        for cp in own + copies:
            cp.start()

    def finish():
        for cp in copies:
            cp.wait_recv()
        for cp in copies:
            cp.wait_send()
        for cp in own:
            cp.wait()

    return start, None, finish


def _side_out_shapes(side):
    kind, arrs = side
    return [jax.ShapeDtypeStruct((N_DEV,) + a.shape if kind == "gather" else a.shape, a.dtype) for a in arrs]


def _run_side(side, x_refs, out_refs, sems, nsteps_major, nsteps_minor):
    start, relay, finish = (_gather_phases if side[0] == "gather" else _exchange_phases)(x_refs, out_refs, *sems)
    a, b = pl.program_id(0), pl.program_id(1)
    pl.when(jnp.logical_and(a == 0, b == 0))(start)
    if relay is not None:
        pl.when(jnp.logical_and(a == nsteps_major - 1, b == 0))(relay)
    return lambda: pl.when(jnp.logical_and(a == nsteps_major - 1, b == nsteps_minor - 1))(finish)


def _exchange_call(kind, arrs, name):
    n = len(arrs)

    def body(*refs):
        start, relay, finish = (_gather_phases if kind == "gather" else _exchange_phases)(refs[:n], refs[n : 2 * n], *refs[2 * n :])
        start()
        if relay is not None:
            relay()
        finish()

    return pl.pallas_call(body, name=name, in_specs=[ANY_SPEC] * n, out_specs=[ANY_SPEC] * n,
                          out_shape=_side_out_shapes((kind, arrs)), scratch_shapes=_comm_sems(n))(*arrs)


def _sum_slots(recv, name):
    _, r, c = recv.shape
    tr = _tile(r, (256, 128, 64, 32, 16, 8))

    def body(r_ref, o_ref):
        acc = r_ref[0].astype(F32)
        for k in range(1, N_DEV):
            acc = acc + r_ref[k].astype(F32)
        o_ref[...] = acc

    return pl.pallas_call(
        body,
        name=name,
        grid=(r // tr,),
        in_specs=[pl.BlockSpec((N_DEV, tr, c), lambda i: (0, i, 0))],
        out_specs=pl.BlockSpec((tr, c), lambda i: (i, 0)),
        out_shape=jax.ShapeDtypeStruct((r, c), F32),
        compiler_params=_params(("parallel",)),
    )(recv)


def _adamw(w, g, m, v, name):
    shape = w.shape
    c = shape[-1]
    as2d = lambda a: a.reshape(-1, c)
    w2, g2, m2, v2 = as2d(w), as2d(g), as2d(m), as2d(v)
    r = w2.shape[0]
    tr = _tile(r, (256, 128, 64, 32, 16, 8))
    c1 = 1.0 / (1.0 - ADAM_B1 ** ADAM_STEP)
    c2 = 1.0 / (1.0 - ADAM_B2 ** ADAM_STEP)

    def body(w_ref, g_ref, m_ref, v_ref, d_ref, mo_ref, vo_ref):
        gg = g_ref[...]
        mn = ADAM_B1 * m_ref[...] + (1.0 - ADAM_B1) * gg
        vn = ADAM_B2 * v_ref[...] + (1.0 - ADAM_B2) * (gg * gg)
        d_ref[...] = -ADAM_LR * ((mn * c1) / (jnp.sqrt(vn * c2) + ADAM_EPS) + ADAM_WD * w_ref[...])
        mo_ref[...] = mn
        vo_ref[...] = vn

    spec = pl.BlockSpec((tr, c), lambda i: (i, 0))
    d, mn, vn = pl.pallas_call(
        body,
        name=name,
        grid=(r // tr,),
        in_specs=[spec] * 4,
        out_specs=[spec] * 3,
        out_shape=[jax.ShapeDtypeStruct((r, c), F32)] * 3,
        compiler_params=_params(("parallel",)),
    )(w2, g2, m2, v2)
    return d.reshape(shape), mn.reshape(shape), vn.reshape(shape)


COL_SHARDED = ("attn_w_in", "mla_w_q_b", "mla_w_kv_b", "s5_d", "s5_w_glu", "ffn_w_in", "ple_w")
ROW_SHARDED = ("attn_w_out", "ffn_w_out", "ple_gate_w")
REPLICATED = ("mla_q_norm", "mla_kv_norm", "s5_a_re", "s5_a_im", "s5_log_dt", "s5_b_re", "s5_b_im", "s5_c_re",
              "s5_c_im", "ln1_g", "ln1_b", "ln2_g", "ln2_b")
ATTENTION_WEIGHTS = ("attn_w_in", "mla_w_q_b", "mla_w_kv_b")
BIG_WEIGHTS = ("attn_w_in", "mla_w_q_b", "mla_w_kv_b", "attn_w_out", "s5_w_glu", "ffn_w_in", "ffn_w_out", "ple_w", "ple_gate_w")
WEIGHTS = ("attn_w_in", "mla_q_norm", "mla_w_q_b", "mla_kv_norm", "mla_w_kv_b", "attn_w_out", "s5_a_re", "s5_a_im",
           "s5_log_dt", "s5_b_re", "s5_b_im", "s5_c_re", "s5_c_im", "s5_d", "s5_w_glu", "ln1_g", "ln1_b", "ffn_w_in",
           "ffn_w_out", "ple_w", "ple_gate_w", "ln2_g", "ln2_b")


def _unshard(name, gathered):
    if name in COL_SHARDED:
        full = jnp.moveaxis(gathered, 0, -2)
        return full.reshape(full.shape[:-2] + (full.shape[-2] * full.shape[-1],))
    full = jnp.moveaxis(gathered, 0, 1)
    return full.reshape((full.shape[0], full.shape[1] * full.shape[2]) + full.shape[3:])


def _by_owner(name, grad):
    if name in COL_SHARDED:
        g = grad.reshape(grad.shape[:-1] + (N_DEV, grad.shape[-1] // N_DEV))
        return jnp.moveaxis(g, -2, 0).reshape(N_DEV, -1)
    if name in ROW_SHARDED:
        g = grad.reshape((grad.shape[0], N_DEV, grad.shape[1] // N_DEV) + grad.shape[2:])
        return jnp.moveaxis(g, 1, 0).reshape(N_DEV, -1)
    return jnp.broadcast_to(grad.reshape(1, -1), (N_DEV, grad.size))


def _owner_rows(name, g):
    k, n = g.shape
    if name in COL_SHARDED:
        return g.reshape(k, N_DEV, n // N_DEV).transpose(1, 0, 2)
    assert name in ROW_SHARDED, name
    return g.reshape(N_DEV, k // N_DEV, n)


def _pack(pieces, axis):
    blocks, where, row = [], [], 0
    for p in pieces:
        n = p.shape[axis]
        pad = (-n) % (PACK_COLS * PACK_ROW_TILE)
        if pad:
            shape = list(p.shape)
            shape[axis] = pad
            p = jnp.concatenate([p, jnp.zeros(shape, p.dtype)], axis=axis)
        rows = (n + pad) // PACK_COLS
        blocks.append(p.reshape(p.shape[:axis] + (rows, PACK_COLS)))
        where.append((row, rows, n))
        row += rows
    return jnp.concatenate(blocks, axis=axis), where


def _unpack(buf, axis, where):
    row, rows, n = where
    part = lax.slice_in_dim(buf, row, row + rows, axis=axis)
    return lax.slice_in_dim(part.reshape(part.shape[:axis] + (rows * PACK_COLS,)), 0, n, axis=axis)


def _twice(fn):
    return lambda *a: fn(*a) * 2


def _ffn_block(h, hb, p_i, w_in, w_out, w_ple, w_pg, g, b, alpha, tag):
    t, d = h.shape
    hid = w_out.shape[0]
    cw = _tile(hid, (1408, 512, 256, 128))
    ncb = hid // cw
    gu = _mm(hb, w_in, out_dtype=BF16, name=f"ffn_in_{tag}")
    act = _rowwise(_swiglu_fn, [(gu, cw, 0), (gu, cw, ncb)], [], [(hid, cw, BF16)], name=f"swiglu_{tag}", tq=1024, ncol=ncb)[0]
    f = _mm(act, w_out, name=f"ffn_out_{tag}")
    pw = _mm(p_i, w_ple, name=f"ple_{tag}")
    gate = _mm(hb, w_pg, name=f"ple_gate_{tag}")
    out, outb = _rowwise(_twice(_ln_ffn_fn(alpha)), [_whole(h), _whole(f), _whole(pw), _whole(gate)], [g, b],
                         [(d, d, F32), (d, d, BF16)], name=f"ln2_{tag}", tq=256)
    return out, outb, (h, hb, p_i, gu, act, f, pw, gate)


def _ffn_block_bwd(saved, dout, w_in, w_out, w_pg, g, b, alpha, tag):
    h, hb, p_i, gu, act, f, pw, gate = saved
    t, d = h.shape
    hid = w_out.shape[0]
    cw = _tile(hid, (1408, 512, 256, 128))
    ncb = hid // cw
    dh_a, df, dpw, dgate, dg, db = _rowwise_bwd(
        _ln_ffn_fn(alpha), [_whole(h), _whole(f), _whole(pw), _whole(gate)], [g, b], [dout],
        need=[True] * 4, drow=[(d, F32), (d, BF16), (d, BF16), (d, BF16)], name=f"ln2_bwd_{tag}", tq=128)
    dw_out = _mm(act, df, ta=True, out_dtype=BF16, name=f"ffn_out_dw_{tag}")
    dact = _mm(df, w_out, tb=True, out_dtype=BF16, name=f"ffn_out_dx_{tag}")
    dg_, du_ = _rowwise_bwd(_swiglu_fn, [(gu, cw, 0), (gu, cw, ncb)], [], [[(dact, cw, 0)]], need=[True, True],
                            drow=[(hid, BF16), (hid, BF16)], name=f"swiglu_bwd_{tag}", tq=512, ncol=ncb)
    dgu = jnp.concatenate([dg_, du_], axis=1)
    dw_in = _mm(hb, dgu, ta=True, out_dtype=BF16, name=f"ffn_in_dw_{tag}")
    dh_b = _mm(dgu, w_in, tb=True, name=f"ffn_in_dx_{tag}")
    dw_ple = _mm(p_i, dpw, ta=True, out_dtype=BF16, name=f"ple_dw_{tag}")
    dw_pg = _mm(hb, dgate, ta=True, out_dtype=BF16, name=f"ple_gate_dw_{tag}")
    dh_c = _mm(dgate, w_pg, tb=True, name=f"ple_gate_dx_{tag}")
    return [dh_a, dh_b, dh_c], dict(ffn_w_in=dw_in, ffn_w_out=dw_out, ple_w=dw_ple, ple_gate_w=dw_pg, ln2_g=dg, ln2_b=db)


def kernel(x, p, positions, attn_w_in, mla_q_norm, mla_w_q_b, mla_kv_norm, mla_w_kv_b, attn_w_out, s5_a_re, s5_a_im, s5_log_dt, s5_b_re, s5_b_im, s5_c_re, s5_c_im, s5_d, s5_w_glu, ln1_g, ln1_b, ffn_w_in, ffn_w_out, ple_w, ple_gate_w, ln2_g, ln2_b, loss_target, m_attn_w_in, m_mla_q_norm, m_mla_w_q_b, m_mla_kv_norm, m_mla_w_kv_b, m_attn_w_out, m_s5_a_re, m_s5_a_im, m_s5_log_dt, m_s5_b_re, m_s5_b_im, m_s5_c_re, m_s5_c_im, m_s5_d, m_s5_w_glu, m_ln1_g, m_ln1_b, m_ffn_w_in, m_ffn_w_out, m_ple_w, m_ple_gate_w, m_ln2_g, m_ln2_b, v_attn_w_in, v_mla_q_norm, v_mla_w_q_b, v_mla_kv_norm, v_mla_w_kv_b, v_attn_w_out, v_s5_a_re, v_s5_a_im, v_s5_log_dt, v_s5_b_re, v_s5_b_im, v_s5_c_re, v_s5_c_im, v_s5_d, v_s5_w_glu, v_ln1_g, v_ln1_b, v_ffn_w_in, v_ffn_w_out, v_ple_w, v_ple_gate_w, v_ln2_g, v_ln2_b):
    local = dict(attn_w_in=attn_w_in, mla_q_norm=mla_q_norm, mla_w_q_b=mla_w_q_b, mla_kv_norm=mla_kv_norm,
                 mla_w_kv_b=mla_w_kv_b, attn_w_out=attn_w_out, s5_a_re=s5_a_re, s5_a_im=s5_a_im, s5_log_dt=s5_log_dt,
                 s5_b_re=s5_b_re, s5_b_im=s5_b_im, s5_c_re=s5_c_re, s5_c_im=s5_c_im, s5_d=s5_d, s5_w_glu=s5_w_glu,
                 ln1_g=ln1_g, ln1_b=ln1_b, ffn_w_in=ffn_w_in, ffn_w_out=ffn_w_out, ple_w=ple_w, ple_gate_w=ple_gate_w,
                 ln2_g=ln2_g, ln2_b=ln2_b)
    mom_m = dict(zip(WEIGHTS, (m_attn_w_in, m_mla_q_norm, m_mla_w_q_b, m_mla_kv_norm, m_mla_w_kv_b, m_attn_w_out, m_s5_a_re, m_s5_a_im, m_s5_log_dt, m_s5_b_re, m_s5_b_im, m_s5_c_re, m_s5_c_im, m_s5_d, m_s5_w_glu, m_ln1_g, m_ln1_b, m_ffn_w_in, m_ffn_w_out, m_ple_w, m_ple_gate_w, m_ln2_g, m_ln2_b)))
    mom_v = dict(zip(WEIGHTS, (v_attn_w_in, v_mla_q_norm, v_mla_w_q_b, v_mla_kv_norm, v_mla_w_kv_b, v_attn_w_out, v_s5_a_re, v_s5_a_im, v_s5_log_dt, v_s5_b_re, v_s5_b_im, v_s5_c_re, v_s5_c_im, v_s5_d, v_s5_w_glu, v_ln1_g, v_ln1_b, v_ffn_w_in, v_ffn_w_out, v_ple_w, v_ple_gate_w, v_ln2_g, v_ln2_b)))

    t, d = x.shape[1], x.shape[2]
    depth = ln1_g.shape[0]
    alpha = (2.0 * depth) ** 0.25
    ql, kvl = mla_q_norm.shape[-1], mla_kv_norm.shape[-1]
    nh = mla_w_q_b.shape[-1] * N_DEV // (NOPE_DIM + ROPE_DIM)
    dw = (attn_w_in.shape[-1] * N_DEV - ql - kvl - ROPE_DIM) // 3
    ndh = dw // DIL_HEAD_DIM
    ng = d // S5_GROUP
    assert ql == kvl and (ql + kvl) % LANE == 0 and nh * V_DIM == dw

    sharded = [n for n in WEIGHTS if n in COL_SHARDED or n in ROW_SHARDED]
    def as_sent(n):
        if n == "s5_d":
            return local[n]
        if n == "attn_w_in":
            return local[n][0].T.astype(BF16)
        return local[n].astype(BF16).reshape(-1, local[n].shape[-1])

    full = {}

    def take_weights(names, gathered):
        for n, got in zip(names, gathered):
            if n == "attn_w_in":
                full[n] = got.reshape(N_DEV * local[n].shape[2], d)
            else:
                full[n] = _unshard(n, got.reshape((N_DEV,) + local[n].shape))

    first_w = [n for n in sharded if n in ATTENTION_WEIGHTS]
    later_w = [n for n in sharded if n not in ATTENTION_WEIGHTS]
    take_weights(first_w, _exchange_call("gather", [as_sent(n) for n in first_w], "gather_attn_weights"))

    w_in_t = full["attn_w_in"]
    lat = ql + kvl
    w_lat_t = jnp.concatenate([w_in_t[: lat + ROPE_DIM], jnp.zeros((LANE - ROPE_DIM, d), BF16)], axis=0)
    w_dil_t = w_in_t[lat + ROPE_DIM :]
    wq = full["mla_w_q_b"][0].reshape(ql, nh, NOPE_DIM + ROPE_DIM)
    wq_pe = jnp.pad(wq[:, :, NOPE_DIM:], ((0, 0), (0, 0), (0, LANE - ROPE_DIM)))
    wqp = jnp.concatenate([wq[:, :, :NOPE_DIM].reshape(ql, nh * LANE), wq_pe.reshape(ql, nh * LANE)], axis=1)
    wkv = full["mla_w_kv_b"][0].reshape(kvl, nh, NOPE_DIM + V_DIM)
    wkvp = jnp.concatenate([wkv[:, :, :NOPE_DIM].reshape(kvl, nh * LANE), wkv[:, :, NOPE_DIM:].reshape(kvl, nh * LANE)], axis=1)

    h0 = x[0]
    h0b = h0.astype(BF16)
    pb = p.astype(BF16)
    target = loss_target[0]
    pos = positions.reshape(t, 1)
    grads = {}

    z = _mm(h0b, w_lat_t, tb=True, name="attn_in_lat")
    zd = _mm(h0b, w_dil_t, tb=True, out_dtype=BF16, name="attn_in_dil")
    gq, gk = mla_q_norm.reshape(1, ql), mla_kv_norm.reshape(1, kvl)
    qn, kvn = _rowwise(_rms_fn, [(z, ql, 0), (z, kvl, 1)], [gq, gk], [(ql, ql, BF16), (kvl, kvl, BF16)], name="rms", tq=256)
    qf = _mm(qn, wqp, name="q_up")
    kvf = _mm(kvn, wkvp, out_dtype=BF16, name="kv_up")
    cos, sin = _rope_tables(pos, "rope_tables")
    pe_cb = lat // LANE
    qpe, kpe = _rowwise(_rope_fwd_fn(nh), [(qf, nh * LANE, 1), (z, LANE, pe_cb), _whole(cos), _whole(sin)], [],
                        [(nh * LANE, nh * LANE, BF16), (LANE, LANE, BF16)], name="rope", tq=256)
    out_a, lse_a, later_gathered = _mla_fwd(qf, qpe, kvf, kpe, nh, "mla_fwd", side=("gather", [as_sent(n) for n in later_w]))
    take_weights(later_w, later_gathered)
    w_out = full["attn_w_out"][0]
    w_glu = full["s5_w_glu"][0]
    d_skip = full["s5_d"]

    def to_classes(a, dil):
        if dil == 1:
            return a
        return a.reshape(t // dil, dil, a.shape[1]).transpose(1, 0, 2).reshape(t, a.shape[1])

    def from_classes(a, dil):
        if dil == 1:
            return a
        return a.reshape(dil, t // dil, a.shape[1]).transpose(1, 0, 2).reshape(t, a.shape[1])

    band = []
    for window, dil in DIL_BRANCHES:
        assert window // dil == BLK
        zc = to_classes(zd, dil)
        o_c, l_c = _band_fwd(zc, ndh, dil, f"band_fwd_d{dil}")
        band.append((dil, zc, o_c, l_c, from_classes(o_c, dil), from_classes(l_c, dil)))
    merge_rows = [_whole(b[4]) for b in band] + [_whole(b[5]) for b in band]
    out_b = _rowwise(_merge_fn, merge_rows, [], [(dw, dw, BF16)], name="merge", tq=512)[0]
    att = jnp.concatenate([out_a, out_b], axis=1)
    mix0 = _mm(att, w_out, name="attn_out")
    g1, b1 = ln1_g[0:1], ln1_b[0:1]
    h1, h1b = _rowwise(_twice(_ln_mix_fn(alpha)), [_whole(h0), _whole(mix0)], [g1, b1], [(d, d, F32), (d, d, BF16)],
                       name="ln1_l0", tq=256)
    h2, _, saved_f0 = _ffn_block(h1, h1b, pb[0, 0], full["ffn_w_in"][0], full["ffn_w_out"][0], full["ple_w"][0],
                                 full["ple_gate_w"][0], ln2_g[0:1], ln2_b[0:1], alpha, "l0")

    prm = [s5_a_re[0].reshape(ng, 1, S5_STATE), s5_a_im[0].reshape(ng, 1, S5_STATE), s5_log_dt[0].reshape(ng, 1, 1),
           s5_b_re[0].transpose(0, 2, 1), s5_b_im[0].transpose(0, 2, 1), s5_c_re[0], s5_c_im[0]]
    u = _to_groups(h2, "s5_regroup_u")
    e_re, e_im, lam_re, lam_im = _s5_local(prm, u, "s5_local")
    lam_re_c, lam_im_c = lam_re.reshape(1, ng * S5_STATE), lam_im.reshape(1, ng * S5_STATE)
    sp_re_c, sp_im_c = _s5_carry(_states_to_cols(e_re), _states_to_cols(e_im), lam_re_c, lam_im_c, "s5_carry")
    sp_re, sp_im = _cols_to_states(sp_re_c, ng), _cols_to_states(sp_im_c, ng)
    ys = _from_groups(_s5_out(prm, u, sp_re, sp_im, "s5_out"), "s5_ungroup_y")
    z5 = _rowwise(_gelu_fn, [_whole(ys), _whole(h2)], [d_skip], [(d, d, BF16)], name="gelu", tq=512)[0]
    vg = _mm(z5, w_glu, name="glu_in")
    g3, b3 = ln1_g[1:2], ln1_b[1:2]
    h3, h3b = _rowwise(_twice(_ln_glu_fn(alpha)), [_whole(h2), (vg, d, 0), (vg, d, 1)], [g3, b3],
                       [(d, d, F32), (d, d, BF16)], name="ln1_l1", tq=256)
    h4, _, saved_f1 = _ffn_block(h3, h3b, pb[1, 0], full["ffn_w_in"][1], full["ffn_w_out"][1], full["ple_w"][1],
                                 full["ple_gate_w"][1], ln2_g[1:2], ln2_b[1:2], alpha, "l1")

    dh4, loss_acc = _loss_kernel(h4, target, "loss")
    loss = lax.psum(loss_acc[0, 0], AXES)

    dh3, gf1 = _ffn_block_bwd(saved_f1, [_whole(dh4)], full["ffn_w_in"][1], full["ffn_w_out"][1],
                              full["ple_gate_w"][1], ln2_g[1:2], ln2_b[1:2], alpha, "l1")
    dh2_a, dval, dgate, dg3, db3 = _rowwise_bwd(
        _ln_glu_fn(alpha), [_whole(h2), (vg, d, 0), (vg, d, 1)], [g3, b3], [[_whole(a) for a in dh3]],
        need=[True] * 3, drow=[(d, F32), (d, BF16), (d, BF16)], name="ln1_bwd_l1", tq=128)
    dvg = jnp.concatenate([dval, dgate], axis=1)
    dw_glu = _mm(z5, dvg, ta=True, out_dtype=BF16, name="glu_dw")
    dz5 = _mm(dvg, w_glu, tb=True, name="glu_dx")
    dys, dh2_b, dd = _rowwise_bwd(_gelu_fn, [_whole(ys), _whole(h2)], [d_skip], [[_whole(dz5)]], need=[True, True],
                                  drow=[(d, F32), (d, F32)], name="gelu_bwd", tq=256)
    grads["s5_d"] = dd
    dy = _to_groups(dys, "s5_regroup_dy")
    dsp_re, dsp_im = _s5_bwd_state(prm, dy, "s5_bwd_state")
    g_re_c, g_im_c, dlam_re_c, dlam_im_c = _s5_carry_bwd(_states_to_cols(dsp_re), _states_to_cols(dsp_im), sp_re_c, sp_im_c,
                                                         lam_re_c, lam_im_c, "s5_carry_bwd")
    s5g = _s5_bwd_main(prm, u, sp_re, sp_im, dy, _cols_to_states(g_re_c, ng), _cols_to_states(g_im_c, ng),
                       dlam_re_c.reshape(ng, 1, S5_STATE), dlam_im_c.reshape(ng, 1, S5_STATE), "s5_bwd_main")
    grads["s5_a_re"] = s5g[0].reshape(s5_a_re.shape)
    grads["s5_a_im"] = s5g[1].reshape(s5_a_im.shape)
    grads["s5_log_dt"] = s5g[2].reshape(s5_log_dt.shape)
    grads["s5_b_re"] = s5g[3].transpose(0, 2, 1)[None]
    grads["s5_b_im"] = s5g[4].transpose(0, 2, 1)[None]
    grads["s5_c_re"] = s5g[5][None]
    grads["s5_c_im"] = s5g[6][None]
    dh2_c = _from_groups(s5g[7], "s5_ungroup_du")

    dh1, gf0 = _ffn_block_bwd(saved_f0, [_whole(dh2_a), _whole(dh2_b), _whole(dh2_c)], full["ffn_w_in"][0],
                              full["ffn_w_out"][0], full["ple_gate_w"][0], ln2_g[0:1], ln2_b[0:1], alpha, "l0")
    for k in ("ln2_g", "ln2_b"):
        grads[k] = jnp.concatenate([gf0[k], gf1[k]])

    shards = {}

    def take(items, received):
        for (key, _), recv in zip(items, received):
            shards[key] = _sum_slots(recv, f"sum_grads_{key[0]}_{key[1]}")

    ffn_names = ("ffn_w_in", "ffn_w_out", "ple_w", "ple_gate_w")
    items_l1 = [(("s5_w_glu", 0), _owner_rows("s5_w_glu", dw_glu))] + [((k, 1), _owner_rows(k, gf1[k])) for k in ffn_names]
    items_l0 = [((k, 0), _owner_rows(k, gf0[k])) for k in ffn_names]
    dh0_a, dmix, dg1, db1 = _rowwise_bwd(_ln_mix_fn(alpha), [_whole(h0), _whole(mix0)], [g1, b1], [[_whole(a) for a in dh1]],
                                         need=[True, True], drow=[(d, F32), (d, BF16)], name="ln1_bwd_l0", tq=128)
    grads["ln1_g"] = jnp.concatenate([dg1, dg3])
    grads["ln1_b"] = jnp.concatenate([db1, db3])
    dw_out = _mm(att, dmix, ta=True, out_dtype=BF16, name="attn_out_dw")
    datt = _mm(dmix, w_out, tb=True, name="attn_out_dx")
    items_l0.append((("attn_w_out", 0), _owner_rows("attn_w_out", dw_out)))
    small_early = [n for n in WEIGHTS if n not in BIG_WEIGHTS and n not in ("mla_q_norm", "mla_kv_norm")]
    send_small, where_small = _pack([_by_owner(n, grads[n]).astype(F32) for n in small_early], 1)

    dmerge = _rowwise_bwd(_merge_fn, merge_rows, [], [[(datt, dw, 1)]], need=[True] * 6, drow=[(dw, F32)] * 6,
                          name="merge_bwd", tq=256)
    dq_s = dk_s = dv_s = None
    for k, (dil, zc, o_c, l_c, _, _) in enumerate(band):
        do_c, dl_c = to_classes(dmerge[k], dil), to_classes(dmerge[3 + k], dil)
        dq_c, dk_c, dv_c = _band_bwd(zc, o_c, l_c, do_c, dl_c, ndh, dil, f"band_bwd_d{dil}")
        dq_n, dk_n, dv_n = from_classes(dq_c, dil), from_classes(dk_c, dil), from_classes(dv_c, dil)
        dq_s = dq_n if dq_s is None else dq_s + dq_n
        dk_s = dk_n if dk_s is None else dk_s + dk_n
        dv_s = dv_n if dv_s is None else dv_s + dv_n

    dqn, dqp, delta, recv_l1 = _mla_bwd_q(qf, qpe, kvf, kpe, datt, out_a, lse_a, nh, "mla_bwd_q",
                                          side=("exchange", [rows for _, rows in items_l1]))
    take(items_l1, recv_l1)
    to_row = lambda a: a[:, ::LANE].T.reshape(nh, 1, t)
    dkn, dkp, dv, recv_l0 = _mla_bwd_kv(qf[:, : nh * LANE].astype(BF16), qpe, kvf, kpe, datt[:, : nh * LANE].astype(BF16),
                                        to_row(lse_a), to_row(delta), nh, "mla_bwd_kv",
                                        side=("exchange", [rows for _, rows in items_l0] + [send_small]))
    take(items_l0, recv_l0[:-1])
    sum_small = _sum_slots(recv_l0[-1], "sum_grads_small")
    dq_pe, dk_pe = _rowwise(_rope_bwd_fn(nh), [_whole(dqp), _whole(dkp), _whole(cos), _whole(sin)], [],
                            [(nh * LANE, nh * LANE, BF16), (LANE, LANE, BF16)], name="rope_bwd", tq=256)
    dqf = jnp.concatenate([dqn.astype(BF16), dq_pe], axis=1)
    dkvf = jnp.concatenate([dkn, dv], axis=1).astype(BF16)
    dwqp = _mm(qn, dqf, ta=True, name="q_up_dw")
    dqn_in = _mm(dqf, wqp, tb=True, name="q_up_dx")
    dwkvp = _mm(kvn, dkvf, ta=True, name="kv_up_dw")
    dkvn_in = _mm(dkvf, wkvp, tb=True, name="kv_up_dx")
    dql, dkvl, dgq, dgk = _rowwise_bwd(_rms_fn, [(z, ql, 0), (z, kvl, 1)], [gq, gk], [[_whole(dqn_in)], [_whole(dkvn_in)]],
                                       need=[True, True], drow=[(ql, BF16), (kvl, BF16)], name="rms_bwd", tq=256)
    grads["mla_q_norm"], grads["mla_kv_norm"] = dgq, dgk
    dz_lat = jnp.concatenate([dql, dkvl, dk_pe], axis=1)
    dz_dil = jnp.concatenate([dq_s, dk_s, dv_s], axis=1).astype(BF16)
    dw_lat_t = _mm(dz_lat, h0b, ta=True, out_dtype=BF16, name="attn_in_lat_dw")
    dw_dil_t = _mm(dz_dil, h0b, ta=True, out_dtype=BF16, name="attn_in_dil_dw")
    dh0_b = _mm(dz_lat, w_lat_t, name="attn_in_lat_dx")
    dh0_c = _mm(dz_dil, w_dil_t, name="attn_in_dil_dx")
    grad_x = _addn([dh0_a, dh0_b, dh0_c], "grad_x")[None]
    dw_in_t = jnp.concatenate([dw_lat_t[: lat + ROPE_DIM], dw_dil_t], axis=0)
    dwq_n = dwqp[:, : nh * LANE].reshape(ql, nh, NOPE_DIM)
    dwq_r = dwqp[:, nh * LANE :].reshape(ql, nh, LANE)[:, :, :ROPE_DIM]
    dwq = jnp.concatenate([dwq_n, dwq_r], axis=2).reshape(ql, nh * (NOPE_DIM + ROPE_DIM))
    dwkv_k = dwkvp[:, : nh * LANE].reshape(kvl, nh, NOPE_DIM)
    dwkv_v = dwkvp[:, nh * LANE :].reshape(kvl, nh, V_DIM)
    dwkv = jnp.concatenate([dwkv_k, dwkv_v], axis=2).reshape(kvl, nh * (NOPE_DIM + V_DIM))

    items_att = [(("attn_w_in", 0), dw_in_t.reshape(N_DEV, -1, PACK_COLS)),
                 (("mla_w_q_b", 0), _owner_rows("mla_w_q_b", dwq.astype(BF16))),
                 (("mla_w_kv_b", 0), _owner_rows("mla_w_kv_b", dwkv.astype(BF16)))]
    small_late = ["mla_q_norm", "mla_kv_norm"]
    send_late, where_late = _pack([_by_owner(n, grads[n]).astype(F32) for n in small_late], 1)
    received = _exchange_call("exchange", [rows for _, rows in items_att] + [send_late], "exchange_grads_last")
    take(items_att, received[:-1])
    sum_late = _sum_slots(received[-1], "sum_grads_norms")

    g_out, d_out, m_out, v_out = [], [], [], []
    for n in WEIGHTS:
        if n == "attn_w_in":
            g = shards[(n, 0)].reshape(local[n].shape[2], d).T[None]
        elif n in BIG_WEIGHTS:
            g = jnp.stack([shards[(n, layer)] for layer in range(local[n].shape[0])]).reshape(local[n].shape)
        elif n in small_late:
            g = _unpack(sum_late, 0, where_late[small_late.index(n)]).reshape(local[n].shape)
        else:
            g = _unpack(sum_small, 0, where_small[small_early.index(n)]).reshape(local[n].shape)
        dlt, mn, vn = _adamw(local[n], g, mom_m[n], mom_v[n], f"adamw_{n}")
        g_out.append(g)
        d_out.append(dlt)
        m_out.append(mn)
        v_out.append(vn)
    return (loss, grad_x, *g_out, *d_out, *m_out, *v_out)
```

```python
import functools
import math

import numpy as np
import jax
import jax.numpy as jnp
from jax import lax
from jax.experimental import pallas as pl
from jax.experimental.pallas import tpu as pltpu

F32 = jnp.float32
BF16 = jnp.bfloat16

NOPE_DIM, ROPE_DIM, V_DIM = 128, 64, 128
DIL_HEAD_DIM = 128
DIL_BRANCHES = ((128, 1), (512, 4), (2048, 16))
BLK = 128
LANE = 128
ROPE_THETA = 10000.0
S5_GROUP, S5_STATE = 16, 64
S5_CHUNK = 32
NEG = -1e30
ADAM_LR, ADAM_B1, ADAM_B2, ADAM_EPS, ADAM_WD, ADAM_STEP = 0.001, 0.9, 0.999, 1e-08, 0.01, 10
N_DEV = 8
AXES = ("x", "y", "c")
VMEM_LIMIT = 56 * 1024 * 1024
MM_VMEM_BUDGET = 40 * 1024 * 1024
PACK_COLS = 1024
PACK_ROW_TILE = 16
HI = lax.Precision.HIGH


def _tile(n, cands):
    for c in cands:
        if n % c == 0:
            return c
    return n


def _params(sem=None):
    return pltpu.CompilerParams(dimension_semantics=sem, vmem_limit_bytes=VMEM_LIMIT)


def _dot(a, b, dims, prec=None):
    return lax.dot_general(a, b, (dims, ((), ())), preferred_element_type=F32, precision=prec)


NN = ((1,), (0,))
NT = ((1,), (1,))
TN = ((0,), (0,))


def _mm(a, b, *, ta=False, tb=False, out_dtype=F32, name):
    (k, m) = a.shape if ta else a.shape[::-1]
    (n, k2) = b.shape if tb else b.shape[::-1]
    assert k == k2, (a.shape, b.shape, ta, tb)
    tm = _tile(m, (1024, 512, 256, 128))
    tn = _tile(n, (1024, 512, 384, 256, 128))
    sa, sb, so = a.dtype.itemsize, b.dtype.itemsize, jnp.dtype(out_dtype).itemsize

    def fits(tk):
        return 2 * (tm * tk * sa + tk * tn * sb) + 2 * tm * tn * so + 4 * tm * tn <= MM_VMEM_BUDGET

    tk = next((c for c in (2816, 2048, 1536, 1408, 1152, 1024, 768, 512, 384, 256, 128) if k % c == 0 and fits(c)), k)
    nk = k // tk
    dims = (((0,) if ta else (1,)), ((1,) if tb else (0,)))

    def body(a_ref, b_ref, o_ref, *acc):
        part = _dot(a_ref[...].astype(BF16), b_ref[...].astype(BF16), dims)
        if nk == 1:
            o_ref[...] = part.astype(o_ref.dtype)
            return
        (acc_ref,) = acc
        kk = pl.program_id(2)

        @pl.when(kk == 0)
        def _():
            acc_ref[...] = part

        @pl.when(kk > 0)
        def _():
            acc_ref[...] += part

        @pl.when(kk == nk - 1)
        def _():
            o_ref[...] = acc_ref[...].astype(o_ref.dtype)

    a_spec = pl.BlockSpec((tk, tm), lambda i, j, kk: (kk, i)) if ta else pl.BlockSpec((tm, tk), lambda i, j, kk: (i, kk))
    b_spec = pl.BlockSpec((tn, tk), lambda i, j, kk: (j, kk)) if tb else pl.BlockSpec((tk, tn), lambda i, j, kk: (kk, j))
    return pl.pallas_call(
        body,
        name=name,
        grid=(m // tm, n // tn, nk),
        in_specs=[a_spec, b_spec],
        out_specs=pl.BlockSpec((tm, tn), lambda i, j, kk: (i, j)),
        out_shape=jax.ShapeDtypeStruct((m, n), out_dtype),
        scratch_shapes=[pltpu.VMEM((tm, tn), F32)] if nk > 1 else [],
        compiler_params=_params(("parallel", "parallel", "arbitrary")),
    )(a, b)


def _row_spec(spec, tq):
    _, w, cb = spec
    return pl.BlockSpec((tq, w), lambda i, j: (i, cb + j))


def _full_spec(p):
    return pl.BlockSpec(p.shape, lambda i, j: (0,) * p.ndim)


def _rowwise(fn, rows, pars, outs, *, name, tq, ncol=1):
    t = rows[0][0].shape[0]
    tq = _tile(t, (tq, 128, 64, 32, 16, 8))
    nr, npar = len(rows), len(pars)

    def body(*refs):
        vals = [r[...].astype(F32) for r in refs[: nr + npar]]
        res = fn(*vals)
        for o, r in zip(refs[nr + npar :], res):
            o[...] = r.astype(o.dtype)

    res = pl.pallas_call(
        body,
        name=name,
        grid=(t // tq, ncol),
        in_specs=[_row_spec(s, tq) for s in rows] + [_full_spec(p) for p in pars],
        out_specs=[pl.BlockSpec((tq, w), lambda i, j: (i, j)) for (_, w, _) in outs],
        out_shape=[jax.ShapeDtypeStruct((t, wt), dt) for (wt, _, dt) in outs],
        compiler_params=_params(("parallel", "parallel")),
    )(*[s[0] for s in rows], *pars)
    return res


def _rowwise_bwd(fn, rows, pars, cots, *, need, drow, name, tq, ncol=1):
    t = rows[0][0].shape[0]
    tq = _tile(t, (tq, 128, 64, 32, 16, 8))
    nr, npar = len(rows), len(pars)
    assert ncol == 1 or npar == 0
    flat_cots = [s for c in cots for s in c]
    ncot = len(flat_cots)
    want = [k for k in range(nr) if need[k]]

    def body(*refs):
        vals = [r[...].astype(F32) for r in refs[: nr + npar]]
        cref = refs[nr + npar : nr + npar + ncot]
        oref = refs[nr + npar + ncot :]
        cvals, pos = [], 0
        for c in cots:
            acc = cref[pos][...].astype(F32)
            for q in range(1, len(c)):
                acc = acc + cref[pos + q][...].astype(F32)
            cvals.append(acc)
            pos += len(c)

        def closed(*diff):
            full = list(vals)
            for k, dv in zip(want + list(range(nr, nr + npar)), diff):
                full[k] = dv
            return tuple(fn(*full))

        diff_in = [vals[k] for k in want] + vals[nr:]
        _, vjp = jax.vjp(closed, *diff_in)
        grads = vjp(tuple(cvals))
        for q in range(len(want)):
            oref[q][...] = grads[q].astype(oref[q].dtype)
        if npar:
            first = pl.program_id(0) == 0

            @pl.when(first)
            def _():
                for q in range(npar):
                    oref[len(want) + q][...] = jnp.zeros_like(oref[len(want) + q])

            for q in range(npar):
                oref[len(want) + q][...] += grads[len(want) + q]

    out_specs = [pl.BlockSpec((tq, rows[k][1]), lambda i, j, cb=rows[k][2]: (i, j)) for k in want]
    out_specs += [_full_spec(p) for p in pars]
    out_shape = [jax.ShapeDtypeStruct((t, wt), dt) for (wt, dt) in drow]
    out_shape += [jax.ShapeDtypeStruct(p.shape, F32) for p in pars]
    return pl.pallas_call(
        body,
        name=name,
        grid=(t // tq, ncol),
        in_specs=[_row_spec(s, tq) for s in rows] + [_full_spec(p) for p in pars] + [_row_spec(s, tq) for s in flat_cots],
        out_specs=out_specs,
        out_shape=out_shape,
        compiler_params=_params(("arbitrary", "arbitrary") if npar else ("parallel", "parallel")),
    )(*[s[0] for s in rows], *pars, *[s[0] for s in flat_cots])


def _whole(a, w=None):
    return (a, a.shape[1] if w is None else w, 0)


def _rms_fn(ql, kvl, gq, gk):
    def one(x, g):
        return x * lax.rsqrt(jnp.mean(x * x, -1, keepdims=True) + 1e-6) * g

    return one(ql, gq), one(kvl, gk)


def _layernorm(s, g, b):
    mu = jnp.mean(s, -1, keepdims=True)
    d = s - mu
    var = jnp.mean(d * d, -1, keepdims=True)
    return d * lax.rsqrt(var + 1e-5) * g + b


def _ln_mix_fn(alpha):
    def fn(h, mix, g, b):
        return (_layernorm(alpha * h + mix, g, b),)

    return fn


def _ln_glu_fn(alpha):
    def fn(h, vg, g, b):
        d = h.shape[1]
        return (_layernorm(alpha * h + vg[:, :d] * jax.nn.sigmoid(vg[:, d:]), g, b),)

    return fn


def _ln_ffn_fn(alpha):
    def fn(h, f, pw, gate, g, b):
        return (_layernorm(alpha * h + f + pw * jax.nn.sigmoid(gate), g, b),)

    return fn


def _swiglu_fn(g, u):
    return (jax.nn.silu(g) * u,)


def _gelu_fn(ys, h, d):
    return (jax.nn.gelu(ys + d * h),)


def _merge_fn(o1, o2, o3, l1, l2, l3):
    m = jnp.maximum(jnp.maximum(l1, l2), l3)
    e1, e2, e3 = jnp.exp(l1 - m), jnp.exp(l2 - m), jnp.exp(l3 - m)
    return ((e1 * o1 + e2 * o2 + e3 * o3) / (e1 + e2 + e3),)


def _swap_halves(t):
    w = t.shape[1]
    lane = lax.broadcasted_iota(jnp.int32, t.shape, 1) % LANE
    up = jnp.where(lane < ROPE_DIM, pltpu.roll(t, ROPE_DIM // 2, 1), 0.0)
    return jnp.where(lane < ROPE_DIM // 2, pltpu.roll(t, w - ROPE_DIM // 2, 1), up)


def _rope_fwd_fn(nh):
    def fn(qpe, kpe, cos, sin):
        cq, sq = jnp.tile(cos, (1, nh)), jnp.tile(sin, (1, nh))
        return qpe * cq + _swap_halves(qpe) * sq, kpe * cos + _swap_halves(kpe) * sin

    return fn


def _rope_bwd_fn(nh):
    def fn(dq, dk_heads, cos, sin):
        cq, sq = jnp.tile(cos, (1, nh)), jnp.tile(sin, (1, nh))
        dk = dk_heads[:, :LANE]
        for h in range(1, nh):
            dk = dk + dk_heads[:, h * LANE : (h + 1) * LANE]
        return dq * cq + _swap_halves(dq * sq), dk * cos + _swap_halves(dk * sin)

    return fn


def _rope_tables(positions, name):
    t = positions.shape[0]
    tq = _tile(t, (512, 128, 8))
    half = ROPE_DIM // 2

    def body(p_ref, c_ref, s_ref):
        lane = lax.broadcasted_iota(jnp.int32, (tq, LANE), 1)
        idx = (lane % half).astype(F32)
        inv_freq = jnp.exp(idx * (-math.log(ROPE_THETA) / half))
        ang = p_ref[...].astype(F32) * inv_freq
        live = lane < ROPE_DIM
        c_ref[...] = jnp.where(live, jnp.cos(ang), 0.0)
        s_ref[...] = jnp.where(live, jnp.where(lane < half, -jnp.sin(ang), jnp.sin(ang)), 0.0)

    return pl.pallas_call(
        body,
        name=name,
        grid=(t // tq,),
        in_specs=[pl.BlockSpec((tq, 1), lambda i: (i, 0))],
        out_specs=[pl.BlockSpec((tq, LANE), lambda i: (i, 0))] * 2,
        out_shape=[jax.ShapeDtypeStruct((t, LANE), F32)] * 2,
        compiler_params=_params(("parallel",)),
    )(positions)


def _loss_kernel(y, target, name):
    t, d = y.shape
    tq = _tile(t, (256, 128, 8))

    def body(y_ref, t_ref, dy_ref, l_ref):
        @pl.when(pl.program_id(0) == 0)
        def _():
            l_ref[...] = jnp.zeros_like(l_ref)

        e = y_ref[...] - t_ref[...]
        dy_ref[...] = e * (1.0 / d)
        l_ref[...] += jnp.sum(e * e) * (0.5 / d)

    return pl.pallas_call(
        body,
        name=name,
        grid=(t // tq,),
        in_specs=[pl.BlockSpec((tq, d), lambda i: (i, 0))] * 2,
        out_specs=[pl.BlockSpec((tq, d), lambda i: (i, 0)), pl.BlockSpec((8, LANE), lambda i: (0, 0))],
        out_shape=[jax.ShapeDtypeStruct((t, d), F32), jax.ShapeDtypeStruct((8, LANE), F32)],
        compiler_params=_params(("arbitrary",)),
    )(y, target)


def _addn(arrs, name):
    w = arrs[0].shape[1]
    return _rowwise(lambda *v: (functools.reduce(lambda p, q: p + q, v),), [_whole(a) for a in arrs], [], [(w, w, F32)],
                    name=name, tq=256)[0]


def _mla_tiles(t):
    tq = _tile(t, (512, 256, 128))
    return tq, t // tq


def _call_with_side(body, side, n_in, grid, *, name, in_specs, out_specs, out_shape):
    if side is None:
        return pl.pallas_call(body, name=name, grid=grid, in_specs=in_specs, out_specs=out_specs, out_shape=out_shape,
                              compiler_params=_params(("parallel", "parallel")))
    n_out, n = len(out_specs), len(side[1])

    def wrapped(*refs):
        ins, side_ins = refs[:n_in], refs[n_in : n_in + n]
        outs, side_outs = refs[n_in + n : n_in + n + n_out], refs[n_in + n + n_out : n_in + 2 * n + n_out]
        done = _run_side(side, side_ins, side_outs, refs[n_in + 2 * n + n_out :], grid[0], grid[1])
        body(*ins, *outs)
        done()

    call = pl.pallas_call(wrapped, name=name, grid=grid, in_specs=in_specs + [ANY_SPEC] * n,
                          out_specs=out_specs + [ANY_SPEC] * n, out_shape=out_shape + _side_out_shapes(side),
                          scratch_shapes=_comm_sems(n), compiler_params=_params(("arbitrary", "arbitrary")))

    def run(*args):
        res = call(*args, *side[1])
        return (*res[:n_out], list(res[n_out:]))

    return run


def _mla_fwd(qf, qpe, kvf, kpe, nh, name, side=None):
    t = qf.shape[0]
    tq, nq = _mla_tiles(t)
    scale = (NOPE_DIM + ROPE_DIM) ** -0.5

    def body(qn_ref, qp_ref, kn_ref, kp_ref, v_ref, o_ref, lse_ref):
        i = pl.program_id(1)
        qn = qn_ref[...].astype(BF16)
        qp = qp_ref[...]

        def step(j, carry, masked):
            m, l, acc = carry
            ks = pl.ds(pl.multiple_of(j * tq, tq), tq)
            s = (_dot(qn, kn_ref[ks, :], NT) + _dot(qp, kp_ref[ks, :], NT)) * scale
            if masked:
                row = lax.broadcasted_iota(jnp.int32, (tq, tq), 0)
                col = lax.broadcasted_iota(jnp.int32, (tq, tq), 1)
                s = jnp.where(col <= row, s, NEG)
            m_new = jnp.maximum(m, jnp.max(s, -1, keepdims=True))
            p = jnp.exp(s - m_new)
            a = jnp.exp(m - m_new)
            return m_new, a * l + jnp.sum(p, -1, keepdims=True), a * acc + _dot(p.astype(BF16), v_ref[ks, :], NN)

        init = (jnp.full((tq, 1), NEG, F32), jnp.zeros((tq, 1), F32), jnp.zeros((tq, V_DIM), F32))
        carry = lax.fori_loop(0, i, lambda j, c: step(j, c, False), init)
        m, l, acc = step(i, carry, True)
        o_ref[...] = (acc / l).astype(o_ref.dtype)
        lse_ref[...] = jnp.broadcast_to(m + jnp.log(l), (tq, LANE))

    blk = lambda h, i: (i, h)
    return _call_with_side(
        body, side, 5, (nh, nq),
        name=name,
        in_specs=[
            pl.BlockSpec((tq, LANE), blk),
            pl.BlockSpec((tq, LANE), blk),
            pl.BlockSpec((t, LANE), lambda h, i: (0, h)),
            pl.BlockSpec((t, LANE), lambda h, i: (0, 0)),
            pl.BlockSpec((t, LANE), lambda h, i: (0, nh + h)),
        ],
        out_specs=[pl.BlockSpec((tq, LANE), blk), pl.BlockSpec((tq, LANE), blk)],
        out_shape=[jax.ShapeDtypeStruct((t, nh * LANE), BF16), jax.ShapeDtypeStruct((t, nh * LANE), F32)],
    )(qf, qpe, kvf, kpe, kvf)


def _mla_bwd_q(qf, qpe, kvf, kpe, do, o, lse, nh, name, side=None):
    t = qf.shape[0]
    tq, nq = _mla_tiles(t)
    scale = (NOPE_DIM + ROPE_DIM) ** -0.5

    def body(qn_ref, qp_ref, kn_ref, kp_ref, v_ref, do_ref, o_ref, lse_ref, dqn_ref, dqp_ref, dl_ref):
        i = pl.program_id(1)
        qn = qn_ref[...].astype(BF16)
        qp = qp_ref[...]
        dof = do_ref[...].astype(F32)
        dob = dof.astype(BF16)
        delta = jnp.sum(dof * o_ref[...].astype(F32), -1, keepdims=True)
        lse1 = lse_ref[:, :1]

        def step(j, carry, masked):
            dqn, dqp = carry
            ks = pl.ds(pl.multiple_of(j * tq, tq), tq)
            kn, kp = kn_ref[ks, :], kp_ref[ks, :]
            s = (_dot(qn, kn, NT) + _dot(qp, kp, NT)) * scale
            p = jnp.exp(s - lse1)
            if masked:
                row = lax.broadcasted_iota(jnp.int32, (tq, tq), 0)
                col = lax.broadcasted_iota(jnp.int32, (tq, tq), 1)
                p = jnp.where(col <= row, p, 0.0)
            dp = _dot(dob, v_ref[ks, :], NT)
            ds = (p * (dp - delta) * scale).astype(BF16)
            return dqn + _dot(ds, kn, NN), dqp + _dot(ds, kp, NN)

        init = (jnp.zeros((tq, LANE), F32), jnp.zeros((tq, LANE), F32))
        carry = lax.fori_loop(0, i, lambda j, c: step(j, c, False), init)
        dqn, dqp = step(i, carry, True)
        dqn_ref[...] = dqn
        dqp_ref[...] = dqp
        dl_ref[...] = jnp.broadcast_to(delta, (tq, LANE))

    blk = lambda h, i: (i, h)
    bs = pl.BlockSpec((tq, LANE), blk)
    return _call_with_side(
        body, side, 8, (nh, nq),
        name=name,
        in_specs=[
            bs,
            bs,
            pl.BlockSpec((t, LANE), lambda h, i: (0, h)),
            pl.BlockSpec((t, LANE), lambda h, i: (0, 0)),
            pl.BlockSpec((t, LANE), lambda h, i: (0, nh + h)),
            bs,
            bs,
            bs,
        ],
        out_specs=[bs, bs, bs],
        out_shape=[jax.ShapeDtypeStruct((t, nh * LANE), F32)] * 3,
    )(qf, qpe, kvf, kpe, kvf, do, o, lse)


def _mla_bwd_kv(qn16, qpe, kvf, kpe, do16, lse_row, delta_row, nh, name, side=None):
    t = qn16.shape[0]
    tq, nq = _mla_tiles(t)
    scale = (NOPE_DIM + ROPE_DIM) ** -0.5

    def body(kn_ref, kp_ref, v_ref, qn_ref, qp_ref, do_ref, lse_ref, dl_ref, dkn_ref, dkp_ref, dv_ref):
        j = pl.program_id(1)
        kn, kp, v = kn_ref[...], kp_ref[...], v_ref[...]

        def step(i, carry, masked):
            dkn, dkp, dv = carry
            qs = pl.ds(pl.multiple_of(i * tq, tq), tq)
            qn, qp, dob = qn_ref[qs, :], qp_ref[qs, :], do_ref[qs, :]
            st = (_dot(kn, qn, NT) + _dot(kp, qp, NT)) * scale
            pt = jnp.exp(st - lse_ref[0, :, qs])
            if masked:
                key = lax.broadcasted_iota(jnp.int32, (tq, tq), 0)
                qry = lax.broadcasted_iota(jnp.int32, (tq, tq), 1)
                pt = jnp.where(key <= qry, pt, 0.0)
            dv = dv + _dot(pt.astype(BF16), dob, NN)
            dpt = _dot(v, dob, NT)
            dst = (pt * (dpt - dl_ref[0, :, qs]) * scale).astype(BF16)
            return dkn + _dot(dst, qn, NN), dkp + _dot(dst, qp, NN), dv

        z = jnp.zeros((tq, LANE), F32)
        carry = step(j, (z, z, z), True)
        dkn, dkp, dv = lax.fori_loop(j + 1, nq, lambda i, c: step(i, c, False), carry)
        dkn_ref[...] = dkn
        dkp_ref[...] = dkp
        dv_ref[...] = dv

    blk = pl.BlockSpec((tq, LANE), lambda h, j: (j, h))
    res = lambda f: pl.BlockSpec((t, LANE), f)
    row = pl.BlockSpec((1, 1, t), lambda h, j: (h, 0, 0))
    return _call_with_side(
        body, side, 8, (nh, nq),
        name=name,
        in_specs=[
            blk,
            pl.BlockSpec((tq, LANE), lambda h, j: (j, 0)),
            pl.BlockSpec((tq, LANE), lambda h, j: (j, nh + h)),
            res(lambda h, j: (0, h)),
            res(lambda h, j: (0, h)),
            res(lambda h, j: (0, h)),
            row,
            row,
        ],
        out_specs=[blk, blk, blk],
        out_shape=[jax.ShapeDtypeStruct((t, nh * LANE), F32)] * 3,
    )(kvf, kpe, kvf, qn16, qpe, do16, lse_row, delta_row)


def _alibi_slopes(n):
    return [float(2.0 ** (-8.0 * i / n)) for i in range(1, n + 1)]


def _window_mask(has_prev):
    qi = lax.broadcasted_iota(jnp.int32, (BLK, 2 * BLK), 0)
    ki = lax.broadcasted_iota(jnp.int32, (BLK, 2 * BLK), 1)
    dist = qi + BLK - ki
    valid = (dist >= 0) & (dist <= BLK) & ((ki >= BLK) | has_prev)
    return valid, dist.astype(F32)


def _band_fwd(zd, nh, dil, name):
    t = zd.shape[0]
    nbc = t // BLK // dil
    scale = DIL_HEAD_DIM ** -0.5
    slopes = _alibi_slopes(nh)
    dw = nh * LANE

    def body(q_ref, kc_ref, kp_ref, vc_ref, vp_ref, o_ref, l_ref):
        mask, dist = _window_mask(pl.program_id(1) > 0)
        for h in range(nh):
            sl = slice(h * LANE, (h + 1) * LANE)
            keys = jnp.concatenate([kp_ref[:, sl], kc_ref[:, sl]], axis=0)
            vals = jnp.concatenate([vp_ref[:, sl], vc_ref[:, sl]], axis=0)
            s = jnp.where(mask, _dot(q_ref[:, sl], keys, NT) * scale - (slopes[h] * dil) * dist, NEG)
            m = jnp.max(s, -1, keepdims=True)
            e = jnp.exp(s - m)
            l = jnp.sum(e, -1, keepdims=True)
            o_ref[:, sl] = _dot((e * (1.0 / l)).astype(BF16), vals, NN)
            l_ref[:, sl] = jnp.broadcast_to(m + jnp.log(l), (BLK, LANE))

    prev = lambda i: jnp.maximum(i - 1, 0)
    spec = lambda f: pl.BlockSpec((BLK, dw), f)
    return pl.pallas_call(
        body,
        name=name,
        grid=(dil, nbc),
        in_specs=[
            spec(lambda r, i: (r * nbc + i, 0)),
            spec(lambda r, i: (r * nbc + i, 1)),
            spec(lambda r, i: (r * nbc + prev(i), 1)),
            spec(lambda r, i: (r * nbc + i, 2)),
            spec(lambda r, i: (r * nbc + prev(i), 2)),
        ],
        out_specs=[spec(lambda r, i: (r * nbc + i, 0))] * 2,
        out_shape=[jax.ShapeDtypeStruct((t, dw), F32)] * 2,
        compiler_params=_params(("parallel", "parallel")),
    )(zd, zd, zd, zd, zd)


def _band_bwd(zd, o, lse, do, dl, nh, dil, name):
    t = zd.shape[0]
    nbc = t // BLK // dil
    scale = DIL_HEAD_DIM ** -0.5
    slopes = _alibi_slopes(nh)
    dw = nh * LANE

    def body(q_ref, k_ref, v_ref, kp_ref, vp_ref, qn_ref, o_ref, l_ref, do_ref, dl_ref, on_ref, ln_ref, don_ref, dln_ref,
             dq_ref, dk_ref, dv_ref):
        mask, dist = _window_mask(pl.program_id(1) > 0)
        mask_n, dist_n = _window_mask(pl.program_id(1) + 1 < nbc)
        mask_n, dist_n = mask_n[:, :BLK], dist_n[:, :BLK]
        for h in range(nh):
            sl = slice(h * LANE, (h + 1) * LANE)
            bias = slopes[h] * dil
            q, k, v, qn = q_ref[:, sl], k_ref[:, sl], v_ref[:, sl], qn_ref[:, sl]
            keys = jnp.concatenate([kp_ref[:, sl], k], axis=0)
            vals = jnp.concatenate([vp_ref[:, sl], v], axis=0)
            dof, donf = do_ref[:, sl], don_ref[:, sl]
            dob, donb = dof.astype(BF16), donf.astype(BF16)
            lse1, lsen1 = l_ref[:, sl][:, :1], ln_ref[:, sl][:, :1]
            adj = jnp.sum(dl_ref[:, sl] - dof * o_ref[:, sl], -1, keepdims=True)
            adjn = jnp.sum(dln_ref[:, sl] - donf * on_ref[:, sl], -1, keepdims=True)
            p = jnp.where(mask, jnp.exp(_dot(q, keys, NT) * scale - bias * dist - lse1), 0.0)
            ds = (p * (_dot(dob, vals, NT) + adj)).astype(BF16)
            dq_ref[:, sl] = _dot(ds, keys, NN) * scale
            pn = jnp.where(mask_n, jnp.exp(_dot(qn, k, NT) * scale - bias * dist_n - lsen1), 0.0)
            dsn = (pn * (_dot(donb, v, NT) + adjn)).astype(BF16)
            both_q = jnp.concatenate([q, qn], axis=0)
            dk_ref[:, sl] = _dot(jnp.concatenate([ds[:, BLK:], dsn], axis=0), both_q, TN) * scale
            dv_ref[:, sl] = _dot(jnp.concatenate([p[:, BLK:].astype(BF16), pn.astype(BF16)], axis=0),
                                 jnp.concatenate([dob, donb], axis=0), TN)

    prev = lambda i: jnp.maximum(i - 1, 0)
    nxt = lambda i: jnp.minimum(i + 1, nbc - 1)
    spec = lambda f: pl.BlockSpec((BLK, dw), f)
    cur, nx = spec(lambda r, i: (r * nbc + i, 0)), spec(lambda r, i: (r * nbc + nxt(i), 0))
    return pl.pallas_call(
        body,
        name=name,
        grid=(dil, nbc),
        in_specs=[cur, spec(lambda r, i: (r * nbc + i, 1)), spec(lambda r, i: (r * nbc + i, 2)),
                  spec(lambda r, i: (r * nbc + prev(i), 1)), spec(lambda r, i: (r * nbc + prev(i), 2)),
                  nx, cur, cur, cur, cur, nx, nx, nx, nx],
        out_specs=[cur] * 3,
        out_shape=[jax.ShapeDtypeStruct((t, dw), F32)] * 3,
        compiler_params=_params(("parallel", "parallel")),
    )(zd, zd, zd, zd, zd, zd, o, lse, do, dl, o, lse, do, dl)


def _s5_ops(a_re, a_im, ldt, bt_re, bt_im, c_re, c_im):
    L, g, p = S5_CHUNK, S5_GROUP, S5_STATE
    dt = jnp.exp(ldt)
    lr, li = a_re * dt, a_im * dt
    er = jnp.exp(lr)
    lam_re, lam_im = er * jnp.cos(li), er * jnp.sin(li)
    nr, ni = lam_re - 1.0, lam_im
    den = a_re * a_re + a_im * a_im
    fr, fi = (nr * a_re + ni * a_im) / den, (ni * a_re - nr * a_im) / den
    bb_re, bb_im = fr * bt_re - fi * bt_im, fr * bt_im + fi * bt_re

    def power(tau):
        mag = jnp.exp(tau * lr)
        return mag * jnp.cos(tau * li), mag * jnp.sin(tau * li)

    step = lax.broadcasted_iota(jnp.int32, (L, 1), 0).astype(F32)

    def outer(pr, pi, mr, mi):
        re = pr[:, None, :] * mr[None] - pi[:, None, :] * mi[None]
        im = pr[:, None, :] * mi[None] + pi[:, None, :] * mr[None]
        return re.reshape(L * g, p), im.reshape(L * g, p)

    half = float(L // 2)
    cp_re, cp_im = outer(*power(step - half), c_re, c_im)
    pb_re, pb_im = outer(*power(half - step), bb_re, bb_im)
    toep = _dot(cp_re, pb_re, NT, HI) - _dot(cp_im, pb_im, NT, HI)
    trow = lax.broadcasted_iota(jnp.int32, (L * g, L * g), 0) // g
    scol = lax.broadcasted_iota(jnp.int32, (L * g, L * g), 1) // g
    toep = jnp.where(trow >= scol, toep, 0.0)
    et_re, et_im = outer(*power(float(L - 1) - step), bb_re, bb_im)
    f_re, f_im = outer(*power(step + 1.0), c_re, c_im)
    big_re, big_im = power(jnp.full((1, 1), float(L), F32))
    return toep, et_re, et_im, f_re, f_im, big_re, big_im


def _s5_y(ops, u, sp_re, sp_im):
    toep, _, _, f_re, f_im, _, _ = ops
    return _dot(u, toep, NT) + _dot(sp_re, f_re, NT, HI) - _dot(sp_im, f_im, NT, HI)


def _s5_group_specs(ng):
    vec = pl.BlockSpec((1, 1, S5_STATE), lambda g: (g, 0, 0))
    one = pl.BlockSpec((1, 1, 1), lambda g: (g, 0, 0))
    mat = pl.BlockSpec((1, S5_GROUP, S5_STATE), lambda g: (g, 0, 0))
    return [vec, vec, one, mat, mat, mat, mat]


def _s5_load(refs):
    return [r[0] for r in refs]


def _s5_local(prm, u, name):
    ng, n, w = u.shape
    p = S5_STATE

    def body(*refs):
        ops = _s5_ops(*_s5_load(refs[:7]))
        uu = refs[7][0]
        refs[8][0] = _dot(uu, ops[1], NN, HI)
        refs[9][0] = _dot(uu, ops[2], NN, HI)
        refs[10][0] = ops[5]
        refs[11][0] = ops[6]

    blk = lambda a, b: pl.BlockSpec((1, a, b), lambda g: (g, 0, 0))
    return pl.pallas_call(
        body,
        name=name,
        grid=(ng,),
        in_specs=_s5_group_specs(ng) + [blk(n, w)],
        out_specs=[blk(n, p), blk(n, p), blk(1, p), blk(1, p)],
        out_shape=[jax.ShapeDtypeStruct((ng, n, p), F32)] * 2 + [jax.ShapeDtypeStruct((ng, 1, p), F32)] * 2,
        compiler_params=_params(("parallel",)),
    )(*prm, u)


def _s5_carry(e_re, e_im, lam_re, lam_im, name):
    n, w = e_re.shape
    cw = _tile(w, (1024, 512, 256, 128))

    def body(er_ref, ei_ref, lr_ref, li_ref, sr_ref, si_ref):
        lr, li = lr_ref[...], li_ref[...]

        def step(k, carry):
            sr, si = carry
            row = pl.ds(k, 1)
            sr_ref[row, :] = sr
            si_ref[row, :] = si
            return lr * sr - li * si + er_ref[row, :], li * sr + lr * si + ei_ref[row, :]

        z = jnp.zeros((1, cw), F32)
        lax.fori_loop(0, n, step, (z, z))

    col = pl.BlockSpec((n, cw), lambda j: (0, j))
    one = pl.BlockSpec((1, cw), lambda j: (0, j))
    return pl.pallas_call(
        body,
        name=name,
        grid=(w // cw,),
        in_specs=[col, col, one, one],
        out_specs=[col, col],
        out_shape=[jax.ShapeDtypeStruct((n, w), F32)] * 2,
        compiler_params=_params(("parallel",)),
    )(e_re, e_im, lam_re, lam_im)


def _s5_carry_bwd(dsp_re, dsp_im, sp_re, sp_im, lam_re, lam_im, name):
    n, w = dsp_re.shape
    cw = _tile(w, (1024, 512, 256, 128))

    def body(dr_ref, di_ref, sr_ref, si_ref, lr_ref, li_ref, gr_ref, gi_ref, dlr_ref, dli_ref):
        lr, li = lr_ref[...], li_ref[...]

        def step(q, carry):
            gr_next, gi_next, dr_next, di_next, alr, ali = carry
            k = n - 1 - q
            row = pl.ds(k, 1)
            gr = dr_next + lr * gr_next + li * gi_next
            gi = di_next - li * gr_next + lr * gi_next
            gr_ref[row, :] = gr
            gi_ref[row, :] = gi
            sr, si = sr_ref[row, :], si_ref[row, :]
            return gr, gi, dr_ref[row, :], di_ref[row, :], alr + gr * sr + gi * si, ali + gi * sr - gr * si

        z = jnp.zeros((1, cw), F32)
        out = lax.fori_loop(0, n, step, (z, z, z, z, z, z))
        dlr_ref[...] = out[4]
        dli_ref[...] = out[5]

    col = pl.BlockSpec((n, cw), lambda j: (0, j))
    one = pl.BlockSpec((1, cw), lambda j: (0, j))
    return pl.pallas_call(
        body,
        name=name,
        grid=(w // cw,),
        in_specs=[col, col, col, col, one, one],
        out_specs=[col, col, one, one],
        out_shape=[jax.ShapeDtypeStruct((n, w), F32)] * 2 + [jax.ShapeDtypeStruct((1, w), F32)] * 2,
        compiler_params=_params(("parallel",)),
    )(dsp_re, dsp_im, sp_re, sp_im, lam_re, lam_im)


def _s5_out(prm, u, sp_re, sp_im, name):
    ng, n, w = u.shape
    p = S5_STATE

    def body(*refs):
        ops = _s5_ops(*_s5_load(refs[:7]))
        refs[10][0] = _s5_y(ops, refs[7][0], refs[8][0], refs[9][0])

    blk = lambda a, b: pl.BlockSpec((1, a, b), lambda g: (g, 0, 0))
    return pl.pallas_call(
        body,
        name=name,
        grid=(ng,),
        in_specs=_s5_group_specs(ng) + [blk(n, w), blk(n, p), blk(n, p)],
        out_specs=blk(n, w),
        out_shape=jax.ShapeDtypeStruct((ng, n, w), F32),
        compiler_params=_params(("parallel",)),
    )(*prm, u, sp_re, sp_im)


def _s5_bwd_state(prm, dy, name):
    ng, n, w = dy.shape
    p = S5_STATE

    def body(*refs):
        ops = _s5_ops(*_s5_load(refs[:7]))
        d = refs[7][0]
        refs[8][0] = _dot(d, ops[3], NN, HI)
        refs[9][0] = -_dot(d, ops[4], NN, HI)

    blk = lambda a, b: pl.BlockSpec((1, a, b), lambda g: (g, 0, 0))
    return pl.pallas_call(
        body,
        name=name,
        grid=(ng,),
        in_specs=_s5_group_specs(ng) + [blk(n, w)],
        out_specs=[blk(n, p), blk(n, p)],
        out_shape=[jax.ShapeDtypeStruct((ng, n, p), F32)] * 2,
        compiler_params=_params(("parallel",)),
    )(*prm, dy)


def _s5_bwd_main(prm, u, sp_re, sp_im, dy, g_re, g_im, dlam_re, dlam_im, name):
    ng, n, w = u.shape
    p = S5_STATE

    def body(*refs):
        prm_v = _s5_load(refs[:7])
        uu, sr, si, d, gr, gi, dlr, dli = [r[0] for r in refs[7:15]]
        ops, ops_vjp = jax.vjp(_s5_ops, *prm_v)
        cots = (_dot(d, uu, TN), _dot(uu, gr, TN, HI), _dot(uu, gi, TN, HI), _dot(d, sr, TN, HI), -_dot(d, si, TN, HI),
                dlr, dli)
        grads = ops_vjp(cots)
        for q in range(7):
            refs[15 + q][0] = grads[q]
        refs[22][0] = _dot(d, ops[0], NN) + _dot(gr, ops[1], NT, HI) + _dot(gi, ops[2], NT, HI)

    blk = lambda a, b: pl.BlockSpec((1, a, b), lambda g: (g, 0, 0))
    prm_specs = _s5_group_specs(ng)
    return pl.pallas_call(
        body,
        name=name,
        grid=(ng,),
        in_specs=prm_specs + [blk(n, w), blk(n, p), blk(n, p), blk(n, w), blk(n, p), blk(n, p), blk(1, p), blk(1, p)],
        out_specs=prm_specs + [blk(n, w)],
        out_shape=[jax.ShapeDtypeStruct(a.shape, F32) for a in prm] + [jax.ShapeDtypeStruct((ng, n, w), F32)],
        compiler_params=_params(("parallel",)),
    )(*prm, u, sp_re, sp_im, dy, g_re, g_im, dlam_re, dlam_im)


SLAB_GROUPS = LANE // S5_GROUP
CHUNK_TILES = S5_CHUNK * S5_GROUP // LANE


def _segment_transpose(tiles):
    seg = lax.broadcasted_iota(jnp.int32, (1, LANE), 1) // S5_GROUP
    tiles = list(tiles)
    stride = 1
    while stride < SLAB_GROUPS:
        upper = (seg & stride) != 0
        shift = S5_GROUP * stride
        new = list(tiles)
        for i in range(SLAB_GROUPS):
            if i & stride:
                continue
            j = i + stride
            new[i] = jnp.where(upper, pltpu.roll(tiles[j], shift, 1), tiles[i])
            new[j] = jnp.where(upper, tiles[j], pltpu.roll(tiles[i], LANE - shift, 1))
        tiles = new
        stride *= 2
    return tiles


def _to_groups(h, name):
    t, d = h.shape
    n = t // S5_CHUNK

    def body(x_ref, o_ref):
        for a in range(CHUNK_TILES):
            rows = [x_ref[pl.ds(SLAB_GROUPS * a + b, n, stride=S5_CHUNK), :] for b in range(SLAB_GROUPS)]
            for g, tile in enumerate(_segment_transpose(rows)):
                o_ref[g, :, a * LANE : (a + 1) * LANE] = tile

    return pl.pallas_call(
        body,
        name=name,
        grid=(d // LANE,),
        in_specs=[pl.BlockSpec((t, LANE), lambda q: (0, q))],
        out_specs=pl.BlockSpec((SLAB_GROUPS, n, S5_CHUNK * S5_GROUP), lambda q: (q, 0, 0)),
        out_shape=jax.ShapeDtypeStruct((d // S5_GROUP, n, S5_CHUNK * S5_GROUP), F32),
        compiler_params=_params(("parallel",)),
    )(h)


def _from_groups(y, name):
    ng, n, w = y.shape
    t, d = n * S5_CHUNK, ng * S5_GROUP

    def body(y_ref, x_ref):
        for a in range(CHUNK_TILES):
            tiles = [y_ref[g, :, a * LANE : (a + 1) * LANE] for g in range(SLAB_GROUPS)]
            for b, row in enumerate(_segment_transpose(tiles)):
                x_ref[pl.ds(SLAB_GROUPS * a + b, n, stride=S5_CHUNK), :] = row

    return pl.pallas_call(
        body,
        name=name,
        grid=(d // LANE,),
        in_specs=[pl.BlockSpec((SLAB_GROUPS, n, w), lambda q: (q, 0, 0))],
        out_specs=pl.BlockSpec((t, LANE), lambda q: (0, q)),
        out_shape=jax.ShapeDtypeStruct((t, d), F32),
        compiler_params=_params(("parallel",)),
    )(y)


def _states_to_cols(e):
    ng, n, p = e.shape
    return e.transpose(1, 0, 2).reshape(n, ng * p)


def _cols_to_states(s, ng):
    n = s.shape[0]
    return s.reshape(n, ng, S5_STATE).transpose(1, 0, 2)


def _me():
    return lax.axis_index("x"), lax.axis_index("y"), lax.axis_index("c")


def _flip(v, bit):
    return 1 - v if bit else v


COPIES = N_DEV - 1
ANY_SPEC = pl.BlockSpec(memory_space=pl.ANY)


def _comm_sems(n):
    return [pltpu.SemaphoreType.DMA((COPIES * n,)), pltpu.SemaphoreType.DMA((COPIES * n,)), pltpu.SemaphoreType.DMA((n,))]


def _gather_phases(x_refs, out_refs, send_sems, recv_sems, local_sems):
    mx, my, mc = _me()
    me, sibling = (mx, my, mc), (mx, my, 1 - mc)
    chips = [(1 - mx, my), (mx, 1 - my), (1 - mx, 1 - my)]
    arrays = range(len(x_refs))

    def slot(a, px, py, pc):
        return out_refs[a].at[4 * px + 2 * py + pc]

    def copy(a, k, block, to, from_input=False):
        return pltpu.make_async_remote_copy(
            src_ref=x_refs[a] if from_input else slot(a, *block),
            dst_ref=slot(a, *block),
            send_sem=send_sems.at[COPIES * a + k],
            recv_sem=recv_sems.at[COPIES * a + k],
            device_id=to,
            device_id_type=pl.DeviceIdType.MESH,
        )

    mine = [pltpu.make_async_copy(x_refs[a], slot(a, *me), local_sems.at[a]) for a in arrays]
    first = [copy(a, 0, me, sibling, True) for a in arrays]
    first += [copy(a, 1 + j, me, (*chip, mc), True) for a in arrays for j, chip in enumerate(chips)]
    passed = {(a, j): copy(a, 4 + j, (*chip, mc), sibling) for a in arrays for j, chip in enumerate(chips)}

    def start():
        for cp in mine + first:
            cp.start()

    def relay():
        for j, chip in enumerate(chips):
            for a in arrays:
                copy(a, 1 + j, (*chip, mc), me).wait_recv()
                passed[(a, j)].start()

    def finish():
        for a in arrays:
            copy(a, 0, sibling, me).wait_recv()
            for j, chip in enumerate(chips):
                copy(a, 4 + j, (*chip, 1 - mc), me).wait_recv()
        for cp in first + list(passed.values()):
            cp.wait_send()
        for cp in mine:
            cp.wait()

    return start, relay, finish


def _exchange_phases(g_refs, out_refs, send_sems, recv_sems, local_sems):
    mx, my, mc = _me()
    mine_idx = 4 * mx + 2 * my + mc
    arrays = range(len(g_refs))
    own = [pltpu.make_async_copy(g_refs[a].at[mine_idx], out_refs[a].at[mine_idx], local_sems.at[a]) for a in arrays]
    copies = []
    for m in range(1, N_DEV):
        px, py, pc = _flip(mx, m & 4), _flip(my, m & 2), _flip(mc, m & 1)
        for a in arrays:
            copies.append(
                pltpu.make_async_remote_copy(
                    src_ref=g_refs[a].at[4 * px + 2 * py + pc],
                    dst_ref=out_refs[a].at[mine_idx],
                    send_sem=send_sems.at[COPIES * a + m - 1],
                    recv_sem=recv_sems.at[COPIES * a + m - 1],
                    device_id=(px, py, pc),
                    device_id_type=pl.DeviceIdType.MESH,
                )
            )

    def start():
        for cp in own + copies:
            cp.start()

    def finish():
        for cp in copies:
            cp.wait_recv()
        for cp in copies:
            cp.wait_send()
        for cp in own:
            cp.wait()

    return start, None, finish


def _side_out_shapes(side):
    kind, arrs = side
    return [jax.ShapeDtypeStruct((N_DEV,) + a.shape if kind == "gather" else a.shape, a.dtype) for a in arrs]


def _run_side(side, x_refs, out_refs, sems, nsteps_major, nsteps_minor):
    start, relay, finish = (_gather_phases if side[0] == "gather" else _exchange_phases)(x_refs, out_refs, *sems)
    a, b = pl.program_id(0), pl.program_id(1)
    pl.when(jnp.logical_and(a == 0, b == 0))(start)
    if relay is not None:
        pl.when(jnp.logical_and(a == nsteps_major - 1, b == 0))(relay)
    return lambda: pl.when(jnp.logical_and(a == nsteps_major - 1, b == nsteps_minor - 1))(finish)


def _exchange_call(kind, arrs, name):
    n = len(arrs)

    def body(*refs):
        start, relay, finish = (_gather_phases if kind == "gather" else _exchange_phases)(refs[:n], refs[n : 2 * n], *refs[2 * n :])
        start()
        if relay is not None:
            relay()
        finish()

    return pl.pallas_call(body, name=name, in_specs=[ANY_SPEC] * n, out_specs=[ANY_SPEC] * n,
                          out_shape=_side_out_shapes((kind, arrs)), scratch_shapes=_comm_sems(n))(*arrs)


def _sum_slots(recv, name):
    _, r, c = recv.shape
    tr = _tile(r, (256, 128, 64, 32, 16, 8))

    def body(r_ref, o_ref):
        acc = r_ref[0].astype(F32)
        for k in range(1, N_DEV):
            acc = acc + r_ref[k].astype(F32)
        o_ref[...] = acc

    return pl.pallas_call(
        body,
        name=name,
        grid=(r // tr,),
        in_specs=[pl.BlockSpec((N_DEV, tr, c), lambda i: (0, i, 0))],
        out_specs=pl.BlockSpec((tr, c), lambda i: (i, 0)),
        out_shape=jax.ShapeDtypeStruct((r, c), F32),
        compiler_params=_params(("parallel",)),
    )(recv)


def _adamw(w, g, m, v, name):
    shape = w.shape
    c = shape[-1]
    as2d = lambda a: a.reshape(-1, c)
    w2, g2, m2, v2 = as2d(w), as2d(g), as2d(m), as2d(v)
    r = w2.shape[0]
    tr = _tile(r, (256, 128, 64, 32, 16, 8))
    c1 = 1.0 / (1.0 - ADAM_B1 ** ADAM_STEP)
    c2 = 1.0 / (1.0 - ADAM_B2 ** ADAM_STEP)

    def body(w_ref, g_ref, m_ref, v_ref, d_ref, mo_ref, vo_ref):
        gg = g_ref[...]
        mn = ADAM_B1 * m_ref[...] + (1.0 - ADAM_B1) * gg
        vn = ADAM_B2 * v_ref[...] + (1.0 - ADAM_B2) * (gg * gg)
        d_ref[...] = -ADAM_LR * ((mn * c1) / (jnp.sqrt(vn * c2) + ADAM_EPS) + ADAM_WD * w_ref[...])
        mo_ref[...] = mn
        vo_ref[...] = vn

    spec = pl.BlockSpec((tr, c), lambda i: (i, 0))
    d, mn, vn = pl.pallas_call(
        body,
        name=name,
        grid=(r // tr,),
        in_specs=[spec] * 4,
        out_specs=[spec] * 3,
        out_shape=[jax.ShapeDtypeStruct((r, c), F32)] * 3,
        compiler_params=_params(("parallel",)),
    )(w2, g2, m2, v2)
    return d.reshape(shape), mn.reshape(shape), vn.reshape(shape)


COL_SHARDED = ("attn_w_in", "mla_w_q_b", "mla_w_kv_b", "s5_d", "s5_w_glu", "ffn_w_in", "ple_w")
ROW_SHARDED = ("attn_w_out", "ffn_w_out", "ple_gate_w")
REPLICATED = ("mla_q_norm", "mla_kv_norm", "s5_a_re", "s5_a_im", "s5_log_dt", "s5_b_re", "s5_b_im", "s5_c_re",
              "s5_c_im", "ln1_g", "ln1_b", "ln2_g", "ln2_b")
ATTENTION_WEIGHTS = ("attn_w_in", "mla_w_q_b", "mla_w_kv_b")
BIG_WEIGHTS = ("attn_w_in", "mla_w_q_b", "mla_w_kv_b", "attn_w_out", "s5_w_glu", "ffn_w_in", "ffn_w_out", "ple_w", "ple_gate_w")
WEIGHTS = ("attn_w_in", "mla_q_norm", "mla_w_q_b", "mla_kv_norm", "mla_w_kv_b", "attn_w_out", "s5_a_re", "s5_a_im",
           "s5_log_dt", "s5_b_re", "s5_b_im", "s5_c_re", "s5_c_im", "s5_d", "s5_w_glu", "ln1_g", "ln1_b", "ffn_w_in",
           "ffn_w_out", "ple_w", "ple_gate_w", "ln2_g", "ln2_b")


def _unshard(name, gathered):
    if name in COL_SHARDED:
        full = jnp.moveaxis(gathered, 0, -2)
        return full.reshape(full.shape[:-2] + (full.shape[-2] * full.shape[-1],))
    full = jnp.moveaxis(gathered, 0, 1)
    return full.reshape((full.shape[0], full.shape[1] * full.shape[2]) + full.shape[3:])


def _by_owner(name, grad):
    if name in COL_SHARDED:
        g = grad.reshape(grad.shape[:-1] + (N_DEV, grad.shape[-1] // N_DEV))
        return jnp.moveaxis(g, -2, 0).reshape(N_DEV, -1)
    if name in ROW_SHARDED:
        g = grad.reshape((grad.shape[0], N_DEV, grad.shape[1] // N_DEV) + grad.shape[2:])
        return jnp.moveaxis(g, 1, 0).reshape(N_DEV, -1)
    return jnp.broadcast_to(grad.reshape(1, -1), (N_DEV, grad.size))


def _owner_rows(name, g):
    k, n = g.shape
    if name in COL_SHARDED:
        return g.reshape(k, N_DEV, n // N_DEV).transpose(1, 0, 2)
    assert name in ROW_SHARDED, name
    return g.reshape(N_DEV, k // N_DEV, n)


def _pack(pieces, axis):
    blocks, where, row = [], [], 0
    for p in pieces:
        n = p.shape[axis]
        pad = (-n) % (PACK_COLS * PACK_ROW_TILE)
        if pad:
            shape = list(p.shape)
            shape[axis] = pad
            p = jnp.concatenate([p, jnp.zeros(shape, p.dtype)], axis=axis)
        rows = (n + pad) // PACK_COLS
        blocks.append(p.reshape(p.shape[:axis] + (rows, PACK_COLS)))
        where.append((row, rows, n))
        row += rows
    return jnp.concatenate(blocks, axis=axis), where


def _unpack(buf, axis, where):
    row, rows, n = where
    part = lax.slice_in_dim(buf, row, row + rows, axis=axis)
    return lax.slice_in_dim(part.reshape(part.shape[:axis] + (rows * PACK_COLS,)), 0, n, axis=axis)


def _twice(fn):
    return lambda *a: fn(*a) * 2


def _ffn_block(h, hb, p_i, w_in, w_out, w_ple, w_pg, g, b, alpha, tag):
    t, d = h.shape
    hid = w_out.shape[0]
    cw = _tile(hid, (1408, 512, 256, 128))
    ncb = hid // cw
    gu = _mm(hb, w_in, out_dtype=BF16, name=f"ffn_in_{tag}")
    act = _rowwise(_swiglu_fn, [(gu, cw, 0), (gu, cw, ncb)], [], [(hid, cw, BF16)], name=f"swiglu_{tag}", tq=1024, ncol=ncb)[0]
    f = _mm(act, w_out, name=f"ffn_out_{tag}")
    pw = _mm(p_i, w_ple, name=f"ple_{tag}")
    gate = _mm(hb, w_pg, name=f"ple_gate_{tag}")
    out, outb = _rowwise(_twice(_ln_ffn_fn(alpha)), [_whole(h), _whole(f), _whole(pw), _whole(gate)], [g, b],
                         [(d, d, F32), (d, d, BF16)], name=f"ln2_{tag}", tq=256)
    return out, outb, (h, hb, p_i, gu, act, f, pw, gate)


def _ffn_block_bwd(saved, dout, w_in, w_out, w_pg, g, b, alpha, tag):
    h, hb, p_i, gu, act, f, pw, gate = saved
    t, d = h.shape
    hid = w_out.shape[0]
    cw = _tile(hid, (1408, 512, 256, 128))
    ncb = hid // cw
    dh_a, df, dpw, dgate, dg, db = _rowwise_bwd(
        _ln_ffn_fn(alpha), [_whole(h), _whole(f), _whole(pw), _whole(gate)], [g, b], [dout],
        need=[True] * 4, drow=[(d, F32), (d, BF16), (d, BF16), (d, BF16)], name=f"ln2_bwd_{tag}", tq=128)
    dw_out = _mm(act, df, ta=True, out_dtype=BF16, name=f"ffn_out_dw_{tag}")
    dact = _mm(df, w_out, tb=True, out_dtype=BF16, name=f"ffn_out_dx_{tag}")
    dg_, du_ = _rowwise_bwd(_swiglu_fn, [(gu, cw, 0), (gu, cw, ncb)], [], [[(dact, cw, 0)]], need=[True, True],
                            drow=[(hid, BF16), (hid, BF16)], name=f"swiglu_bwd_{tag}", tq=512, ncol=ncb)
    dgu = jnp.concatenate([dg_, du_], axis=1)
    dw_in = _mm(hb, dgu, ta=True, out_dtype=BF16, name=f"ffn_in_dw_{tag}")
    dh_b = _mm(dgu, w_in, tb=True, name=f"ffn_in_dx_{tag}")
    dw_ple = _mm(p_i, dpw, ta=True, out_dtype=BF16, name=f"ple_dw_{tag}")
    dw_pg = _mm(hb, dgate, ta=True, out_dtype=BF16, name=f"ple_gate_dw_{tag}")
    dh_c = _mm(dgate, w_pg, tb=True, name=f"ple_gate_dx_{tag}")
    return [dh_a, dh_b, dh_c], dict(ffn_w_in=dw_in, ffn_w_out=dw_out, ple_w=dw_ple, ple_gate_w=dw_pg, ln2_g=dg, ln2_b=db)


def kernel(x, p, positions, attn_w_in, mla_q_norm, mla_w_q_b, mla_kv_norm, mla_w_kv_b, attn_w_out, s5_a_re, s5_a_im, s5_log_dt, s5_b_re, s5_b_im, s5_c_re, s5_c_im, s5_d, s5_w_glu, ln1_g, ln1_b, ffn_w_in, ffn_w_out, ple_w, ple_gate_w, ln2_g, ln2_b, loss_target, m_attn_w_in, m_mla_q_norm, m_mla_w_q_b, m_mla_kv_norm, m_mla_w_kv_b, m_attn_w_out, m_s5_a_re, m_s5_a_im, m_s5_log_dt, m_s5_b_re, m_s5_b_im, m_s5_c_re, m_s5_c_im, m_s5_d, m_s5_w_glu, m_ln1_g, m_ln1_b, m_ffn_w_in, m_ffn_w_out, m_ple_w, m_ple_gate_w, m_ln2_g, m_ln2_b, v_attn_w_in, v_mla_q_norm, v_mla_w_q_b, v_mla_kv_norm, v_mla_w_kv_b, v_attn_w_out, v_s5_a_re, v_s5_a_im, v_s5_log_dt, v_s5_b_re, v_s5_b_im, v_s5_c_re, v_s5_c_im, v_s5_d, v_s5_w_glu, v_ln1_g, v_ln1_b, v_ffn_w_in, v_ffn_w_out, v_ple_w, v_ple_gate_w, v_ln2_g, v_ln2_b):
    local = dict(attn_w_in=attn_w_in, mla_q_norm=mla_q_norm, mla_w_q_b=mla_w_q_b, mla_kv_norm=mla_kv_norm,
                 mla_w_kv_b=mla_w_kv_b, attn_w_out=attn_w_out, s5_a_re=s5_a_re, s5_a_im=s5_a_im, s5_log_dt=s5_log_dt,
                 s5_b_re=s5_b_re, s5_b_im=s5_b_im, s5_c_re=s5_c_re, s5_c_im=s5_c_im, s5_d=s5_d, s5_w_glu=s5_w_glu,
                 ln1_g=ln1_g, ln1_b=ln1_b, ffn_w_in=ffn_w_in, ffn_w_out=ffn_w_out, ple_w=ple_w, ple_gate_w=ple_gate_w,
                 ln2_g=ln2_g, ln2_b=ln2_b)
    mom_m = dict(zip(WEIGHTS, (m_attn_w_in, m_mla_q_norm, m_mla_w_q_b, m_mla_kv_norm, m_mla_w_kv_b, m_attn_w_out, m_s5_a_re, m_s5_a_im, m_s5_log_dt, m_s5_b_re, m_s5_b_im, m_s5_c_re, m_s5_c_im, m_s5_d, m_s5_w_glu, m_ln1_g, m_ln1_b, m_ffn_w_in, m_ffn_w_out, m_ple_w, m_ple_gate_w, m_ln2_g, m_ln2_b)))
    mom_v = dict(zip(WEIGHTS, (v_attn_w_in, v_mla_q_norm, v_mla_w_q_b, v_mla_kv_norm, v_mla_w_kv_b, v_attn_w_out, v_s5_a_re, v_s5_a_im, v_s5_log_dt, v_s5_b_re, v_s5_b_im, v_s5_c_re, v_s5_c_im, v_s5_d, v_s5_w_glu, v_ln1_g, v_ln1_b, v_ffn_w_in, v_ffn_w_out, v_ple_w, v_ple_gate_w, v_ln2_g, v_ln2_b)))

    t, d = x.shape[1], x.shape[2]
    depth = ln1_g.shape[0]
    alpha = (2.0 * depth) ** 0.25
    ql, kvl = mla_q_norm.shape[-1], mla_kv_norm.shape[-1]
    nh = mla_w_q_b.shape[-1] * N_DEV // (NOPE_DIM + ROPE_DIM)
    dw = (attn_w_in.shape[-1] * N_DEV - ql - kvl - ROPE_DIM) // 3
    ndh = dw // DIL_HEAD_DIM
    ng = d // S5_GROUP
    assert ql == kvl and (ql + kvl) % LANE == 0 and nh * V_DIM == dw

    sharded = [n for n in WEIGHTS if n in COL_SHARDED or n in ROW_SHARDED]
    def as_sent(n):
        if n == "s5_d":
            return local[n]
        if n == "attn_w_in":
            return local[n][0].T.astype(BF16)
        return local[n].astype(BF16).reshape(-1, local[n].shape[-1])

    full = {}

    def take_weights(names, gathered):
        for n, got in zip(names, gathered):
            if n == "attn_w_in":
                full[n] = got.reshape(N_DEV * local[n].shape[2], d)
            else:
                full[n] = _unshard(n, got.reshape((N_DEV,) + local[n].shape))

    first_w = [n for n in sharded if n in ATTENTION_WEIGHTS]
    later_w = [n for n in sharded if n not in ATTENTION_WEIGHTS]
    take_weights(first_w, _exchange_call("gather", [as_sent(n) for n in first_w], "gather_attn_weights"))

    w_in_t = full["attn_w_in"]
    lat = ql + kvl
    w_lat_t = jnp.concatenate([w_in_t[: lat + ROPE_DIM], jnp.zeros((LANE - ROPE_DIM, d), BF16)], axis=0)
    w_dil_t = w_in_t[lat + ROPE_DIM :]
    wq = full["mla_w_q_b"][0].reshape(ql, nh, NOPE_DIM + ROPE_DIM)
    wq_pe = jnp.pad(wq[:, :, NOPE_DIM:], ((0, 0), (0, 0), (0, LANE - ROPE_DIM)))
    wqp = jnp.concatenate([wq[:, :, :NOPE_DIM].reshape(ql, nh * LANE), wq_pe.reshape(ql, nh * LANE)], axis=1)
    wkv = full["mla_w_kv_b"][0].reshape(kvl, nh, NOPE_DIM + V_DIM)
    wkvp = jnp.concatenate([wkv[:, :, :NOPE_DIM].reshape(kvl, nh * LANE), wkv[:, :, NOPE_DIM:].reshape(kvl, nh * LANE)], axis=1)

    h0 = x[0]
    h0b = h0.astype(BF16)
    pb = p.astype(BF16)
    target = loss_target[0]
    pos = positions.reshape(t, 1)
    grads = {}

    z = _mm(h0b, w_lat_t, tb=True, name="attn_in_lat")
    zd = _mm(h0b, w_dil_t, tb=True, out_dtype=BF16, name="attn_in_dil")
    gq, gk = mla_q_norm.reshape(1, ql), mla_kv_norm.reshape(1, kvl)
    qn, kvn = _rowwise(_rms_fn, [(z, ql, 0), (z, kvl, 1)], [gq, gk], [(ql, ql, BF16), (kvl, kvl, BF16)], name="rms", tq=256)
    qf = _mm(qn, wqp, name="q_up")
    kvf = _mm(kvn, wkvp, out_dtype=BF16, name="kv_up")
    cos, sin = _rope_tables(pos, "rope_tables")
    pe_cb = lat // LANE
    qpe, kpe = _rowwise(_rope_fwd_fn(nh), [(qf, nh * LANE, 1), (z, LANE, pe_cb), _whole(cos), _whole(sin)], [],
                        [(nh * LANE, nh * LANE, BF16), (LANE, LANE, BF16)], name="rope", tq=256)
    out_a, lse_a, later_gathered = _mla_fwd(qf, qpe, kvf, kpe, nh, "mla_fwd", side=("gather", [as_sent(n) for n in later_w]))
    take_weights(later_w, later_gathered)
    w_out = full["attn_w_out"][0]
    w_glu = full["s5_w_glu"][0]
    d_skip = full["s5_d"]

    def to_classes(a, dil):
        if dil == 1:
            return a
        return a.reshape(t // dil, dil, a.shape[1]).transpose(1, 0, 2).reshape(t, a.shape[1])

    def from_classes(a, dil):
        if dil == 1:
            return a
        return a.reshape(dil, t // dil, a.shape[1]).transpose(1, 0, 2).reshape(t, a.shape[1])

    band = []
    for window, dil in DIL_BRANCHES:
        assert window // dil == BLK
        zc = to_classes(zd, dil)
        o_c, l_c = _band_fwd(zc, ndh, dil, f"band_fwd_d{dil}")
        band.append((dil, zc, o_c, l_c, from_classes(o_c, dil), from_classes(l_c, dil)))
    merge_rows = [_whole(b[4]) for b in band] + [_whole(b[5]) for b in band]
    out_b = _rowwise(_merge_fn, merge_rows, [], [(dw, dw, BF16)], name="merge", tq=512)[0]
    att = jnp.concatenate([out_a, out_b], axis=1)
    mix0 = _mm(att, w_out, name="attn_out")
    g1, b1 = ln1_g[0:1], ln1_b[0:1]
    h1, h1b = _rowwise(_twice(_ln_mix_fn(alpha)), [_whole(h0), _whole(mix0)], [g1, b1], [(d, d, F32), (d, d, BF16)],
                       name="ln1_l0", tq=256)
    h2, _, saved_f0 = _ffn_block(h1, h1b, pb[0, 0], full["ffn_w_in"][0], full["ffn_w_out"][0], full["ple_w"][0],
                                 full["ple_gate_w"][0], ln2_g[0:1], ln2_b[0:1], alpha, "l0")

    prm = [s5_a_re[0].reshape(ng, 1, S5_STATE), s5_a_im[0].reshape(ng, 1, S5_STATE), s5_log_dt[0].reshape(ng, 1, 1),
           s5_b_re[0].transpose(0, 2, 1), s5_b_im[0].transpose(0, 2, 1), s5_c_re[0], s5_c_im[0]]
    u = _to_groups(h2, "s5_regroup_u")
    e_re, e_im, lam_re, lam_im = _s5_local(prm, u, "s5_local")
    lam_re_c, lam_im_c = lam_re.reshape(1, ng * S5_STATE), lam_im.reshape(1, ng * S5_STATE)
    sp_re_c, sp_im_c = _s5_carry(_states_to_cols(e_re), _states_to_cols(e_im), lam_re_c, lam_im_c, "s5_carry")
    sp_re, sp_im = _cols_to_states(sp_re_c, ng), _cols_to_states(sp_im_c, ng)
    ys = _from_groups(_s5_out(prm, u, sp_re, sp_im, "s5_out"), "s5_ungroup_y")
    z5 = _rowwise(_gelu_fn, [_whole(ys), _whole(h2)], [d_skip], [(d, d, BF16)], name="gelu", tq=512)[0]
    vg = _mm(z5, w_glu, name="glu_in")
    g3, b3 = ln1_g[1:2], ln1_b[1:2]
    h3, h3b = _rowwise(_twice(_ln_glu_fn(alpha)), [_whole(h2), _whole(vg)], [g3, b3],
                       [(d, d, F32), (d, d, BF16)], name="ln1_l1", tq=256)
    h4, _, saved_f1 = _ffn_block(h3, h3b, pb[1, 0], full["ffn_w_in"][1], full["ffn_w_out"][1], full["ple_w"][1],
                                 full["ple_gate_w"][1], ln2_g[1:2], ln2_b[1:2], alpha, "l1")

    dh4, loss_acc = _loss_kernel(h4, target, "loss")
    loss = lax.psum(loss_acc[0, 0], AXES)

    dh3, gf1 = _ffn_block_bwd(saved_f1, [_whole(dh4)], full["ffn_w_in"][1], full["ffn_w_out"][1],
                              full["ple_gate_w"][1], ln2_g[1:2], ln2_b[1:2], alpha, "l1")
    dh2_a, dvg, dg3, db3 = _rowwise_bwd(
        _ln_glu_fn(alpha), [_whole(h2), _whole(vg)], [g3, b3], [[_whole(a) for a in dh3]],
        need=[True] * 2, drow=[(d, F32), (2 * d, BF16)], name="ln1_bwd_l1", tq=128)
    dw_glu = _mm(z5, dvg, ta=True, out_dtype=BF16, name="glu_dw")
    dz5 = _mm(dvg, w_glu, tb=True, name="glu_dx")
    dys, dh2_b, dd = _rowwise_bwd(_gelu_fn, [_whole(ys), _whole(h2)], [d_skip], [[_whole(dz5)]], need=[True, True],
                                  drow=[(d, F32), (d, F32)], name="gelu_bwd", tq=256)
    grads["s5_d"] = dd
    dy = _to_groups(dys, "s5_regroup_dy")
    dsp_re, dsp_im = _s5_bwd_state(prm, dy, "s5_bwd_state")
    g_re_c, g_im_c, dlam_re_c, dlam_im_c = _s5_carry_bwd(_states_to_cols(dsp_re), _states_to_cols(dsp_im), sp_re_c, sp_im_c,
                                                         lam_re_c, lam_im_c, "s5_carry_bwd")
    s5g = _s5_bwd_main(prm, u, sp_re, sp_im, dy, _cols_to_states(g_re_c, ng), _cols_to_states(g_im_c, ng),
                       dlam_re_c.reshape(ng, 1, S5_STATE), dlam_im_c.reshape(ng, 1, S5_STATE), "s5_bwd_main")
    grads["s5_a_re"] = s5g[0].reshape(s5_a_re.shape)
    grads["s5_a_im"] = s5g[1].reshape(s5_a_im.shape)
    grads["s5_log_dt"] = s5g[2].reshape(s5_log_dt.shape)
    grads["s5_b_re"] = s5g[3].transpose(0, 2, 1)[None]
    grads["s5_b_im"] = s5g[4].transpose(0, 2, 1)[None]
    grads["s5_c_re"] = s5g[5][None]
    grads["s5_c_im"] = s5g[6][None]
    dh2_c = _from_groups(s5g[7], "s5_ungroup_du")

    dh1, gf0 = _ffn_block_bwd(saved_f0, [_whole(dh2_a), _whole(dh2_b), _whole(dh2_c)], full["ffn_w_in"][0],
                              full["ffn_w_out"][0], full["ple_gate_w"][0], ln2_g[0:1], ln2_b[0:1], alpha, "l0")
    for k in ("ln2_g", "ln2_b"):
        grads[k] = jnp.concatenate([gf0[k], gf1[k]])

    shards = {}

    def take(items, received):
        for (key, _), recv in zip(items, received):
            shards[key] = _sum_slots(recv, f"sum_grads_{key[0]}_{key[1]}")

    ffn_names = ("ffn_w_in", "ffn_w_out", "ple_w", "ple_gate_w")
    items_l1 = [(("s5_w_glu", 0), _owner_rows("s5_w_glu", dw_glu))] + [((k, 1), _owner_rows(k, gf1[k])) for k in ffn_names]
    items_l0 = [((k, 0), _owner_rows(k, gf0[k])) for k in ffn_names]
    dh0_a, dmix, dg1, db1 = _rowwise_bwd(_ln_mix_fn(alpha), [_whole(h0), _whole(mix0)], [g1, b1], [[_whole(a) for a in dh1]],
                                         need=[True, True], drow=[(d, F32), (d, BF16)], name="ln1_bwd_l0", tq=128)
    grads["ln1_g"] = jnp.concatenate([dg1, dg3])
    grads["ln1_b"] = jnp.concatenate([db1, db3])
    dw_out = _mm(att, dmix, ta=True, out_dtype=BF16, name="attn_out_dw")
    datt = _mm(dmix, w_out, tb=True, name="attn_out_dx")
    items_l0.append((("attn_w_out", 0), _owner_rows("attn_w_out", dw_out)))
    small_early = [n for n in WEIGHTS if n not in BIG_WEIGHTS and n not in ("mla_q_norm", "mla_kv_norm")]
    send_small, where_small = _pack([_by_owner(n, grads[n]).astype(F32) for n in small_early], 1)

    dmerge = _rowwise_bwd(_merge_fn, merge_rows, [], [[(datt, dw, 1)]], need=[True] * 6, drow=[(dw, F32)] * 6,
                          name="merge_bwd", tq=256)
    dq_s = dk_s = dv_s = None
    for k, (dil, zc, o_c, l_c, _, _) in enumerate(band):
        do_c, dl_c = to_classes(dmerge[k], dil), to_classes(dmerge[3 + k], dil)
        dq_c, dk_c, dv_c = _band_bwd(zc, o_c, l_c, do_c, dl_c, ndh, dil, f"band_bwd_d{dil}")
        dq_n, dk_n, dv_n = from_classes(dq_c, dil), from_classes(dk_c, dil), from_classes(dv_c, dil)
        dq_s = dq_n if dq_s is None else dq_s + dq_n
        dk_s = dk_n if dk_s is None else dk_s + dk_n
        dv_s = dv_n if dv_s is None else dv_s + dv_n

    dqn, dqp, delta, recv_l1 = _mla_bwd_q(qf, qpe, kvf, kpe, datt, out_a, lse_a, nh, "mla_bwd_q",
                                          side=("exchange", [rows for _, rows in items_l1]))
    take(items_l1, recv_l1)
    to_row = lambda a: a[:, ::LANE].T.reshape(nh, 1, t)
    dkn, dkp, dv, recv_l0 = _mla_bwd_kv(qf[:, : nh * LANE].astype(BF16), qpe, kvf, kpe, datt[:, : nh * LANE].astype(BF16),
                                        to_row(lse_a), to_row(delta), nh, "mla_bwd_kv",
                                        side=("exchange", [rows for _, rows in items_l0] + [send_small]))
    take(items_l0, recv_l0[:-1])
    sum_small = _sum_slots(recv_l0[-1], "sum_grads_small")
    dq_pe, dk_pe = _rowwise(_rope_bwd_fn(nh), [_whole(dqp), _whole(dkp), _whole(cos), _whole(sin)], [],
                            [(nh * LANE, nh * LANE, BF16), (LANE, LANE, BF16)], name="rope_bwd", tq=256)
    dqf = jnp.concatenate([dqn.astype(BF16), dq_pe], axis=1)
    dkvf = jnp.concatenate([dkn, dv], axis=1).astype(BF16)
    dwqp = _mm(qn, dqf, ta=True, name="q_up_dw")
    dqn_in = _mm(dqf, wqp, tb=True, name="q_up_dx")
    dwkvp = _mm(kvn, dkvf, ta=True, name="kv_up_dw")
    dkvn_in = _mm(dkvf, wkvp, tb=True, name="kv_up_dx")
    dql, dkvl, dgq, dgk = _rowwise_bwd(_rms_fn, [(z, ql, 0), (z, kvl, 1)], [gq, gk], [[_whole(dqn_in)], [_whole(dkvn_in)]],
                                       need=[True, True], drow=[(ql, BF16), (kvl, BF16)], name="rms_bwd", tq=256)
    grads["mla_q_norm"], grads["mla_kv_norm"] = dgq, dgk
    dz_lat = jnp.concatenate([dql, dkvl, dk_pe], axis=1)
    dz_dil = jnp.concatenate([dq_s, dk_s, dv_s], axis=1).astype(BF16)
    dw_lat_t = _mm(dz_lat, h0b, ta=True, out_dtype=BF16, name="attn_in_lat_dw")
    dw_dil_t = _mm(dz_dil, h0b, ta=True, out_dtype=BF16, name="attn_in_dil_dw")
    dh0_b = _mm(dz_lat, w_lat_t, name="attn_in_lat_dx")
    dh0_c = _mm(dz_dil, w_dil_t, name="attn_in_dil_dx")
    grad_x = _addn([dh0_a, dh0_b, dh0_c], "grad_x")[None]
    dw_in_t = jnp.concatenate([dw_lat_t[: lat + ROPE_DIM], dw_dil_t], axis=0)
    dwq_n = dwqp[:, : nh * LANE].reshape(ql, nh, NOPE_DIM)
    dwq_r = dwqp[:, nh * LANE :].reshape(ql, nh, LANE)[:, :, :ROPE_DIM]
    dwq = jnp.concatenate([dwq_n, dwq_r], axis=2).reshape(ql, nh * (NOPE_DIM + ROPE_DIM))
    dwkv_k = dwkvp[:, : nh * LANE].reshape(kvl, nh, NOPE_DIM)
    dwkv_v = dwkvp[:, nh * LANE :].reshape(kvl, nh, V_DIM)
    dwkv = jnp.concatenate([dwkv_k, dwkv_v], axis=2).reshape(kvl, nh * (NOPE_DIM + V_DIM))

    items_att = [(("attn_w_in", 0), dw_in_t.reshape(N_DEV, -1, PACK_COLS)),
                 (("mla_w_q_b", 0), _owner_rows("mla_w_q_b", dwq.astype(BF16))),
                 (("mla_w_kv_b", 0), _owner_rows("mla_w_kv_b", dwkv.astype(BF16)))]
    small_late = ["mla_q_norm", "mla_kv_norm"]
    send_late, where_late = _pack([_by_owner(n, grads[n]).astype(F32) for n in small_late], 1)
    received = _exchange_call("exchange", [rows for _, rows in items_att] + [send_late], "exchange_grads_last")
    take(items_att, received[:-1])
    sum_late = _sum_slots(received[-1], "sum_grads_norms")

    g_out, d_out, m_out, v_out = [], [], [], []
    for n in WEIGHTS:
        if n == "attn_w_in":
            g = shards[(n, 0)].reshape(local[n].shape[2], d).T[None]
        elif n in BIG_WEIGHTS:
            g = jnp.stack([shards[(n, layer)] for layer in range(local[n].shape[0])]).reshape(local[n].shape)
        elif n in small_late:
            g = _unpack(sum_late, 0, where_late[small_late.index(n)]).reshape(local[n].shape)
        else:
            g = _unpack(sum_small, 0, where_small[small_early.index(n)]).reshape(local[n].shape)
        dlt, mn, vn = _adamw(local[n], g, mom_m[n], mom_v[n], f"adamw_{n}")
        g_out.append(g)
        d_out.append(dlt)
        m_out.append(mn)
        v_out.append(vn)
    return (loss, grad_x, *g_out, *d_out, *m_out, *v_out)
```

```python
import functools
import math

import numpy as np
import jax
import jax.numpy as jnp
from jax import lax
from jax.experimental import pallas as pl
from jax.experimental.pallas import tpu as pltpu

F32 = jnp.float32
BF16 = jnp.bfloat16

NOPE_DIM, ROPE_DIM, V_DIM = 128, 64, 128
DIL_HEAD_DIM = 128
DIL_BRANCHES = ((128, 1), (512, 4), (2048, 16))
BLK = 128
LANE = 128
ROPE_THETA = 10000.0
S5_GROUP, S5_STATE = 16, 64
S5_CHUNK = 32
NEG = -1e30
ADAM_LR, ADAM_B1, ADAM_B2, ADAM_EPS, ADAM_WD, ADAM_STEP = 0.001, 0.9, 0.999, 1e-08, 0.01, 10
N_DEV = 8
AXES = ("x", "y", "c")
VMEM_LIMIT = 56 * 1024 * 1024
MM_VMEM_BUDGET = 40 * 1024 * 1024
PACK_COLS = 1024
PACK_ROW_TILE = 16
HI = lax.Precision.HIGH


def _tile(n, cands):
    for c in cands:
        if n % c == 0:
            return c
    return n


def _params(sem=None):
    return pltpu.CompilerParams(dimension_semantics=sem, vmem_limit_bytes=VMEM_LIMIT)


def _dot(a, b, dims, prec=None):
    return lax.dot_general(a, b, (dims, ((), ())), preferred_element_type=F32, precision=prec)


NN = ((1,), (0,))
NT = ((1,), (1,))
TN = ((0,), (0,))


def _mm(a, b, *, ta=False, tb=False, out_dtype=F32, name):
    (k, m) = a.shape if ta else a.shape[::-1]
    (n, k2) = b.shape if tb else b.shape[::-1]
    assert k == k2, (a.shape, b.shape, ta, tb)
    tm = _tile(m, (1024, 512, 256, 128))
    tn = _tile(n, (1024, 512, 384, 256, 128))
    sa, sb, so = a.dtype.itemsize, b.dtype.itemsize, jnp.dtype(out_dtype).itemsize

    def fits(tk):
        return 2 * (tm * tk * sa + tk * tn * sb) + 2 * tm * tn * so + 4 * tm * tn <= MM_VMEM_BUDGET

    tk = next((c for c in (2816, 2048, 1536, 1408, 1152, 1024, 768, 512, 384, 256, 128) if k % c == 0 and fits(c)), k)
    nk = k // tk
    dims = (((0,) if ta else (1,)), ((1,) if tb else (0,)))

    def body(a_ref, b_ref, o_ref, *acc):
        part = _dot(a_ref[...].astype(BF16), b_ref[...].astype(BF16), dims)
        if nk == 1:
            o_ref[...] = part.astype(o_ref.dtype)
            return
        (acc_ref,) = acc
        kk = pl.program_id(2)

        @pl.when(kk == 0)
        def _():
            acc_ref[...] = part

        @pl.when(kk > 0)
        def _():
            acc_ref[...] += part

        @pl.when(kk == nk - 1)
        def _():
            o_ref[...] = acc_ref[...].astype(o_ref.dtype)

    a_spec = pl.BlockSpec((tk, tm), lambda i, j, kk: (kk, i)) if ta else pl.BlockSpec((tm, tk), lambda i, j, kk: (i, kk))
    b_spec = pl.BlockSpec((tn, tk), lambda i, j, kk: (j, kk)) if tb else pl.BlockSpec((tk, tn), lambda i, j, kk: (kk, j))
    return pl.pallas_call(
        body,
        name=name,
        grid=(m // tm, n // tn, nk),
        in_specs=[a_spec, b_spec],
        out_specs=pl.BlockSpec((tm, tn), lambda i, j, kk: (i, j)),
        out_shape=jax.ShapeDtypeStruct((m, n), out_dtype),
        scratch_shapes=[pltpu.VMEM((tm, tn), F32)] if nk > 1 else [],
        compiler_params=_params(("parallel", "parallel", "arbitrary")),
    )(a, b)


def _row_spec(spec, tq):
    _, w, cb = spec
    return pl.BlockSpec((tq, w), lambda i, j: (i, cb + j))


def _full_spec(p):
    return pl.BlockSpec(p.shape, lambda i, j: (0,) * p.ndim)


def _rowwise(fn, rows, pars, outs, *, name, tq, ncol=1):
    t = rows[0][0].shape[0]
    tq = _tile(t, (tq, 128, 64, 32, 16, 8))
    nr, npar = len(rows), len(pars)

    def body(*refs):
        vals = [r[...].astype(F32) for r in refs[: nr + npar]]
        res = fn(*vals)
        for o, r in zip(refs[nr + npar :], res):
            o[...] = r.astype(o.dtype)

    res = pl.pallas_call(
        body,
        name=name,
        grid=(t // tq, ncol),
        in_specs=[_row_spec(s, tq) for s in rows] + [_full_spec(p) for p in pars],
        out_specs=[pl.BlockSpec((tq, w), lambda i, j: (i, j)) for (_, w, _) in outs],
        out_shape=[jax.ShapeDtypeStruct((t, wt), dt) for (wt, _, dt) in outs],
        compiler_params=_params(("parallel", "parallel")),
    )(*[s[0] for s in rows], *pars)
    return res


def _rowwise_bwd(fn, rows, pars, cots, *, need, drow, name, tq, ncol=1):
    t = rows[0][0].shape[0]
    tq = _tile(t, (tq, 128, 64, 32, 16, 8))
    nr, npar = len(rows), len(pars)
    assert ncol == 1 or npar == 0
    flat_cots = [s for c in cots for s in c]
    ncot = len(flat_cots)
    want = [k for k in range(nr) if need[k]]

    def body(*refs):
        vals = [r[...].astype(F32) for r in refs[: nr + npar]]
        cref = refs[nr + npar : nr + npar + ncot]
        oref = refs[nr + npar + ncot :]
        cvals, pos = [], 0
        for c in cots:
            acc = cref[pos][...].astype(F32)
            for q in range(1, len(c)):
                acc = acc + cref[pos + q][...].astype(F32)
            cvals.append(acc)
            pos += len(c)

        def closed(*diff):
            full = list(vals)
            for k, dv in zip(want + list(range(nr, nr + npar)), diff):
                full[k] = dv
            return tuple(fn(*full))

        diff_in = [vals[k] for k in want] + vals[nr:]
        _, vjp = jax.vjp(closed, *diff_in)
        grads = vjp(tuple(cvals))
        for q in range(len(want)):
            oref[q][...] = grads[q].astype(oref[q].dtype)
        if npar:
            first = pl.program_id(0) == 0

            @pl.when(first)
            def _():
                for q in range(npar):
                    oref[len(want) + q][...] = jnp.zeros_like(oref[len(want) + q])

            for q in range(npar):
                oref[len(want) + q][...] += grads[len(want) + q]

    out_specs = [pl.BlockSpec((tq, rows[k][1]), lambda i, j, cb=rows[k][2]: (i, j)) for k in want]
    out_specs += [_full_spec(p) for p in pars]
    out_shape = [jax.ShapeDtypeStruct((t, wt), dt) for (wt, dt) in drow]
    out_shape += [jax.ShapeDtypeStruct(p.shape, F32) for p in pars]
    return pl.pallas_call(
        body,
        name=name,
        grid=(t // tq, ncol),
        in_specs=[_row_spec(s, tq) for s in rows] + [_full_spec(p) for p in pars] + [_row_spec(s, tq) for s in flat_cots],
        out_specs=out_specs,
        out_shape=out_shape,
        compiler_params=_params(("arbitrary", "arbitrary") if npar else ("parallel", "parallel")),
    )(*[s[0] for s in rows], *pars, *[s[0] for s in flat_cots])


def _whole(a, w=None):
    return (a, a.shape[1] if w is None else w, 0)


def _rms_fn(ql, kvl, gq, gk):
    def one(x, g):
        return x * lax.rsqrt(jnp.mean(x * x, -1, keepdims=True) + 1e-6) * g

    return one(ql, gq), one(kvl, gk)


def _layernorm(s, g, b):
    mu = jnp.mean(s, -1, keepdims=True)
    d = s - mu
    var = jnp.mean(d * d, -1, keepdims=True)
    return d * lax.rsqrt(var + 1e-5) * g + b


def _ln_mix_fn(alpha):
    def fn(h, mix, g, b):
        return (_layernorm(alpha * h + mix, g, b),)

    return fn


def _ln_glu_fn(alpha):
    def fn(h, vg, g, b):
        d = h.shape[1]
        return (_layernorm(alpha * h + vg[:, :d] * jax.nn.sigmoid(vg[:, d:]), g, b),)

    return fn


def _ln_ffn_fn(alpha):
    def fn(h, f, pw, gate, g, b):
        return (_layernorm(alpha * h + f + pw * jax.nn.sigmoid(gate), g, b),)

    return fn


def _swiglu_fn(g, u):
    return (jax.nn.silu(g) * u,)


def _gelu_fn(ys, h, d):
    return (jax.nn.gelu(ys + d * h),)


def _merge_fn(o1, o2, o3, l1, l2, l3):
    m = jnp.maximum(jnp.maximum(l1, l2), l3)
    e1, e2, e3 = jnp.exp(l1 - m), jnp.exp(l2 - m), jnp.exp(l3 - m)
    return ((e1 * o1 + e2 * o2 + e3 * o3) / (e1 + e2 + e3),)


def _swap_halves(t):
    w = t.shape[1]
    lane = lax.broadcasted_iota(jnp.int32, t.shape, 1) % LANE
    up = jnp.where(lane < ROPE_DIM, pltpu.roll(t, ROPE_DIM // 2, 1), 0.0)
    return jnp.where(lane < ROPE_DIM // 2, pltpu.roll(t, w - ROPE_DIM // 2, 1), up)


def _rope_fwd_fn(nh):
    def fn(qpe, kpe, cos, sin):
        cq, sq = jnp.tile(cos, (1, nh)), jnp.tile(sin, (1, nh))
        return qpe * cq + _swap_halves(qpe) * sq, kpe * cos + _swap_halves(kpe) * sin

    return fn


def _rope_bwd_fn(nh):
    def fn(dq, dk_heads, cos, sin):
        cq, sq = jnp.tile(cos, (1, nh)), jnp.tile(sin, (1, nh))
        dk = dk_heads[:, :LANE]
        for h in range(1, nh):
            dk = dk + dk_heads[:, h * LANE : (h + 1) * LANE]
        return dq * cq + _swap_halves(dq * sq), dk * cos + _swap_halves(dk * sin)

    return fn


def _rope_tables(positions, name):
    t = positions.shape[0]
    tq = _tile(t, (512, 128, 8))
    half = ROPE_DIM // 2

    def body(p_ref, c_ref, s_ref):
        lane = lax.broadcasted_iota(jnp.int32, (tq, LANE), 1)
        idx = (lane % half).astype(F32)
        inv_freq = jnp.exp(idx * (-math.log(ROPE_THETA) / half))
        ang = p_ref[...].astype(F32) * inv_freq
        live = lane < ROPE_DIM
        c_ref[...] = jnp.where(live, jnp.cos(ang), 0.0)
        s_ref[...] = jnp.where(live, jnp.where(lane < half, -jnp.sin(ang), jnp.sin(ang)), 0.0)

    return pl.pallas_call(
        body,
        name=name,
        grid=(t // tq,),
        in_specs=[pl.BlockSpec((tq, 1), lambda i: (i, 0))],
        out_specs=[pl.BlockSpec((tq, LANE), lambda i: (i, 0))] * 2,
        out_shape=[jax.ShapeDtypeStruct((t, LANE), F32)] * 2,
        compiler_params=_params(("parallel",)),
    )(positions)


def _loss_kernel(y, target, name):
    t, d = y.shape
    tq = _tile(t, (256, 128, 8))

    def body(y_ref, t_ref, dy_ref, l_ref):
        @pl.when(pl.program_id(0) == 0)
        def _():
            l_ref[...] = jnp.zeros_like(l_ref)

        e = y_ref[...] - t_ref[...]
        dy_ref[...] = e * (1.0 / d)
        l_ref[...] += jnp.sum(e * e) * (0.5 / d)

    return pl.pallas_call(
        body,
        name=name,
        grid=(t // tq,),
        in_specs=[pl.BlockSpec((tq, d), lambda i: (i, 0))] * 2,
        out_specs=[pl.BlockSpec((tq, d), lambda i: (i, 0)), pl.BlockSpec((8, LANE), lambda i: (0, 0))],
        out_shape=[jax.ShapeDtypeStruct((t, d), F32), jax.ShapeDtypeStruct((8, LANE), F32)],
        compiler_params=_params(("arbitrary",)),
    )(y, target)


def _addn(arrs, name):
    w = arrs[0].shape[1]
    return _rowwise(lambda *v: (functools.reduce(lambda p, q: p + q, v),), [_whole(a) for a in arrs], [], [(w, w, F32)],
                    name=name, tq=256)[0]


def _mla_tiles(t):
    tq = _tile(t, (512, 256, 128))
    return tq, t // tq


def _call_with_side(body, side, n_in, grid, *, name, in_specs, out_specs, out_shape):
    if side is None:
        return pl.pallas_call(body, name=name, grid=grid, in_specs=in_specs, out_specs=out_specs, out_shape=out_shape,
                              compiler_params=_params(("parallel", "parallel")))
    n_out, n = len(out_specs), len(side[1])

    def wrapped(*refs):
        ins, side_ins = refs[:n_in], refs[n_in : n_in + n]
        outs, side_outs = refs[n_in + n : n_in + n + n_out], refs[n_in + n + n_out : n_in + 2 * n + n_out]
        done = _run_side(side, side_ins, side_outs, refs[n_in + 2 * n + n_out :], grid[0], grid[1])
        body(*ins, *outs)
        done()

    call = pl.pallas_call(wrapped, name=name, grid=grid, in_specs=in_specs + [ANY_SPEC] * n,
                          out_specs=out_specs + [ANY_SPEC] * n, out_shape=out_shape + _side_out_shapes(side),
                          scratch_shapes=_comm_sems(n), compiler_params=_params(("arbitrary", "arbitrary")))

    def run(*args):
        res = call(*args, *side[1])
        return (*res[:n_out], list(res[n_out:]))

    return run


def _mla_fwd(qf, qpe, kvf, kpe, nh, name, side=None):
    t = qf.shape[0]
    tq, nq = _mla_tiles(t)
    scale = (NOPE_DIM + ROPE_DIM) ** -0.5

    def body(qn_ref, qp_ref, kn_ref, kp_ref, v_ref, o_ref, lse_ref):
        i = pl.program_id(1)
        qn = qn_ref[...].astype(BF16)
        qp = qp_ref[...]

        def step(j, carry, masked):
            m, l, acc = carry
            ks = pl.ds(pl.multiple_of(j * tq, tq), tq)
            s = (_dot(qn, kn_ref[ks, :], NT) + _dot(qp, kp_ref[ks, :], NT)) * scale
            if masked:
                row = lax.broadcasted_iota(jnp.int32, (tq, tq), 0)
                col = lax.broadcasted_iota(jnp.int32, (tq, tq), 1)
                s = jnp.where(col <= row, s, NEG)
            m_new = jnp.maximum(m, jnp.max(s, -1, keepdims=True))
            p = jnp.exp(s - m_new)
            a = jnp.exp(m - m_new)
            return m_new, a * l + jnp.sum(p, -1, keepdims=True), a * acc + _dot(p.astype(BF16), v_ref[ks, :], NN)

        init = (jnp.full((tq, 1), NEG, F32), jnp.zeros((tq, 1), F32), jnp.zeros((tq, V_DIM), F32))
        carry = lax.fori_loop(0, i, lambda j, c: step(j, c, False), init)
        m, l, acc = step(i, carry, True)
        o_ref[...] = (acc / l).astype(o_ref.dtype)
        lse_ref[...] = jnp.broadcast_to(m + jnp.log(l), (tq, LANE))

    blk = lambda h, i: (i, h)
    return _call_with_side(
        body, side, 5, (nh, nq),
        name=name,
        in_specs=[
            pl.BlockSpec((tq, LANE), blk),
            pl.BlockSpec((tq, LANE), blk),
            pl.BlockSpec((t, LANE), lambda h, i: (0, h)),
            pl.BlockSpec((t, LANE), lambda h, i: (0, 0)),
            pl.BlockSpec((t, LANE), lambda h, i: (0, nh + h)),
        ],
        out_specs=[pl.BlockSpec((tq, LANE), blk), pl.BlockSpec((tq, LANE), blk)],
        out_shape=[jax.ShapeDtypeStruct((t, nh * LANE), BF16), jax.ShapeDtypeStruct((t, nh * LANE), F32)],
    )(qf, qpe, kvf, kpe, kvf)


def _mla_bwd_q(qf, qpe, kvf, kpe, do, o, lse, nh, name, side=None):
    t = qf.shape[0]
    tq, nq = _mla_tiles(t)
    scale = (NOPE_DIM + ROPE_DIM) ** -0.5

    def body(qn_ref, qp_ref, kn_ref, kp_ref, v_ref, do_ref, o_ref, lse_ref, dqn_ref, dqp_ref, dl_ref):
        i = pl.program_id(1)
        qn = qn_ref[...].astype(BF16)
        qp = qp_ref[...]
        dof = do_ref[...].astype(F32)
        dob = dof.astype(BF16)
        delta = jnp.sum(dof * o_ref[...].astype(F32), -1, keepdims=True)
        lse1 = lse_ref[:, :1]

        def step(j, carry, masked):
            dqn, dqp = carry
            ks = pl.ds(pl.multiple_of(j * tq, tq), tq)
            kn, kp = kn_ref[ks, :], kp_ref[ks, :]
            s = (_dot(qn, kn, NT) + _dot(qp, kp, NT)) * scale
            p = jnp.exp(s - lse1)
            if masked:
                row = lax.broadcasted_iota(jnp.int32, (tq, tq), 0)
                col = lax.broadcasted_iota(jnp.int32, (tq, tq), 1)
                p = jnp.where(col <= row, p, 0.0)
            dp = _dot(dob, v_ref[ks, :], NT)
            ds = (p * (dp - delta) * scale).astype(BF16)
            return dqn + _dot(ds, kn, NN), dqp + _dot(ds, kp, NN)

        init = (jnp.zeros((tq, LANE), F32), jnp.zeros((tq, LANE), F32))
        carry = lax.fori_loop(0, i, lambda j, c: step(j, c, False), init)
        dqn, dqp = step(i, carry, True)
        dqn_ref[...] = dqn
        dqp_ref[...] = dqp
        dl_ref[...] = jnp.broadcast_to(delta, (tq, LANE))

    blk = lambda h, i: (i, h)
    bs = pl.BlockSpec((tq, LANE), blk)
    return _call_with_side(
        body, side, 8, (nh, nq),
        name=name,
        in_specs=[
            bs,
            bs,
            pl.BlockSpec((t, LANE), lambda h, i: (0, h)),
            pl.BlockSpec((t, LANE), lambda h, i: (0, 0)),
            pl.BlockSpec((t, LANE), lambda h, i: (0, nh + h)),
            bs,
            bs,
            bs,
        ],
        out_specs=[bs, bs, bs],
        out_shape=[jax.ShapeDtypeStruct((t, nh * LANE), F32)] * 3,
    )(qf, qpe, kvf, kpe, kvf, do, o, lse)


def _mla_bwd_kv(qn16, qpe, kvf, kpe, do16, lse_row, delta_row, nh, name, side=None):
    t = qn16.shape[0]
    tq, nq = _mla_tiles(t)
    scale = (NOPE_DIM + ROPE_DIM) ** -0.5

    def body(kn_ref, kp_ref, v_ref, qn_ref, qp_ref, do_ref, lse_ref, dl_ref, dkn_ref, dkp_ref, dv_ref):
        j = pl.program_id(1)
        kn, kp, v = kn_ref[...], kp_ref[...], v_ref[...]

        def step(i, carry, masked):
            dkn, dkp, dv = carry
            qs = pl.ds(pl.multiple_of(i * tq, tq), tq)
            qn, qp, dob = qn_ref[qs, :], qp_ref[qs, :], do_ref[qs, :]
            st = (_dot(kn, qn, NT) + _dot(kp, qp, NT)) * scale
            pt = jnp.exp(st - lse_ref[0, :, qs])
            if masked:
                key = lax.broadcasted_iota(jnp.int32, (tq, tq), 0)
                qry = lax.broadcasted_iota(jnp.int32, (tq, tq), 1)
                pt = jnp.where(key <= qry, pt, 0.0)
            dv = dv + _dot(pt.astype(BF16), dob, NN)
            dpt = _dot(v, dob, NT)
            dst = (pt * (dpt - dl_ref[0, :, qs]) * scale).astype(BF16)
            return dkn + _dot(dst, qn, NN), dkp + _dot(dst, qp, NN), dv

        z = jnp.zeros((tq, LANE), F32)
        carry = step(j, (z, z, z), True)
        dkn, dkp, dv = lax.fori_loop(j + 1, nq, lambda i, c: step(i, c, False), carry)
        dkn_ref[...] = dkn
        dkp_ref[...] = dkp
        dv_ref[...] = dv

    blk = pl.BlockSpec((tq, LANE), lambda h, j: (j, h))
    res = lambda f: pl.BlockSpec((t, LANE), f)
    row = pl.BlockSpec((1, 1, t), lambda h, j: (h, 0, 0))
    return _call_with_side(
        body, side, 8, (nh, nq),
        name=name,
        in_specs=[
            blk,
            pl.BlockSpec((tq, LANE), lambda h, j: (j, 0)),
            pl.BlockSpec((tq, LANE), lambda h, j: (j, nh + h)),
            res(lambda h, j: (0, h)),
            res(lambda h, j: (0, h)),
            res(lambda h, j: (0, h)),
            row,
            row,
        ],
        out_specs=[blk, blk, blk],
        out_shape=[jax.ShapeDtypeStruct((t, nh * LANE), F32)] * 3,
    )(kvf, kpe, kvf, qn16, qpe, do16, lse_row, delta_row)


def _alibi_slopes(n):
    return [float(2.0 ** (-8.0 * i / n)) for i in range(1, n + 1)]


def _window_mask(has_prev):
    qi = lax.broadcasted_iota(jnp.int32, (BLK, 2 * BLK), 0)
    ki = lax.broadcasted_iota(jnp.int32, (BLK, 2 * BLK), 1)
    dist = qi + BLK - ki
    valid = (dist >= 0) & (dist <= BLK) & ((ki >= BLK) | has_prev)
    return valid, dist.astype(F32)


def _band_fwd(zd, nh, dil, name):
    t = zd.shape[0]
    nbc = t // BLK // dil
    scale = DIL_HEAD_DIM ** -0.5
    slopes = _alibi_slopes(nh)
    dw = nh * LANE

    def body(q_ref, kc_ref, kp_ref, vc_ref, vp_ref, o_ref, l_ref):
        mask, dist = _window_mask(pl.program_id(1) > 0)
        for h in range(nh):
            sl = slice(h * LANE, (h + 1) * LANE)
            keys = jnp.concatenate([kp_ref[:, sl], kc_ref[:, sl]], axis=0)
            vals = jnp.concatenate([vp_ref[:, sl], vc_ref[:, sl]], axis=0)
            s = jnp.where(mask, _dot(q_ref[:, sl], keys, NT) * scale - (slopes[h] * dil) * dist, NEG)
            m = jnp.max(s, -1, keepdims=True)
            e = jnp.exp(s - m)
            l = jnp.sum(e, -1, keepdims=True)
            o_ref[:, sl] = _dot((e * (1.0 / l)).astype(BF16), vals, NN)
            l_ref[:, sl] = jnp.broadcast_to(m + jnp.log(l), (BLK, LANE))

    prev = lambda i: jnp.maximum(i - 1, 0)
    spec = lambda f: pl.BlockSpec((BLK, dw), f)
    return pl.pallas_call(
        body,
        name=name,
        grid=(dil, nbc),
        in_specs=[
            spec(lambda r, i: (r * nbc + i, 0)),
            spec(lambda r, i: (r * nbc + i, 1)),
            spec(lambda r, i: (r * nbc + prev(i), 1)),
            spec(lambda r, i: (r * nbc + i, 2)),
            spec(lambda r, i: (r * nbc + prev(i), 2)),
        ],
        out_specs=[spec(lambda r, i: (r * nbc + i, 0))] * 2,
        out_shape=[jax.ShapeDtypeStruct((t, dw), F32)] * 2,
        compiler_params=_params(("parallel", "parallel")),
    )(zd, zd, zd, zd, zd)


def _band_bwd(zd, o, lse, do, dl, nh, dil, name):
    t = zd.shape[0]
    nbc = t // BLK // dil
    scale = DIL_HEAD_DIM ** -0.5
    slopes = _alibi_slopes(nh)
    dw = nh * LANE

    def body(q_ref, k_ref, v_ref, kp_ref, vp_ref, qn_ref, o_ref, l_ref, do_ref, dl_ref, on_ref, ln_ref, don_ref, dln_ref,
             dq_ref, dk_ref, dv_ref):
        mask, dist = _window_mask(pl.program_id(1) > 0)
        mask_n, dist_n = _window_mask(pl.program_id(1) + 1 < nbc)
        mask_n, dist_n = mask_n[:, :BLK], dist_n[:, :BLK]
        for h in range(nh):
            sl = slice(h * LANE, (h + 1) * LANE)
            bias = slopes[h] * dil
            q, k, v, qn = q_ref[:, sl], k_ref[:, sl], v_ref[:, sl], qn_ref[:, sl]
            keys = jnp.concatenate([kp_ref[:, sl], k], axis=0)
            vals = jnp.concatenate([vp_ref[:, sl], v], axis=0)
            dof, donf = do_ref[:, sl], don_ref[:, sl]
            dob, donb = dof.astype(BF16), donf.astype(BF16)
            lse1, lsen1 = l_ref[:, sl][:, :1], ln_ref[:, sl][:, :1]
            adj = jnp.sum(dl_ref[:, sl] - dof * o_ref[:, sl], -1, keepdims=True)
            adjn = jnp.sum(dln_ref[:, sl] - donf * on_ref[:, sl], -1, keepdims=True)
            p = jnp.where(mask, jnp.exp(_dot(q, keys, NT) * scale - bias * dist - lse1), 0.0)
            ds = (p * (_dot(dob, vals, NT) + adj)).astype(BF16)
            dq_ref[:, sl] = _dot(ds, keys, NN) * scale
            pn = jnp.where(mask_n, jnp.exp(_dot(qn, k, NT) * scale - bias * dist_n - lsen1), 0.0)
            dsn = (pn * (_dot(donb, v, NT) + adjn)).astype(BF16)
            both_q = jnp.concatenate([q, qn], axis=0)
            dk_ref[:, sl] = _dot(jnp.concatenate([ds[:, BLK:], dsn], axis=0), both_q, TN) * scale
            dv_ref[:, sl] = _dot(jnp.concatenate([p[:, BLK:].astype(BF16), pn.astype(BF16)], axis=0),
                                 jnp.concatenate([dob, donb], axis=0), TN)

    prev = lambda i: jnp.maximum(i - 1, 0)
    nxt = lambda i: jnp.minimum(i + 1, nbc - 1)
    spec = lambda f: pl.BlockSpec((BLK, dw), f)
    cur, nx = spec(lambda r, i: (r * nbc + i, 0)), spec(lambda r, i: (r * nbc + nxt(i), 0))
    return pl.pallas_call(
        body,
        name=name,
        grid=(dil, nbc),
        in_specs=[cur, spec(lambda r, i: (r * nbc + i, 1)), spec(lambda r, i: (r * nbc + i, 2)),
                  spec(lambda r, i: (r * nbc + prev(i), 1)), spec(lambda r, i: (r * nbc + prev(i), 2)),
                  nx, cur, cur, cur, cur, nx, nx, nx, nx],
        out_specs=[cur] * 3,
        out_shape=[jax.ShapeDtypeStruct((t, dw), F32)] * 3,
        compiler_params=_params(("parallel", "parallel")),
    )(zd, zd, zd, zd, zd, zd, o, lse, do, dl, o, lse, do, dl)


def _s5_ops(a_re, a_im, ldt, bt_re, bt_im, c_re, c_im):
    L, g, p = S5_CHUNK, S5_GROUP, S5_STATE
    dt = jnp.exp(ldt)
    lr, li = a_re * dt, a_im * dt
    er = jnp.exp(lr)
    lam_re, lam_im = er * jnp.cos(li), er * jnp.sin(li)
    nr, ni = lam_re - 1.0, lam_im
    den = a_re * a_re + a_im * a_im
    fr, fi = (nr * a_re + ni * a_im) / den, (ni * a_re - nr * a_im) / den
    bb_re, bb_im = fr * bt_re - fi * bt_im, fr * bt_im + fi * bt_re

    def power(tau):
        mag = jnp.exp(tau * lr)
        return mag * jnp.cos(tau * li), mag * jnp.sin(tau * li)

    step = lax.broadcasted_iota(jnp.int32, (L, 1), 0).astype(F32)

    def outer(pr, pi, mr, mi):
        re = pr[:, None, :] * mr[None] - pi[:, None, :] * mi[None]
        im = pr[:, None, :] * mi[None] + pi[:, None, :] * mr[None]
        return re.reshape(L * g, p), im.reshape(L * g, p)

    half = float(L // 2)
    cp_re, cp_im = outer(*power(step - half), c_re, c_im)
    pb_re, pb_im = outer(*power(half - step), bb_re, bb_im)
    toep = _dot(cp_re, pb_re, NT, HI) - _dot(cp_im, pb_im, NT, HI)
    trow = lax.broadcasted_iota(jnp.int32, (L * g, L * g), 0) // g
    scol = lax.broadcasted_iota(jnp.int32, (L * g, L * g), 1) // g
    toep = jnp.where(trow >= scol, toep, 0.0)
    et_re, et_im = outer(*power(float(L - 1) - step), bb_re, bb_im)
    f_re, f_im = outer(*power(step + 1.0), c_re, c_im)
    big_re, big_im = power(jnp.full((1, 1), float(L), F32))
    return toep, et_re, et_im, f_re, f_im, big_re, big_im


def _s5_y(ops, u, sp_re, sp_im):
    toep, _, _, f_re, f_im, _, _ = ops
    return _dot(u, toep, NT) + _dot(sp_re, f_re, NT, HI) - _dot(sp_im, f_im, NT, HI)


def _s5_group_specs(ng):
    vec = pl.BlockSpec((1, 1, S5_STATE), lambda g: (g, 0, 0))
    one = pl.BlockSpec((1, 1, 1), lambda g: (g, 0, 0))
    mat = pl.BlockSpec((1, S5_GROUP, S5_STATE), lambda g: (g, 0, 0))
    return [vec, vec, one, mat, mat, mat, mat]


def _s5_load(refs):
    return [r[0] for r in refs]


def _s5_local(prm, u, name):
    ng, n, w = u.shape
    p = S5_STATE

    def body(*refs):
        ops = _s5_ops(*_s5_load(refs[:7]))
        uu = refs[7][0]
        refs[8][0] = _dot(uu, ops[1], NN, HI)
        refs[9][0] = _dot(uu, ops[2], NN, HI)
        refs[10][0] = ops[5]
        refs[11][0] = ops[6]

    blk = lambda a, b: pl.BlockSpec((1, a, b), lambda g: (g, 0, 0))
    return pl.pallas_call(
        body,
        name=name,
        grid=(ng,),
        in_specs=_s5_group_specs(ng) + [blk(n, w)],
        out_specs=[blk(n, p), blk(n, p), blk(1, p), blk(1, p)],
        out_shape=[jax.ShapeDtypeStruct((ng, n, p), F32)] * 2 + [jax.ShapeDtypeStruct((ng, 1, p), F32)] * 2,
        compiler_params=_params(("parallel",)),
    )(*prm, u)


def _s5_carry(e_re, e_im, lam_re, lam_im, name):
    n, w = e_re.shape
    cw = _tile(w, (1024, 512, 256, 128))

    def body(er_ref, ei_ref, lr_ref, li_ref, sr_ref, si_ref):
        lr, li = lr_ref[...], li_ref[...]

        def step(k, carry):
            sr, si = carry
            row = pl.ds(k, 1)
            sr_ref[row, :] = sr
            si_ref[row, :] = si
            return lr * sr - li * si + er_ref[row, :], li * sr + lr * si + ei_ref[row, :]

        z = jnp.zeros((1, cw), F32)
        lax.fori_loop(0, n, step, (z, z))

    col = pl.BlockSpec((n, cw), lambda j: (0, j))
    one = pl.BlockSpec((1, cw), lambda j: (0, j))
    return pl.pallas_call(
        body,
        name=name,
        grid=(w // cw,),
        in_specs=[col, col, one, one],
        out_specs=[col, col],
        out_shape=[jax.ShapeDtypeStruct((n, w), F32)] * 2,
        compiler_params=_params(("parallel",)),
    )(e_re, e_im, lam_re, lam_im)


def _s5_carry_bwd(dsp_re, dsp_im, sp_re, sp_im, lam_re, lam_im, name):
    n, w = dsp_re.shape
    cw = _tile(w, (1024, 512, 256, 128))

    def body(dr_ref, di_ref, sr_ref, si_ref, lr_ref, li_ref, gr_ref, gi_ref, dlr_ref, dli_ref):
        lr, li = lr_ref[...], li_ref[...]

        def step(q, carry):
            gr_next, gi_next, dr_next, di_next, alr, ali = carry
            k = n - 1 - q
            row = pl.ds(k, 1)
            gr = dr_next + lr * gr_next + li * gi_next
            gi = di_next - li * gr_next + lr * gi_next
            gr_ref[row, :] = gr
            gi_ref[row, :] = gi
            sr, si = sr_ref[row, :], si_ref[row, :]
            return gr, gi, dr_ref[row, :], di_ref[row, :], alr + gr * sr + gi * si, ali + gi * sr - gr * si

        z = jnp.zeros((1, cw), F32)
        out = lax.fori_loop(0, n, step, (z, z, z, z, z, z))
        dlr_ref[...] = out[4]
        dli_ref[...] = out[5]

    col = pl.BlockSpec((n, cw), lambda j: (0, j))
    one = pl.BlockSpec((1, cw), lambda j: (0, j))
    return pl.pallas_call(
        body,
        name=name,
        grid=(w // cw,),
        in_specs=[col, col, col, col, one, one],
        out_specs=[col, col, one, one],
        out_shape=[jax.ShapeDtypeStruct((n, w), F32)] * 2 + [jax.ShapeDtypeStruct((1, w), F32)] * 2,
        compiler_params=_params(("parallel",)),
    )(dsp_re, dsp_im, sp_re, sp_im, lam_re, lam_im)


def _s5_out(prm, u, sp_re, sp_im, name):
    ng, n, w = u.shape
    p = S5_STATE

    def body(*refs):
        ops = _s5_ops(*_s5_load(refs[:7]))
        refs[10][0] = _s5_y(ops, refs[7][0], refs[8][0], refs[9][0])

    blk = lambda a, b: pl.BlockSpec((1, a, b), lambda g: (g, 0, 0))
    return pl.pallas_call(
        body,
        name=name,
        grid=(ng,),
        in_specs=_s5_group_specs(ng) + [blk(n, w), blk(n, p), blk(n, p)],
        out_specs=blk(n, w),
        out_shape=jax.ShapeDtypeStruct((ng, n, w), F32),
        compiler_params=_params(("parallel",)),
    )(*prm, u, sp_re, sp_im)


def _s5_bwd_state(prm, dy, name):
    ng, n, w = dy.shape
    p = S5_STATE

    def body(*refs):
        ops = _s5_ops(*_s5_load(refs[:7]))
        d = refs[7][0]
        refs[8][0] = _dot(d, ops[3], NN, HI)
        refs[9][0] = -_dot(d, ops[4], NN, HI)

    blk = lambda a, b: pl.BlockSpec((1, a, b), lambda g: (g, 0, 0))
    return pl.pallas_call(
        body,
        name=name,
        grid=(ng,),
        in_specs=_s5_group_specs(ng) + [blk(n, w)],
        out_specs=[blk(n, p), blk(n, p)],
        out_shape=[jax.ShapeDtypeStruct((ng, n, p), F32)] * 2,
        compiler_params=_params(("parallel",)),
    )(*prm, dy)


def _s5_bwd_main(prm, u, sp_re, sp_im, dy, g_re, g_im, dlam_re, dlam_im, name):
    ng, n, w = u.shape
    p = S5_STATE

    def body(*refs):
        prm_v = _s5_load(refs[:7])
        uu, sr, si, d, gr, gi, dlr, dli = [r[0] for r in refs[7:15]]
        ops, ops_vjp = jax.vjp(_s5_ops, *prm_v)
        cots = (_dot(d, uu, TN), _dot(uu, gr, TN, HI), _dot(uu, gi, TN, HI), _dot(d, sr, TN, HI), -_dot(d, si, TN, HI),
                dlr, dli)
        grads = ops_vjp(cots)
        for q in range(7):
            refs[15 + q][0] = grads[q]
        refs[22][0] = _dot(d, ops[0], NN) + _dot(gr, ops[1], NT, HI) + _dot(gi, ops[2], NT, HI)

    blk = lambda a, b: pl.BlockSpec((1, a, b), lambda g: (g, 0, 0))
    prm_specs = _s5_group_specs(ng)
    return pl.pallas_call(
        body,
        name=name,
        grid=(ng,),
        in_specs=prm_specs + [blk(n, w), blk(n, p), blk(n, p), blk(n, w), blk(n, p), blk(n, p), blk(1, p), blk(1, p)],
        out_specs=prm_specs + [blk(n, w)],
        out_shape=[jax.ShapeDtypeStruct(a.shape, F32) for a in prm] + [jax.ShapeDtypeStruct((ng, n, w), F32)],
        compiler_params=_params(("parallel",)),
    )(*prm, u, sp_re, sp_im, dy, g_re, g_im, dlam_re, dlam_im)


SLAB_GROUPS = LANE // S5_GROUP
CHUNK_TILES = S5_CHUNK * S5_GROUP // LANE


def _segment_transpose(tiles):
    seg = lax.broadcasted_iota(jnp.int32, (1, LANE), 1) // S5_GROUP
    tiles = list(tiles)
    stride = 1
    while stride < SLAB_GROUPS:
        upper = (seg & stride) != 0
        shift = S5_GROUP * stride
        new = list(tiles)
        for i in range(SLAB_GROUPS):
            if i & stride:
                continue
            j = i + stride
            new[i] = jnp.where(upper, pltpu.roll(tiles[j], shift, 1), tiles[i])
            new[j] = jnp.where(upper, tiles[j], pltpu.roll(tiles[i], LANE - shift, 1))
        tiles = new
        stride *= 2
    return tiles


def _to_groups(h, name):
    t, d = h.shape
    n = t // S5_CHUNK

    def body(x_ref, o_ref):
        for a in range(CHUNK_TILES):
            rows = [x_ref[pl.ds(SLAB_GROUPS * a + b, n, stride=S5_CHUNK), :] for b in range(SLAB_GROUPS)]
            for g, tile in enumerate(_segment_transpose(rows)):
                o_ref[g, :, a * LANE : (a + 1) * LANE] = tile

    return pl.pallas_call(
        body,
        name=name,
        grid=(d // LANE,),
        in_specs=[pl.BlockSpec((t, LANE), lambda q: (0, q))],
        out_specs=pl.BlockSpec((SLAB_GROUPS, n, S5_CHUNK * S5_GROUP), lambda q: (q, 0, 0)),
        out_shape=jax.ShapeDtypeStruct((d // S5_GROUP, n, S5_CHUNK * S5_GROUP), F32),
        compiler_params=_params(("parallel",)),
    )(h)


def _from_groups(y, name):
    ng, n, w = y.shape
    t, d = n * S5_CHUNK, ng * S5_GROUP

    def body(y_ref, x_ref):
        for a in range(CHUNK_TILES):
            tiles = [y_ref[g, :, a * LANE : (a + 1) * LANE] for g in range(SLAB_GROUPS)]
            for b, row in enumerate(_segment_transpose(tiles)):
                x_ref[pl.ds(SLAB_GROUPS * a + b, n, stride=S5_CHUNK), :] = row

    return pl.pallas_call(
        body,
        name=name,
        grid=(d // LANE,),
        in_specs=[pl.BlockSpec((SLAB_GROUPS, n, w), lambda q: (q, 0, 0))],
        out_specs=pl.BlockSpec((t, LANE), lambda q: (0, q)),
        out_shape=jax.ShapeDtypeStruct((t, d), F32),
        compiler_params=_params(("parallel",)),
    )(y)


def _states_to_cols(e):
    ng, n, p = e.shape
    return e.transpose(1, 0, 2).reshape(n, ng * p)


def _cols_to_states(s, ng):
    n = s.shape[0]
    return s.reshape(n, ng, S5_STATE).transpose(1, 0, 2)


def _me():
    return lax.axis_index("x"), lax.axis_index("y"), lax.axis_index("c")


def _flip(v, bit):
    return 1 - v if bit else v


COPIES = N_DEV - 1
ANY_SPEC = pl.BlockSpec(memory_space=pl.ANY)


def _comm_sems(n):
    return [pltpu.SemaphoreType.DMA((COPIES * n,)), pltpu.SemaphoreType.DMA((COPIES * n,)), pltpu.SemaphoreType.DMA((n,))]


def _gather_phases(x_refs, out_refs, send_sems, recv_sems, local_sems):
    mx, my, mc = _me()
    me, sibling = (mx, my, mc), (mx, my, 1 - mc)
    chips = [(1 - mx, my), (mx, 1 - my), (1 - mx, 1 - my)]
    arrays = range(len(x_refs))

    def slot(a, px, py, pc):
        return out_refs[a].at[4 * px + 2 * py + pc]

    def copy(a, k, block, to, from_input=False):
        return pltpu.make_async_remote_copy(
            src_ref=x_refs[a] if from_input else slot(a, *block),
            dst_ref=slot(a, *block),
            send_sem=send_sems.at[COPIES * a + k],
            recv_sem=recv_sems.at[COPIES * a + k],
            device_id=to,
            device_id_type=pl.DeviceIdType.MESH,
        )

    mine = [pltpu.make_async_copy(x_refs[a], slot(a, *me), local_sems.at[a]) for a in arrays]
    first = [copy(a, 0, me, sibling, True) for a in arrays]
    first += [copy(a, 1 + j, me, (*chip, mc), True) for a in arrays for j, chip in enumerate(chips)]
    passed = {(a, j): copy(a, 4 + j, (*chip, mc), sibling) for a in arrays for j, chip in enumerate(chips)}

    def start():
        for cp in mine + first:
            cp.start()

    def relay():
        for j, chip in enumerate(chips):
            for a in arrays:
                copy(a, 1 + j, (*chip, mc), me).wait_recv()
                passed[(a, j)].start()

    def finish():
        for a in arrays:
            copy(a, 0, sibling, me).wait_recv()
            for j, chip in enumerate(chips):
                copy(a, 4 + j, (*chip, 1 - mc), me).wait_recv()
        for cp in first + list(passed.values()):
            cp.wait_send()
        for cp in mine:
            cp.wait()

    return start, relay, finish


def _exchange_phases(g_refs, out_refs, send_sems, recv_sems, local_sems):
    mx, my, mc = _me()
    mine_idx = 4 * mx + 2 * my + mc
    arrays = range(len(g_refs))
    own = [pltpu.make_async_copy(g_refs[a].at[mine_idx], out_refs[a].at[mine_idx], local_sems.at[a]) for a in arrays]
    copies = []
    for m in range(1, N_DEV):
        px, py, pc = _flip(mx, m & 4), _flip(my, m & 2), _flip(mc, m & 1)
        for a in arrays:
            copies.append(
                pltpu.make_async_remote_copy(
                    src_ref=g_refs[a].at[4 * px + 2 * py + pc],
                    dst_ref=out_refs[a].at[mine_idx],
                    send_sem=send_sems.at[COPIES * a + m - 1],
                    recv_sem=recv_sems.at[COPIES * a + m - 1],
                    device_id=(px, py, pc),
                    device_id_type=pl.DeviceIdType.MESH,
                )
            )

    def start():
        for cp in own + copies:
            cp.start()

    def finish():
        for cp in copies:
            cp.wait_recv()
        for cp in copies:
            cp.wait_send()
        for cp in own:
            cp.wait()

    return start, None, finish


def _side_out_shapes(side):
    kind, arrs = side
    return [jax.ShapeDtypeStruct((N_DEV,) + a.shape if kind == "gather" else a.shape, a.dtype) for a in arrs]


def _run_side(side, x_refs, out_refs, sems, nsteps_major, nsteps_minor):
    start, relay, finish = (_gather_phases if side[0] == "gather" else _exchange_phases)(x_refs, out_refs, *sems)
    a, b = pl.program_id(0), pl.program_id(1)
    pl.when(jnp.logical_and(a == 0, b == 0))(start)
    if relay is not None:
        pl.when(jnp.logical_and(a == nsteps_major - 1, b == 0))(relay)
    return lambda: pl.when(jnp.logical_and(a == nsteps_major - 1, b == nsteps_minor - 1))(finish)


def _exchange_call(kind, arrs, name):
    n = len(arrs)

    def body(*refs):
        start, relay, finish = (_gather_phases if kind == "gather" else _exchange_phases)(refs[:n], refs[n : 2 * n], *refs[2 * n :])
        start()
        if relay is not None:
            relay()
        finish()

    return pl.pallas_call(body, name=name, in_specs=[ANY_SPEC] * n, out_specs=[ANY_SPEC] * n,
                          out_shape=_side_out_shapes((kind, arrs)), scratch_shapes=_comm_sems(n))(*arrs)


def _sum_slots(recv, name):
    _, r, c = recv.shape
    tr = _tile(r, (256, 128, 64, 32, 16, 8))

    def body(r_ref, o_ref):
        acc = r_ref[0].astype(F32)
        for k in range(1, N_DEV):
            acc = acc + r_ref[k].astype(F32)
        o_ref[...] = acc

    return pl.pallas_call(
        body,
        name=name,
        grid=(r // tr,),
        in_specs=[pl.BlockSpec((N_DEV, tr, c), lambda i: (0, i, 0))],
        out_specs=pl.BlockSpec((tr, c), lambda i: (i, 0)),
        out_shape=jax.ShapeDtypeStruct((r, c), F32),
        compiler_params=_params(("parallel",)),
    )(recv)


def _adamw(w, g, m, v, name):
    shape = w.shape
    c = shape[-1]
    as2d = lambda a: a.reshape(-1, c)
    w2, g2, m2, v2 = as2d(w), as2d(g), as2d(m), as2d(v)
    r = w2.shape[0]
    tr = _tile(r, (256, 128, 64, 32, 16, 8))
    c1 = 1.0 / (1.0 - ADAM_B1 ** ADAM_STEP)
    c2 = 1.0 / (1.0 - ADAM_B2 ** ADAM_STEP)

    def body(w_ref, g_ref, m_ref, v_ref, d_ref, mo_ref, vo_ref):
        gg = g_ref[...]
        mn = ADAM_B1 * m_ref[...] + (1.0 - ADAM_B1) * gg
        vn = ADAM_B2 * v_ref[...] + (1.0 - ADAM_B2) * (gg * gg)
        d_ref[...] = -ADAM_LR * ((mn * c1) / (jnp.sqrt(vn * c2) + ADAM_EPS) + ADAM_WD * w_ref[...])
        mo_ref[...] = mn
        vo_ref[...] = vn

    spec = pl.BlockSpec((tr, c), lambda i: (i, 0))
    d, mn, vn = pl.pallas_call(
        body,
        name=name,
        grid=(r // tr,),
        in_specs=[spec] * 4,
        out_specs=[spec] * 3,
        out_shape=[jax.ShapeDtypeStruct((r, c), F32)] * 3,
        compiler_params=_params(("parallel",)),
    )(w2, g2, m2, v2)
    return d.reshape(shape), mn.reshape(shape), vn.reshape(shape)


COL_SHARDED = ("attn_w_in", "mla_w_q_b", "mla_w_kv_b", "s5_d", "s5_w_glu", "ffn_w_in", "ple_w")
ROW_SHARDED = ("attn_w_out", "ffn_w_out", "ple_gate_w")
REPLICATED = ("mla_q_norm", "mla_kv_norm", "s5_a_re", "s5_a_im", "s5_log_dt", "s5_b_re", "s5_b_im", "s5_c_re",
              "s5_c_im", "ln1_g", "ln1_b", "ln2_g", "ln2_b")
ATTENTION_WEIGHTS = ("attn_w_in", "mla_w_q_b", "mla_w_kv_b")
BIG_WEIGHTS = ("attn_w_in", "mla_w_q_b", "mla_w_kv_b", "attn_w_out", "s5_w_glu", "ffn_w_in", "ffn_w_out", "ple_w", "ple_gate_w")
WEIGHTS = ("attn_w_in", "mla_q_norm", "mla_w_q_b", "mla_kv_norm", "mla_w_kv_b", "attn_w_out", "s5_a_re", "s5_a_im",
           "s5_log_dt", "s5_b_re", "s5_b_im", "s5_c_re", "s5_c_im", "s5_d", "s5_w_glu", "ln1_g", "ln1_b", "ffn_w_in",
           "ffn_w_out", "ple_w", "ple_gate_w", "ln2_g", "ln2_b")


def _unshard(name, gathered):
    if name in COL_SHARDED:
        full = jnp.moveaxis(gathered, 0, -2)
        return full.reshape(full.shape[:-2] + (full.shape[-2] * full.shape[-1],))
    full = jnp.moveaxis(gathered, 0, 1)
    return full.reshape((full.shape[0], full.shape[1] * full.shape[2]) + full.shape[3:])


def _by_owner(name, grad):
    if name in COL_SHARDED:
        g = grad.reshape(grad.shape[:-1] + (N_DEV, grad.shape[-1] // N_DEV))
        return jnp.moveaxis(g, -2, 0).reshape(N_DEV, -1)
    if name in ROW_SHARDED:
        g = grad.reshape((grad.shape[0], N_DEV, grad.shape[1] // N_DEV) + grad.shape[2:])
        return jnp.moveaxis(g, 1, 0).reshape(N_DEV, -1)
    return jnp.broadcast_to(grad.reshape(1, -1), (N_DEV, grad.size))


def _owner_rows(name, g):
    k, n = g.shape
    if name in COL_SHARDED:
        return g.reshape(k, N_DEV, n // N_DEV).transpose(1, 0, 2)
    assert name in ROW_SHARDED, name
    return g.reshape(N_DEV, k // N_DEV, n)


def _pack(pieces, axis):
    blocks, where, row = [], [], 0
    for p in pieces:
        n = p.shape[axis]
        pad = (-n) % (PACK_COLS * PACK_ROW_TILE)
        if pad:
            shape = list(p.shape)
            shape[axis] = pad
            p = jnp.concatenate([p, jnp.zeros(shape, p.dtype)], axis=axis)
        rows = (n + pad) // PACK_COLS
        blocks.append(p.reshape(p.shape[:axis] + (rows, PACK_COLS)))
        where.append((row, rows, n))
        row += rows
    return jnp.concatenate(blocks, axis=axis), where


def _unpack(buf, axis, where):
    row, rows, n = where
    part = lax.slice_in_dim(buf, row, row + rows, axis=axis)
    return lax.slice_in_dim(part.reshape(part.shape[:axis] + (rows * PACK_COLS,)), 0, n, axis=axis)


def _twice(fn):
    return lambda *a: fn(*a) * 2


def _ffn_block(h, hb, p_i, w_in, w_out, w_ple, w_pg, g, b, alpha, tag):
    t, d = h.shape
    hid = w_out.shape[0]
    cw = _tile(hid, (1408, 512, 256, 128))
    ncb = hid // cw
    gu = _mm(hb, w_in, out_dtype=BF16, name=f"ffn_in_{tag}")
    act = _rowwise(_swiglu_fn, [(gu, cw, 0), (gu, cw, ncb)], [], [(hid, cw, BF16)], name=f"swiglu_{tag}", tq=1024, ncol=ncb)[0]
    f = _mm(act, w_out, name=f"ffn_out_{tag}")
    pw = _mm(p_i, w_ple, name=f"ple_{tag}")
    gate = _mm(hb, w_pg, name=f"ple_gate_{tag}")
    out, outb = _rowwise(_twice(_ln_ffn_fn(alpha)), [_whole(h), _whole(f), _whole(pw), _whole(gate)], [g, b],
                         [(d, d, F32), (d, d, BF16)], name=f"ln2_{tag}", tq=256)
    return out, outb, (h, hb, p_i, gu, act, f, pw, gate)


def _ffn_block_bwd(saved, dout, w_in, w_out, w_pg, g, b, alpha, tag):
    h, hb, p_i, gu, act, f, pw, gate = saved
    t, d = h.shape
    hid = w_out.shape[0]
    cw = _tile(hid, (1408, 512, 256, 128))
    ncb = hid // cw
    dh_a, df, dpw, dgate, dg, db = _rowwise_bwd(
        _ln_ffn_fn(alpha), [_whole(h), _whole(f), _whole(pw), _whole(gate)], [g, b], [dout],
        need=[True] * 4, drow=[(d, F32), (d, BF16), (d, BF16), (d, BF16)], name=f"ln2_bwd_{tag}", tq=128)
    dw_out = _mm(act, df, ta=True, out_dtype=BF16, name=f"ffn_out_dw_{tag}")
    dact = _mm(df, w_out, tb=True, out_dtype=BF16, name=f"ffn_out_dx_{tag}")
    (dgu,) = _rowwise_bwd(lambda v: _swiglu_fn(v[:, :hid], v[:, hid:]), [_whole(gu)], [], [[_whole(dact)]], need=[True],
                          drow=[(2 * hid, BF16)], name=f"swiglu_bwd_{tag}", tq=128)
    dw_in = _mm(hb, dgu, ta=True, out_dtype=BF16, name=f"ffn_in_dw_{tag}")
    dh_b = _mm(dgu, w_in, tb=True, name=f"ffn_in_dx_{tag}")
    dw_ple = _mm(p_i, dpw, ta=True, out_dtype=BF16, name=f"ple_dw_{tag}")
    dw_pg = _mm(hb, dgate, ta=True, out_dtype=BF16, name=f"ple_gate_dw_{tag}")
    dh_c = _mm(dgate, w_pg, tb=True, name=f"ple_gate_dx_{tag}")
    return [dh_a, dh_b, dh_c], dict(ffn_w_in=dw_in, ffn_w_out=dw_out, ple_w=dw_ple, ple_gate_w=dw_pg, ln2_g=dg, ln2_b=db)


def kernel(x, p, positions, attn_w_in, mla_q_norm, mla_w_q_b, mla_kv_norm, mla_w_kv_b, attn_w_out, s5_a_re, s5_a_im, s5_log_dt, s5_b_re, s5_b_im, s5_c_re, s5_c_im, s5_d, s5_w_glu, ln1_g, ln1_b, ffn_w_in, ffn_w_out, ple_w, ple_gate_w, ln2_g, ln2_b, loss_target, m_attn_w_in, m_mla_q_norm, m_mla_w_q_b, m_mla_kv_norm, m_mla_w_kv_b, m_attn_w_out, m_s5_a_re, m_s5_a_im, m_s5_log_dt, m_s5_b_re, m_s5_b_im, m_s5_c_re, m_s5_c_im, m_s5_d, m_s5_w_glu, m_ln1_g, m_ln1_b, m_ffn_w_in, m_ffn_w_out, m_ple_w, m_ple_gate_w, m_ln2_g, m_ln2_b, v_attn_w_in, v_mla_q_norm, v_mla_w_q_b, v_mla_kv_norm, v_mla_w_kv_b, v_attn_w_out, v_s5_a_re, v_s5_a_im, v_s5_log_dt, v_s5_b_re, v_s5_b_im, v_s5_c_re, v_s5_c_im, v_s5_d, v_s5_w_glu, v_ln1_g, v_ln1_b, v_ffn_w_in, v_ffn_w_out, v_ple_w, v_ple_gate_w, v_ln2_g, v_ln2_b):
    local = dict(attn_w_in=attn_w_in, mla_q_norm=mla_q_norm, mla_w_q_b=mla_w_q_b, mla_kv_norm=mla_kv_norm,
                 mla_w_kv_b=mla_w_kv_b, attn_w_out=attn_w_out, s5_a_re=s5_a_re, s5_a_im=s5_a_im, s5_log_dt=s5_log_dt,
                 s5_b_re=s5_b_re, s5_b_im=s5_b_im, s5_c_re=s5_c_re, s5_c_im=s5_c_im, s5_d=s5_d, s5_w_glu=s5_w_glu,
                 ln1_g=ln1_g, ln1_b=ln1_b, ffn_w_in=ffn_w_in, ffn_w_out=ffn_w_out, ple_w=ple_w, ple_gate_w=ple_gate_w,
                 ln2_g=ln2_g, ln2_b=ln2_b)
    mom_m = dict(zip(WEIGHTS, (m_attn_w_in, m_mla_q_norm, m_mla_w_q_b, m_mla_kv_norm, m_mla_w_kv_b, m_attn_w_out, m_s5_a_re, m_s5_a_im, m_s5_log_dt, m_s5_b_re, m_s5_b_im, m_s5_c_re, m_s5_c_im, m_s5_d, m_s5_w_glu, m_ln1_g, m_ln1_b, m_ffn_w_in, m_ffn_w_out, m_ple_w, m_ple_gate_w, m_ln2_g, m_ln2_b)))
    mom_v = dict(zip(WEIGHTS, (v_attn_w_in, v_mla_q_norm, v_mla_w_q_b, v_mla_kv_norm, v_mla_w_kv_b, v_attn_w_out, v_s5_a_re, v_s5_a_im, v_s5_log_dt, v_s5_b_re, v_s5_b_im, v_s5_c_re, v_s5_c_im, v_s5_d, v_s5_w_glu, v_ln1_g, v_ln1_b, v_ffn_w_in, v_ffn_w_out, v_ple_w, v_ple_gate_w, v_ln2_g, v_ln2_b)))

    t, d = x.shape[1], x.shape[2]
    depth = ln1_g.shape[0]
    alpha = (2.0 * depth) ** 0.25
    ql, kvl = mla_q_norm.shape[-1], mla_kv_norm.shape[-1]
    nh = mla_w_q_b.shape[-1] * N_DEV // (NOPE_DIM + ROPE_DIM)
    dw = (attn_w_in.shape[-1] * N_DEV - ql - kvl - ROPE_DIM) // 3
    ndh = dw // DIL_HEAD_DIM
    ng = d // S5_GROUP
    assert ql == kvl and (ql + kvl) % LANE == 0 and nh * V_DIM == dw

    sharded = [n for n in WEIGHTS if n in COL_SHARDED or n in ROW_SHARDED]
    def as_sent(n):
        if n == "s5_d":
            return local[n]
        if n == "attn_w_in":
            return local[n][0].T.astype(BF16)
        return local[n].astype(BF16).reshape(-1, local[n].shape[-1])

    full = {}

    def take_weights(names, gathered):
        for n, got in zip(names, gathered):
            if n == "attn_w_in":
                full[n] = got.reshape(N_DEV * local[n].shape[2], d)
            else:
                full[n] = _unshard(n, got.reshape((N_DEV,) + local[n].shape))

    first_w = [n for n in sharded if n in ATTENTION_WEIGHTS]
    later_w = [n for n in sharded if n not in ATTENTION_WEIGHTS]
    take_weights(first_w, _exchange_call("gather", [as_sent(n) for n in first_w], "gather_attn_weights"))

    w_in_t = full["attn_w_in"]
    lat = ql + kvl
    w_lat_t = jnp.concatenate([w_in_t[: lat + ROPE_DIM], jnp.zeros((LANE - ROPE_DIM, d), BF16)], axis=0)
    w_dil_t = w_in_t[lat + ROPE_DIM :]
    wq = full["mla_w_q_b"][0].reshape(ql, nh, NOPE_DIM + ROPE_DIM)
    wq_pe = jnp.pad(wq[:, :, NOPE_DIM:], ((0, 0), (0, 0), (0, LANE - ROPE_DIM)))
    wqp = jnp.concatenate([wq[:, :, :NOPE_DIM].reshape(ql, nh * LANE), wq_pe.reshape(ql, nh * LANE)], axis=1)
    wkv = full["mla_w_kv_b"][0].reshape(kvl, nh, NOPE_DIM + V_DIM)
    wkvp = jnp.concatenate([wkv[:, :, :NOPE_DIM].reshape(kvl, nh * LANE), wkv[:, :, NOPE_DIM:].reshape(kvl, nh * LANE)], axis=1)

    h0 = x[0]
    h0b = h0.astype(BF16)
    pb = p.astype(BF16)
    target = loss_target[0]
    pos = positions.reshape(t, 1)
    grads = {}

    z = _mm(h0b, w_lat_t, tb=True, name="attn_in_lat")
    zd = _mm(h0b, w_dil_t, tb=True, out_dtype=BF16, name="attn_in_dil")
    gq, gk = mla_q_norm.reshape(1, ql), mla_kv_norm.reshape(1, kvl)
    qn, kvn = _rowwise(_rms_fn, [(z, ql, 0), (z, kvl, 1)], [gq, gk], [(ql, ql, BF16), (kvl, kvl, BF16)], name="rms", tq=256)
    qf = _mm(qn, wqp, name="q_up")
    kvf = _mm(kvn, wkvp, out_dtype=BF16, name="kv_up")
    cos, sin = _rope_tables(pos, "rope_tables")
    pe_cb = lat // LANE
    qpe, kpe = _rowwise(_rope_fwd_fn(nh), [(qf, nh * LANE, 1), (z, LANE, pe_cb), _whole(cos), _whole(sin)], [],
                        [(nh * LANE, nh * LANE, BF16), (LANE, LANE, BF16)], name="rope", tq=256)
    out_a, lse_a, later_gathered = _mla_fwd(qf, qpe, kvf, kpe, nh, "mla_fwd", side=("gather", [as_sent(n) for n in later_w]))
    take_weights(later_w, later_gathered)
    w_out = full["attn_w_out"][0]
    w_glu = full["s5_w_glu"][0]
    d_skip = full["s5_d"]

    def to_classes(a, dil):
        if dil == 1:
            return a
        return a.reshape(t // dil, dil, a.shape[1]).transpose(1, 0, 2).reshape(t, a.shape[1])

    def from_classes(a, dil):
        if dil == 1:
            return a
        return a.reshape(dil, t // dil, a.shape[1]).transpose(1, 0, 2).reshape(t, a.shape[1])

    band = []
    for window, dil in DIL_BRANCHES:
        assert window // dil == BLK
        zc = to_classes(zd, dil)
        o_c, l_c = _band_fwd(zc, ndh, dil, f"band_fwd_d{dil}")
        band.append((dil, zc, o_c, l_c, from_classes(o_c, dil), from_classes(l_c, dil)))
    merge_rows = [_whole(b[4]) for b in band] + [_whole(b[5]) for b in band]
    out_b = _rowwise(_merge_fn, merge_rows, [], [(dw, dw, BF16)], name="merge", tq=512)[0]
    att = jnp.concatenate([out_a, out_b], axis=1)
    mix0 = _mm(att, w_out, name="attn_out")
    g1, b1 = ln1_g[0:1], ln1_b[0:1]
    h1, h1b = _rowwise(_twice(_ln_mix_fn(alpha)), [_whole(h0), _whole(mix0)], [g1, b1], [(d, d, F32), (d, d, BF16)],
                       name="ln1_l0", tq=256)
    h2, _, saved_f0 = _ffn_block(h1, h1b, pb[0, 0], full["ffn_w_in"][0], full["ffn_w_out"][0], full["ple_w"][0],
                                 full["ple_gate_w"][0], ln2_g[0:1], ln2_b[0:1], alpha, "l0")

    prm = [s5_a_re[0].reshape(ng, 1, S5_STATE), s5_a_im[0].reshape(ng, 1, S5_STATE), s5_log_dt[0].reshape(ng, 1, 1),
           s5_b_re[0].transpose(0, 2, 1), s5_b_im[0].transpose(0, 2, 1), s5_c_re[0], s5_c_im[0]]
    u = _to_groups(h2, "s5_regroup_u")
    e_re, e_im, lam_re, lam_im = _s5_local(prm, u, "s5_local")
    lam_re_c, lam_im_c = lam_re.reshape(1, ng * S5_STATE), lam_im.reshape(1, ng * S5_STATE)
    sp_re_c, sp_im_c = _s5_carry(_states_to_cols(e_re), _states_to_cols(e_im), lam_re_c, lam_im_c, "s5_carry")
    sp_re, sp_im = _cols_to_states(sp_re_c, ng), _cols_to_states(sp_im_c, ng)
    ys = _from_groups(_s5_out(prm, u, sp_re, sp_im, "s5_out"), "s5_ungroup_y")
    z5 = _rowwise(_gelu_fn, [_whole(ys), _whole(h2)], [d_skip], [(d, d, BF16)], name="gelu", tq=512)[0]
    vg = _mm(z5, w_glu, name="glu_in")
    g3, b3 = ln1_g[1:2], ln1_b[1:2]
    h3, h3b = _rowwise(_twice(_ln_glu_fn(alpha)), [_whole(h2), _whole(vg)], [g3, b3],
                       [(d, d, F32), (d, d, BF16)], name="ln1_l1", tq=256)
    h4, _, saved_f1 = _ffn_block(h3, h3b, pb[1, 0], full["ffn_w_in"][1], full["ffn_w_out"][1], full["ple_w"][1],
                                 full["ple_gate_w"][1], ln2_g[1:2], ln2_b[1:2], alpha, "l1")

    dh4, loss_acc = _loss_kernel(h4, target, "loss")
    loss = lax.psum(loss_acc[0, 0], AXES)

    dh3, gf1 = _ffn_block_bwd(saved_f1, [_whole(dh4)], full["ffn_w_in"][1], full["ffn_w_out"][1],
                              full["ple_gate_w"][1], ln2_g[1:2], ln2_b[1:2], alpha, "l1")
    dh2_a, dvg, dg3, db3 = _rowwise_bwd(
        _ln_glu_fn(alpha), [_whole(h2), _whole(vg)], [g3, b3], [[_whole(a) for a in dh3]],
        need=[True] * 2, drow=[(d, F32), (2 * d, BF16)], name="ln1_bwd_l1", tq=128)
    dw_glu = _mm(z5, dvg, ta=True, out_dtype=BF16, name="glu_dw")
    dz5 = _mm(dvg, w_glu, tb=True, name="glu_dx")
    dys, dh2_b, dd = _rowwise_bwd(_gelu_fn, [_whole(ys), _whole(h2)], [d_skip], [[_whole(dz5)]], need=[True, True],
                                  drow=[(d, F32), (d, F32)], name="gelu_bwd", tq=256)
    grads["s5_d"] = dd
    dy = _to_groups(dys, "s5_regroup_dy")
    dsp_re, dsp_im = _s5_bwd_state(prm, dy, "s5_bwd_state")
    g_re_c, g_im_c, dlam_re_c, dlam_im_c = _s5_carry_bwd(_states_to_cols(dsp_re), _states_to_cols(dsp_im), sp_re_c, sp_im_c,
                                                         lam_re_c, lam_im_c, "s5_carry_bwd")
    s5g = _s5_bwd_main(prm, u, sp_re, sp_im, dy, _cols_to_states(g_re_c, ng), _cols_to_states(g_im_c, ng),
                       dlam_re_c.reshape(ng, 1, S5_STATE), dlam_im_c.reshape(ng, 1, S5_STATE), "s5_bwd_main")
    grads["s5_a_re"] = s5g[0].reshape(s5_a_re.shape)
    grads["s5_a_im"] = s5g[1].reshape(s5_a_im.shape)
    grads["s5_log_dt"] = s5g[2].reshape(s5_log_dt.shape)
    grads["s5_b_re"] = s5g[3].transpose(0, 2, 1)[None]
    grads["s5_b_im"] = s5g[4].transpose(0, 2, 1)[None]
    grads["s5_c_re"] = s5g[5][None]
    grads["s5_c_im"] = s5g[6][None]
    dh2_c = _from_groups(s5g[7], "s5_ungroup_du")

    dh1, gf0 = _ffn_block_bwd(saved_f0, [_whole(dh2_a), _whole(dh2_b), _whole(dh2_c)], full["ffn_w_in"][0],
                              full["ffn_w_out"][0], full["ple_gate_w"][0], ln2_g[0:1], ln2_b[0:1], alpha, "l0")
    for k in ("ln2_g", "ln2_b"):
        grads[k] = jnp.concatenate([gf0[k], gf1[k]])

    shards = {}

    def take(items, received):
        for (key, _), recv in zip(items, received):
            shards[key] = _sum_slots(recv, f"sum_grads_{key[0]}_{key[1]}")

    ffn_names = ("ffn_w_in", "ffn_w_out", "ple_w", "ple_gate_w")
    items_l1 = [(("s5_w_glu", 0), _owner_rows("s5_w_glu", dw_glu))] + [((k, 1), _owner_rows(k, gf1[k])) for k in ffn_names]
    items_l0 = [((k, 0), _owner_rows(k, gf0[k])) for k in ffn_names]
    dh0_a, dmix, dg1, db1 = _rowwise_bwd(_ln_mix_fn(alpha), [_whole(h0), _whole(mix0)], [g1, b1], [[_whole(a) for a in dh1]],
                                         need=[True, True], drow=[(d, F32), (d, BF16)], name="ln1_bwd_l0", tq=128)
    grads["ln1_g"] = jnp.concatenate([dg1, dg3])
    grads["ln1_b"] = jnp.concatenate([db1, db3])
    dw_out = _mm(att, dmix, ta=True, out_dtype=BF16, name="attn_out_dw")
    datt = _mm(dmix, w_out, tb=True, name="attn_out_dx")
    items_l0.append((("attn_w_out", 0), _owner_rows("attn_w_out", dw_out)))
    small_early = [n for n in WEIGHTS if n not in BIG_WEIGHTS and n not in ("mla_q_norm", "mla_kv_norm")]
    send_small, where_small = _pack([_by_owner(n, grads[n]).astype(F32) for n in small_early], 1)

    dmerge = _rowwise_bwd(_merge_fn, merge_rows, [], [[(datt, dw, 1)]], need=[True] * 6, drow=[(dw, F32)] * 6,
                          name="merge_bwd", tq=256)
    dq_s = dk_s = dv_s = None
    for k, (dil, zc, o_c, l_c, _, _) in enumerate(band):
        do_c, dl_c = to_classes(dmerge[k], dil), to_classes(dmerge[3 + k], dil)
        dq_c, dk_c, dv_c = _band_bwd(zc, o_c, l_c, do_c, dl_c, ndh, dil, f"band_bwd_d{dil}")
        dq_n, dk_n, dv_n = from_classes(dq_c, dil), from_classes(dk_c, dil), from_classes(dv_c, dil)
        dq_s = dq_n if dq_s is None else dq_s + dq_n
        dk_s = dk_n if dk_s is None else dk_s + dk_n
        dv_s = dv_n if dv_s is None else dv_s + dv_n

    dqn, dqp, delta, recv_l1 = _mla_bwd_q(qf, qpe, kvf, kpe, datt, out_a, lse_a, nh, "mla_bwd_q",
                                          side=("exchange", [rows for _, rows in items_l1]))
    take(items_l1, recv_l1)
    to_row = lambda a: a[:, ::LANE].T.reshape(nh, 1, t)
    dkn, dkp, dv, recv_l0 = _mla_bwd_kv(qf[:, : nh * LANE].astype(BF16), qpe, kvf, kpe, datt[:, : nh * LANE].astype(BF16),
                                        to_row(lse_a), to_row(delta), nh, "mla_bwd_kv",
                                        side=("exchange", [rows for _, rows in items_l0] + [send_small]))
    take(items_l0, recv_l0[:-1])
    sum_small = _sum_slots(recv_l0[-1], "sum_grads_small")
    dq_pe, dk_pe = _rowwise(_rope_bwd_fn(nh), [_whole(dqp), _whole(dkp), _whole(cos), _whole(sin)], [],
                            [(nh * LANE, nh * LANE, BF16), (LANE, LANE, BF16)], name="rope_bwd", tq=256)
    dqf = jnp.concatenate([dqn.astype(BF16), dq_pe], axis=1)
    dkvf = jnp.concatenate([dkn, dv], axis=1).astype(BF16)
    dwqp = _mm(qn, dqf, ta=True, name="q_up_dw")
    dqn_in = _mm(dqf, wqp, tb=True, name="q_up_dx")
    dwkvp = _mm(kvn, dkvf, ta=True, name="kv_up_dw")
    dkvn_in = _mm(dkvf, wkvp, tb=True, name="kv_up_dx")
    dql, dkvl, dgq, dgk = _rowwise_bwd(_rms_fn, [(z, ql, 0), (z, kvl, 1)], [gq, gk], [[_whole(dqn_in)], [_whole(dkvn_in)]],
                                       need=[True, True], drow=[(ql, BF16), (kvl, BF16)], name="rms_bwd", tq=256)
    grads["mla_q_norm"], grads["mla_kv_norm"] = dgq, dgk
    dz_lat = jnp.concatenate([dql, dkvl, dk_pe], axis=1)
    dz_dil = jnp.concatenate([dq_s, dk_s, dv_s], axis=1).astype(BF16)
    dw_lat_t = _mm(dz_lat, h0b, ta=True, out_dtype=BF16, name="attn_in_lat_dw")
    dw_dil_t = _mm(dz_dil, h0b, ta=True, out_dtype=BF16, name="attn_in_dil_dw")
    dh0_b = _mm(dz_lat, w_lat_t, name="attn_in_lat_dx")
    dh0_c = _mm(dz_dil, w_dil_t, name="attn_in_dil_dx")
    grad_x = _addn([dh0_a, dh0_b, dh0_c], "grad_x")[None]
    dw_in_t = jnp.concatenate([dw_lat_t[: lat + ROPE_DIM], dw_dil_t], axis=0)
    dwq_n = dwqp[:, : nh * LANE].reshape(ql, nh, NOPE_DIM)
    dwq_r = dwqp[:, nh * LANE :].reshape(ql, nh, LANE)[:, :, :ROPE_DIM]
    dwq = jnp.concatenate([dwq_n, dwq_r], axis=2).reshape(ql, nh * (NOPE_DIM + ROPE_DIM))
    dwkv_k = dwkvp[:, : nh * LANE].reshape(kvl, nh, NOPE_DIM)
    dwkv_v = dwkvp[:, nh * LANE :].reshape(kvl, nh, V_DIM)
    dwkv = jnp.concatenate([dwkv_k, dwkv_v], axis=2).reshape(kvl, nh * (NOPE_DIM + V_DIM))

    items_att = [(("attn_w_in", 0), dw_in_t.reshape(N_DEV, -1, PACK_COLS)),
                 (("mla_w_q_b", 0), _owner_rows("mla_w_q_b", dwq.astype(BF16))),
                 (("mla_w_kv_b", 0), _owner_rows("mla_w_kv_b", dwkv.astype(BF16)))]
    small_late = ["mla_q_norm", "mla_kv_norm"]
    send_late, where_late = _pack([_by_owner(n, grads[n]).astype(F32) for n in small_late], 1)
    received = _exchange_call("exchange", [rows for _, rows in items_att] + [send_late], "exchange_grads_last")
    take(items_att, received[:-1])
    sum_late = _sum_slots(received[-1], "sum_grads_norms")

    g_out, d_out, m_out, v_out = [], [], [], []
    for n in WEIGHTS:
        if n == "attn_w_in":
            g = shards[(n, 0)].reshape(local[n].shape[2], d).T[None]
        elif n in BIG_WEIGHTS:
            g = jnp.stack([shards[(n, layer)] for layer in range(local[n].shape[0])]).reshape(local[n].shape)
        elif n in small_late:
            g = _unpack(sum_late, 0, where_late[small_late.index(n)]).reshape(local[n].shape)
        else:
            g = _unpack(sum_small, 0, where_small[small_early.index(n)]).reshape(local[n].shape)
        dlt, mn, vn = _adamw(local[n], g, mom_m[n], mom_v[n], f"adamw_{n}")
        g_out.append(g)
        d_out.append(dlt)
        m_out.append(mn)
        v_out.append(vn)
    return (loss, grad_x, *g_out, *d_out, *m_out, *v_out)
```
